```python
import math
import jax, jax.numpy as jnp
from jax import lax
import numpy as np

D_MODEL = 1024
BATCH = 8
SEQ = 4096
DEPTH = 2

N_META = 16
D_FF = 2816
NORM_EPS = 1e-6
NEG_INF = -1e30
SWA_Q_HEADS = 8
SWA_KV_HEADS = 2
SWA_HEAD_DIM = 64
SWA_WINDOW = 128
SWA_BLOCK = 128
REL_BUCKETS = 32
REL_MAX_DIST = 128
SWA_Q_DIM = SWA_Q_HEADS * SWA_HEAD_DIM
SWA_KV_DIM = SWA_KV_HEADS * SWA_HEAD_DIM
DN_HEADS = 4
DN_HEAD_DIM = 128
DN_DIM = DN_HEADS * DN_HEAD_DIM
DN_CONV = 4
DN_CHUNK = 64
GLA_HEADS = 4
GLA_KEY_DIM = D_MODEL // 2
GLA_VAL_DIM = D_MODEL
GLA_GATE_RANK = 16
GLA_GATE_NORM = 16.0
GLA_CHUNK = 64
N_EVEN = (DEPTH + 1) // 2
N_ODD = DEPTH // 2
EVEN_IN_SIZES = (SWA_Q_DIM, SWA_KV_DIM, SWA_KV_DIM, DN_DIM, DN_DIM, DN_DIM, DN_DIM, DN_HEADS, DN_HEADS)
EVEN_IN_DIM = sum(EVEN_IN_SIZES)
EVEN_MIX_DIM = SWA_Q_DIM + DN_DIM
ODD_IN_SIZES = (GLA_KEY_DIM, GLA_KEY_DIM, GLA_VAL_DIM, GLA_VAL_DIM, GLA_GATE_RANK)
ODD_IN_DIM = sum(ODD_IN_SIZES)

kernel_name = "hybrid_swa_deltanet_gla_macaron"


def _split(t, sizes):
    idx = np.cumsum(sizes)[:-1].tolist()
    return jnp.split(t, idx, axis=-1)


def rms_norm(x, w):
    x32 = x.astype(jnp.float32)
    y = x32 * lax.rsqrt(jnp.mean(x32 * x32, axis=-1, keepdims=True) + NORM_EPS)
    return (y * w.astype(jnp.float32)).astype(x.dtype)


def l2_norm(x):
    return x * lax.rsqrt(jnp.sum(x * x, axis=-1, keepdims=True) + 1e-6)


def swiglu(x, w_gate, w_up, w_down):
    return (jax.nn.silu(x @ w_gate) * (x @ w_up)) @ w_down


def causal_depthwise_conv(x, w):
    k = w.shape[0]
    return lax.conv_general_dilated(x, w[:, None, :].astype(x.dtype), window_strides=(1,),
                                    padding=[(k - 1, 0)], dimension_numbers=('NWC', 'WIO', 'NWC'),
                                    feature_group_count=x.shape[-1])


def t5_bucket(rel):
    n = jnp.maximum(rel, 0)
    max_exact = REL_BUCKETS // 2
    n_f = jnp.maximum(n, 1).astype(jnp.float32)
    large = max_exact + (jnp.log(n_f / max_exact) / math.log(REL_MAX_DIST / max_exact)
                         * (REL_BUCKETS - max_exact)).astype(jnp.int32)
    large = jnp.minimum(large, REL_BUCKETS - 1)
    return jnp.where(n < max_exact, n, large)


def to_chunks(t, chunk):
    pad = chunk - N_META
    t = jnp.pad(t, [(0, 0), (pad, 0)] + [(0, 0)] * (t.ndim - 2))
    b, lp = t.shape[:2]
    t = t.reshape(b, lp // chunk, chunk, *t.shape[2:])
    return jnp.moveaxis(t, 3, 1)


def from_chunks(t, chunk):
    t = jnp.moveaxis(t, 1, 3)
    b, n, c = t.shape[:3]
    return t.reshape(b, n * c, *t.shape[3:])[:, chunk - N_META:]


def _chunk_first(t):
    return jnp.moveaxis(t, 2, 0)


def sliding_window_attention(q, k, v, sinks, rel_table):
    f32 = jnp.float32
    b, l = q.shape[:2]
    g = SWA_Q_HEADS // SWA_KV_HEADS
    dh = SWA_HEAD_DIM
    pad = SWA_BLOCK - N_META
    lp = l + pad
    nb = lp // SWA_BLOCK
    padw = ((0, 0), (pad, 0), (0, 0), (0, 0))
    qb = jnp.pad(q.astype(f32), padw).reshape(b, nb, SWA_BLOCK, SWA_KV_HEADS, g, dh)
    kb = jnp.pad(k.astype(f32), padw).reshape(b, nb, SWA_BLOCK, SWA_KV_HEADS, dh)
    vb = jnp.pad(v.astype(f32), padw).reshape(b, nb, SWA_BLOCK, SWA_KV_HEADS, dh)
    prev = ((0, 0), (1, 0), (0, 0), (0, 0), (0, 0))
    k_band = jnp.concatenate([jnp.pad(kb[:, :-1], prev), kb], axis=2)
    v_band = jnp.concatenate([jnp.pad(vb[:, :-1], prev), vb], axis=2)
    k_meta = k[:, :N_META].astype(f32)
    v_meta = v[:, :N_META].astype(f32)
    scale = dh ** -0.5
    s_band = jnp.einsum('bnqhgd,bnkhd->bhgnqk', qb, k_band) * scale
    s_meta = jnp.einsum('bnqhgd,bmhd->bhgnqm', qb, k_meta) * scale
    blk = jnp.arange(nb)[:, None]
    pos_q = blk * SWA_BLOCK + jnp.arange(SWA_BLOCK)[None, :] - pad
    pos_kb = blk * SWA_BLOCK + jnp.arange(2 * SWA_BLOCK)[None, :] - SWA_BLOCK - pad
    rel_b = pos_q[:, :, None] - pos_kb[:, None, :]
    mask_b = (pos_kb[:, None, :] >= N_META) & (rel_b >= 0) & (rel_b < SWA_WINDOW)
    rel_m = pos_q[:, :, None] - jnp.arange(N_META)[None, None, :]
    mask_m = rel_m >= 0
    table = rel_table.astype(f32)

    def rel_bias(rel):
        bias = table[t5_bucket(rel)]
        return jnp.moveaxis(bias, -1, 0).reshape(SWA_KV_HEADS, g, *rel.shape)

    s_band = jnp.where(mask_b, s_band + rel_bias(rel_b), NEG_INF)
    s_meta = jnp.where(mask_m, s_meta + rel_bias(rel_m), NEG_INF)
    sink = jnp.broadcast_to(sinks.astype(f32).reshape(SWA_KV_HEADS, g, 1, 1, 1),
                            (b, SWA_KV_HEADS, g, nb, SWA_BLOCK, 1))
    p = jax.nn.softmax(jnp.concatenate([s_band, s_meta, sink], axis=-1), axis=-1)
    nk = 2 * SWA_BLOCK
    o = (jnp.einsum('bhgnqk,bnkhd->bnqhgd', p[..., :nk], v_band)
         + jnp.einsum('bhgnqm,bmhd->bnqhgd', p[..., nk:nk + N_META], v_meta))
    return o.reshape(b, lp, SWA_Q_DIM)[:, pad:]


def chunk_gated_delta_rule(q, k, v, g, beta):
    c = q.shape[-2]
    tri_incl = jnp.tril(jnp.ones((c, c), bool))
    tri_strict = jnp.tril(jnp.ones((c, c), bool), -1)
    gc = jnp.cumsum(g, axis=-1)
    diff = gc[..., :, None] - gc[..., None, :]
    gamma = jnp.where(tri_incl, jnp.exp(jnp.where(tri_incl, diff, 0.0)), 0.0)
    k_beta = k * beta[..., None]
    v_beta = v * beta[..., None]
    a_strict = jnp.where(tri_strict, jnp.einsum('bhnid,bhnjd->bhnij', k_beta, k) * gamma, 0.0)
    m = a_strict + jnp.eye(c, dtype=jnp.float32)
    u = lax.linalg.triangular_solve(m, v_beta, left_side=True, lower=True, unit_diagonal=True)
    w = lax.linalg.triangular_solve(m, k_beta * jnp.exp(gc)[..., None], left_side=True,
                                    lower=True, unit_diagonal=True)
    attn = jnp.einsum('bhnid,bhnjd->bhnij', q, k) * gamma
    q_dec = q * jnp.exp(gc)[..., None]
    k_dec = k * jnp.exp(gc[..., -1:] - gc)[..., None]
    g_last = jnp.exp(gc[..., -1])

    def step(s, xs):
        attn_n, u_n, w_n, qd_n, kd_n, gl_n = xs
        v_new = u_n - jnp.einsum('bhik,bhkv->bhiv', w_n, s)
        o = jnp.einsum('bhik,bhkv->bhiv', qd_n, s) + jnp.einsum('bhij,bhjv->bhiv', attn_n, v_new)
        s = s * gl_n[..., None, None] + jnp.einsum('bhik,bhiv->bhkv', kd_n, v_new)
        return s, o

    bsz, h, _, _, dk = q.shape
    s0 = jnp.zeros((bsz, h, dk, v.shape[-1]), jnp.float32)
    xs = tuple(_chunk_first(t) for t in (attn, u, w, q_dec, k_dec, g_last))
    _, o = lax.scan(step, s0, xs)
    return jnp.moveaxis(o, 0, 2)


def chunk_gla(q, k, v, glog):
    c = q.shape[-2]
    tri = jnp.tril(jnp.ones((c, c), bool))
    bcum = jnp.cumsum(glog, axis=-2)
    q_dec = q * jnp.exp(bcum)
    attn = jnp.where(tri, jnp.einsum('bhnik,bhnjk->bhnij', q_dec, k * jnp.exp(-bcum)), 0.0)
    o_intra = jnp.einsum('bhnij,bhnjv->bhniv', attn, v)
    b_last = bcum[..., -1:, :]
    k_dec = k * jnp.exp(b_last - bcum)
    decay = jnp.exp(b_last[..., 0, :])

    def step(s, xs):
        qd, kd, vv, dl = xs
        o = jnp.einsum('bhik,bhkv->bhiv', qd, s)
        s = s * dl[..., :, None] + jnp.einsum('bhik,bhiv->bhkv', kd, vv)
        return s, o

    bsz, h, _, _, dk = q.shape
    s0 = jnp.zeros((bsz, h, dk, v.shape[-1]), jnp.float32)
    _, o_inter = lax.scan(step, s0, tuple(_chunk_first(t) for t in (q_dec, k_dec, v, decay)))
    return o_intra + jnp.moveaxis(o_inter, 0, 2)


def gated_deltanet(q, k, v, a, bgate, z, conv_w, a_log, dt_bias, norm_w):
    f32 = jnp.float32
    bsz, l, _ = q.shape
    hd = (bsz, l, DN_HEADS, DN_HEAD_DIM)
    qkv = jax.nn.silu(causal_depthwise_conv(jnp.concatenate([q, k, v], axis=-1), conv_w))
    q, k, v = _split(qkv, (DN_DIM, DN_DIM, DN_DIM))
    q = l2_norm(q.reshape(hd).astype(f32)) * DN_HEAD_DIM ** -0.5
    k = l2_norm(k.reshape(hd).astype(f32))
    v = v.reshape(hd).astype(f32)
    beta = jax.nn.sigmoid(bgate.astype(f32))
    g = -jnp.exp(a_log.astype(f32)) * jax.nn.softplus(a.astype(f32) + dt_bias.astype(f32))
    o = chunk_gated_delta_rule(to_chunks(q, DN_CHUNK), to_chunks(k, DN_CHUNK), to_chunks(v, DN_CHUNK),
                               to_chunks(g, DN_CHUNK), to_chunks(beta, DN_CHUNK))
    o = from_chunks(o, DN_CHUNK)
    o = rms_norm(o, norm_w) * jax.nn.silu(z.reshape(hd).astype(f32))
    return o.reshape(bsz, l, DN_DIM).astype(z.dtype)


def even_mixer(h, w_in, conv_w, sinks, a_log, dt_bias, dn_norm_w, w_out, rel_table):
    bsz, l, _ = h.shape
    qa, ka, va, qb, kb, vb, zb, bb, ab = _split(h @ w_in, EVEN_IN_SIZES)
    o_a = sliding_window_attention(qa.reshape(bsz, l, SWA_Q_HEADS, SWA_HEAD_DIM),
                                   ka.reshape(bsz, l, SWA_KV_HEADS, SWA_HEAD_DIM),
                                   va.reshape(bsz, l, SWA_KV_HEADS, SWA_HEAD_DIM), sinks, rel_table)
    o_b = gated_deltanet(qb, kb, vb, ab, bb, zb, conv_w, a_log, dt_bias, dn_norm_w)
    return jnp.concatenate([o_a.astype(h.dtype), o_b.astype(h.dtype)], axis=-1) @ w_out


def odd_mixer(h, w_in, w_gate_up, b_gate, norm_w, w_out):
    f32 = jnp.float32
    bsz, l, _ = h.shape
    dk = GLA_KEY_DIM // GLA_HEADS
    dv = GLA_VAL_DIM // GLA_HEADS
    q, k, v, g, gk_low = _split(h @ w_in, ODD_IN_SIZES)
    glog = jax.nn.log_sigmoid((gk_low @ w_gate_up + b_gate).astype(f32)) / GLA_GATE_NORM
    q = q.reshape(bsz, l, GLA_HEADS, dk).astype(f32) * dk ** -0.5
    k = k.reshape(bsz, l, GLA_HEADS, dk).astype(f32)
    v = v.reshape(bsz, l, GLA_HEADS, dv).astype(f32)
    glog = glog.reshape(bsz, l, GLA_HEADS, dk)
    o = chunk_gla(to_chunks(q, GLA_CHUNK), to_chunks(k, GLA_CHUNK), to_chunks(v, GLA_CHUNK),
                  to_chunks(glog, GLA_CHUNK))
    o = from_chunks(o, GLA_CHUNK)
    o = rms_norm(o, norm_w) * jax.nn.silu(g.reshape(bsz, l, GLA_HEADS, dv).astype(f32))
    return o.reshape(bsz, l, GLA_VAL_DIM).astype(h.dtype) @ w_out


def _fwd_setup_inputs(seed: int = 0) -> dict:
    key = jax.random.key(seed)
    ks = jax.random.split(key, 24)
    f32 = jnp.float32

    def nrm(k, shape, fan_in):
        return jax.random.normal(k, shape, f32) * fan_in ** -0.5

    dt = jnp.exp(jax.random.uniform(ks[10], (N_EVEN, DN_HEADS), f32) * (math.log(0.1) - math.log(0.001))
                 + math.log(0.001))
    return {
        "x": jax.random.normal(ks[0], (BATCH, SEQ, D_MODEL), f32),
        "meta_tokens": jax.random.normal(ks[1], (N_META, D_MODEL), f32),
        "norm_w": 1.0 + 0.02 * jax.random.normal(ks[2], (DEPTH, 6, D_MODEL), f32),
        "ffn_w_gate": nrm(ks[3], (DEPTH, 2, D_MODEL, D_FF), D_MODEL),
        "ffn_w_up": nrm(ks[4], (DEPTH, 2, D_MODEL, D_FF), D_MODEL),
        "ffn_w_down": nrm(ks[5], (DEPTH, 2, D_FF, D_MODEL), D_FF),
        "rel_bias_table": 0.5 * jax.random.normal(ks[6], (REL_BUCKETS, SWA_Q_HEADS), f32),
        "even_w_in": nrm(ks[7], (N_EVEN, D_MODEL, EVEN_IN_DIM), D_MODEL),
        "even_conv_w": nrm(ks[8], (N_EVEN, DN_CONV, 3 * DN_DIM), DN_CONV),
        "swa_sinks": jax.random.normal(ks[9], (N_EVEN, SWA_Q_HEADS), f32),
        "dn_a_log": jnp.log(jax.random.uniform(ks[11], (N_EVEN, DN_HEADS), f32, 1.0, 16.0)),
        "dn_dt_bias": dt + jnp.log(-jnp.expm1(-dt)),
        "dn_norm_w": 1.0 + 0.02 * jax.random.normal(ks[12], (N_EVEN, DN_HEAD_DIM), f32),
        "even_w_out": nrm(ks[13], (N_EVEN, EVEN_MIX_DIM, D_MODEL), EVEN_MIX_DIM),
        "odd_w_in": nrm(ks[14], (N_ODD, D_MODEL, ODD_IN_DIM), D_MODEL),
        "gla_w_gate_up": nrm(ks[15], (N_ODD, GLA_GATE_RANK, GLA_KEY_DIM), GLA_GATE_RANK),
        "gla_b_gate": 0.1 * jax.random.normal(ks[16], (N_ODD, GLA_KEY_DIM), f32),
        "gla_norm_w": 1.0 + 0.02 * jax.random.normal(ks[17], (N_ODD, GLA_VAL_DIM // GLA_HEADS), f32),
        "odd_w_out": nrm(ks[18], (N_ODD, GLA_VAL_DIM, D_MODEL), GLA_VAL_DIM),
    }


def _fwd_reference(x, meta_tokens, norm_w, ffn_w_gate, ffn_w_up, ffn_w_down, rel_bias_table,
              even_w_in, even_conv_w, swa_sinks, dn_a_log, dn_dt_bias, dn_norm_w, even_w_out,
              odd_w_in, gla_w_gate_up, gla_b_gate, gla_norm_w, odd_w_out):
    bsz = x.shape[0]
    meta = jnp.broadcast_to(meta_tokens[None].astype(x.dtype), (bsz, N_META, x.shape[-1]))
    h = jnp.concatenate([meta, x], axis=1)
    for layer in range(DEPTH):
        nw = norm_w[layer]
        f = swiglu(rms_norm(h, nw[0]), ffn_w_gate[layer, 0], ffn_w_up[layer, 0], ffn_w_down[layer, 0])
        h = h + 0.5 * rms_norm(f, nw[1])
        hn = rms_norm(h, nw[2])
        if layer % 2 == 0:
            i = layer // 2
            mix = even_mixer(hn, even_w_in[i], even_conv_w[i], swa_sinks[i], dn_a_log[i], dn_dt_bias[i],
                             dn_norm_w[i], even_w_out[i], rel_bias_table)
        else:
            i = layer // 2
            mix = odd_mixer(hn, odd_w_in[i], gla_w_gate_up[i], gla_b_gate[i], gla_norm_w[i], odd_w_out[i])
        h = h + rms_norm(mix, nw[3])
        f = swiglu(rms_norm(h, nw[4]), ffn_w_gate[layer, 1], ffn_w_up[layer, 1], ffn_w_down[layer, 1])
        h = h + 0.5 * rms_norm(f, nw[5])
    return h[:, N_META:]


import jax as _jax
import jax.numpy as _jnp

TWIN_FORMAT = 'train_step'
FWD_PARAMS = ['x', 'meta_tokens', 'norm_w', 'ffn_w_gate', 'ffn_w_up', 'ffn_w_down', 'rel_bias_table', 'even_w_in', 'even_conv_w', 'swa_sinks', 'dn_a_log', 'dn_dt_bias', 'dn_norm_w', 'even_w_out', 'odd_w_in', 'gla_w_gate_up', 'gla_b_gate', 'gla_norm_w', 'odd_w_out']
TWIN_WEIGHTS = ['meta_tokens', 'norm_w', 'ffn_w_gate', 'ffn_w_up', 'ffn_w_down', 'rel_bias_table', 'even_w_in', 'even_conv_w', 'swa_sinks', 'dn_a_log', 'dn_dt_bias', 'dn_norm_w', 'even_w_out', 'odd_w_in', 'gla_w_gate_up', 'gla_b_gate', 'gla_norm_w', 'odd_w_out']
TWIN_DIFF_INPUT = 'x'
TWIN_INPUTS = ['x', 'meta_tokens', 'norm_w', 'ffn_w_gate', 'ffn_w_up', 'ffn_w_down', 'rel_bias_table', 'even_w_in', 'even_conv_w', 'swa_sinks', 'dn_a_log', 'dn_dt_bias', 'dn_norm_w', 'even_w_out', 'odd_w_in', 'gla_w_gate_up', 'gla_b_gate', 'gla_norm_w', 'odd_w_out', 'loss_target', 'm_meta_tokens', 'm_norm_w', 'm_ffn_w_gate', 'm_ffn_w_up', 'm_ffn_w_down', 'm_rel_bias_table', 'm_even_w_in', 'm_even_conv_w', 'm_swa_sinks', 'm_dn_a_log', 'm_dn_dt_bias', 'm_dn_norm_w', 'm_even_w_out', 'm_odd_w_in', 'm_gla_w_gate_up', 'm_gla_b_gate', 'm_gla_norm_w', 'm_odd_w_out', 'v_meta_tokens', 'v_norm_w', 'v_ffn_w_gate', 'v_ffn_w_up', 'v_ffn_w_down', 'v_rel_bias_table', 'v_even_w_in', 'v_even_conv_w', 'v_swa_sinks', 'v_dn_a_log', 'v_dn_dt_bias', 'v_dn_norm_w', 'v_even_w_out', 'v_odd_w_in', 'v_gla_w_gate_up', 'v_gla_b_gate', 'v_gla_norm_w', 'v_odd_w_out']
TWIN_OUTPUTS = ['loss', 'grad_x', 'grad_meta_tokens', 'grad_norm_w', 'grad_ffn_w_gate', 'grad_ffn_w_up', 'grad_ffn_w_down', 'grad_rel_bias_table', 'grad_even_w_in', 'grad_even_conv_w', 'grad_swa_sinks', 'grad_dn_a_log', 'grad_dn_dt_bias', 'grad_dn_norm_w', 'grad_even_w_out', 'grad_odd_w_in', 'grad_gla_w_gate_up', 'grad_gla_b_gate', 'grad_gla_norm_w', 'grad_odd_w_out', 'delta_meta_tokens', 'delta_norm_w', 'delta_ffn_w_gate', 'delta_ffn_w_up', 'delta_ffn_w_down', 'delta_rel_bias_table', 'delta_even_w_in', 'delta_even_conv_w', 'delta_swa_sinks', 'delta_dn_a_log', 'delta_dn_dt_bias', 'delta_dn_norm_w', 'delta_even_w_out', 'delta_odd_w_in', 'delta_gla_w_gate_up', 'delta_gla_b_gate', 'delta_gla_norm_w', 'delta_odd_w_out', 'new_m_meta_tokens', 'new_m_norm_w', 'new_m_ffn_w_gate', 'new_m_ffn_w_up', 'new_m_ffn_w_down', 'new_m_rel_bias_table', 'new_m_even_w_in', 'new_m_even_conv_w', 'new_m_swa_sinks', 'new_m_dn_a_log', 'new_m_dn_dt_bias', 'new_m_dn_norm_w', 'new_m_even_w_out', 'new_m_odd_w_in', 'new_m_gla_w_gate_up', 'new_m_gla_b_gate', 'new_m_gla_norm_w', 'new_m_odd_w_out', 'new_v_meta_tokens', 'new_v_norm_w', 'new_v_ffn_w_gate', 'new_v_ffn_w_up', 'new_v_ffn_w_down', 'new_v_rel_bias_table', 'new_v_even_w_in', 'new_v_even_conv_w', 'new_v_swa_sinks', 'new_v_dn_a_log', 'new_v_dn_dt_bias', 'new_v_dn_norm_w', 'new_v_even_w_out', 'new_v_odd_w_in', 'new_v_gla_w_gate_up', 'new_v_gla_b_gate', 'new_v_gla_norm_w', 'new_v_odd_w_out']
TWIN_LEAF_KINDS = {'loss': 'loss', 'grad_x': 'grad_x', 'grad_meta_tokens': 'grad_w', 'grad_norm_w': 'grad_w', 'grad_ffn_w_gate': 'grad_w', 'grad_ffn_w_up': 'grad_w', 'grad_ffn_w_down': 'grad_w', 'grad_rel_bias_table': 'grad_w', 'grad_even_w_in': 'grad_w', 'grad_even_conv_w': 'grad_w', 'grad_swa_sinks': 'grad_w', 'grad_dn_a_log': 'grad_w', 'grad_dn_dt_bias': 'grad_w', 'grad_dn_norm_w': 'grad_w', 'grad_even_w_out': 'grad_w', 'grad_odd_w_in': 'grad_w', 'grad_gla_w_gate_up': 'grad_w', 'grad_gla_b_gate': 'grad_w', 'grad_gla_norm_w': 'grad_w', 'grad_odd_w_out': 'grad_w', 'delta_meta_tokens': 'delta_w', 'delta_norm_w': 'delta_w', 'delta_ffn_w_gate': 'delta_w', 'delta_ffn_w_up': 'delta_w', 'delta_ffn_w_down': 'delta_w', 'delta_rel_bias_table': 'delta_w', 'delta_even_w_in': 'delta_w', 'delta_even_conv_w': 'delta_w', 'delta_swa_sinks': 'delta_w', 'delta_dn_a_log': 'delta_w', 'delta_dn_dt_bias': 'delta_w', 'delta_dn_norm_w': 'delta_w', 'delta_even_w_out': 'delta_w', 'delta_odd_w_in': 'delta_w', 'delta_gla_w_gate_up': 'delta_w', 'delta_gla_b_gate': 'delta_w', 'delta_gla_norm_w': 'delta_w', 'delta_odd_w_out': 'delta_w', 'new_m_meta_tokens': 'new_m', 'new_m_norm_w': 'new_m', 'new_m_ffn_w_gate': 'new_m', 'new_m_ffn_w_up': 'new_m', 'new_m_ffn_w_down': 'new_m', 'new_m_rel_bias_table': 'new_m', 'new_m_even_w_in': 'new_m', 'new_m_even_conv_w': 'new_m', 'new_m_swa_sinks': 'new_m', 'new_m_dn_a_log': 'new_m', 'new_m_dn_dt_bias': 'new_m', 'new_m_dn_norm_w': 'new_m', 'new_m_even_w_out': 'new_m', 'new_m_odd_w_in': 'new_m', 'new_m_gla_w_gate_up': 'new_m', 'new_m_gla_b_gate': 'new_m', 'new_m_gla_norm_w': 'new_m', 'new_m_odd_w_out': 'new_m', 'new_v_meta_tokens': 'new_v', 'new_v_norm_w': 'new_v', 'new_v_ffn_w_gate': 'new_v', 'new_v_ffn_w_up': 'new_v', 'new_v_ffn_w_down': 'new_v', 'new_v_rel_bias_table': 'new_v', 'new_v_even_w_in': 'new_v', 'new_v_even_conv_w': 'new_v', 'new_v_swa_sinks': 'new_v', 'new_v_dn_a_log': 'new_v', 'new_v_dn_dt_bias': 'new_v', 'new_v_dn_norm_w': 'new_v', 'new_v_even_w_out': 'new_v', 'new_v_odd_w_in': 'new_v', 'new_v_gla_w_gate_up': 'new_v', 'new_v_gla_b_gate': 'new_v', 'new_v_gla_norm_w': 'new_v', 'new_v_odd_w_out': 'new_v'}


def _forward(args):
    return _fwd_reference(*[args[k] for k in FWD_PARAMS])


def _output_shape():
    def fwd():
        inp = _fwd_setup_inputs(0)
        return _fwd_reference(*[inp[k] for k in FWD_PARAMS])
    out = _jax.eval_shape(fwd)
    return out.shape, out.dtype

N_MICROBATCH = 1
ADAM_LR = 0.001
ADAM_B1 = 0.9
ADAM_B2 = 0.999
ADAM_EPS = 1e-08
ADAM_WD = 0.01
ADAM_STEP = 10
PER_EXAMPLE_BATCH_AXIS = {'x': 0, 'loss_target': 0}
SHARED_INPUTS = []
_WEIGHT_DTYPES = {'meta_tokens': _jnp.float32, 'norm_w': _jnp.float32, 'ffn_w_gate': _jnp.float32, 'ffn_w_up': _jnp.float32, 'ffn_w_down': _jnp.float32, 'rel_bias_table': _jnp.float32, 'even_w_in': _jnp.float32, 'even_conv_w': _jnp.float32, 'swa_sinks': _jnp.float32, 'dn_a_log': _jnp.float32, 'dn_dt_bias': _jnp.float32, 'dn_norm_w': _jnp.float32, 'even_w_out': _jnp.float32, 'odd_w_in': _jnp.float32, 'gla_w_gate_up': _jnp.float32, 'gla_b_gate': _jnp.float32, 'gla_norm_w': _jnp.float32, 'odd_w_out': _jnp.float32}
MOMENT_SCALE = {'meta_tokens': 7.235296e-02, 'norm_w': 1.379313e+01, 'ffn_w_gate': 2.759982e-01, 'ffn_w_up': 3.099709e-01, 'ffn_w_down': 5.150839e-01, 'rel_bias_table': 4.058525e-01, 'even_w_in': 8.010127e-01, 'even_conv_w': 8.958198e-01, 'swa_sinks': 1.230460e-01, 'dn_a_log': 5.807166e+00, 'dn_dt_bias': 5.475800e+00, 'dn_norm_w': 2.832151e+00, 'even_w_out': 1.073951e+00, 'odd_w_in': 4.843111e-01, 'gla_w_gate_up': 7.684928e-02, 'gla_b_gate': 3.204965e-01, 'gla_norm_w': 8.872845e-01, 'odd_w_out': 4.305628e-01}


def _to_microbatches(a, axis):
    t = _jnp.moveaxis(a, axis, 0)
    t = t.reshape((N_MICROBATCH, t.shape[0] // N_MICROBATCH) + t.shape[1:])
    return _jnp.moveaxis(t, 1, axis + 1)


def setup_inputs(seed: int = 0) -> dict:
    inp = _fwd_setup_inputs(seed)
    key = _jax.random.fold_in(_jax.random.key(seed), 7919)
    shape, _ = _output_shape()
    out = dict(inp)
    out["loss_target"] = _jax.random.normal(_jax.random.fold_in(key, 0), shape, _jnp.float32)
    for i, name in enumerate(TWIN_WEIGHTS):
        w = inp[name].astype(_jnp.float32)
        if MOMENT_SCALE is None:
            s = _jnp.sqrt(_jnp.mean(_jnp.square(w)) + 1e-30)
        else:
            s = MOMENT_SCALE[name]
        km, kv = _jax.random.split(_jax.random.fold_in(key, i + 1))
        out[name] = w
        out["m_" + name] = s * _jax.random.normal(km, w.shape, _jnp.float32)
        out["v_" + name] = (s * s) * _jax.random.uniform(kv, w.shape, _jnp.float32, 0.5, 1.5)
    if N_MICROBATCH > 1:
        for name, axis in PER_EXAMPLE_BATCH_AXIS.items():
            out[name] = _to_microbatches(out[name], axis)
    return {'x': out['x'], 'meta_tokens': out['meta_tokens'], 'norm_w': out['norm_w'], 'ffn_w_gate': out['ffn_w_gate'], 'ffn_w_up': out['ffn_w_up'], 'ffn_w_down': out['ffn_w_down'], 'rel_bias_table': out['rel_bias_table'], 'even_w_in': out['even_w_in'], 'even_conv_w': out['even_conv_w'], 'swa_sinks': out['swa_sinks'], 'dn_a_log': out['dn_a_log'], 'dn_dt_bias': out['dn_dt_bias'], 'dn_norm_w': out['dn_norm_w'], 'even_w_out': out['even_w_out'], 'odd_w_in': out['odd_w_in'], 'gla_w_gate_up': out['gla_w_gate_up'], 'gla_b_gate': out['gla_b_gate'], 'gla_norm_w': out['gla_norm_w'], 'odd_w_out': out['odd_w_out'], 'loss_target': out['loss_target'], 'm_meta_tokens': out['m_meta_tokens'], 'm_norm_w': out['m_norm_w'], 'm_ffn_w_gate': out['m_ffn_w_gate'], 'm_ffn_w_up': out['m_ffn_w_up'], 'm_ffn_w_down': out['m_ffn_w_down'], 'm_rel_bias_table': out['m_rel_bias_table'], 'm_even_w_in': out['m_even_w_in'], 'm_even_conv_w': out['m_even_conv_w'], 'm_swa_sinks': out['m_swa_sinks'], 'm_dn_a_log': out['m_dn_a_log'], 'm_dn_dt_bias': out['m_dn_dt_bias'], 'm_dn_norm_w': out['m_dn_norm_w'], 'm_even_w_out': out['m_even_w_out'], 'm_odd_w_in': out['m_odd_w_in'], 'm_gla_w_gate_up': out['m_gla_w_gate_up'], 'm_gla_b_gate': out['m_gla_b_gate'], 'm_gla_norm_w': out['m_gla_norm_w'], 'm_odd_w_out': out['m_odd_w_out'], 'v_meta_tokens': out['v_meta_tokens'], 'v_norm_w': out['v_norm_w'], 'v_ffn_w_gate': out['v_ffn_w_gate'], 'v_ffn_w_up': out['v_ffn_w_up'], 'v_ffn_w_down': out['v_ffn_w_down'], 'v_rel_bias_table': out['v_rel_bias_table'], 'v_even_w_in': out['v_even_w_in'], 'v_even_conv_w': out['v_even_conv_w'], 'v_swa_sinks': out['v_swa_sinks'], 'v_dn_a_log': out['v_dn_a_log'], 'v_dn_dt_bias': out['v_dn_dt_bias'], 'v_dn_norm_w': out['v_dn_norm_w'], 'v_even_w_out': out['v_even_w_out'], 'v_odd_w_in': out['v_odd_w_in'], 'v_gla_w_gate_up': out['v_gla_w_gate_up'], 'v_gla_b_gate': out['v_gla_b_gate'], 'v_gla_norm_w': out['v_gla_norm_w'], 'v_odd_w_out': out['v_odd_w_out']}


def _loss(weights, diff, rest, loss_target):
    with _jax.named_scope("forward"):
        args = {**rest, TWIN_DIFF_INPUT: diff, **{k: w.astype(_WEIGHT_DTYPES[k]) for k, w in weights.items()}}
        y = _forward(args)
    with _jax.named_scope("loss_head"):
        err = _jnp.square(y.astype(_jnp.float32) - loss_target)
        return 0.5 * _jnp.sum(_jnp.mean(err, axis=-1)) if err.ndim else 0.5 * err


def _adamw(w, g, m, v):
    m = ADAM_B1 * m + (1.0 - ADAM_B1) * g
    v = ADAM_B2 * v + (1.0 - ADAM_B2) * _jnp.square(g)
    m_hat = m / (1.0 - ADAM_B1 ** ADAM_STEP)
    v_hat = v / (1.0 - ADAM_B2 ** ADAM_STEP)
    delta = -ADAM_LR * (m_hat / (_jnp.sqrt(v_hat) + ADAM_EPS) + ADAM_WD * w)
    return delta, m, v


def reference(x, meta_tokens, norm_w, ffn_w_gate, ffn_w_up, ffn_w_down, rel_bias_table, even_w_in, even_conv_w, swa_sinks, dn_a_log, dn_dt_bias, dn_norm_w, even_w_out, odd_w_in, gla_w_gate_up, gla_b_gate, gla_norm_w, odd_w_out, loss_target, m_meta_tokens, m_norm_w, m_ffn_w_gate, m_ffn_w_up, m_ffn_w_down, m_rel_bias_table, m_even_w_in, m_even_conv_w, m_swa_sinks, m_dn_a_log, m_dn_dt_bias, m_dn_norm_w, m_even_w_out, m_odd_w_in, m_gla_w_gate_up, m_gla_b_gate, m_gla_norm_w, m_odd_w_out, v_meta_tokens, v_norm_w, v_ffn_w_gate, v_ffn_w_up, v_ffn_w_down, v_rel_bias_table, v_even_w_in, v_even_conv_w, v_swa_sinks, v_dn_a_log, v_dn_dt_bias, v_dn_norm_w, v_even_w_out, v_odd_w_in, v_gla_w_gate_up, v_gla_b_gate, v_gla_norm_w, v_odd_w_out):
    given = dict(x=x, meta_tokens=meta_tokens, norm_w=norm_w, ffn_w_gate=ffn_w_gate, ffn_w_up=ffn_w_up, ffn_w_down=ffn_w_down, rel_bias_table=rel_bias_table, even_w_in=even_w_in, even_conv_w=even_conv_w, swa_sinks=swa_sinks, dn_a_log=dn_a_log, dn_dt_bias=dn_dt_bias, dn_norm_w=dn_norm_w, even_w_out=even_w_out, odd_w_in=odd_w_in, gla_w_gate_up=gla_w_gate_up, gla_b_gate=gla_b_gate, gla_norm_w=gla_norm_w, odd_w_out=odd_w_out, loss_target=loss_target, m_meta_tokens=m_meta_tokens, m_norm_w=m_norm_w, m_ffn_w_gate=m_ffn_w_gate, m_ffn_w_up=m_ffn_w_up, m_ffn_w_down=m_ffn_w_down, m_rel_bias_table=m_rel_bias_table, m_even_w_in=m_even_w_in, m_even_conv_w=m_even_conv_w, m_swa_sinks=m_swa_sinks, m_dn_a_log=m_dn_a_log, m_dn_dt_bias=m_dn_dt_bias, m_dn_norm_w=m_dn_norm_w, m_even_w_out=m_even_w_out, m_odd_w_in=m_odd_w_in, m_gla_w_gate_up=m_gla_w_gate_up, m_gla_b_gate=m_gla_b_gate, m_gla_norm_w=m_gla_norm_w, m_odd_w_out=m_odd_w_out, v_meta_tokens=v_meta_tokens, v_norm_w=v_norm_w, v_ffn_w_gate=v_ffn_w_gate, v_ffn_w_up=v_ffn_w_up, v_ffn_w_down=v_ffn_w_down, v_rel_bias_table=v_rel_bias_table, v_even_w_in=v_even_w_in, v_even_conv_w=v_even_conv_w, v_swa_sinks=v_swa_sinks, v_dn_a_log=v_dn_a_log, v_dn_dt_bias=v_dn_dt_bias, v_dn_norm_w=v_dn_norm_w, v_even_w_out=v_even_w_out, v_odd_w_in=v_odd_w_in, v_gla_w_gate_up=v_gla_w_gate_up, v_gla_b_gate=v_gla_b_gate, v_gla_norm_w=v_gla_norm_w, v_odd_w_out=v_odd_w_out)
    weights = {n: given[n] for n in TWIN_WEIGHTS}
    shared = {n: given[n] for n in SHARED_INPUTS}
    per_example = {n: given[n] for n in ['x']}
    grad_fn = _jax.value_and_grad(_loss, argnums=(0, 1))

    def one_microbatch(ex, loss_target):
        ex = dict(ex)
        diff = ex.pop(TWIN_DIFF_INPUT)
        return grad_fn(weights, diff, {**shared, **ex}, loss_target)

    if N_MICROBATCH == 1:
        loss, (grad_w, grad_x) = one_microbatch(per_example, given["loss_target"])
    else:
        def body(carry, xs):
            loss_sum, grad_sum = carry
            l_k, (gw_k, gx_k) = one_microbatch(xs[0], xs[1])
            with _jax.named_scope("update"):
                return (loss_sum + l_k, _jax.tree.map(_jnp.add, grad_sum, gw_k)), gx_k

        init = (_jnp.zeros((), _jnp.float32), _jax.tree.map(_jnp.zeros_like, weights))
        (loss, grad_w), grad_x = _jax.lax.scan(body, init, (per_example, given["loss_target"]))
    with _jax.named_scope("update"):
        delta_w, new_m, new_v = {}, {}, {}
        for n in TWIN_WEIGHTS:
            delta_w[n], new_m[n], new_v[n] = _adamw(weights[n], grad_w[n], given["m_" + n], given["v_" + n])
    return (loss, grad_x, *[grad_w[n] for n in TWIN_WEIGHTS], *[delta_w[n] for n in TWIN_WEIGHTS],
            *[new_m[n] for n in TWIN_WEIGHTS], *[new_v[n] for n in TWIN_WEIGHTS])
```

```python
import functools
import math

import numpy as np
import jax
import jax.numpy as jnp
from jax import lax
from jax.experimental import pallas as pl
from jax.experimental.pallas import tpu as pltpu

F32 = jnp.float32
BF16 = jnp.bfloat16
HI = lax.Precision.HIGHEST

D = 1024
N_META = 16
PADR = 128
ZROWS = PADR - N_META
D_FF = 2816
NSH = 4
FSH = D_FF // NSH
EPS = 1e-6
NEG = -1e30
CH = 64
BLK = 128
LANE = 128
VMEM_LIMIT = 56 * 1024 * 1024

E_QA, E_KA, E_VA, E_QB, E_KB, E_VB, E_ZB, E_BA, E_END = 0, 1024, 1280, 1536, 2048, 2560, 3072, 3584, 4096


def _even_in_map():
    m = np.full((E_END,), -1, np.int64)
    for h in range(8):
        m[E_QA + h * 128:E_QA + h * 128 + 64] = np.arange(h * 64, (h + 1) * 64)
    for h in range(2):
        m[E_KA + h * 128:E_KA + h * 128 + 64] = 512 + np.arange(h * 64, (h + 1) * 64)
        m[E_VA + h * 128:E_VA + h * 128 + 64] = 640 + np.arange(h * 64, (h + 1) * 64)
    m[E_QB:E_QB + 2048] = 768 + np.arange(2048)
    m[E_BA:E_BA + 8] = 2816 + np.arange(8)
    return m


def _even_out_map():
    m = np.full((1536,), -1, np.int64)
    for h in range(8):
        m[h * 128:h * 128 + 64] = np.arange(h * 64, (h + 1) * 64)
    m[1024:1536] = 512 + np.arange(512)
    return m


O_Q, O_K, O_V, O_G, O_GK, O_END = 0, 512, 1024, 2048, 3072, 3584


def _odd_in_map():
    m = np.full((O_END,), -1, np.int64)
    m[:3072] = np.arange(3072)
    m[O_GK:O_GK + 16] = 3072 + np.arange(16)
    return m


def _inverse(m, n):
    inv = np.zeros((n,), np.int64)
    for p, o in enumerate(m):
        if o >= 0:
            inv[o] = p
    return inv


def _take_pad(w, m, axis):
    t = jnp.take(w, jnp.asarray(np.maximum(m, 0)), axis=axis)
    shape = [1] * w.ndim
    shape[axis] = m.shape[0]
    return jnp.where(jnp.asarray(m >= 0).reshape(shape), t, jnp.zeros((), w.dtype))


def _mm(a, b, prec=HI):
    return lax.dot_general(a, b, (((1,), (0,)), ((), ())), precision=prec, preferred_element_type=F32)


def _mm_nt(a, b, prec=HI):
    return lax.dot_general(a, b, (((1,), (1,)), ((), ())), precision=prec, preferred_element_type=F32)


def _mm_tn(a, b, prec=HI):
    return lax.dot_general(a, b, (((0,), (0,)), ((), ())), precision=prec, preferred_element_type=F32)


def _rms(x, w):
    return x * lax.rsqrt(jnp.mean(x * x, axis=-1, keepdims=True) + EPS) * w


def _sigmoid(x):
    return 1.0 / (1.0 + jnp.exp(-x))


def _silu(x):
    return x * _sigmoid(x)


def _softplus(x):
    return jnp.maximum(x, 0.0) + jnp.log(1.0 + jnp.exp(-jnp.abs(x)))


def _lane_pick(row, idx):
    lane = lax.broadcasted_iota(jnp.int32, row.shape, row.ndim - 1)
    return jnp.sum(jnp.where(lane == idx, row, 0.0), axis=-1, keepdims=True)


def _row_ids(row0, n):
    return row0 + lax.broadcasted_iota(jnp.int32, (n, 1), 0)


def _pick(m, cap):
    best = 64
    for t in range(64, min(m, cap) + 1, 64):
        if m % t == 0:
            best = t
    return best


def _cparams(sem):
    return pltpu.CompilerParams(dimension_semantics=sem, vmem_limit_bytes=VMEM_LIMIT)


def mm_nn(a, b, name, out_dtype=F32):
    M, K = a.shape
    N = b.shape[1]
    tm = _pick(M, 1408)
    tn = _pick(N, 512)

    def body(a_ref, b_ref, o_ref):
        o_ref[...] = _mm(a_ref[...], b_ref[...], None).astype(o_ref.dtype)

    return pl.pallas_call(
        body, name=name, grid=(N // tn, M // tm),
        in_specs=[pl.BlockSpec((tm, K), lambda j, i: (i, 0)), pl.BlockSpec((K, tn), lambda j, i: (0, j))],
        out_specs=pl.BlockSpec((tm, tn), lambda j, i: (i, j)),
        out_shape=jax.ShapeDtypeStruct((M, N), out_dtype),
        compiler_params=_cparams(("parallel", "parallel")),
    )(a, b)


def mm_nt(a, b, name, out_dtype=F32):
    M, K = a.shape
    N = b.shape[0]
    tm = _pick(M, 768)
    tn = _pick(N, 512)

    def body(a_ref, b_ref, o_ref):
        o_ref[...] = _mm_nt(a_ref[...], b_ref[...], None).astype(o_ref.dtype)

    return pl.pallas_call(
        body, name=name, grid=(N // tn, M // tm),
        in_specs=[pl.BlockSpec((tm, K), lambda j, i: (i, 0)), pl.BlockSpec((tn, K), lambda j, i: (j, 0))],
        out_specs=pl.BlockSpec((tm, tn), lambda j, i: (i, j)),
        out_shape=jax.ShapeDtypeStruct((M, N), out_dtype),
        compiler_params=_cparams(("parallel", "parallel")),
    )(a, b)


def mm_tn(a, b, name):
    M, K = a.shape
    N = b.shape[1]
    tk = _pick(K, 512)
    tn = _pick(N, 512)

    def body(a_ref, b_ref, o_ref):
        o_ref[...] = _mm_tn(a_ref[...], b_ref[...], None)

    return pl.pallas_call(
        body, name=name, grid=(K // tk, N // tn),
        in_specs=[pl.BlockSpec((M, tk), lambda i, j: (0, i)), pl.BlockSpec((M, tn), lambda i, j: (0, j))],
        out_specs=pl.BlockSpec((tk, tn), lambda i, j: (i, j)),
        out_shape=jax.ShapeDtypeStruct((K, N), F32),
        compiler_params=_cparams(("parallel", "parallel")),
    )(a, b)


def _row_specs(rows, tm):
    return [pl.BlockSpec((tm, w), functools.partial(lambda i, cb: (i, cb), cb=cb)) for (_, w, cb) in rows]


def _param_specs(params):
    return [pl.BlockSpec(p.shape, functools.partial(lambda i, nd: (0,) * nd, nd=p.ndim)) for p in params]


def rowwise_fwd(name, fn, rows, params, outs, tm=None):
    M = rows[0][0].shape[0]
    tm = tm or _pick(M, 704)
    nr, npar = len(rows), len(params)

    def body(*refs):
        row0 = pl.program_id(0) * tm
        vals = [r[...].astype(F32) for r in refs[:nr]] + [p[...] for p in refs[nr:nr + npar]]
        res = fn(row0, *vals)
        for o_ref, r in zip(refs[nr + npar:], res):
            o_ref[...] = r.astype(o_ref.dtype)

    return pl.pallas_call(
        body, name=name, grid=(M // tm,),
        in_specs=_row_specs(rows, tm) + _param_specs(params),
        out_specs=[pl.BlockSpec((tm, w), lambda i: (i, 0)) for (w, _) in outs],
        out_shape=[jax.ShapeDtypeStruct((M, w), dt) for (w, dt) in outs],
        compiler_params=_cparams(("parallel",)),
    )(*[r[0] for r in rows], *params)


def rowwise_bwd(name, fn, rows, params, douts, drow_dtypes, tm=None):
    M = rows[0][0].shape[0]
    tm = tm or _pick(M, 704)
    nr, npar, nd = len(rows), len(params), len(douts)
    want = [k for k, dt in enumerate(drow_dtypes) if dt is not None]

    def body(*refs):
        i = pl.program_id(0)
        row0 = i * tm
        vals = [r[...].astype(F32) for r in refs[:nr]] + [p[...] for p in refs[nr:nr + npar]]
        cots = tuple(d[...].astype(F32) for d in refs[nr + npar:nr + npar + nd])
        _, vjp = jax.vjp(functools.partial(fn, row0), *vals)
        grads = vjp(cots)
        o_refs = refs[nr + npar + nd:]
        for o_ref, k in zip(o_refs[:len(want)], want):
            o_ref[...] = grads[k].astype(o_ref.dtype)
        for o_ref, g in zip(o_refs[len(want):], grads[nr:]):
            @pl.when(i == 0)
            def _():
                o_ref[...] = g

            @pl.when(i > 0)
            def _():
                o_ref[...] += g

    res = pl.pallas_call(
        body, name=name, grid=(M // tm,),
        in_specs=_row_specs(rows, tm) + _param_specs(params) + _row_specs(douts, tm),
        out_specs=[pl.BlockSpec((tm, rows[k][1]), lambda i: (i, 0)) for k in want] + _param_specs(params),
        out_shape=[jax.ShapeDtypeStruct((M, rows[k][1]), drow_dtypes[k]) for k in want]
        + [jax.ShapeDtypeStruct(p.shape, F32) for p in params],
        compiler_params=_cparams(("arbitrary",)),
    )(*[r[0] for r in rows], *params, *[d[0] for d in douts])
    return res[:len(want)], res[len(want):]


def _fn_prenorm(row0, h, wpre):
    return (_rms(h, wpre),)


def _fn_resnorm(scale, row0, h, f, wpost, wpre):
    h2 = h + scale * _rms(f, wpost)
    return h2, _rms(h2, wpre)


def _fn_res_last(scale, row0, h, f, wpost):
    return (h + scale * _rms(f, wpost),)


def ffn_fwd(name, xn, wg, wu, wd, l, j):
    M = xn.shape[0]
    tm = _pick(M, 704)

    def body(x_ref, wg_ref, wu_ref, wd_ref, f_ref, a_ref, b_ref):
        s = pl.program_id(1)
        x = x_ref[...]
        a = _mm(x, wg_ref[...], None)
        b = _mm(x, wu_ref[...], None)
        hm = (_silu(a) * b).astype(BF16)
        c = _mm(hm, wd_ref[...], None)

        @pl.when(s == 0)
        def _():
            f_ref[...] = c

        @pl.when(s > 0)
        def _():
            f_ref[...] += c

        a_ref[...] = a.astype(BF16)
        b_ref[...] = b.astype(BF16)

    wspec = pl.BlockSpec((None, None, None, D, FSH), lambda i, s: (s, l, j, 0, 0))
    wdspec = pl.BlockSpec((None, None, None, FSH, D), lambda i, s: (s, l, j, 0, 0))
    abspec = pl.BlockSpec((None, tm, FSH), lambda i, s: (s, i, 0))
    return pl.pallas_call(
        body, name=name, grid=(M // tm, NSH),
        in_specs=[pl.BlockSpec((tm, D), lambda i, s: (i, 0)), wspec, wspec, wdspec],
        out_specs=[pl.BlockSpec((tm, D), lambda i, s: (i, 0)), abspec, abspec],
        out_shape=[jax.ShapeDtypeStruct((M, D), F32), jax.ShapeDtypeStruct((NSH, M, FSH), BF16),
                   jax.ShapeDtypeStruct((NSH, M, FSH), BF16)],
        compiler_params=_cparams(("parallel", "arbitrary")),
    )(xn, wg, wu, wd)


def ffn_bwd_x(name, df, a, b, wg, wu, wd, l, j):
    M = df.shape[0]
    tm = _pick(M, 704)

    def body(df_ref, a_ref, b_ref, wg_ref, wu_ref, wd_ref, dx_ref, da_ref, db_ref, hm_ref):
        s = pl.program_id(1)
        a_ = a_ref[...].astype(F32)
        b_ = b_ref[...].astype(F32)
        dh = _mm_nt(df_ref[...], wd_ref[...], None)
        sig = _sigmoid(a_)
        sil = a_ * sig
        da = (dh * b_ * (sig * (1.0 + a_ * (1.0 - sig)))).astype(BF16)
        db = (dh * sil).astype(BF16)
        c = _mm_nt(da, wg_ref[...], None) + _mm_nt(db, wu_ref[...], None)

        @pl.when(s == 0)
        def _():
            dx_ref[...] = c

        @pl.when(s > 0)
        def _():
            dx_ref[...] += c

        da_ref[...] = da
        db_ref[...] = db
        hm_ref[...] = (sil * b_).astype(BF16)

    wspec = pl.BlockSpec((None, None, None, D, FSH), lambda i, s: (s, l, j, 0, 0))
    wdspec = pl.BlockSpec((None, None, None, FSH, D), lambda i, s: (s, l, j, 0, 0))
    abspec = pl.BlockSpec((None, tm, FSH), lambda i, s: (s, i, 0))
    ab = jax.ShapeDtypeStruct((NSH, M, FSH), BF16)
    return pl.pallas_call(
        body, name=name, grid=(M // tm, NSH),
        in_specs=[pl.BlockSpec((tm, D), lambda i, s: (i, 0)), abspec, abspec, wspec, wspec, wdspec],
        out_specs=[pl.BlockSpec((tm, D), lambda i, s: (i, 0)), abspec, abspec, abspec],
        out_shape=[jax.ShapeDtypeStruct((M, D), F32), ab, ab, ab],
        compiler_params=_cparams(("parallel", "arbitrary")),
    )(df, a, b, wg, wu, wd)


def ffn_bwd_w(name, xn, df, da, db, hm):
    M = xn.shape[0]
    tm = _pick(M, 704)

    def body(x_ref, df_ref, da_ref, db_ref, hm_ref, dwg_ref, dwu_ref, dwd_ref):
        i = pl.program_id(1)
        x = x_ref[...]
        g = _mm_tn(x, da_ref[...], None)
        u = _mm_tn(x, db_ref[...], None)
        d = _mm_tn(hm_ref[...], df_ref[...], None)

        @pl.when(i == 0)
        def _():
            dwg_ref[...] = g
            dwu_ref[...] = u
            dwd_ref[...] = d

        @pl.when(i > 0)
        def _():
            dwg_ref[...] += g
            dwu_ref[...] += u
            dwd_ref[...] += d

    xspec = pl.BlockSpec((tm, D), lambda s, i: (i, 0))
    abspec = pl.BlockSpec((None, tm, FSH), lambda s, i: (s, i, 0))
    return pl.pallas_call(
        body, name=name, grid=(NSH, M // tm),
        in_specs=[xspec, xspec, abspec, abspec, abspec],
        out_specs=[pl.BlockSpec((None, D, FSH), lambda s, i: (s, 0, 0)), pl.BlockSpec((None, D, FSH), lambda s, i: (s, 0, 0)),
                   pl.BlockSpec((None, FSH, D), lambda s, i: (s, 0, 0))],
        out_shape=[jax.ShapeDtypeStruct((NSH, D, FSH), F32), jax.ShapeDtypeStruct((NSH, D, FSH), F32),
                   jax.ShapeDtypeStruct((NSH, FSH, D), F32)],
        compiler_params=_cparams(("parallel", "arbitrary")),
    )(xn, df, da, db, hm)


def _t5_bucket_np(rel):
    n = np.maximum(rel, 0)
    n_f = np.maximum(n, 1).astype(np.float32)
    large = 16 + (np.log(n_f / np.float32(16)) / np.float32(math.log(8.0)) * np.float32(16)).astype(np.int32)
    large = np.minimum(large, 31)
    return np.where(n < 16, n, large).astype(np.int32)


def _swa_bucket_ids():
    qi = np.arange(BLK)[:, None]
    kj = np.arange(BLK)[None, :]
    out = np.full((3, BLK, 3 * BLK), -1, np.int32)
    for v in range(3):
        pos_q = v * BLK + qi - ZROWS
        rel_m = pos_q - (kj - ZROWS)
        ok_m = (kj >= ZROWS) & (rel_m >= 0) & (pos_q >= 0)
        out[v, :, 0:BLK] = np.where(ok_m, _t5_bucket_np(rel_m), -1)
        pos_kp = (v - 1) * BLK + kj - ZROWS
        rel_p = BLK + qi - kj
        ok_p = (pos_kp >= N_META) & (rel_p >= 0) & (rel_p < BLK) & np.full_like(ok_m, v >= 1)
        out[v, :, BLK:2 * BLK] = np.where(ok_p, _t5_bucket_np(rel_p), -1)
        pos_kc = v * BLK + kj - ZROWS
        rel_c = qi - kj
        ok_c = (pos_kc >= N_META) & (rel_c >= 0) & (rel_c < BLK)
        out[v, :, 2 * BLK:] = np.where(ok_c, _t5_bucket_np(rel_c), -1)
    return out


def swa_bias_fwd(table, ids):
    def body(t_ref, id_ref, o_ref):
        for v in range(3):
            for h in range(8):
                o_ref[v, h] = jnp.where(id_ref[v] < 0, NEG, 0.0)

            def step(b, carry):
                hit = id_ref[v] == b
                for h in range(8):
                    o_ref[v, h] += jnp.where(hit, t_ref[b, h], 0.0)
                return carry

            lax.fori_loop(0, 32, step, 0)

    return pl.pallas_call(
        body, name="swa_bias_fwd",
        in_specs=[pl.BlockSpec(memory_space=pltpu.SMEM), pl.BlockSpec(memory_space=pltpu.VMEM)],
        out_specs=pl.BlockSpec(memory_space=pltpu.VMEM),
        out_shape=jax.ShapeDtypeStruct((3, 8, BLK, 3 * BLK), F32),
        compiler_params=pltpu.CompilerParams(vmem_limit_bytes=VMEM_LIMIT),
    )(table, ids)


def swa_bias_bwd(dbias, ids):
    def body(d_ref, id_ref, o_ref):
        r = lax.broadcasted_iota(jnp.int32, (32, LANE), 0)
        c = lax.broadcasted_iota(jnp.int32, (32, LANE), 1)

        def step(b, acc):
            for v in range(3):
                hit = id_ref[v] == b
                for h in range(8):
                    m = jnp.where(hit, d_ref[v, h], 0.0)
                    s = jnp.sum(jnp.sum(m, axis=1, keepdims=True), axis=0, keepdims=True)
                    acc = acc + jnp.where((r == b) & (c == h), s, 0.0)
            return acc

        o_ref[...] = lax.fori_loop(0, 32, step, jnp.zeros((32, LANE), F32))

    return pl.pallas_call(
        body, name="swa_bias_bwd",
        in_specs=[pl.BlockSpec(memory_space=pltpu.VMEM), pl.BlockSpec(memory_space=pltpu.VMEM)],
        out_specs=pl.BlockSpec(memory_space=pltpu.VMEM),
        out_shape=jax.ShapeDtypeStruct((32, LANE), F32),
        compiler_params=pltpu.CompilerParams(vmem_limit_bytes=VMEM_LIMIT),
    )(dbias, ids)


def _swa_block(q, k3, v3, bias, sinks):
    outs = []
    for hk in range(2):
        kh = k3[:, hk * 128:(hk + 1) * 128]
        vh = v3[:, hk * 128:(hk + 1) * 128]
        for g in range(4):
            h = hk * 4 + g
            s = _mm_nt(q[:, h * 128:(h + 1) * 128], kh) * 0.125 + bias[h]
            sink = _lane_pick(sinks, h)
            m = lax.stop_gradient(jnp.maximum(jnp.max(s, axis=-1, keepdims=True), sink))
            e = jnp.exp(s - m)
            den = jnp.sum(e, axis=-1, keepdims=True) + jnp.exp(sink - m)
            outs.append(_mm(e / den, vh))
    return jnp.concatenate(outs, axis=1)


def _swa_in_specs():
    qs = pl.BlockSpec((BLK, 1024), lambda n: (n, E_QA // 1024))
    ks = [pl.BlockSpec((BLK, 256), lambda n: (0, E_KA // 256)),
          pl.BlockSpec((BLK, 256), lambda n: (jnp.maximum(n - 1, 0), E_KA // 256)),
          pl.BlockSpec((BLK, 256), lambda n: (n, E_KA // 256))]
    vs = [pl.BlockSpec((BLK, 256), lambda n: (0, E_VA // 256)),
          pl.BlockSpec((BLK, 256), lambda n: (jnp.maximum(n - 1, 0), E_VA // 256)),
          pl.BlockSpec((BLK, 256), lambda n: (n, E_VA // 256))]
    bs = pl.BlockSpec((None, 8, BLK, 3 * BLK), lambda n: (jnp.minimum(n, 2), 0, 0, 0))
    ss = pl.BlockSpec((1, LANE), lambda n: (0, 0))
    return [qs] + ks + vs + [bs, ss]


def swa_fwd(proj, bias, sinks):
    M = proj.shape[0]

    def body(q_ref, k0, k1, k2, v0, v1, v2, b_ref, s_ref, o_ref):
        k3 = jnp.concatenate([k0[...], k1[...], k2[...]], axis=0)
        v3 = jnp.concatenate([v0[...], v1[...], v2[...]], axis=0)
        o_ref[...] = _swa_block(q_ref[...], k3, v3, b_ref[...], s_ref[...]).astype(o_ref.dtype)

    return pl.pallas_call(
        body, name="swa_fwd", grid=(M // BLK,),
        in_specs=_swa_in_specs(),
        out_specs=pl.BlockSpec((BLK, 1024), lambda n: (n, 0)),
        out_shape=jax.ShapeDtypeStruct((M, 1024), BF16),
        compiler_params=_cparams(("parallel",)),
    )(proj, proj, proj, proj, proj, proj, proj, bias, sinks)


def swa_bwd(proj, bias, sinks, do):
    M = proj.shape[0]

    def body(q_ref, k0, k1, k2, v0, v1, v2, b_ref, s_ref, do_ref, dq_ref, dk_ref, dv_ref, db_ref, ds_ref):
        n = pl.program_id(0)

        @pl.when(n == 0)
        def _():
            dk_ref[...] = jnp.zeros_like(dk_ref)
            dv_ref[...] = jnp.zeros_like(dv_ref)
            ds_ref[...] = jnp.zeros_like(ds_ref)

        k3 = jnp.concatenate([k0[...], k1[...], k2[...]], axis=0)
        v3 = jnp.concatenate([v0[...], v1[...], v2[...]], axis=0)
        _, vjp = jax.vjp(_swa_block, q_ref[...], k3, v3, b_ref[...], s_ref[...])
        dq, dk3, dv3, dbias, dsink = vjp(do_ref[...].astype(F32))
        dq_ref[...] = dq
        prev = pl.multiple_of(jnp.maximum(n - 1, 0) * BLK, BLK)
        cur = pl.multiple_of(n * BLK, BLK)
        dk_ref[pl.ds(0, BLK), :] += dk3[0:BLK]
        dv_ref[pl.ds(0, BLK), :] += dv3[0:BLK]
        dk_ref[pl.ds(prev, BLK), :] += dk3[BLK:2 * BLK]
        dv_ref[pl.ds(prev, BLK), :] += dv3[BLK:2 * BLK]
        dk_ref[pl.ds(cur, BLK), :] += dk3[2 * BLK:]
        dv_ref[pl.ds(cur, BLK), :] += dv3[2 * BLK:]
        ds_ref[...] += dsink

        @pl.when(n <= 2)
        def _():
            db_ref[...] = dbias

        @pl.when(n > 2)
        def _():
            db_ref[...] += dbias

    return pl.pallas_call(
        body, name="swa_bwd", grid=(M // BLK,),
        in_specs=_swa_in_specs() + [pl.BlockSpec((BLK, 1024), lambda n: (n, 0))],
        out_specs=[pl.BlockSpec((BLK, 1024), lambda n: (n, 0)),
                   pl.BlockSpec((M, 256), lambda n: (0, 0)), pl.BlockSpec((M, 256), lambda n: (0, 0)),
                   pl.BlockSpec((None, 8, BLK, 3 * BLK), lambda n: (jnp.minimum(n, 2), 0, 0, 0)),
                   pl.BlockSpec((1, LANE), lambda n: (0, 0))],
        out_shape=[jax.ShapeDtypeStruct((M, 1024), F32), jax.ShapeDtypeStruct((M, 256), F32),
                   jax.ShapeDtypeStruct((M, 256), F32), jax.ShapeDtypeStruct((3, 8, BLK, 3 * BLK), F32),
                   jax.ShapeDtypeStruct((1, LANE), F32)],
        compiler_params=_cparams(("arbitrary",)),
    )(proj, proj, proj, proj, proj, proj, proj, bias, sinks, do)


def _shift_rows_impl(x, k):
    n = x.shape[0]
    rolled = pltpu.roll(x, k, 0)
    return jnp.where(_row_ids(0, n) >= k, rolled, 0.0)


def _unshift_rows_impl(g, k):
    n = g.shape[0]
    rolled = pltpu.roll(g, n - k, 0)
    return jnp.where(_row_ids(0, n) < n - k, rolled, 0.0)


@functools.partial(jax.custom_vjp, nondiff_argnums=(1,))
def _shift_rows(x, k):
    return _shift_rows_impl(x, k)


def _shift_rows_f(x, k):
    return _shift_rows_impl(x, k), None


def _shift_rows_b(k, _, g):
    return (_unshift_rows_impl(g, k),)


_shift_rows.defvjp(_shift_rows_f, _shift_rows_b)


def _conv_silu(x, w):
    rid = lax.broadcasted_iota(jnp.int32, w.shape, 0)
    y = x * jnp.sum(jnp.where(rid == 3, w, 0.0), axis=0, keepdims=True)
    for k in range(1, 4):
        y = y + _shift_rows(x, k) * jnp.sum(jnp.where(rid == 3 - k, w, 0.0), axis=0, keepdims=True)
    y = jnp.where(_row_ids(0, x.shape[0]) >= ZROWS, y, 0.0)
    return _silu(y)


def conv_fwd(proj, conv_w):
    M = proj.shape[0]
    nb = conv_w.shape[1] // LANE

    def body(x_ref, w_ref, o_ref):
        o_ref[...] = _conv_silu(x_ref[...], w_ref[...])

    return pl.pallas_call(
        body, name="conv_fwd", grid=(nb,),
        in_specs=[pl.BlockSpec((M, LANE), lambda c: (0, E_QB // LANE + c)), pl.BlockSpec((4, LANE), lambda c: (0, c))],
        out_specs=pl.BlockSpec((M, LANE), lambda c: (0, c)),
        out_shape=jax.ShapeDtypeStruct((M, conv_w.shape[1]), F32),
        compiler_params=_cparams(("parallel",)),
    )(proj, conv_w)


def conv_bwd(proj, conv_w, dy):
    M = proj.shape[0]
    nb = conv_w.shape[1] // LANE

    def body(x_ref, w_ref, dy_ref, dx_ref, dw_ref):
        _, vjp = jax.vjp(_conv_silu, x_ref[...], w_ref[...])
        dx, dw = vjp(dy_ref[...])
        dx_ref[...] = dx
        dw_ref[...] = dw

    return pl.pallas_call(
        body, name="conv_bwd", grid=(nb,),
        in_specs=[pl.BlockSpec((M, LANE), lambda c: (0, E_QB // LANE + c)), pl.BlockSpec((4, LANE), lambda c: (0, c)),
                  pl.BlockSpec((M, LANE), lambda c: (0, c))],
        out_specs=[pl.BlockSpec((M, LANE), lambda c: (0, c)), pl.BlockSpec((4, LANE), lambda c: (0, c))],
        out_shape=[jax.ShapeDtypeStruct((M, conv_w.shape[1]), F32), jax.ShapeDtypeStruct(conv_w.shape, F32)],
        compiler_params=_cparams(("parallel",)),
    )(proj, conv_w, dy)


def _fn_dn_prep(row0, yq, yk, ba, dnp):
    tm = yq.shape[0]
    real = _row_ids(row0, tm) >= ZROWS
    qs, ks, gs, bs = [], [], [], []
    for h in range(4):
        q = yq[:, h * 128:(h + 1) * 128]
        k = yk[:, h * 128:(h + 1) * 128]
        qs.append(q * lax.rsqrt(jnp.sum(q * q, axis=-1, keepdims=True) + 1e-6) * (128.0 ** -0.5))
        ks.append(k * lax.rsqrt(jnp.sum(k * k, axis=-1, keepdims=True) + 1e-6))
        beta = _sigmoid(_lane_pick(ba, h))
        g = -jnp.exp(_lane_pick(dnp, h)) * _softplus(_lane_pick(ba, 4 + h) + _lane_pick(dnp, 4 + h))
        g = jnp.where(real, g, 0.0)
        gs.append(jnp.broadcast_to(g, (tm, 128)))
        bs.append(jnp.broadcast_to(beta, (tm, 128)))
    cat = lambda xs: jnp.concatenate(xs, axis=1)
    return cat(qs), cat(ks), cat(gs), cat(bs)


def _dn_chunk(q, k, v, gb, bb, S):
    r = lax.broadcasted_iota(jnp.int32, (CH, CH), 0)
    c = lax.broadcasted_iota(jnp.int32, (CH, CH), 1)
    tri_incl = r >= c
    gcb = _mm(tri_incl.astype(F32), gb)
    g1 = gcb[:, :CH]
    diff = g1 - g1.T
    gamma = jnp.where(tri_incl, jnp.exp(jnp.where(tri_incl, diff, 0.0)), 0.0)
    kb = k * bb
    vb = v * bb
    a = jnp.where(r > c, _mm_nt(kb, k) * gamma, 0.0)
    p = -a
    t = (r == c).astype(F32) + p
    for _ in range(5):
        p = _mm(p, p)
        t = t + _mm(t, p)
    eg = jnp.exp(gcb)
    u = _mm(t, vb)
    w = _mm(t, kb * eg)
    attn = _mm_nt(q, k) * gamma
    gtot = _mm(jnp.ones((CH, CH), F32), gb)
    k_dec = k * jnp.exp(gtot - gcb)
    g_last = jnp.exp(_mm(jnp.ones((128, CH), F32), gb))
    v_new = u - _mm(w, S)
    o = _mm(q * eg, S) + _mm(attn, v_new)
    return o, S * g_last + _mm_tn(k_dec, v_new)


def _gla_chunk(q, k, v, glog, S):
    r = lax.broadcasted_iota(jnp.int32, (CH, CH), 0)
    c = lax.broadcasted_iota(jnp.int32, (CH, CH), 1)
    tri = r >= c
    bcum = _mm(tri.astype(F32), glog)
    q_dec = q * (128.0 ** -0.5) * jnp.exp(bcum)
    attn = jnp.where(tri, _mm_nt(q_dec, k * jnp.exp(-bcum)), 0.0)
    o = _mm(attn, v) + _mm(q_dec, S)
    btot = _mm(jnp.ones((CH, CH), F32), glog)
    k_dec = k * jnp.exp(btot - bcum)
    decay = jnp.exp(_mm_tn(glog, jnp.ones((CH, v.shape[1]), F32)))
    return o, S * decay + _mm_tn(k_dec, v)


def chunk_fwd(name, chunk_fn, ins, dv):
    M = ins[0][0].shape[0]
    N = M // CH
    ni = len(ins)

    def body(*refs):
        o_ref, sall_ref, s_ref = refs[ni:]

        @pl.when(pl.program_id(1) == 0)
        def _():
            s_ref[...] = jnp.zeros_like(s_ref)

        S = s_ref[...]
        sall_ref[...] = S
        o, s_new = chunk_fn(*[r[...] for r in refs[:ni]], S)
        o_ref[...] = o
        s_ref[...] = s_new

    specs = [pl.BlockSpec((CH, w), functools.partial(lambda h, n, cb: (n, cb + h), cb=cb)) for (_, w, cb) in ins]
    return pl.pallas_call(
        body, name=name, grid=(4, N),
        in_specs=specs,
        out_specs=[pl.BlockSpec((CH, dv), lambda h, n: (n, h)),
                   pl.BlockSpec((None, None, 128, dv), lambda h, n: (h, n, 0, 0))],
        out_shape=[jax.ShapeDtypeStruct((M, 4 * dv), F32), jax.ShapeDtypeStruct((4, N, 128, dv), F32)],
        scratch_shapes=[pltpu.VMEM((128, dv), F32)],
        compiler_params=_cparams(("parallel", "arbitrary")),
    )(*[a for (a, _, _) in ins])


def chunk_bwd(name, chunk_fn, ins, dv, s_all, do):
    M = ins[0][0].shape[0]
    N = M // CH
    ni = len(ins)

    def body(*refs):
        sall_ref, do_ref = refs[ni:ni + 2]
        d_refs = refs[ni + 2:2 * ni + 2]
        ds_ref = refs[2 * ni + 2]

        @pl.when(pl.program_id(1) == 0)
        def _():
            ds_ref[...] = jnp.zeros_like(ds_ref)

        _, vjp = jax.vjp(chunk_fn, *[r[...] for r in refs[:ni]], sall_ref[...])
        grads = vjp((do_ref[...], ds_ref[...]))
        for d_ref, g in zip(d_refs, grads[:ni]):
            d_ref[...] = g
        ds_ref[...] = grads[ni]

    rev = lambda n: N - 1 - n
    specs = [pl.BlockSpec((CH, w), functools.partial(lambda h, n, cb: (rev(n), cb + h), cb=cb)) for (_, w, cb) in ins]
    return pl.pallas_call(
        body, name=name, grid=(4, N),
        in_specs=specs + [pl.BlockSpec((None, None, 128, dv), lambda h, n: (h, rev(n), 0, 0)),
                          pl.BlockSpec((CH, dv), lambda h, n: (rev(n), h))],
        out_specs=[pl.BlockSpec((CH, w), lambda h, n: (rev(n), h)) for (_, w, _) in ins],
        out_shape=[jax.ShapeDtypeStruct((M, 4 * w), F32) for (_, w, _) in ins],
        scratch_shapes=[pltpu.VMEM((128, dv), F32)],
        compiler_params=_cparams(("parallel", "arbitrary")),
    )(*[a for (a, _, _) in ins], s_all, do)


def _fn_gate_out(hd, row0, o, z, w):
    outs = []
    for h in range(4):
        outs.append(_rms(o[:, h * hd:(h + 1) * hd], w) * _silu(z[:, h * hd:(h + 1) * hd]))
    return (jnp.concatenate(outs, axis=1),)


def _fn_gla_prep(row0, gk, wgu, bg):
    x = _mm(gk, wgu) + bg
    ls = jnp.minimum(x, 0.0) - jnp.log(1.0 + jnp.exp(-jnp.abs(x)))
    return (jnp.where(_row_ids(row0, gk.shape[0]) >= ZROWS, ls / 16.0, 0.0),)


def loss_call(y, tgt):
    M = y.shape[0]
    tm = _pick(M, 512)

    def body(y_ref, t_ref, l_ref, dy_ref):
        i = pl.program_id(0)
        e = y_ref[...] - t_ref[...]
        dy_ref[...] = e * (1.0 / D)
        part = 0.5 * jnp.sum(jnp.sum(e * e, axis=1, keepdims=True) * (1.0 / D), axis=0, keepdims=True)
        part = jnp.broadcast_to(part, (8, LANE))

        @pl.when(i == 0)
        def _():
            l_ref[...] = part

        @pl.when(i > 0)
        def _():
            l_ref[...] += part

    return pl.pallas_call(
        body, name="loss", grid=(M // tm,),
        in_specs=[pl.BlockSpec((tm, D), lambda i: (i, 0))] * 2,
        out_specs=[pl.BlockSpec((8, LANE), lambda i: (0, 0)), pl.BlockSpec((tm, D), lambda i: (i, 0))],
        out_shape=[jax.ShapeDtypeStruct((8, LANE), F32), jax.ShapeDtypeStruct((M, D), F32)],
        compiler_params=_cparams(("arbitrary",)),
    )(y, tgt)


def _bf(x):
    return x.astype(BF16)


def core_step(x, tgt, W):
    S = x.shape[0]
    M = S + PADR
    ids = jnp.asarray(_swa_bucket_ids())
    h0 = jnp.concatenate([jnp.zeros((ZROWS, D), F32), W["meta"], x], axis=0)
    nw = W["norm"]
    nrow = lambda l, k: nw[l, k][None, :]
    wg, wu, wd = W["ffn_g"], W["ffn_u"], W["ffn_d"]
    e_in, e_out = W["e_in"], W["e_out"]
    o_in, o_out = W["o_in"], W["o_out"]
    sinks = jnp.pad(W["sinks"], ((0, 0), (0, LANE - 8)))
    dnp = jnp.pad(jnp.concatenate([W["a_log"], W["dt_bias"]], axis=1), ((0, 0), (0, LANE - 8)))
    wgu = jnp.pad(W["gate_up"], ((0, LANE - 16), (0, 0)))
    bg = W["b_gate"]
    full = lambda a: (a, a.shape[1], 0)

    saved = []
    h = h0
    (hn,) = rowwise_fwd("prenorm_0", _fn_prenorm, [full(h)], [nrow(0, 0)], [(D, BF16)])
    bias = swa_bias_fwd(W["rel"], ids)
    for l in range(2):
        st = {"h_a": h, "hn_a": hn}
        f1, a1, b1 = ffn_fwd(f"ffn_fwd_{l}0", hn, wg, wu, wd, l, 0)
        h, hn = rowwise_fwd(f"resnorm_{l}1", functools.partial(_fn_resnorm, 0.5), [full(h), full(f1)],
                            [nrow(l, 1), nrow(l, 2)], [(D, F32), (D, BF16)])
        st.update(f1=f1, a1=a1, b1=b1, h_b=h, hn_b=hn)
        if l == 0:
            proj = mm_nn(hn, e_in, "e_proj")
            o_a = swa_fwd(proj, bias, sinks)
            y = conv_fwd(proj, W["conv"])
            qn, kn, gb, bb = rowwise_fwd(
                "dn_prep", _fn_dn_prep, [(y, 512, 0), (y, 512, 1), (proj, LANE, E_BA // LANE)], [dnp], [(512, F32)] * 4)
            ins = [(qn, 128, 0), (kn, 128, 0), (y, 128, 8), (gb, 128, 0), (bb, 128, 0)]
            o_dn, s_all = chunk_fwd("dn_fwd", _dn_chunk, ins, 128)
            (o_b,) = rowwise_fwd("dn_out", functools.partial(_fn_gate_out, 128),
                                 [full(o_dn), (proj, 512, E_ZB // 512)], [W["dn_norm"]], [(512, BF16)])
            omix = jnp.concatenate([o_a, o_b], axis=1)
            mix = mm_nn(omix, e_out, "e_mix")
            st.update(proj=proj, y=y, qn=qn, kn=kn, gb=gb, bb=bb, o_dn=o_dn, s_all=s_all, omix=omix)
        else:
            proj = mm_nn(hn, o_in, "o_proj")
            (glog,) = rowwise_fwd("gla_prep", _fn_gla_prep, [(proj, LANE, O_GK // LANE)], [wgu, bg], [(512, F32)])
            ins = [(proj, 128, O_Q // 128), (proj, 128, O_K // 128), (proj, 256, O_V // 256), (glog, 128, 0)]
            o_g, s_all = chunk_fwd("gla_fwd", _gla_chunk, ins, 256)
            (omix,) = rowwise_fwd("gla_out", functools.partial(_fn_gate_out, 256),
                                  [full(o_g), (proj, 1024, O_G // 1024)], [W["gla_norm"]], [(1024, BF16)])
            mix = mm_nn(omix, o_out, "o_mix")
            st.update(proj=proj, glog=glog, o_g=o_g, s_all=s_all, omix=omix)
        h, hn = rowwise_fwd(f"resnorm_{l}3", functools.partial(_fn_resnorm, 1.0), [full(h), full(mix)],
                            [nrow(l, 3), nrow(l, 4)], [(D, F32), (D, BF16)])
        st.update(mix=mix, h_c=h, hn_c=hn)
        f2, a2, b2 = ffn_fwd(f"ffn_fwd_{l}1", hn, wg, wu, wd, l, 1)
        st.update(f2=f2, a2=a2, b2=b2)
        if l == 0:
            h, hn = rowwise_fwd("resnorm_05", functools.partial(_fn_resnorm, 0.5), [full(h), full(f2)],
                                [nrow(0, 5), nrow(1, 0)], [(D, F32), (D, BF16)])
        else:
            (h,) = rowwise_fwd("res_last", functools.partial(_fn_res_last, 0.5), [full(h), full(f2)],
                               [nrow(1, 5)], [(D, F32)])
        saved.append(st)

    loss_blk, dy = loss_call(h[PADR:], tgt)
    dh = jnp.concatenate([jnp.zeros((PADR, D), F32), dy], axis=0)

    G = {}
    dnorm = [[None] * 6 for _ in range(2)]
    dWg = [[None, None], [None, None]]
    dWu = [[None, None], [None, None]]
    dWd = [[None, None], [None, None]]
    dhn = None
    for l in (1, 0):
        st = saved[l]
        if l == 1:
            (dh_, df), (dw5,) = rowwise_bwd(
                "res_last_b", functools.partial(_fn_res_last, 0.5), [full(st["h_c"]), full(st["f2"])], [nrow(1, 5)],
                [full(dh)], [F32, BF16])
            dnorm[1][5] = dw5
        else:
            (dh_, df), (dw5, dw0n) = rowwise_bwd(
                "resnorm_05_b", functools.partial(_fn_resnorm, 0.5), [full(st["h_c"]), full(st["f2"])],
                [nrow(0, 5), nrow(1, 0)], [full(dh), full(dhn)], [F32, BF16])
            dnorm[0][5] = dw5
            dnorm[1][0] = dw0n
        dh = dh_
        dxn, da, db, hm = ffn_bwd_x(f"ffn_bx_{l}1", df, st["a2"], st["b2"], wg, wu, wd, l, 1)
        dWg[l][1], dWu[l][1], dWd[l][1] = ffn_bwd_w(f"ffn_bw_{l}1", st["hn_c"], df, da, db, hm)
        (dh_, dmix), (dw3, dw4) = rowwise_bwd(
            f"resnorm_{l}3_b", functools.partial(_fn_resnorm, 1.0), [full(st["h_b"]), full(st["mix"])],
            [nrow(l, 3), nrow(l, 4)], [full(dh), full(dxn)], [F32, BF16])
        dnorm[l][3], dnorm[l][4] = dw3, dw4
        dh = dh_
        proj = st["proj"]
        if l == 1:
            G["o_out"] = mm_tn(st["omix"], dmix, "o_out_dw")
            domix = mm_nt(dmix, o_out, "o_mix_dx")
            (do_g, dgate), (dgn,) = rowwise_bwd(
                "gla_out_b", functools.partial(_fn_gate_out, 256), [full(st["o_g"]), (proj, 1024, O_G // 1024)],
                [W["gla_norm"]], [full(domix)], [F32, F32])
            G["gla_norm"] = dgn
            ins = [(proj, 128, O_Q // 128), (proj, 128, O_K // 128), (proj, 256, O_V // 256), (st["glog"], 128, 0)]
            dq, dk, dv, dglog = chunk_bwd("gla_bwd", _gla_chunk, ins, 256, st["s_all"], do_g)
            (dgk,), (dwgu, dbg) = rowwise_bwd("gla_prep_b", _fn_gla_prep, [(proj, LANE, O_GK // LANE)], [wgu, bg],
                                              [full(dglog)], [F32])
            G["gate_up"] = dwgu[:16]
            G["b_gate"] = dbg
            dproj = _bf(jnp.concatenate([dq, dk, dv, dgate, dgk, jnp.zeros((M, O_END - O_GK - LANE), F32)], axis=1))
            G["o_in"] = mm_tn(st["hn_b"], dproj, "o_in_dw")
            dhn_b = mm_nt(dproj, o_in, "o_proj_dx")
        else:
            G["e_out"] = mm_tn(st["omix"], dmix, "e_out_dw")
            domix = mm_nt(dmix, e_out, "e_mix_dx")
            (do_dn, dz), (ddn,) = rowwise_bwd(
                "dn_out_b", functools.partial(_fn_gate_out, 128), [full(st["o_dn"]), (proj, 512, E_ZB // 512)],
                [W["dn_norm"]], [(domix, 512, 2)], [F32, F32])
            G["dn_norm"] = ddn
            ins = [(st["qn"], 128, 0), (st["kn"], 128, 0), (st["y"], 128, 8), (st["gb"], 128, 0), (st["bb"], 128, 0)]
            dqn, dkn, dvv, dgb, dbb = chunk_bwd("dn_bwd", _dn_chunk, ins, 128, st["s_all"], do_dn)
            (dyq, dyk, dba), (ddnp,) = rowwise_bwd(
                "dn_prep_b", _fn_dn_prep, [(st["y"], 512, 0), (st["y"], 512, 1), (proj, LANE, E_BA // LANE)], [dnp],
                [full(dqn), full(dkn), full(dgb), full(dbb)], [F32, F32, F32])
            G["a_log"] = ddnp[:, 0:4]
            G["dt_bias"] = ddnp[:, 4:8]
            dyc = jnp.concatenate([dyq, dyk, dvv], axis=1)
            dxc, dconv = conv_bwd(proj, W["conv"], dyc)
            G["conv"] = dconv
            dq_a, dk_a, dv_a, dbias, dsink = swa_bwd(proj, bias, sinks, domix)
            G["sinks"] = dsink[:, :8]
            G["rel"] = swa_bias_bwd(dbias, ids)[:, :8]
            dproj = _bf(jnp.concatenate([dq_a, dk_a, dv_a, dxc, dz, dba, jnp.zeros((M, E_END - E_BA - LANE), F32)], axis=1))
            G["e_in"] = mm_tn(st["hn_b"], dproj, "e_in_dw")
            dhn_b = mm_nt(dproj, e_in, "e_proj_dx")
        (dh_, df), (dw1, dw2) = rowwise_bwd(
            f"resnorm_{l}1_b", functools.partial(_fn_resnorm, 0.5), [full(st["h_a"]), full(st["f1"])],
            [nrow(l, 1), nrow(l, 2)], [full(dh), full(dhn_b)], [F32, BF16])
        dnorm[l][1], dnorm[l][2] = dw1, dw2
        dh = dh_
        dxn, da, db, hm = ffn_bwd_x(f"ffn_bx_{l}0", df, st["a1"], st["b1"], wg, wu, wd, l, 0)
        dWg[l][0], dWu[l][0], dWd[l][0] = ffn_bwd_w(f"ffn_bw_{l}0", st["hn_a"], df, da, db, hm)
        dhn = dxn
    (dh0p,), (dw00,) = rowwise_bwd("prenorm_0_b", _fn_prenorm, [full(saved[0]["h_a"])], [nrow(0, 0)], [full(dhn)], [F32])
    dnorm[0][0] = dw00
    dh = dh + dh0p
    G["meta"] = dh[ZROWS:PADR]
    G["norm"] = jnp.stack([jnp.concatenate(r, axis=0) for r in dnorm], axis=0)
    stack4 = lambda t: jnp.stack([t[0][0], t[0][1], t[1][0], t[1][1]], axis=1)
    G["ffn_g"], G["ffn_u"], G["ffn_d"] = stack4(dWg), stack4(dWu), stack4(dWd)
    return loss_blk, dh[PADR:], G


NAMES = [("meta", "meta_tokens"), ("norm", "norm_w"), ("ffn_g", "ffn_w_gate"), ("ffn_u", "ffn_w_up"),
         ("ffn_d", "ffn_w_down"), ("rel", "rel_bias_table"), ("e_in", "even_w_in"), ("conv", "even_conv_w"),
         ("sinks", "swa_sinks"), ("a_log", "dn_a_log"), ("dt_bias", "dn_dt_bias"), ("dn_norm", "dn_norm_w"),
         ("e_out", "even_w_out"), ("o_in", "odd_w_in"), ("gate_up", "gla_w_gate_up"), ("b_gate", "gla_b_gate"),
         ("gla_norm", "gla_norm_w"), ("o_out", "odd_w_out")]
BIG = [("ffn_g", (2, 2, D, FSH)), ("ffn_u", (2, 2, D, FSH)), ("ffn_d", (2, 2, FSH, D)), ("e_in", (1, D, 706)),
       ("e_out", (1, 256, D)), ("o_in", (1, D, 772)), ("o_out", (1, 256, D))]
BIG_ROWS = 10752
HALF_ROWS = BIG_ROWS // 2
SMALL = [("meta", (16, 256)), ("norm", (2, 6, 256)), ("conv", (1, 4, 384)), ("gate_up", (1, 16, 128)),
         ("b_gate", (1, 128)), ("gla_norm", (1, 64))]
REPL = [("rel", (32, 8)), ("sinks", (1, 8)), ("a_log", (1, 4)), ("dt_bias", (1, 4)), ("dn_norm", (1, 128))]
SMALL_REP = 88 * LANE
SMALL_ROWS = 96


def pack_big(t):
    parts = [t[n].reshape(-1, D) for n, _ in BIG]
    used = sum(p.shape[0] for p in parts)
    return jnp.concatenate(parts + [jnp.zeros((BIG_ROWS - used, D), parts[0].dtype)], axis=0)


def unpack_big(p):
    out, r = {}, 0
    for n, shp in BIG:
        rows = int(np.prod(shp)) // D
        out[n] = p[r:r + rows].reshape(shp)
        r += rows
    return out


def pack_small(t):
    a = jnp.concatenate([t[n].reshape(-1) for n, _ in SMALL])
    b = jnp.concatenate([t[n].reshape(-1) for n, _ in REPL])
    flat = jnp.concatenate([a, jnp.zeros((SMALL_REP - a.shape[0],), F32), b,
                            jnp.zeros((SMALL_ROWS * LANE - SMALL_REP - b.shape[0],), F32)])
    return flat.reshape(SMALL_ROWS, LANE)


def unpack_small(p):
    flat = p.reshape(-1)
    out, r = {}, 0
    for n, shp in SMALL:
        k = int(np.prod(shp))
        out[n] = flat[r:r + k].reshape(shp)
        r += k
    r = SMALL_REP
    for n, shp in REPL:
        k = int(np.prod(shp))
        out[n] = flat[r:r + k].reshape(shp)
        r += k
    return out


def full_from_gathered(gb, gs):
    sh = [unpack_big(gb[s]) for s in range(NSH)]
    sm = [unpack_small(gs[s]) for s in range(NSH)]
    full = {}
    for n in ("ffn_g", "ffn_u", "ffn_d"):
        full[n] = jnp.stack([sh[s][n] for s in range(NSH)], axis=0)
    full["e_in"] = jnp.concatenate([sh[s]["e_in"][0] for s in range(NSH)], axis=1)
    full["o_in"] = jnp.concatenate([sh[s]["o_in"][0] for s in range(NSH)], axis=1)
    full["e_out"] = jnp.concatenate([sh[s]["e_out"][0] for s in range(NSH)], axis=0)
    full["o_out"] = jnp.concatenate([sh[s]["o_out"][0] for s in range(NSH)], axis=0)
    full["meta"] = jnp.concatenate([sm[s]["meta"] for s in range(NSH)], axis=1)
    full["norm"] = jnp.concatenate([sm[s]["norm"] for s in range(NSH)], axis=2)
    full["conv"] = jnp.concatenate([sm[s]["conv"][0] for s in range(NSH)], axis=1)
    full["gate_up"] = jnp.concatenate([sm[s]["gate_up"][0] for s in range(NSH)], axis=1)
    full["b_gate"] = jnp.concatenate([sm[s]["b_gate"] for s in range(NSH)], axis=1)
    full["gla_norm"] = jnp.concatenate([sm[s]["gla_norm"] for s in range(NSH)], axis=1)
    return full


def make_W(full, rep):
    W = dict(full)
    W.update(rep)
    W["e_in"] = _take_pad(full["e_in"], _even_in_map(), 1).astype(BF16)
    W["e_out"] = _take_pad(full["e_out"], _even_out_map(), 0).astype(BF16)
    W["o_in"] = _take_pad(full["o_in"], _odd_in_map(), 1).astype(BF16)
    W["o_out"] = full["o_out"].astype(BF16)
    for n in ("ffn_g", "ffn_u", "ffn_d"):
        W[n] = full[n].astype(BF16)
    for n in ("meta", "norm", "conv", "gate_up", "b_gate", "gla_norm"):
        W[n] = full[n].astype(F32)
    return W


def pack_grads(G):
    ei = jnp.take(G["e_in"], jnp.asarray(_inverse(_even_in_map(), 2824)), axis=1)
    oi = jnp.take(G["o_in"], jnp.asarray(_inverse(_odd_in_map(), 3088)), axis=1)
    eo = jnp.take(G["e_out"], jnp.asarray(_inverse(_even_out_map(), 1024)), axis=0)
    col_sh = lambda w: jnp.moveaxis(w.reshape(w.shape[0], NSH, w.shape[1] // NSH), 1, 0)
    parts = [G["ffn_g"].reshape(NSH, -1, D), G["ffn_u"].reshape(NSH, -1, D), G["ffn_d"].reshape(NSH, -1, D),
             col_sh(ei).reshape(NSH, -1, D), eo.reshape(NSH, 256, D), col_sh(oi).reshape(NSH, -1, D),
             G["o_out"].reshape(NSH, 256, D)]
    used = sum(p.shape[1] for p in parts)
    big = jnp.concatenate(parts + [jnp.zeros((NSH, BIG_ROWS - used, D), F32)], axis=1)
    norm_sh = jnp.moveaxis(G["norm"].reshape(2, 6, NSH, 256), 2, 0)
    a = jnp.concatenate([col_sh(G["meta"]).reshape(NSH, -1), norm_sh.reshape(NSH, -1), col_sh(G["conv"]).reshape(NSH, -1),
                         col_sh(G["gate_up"]).reshape(NSH, -1), G["b_gate"].reshape(NSH, -1),
                         G["gla_norm"].reshape(NSH, -1)], axis=1)
    b = jnp.concatenate([G[n].reshape(-1) for n, _ in REPL])
    b = jnp.broadcast_to(b[None], (NSH, b.shape[0]))
    small = jnp.concatenate([a, jnp.zeros((NSH, SMALL_REP - a.shape[1]), F32), b,
                             jnp.zeros((NSH, SMALL_ROWS * LANE - SMALL_REP - b.shape[1]), F32)], axis=1)
    return big, small.reshape(NSH, SMALL_ROWS, LANE)


MESH = pl.DeviceIdType.MESH
ANY = pl.BlockSpec(memory_space=pl.ANY)
VMEM = pl.BlockSpec(memory_space=pltpu.VMEM)


def _place():
    return lax.axis_index("x"), lax.axis_index("y"), lax.axis_index("c")


def _other_chips(x, y):
    return [(1 - x, y), (x, 1 - y), (1 - x, 1 - y)]


def ag_big(pack):
    R = pack.shape[0]
    Rh = R // 2

    def body(x_ref, out_ref, send_sems, recv_sems, local_sem):
        x, y, c = _place()
        s = 2 * x + y
        chips = _other_chips(x, y)
        half = lambda cc: pl.ds(cc * Rh, Rh)

        def copy(k, src, dst, to):
            return pltpu.make_async_remote_copy(src_ref=src, dst_ref=dst, send_sem=send_sems.at[k], recv_sem=recv_sems.at[k],
                                                device_id=to, device_id_type=MESH)

        mine = pltpu.make_async_copy(x_ref, out_ref.at[s], local_sem)
        mine.start()
        first = [copy(j, x_ref.at[half(c)], out_ref.at[s, half(c)], (cx, cy, c)) for j, (cx, cy) in enumerate(chips)]
        for cp in first:
            cp.start()
        passed = []
        for j, (cx, cy) in enumerate(chips):
            blk = out_ref.at[2 * cx + cy, half(c)]
            copy(j, blk, blk, (x, y, c)).wait_recv()
            p = copy(3 + j, blk, blk, (x, y, 1 - c))
            p.start()
            passed.append(p)
        for j, (cx, cy) in enumerate(chips):
            blk = out_ref.at[2 * cx + cy, half(1 - c)]
            copy(3 + j, blk, blk, (x, y, c)).wait_recv()
        for cp in first + passed:
            cp.wait_send()
        mine.wait()

    return pl.pallas_call(
        body, name="ag_big", in_specs=[ANY], out_specs=ANY,
        out_shape=jax.ShapeDtypeStruct((NSH,) + pack.shape, pack.dtype),
        scratch_shapes=[pltpu.SemaphoreType.DMA((6,)), pltpu.SemaphoreType.DMA((6,)), pltpu.SemaphoreType.DMA],
    )(pack)


def ag_small(pack):
    def body(x_ref, out_ref, send_sems, recv_sems):
        x, y, c = _place()
        s = 2 * x + y
        chips = _other_chips(x, y)

        def copy(k, src, dst, to):
            return pltpu.make_async_remote_copy(src_ref=src, dst_ref=dst, send_sem=send_sems.at[k], recv_sem=recv_sems.at[k],
                                                device_id=to, device_id_type=MESH)

        out_ref[s] = x_ref[...]
        sends = [copy(j, x_ref, out_ref.at[s], (cx, cy, c)) for j, (cx, cy) in enumerate(chips)]
        for cp in sends:
            cp.start()
        for j, (cx, cy) in enumerate(chips):
            blk = out_ref.at[2 * cx + cy]
            copy(j, blk, blk, (x, y, c)).wait_recv()
        for cp in sends:
            cp.wait_send()

    return pl.pallas_call(
        body, name="ag_small", in_specs=[VMEM], out_specs=VMEM,
        out_shape=jax.ShapeDtypeStruct((NSH,) + pack.shape, pack.dtype),
        scratch_shapes=[pltpu.SemaphoreType.DMA((3,)), pltpu.SemaphoreType.DMA((3,))],
    )(pack)


def rs_pair(g):
    Rh = g.shape[1] // 2

    def body(g_ref, own_ref, recv_ref, send_sem, recv_sem, local_sem):
        x, y, c = _place()
        keep = pltpu.make_async_copy(g_ref.at[pl.ds(0, NSH), pl.ds(c * Rh, Rh)], own_ref, local_sem)
        keep.start()
        give = pltpu.make_async_remote_copy(src_ref=g_ref.at[pl.ds(0, NSH), pl.ds((1 - c) * Rh, Rh)], dst_ref=recv_ref,
                                            send_sem=send_sem, recv_sem=recv_sem, device_id=(x, y, 1 - c), device_id_type=MESH)
        give.start()
        give.wait()
        keep.wait()

    sh = jax.ShapeDtypeStruct((NSH, Rh, g.shape[2]), g.dtype)
    return pl.pallas_call(
        body, name="rs_pair", in_specs=[ANY], out_specs=[ANY, ANY], out_shape=[sh, sh],
        scratch_shapes=[pltpu.SemaphoreType.DMA, pltpu.SemaphoreType.DMA, pltpu.SemaphoreType.DMA],
    )(g)


def rs_chips(a):
    def body(a_ref, out_ref, send_sems, recv_sems, local_sem):
        x, y, c = _place()
        s = 2 * x + y
        chips = _other_chips(x, y)
        mine = pltpu.make_async_copy(a_ref.at[s], out_ref.at[s], local_sem)
        mine.start()

        def copy(k, src, dst, to):
            return pltpu.make_async_remote_copy(src_ref=src, dst_ref=dst, send_sem=send_sems.at[k], recv_sem=recv_sems.at[k],
                                                device_id=to, device_id_type=MESH)

        sends = [copy(j, a_ref.at[2 * cx + cy], out_ref.at[s], (cx, cy, c)) for j, (cx, cy) in enumerate(chips)]
        for cp in sends:
            cp.start()
        for j, (cx, cy) in enumerate(chips):
            blk = out_ref.at[2 * cx + cy]
            copy(j, blk, blk, (x, y, c)).wait_recv()
        for cp in sends:
            cp.wait_send()
        mine.wait()

    return pl.pallas_call(
        body, name="rs_chips", in_specs=[ANY], out_specs=ANY, out_shape=jax.ShapeDtypeStruct(a.shape, a.dtype),
        scratch_shapes=[pltpu.SemaphoreType.DMA((3,)), pltpu.SemaphoreType.DMA((3,)), pltpu.SemaphoreType.DMA],
    )(a)


def ag_pair(gh):
    Rh = gh.shape[0]

    def body(g_ref, out_ref, send_sem, recv_sem, local_sem):
        x, y, c = _place()
        mine = pltpu.make_async_copy(g_ref, out_ref.at[pl.ds(c * Rh, Rh)], local_sem)
        mine.start()
        give = pltpu.make_async_remote_copy(src_ref=g_ref, dst_ref=out_ref.at[pl.ds(c * Rh, Rh)], send_sem=send_sem,
                                            recv_sem=recv_sem, device_id=(x, y, 1 - c), device_id_type=MESH)
        give.start()
        give.wait_send()
        got = out_ref.at[pl.ds((1 - c) * Rh, Rh)]
        pltpu.make_async_remote_copy(src_ref=got, dst_ref=got, send_sem=send_sem, recv_sem=recv_sem,
                                     device_id=(x, y, c), device_id_type=MESH).wait_recv()
        mine.wait()

    return pl.pallas_call(
        body, name="ag_pair", in_specs=[ANY], out_specs=ANY, out_shape=jax.ShapeDtypeStruct((2 * Rh, gh.shape[1]), gh.dtype),
        scratch_shapes=[pltpu.SemaphoreType.DMA, pltpu.SemaphoreType.DMA, pltpu.SemaphoreType.DMA],
    )(gh)


def small_allreduce(p):
    def body(p_ref, out_ref, rbuf, send_sems, recv_sems):
        x, y, c = _place()
        me = 4 * x + 2 * y + c
        rbuf[me] = p_ref[2 * x + y]
        flip = lambda v, f: (1 - v) if f else v
        peers = [(flip(x, k >> 2 & 1), flip(y, k >> 1 & 1), flip(c, k & 1)) for k in range(1, 8)]

        def copy(k, src, dst, to):
            return pltpu.make_async_remote_copy(src_ref=src, dst_ref=dst, send_sem=send_sems.at[k], recv_sem=recv_sems.at[k],
                                                device_id=to, device_id_type=MESH)

        sends = [copy(k, p_ref.at[2 * px + py], rbuf.at[me], (px, py, pc)) for k, (px, py, pc) in enumerate(peers)]
        for cp in sends:
            cp.start()
        for k, (px, py, pc) in enumerate(peers):
            blk = rbuf.at[4 * px + 2 * py + pc]
            copy(k, blk, blk, (x, y, c)).wait_recv()
        for cp in sends:
            cp.wait_send()
        acc = rbuf[0]
        for d in range(1, 8):
            acc = acc + rbuf[d]
        out_ref[...] = acc

    return pl.pallas_call(
        body, name="small_allreduce", in_specs=[VMEM], out_specs=VMEM,
        out_shape=jax.ShapeDtypeStruct(p.shape[1:], F32),
        scratch_shapes=[pltpu.VMEM((8,) + p.shape[1:], F32), pltpu.SemaphoreType.DMA((7,)), pltpu.SemaphoreType.DMA((7,))],
    )(p)


def _rows_tile(rows, cap):
    return _pick(rows, cap) if rows % 128 == 0 else rows


def sum_pair(own, recv):
    n, rh, d = own.shape
    tr = _rows_tile(rh, 512)

    def body(a_ref, b_ref, o_ref):
        o_ref[...] = (a_ref[...].astype(F32) + b_ref[...].astype(F32)).astype(o_ref.dtype)

    spec = pl.BlockSpec((None, tr, d), lambda s, i: (s, i, 0))
    return pl.pallas_call(
        body, name="sum_pair", grid=(n, rh // tr), in_specs=[spec, spec], out_specs=spec,
        out_shape=jax.ShapeDtypeStruct(own.shape, BF16), compiler_params=_cparams(("parallel", "parallel")),
    )(own, recv)


def sum_chips(parts):
    n, rh, d = parts.shape
    tr = _rows_tile(rh, 512)

    def body(p_ref, o_ref):
        acc = p_ref[0].astype(F32)
        for s in range(1, n):
            acc = acc + p_ref[s].astype(F32)
        o_ref[...] = acc

    return pl.pallas_call(
        body, name="sum_chips", grid=(rh // tr,), in_specs=[pl.BlockSpec((n, tr, d), lambda i: (0, i, 0))],
        out_specs=pl.BlockSpec((tr, d), lambda i: (i, 0)), out_shape=jax.ShapeDtypeStruct((rh, d), F32),
        compiler_params=_cparams(("parallel",)),
    )(parts)


ADAM_LR, ADAM_B1, ADAM_B2, ADAM_EPS, ADAM_WD, ADAM_STEP = 0.001, 0.9, 0.999, 1e-08, 0.01, 10


def adamw_call(name, w, g, m, v):
    rows, cols = w.shape
    tr = _rows_tile(rows, 512)

    def body(w_ref, g_ref, m_ref, v_ref, d_ref, nm_ref, nv_ref):
        g_ = g_ref[...]
        m_ = ADAM_B1 * m_ref[...] + (1.0 - ADAM_B1) * g_
        v_ = ADAM_B2 * v_ref[...] + (1.0 - ADAM_B2) * (g_ * g_)
        m_hat = m_ / (1.0 - ADAM_B1 ** ADAM_STEP)
        v_hat = v_ / (1.0 - ADAM_B2 ** ADAM_STEP)
        d_ref[...] = -ADAM_LR * (m_hat / (jnp.sqrt(v_hat) + ADAM_EPS) + ADAM_WD * w_ref[...])
        nm_ref[...] = m_
        nv_ref[...] = v_

    spec = pl.BlockSpec((tr, cols), lambda i: (i, 0))
    sh = jax.ShapeDtypeStruct((rows, cols), F32)
    return pl.pallas_call(
        body, name=name, grid=(rows // tr,), in_specs=[spec] * 4, out_specs=[spec] * 3, out_shape=[sh] * 3,
        compiler_params=_cparams(("parallel",)),
    )(w, g, m, v)


def kernel(x, meta_tokens, norm_w, ffn_w_gate, ffn_w_up, ffn_w_down, rel_bias_table, even_w_in, even_conv_w, swa_sinks, dn_a_log, dn_dt_bias, dn_norm_w, even_w_out, odd_w_in, gla_w_gate_up, gla_b_gate, gla_norm_w, odd_w_out, loss_target, m_meta_tokens, m_norm_w, m_ffn_w_gate, m_ffn_w_up, m_ffn_w_down, m_rel_bias_table, m_even_w_in, m_even_conv_w, m_swa_sinks, m_dn_a_log, m_dn_dt_bias, m_dn_norm_w, m_even_w_out, m_odd_w_in, m_gla_w_gate_up, m_gla_b_gate, m_gla_norm_w, m_odd_w_out, v_meta_tokens, v_norm_w, v_ffn_w_gate, v_ffn_w_up, v_ffn_w_down, v_rel_bias_table, v_even_w_in, v_even_conv_w, v_swa_sinks, v_dn_a_log, v_dn_dt_bias, v_dn_norm_w, v_even_w_out, v_odd_w_in, v_gla_w_gate_up, v_gla_b_gate, v_gla_norm_w, v_odd_w_out):
    ws = [meta_tokens, norm_w, ffn_w_gate, ffn_w_up, ffn_w_down, rel_bias_table, even_w_in, even_conv_w, swa_sinks, dn_a_log,
          dn_dt_bias, dn_norm_w, even_w_out, odd_w_in, gla_w_gate_up, gla_b_gate, gla_norm_w, odd_w_out]
    ms = [m_meta_tokens, m_norm_w, m_ffn_w_gate, m_ffn_w_up, m_ffn_w_down, m_rel_bias_table, m_even_w_in, m_even_conv_w,
          m_swa_sinks, m_dn_a_log, m_dn_dt_bias, m_dn_norm_w, m_even_w_out, m_odd_w_in, m_gla_w_gate_up, m_gla_b_gate,
          m_gla_norm_w, m_odd_w_out]
    vs = [v_meta_tokens, v_norm_w, v_ffn_w_gate, v_ffn_w_up, v_ffn_w_down, v_rel_bias_table, v_even_w_in, v_even_conv_w,
          v_swa_sinks, v_dn_a_log, v_dn_dt_bias, v_dn_norm_w, v_even_w_out, v_odd_w_in, v_gla_w_gate_up, v_gla_b_gate,
          v_gla_norm_w, v_odd_w_out]
    short = [n for n, _ in NAMES]
    w = dict(zip(short, ws))
    m = dict(zip(short, ms))
    v = dict(zip(short, vs))

    gb = ag_big(pack_big(w).astype(BF16))
    gs = ag_small(pack_small(w))
    W = make_W(full_from_gathered(gb, gs), {n: w[n] for n, _ in REPL})

    loss_blk, gx, G = core_step(x[0], loss_target[0], W)

    big, small = pack_grads(G)
    own, recv = rs_pair(big.astype(BF16))
    parts = rs_chips(sum_pair(own, recv))
    g_big = unpack_big(ag_pair(sum_chips(parts)))
    g_small_pack = small_allreduce(small)
    g = {**g_big, **unpack_small(g_small_pack)}

    delta, new_m, new_v = {}, {}, {}
    for n, shp in BIG:
        two = lambda t: t.reshape(-1, shp[-1])
        d_, m_, v_ = adamw_call("adamw_" + n, two(w[n]), two(g[n]), two(m[n]), two(v[n]))
        delta[n], new_m[n], new_v[n] = d_.reshape(shp), m_.reshape(shp), v_.reshape(shp)
    d_, m_, v_ = adamw_call("adamw_small", pack_small(w), g_small_pack, pack_small(m), pack_small(v))
    delta.update(unpack_small(d_))
    new_m.update(unpack_small(m_))
    new_v.update(unpack_small(v_))

    loss = lax.psum(loss_blk[0, 0], ("x", "y", "c"))
    return (loss, gx[None], *[g[n] for n in short], *[delta[n] for n in short], *[new_m[n] for n in short],
            *[new_v[n] for n in short])
```

```python
import functools
import math

import numpy as np
import jax
import jax.numpy as jnp
from jax import lax
from jax.experimental import pallas as pl
from jax.experimental.pallas import tpu as pltpu

F32 = jnp.float32
BF16 = jnp.bfloat16
HI = lax.Precision.HIGHEST

D = 1024
N_META = 16
PADR = 128
ZROWS = PADR - N_META
D_FF = 2816
NSH = 4
FSH = D_FF // NSH
EPS = 1e-6
NEG = -1e30
CH = 64
BLK = 128
LANE = 128
VMEM_LIMIT = 56 * 1024 * 1024

E_QA, E_KA, E_VA, E_QB, E_KB, E_VB, E_ZB, E_BA, E_END = 0, 1024, 1280, 1536, 2048, 2560, 3072, 3584, 4096


def _even_in_map():
    m = np.full((E_END,), -1, np.int64)
    for h in range(8):
        m[E_QA + h * 128:E_QA + h * 128 + 64] = np.arange(h * 64, (h + 1) * 64)
    for h in range(2):
        m[E_KA + h * 128:E_KA + h * 128 + 64] = 512 + np.arange(h * 64, (h + 1) * 64)
        m[E_VA + h * 128:E_VA + h * 128 + 64] = 640 + np.arange(h * 64, (h + 1) * 64)
    m[E_QB:E_QB + 2048] = 768 + np.arange(2048)
    m[E_BA:E_BA + 8] = 2816 + np.arange(8)
    return m


def _even_out_map():
    m = np.full((1536,), -1, np.int64)
    for h in range(8):
        m[h * 128:h * 128 + 64] = np.arange(h * 64, (h + 1) * 64)
    m[1024:1536] = 512 + np.arange(512)
    return m


O_Q, O_K, O_V, O_G, O_GK, O_END = 0, 512, 1024, 2048, 3072, 3584


def _odd_in_map():
    m = np.full((O_END,), -1, np.int64)
    m[:3072] = np.arange(3072)
    m[O_GK:O_GK + 16] = 3072 + np.arange(16)
    return m


def _inverse(m, n):
    inv = np.zeros((n,), np.int64)
    for p, o in enumerate(m):
        if o >= 0:
            inv[o] = p
    return inv


def _take_pad(w, m, axis):
    t = jnp.take(w, jnp.asarray(np.maximum(m, 0)), axis=axis)
    shape = [1] * w.ndim
    shape[axis] = m.shape[0]
    return jnp.where(jnp.asarray(m >= 0).reshape(shape), t, jnp.zeros((), w.dtype))


def _mm(a, b, prec=HI):
    return lax.dot_general(a, b, (((1,), (0,)), ((), ())), precision=prec, preferred_element_type=F32)


def _mm_nt(a, b, prec=HI):
    return lax.dot_general(a, b, (((1,), (1,)), ((), ())), precision=prec, preferred_element_type=F32)


def _mm_tn(a, b, prec=HI):
    return lax.dot_general(a, b, (((0,), (0,)), ((), ())), precision=prec, preferred_element_type=F32)


def _bdot(a, b, dims):
    return lax.dot_general(a.astype(BF16), b.astype(BF16), (dims, ((), ())), preferred_element_type=F32)


@jax.custom_vjp
def _bmm(a, b):
    return _bdot(a, b, ((1,), (0,)))


@jax.custom_vjp
def _bmm_nt(a, b):
    return _bdot(a, b, ((1,), (1,)))


@jax.custom_vjp
def _bmm_tn(a, b):
    return _bdot(a, b, ((0,), (0,)))


_bmm.defvjp(lambda a, b: (_bmm(a, b), (a, b)), lambda r, g: (_bmm_nt(g, r[1]), _bmm_tn(r[0], g)))
_bmm_nt.defvjp(lambda a, b: (_bmm_nt(a, b), (a, b)), lambda r, g: (_bmm(g, r[1]), _bmm_tn(g, r[0])))
_bmm_tn.defvjp(lambda a, b: (_bmm_tn(a, b), (a, b)), lambda r, g: (_bmm_nt(r[1], g), _bmm(r[0], g)))


def _rms(x, w):
    return x * lax.rsqrt(jnp.mean(x * x, axis=-1, keepdims=True) + EPS) * w


def _sigmoid(x):
    return 1.0 / (1.0 + jnp.exp(-x))


def _silu(x):
    return x * _sigmoid(x)


def _softplus(x):
    return jnp.maximum(x, 0.0) + jnp.log(1.0 + jnp.exp(-jnp.abs(x)))


def _lane_pick(row, idx):
    lane = lax.broadcasted_iota(jnp.int32, row.shape, row.ndim - 1)
    return jnp.sum(jnp.where(lane == idx, row, 0.0), axis=-1, keepdims=True)


def _row_ids(row0, n):
    return row0 + lax.broadcasted_iota(jnp.int32, (n, 1), 0)


def _pick(m, cap):
    best = 64
    for t in range(64, min(m, cap) + 1, 64):
        if m % t == 0:
            best = t
    return best


def _cparams(sem):
    return pltpu.CompilerParams(dimension_semantics=sem, vmem_limit_bytes=VMEM_LIMIT)


def mm_nn(a, b, name, out_dtype=F32):
    M, K = a.shape
    N = b.shape[1]
    tm = _pick(M, 1408)
    tn = _pick(N, 512)

    def body(a_ref, b_ref, o_ref):
        o_ref[...] = _mm(a_ref[...], b_ref[...], None).astype(o_ref.dtype)

    return pl.pallas_call(
        body, name=name, grid=(N // tn, M // tm),
        in_specs=[pl.BlockSpec((tm, K), lambda j, i: (i, 0)), pl.BlockSpec((K, tn), lambda j, i: (0, j))],
        out_specs=pl.BlockSpec((tm, tn), lambda j, i: (i, j)),
        out_shape=jax.ShapeDtypeStruct((M, N), out_dtype),
        compiler_params=_cparams(("parallel", "parallel")),
    )(a, b)


def mm_nt(a, b, name, out_dtype=F32):
    M, K = a.shape
    N = b.shape[0]
    tm = _pick(M, 768)
    tn = _pick(N, 512)

    def body(a_ref, b_ref, o_ref):
        o_ref[...] = _mm_nt(a_ref[...], b_ref[...], None).astype(o_ref.dtype)

    return pl.pallas_call(
        body, name=name, grid=(N // tn, M // tm),
        in_specs=[pl.BlockSpec((tm, K), lambda j, i: (i, 0)), pl.BlockSpec((tn, K), lambda j, i: (j, 0))],
        out_specs=pl.BlockSpec((tm, tn), lambda j, i: (i, j)),
        out_shape=jax.ShapeDtypeStruct((M, N), out_dtype),
        compiler_params=_cparams(("parallel", "parallel")),
    )(a, b)


def mm_tn(a, b, name):
    M, K = a.shape
    N = b.shape[1]
    tk = _pick(K, 512)
    tn = _pick(N, 512)

    def body(a_ref, b_ref, o_ref):
        o_ref[...] = _mm_tn(a_ref[...], b_ref[...], None)

    return pl.pallas_call(
        body, name=name, grid=(K // tk, N // tn),
        in_specs=[pl.BlockSpec((M, tk), lambda i, j: (0, i)), pl.BlockSpec((M, tn), lambda i, j: (0, j))],
        out_specs=pl.BlockSpec((tk, tn), lambda i, j: (i, j)),
        out_shape=jax.ShapeDtypeStruct((K, N), F32),
        compiler_params=_cparams(("parallel", "parallel")),
    )(a, b)


def _row_specs(rows, tm):
    return [pl.BlockSpec((tm, w), functools.partial(lambda i, cb: (i, cb), cb=cb)) for (_, w, cb) in rows]


def _param_specs(params):
    return [pl.BlockSpec(p.shape, functools.partial(lambda i, nd: (0,) * nd, nd=p.ndim)) for p in params]


def rowwise_fwd(name, fn, rows, params, outs, tm=None):
    M = rows[0][0].shape[0]
    tm = tm or _pick(M, 704)
    nr, npar = len(rows), len(params)

    def body(*refs):
        row0 = pl.program_id(0) * tm
        vals = [r[...].astype(F32) for r in refs[:nr]] + [p[...] for p in refs[nr:nr + npar]]
        res = fn(row0, *vals)
        for o_ref, r in zip(refs[nr + npar:], res):
            o_ref[...] = r.astype(o_ref.dtype)

    return pl.pallas_call(
        body, name=name, grid=(M // tm,),
        in_specs=_row_specs(rows, tm) + _param_specs(params),
        out_specs=[pl.BlockSpec((tm, w), lambda i: (i, 0)) for (w, _) in outs],
        out_shape=[jax.ShapeDtypeStruct((M, w), dt) for (w, dt) in outs],
        compiler_params=_cparams(("parallel",)),
    )(*[r[0] for r in rows], *params)


def rowwise_bwd(name, fn, rows, params, douts, drow_dtypes, tm=None):
    M = rows[0][0].shape[0]
    tm = tm or _pick(M, 704)
    nr, npar, nd = len(rows), len(params), len(douts)
    want = [k for k, dt in enumerate(drow_dtypes) if dt is not None]

    def body(*refs):
        i = pl.program_id(0)
        row0 = i * tm
        vals = [r[...].astype(F32) for r in refs[:nr]] + [p[...] for p in refs[nr:nr + npar]]
        cots = tuple(d[...].astype(F32) for d in refs[nr + npar:nr + npar + nd])
        _, vjp = jax.vjp(functools.partial(fn, row0), *vals)
        grads = vjp(cots)
        o_refs = refs[nr + npar + nd:]
        for o_ref, k in zip(o_refs[:len(want)], want):
            o_ref[...] = grads[k].astype(o_ref.dtype)
        for o_ref, g in zip(o_refs[len(want):], grads[nr:]):
            @pl.when(i == 0)
            def _():
                o_ref[...] = g

            @pl.when(i > 0)
            def _():
                o_ref[...] += g

    res = pl.pallas_call(
        body, name=name, grid=(M // tm,),
        in_specs=_row_specs(rows, tm) + _param_specs(params) + _row_specs(douts, tm),
        out_specs=[pl.BlockSpec((tm, rows[k][1]), lambda i: (i, 0)) for k in want] + _param_specs(params),
        out_shape=[jax.ShapeDtypeStruct((M, rows[k][1]), drow_dtypes[k]) for k in want]
        + [jax.ShapeDtypeStruct(p.shape, F32) for p in params],
        compiler_params=_cparams(("arbitrary",)),
    )(*[r[0] for r in rows], *params, *[d[0] for d in douts])
    return res[:len(want)], res[len(want):]


def _fn_prenorm(row0, h, wpre):
    return (_rms(h, wpre),)


def _fn_resnorm(scale, row0, h, f, wpost, wpre):
    h2 = h + scale * _rms(f, wpost)
    return h2, _rms(h2, wpre)


def _fn_res_last(scale, row0, h, f, wpost):
    return (h + scale * _rms(f, wpost),)


def ffn_fwd(name, xn, wg, wu, wd, l, j):
    M = xn.shape[0]
    tm = _pick(M, 704)

    def body(x_ref, wg_ref, wu_ref, wd_ref, f_ref, a_ref, b_ref):
        s = pl.program_id(1)
        x = x_ref[...]
        a = _mm(x, wg_ref[...], None)
        b = _mm(x, wu_ref[...], None)
        hm = (_silu(a) * b).astype(BF16)
        c = _mm(hm, wd_ref[...], None)

        @pl.when(s == 0)
        def _():
            f_ref[...] = c

        @pl.when(s > 0)
        def _():
            f_ref[...] += c

        a_ref[...] = a.astype(BF16)
        b_ref[...] = b.astype(BF16)

    wspec = pl.BlockSpec((None, None, None, D, FSH), lambda i, s: (s, l, j, 0, 0))
    wdspec = pl.BlockSpec((None, None, None, FSH, D), lambda i, s: (s, l, j, 0, 0))
    abspec = pl.BlockSpec((None, tm, FSH), lambda i, s: (s, i, 0))
    return pl.pallas_call(
        body, name=name, grid=(M // tm, NSH),
        in_specs=[pl.BlockSpec((tm, D), lambda i, s: (i, 0)), wspec, wspec, wdspec],
        out_specs=[pl.BlockSpec((tm, D), lambda i, s: (i, 0)), abspec, abspec],
        out_shape=[jax.ShapeDtypeStruct((M, D), F32), jax.ShapeDtypeStruct((NSH, M, FSH), BF16),
                   jax.ShapeDtypeStruct((NSH, M, FSH), BF16)],
        compiler_params=_cparams(("parallel", "arbitrary")),
    )(xn, wg, wu, wd)


def ffn_bwd_x(name, df, a, b, wg, wu, wd, l, j):
    M = df.shape[0]
    tm = _pick(M, 704)

    def body(df_ref, a_ref, b_ref, wg_ref, wu_ref, wd_ref, dx_ref, da_ref, db_ref, hm_ref):
        s = pl.program_id(1)
        a_ = a_ref[...].astype(F32)
        b_ = b_ref[...].astype(F32)
        dh = _mm_nt(df_ref[...], wd_ref[...], None)
        sig = _sigmoid(a_)
        sil = a_ * sig
        da = (dh * b_ * (sig * (1.0 + a_ * (1.0 - sig)))).astype(BF16)
        db = (dh * sil).astype(BF16)
        c = _mm_nt(da, wg_ref[...], None) + _mm_nt(db, wu_ref[...], None)

        @pl.when(s == 0)
        def _():
            dx_ref[...] = c

        @pl.when(s > 0)
        def _():
            dx_ref[...] += c

        da_ref[...] = da
        db_ref[...] = db
        hm_ref[...] = (sil * b_).astype(BF16)

    wspec = pl.BlockSpec((None, None, None, D, FSH), lambda i, s: (s, l, j, 0, 0))
    wdspec = pl.BlockSpec((None, None, None, FSH, D), lambda i, s: (s, l, j, 0, 0))
    abspec = pl.BlockSpec((None, tm, FSH), lambda i, s: (s, i, 0))
    ab = jax.ShapeDtypeStruct((NSH, M, FSH), BF16)
    return pl.pallas_call(
        body, name=name, grid=(M // tm, NSH),
        in_specs=[pl.BlockSpec((tm, D), lambda i, s: (i, 0)), abspec, abspec, wspec, wspec, wdspec],
        out_specs=[pl.BlockSpec((tm, D), lambda i, s: (i, 0)), abspec, abspec, abspec],
        out_shape=[jax.ShapeDtypeStruct((M, D), F32), ab, ab, ab],
        compiler_params=_cparams(("parallel", "arbitrary")),
    )(df, a, b, wg, wu, wd)


def ffn_bwd_w(name, xn, df, da, db, hm):
    M = xn.shape[0]
    tm = _pick(M, 704)
    nt = M // tm

    def body(x_ref, df_ref, da_ref, db_ref, hm_ref, dwg_ref, dwu_ref, dwd_ref, ag, au, ad):
        i = pl.program_id(1)
        x = x_ref[...]
        g = _mm_tn(x, da_ref[...], None)
        u = _mm_tn(x, db_ref[...], None)
        d = _mm_tn(hm_ref[...], df_ref[...], None)

        @pl.when(i == 0)
        def _():
            ag[...] = g
            au[...] = u
            ad[...] = d

        @pl.when(i > 0)
        def _():
            ag[...] += g
            au[...] += u
            ad[...] += d

        @pl.when(i == nt - 1)
        def _():
            dwg_ref[...] = ag[...].astype(BF16)
            dwu_ref[...] = au[...].astype(BF16)
            dwd_ref[...] = ad[...].astype(BF16)

    xspec = pl.BlockSpec((tm, D), lambda s, i: (i, 0))
    abspec = pl.BlockSpec((None, tm, FSH), lambda s, i: (s, i, 0))
    return pl.pallas_call(
        body, name=name, grid=(NSH, nt),
        in_specs=[xspec, xspec, abspec, abspec, abspec],
        out_specs=[pl.BlockSpec((None, D, FSH), lambda s, i: (s, 0, 0)), pl.BlockSpec((None, D, FSH), lambda s, i: (s, 0, 0)),
                   pl.BlockSpec((None, FSH, D), lambda s, i: (s, 0, 0))],
        out_shape=[jax.ShapeDtypeStruct((NSH, D, FSH), BF16), jax.ShapeDtypeStruct((NSH, D, FSH), BF16),
                   jax.ShapeDtypeStruct((NSH, FSH, D), BF16)],
        scratch_shapes=[pltpu.VMEM((D, FSH), F32), pltpu.VMEM((D, FSH), F32), pltpu.VMEM((FSH, D), F32)],
        compiler_params=_cparams(("parallel", "arbitrary")),
    )(xn, df, da, db, hm)


def _t5_bucket_np(rel):
    n = np.maximum(rel, 0)
    n_f = np.maximum(n, 1).astype(np.float32)
    large = 16 + (np.log(n_f / np.float32(16)) / np.float32(math.log(8.0)) * np.float32(16)).astype(np.int32)
    large = np.minimum(large, 31)
    return np.where(n < 16, n, large).astype(np.int32)


def _swa_bucket_ids():
    qi = np.arange(BLK)[:, None]
    kj = np.arange(BLK)[None, :]
    out = np.full((3, BLK, 3 * BLK), -1, np.int32)
    for v in range(3):
        pos_q = v * BLK + qi - ZROWS
        rel_m = pos_q - (kj - ZROWS)
        ok_m = (kj >= ZROWS) & (rel_m >= 0) & (pos_q >= 0)
        out[v, :, 0:BLK] = np.where(ok_m, _t5_bucket_np(rel_m), -1)
        pos_kp = (v - 1) * BLK + kj - ZROWS
        rel_p = BLK + qi - kj
        ok_p = (pos_kp >= N_META) & (rel_p >= 0) & (rel_p < BLK) & np.full_like(ok_m, v >= 1)
        out[v, :, BLK:2 * BLK] = np.where(ok_p, _t5_bucket_np(rel_p), -1)
        pos_kc = v * BLK + kj - ZROWS
        rel_c = qi - kj
        ok_c = (pos_kc >= N_META) & (rel_c >= 0) & (rel_c < BLK)
        out[v, :, 2 * BLK:] = np.where(ok_c, _t5_bucket_np(rel_c), -1)
    return out


def swa_bias_fwd(table, ids):
    def body(t_ref, id_ref, o_ref):
        for v in range(3):
            for h in range(8):
                o_ref[v, h] = jnp.where(id_ref[v] < 0, NEG, 0.0)

            def step(b, carry):
                hit = id_ref[v] == b
                for h in range(8):
                    o_ref[v, h] += jnp.where(hit, t_ref[b, h], 0.0)
                return carry

            lax.fori_loop(0, 32, step, 0)

    return pl.pallas_call(
        body, name="swa_bias_fwd",
        in_specs=[pl.BlockSpec(memory_space=pltpu.SMEM), pl.BlockSpec(memory_space=pltpu.VMEM)],
        out_specs=pl.BlockSpec(memory_space=pltpu.VMEM),
        out_shape=jax.ShapeDtypeStruct((3, 8, BLK, 3 * BLK), F32),
        compiler_params=pltpu.CompilerParams(vmem_limit_bytes=VMEM_LIMIT),
    )(table, ids)


def swa_bias_bwd(dbias, ids):
    def body(d_ref, id_ref, o_ref):
        r = lax.broadcasted_iota(jnp.int32, (32, LANE), 0)
        c = lax.broadcasted_iota(jnp.int32, (32, LANE), 1)

        def step(b, acc):
            for v in range(3):
                hit = id_ref[v] == b
                for h in range(8):
                    m = jnp.where(hit, d_ref[v, h], 0.0)
                    s = jnp.sum(jnp.sum(m, axis=1, keepdims=True), axis=0, keepdims=True)
                    acc = acc + jnp.where((r == b) & (c == h), s, 0.0)
            return acc

        o_ref[...] = lax.fori_loop(0, 32, step, jnp.zeros((32, LANE), F32))

    return pl.pallas_call(
        body, name="swa_bias_bwd",
        in_specs=[pl.BlockSpec(memory_space=pltpu.VMEM), pl.BlockSpec(memory_space=pltpu.VMEM)],
        out_specs=pl.BlockSpec(memory_space=pltpu.VMEM),
        out_shape=jax.ShapeDtypeStruct((32, LANE), F32),
        compiler_params=pltpu.CompilerParams(vmem_limit_bytes=VMEM_LIMIT),
    )(dbias, ids)


def _swa_block(q, k3, v3, bias, sinks):
    outs = []
    for hk in range(2):
        kh = k3[:, hk * 128:(hk + 1) * 128]
        vh = v3[:, hk * 128:(hk + 1) * 128]
        for g in range(4):
            h = hk * 4 + g
            s = _bmm_nt(q[:, h * 128:(h + 1) * 128], kh) * 0.125 + bias[h]
            sink = _lane_pick(sinks, h)
            m = lax.stop_gradient(jnp.maximum(jnp.max(s, axis=-1, keepdims=True), sink))
            e = jnp.exp(s - m)
            den = jnp.sum(e, axis=-1, keepdims=True) + jnp.exp(sink - m)
            outs.append(_bmm(e / den, vh))
    return jnp.concatenate(outs, axis=1)


def _swa_in_specs():
    qs = pl.BlockSpec((BLK, 1024), lambda n: (n, E_QA // 1024))
    ks = [pl.BlockSpec((BLK, 256), lambda n: (0, E_KA // 256)),
          pl.BlockSpec((BLK, 256), lambda n: (jnp.maximum(n - 1, 0), E_KA // 256)),
          pl.BlockSpec((BLK, 256), lambda n: (n, E_KA // 256))]
    vs = [pl.BlockSpec((BLK, 256), lambda n: (0, E_VA // 256)),
          pl.BlockSpec((BLK, 256), lambda n: (jnp.maximum(n - 1, 0), E_VA // 256)),
          pl.BlockSpec((BLK, 256), lambda n: (n, E_VA // 256))]
    bs = pl.BlockSpec((None, 8, BLK, 3 * BLK), lambda n: (jnp.minimum(n, 2), 0, 0, 0))
    ss = pl.BlockSpec((1, LANE), lambda n: (0, 0))
    return [qs] + ks + vs + [bs, ss]


def swa_fwd(proj, bias, sinks):
    M = proj.shape[0]

    def body(q_ref, k0, k1, k2, v0, v1, v2, b_ref, s_ref, o_ref):
        k3 = jnp.concatenate([k0[...], k1[...], k2[...]], axis=0)
        v3 = jnp.concatenate([v0[...], v1[...], v2[...]], axis=0)
        o_ref[...] = _swa_block(q_ref[...], k3, v3, b_ref[...], s_ref[...]).astype(o_ref.dtype)

    return pl.pallas_call(
        body, name="swa_fwd", grid=(M // BLK,),
        in_specs=_swa_in_specs(),
        out_specs=pl.BlockSpec((BLK, 1024), lambda n: (n, 0)),
        out_shape=jax.ShapeDtypeStruct((M, 1024), BF16),
        compiler_params=_cparams(("parallel",)),
    )(proj, proj, proj, proj, proj, proj, proj, bias, sinks)


def swa_bwd(proj, bias, sinks, do):
    M = proj.shape[0]

    def body(q_ref, k0, k1, k2, v0, v1, v2, b_ref, s_ref, do_ref, dq_ref, dk_ref, dv_ref, db_ref, ds_ref):
        n = pl.program_id(0)

        @pl.when(n == 0)
        def _():
            dk_ref[...] = jnp.zeros_like(dk_ref)
            dv_ref[...] = jnp.zeros_like(dv_ref)
            ds_ref[...] = jnp.zeros_like(ds_ref)

        k3 = jnp.concatenate([k0[...], k1[...], k2[...]], axis=0)
        v3 = jnp.concatenate([v0[...], v1[...], v2[...]], axis=0)
        _, vjp = jax.vjp(_swa_block, q_ref[...], k3, v3, b_ref[...], s_ref[...])
        dq, dk3, dv3, dbias, dsink = vjp(do_ref[...].astype(F32))
        dq_ref[...] = dq
        prev = pl.multiple_of(jnp.maximum(n - 1, 0) * BLK, BLK)
        cur = pl.multiple_of(n * BLK, BLK)
        dk_ref[pl.ds(0, BLK), :] += dk3[0:BLK]
        dv_ref[pl.ds(0, BLK), :] += dv3[0:BLK]
        dk_ref[pl.ds(prev, BLK), :] += dk3[BLK:2 * BLK]
        dv_ref[pl.ds(prev, BLK), :] += dv3[BLK:2 * BLK]
        dk_ref[pl.ds(cur, BLK), :] += dk3[2 * BLK:]
        dv_ref[pl.ds(cur, BLK), :] += dv3[2 * BLK:]
        ds_ref[...] += dsink

        @pl.when(n <= 2)
        def _():
            db_ref[...] = dbias

        @pl.when(n > 2)
        def _():
            db_ref[...] += dbias

    return pl.pallas_call(
        body, name="swa_bwd", grid=(M // BLK,),
        in_specs=_swa_in_specs() + [pl.BlockSpec((BLK, 1024), lambda n: (n, 0))],
        out_specs=[pl.BlockSpec((BLK, 1024), lambda n: (n, 0)),
                   pl.BlockSpec((M, 256), lambda n: (0, 0)), pl.BlockSpec((M, 256), lambda n: (0, 0)),
                   pl.BlockSpec((None, 8, BLK, 3 * BLK), lambda n: (jnp.minimum(n, 2), 0, 0, 0)),
                   pl.BlockSpec((1, LANE), lambda n: (0, 0))],
        out_shape=[jax.ShapeDtypeStruct((M, 1024), F32), jax.ShapeDtypeStruct((M, 256), F32),
                   jax.ShapeDtypeStruct((M, 256), F32), jax.ShapeDtypeStruct((3, 8, BLK, 3 * BLK), F32),
                   jax.ShapeDtypeStruct((1, LANE), F32)],
        compiler_params=_cparams(("arbitrary",)),
    )(proj, proj, proj, proj, proj, proj, proj, bias, sinks, do)


def _shift_rows_impl(x, k):
    n = x.shape[0]
    rolled = pltpu.roll(x, k, 0)
    return jnp.where(_row_ids(0, n) >= k, rolled, 0.0)


def _unshift_rows_impl(g, k):
    n = g.shape[0]
    rolled = pltpu.roll(g, n - k, 0)
    return jnp.where(_row_ids(0, n) < n - k, rolled, 0.0)


@functools.partial(jax.custom_vjp, nondiff_argnums=(1,))
def _shift_rows(x, k):
    return _shift_rows_impl(x, k)


def _shift_rows_f(x, k):
    return _shift_rows_impl(x, k), None


def _shift_rows_b(k, _, g):
    return (_unshift_rows_impl(g, k),)


_shift_rows.defvjp(_shift_rows_f, _shift_rows_b)


def _conv_silu(x, w):
    rid = lax.broadcasted_iota(jnp.int32, w.shape, 0)
    y = x * jnp.sum(jnp.where(rid == 3, w, 0.0), axis=0, keepdims=True)
    for k in range(1, 4):
        y = y + _shift_rows(x, k) * jnp.sum(jnp.where(rid == 3 - k, w, 0.0), axis=0, keepdims=True)
    y = jnp.where(_row_ids(0, x.shape[0]) >= ZROWS, y, 0.0)
    return _silu(y)


def conv_fwd(proj, conv_w):
    M = proj.shape[0]
    nb = conv_w.shape[1] // LANE

    def body(x_ref, w_ref, o_ref):
        o_ref[...] = _conv_silu(x_ref[...], w_ref[...])

    return pl.pallas_call(
        body, name="conv_fwd", grid=(nb,),
        in_specs=[pl.BlockSpec((M, LANE), lambda c: (0, E_QB // LANE + c)), pl.BlockSpec((4, LANE), lambda c: (0, c))],
        out_specs=pl.BlockSpec((M, LANE), lambda c: (0, c)),
        out_shape=jax.ShapeDtypeStruct((M, conv_w.shape[1]), F32),
        compiler_params=_cparams(("parallel",)),
    )(proj, conv_w)


def conv_bwd(proj, conv_w, dy):
    M = proj.shape[0]
    nb = conv_w.shape[1] // LANE

    def body(x_ref, w_ref, dy_ref, dx_ref, dw_ref):
        _, vjp = jax.vjp(_conv_silu, x_ref[...], w_ref[...])
        dx, dw = vjp(dy_ref[...])
        dx_ref[...] = dx
        dw_ref[...] = dw

    return pl.pallas_call(
        body, name="conv_bwd", grid=(nb,),
        in_specs=[pl.BlockSpec((M, LANE), lambda c: (0, E_QB // LANE + c)), pl.BlockSpec((4, LANE), lambda c: (0, c)),
                  pl.BlockSpec((M, LANE), lambda c: (0, c))],
        out_specs=[pl.BlockSpec((M, LANE), lambda c: (0, c)), pl.BlockSpec((4, LANE), lambda c: (0, c))],
        out_shape=[jax.ShapeDtypeStruct((M, conv_w.shape[1]), F32), jax.ShapeDtypeStruct(conv_w.shape, F32)],
        compiler_params=_cparams(("parallel",)),
    )(proj, conv_w, dy)


def _fn_dn_prep(row0, yq, yk, ba, dnp):
    tm = yq.shape[0]
    real = _row_ids(row0, tm) >= ZROWS
    qs, ks, gs, bs = [], [], [], []
    for h in range(4):
        q = yq[:, h * 128:(h + 1) * 128]
        k = yk[:, h * 128:(h + 1) * 128]
        qs.append(q * lax.rsqrt(jnp.sum(q * q, axis=-1, keepdims=True) + 1e-6) * (128.0 ** -0.5))
        ks.append(k * lax.rsqrt(jnp.sum(k * k, axis=-1, keepdims=True) + 1e-6))
        beta = _sigmoid(_lane_pick(ba, h))
        g = -jnp.exp(_lane_pick(dnp, h)) * _softplus(_lane_pick(ba, 4 + h) + _lane_pick(dnp, 4 + h))
        g = jnp.where(real, g, 0.0)
        gs.append(jnp.broadcast_to(g, (tm, 128)))
        bs.append(jnp.broadcast_to(beta, (tm, 128)))
    cat = lambda xs: jnp.concatenate(xs, axis=1)
    return cat(qs), cat(ks), cat(gs), cat(bs)


def _dn_chunk(q, k, v, gb, bb, S):
    r = lax.broadcasted_iota(jnp.int32, (CH, CH), 0)
    c = lax.broadcasted_iota(jnp.int32, (CH, CH), 1)
    tri_incl = r >= c
    gcb = _mm(tri_incl.astype(F32), gb)
    g1 = gcb[:, :CH]
    diff = g1 - g1.T
    gamma = jnp.where(tri_incl, jnp.exp(jnp.where(tri_incl, diff, 0.0)), 0.0)
    kb = k * bb
    vb = v * bb
    a = jnp.where(r > c, _bmm_nt(kb, k) * gamma, 0.0)
    p = -a
    t = (r == c).astype(F32) + p
    for _ in range(5):
        p = _mm(p, p)
        t = t + _mm(t, p)
    eg = jnp.exp(gcb)
    u = _mm(t, vb)
    w = _mm(t, kb * eg)
    attn = _bmm_nt(q, k) * gamma
    gtot = _mm(jnp.ones((CH, CH), F32), gb)
    k_dec = k * jnp.exp(gtot - gcb)
    g_last = jnp.exp(_mm(jnp.ones((128, CH), F32), gb))
    v_new = u - _bmm(w, S)
    o = _bmm(q * eg, S) + _bmm(attn, v_new)
    return o, S * g_last + _bmm_tn(k_dec, v_new)


def _gla_chunk(q, k, v, glog, S):
    r = lax.broadcasted_iota(jnp.int32, (CH, CH), 0)
    c = lax.broadcasted_iota(jnp.int32, (CH, CH), 1)
    tri = r >= c
    bcum = _mm(tri.astype(F32), glog)
    q_dec = q * (128.0 ** -0.5) * jnp.exp(bcum)
    attn = jnp.where(tri, _bmm_nt(q_dec, k * jnp.exp(-bcum)), 0.0)
    o = _bmm(attn, v) + _bmm(q_dec, S)
    btot = _mm(jnp.ones((CH, CH), F32), glog)
    k_dec = k * jnp.exp(btot - bcum)
    decay = jnp.exp(_mm_tn(glog, jnp.ones((CH, v.shape[1]), F32)))
    return o, S * decay + _bmm_tn(k_dec, v)


def chunk_fwd(name, chunk_fn, ins, dv):
    M = ins[0][0].shape[0]
    N = M // CH
    ni = len(ins)
    ws = [w for (_, w, _) in ins]

    def body(*refs):
        o_ref, sall_ref, s_ref = refs[ni:]

        @pl.when(pl.program_id(0) == 0)
        def _():
            s_ref[...] = jnp.zeros_like(s_ref)

        for h in range(4):
            S = s_ref[h]
            sall_ref[h] = S
            o, s_new = chunk_fn(*[r[:, h * w:(h + 1) * w] for r, w in zip(refs[:ni], ws)], S)
            o_ref[:, h * dv:(h + 1) * dv] = o
            s_ref[h] = s_new

    specs = [pl.BlockSpec((CH, 4 * w), functools.partial(lambda n, cb: (n, cb), cb=cb // 4)) for (_, w, cb) in ins]
    return pl.pallas_call(
        body, name=name, grid=(N,),
        in_specs=specs,
        out_specs=[pl.BlockSpec((CH, 4 * dv), lambda n: (n, 0)),
                   pl.BlockSpec((4, None, 128, dv), lambda n: (0, n, 0, 0))],
        out_shape=[jax.ShapeDtypeStruct((M, 4 * dv), F32), jax.ShapeDtypeStruct((4, N, 128, dv), F32)],
        scratch_shapes=[pltpu.VMEM((4, 128, dv), F32)],
        compiler_params=_cparams(("arbitrary",)),
    )(*[a for (a, _, _) in ins])


def chunk_bwd(name, chunk_fn, ins, dv, s_all, do):
    M = ins[0][0].shape[0]
    N = M // CH
    ni = len(ins)
    ws = [w for (_, w, _) in ins]

    def body(*refs):
        sall_ref, do_ref = refs[ni:ni + 2]
        d_refs = refs[ni + 2:2 * ni + 2]
        ds_ref = refs[2 * ni + 2]

        @pl.when(pl.program_id(0) == 0)
        def _():
            ds_ref[...] = jnp.zeros_like(ds_ref)

        for h in range(4):
            _, vjp = jax.vjp(chunk_fn, *[r[:, h * w:(h + 1) * w] for r, w in zip(refs[:ni], ws)], sall_ref[h])
            grads = vjp((do_ref[:, h * dv:(h + 1) * dv], ds_ref[h]))
            for d_ref, w, g in zip(d_refs, ws, grads[:ni]):
                d_ref[:, h * w:(h + 1) * w] = g
            ds_ref[h] = grads[ni]

    rev = lambda n: N - 1 - n
    specs = [pl.BlockSpec((CH, 4 * w), functools.partial(lambda n, cb: (rev(n), cb), cb=cb // 4)) for (_, w, cb) in ins]
    return pl.pallas_call(
        body, name=name, grid=(N,),
        in_specs=specs + [pl.BlockSpec((4, None, 128, dv), lambda n: (0, rev(n), 0, 0)),
                          pl.BlockSpec((CH, 4 * dv), lambda n: (rev(n), 0))],
        out_specs=[pl.BlockSpec((CH, 4 * w), lambda n: (rev(n), 0)) for w in ws],
        out_shape=[jax.ShapeDtypeStruct((M, 4 * w), F32) for w in ws],
        scratch_shapes=[pltpu.VMEM((4, 128, dv), F32)],
        compiler_params=_cparams(("arbitrary",)),
    )(*[a for (a, _, _) in ins], s_all, do)


def _fn_gate_out(hd, row0, o, z, w):
    outs = []
    for h in range(4):
        outs.append(_rms(o[:, h * hd:(h + 1) * hd], w) * _silu(z[:, h * hd:(h + 1) * hd]))
    return (jnp.concatenate(outs, axis=1),)


def _fn_gla_prep(row0, gk, wgu, bg):
    x = _mm(gk, wgu) + bg
    ls = jnp.minimum(x, 0.0) - jnp.log(1.0 + jnp.exp(-jnp.abs(x)))
    return (jnp.where(_row_ids(row0, gk.shape[0]) >= ZROWS, ls / 16.0, 0.0),)


def loss_call(y, tgt):
    M = y.shape[0]
    tm = _pick(M, 512)

    def body(y_ref, t_ref, l_ref, dy_ref):
        i = pl.program_id(0)
        e = y_ref[...] - t_ref[...]
        dy_ref[...] = e * (1.0 / D)
        part = 0.5 * jnp.sum(jnp.sum(e * e, axis=1, keepdims=True) * (1.0 / D), axis=0, keepdims=True)
        part = jnp.broadcast_to(part, (8, LANE))

        @pl.when(i == 0)
        def _():
            l_ref[...] = part

        @pl.when(i > 0)
        def _():
            l_ref[...] += part

    return pl.pallas_call(
        body, name="loss", grid=(M // tm,),
        in_specs=[pl.BlockSpec((tm, D), lambda i: (i, 0))] * 2,
        out_specs=[pl.BlockSpec((8, LANE), lambda i: (0, 0)), pl.BlockSpec((tm, D), lambda i: (i, 0))],
        out_shape=[jax.ShapeDtypeStruct((8, LANE), F32), jax.ShapeDtypeStruct((M, D), F32)],
        compiler_params=_cparams(("arbitrary",)),
    )(y, tgt)


def _bf(x):
    return x.astype(BF16)


def core_step(x, tgt, W):
    S = x.shape[0]
    M = S + PADR
    ids = jnp.asarray(_swa_bucket_ids())
    h0 = jnp.concatenate([jnp.zeros((ZROWS, D), F32), W["meta"], x], axis=0)
    nw = W["norm"]
    nrow = lambda l, k: nw[l, k][None, :]
    wg, wu, wd = W["ffn_g"], W["ffn_u"], W["ffn_d"]
    e_in, e_out = W["e_in"], W["e_out"]
    o_in, o_out = W["o_in"], W["o_out"]
    sinks = jnp.pad(W["sinks"], ((0, 0), (0, LANE - 8)))
    dnp = jnp.pad(jnp.concatenate([W["a_log"], W["dt_bias"]], axis=1), ((0, 0), (0, LANE - 8)))
    wgu = jnp.pad(W["gate_up"], ((0, LANE - 16), (0, 0)))
    bg = W["b_gate"]
    full = lambda a: (a, a.shape[1], 0)

    saved = []
    h = h0
    (hn,) = rowwise_fwd("prenorm_0", _fn_prenorm, [full(h)], [nrow(0, 0)], [(D, BF16)])
    bias = swa_bias_fwd(W["rel"], ids)
    for l in range(2):
        st = {"h_a": h, "hn_a": hn}
        f1, a1, b1 = ffn_fwd(f"ffn_fwd_{l}0", hn, wg, wu, wd, l, 0)
        h, hn = rowwise_fwd(f"resnorm_{l}1", functools.partial(_fn_resnorm, 0.5), [full(h), full(f1)],
                            [nrow(l, 1), nrow(l, 2)], [(D, F32), (D, BF16)])
        st.update(f1=f1, a1=a1, b1=b1, h_b=h, hn_b=hn)
        if l == 0:
            proj = mm_nn(hn, e_in, "e_proj")
            o_a = swa_fwd(proj, bias, sinks)
            y = conv_fwd(proj, W["conv"])
            qn, kn, gb, bb = rowwise_fwd(
                "dn_prep", _fn_dn_prep, [(y, 512, 0), (y, 512, 1), (proj, LANE, E_BA // LANE)], [dnp], [(512, F32)] * 4)
            ins = [(qn, 128, 0), (kn, 128, 0), (y, 128, 8), (gb, 128, 0), (bb, 128, 0)]
            o_dn, s_all = chunk_fwd("dn_fwd", _dn_chunk, ins, 128)
            (o_b,) = rowwise_fwd("dn_out", functools.partial(_fn_gate_out, 128),
                                 [full(o_dn), (proj, 512, E_ZB // 512)], [W["dn_norm"]], [(512, BF16)])
            omix = jnp.concatenate([o_a, o_b], axis=1)
            mix = mm_nn(omix, e_out, "e_mix")
            st.update(proj=proj, y=y, qn=qn, kn=kn, gb=gb, bb=bb, o_dn=o_dn, s_all=s_all, omix=omix)
        else:
            proj = mm_nn(hn, o_in, "o_proj")
            (glog,) = rowwise_fwd("gla_prep", _fn_gla_prep, [(proj, LANE, O_GK // LANE)], [wgu, bg], [(512, F32)])
            ins = [(proj, 128, O_Q // 128), (proj, 128, O_K // 128), (proj, 256, O_V // 256), (glog, 128, 0)]
            o_g, s_all = chunk_fwd("gla_fwd", _gla_chunk, ins, 256)
            (omix,) = rowwise_fwd("gla_out", functools.partial(_fn_gate_out, 256),
                                  [full(o_g), (proj, 1024, O_G // 1024)], [W["gla_norm"]], [(1024, BF16)])
            mix = mm_nn(omix, o_out, "o_mix")
            st.update(proj=proj, glog=glog, o_g=o_g, s_all=s_all, omix=omix)
        h, hn = rowwise_fwd(f"resnorm_{l}3", functools.partial(_fn_resnorm, 1.0), [full(h), full(mix)],
                            [nrow(l, 3), nrow(l, 4)], [(D, F32), (D, BF16)])
        st.update(mix=mix, h_c=h, hn_c=hn)
        f2, a2, b2 = ffn_fwd(f"ffn_fwd_{l}1", hn, wg, wu, wd, l, 1)
        st.update(f2=f2, a2=a2, b2=b2)
        if l == 0:
            h, hn = rowwise_fwd("resnorm_05", functools.partial(_fn_resnorm, 0.5), [full(h), full(f2)],
                                [nrow(0, 5), nrow(1, 0)], [(D, F32), (D, BF16)])
        else:
            (h,) = rowwise_fwd("res_last", functools.partial(_fn_res_last, 0.5), [full(h), full(f2)],
                               [nrow(1, 5)], [(D, F32)])
        saved.append(st)

    loss_blk, dy = loss_call(h[PADR:], tgt)
    dh = jnp.concatenate([jnp.zeros((PADR, D), F32), dy], axis=0)

    G = {}
    dnorm = [[None] * 6 for _ in range(2)]
    dWg = [[None, None], [None, None]]
    dWu = [[None, None], [None, None]]
    dWd = [[None, None], [None, None]]
    dhn = None
    for l in (1, 0):
        st = saved[l]
        if l == 1:
            (dh_, df), (dw5,) = rowwise_bwd(
                "res_last_b", functools.partial(_fn_res_last, 0.5), [full(st["h_c"]), full(st["f2"])], [nrow(1, 5)],
                [full(dh)], [F32, BF16])
            dnorm[1][5] = dw5
        else:
            (dh_, df), (dw5, dw0n) = rowwise_bwd(
                "resnorm_05_b", functools.partial(_fn_resnorm, 0.5), [full(st["h_c"]), full(st["f2"])],
                [nrow(0, 5), nrow(1, 0)], [full(dh), full(dhn)], [F32, BF16])
            dnorm[0][5] = dw5
            dnorm[1][0] = dw0n
        dh = dh_
        dxn, da, db, hm = ffn_bwd_x(f"ffn_bx_{l}1", df, st["a2"], st["b2"], wg, wu, wd, l, 1)
        dWg[l][1], dWu[l][1], dWd[l][1] = ffn_bwd_w(f"ffn_bw_{l}1", st["hn_c"], df, da, db, hm)
        (dh_, dmix), (dw3, dw4) = rowwise_bwd(
            f"resnorm_{l}3_b", functools.partial(_fn_resnorm, 1.0), [full(st["h_b"]), full(st["mix"])],
            [nrow(l, 3), nrow(l, 4)], [full(dh), full(dxn)], [F32, BF16])
        dnorm[l][3], dnorm[l][4] = dw3, dw4
        dh = dh_
        proj = st["proj"]
        if l == 1:
            G["o_out"] = mm_tn(st["omix"], dmix, "o_out_dw")
            domix = mm_nt(dmix, o_out, "o_mix_dx")
            (do_g, dgate), (dgn,) = rowwise_bwd(
                "gla_out_b", functools.partial(_fn_gate_out, 256), [full(st["o_g"]), (proj, 1024, O_G // 1024)],
                [W["gla_norm"]], [full(domix)], [F32, F32])
            G["gla_norm"] = dgn
            ins = [(proj, 128, O_Q // 128), (proj, 128, O_K // 128), (proj, 256, O_V // 256), (st["glog"], 128, 0)]
            dq, dk, dv, dglog = chunk_bwd("gla_bwd", _gla_chunk, ins, 256, st["s_all"], do_g)
            (dgk,), (dwgu, dbg) = rowwise_bwd("gla_prep_b", _fn_gla_prep, [(proj, LANE, O_GK // LANE)], [wgu, bg],
                                              [full(dglog)], [F32])
            G["gate_up"] = dwgu[:16]
            G["b_gate"] = dbg
            dproj = _bf(jnp.concatenate([dq, dk, dv, dgate, dgk, jnp.zeros((M, O_END - O_GK - LANE), F32)], axis=1))
            G["o_in"] = mm_tn(st["hn_b"], dproj, "o_in_dw")
            dhn_b = mm_nt(dproj, o_in, "o_proj_dx")
        else:
            G["e_out"] = mm_tn(st["omix"], dmix, "e_out_dw")
            domix = mm_nt(dmix, e_out, "e_mix_dx")
            (do_dn, dz), (ddn,) = rowwise_bwd(
                "dn_out_b", functools.partial(_fn_gate_out, 128), [full(st["o_dn"]), (proj, 512, E_ZB // 512)],
                [W["dn_norm"]], [(domix, 512, 2)], [F32, F32])
            G["dn_norm"] = ddn
            ins = [(st["qn"], 128, 0), (st["kn"], 128, 0), (st["y"], 128, 8), (st["gb"], 128, 0), (st["bb"], 128, 0)]
            dqn, dkn, dvv, dgb, dbb = chunk_bwd("dn_bwd", _dn_chunk, ins, 128, st["s_all"], do_dn)
            (dyq, dyk, dba), (ddnp,) = rowwise_bwd(
                "dn_prep_b", _fn_dn_prep, [(st["y"], 512, 0), (st["y"], 512, 1), (proj, LANE, E_BA // LANE)], [dnp],
                [full(dqn), full(dkn), full(dgb), full(dbb)], [F32, F32, F32])
            G["a_log"] = ddnp[:, 0:4]
            G["dt_bias"] = ddnp[:, 4:8]
            dyc = jnp.concatenate([dyq, dyk, dvv], axis=1)
            dxc, dconv = conv_bwd(proj, W["conv"], dyc)
            G["conv"] = dconv
            dq_a, dk_a, dv_a, dbias, dsink = swa_bwd(proj, bias, sinks, domix)
            G["sinks"] = dsink[:, :8]
            G["rel"] = swa_bias_bwd(dbias, ids)[:, :8]
            dproj = _bf(jnp.concatenate([dq_a, dk_a, dv_a, dxc, dz, dba, jnp.zeros((M, E_END - E_BA - LANE), F32)], axis=1))
            G["e_in"] = mm_tn(st["hn_b"], dproj, "e_in_dw")
            dhn_b = mm_nt(dproj, e_in, "e_proj_dx")
        (dh_, df), (dw1, dw2) = rowwise_bwd(
            f"resnorm_{l}1_b", functools.partial(_fn_resnorm, 0.5), [full(st["h_a"]), full(st["f1"])],
            [nrow(l, 1), nrow(l, 2)], [full(dh), full(dhn_b)], [F32, BF16])
        dnorm[l][1], dnorm[l][2] = dw1, dw2
        dh = dh_
        dxn, da, db, hm = ffn_bwd_x(f"ffn_bx_{l}0", df, st["a1"], st["b1"], wg, wu, wd, l, 0)
        dWg[l][0], dWu[l][0], dWd[l][0] = ffn_bwd_w(f"ffn_bw_{l}0", st["hn_a"], df, da, db, hm)
        dhn = dxn
    (dh0p,), (dw00,) = rowwise_bwd("prenorm_0_b", _fn_prenorm, [full(saved[0]["h_a"])], [nrow(0, 0)], [full(dhn)], [F32])
    dnorm[0][0] = dw00
    dh = dh + dh0p
    G["meta"] = dh[ZROWS:PADR]
    G["norm"] = jnp.stack([jnp.concatenate(r, axis=0) for r in dnorm], axis=0)
    G["ffn_g"], G["ffn_u"], G["ffn_d"] = dWg, dWu, dWd
    return loss_blk, dh[PADR:], G


NAMES = [("meta", "meta_tokens"), ("norm", "norm_w"), ("ffn_g", "ffn_w_gate"), ("ffn_u", "ffn_w_up"),
         ("ffn_d", "ffn_w_down"), ("rel", "rel_bias_table"), ("e_in", "even_w_in"), ("conv", "even_conv_w"),
         ("sinks", "swa_sinks"), ("a_log", "dn_a_log"), ("dt_bias", "dn_dt_bias"), ("dn_norm", "dn_norm_w"),
         ("e_out", "even_w_out"), ("o_in", "odd_w_in"), ("gate_up", "gla_w_gate_up"), ("b_gate", "gla_b_gate"),
         ("gla_norm", "gla_norm_w"), ("o_out", "odd_w_out")]
BIG = [("ffn_g", (2, 2, D, FSH)), ("ffn_u", (2, 2, D, FSH)), ("ffn_d", (2, 2, FSH, D)), ("e_in", (1, D, 706)),
       ("e_out", (1, 256, D)), ("o_in", (1, D, 772)), ("o_out", (1, 256, D))]
IN_COLS = 772
SMALL = [("meta", (16, 256)), ("norm", (2, 6, 256)), ("conv", (1, 4, 384)), ("gate_up", (1, 16, 128)),
         ("b_gate", (1, 128)), ("gla_norm", (1, 64))]
REPL = [("rel", (32, 8)), ("sinks", (1, 8)), ("a_log", (1, 4)), ("dt_bias", (1, 4)), ("dn_norm", (1, 128))]
SMALL_REP = 88 * LANE
SMALL_ROWS = 96


def pack_small(t):
    a = jnp.concatenate([t[n].reshape(-1) for n, _ in SMALL])
    b = jnp.concatenate([t[n].reshape(-1) for n, _ in REPL])
    flat = jnp.concatenate([a, jnp.zeros((SMALL_REP - a.shape[0],), F32), b,
                            jnp.zeros((SMALL_ROWS * LANE - SMALL_REP - b.shape[0],), F32)])
    return flat.reshape(SMALL_ROWS, LANE)


def unpack_small(p):
    flat = p.reshape(-1)
    out, r = {}, 0
    for n, shp in SMALL:
        k = int(np.prod(shp))
        out[n] = flat[r:r + k].reshape(shp)
        r += k
    r = SMALL_REP
    for n, shp in REPL:
        k = int(np.prod(shp))
        out[n] = flat[r:r + k].reshape(shp)
        r += k
    return out


def full_from_gathered(gb, gs):
    sm = [unpack_small(gs[s]) for s in range(NSH)]
    full = {n: gb[n] for n in ("ffn_g", "ffn_u", "ffn_d")}
    full["e_in"] = jnp.moveaxis(gb["e_in"], 0, 1).reshape(D, -1)
    full["o_in"] = jnp.moveaxis(gb["o_in"], 0, 1).reshape(D, -1)
    full["e_out"] = gb["e_out"].reshape(-1, D)
    full["o_out"] = gb["o_out"].reshape(-1, D)
    full["meta"] = jnp.concatenate([sm[s]["meta"] for s in range(NSH)], axis=1)
    full["norm"] = jnp.concatenate([sm[s]["norm"] for s in range(NSH)], axis=2)
    full["conv"] = jnp.concatenate([sm[s]["conv"][0] for s in range(NSH)], axis=1)
    full["gate_up"] = jnp.concatenate([sm[s]["gate_up"][0] for s in range(NSH)], axis=1)
    full["b_gate"] = jnp.concatenate([sm[s]["b_gate"] for s in range(NSH)], axis=1)
    full["gla_norm"] = jnp.concatenate([sm[s]["gla_norm"] for s in range(NSH)], axis=1)
    return full


def make_W(full, rep):
    W = dict(full)
    W.update(rep)
    W["e_in"] = _take_pad(full["e_in"], _even_in_map(), 1).astype(BF16)
    W["e_out"] = _take_pad(full["e_out"], _even_out_map(), 0).astype(BF16)
    W["o_in"] = _take_pad(full["o_in"], _odd_in_map(), 1).astype(BF16)
    W["o_out"] = full["o_out"].astype(BF16)
    for n in ("ffn_g", "ffn_u", "ffn_d"):
        W[n] = full[n].astype(BF16)
    for n in ("meta", "norm", "conv", "gate_up", "b_gate", "gla_norm"):
        W[n] = full[n].astype(F32)
    return W


def _col_sh(w):
    return jnp.moveaxis(w.reshape(w.shape[0], NSH, w.shape[1] // NSH), 1, 0)


def grad_items(G):
    ei = _col_sh(jnp.take(G["e_in"], jnp.asarray(_inverse(_even_in_map(), 2824)), axis=1))
    ei = jnp.pad(ei, ((0, 0), (0, 0), (0, IN_COLS - ei.shape[2])))
    oi = _col_sh(jnp.take(G["o_in"], jnp.asarray(_inverse(_odd_in_map(), 3088)), axis=1))
    eo = jnp.take(G["e_out"], jnp.asarray(_inverse(_even_out_map(), 1024)), axis=0).reshape(NSH, 256, D)
    oo = G["o_out"].reshape(NSH, 256, D)
    proj = [(_bf(ei), _bf(eo)), (_bf(oi), _bf(oo))]
    return [[G["ffn_g"][l][0], G["ffn_g"][l][1], G["ffn_u"][l][0], G["ffn_u"][l][1], G["ffn_d"][l][0], G["ffn_d"][l][1],
             proj[l][0], proj[l][1]] for l in range(2)]


def shard_grads(f0, f1):
    st = lambda a, b: jnp.stack([jnp.stack([f0[a], f0[b]]), jnp.stack([f1[a], f1[b]])])
    return {"ffn_g": st(0, 1), "ffn_u": st(2, 3), "ffn_d": st(4, 5), "e_in": f0[6][None, :, :706], "e_out": f0[7][None],
            "o_in": f1[6][None], "o_out": f1[7][None]}


def pack_small_grads(G):
    col_sh = _col_sh
    norm_sh = jnp.moveaxis(G["norm"].reshape(2, 6, NSH, 256), 2, 0)
    a = jnp.concatenate([col_sh(G["meta"]).reshape(NSH, -1), norm_sh.reshape(NSH, -1), col_sh(G["conv"]).reshape(NSH, -1),
                         col_sh(G["gate_up"]).reshape(NSH, -1), G["b_gate"].reshape(NSH, -1),
                         G["gla_norm"].reshape(NSH, -1)], axis=1)
    b = jnp.concatenate([G[n].reshape(-1) for n, _ in REPL])
    b = jnp.broadcast_to(b[None], (NSH, b.shape[0]))
    small = jnp.concatenate([a, jnp.zeros((NSH, SMALL_REP - a.shape[1]), F32), b,
                             jnp.zeros((NSH, SMALL_ROWS * LANE - SMALL_REP - b.shape[1]), F32)], axis=1)
    return small.reshape(NSH, SMALL_ROWS, LANE)


MESH = pl.DeviceIdType.MESH
ANY = pl.BlockSpec(memory_space=pl.ANY)
VMEM = pl.BlockSpec(memory_space=pltpu.VMEM)


def _place():
    return lax.axis_index("x"), lax.axis_index("y"), lax.axis_index("c")


def _other_chips(x, y):
    return [(1 - x, y), (x, 1 - y), (1 - x, 1 - y)]


def _rcopy(send_sems, recv_sems, k, src, dst, to):
    return pltpu.make_async_remote_copy(src_ref=src, dst_ref=dst, send_sem=send_sems.at[k], recv_sem=recv_sems.at[k],
                                        device_id=to, device_id_type=MESH)


_AG_ITEMS = [[(0, (0,)), (1, (0,)), (2, (0,)), (3, ()), (4, ())], [(0, (1,)), (1, (1,)), (2, (1,)), (5, ()), (6, ())]]


def ag_big(shards):
    n = len(shards)
    ni = len(_AG_ITEMS[0])

    def body(*refs):
        in_refs, out_refs = refs[:n], refs[n:2 * n]
        send_sems, recv_sems, local_sems = refs[2 * n:]
        x, y, c = _place()
        s = 2 * x + y
        chips = _other_chips(x, y)
        copy = functools.partial(_rcopy, send_sems, recv_sems)
        locals_ = [pltpu.make_async_copy(in_refs[k], out_refs[k].at[s], local_sems.at[k]) for k in range(n)]
        for cp in locals_:
            cp.start()
        src = lambda it: in_refs[it[0]].at[it[1]] if it[1] else in_refs[it[0]]
        dst = lambda it, slot: out_refs[it[0]].at[(slot,) + it[1]]
        for cc in range(2):
            @pl.when(c == cc)
            def _():
                mine, theirs = _AG_ITEMS[cc], _AG_ITEMS[1 - cc]
                first, passed = [], []
                for i, it in enumerate(mine):
                    for j, (cx, cy) in enumerate(chips):
                        first.append(copy(i * 3 + j, src(it), dst(it, s), (cx, cy, c)))
                for cp in first:
                    cp.start()
                for j, (cx, cy) in enumerate(chips):
                    for i, it in enumerate(mine):
                        blk = dst(it, 2 * cx + cy)
                        copy(i * 3 + j, blk, blk, (x, y, c)).wait_recv()
                        p = copy(ni * 3 + i * 3 + j, blk, blk, (x, y, 1 - c))
                        p.start()
                        passed.append(p)
                for j, (cx, cy) in enumerate(chips):
                    for i, it in enumerate(theirs):
                        blk = dst(it, 2 * cx + cy)
                        copy(ni * 3 + i * 3 + j, blk, blk, (x, y, c)).wait_recv()
                for cp in first + passed:
                    cp.wait_send()
        for cp in locals_:
            cp.wait()

    return pl.pallas_call(
        body, name="ag_big", in_specs=[ANY] * n, out_specs=[ANY] * n,
        out_shape=[jax.ShapeDtypeStruct((NSH,) + a.shape, a.dtype) for a in shards],
        scratch_shapes=[pltpu.SemaphoreType.DMA((6 * ni,)), pltpu.SemaphoreType.DMA((6 * ni,)), pltpu.SemaphoreType.DMA((n,))],
    )(*shards)


def ag_small(pack):
    def body(x_ref, out_ref, send_sems, recv_sems):
        x, y, c = _place()
        s = 2 * x + y
        chips = _other_chips(x, y)

        def copy(k, src, dst, to):
            return pltpu.make_async_remote_copy(src_ref=src, dst_ref=dst, send_sem=send_sems.at[k], recv_sem=recv_sems.at[k],
                                                device_id=to, device_id_type=MESH)

        out_ref[s] = x_ref[...]
        sends = [copy(j, x_ref, out_ref.at[s], (cx, cy, c)) for j, (cx, cy) in enumerate(chips)]
        for cp in sends:
            cp.start()
        for j, (cx, cy) in enumerate(chips):
            blk = out_ref.at[2 * cx + cy]
            copy(j, blk, blk, (x, y, c)).wait_recv()
        for cp in sends:
            cp.wait_send()

    return pl.pallas_call(
        body, name="ag_small", in_specs=[VMEM], out_specs=VMEM,
        out_shape=jax.ShapeDtypeStruct((NSH,) + pack.shape, pack.dtype),
        scratch_shapes=[pltpu.SemaphoreType.DMA((3,)), pltpu.SemaphoreType.DMA((3,))],
    )(pack)


def rs_pair(items):
    ni = len(items[0])

    def body(*refs):
        in_refs = [refs[:ni], refs[ni:2 * ni]]
        recv_refs = refs[2 * ni:3 * ni]
        send_sems, recv_sems = refs[3 * ni:]
        x, y, c = _place()
        copy = functools.partial(_rcopy, send_sems, recv_sems)
        for cc in range(2):
            @pl.when(c == cc)
            def _():
                cps = [copy(i * NSH + s, in_refs[1 - cc][i].at[s], recv_refs[i].at[s], (x, y, 1 - c))
                       for i in range(ni) for s in range(NSH)]
                for cp in cps:
                    cp.start()
                for cp in cps:
                    cp.wait()

    return pl.pallas_call(
        body, name="rs_pair", in_specs=[ANY] * (2 * ni), out_specs=[ANY] * ni,
        out_shape=[jax.ShapeDtypeStruct(a.shape, a.dtype) for a in items[0]],
        scratch_shapes=[pltpu.SemaphoreType.DMA((ni * NSH,)), pltpu.SemaphoreType.DMA((ni * NSH,))],
    )(*items[0], *items[1])


def rs_chips(arrs):
    n = len(arrs)

    def body(*refs):
        a_refs, out_refs = refs[:n], refs[n:2 * n]
        send_sems, recv_sems, local_sems = refs[2 * n:]
        x, y, c = _place()
        s = 2 * x + y
        chips = _other_chips(x, y)
        copy = functools.partial(_rcopy, send_sems, recv_sems)
        locals_ = [pltpu.make_async_copy(a_refs[i].at[s], out_refs[i].at[s], local_sems.at[i]) for i in range(n)]
        for cp in locals_:
            cp.start()
        sends = [copy(i * 3 + j, a_refs[i].at[2 * cx + cy], out_refs[i].at[s], (cx, cy, c))
                 for i in range(n) for j, (cx, cy) in enumerate(chips)]
        for cp in sends:
            cp.start()
        for i in range(n):
            for j, (cx, cy) in enumerate(chips):
                blk = out_refs[i].at[2 * cx + cy]
                copy(i * 3 + j, blk, blk, (x, y, c)).wait_recv()
        for cp in sends:
            cp.wait_send()
        for cp in locals_:
            cp.wait()

    return pl.pallas_call(
        body, name="rs_chips", in_specs=[ANY] * n, out_specs=[ANY] * n,
        out_shape=[jax.ShapeDtypeStruct(a.shape, a.dtype) for a in arrs],
        scratch_shapes=[pltpu.SemaphoreType.DMA((3 * n,)), pltpu.SemaphoreType.DMA((3 * n,)), pltpu.SemaphoreType.DMA((n,))],
    )(*arrs)


def ag_pair(arrs):
    n = len(arrs)

    def body(*refs):
        g_refs = refs[:n]
        out_refs = [refs[n:2 * n], refs[2 * n:3 * n]]
        send_sems, recv_sems, local_sems = refs[3 * n:]
        x, y, c = _place()
        copy = functools.partial(_rcopy, send_sems, recv_sems)
        for cc in range(2):
            @pl.when(c == cc)
            def _():
                keep = [pltpu.make_async_copy(g_refs[i], out_refs[cc][i], local_sems.at[i]) for i in range(n)]
                give = [copy(i, g_refs[i], out_refs[cc][i], (x, y, 1 - c)) for i in range(n)]
                for cp in keep + give:
                    cp.start()
                for i in range(n):
                    got = out_refs[1 - cc][i]
                    copy(i, got, got, (x, y, c)).wait_recv()
                for cp in give:
                    cp.wait_send()
                for cp in keep:
                    cp.wait()

    shapes = [jax.ShapeDtypeStruct(a.shape, a.dtype) for a in arrs]
    res = pl.pallas_call(
        body, name="ag_pair", in_specs=[ANY] * n, out_specs=[ANY] * (2 * n), out_shape=shapes + shapes,
        scratch_shapes=[pltpu.SemaphoreType.DMA((n,)), pltpu.SemaphoreType.DMA((n,)), pltpu.SemaphoreType.DMA((n,))],
    )(*arrs)
    return res[:n], res[n:]


def small_allreduce(p):
    def body(p_ref, out_ref, rbuf, send_sems, recv_sems):
        x, y, c = _place()
        me = 4 * x + 2 * y + c
        rbuf[me] = p_ref[2 * x + y]
        flip = lambda v, f: (1 - v) if f else v
        peers = [(flip(x, k >> 2 & 1), flip(y, k >> 1 & 1), flip(c, k & 1)) for k in range(1, 8)]

        def copy(k, src, dst, to):
            return pltpu.make_async_remote_copy(src_ref=src, dst_ref=dst, send_sem=send_sems.at[k], recv_sem=recv_sems.at[k],
                                                device_id=to, device_id_type=MESH)

        sends = [copy(k, p_ref.at[2 * px + py], rbuf.at[me], (px, py, pc)) for k, (px, py, pc) in enumerate(peers)]
        for cp in sends:
            cp.start()
        for k, (px, py, pc) in enumerate(peers):
            blk = rbuf.at[4 * px + 2 * py + pc]
            copy(k, blk, blk, (x, y, c)).wait_recv()
        for cp in sends:
            cp.wait_send()
        acc = rbuf[0]
        for d in range(1, 8):
            acc = acc + rbuf[d]
        out_ref[...] = acc

    return pl.pallas_call(
        body, name="small_allreduce", in_specs=[VMEM], out_specs=VMEM,
        out_shape=jax.ShapeDtypeStruct(p.shape[1:], F32),
        scratch_shapes=[pltpu.VMEM((8,) + p.shape[1:], F32), pltpu.SemaphoreType.DMA((7,)), pltpu.SemaphoreType.DMA((7,))],
    )(p)


def _rows_tile(rows, cap):
    return _pick(rows, cap) if rows % 128 == 0 else rows


def sum_pair(name, a0, a1, recv, cflag):
    n, r, d = recv.shape
    tr = _pick(r, 1024) if r % 64 == 0 else r

    def body(c_ref, a0_ref, a1_ref, b_ref, o_ref):
        own = jnp.where(c_ref[0] == 0, a0_ref[...].astype(F32), a1_ref[...].astype(F32))
        o_ref[...] = (own + b_ref[...].astype(F32)).astype(o_ref.dtype)

    spec = pl.BlockSpec((None, tr, d), lambda s, i: (s, i, 0))
    return pl.pallas_call(
        body, name=name, grid=(n, r // tr), in_specs=[pl.BlockSpec(memory_space=pltpu.SMEM), spec, spec, spec],
        out_specs=spec, out_shape=jax.ShapeDtypeStruct(recv.shape, BF16), compiler_params=_cparams(("parallel", "parallel")),
    )(cflag, a0, a1, recv)


def sum_chips(name, parts):
    n, r, d = parts.shape
    tr = _pick(r, 1024) if r % 64 == 0 else r

    def body(p_ref, o_ref):
        acc = p_ref[0].astype(F32)
        for s in range(1, n):
            acc = acc + p_ref[s].astype(F32)
        o_ref[...] = acc

    return pl.pallas_call(
        body, name=name, grid=(r // tr,), in_specs=[pl.BlockSpec((n, tr, d), lambda i: (0, i, 0))],
        out_specs=pl.BlockSpec((tr, d), lambda i: (i, 0)), out_shape=jax.ShapeDtypeStruct((r, d), F32),
        compiler_params=_cparams(("parallel",)),
    )(parts)


ADAM_LR, ADAM_B1, ADAM_B2, ADAM_EPS, ADAM_WD, ADAM_STEP = 0.001, 0.9, 0.999, 1e-08, 0.01, 10


def adamw_call(name, w, g, m, v):
    rows, cols = w.shape
    tr = _rows_tile(rows, 512)

    def body(w_ref, g_ref, m_ref, v_ref, d_ref, nm_ref, nv_ref):
        g_ = g_ref[...]
        m_ = ADAM_B1 * m_ref[...] + (1.0 - ADAM_B1) * g_
        v_ = ADAM_B2 * v_ref[...] + (1.0 - ADAM_B2) * (g_ * g_)
        m_hat = m_ / (1.0 - ADAM_B1 ** ADAM_STEP)
        v_hat = v_ / (1.0 - ADAM_B2 ** ADAM_STEP)
        d_ref[...] = -ADAM_LR * (m_hat / (jnp.sqrt(v_hat) + ADAM_EPS) + ADAM_WD * w_ref[...])
        nm_ref[...] = m_
        nv_ref[...] = v_

    spec = pl.BlockSpec((tr, cols), lambda i: (i, 0))
    sh = jax.ShapeDtypeStruct((rows, cols), F32)
    return pl.pallas_call(
        body, name=name, grid=(rows // tr,), in_specs=[spec] * 4, out_specs=[spec] * 3, out_shape=[sh] * 3,
        compiler_params=_cparams(("parallel",)),
    )(w, g, m, v)


def kernel(x, meta_tokens, norm_w, ffn_w_gate, ffn_w_up, ffn_w_down, rel_bias_table, even_w_in, even_conv_w, swa_sinks, dn_a_log, dn_dt_bias, dn_norm_w, even_w_out, odd_w_in, gla_w_gate_up, gla_b_gate, gla_norm_w, odd_w_out, loss_target, m_meta_tokens, m_norm_w, m_ffn_w_gate, m_ffn_w_up, m_ffn_w_down, m_rel_bias_table, m_even_w_in, m_even_conv_w, m_swa_sinks, m_dn_a_log, m_dn_dt_bias, m_dn_norm_w, m_even_w_out, m_odd_w_in, m_gla_w_gate_up, m_gla_b_gate, m_gla_norm_w, m_odd_w_out, v_meta_tokens, v_norm_w, v_ffn_w_gate, v_ffn_w_up, v_ffn_w_down, v_rel_bias_table, v_even_w_in, v_even_conv_w, v_swa_sinks, v_dn_a_log, v_dn_dt_bias, v_dn_norm_w, v_even_w_out, v_odd_w_in, v_gla_w_gate_up, v_gla_b_gate, v_gla_norm_w, v_odd_w_out):
    ws = [meta_tokens, norm_w, ffn_w_gate, ffn_w_up, ffn_w_down, rel_bias_table, even_w_in, even_conv_w, swa_sinks, dn_a_log,
          dn_dt_bias, dn_norm_w, even_w_out, odd_w_in, gla_w_gate_up, gla_b_gate, gla_norm_w, odd_w_out]
    ms = [m_meta_tokens, m_norm_w, m_ffn_w_gate, m_ffn_w_up, m_ffn_w_down, m_rel_bias_table, m_even_w_in, m_even_conv_w,
          m_swa_sinks, m_dn_a_log, m_dn_dt_bias, m_dn_norm_w, m_even_w_out, m_odd_w_in, m_gla_w_gate_up, m_gla_b_gate,
          m_gla_norm_w, m_odd_w_out]
    vs = [v_meta_tokens, v_norm_w, v_ffn_w_gate, v_ffn_w_up, v_ffn_w_down, v_rel_bias_table, v_even_w_in, v_even_conv_w,
          v_swa_sinks, v_dn_a_log, v_dn_dt_bias, v_dn_norm_w, v_even_w_out, v_odd_w_in, v_gla_w_gate_up, v_gla_b_gate,
          v_gla_norm_w, v_odd_w_out]
    short = [n for n, _ in NAMES]
    w = dict(zip(short, ws))
    m = dict(zip(short, ms))
    v = dict(zip(short, vs))

    big_names = ["ffn_g", "ffn_u", "ffn_d", "e_in", "e_out", "o_in", "o_out"]
    sq = lambda n, a: a if n.startswith("ffn") else a[0]
    gb = dict(zip(big_names, ag_big([sq(n, w[n]).astype(BF16) for n in big_names])))
    gs = ag_small(pack_small(w))
    W = make_W(full_from_gathered(gb, gs), {n: w[n] for n, _ in REPL})

    loss_blk, gx, G = core_step(x[0], loss_target[0], W)

    items = grad_items(G)
    cflag = lax.axis_index("c").astype(jnp.int32).reshape(1)
    recv = rs_pair(items)
    mine = [sum_pair(f"sum_pair_{i}", items[0][i], items[1][i], recv[i], cflag) for i in range(len(recv))]
    parts = rs_chips(mine)
    f0, f1 = ag_pair([sum_chips(f"sum_chips_{i}", p) for i, p in enumerate(parts)])
    g_small_pack = small_allreduce(pack_small_grads(G))
    g = {**shard_grads(f0, f1), **unpack_small(g_small_pack)}

    delta, new_m, new_v = {}, {}, {}
    for n, shp in BIG:
        two = lambda t: t.reshape(-1, shp[-1])
        d_, m_, v_ = adamw_call("adamw_" + n, two(w[n]), two(g[n]), two(m[n]), two(v[n]))
        delta[n], new_m[n], new_v[n] = d_.reshape(shp), m_.reshape(shp), v_.reshape(shp)
    d_, m_, v_ = adamw_call("adamw_small", pack_small(w), g_small_pack, pack_small(m), pack_small(v))
    delta.update(unpack_small(d_))
    new_m.update(unpack_small(m_))
    new_v.update(unpack_small(v_))

    loss = lax.psum(loss_blk[0, 0], ("x", "y", "c"))
    return (loss, gx[None], *[g[n] for n in short], *[delta[n] for n in short], *[new_m[n] for n in short],
            *[new_v[n] for n in short])
```

```python
import functools
import math

import numpy as np
import jax
import jax.numpy as jnp
from jax import lax
from jax.experimental import pallas as pl
from jax.experimental.pallas import tpu as pltpu

F32 = jnp.float32
BF16 = jnp.bfloat16
HI = lax.Precision.HIGHEST

D = 1024
N_META = 16
PADR = 128
ZROWS = PADR - N_META
D_FF = 2816
NSH = 4
FSH = D_FF // NSH
EPS = 1e-6
NEG = -1e30
CH = 64
BLK = 128
LANE = 128
VMEM_LIMIT = 56 * 1024 * 1024

E_QA, E_KA, E_VA, E_QB, E_KB, E_VB, E_ZB, E_BA, E_END = 0, 1024, 1280, 1536, 2048, 2560, 3072, 3584, 4096


def _even_in_map():
    m = np.full((E_END,), -1, np.int64)
    for h in range(8):
        m[E_QA + h * 128:E_QA + h * 128 + 64] = np.arange(h * 64, (h + 1) * 64)
    for h in range(2):
        m[E_KA + h * 128:E_KA + h * 128 + 64] = 512 + np.arange(h * 64, (h + 1) * 64)
        m[E_VA + h * 128:E_VA + h * 128 + 64] = 640 + np.arange(h * 64, (h + 1) * 64)
    m[E_QB:E_QB + 2048] = 768 + np.arange(2048)
    m[E_BA:E_BA + 8] = 2816 + np.arange(8)
    return m


def _even_out_map():
    m = np.full((1536,), -1, np.int64)
    for h in range(8):
        m[h * 128:h * 128 + 64] = np.arange(h * 64, (h + 1) * 64)
    m[1024:1536] = 512 + np.arange(512)
    return m


O_Q, O_K, O_V, O_G, O_GK, O_END = 0, 512, 1024, 2048, 3072, 3584


def _odd_in_map():
    m = np.full((O_END,), -1, np.int64)
    m[:3072] = np.arange(3072)
    m[O_GK:O_GK + 16] = 3072 + np.arange(16)
    return m


def _inverse(m, n):
    inv = np.zeros((n,), np.int64)
    for p, o in enumerate(m):
        if o >= 0:
            inv[o] = p
    return inv


def _take_pad(w, m, axis):
    t = jnp.take(w, jnp.asarray(np.maximum(m, 0)), axis=axis)
    shape = [1] * w.ndim
    shape[axis] = m.shape[0]
    return jnp.where(jnp.asarray(m >= 0).reshape(shape), t, jnp.zeros((), w.dtype))


def _mm(a, b, prec=HI):
    return lax.dot_general(a, b, (((1,), (0,)), ((), ())), precision=prec, preferred_element_type=F32)


def _mm_nt(a, b, prec=HI):
    return lax.dot_general(a, b, (((1,), (1,)), ((), ())), precision=prec, preferred_element_type=F32)


def _mm_tn(a, b, prec=HI):
    return lax.dot_general(a, b, (((0,), (0,)), ((), ())), precision=prec, preferred_element_type=F32)


def _bdot(a, b, dims):
    return lax.dot_general(a.astype(BF16), b.astype(BF16), (dims, ((), ())), preferred_element_type=F32)


@jax.custom_vjp
def _bmm(a, b):
    return _bdot(a, b, ((1,), (0,)))


@jax.custom_vjp
def _bmm_nt(a, b):
    return _bdot(a, b, ((1,), (1,)))


@jax.custom_vjp
def _bmm_tn(a, b):
    return _bdot(a, b, ((0,), (0,)))


_bmm.defvjp(lambda a, b: (_bmm(a, b), (a, b)), lambda r, g: (_bmm_nt(g, r[1]), _bmm_tn(r[0], g)))
_bmm_nt.defvjp(lambda a, b: (_bmm_nt(a, b), (a, b)), lambda r, g: (_bmm(g, r[1]), _bmm_tn(g, r[0])))
_bmm_tn.defvjp(lambda a, b: (_bmm_tn(a, b), (a, b)), lambda r, g: (_bmm_nt(r[1], g), _bmm(r[0], g)))


def _rms(x, w):
    return x * lax.rsqrt(jnp.mean(x * x, axis=-1, keepdims=True) + EPS) * w


def _sigmoid(x):
    return 1.0 / (1.0 + jnp.exp(-x))


def _silu(x):
    return x * _sigmoid(x)


def _softplus(x):
    return jnp.maximum(x, 0.0) + jnp.log(1.0 + jnp.exp(-jnp.abs(x)))


def _lane_pick(row, idx):
    lane = lax.broadcasted_iota(jnp.int32, row.shape, row.ndim - 1)
    return jnp.sum(jnp.where(lane == idx, row, 0.0), axis=-1, keepdims=True)


def _row_ids(row0, n):
    return row0 + lax.broadcasted_iota(jnp.int32, (n, 1), 0)


def _pick(m, cap):
    best = 64
    for t in range(64, min(m, cap) + 1, 64):
        if m % t == 0:
            best = t
    return best


def _cparams(sem):
    return pltpu.CompilerParams(dimension_semantics=sem, vmem_limit_bytes=VMEM_LIMIT)


def mm_nn(a, b, name, out_dtype=F32):
    M, K = a.shape
    N = b.shape[1]
    tm = _pick(M, 1408)
    tn = _pick(N, 512)

    def body(a_ref, b_ref, o_ref):
        o_ref[...] = _mm(a_ref[...], b_ref[...], None).astype(o_ref.dtype)

    return pl.pallas_call(
        body, name=name, grid=(N // tn, M // tm),
        in_specs=[pl.BlockSpec((tm, K), lambda j, i: (i, 0)), pl.BlockSpec((K, tn), lambda j, i: (0, j))],
        out_specs=pl.BlockSpec((tm, tn), lambda j, i: (i, j)),
        out_shape=jax.ShapeDtypeStruct((M, N), out_dtype),
        compiler_params=_cparams(("parallel", "parallel")),
    )(a, b)


def mm_nt(a, b, name, out_dtype=F32):
    M, K = a.shape
    N = b.shape[0]
    tm = _pick(M, 768)
    tn = _pick(N, 512)

    def body(a_ref, b_ref, o_ref):
        o_ref[...] = _mm_nt(a_ref[...], b_ref[...], None).astype(o_ref.dtype)

    return pl.pallas_call(
        body, name=name, grid=(N // tn, M // tm),
        in_specs=[pl.BlockSpec((tm, K), lambda j, i: (i, 0)), pl.BlockSpec((tn, K), lambda j, i: (j, 0))],
        out_specs=pl.BlockSpec((tm, tn), lambda j, i: (i, j)),
        out_shape=jax.ShapeDtypeStruct((M, N), out_dtype),
        compiler_params=_cparams(("parallel", "parallel")),
    )(a, b)


def mm_tn(a, b, name):
    M, K = a.shape
    N = b.shape[1]
    tk = _pick(K, 512)
    tn = _pick(N, 512)

    def body(a_ref, b_ref, o_ref):
        o_ref[...] = _mm_tn(a_ref[...], b_ref[...], None)

    return pl.pallas_call(
        body, name=name, grid=(K // tk, N // tn),
        in_specs=[pl.BlockSpec((M, tk), lambda i, j: (0, i)), pl.BlockSpec((M, tn), lambda i, j: (0, j))],
        out_specs=pl.BlockSpec((tk, tn), lambda i, j: (i, j)),
        out_shape=jax.ShapeDtypeStruct((K, N), F32),
        compiler_params=_cparams(("parallel", "parallel")),
    )(a, b)


def _row_specs(rows, tm):
    return [pl.BlockSpec((tm, w), functools.partial(lambda i, cb: (i, cb), cb=cb)) for (_, w, cb) in rows]


def _param_specs(params):
    return [pl.BlockSpec(p.shape, functools.partial(lambda i, nd: (0,) * nd, nd=p.ndim)) for p in params]


def rowwise_fwd(name, fn, rows, params, outs, tm=None):
    M = rows[0][0].shape[0]
    tm = tm or _pick(M, 704)
    nr, npar = len(rows), len(params)

    def body(*refs):
        row0 = pl.program_id(0) * tm
        vals = [r[...].astype(F32) for r in refs[:nr]] + [p[...] for p in refs[nr:nr + npar]]
        res = fn(row0, *vals)
        for o_ref, r in zip(refs[nr + npar:], res):
            o_ref[...] = r.astype(o_ref.dtype)

    return pl.pallas_call(
        body, name=name, grid=(M // tm,),
        in_specs=_row_specs(rows, tm) + _param_specs(params),
        out_specs=[pl.BlockSpec((tm, w), lambda i: (i, 0)) for (w, _) in outs],
        out_shape=[jax.ShapeDtypeStruct((M, w), dt) for (w, dt) in outs],
        compiler_params=_cparams(("parallel",)),
    )(*[r[0] for r in rows], *params)


def rowwise_bwd(name, fn, rows, params, douts, drow_dtypes, tm=None):
    M = rows[0][0].shape[0]
    tm = tm or _pick(M, 704)
    nr, npar, nd = len(rows), len(params), len(douts)
    want = [k for k, dt in enumerate(drow_dtypes) if dt is not None]

    def body(*refs):
        i = pl.program_id(0)
        row0 = i * tm
        vals = [r[...].astype(F32) for r in refs[:nr]] + [p[...] for p in refs[nr:nr + npar]]
        cots = tuple(d[...].astype(F32) for d in refs[nr + npar:nr + npar + nd])
        _, vjp = jax.vjp(functools.partial(fn, row0), *vals)
        grads = vjp(cots)
        o_refs = refs[nr + npar + nd:]
        for o_ref, k in zip(o_refs[:len(want)], want):
            o_ref[...] = grads[k].astype(o_ref.dtype)
        for o_ref, g in zip(o_refs[len(want):], grads[nr:]):
            @pl.when(i == 0)
            def _():
                o_ref[...] = g

            @pl.when(i > 0)
            def _():
                o_ref[...] += g

    res = pl.pallas_call(
        body, name=name, grid=(M // tm,),
        in_specs=_row_specs(rows, tm) + _param_specs(params) + _row_specs(douts, tm),
        out_specs=[pl.BlockSpec((tm, rows[k][1]), lambda i: (i, 0)) for k in want] + _param_specs(params),
        out_shape=[jax.ShapeDtypeStruct((M, rows[k][1]), drow_dtypes[k]) for k in want]
        + [jax.ShapeDtypeStruct(p.shape, F32) for p in params],
        compiler_params=_cparams(("arbitrary",)),
    )(*[r[0] for r in rows], *params, *[d[0] for d in douts])
    return res[:len(want)], res[len(want):]


def _fn_prenorm(row0, h, wpre):
    return (_rms(h, wpre),)


def _fn_resnorm(scale, row0, h, f, wpost, wpre):
    h2 = h + scale * _rms(f, wpost)
    return h2, _rms(h2, wpre)


def _fn_res_last(scale, row0, h, f, wpost):
    return (h + scale * _rms(f, wpost),)


def ffn_fwd(name, xn, wg, wu, wd, l, j):
    M = xn.shape[0]
    tm = _pick(M, 704)

    def body(x_ref, wg_ref, wu_ref, wd_ref, f_ref, a_ref, b_ref):
        s = pl.program_id(1)
        x = x_ref[...]
        a = _mm(x, wg_ref[...], None)
        b = _mm(x, wu_ref[...], None)
        hm = (_silu(a) * b).astype(BF16)
        c = _mm(hm, wd_ref[...], None)

        @pl.when(s == 0)
        def _():
            f_ref[...] = c

        @pl.when(s > 0)
        def _():
            f_ref[...] += c

        a_ref[...] = a.astype(BF16)
        b_ref[...] = b.astype(BF16)

    wspec = pl.BlockSpec((None, None, None, D, FSH), lambda i, s: (s, l, j, 0, 0))
    wdspec = pl.BlockSpec((None, None, None, FSH, D), lambda i, s: (s, l, j, 0, 0))
    abspec = pl.BlockSpec((None, tm, FSH), lambda i, s: (s, i, 0))
    return pl.pallas_call(
        body, name=name, grid=(M // tm, NSH),
        in_specs=[pl.BlockSpec((tm, D), lambda i, s: (i, 0)), wspec, wspec, wdspec],
        out_specs=[pl.BlockSpec((tm, D), lambda i, s: (i, 0)), abspec, abspec],
        out_shape=[jax.ShapeDtypeStruct((M, D), F32), jax.ShapeDtypeStruct((NSH, M, FSH), BF16),
                   jax.ShapeDtypeStruct((NSH, M, FSH), BF16)],
        compiler_params=_cparams(("parallel", "arbitrary")),
    )(xn, wg, wu, wd)


def ffn_bwd_x(name, df, a, b, wg, wu, wd, l, j):
    M = df.shape[0]
    tm = _pick(M, 704)

    def body(df_ref, a_ref, b_ref, wg_ref, wu_ref, wd_ref, dx_ref, da_ref, db_ref, hm_ref):
        s = pl.program_id(1)
        a_ = a_ref[...].astype(F32)
        b_ = b_ref[...].astype(F32)
        dh = _mm_nt(df_ref[...], wd_ref[...], None)
        sig = _sigmoid(a_)
        sil = a_ * sig
        da = (dh * b_ * (sig * (1.0 + a_ * (1.0 - sig)))).astype(BF16)
        db = (dh * sil).astype(BF16)
        c = _mm_nt(da, wg_ref[...], None) + _mm_nt(db, wu_ref[...], None)

        @pl.when(s == 0)
        def _():
            dx_ref[...] = c

        @pl.when(s > 0)
        def _():
            dx_ref[...] += c

        da_ref[...] = da
        db_ref[...] = db
        hm_ref[...] = (sil * b_).astype(BF16)

    wspec = pl.BlockSpec((None, None, None, D, FSH), lambda i, s: (s, l, j, 0, 0))
    wdspec = pl.BlockSpec((None, None, None, FSH, D), lambda i, s: (s, l, j, 0, 0))
    abspec = pl.BlockSpec((None, tm, FSH), lambda i, s: (s, i, 0))
    ab = jax.ShapeDtypeStruct((NSH, M, FSH), BF16)
    return pl.pallas_call(
        body, name=name, grid=(M // tm, NSH),
        in_specs=[pl.BlockSpec((tm, D), lambda i, s: (i, 0)), abspec, abspec, wspec, wspec, wdspec],
        out_specs=[pl.BlockSpec((tm, D), lambda i, s: (i, 0)), abspec, abspec, abspec],
        out_shape=[jax.ShapeDtypeStruct((M, D), F32), ab, ab, ab],
        compiler_params=_cparams(("parallel", "arbitrary")),
    )(df, a, b, wg, wu, wd)


def ffn_bwd_w(name, xn, df, da, db, hm):
    M = xn.shape[0]
    tm = _pick(M, 704)
    nt = M // tm

    def body(x_ref, df_ref, da_ref, db_ref, hm_ref, dwg_ref, dwu_ref, dwd_ref, ag, au, ad):
        i = pl.program_id(1)
        x = x_ref[...]
        g = _mm_tn(x, da_ref[...], None)
        u = _mm_tn(x, db_ref[...], None)
        d = _mm_tn(hm_ref[...], df_ref[...], None)

        @pl.when(i == 0)
        def _():
            ag[...] = g
            au[...] = u
            ad[...] = d

        @pl.when(i > 0)
        def _():
            ag[...] += g
            au[...] += u
            ad[...] += d

        @pl.when(i == nt - 1)
        def _():
            dwg_ref[...] = ag[...].astype(BF16)
            dwu_ref[...] = au[...].astype(BF16)
            dwd_ref[...] = ad[...].astype(BF16)

    xspec = pl.BlockSpec((tm, D), lambda s, i: (i, 0))
    abspec = pl.BlockSpec((None, tm, FSH), lambda s, i: (s, i, 0))
    return pl.pallas_call(
        body, name=name, grid=(NSH, nt),
        in_specs=[xspec, xspec, abspec, abspec, abspec],
        out_specs=[pl.BlockSpec((None, D, FSH), lambda s, i: (s, 0, 0)), pl.BlockSpec((None, D, FSH), lambda s, i: (s, 0, 0)),
                   pl.BlockSpec((None, FSH, D), lambda s, i: (s, 0, 0))],
        out_shape=[jax.ShapeDtypeStruct((NSH, D, FSH), BF16), jax.ShapeDtypeStruct((NSH, D, FSH), BF16),
                   jax.ShapeDtypeStruct((NSH, FSH, D), BF16)],
        scratch_shapes=[pltpu.VMEM((D, FSH), F32), pltpu.VMEM((D, FSH), F32), pltpu.VMEM((FSH, D), F32)],
        compiler_params=_cparams(("parallel", "arbitrary")),
    )(xn, df, da, db, hm)


def _t5_bucket_np(rel):
    n = np.maximum(rel, 0)
    n_f = np.maximum(n, 1).astype(np.float32)
    large = 16 + (np.log(n_f / np.float32(16)) / np.float32(math.log(8.0)) * np.float32(16)).astype(np.int32)
    large = np.minimum(large, 31)
    return np.where(n < 16, n, large).astype(np.int32)


def _swa_bucket_ids():
    qi = np.arange(BLK)[:, None]
    kj = np.arange(BLK)[None, :]
    out = np.full((3, BLK, 3 * BLK), -1, np.int32)
    for v in range(3):
        pos_q = v * BLK + qi - ZROWS
        rel_m = pos_q - (kj - ZROWS)
        ok_m = (kj >= ZROWS) & (rel_m >= 0) & (pos_q >= 0)
        out[v, :, 0:BLK] = np.where(ok_m, _t5_bucket_np(rel_m), -1)
        pos_kp = (v - 1) * BLK + kj - ZROWS
        rel_p = BLK + qi - kj
        ok_p = (pos_kp >= N_META) & (rel_p >= 0) & (rel_p < BLK) & np.full_like(ok_m, v >= 1)
        out[v, :, BLK:2 * BLK] = np.where(ok_p, _t5_bucket_np(rel_p), -1)
        pos_kc = v * BLK + kj - ZROWS
        rel_c = qi - kj
        ok_c = (pos_kc >= N_META) & (rel_c >= 0) & (rel_c < BLK)
        out[v, :, 2 * BLK:] = np.where(ok_c, _t5_bucket_np(rel_c), -1)
    return out


def swa_bias_fwd(table, ids):
    def body(t_ref, id_ref, o_ref):
        for v in range(3):
            for h in range(8):
                o_ref[v, h] = jnp.where(id_ref[v] < 0, NEG, 0.0)

            def step(b, carry):
                hit = id_ref[v] == b
                for h in range(8):
                    o_ref[v, h] += jnp.where(hit, t_ref[b, h], 0.0)
                return carry

            lax.fori_loop(0, 32, step, 0)

    return pl.pallas_call(
        body, name="swa_bias_fwd",
        in_specs=[pl.BlockSpec(memory_space=pltpu.SMEM), pl.BlockSpec(memory_space=pltpu.VMEM)],
        out_specs=pl.BlockSpec(memory_space=pltpu.VMEM),
        out_shape=jax.ShapeDtypeStruct((3, 8, BLK, 3 * BLK), F32),
        compiler_params=pltpu.CompilerParams(vmem_limit_bytes=VMEM_LIMIT),
    )(table, ids)


def swa_bias_bwd(dbias, ids):
    def body(d_ref, id_ref, o_ref):
        r = lax.broadcasted_iota(jnp.int32, (32, LANE), 0)
        c = lax.broadcasted_iota(jnp.int32, (32, LANE), 1)

        def step(b, acc):
            for v in range(3):
                hit = id_ref[v] == b
                for h in range(8):
                    m = jnp.where(hit, d_ref[v, h], 0.0)
                    s = jnp.sum(jnp.sum(m, axis=1, keepdims=True), axis=0, keepdims=True)
                    acc = acc + jnp.where((r == b) & (c == h), s, 0.0)
            return acc

        o_ref[...] = lax.fori_loop(0, 32, step, jnp.zeros((32, LANE), F32))

    return pl.pallas_call(
        body, name="swa_bias_bwd",
        in_specs=[pl.BlockSpec(memory_space=pltpu.VMEM), pl.BlockSpec(memory_space=pltpu.VMEM)],
        out_specs=pl.BlockSpec(memory_space=pltpu.VMEM),
        out_shape=jax.ShapeDtypeStruct((32, LANE), F32),
        compiler_params=pltpu.CompilerParams(vmem_limit_bytes=VMEM_LIMIT),
    )(dbias, ids)


def _swa_block(q, k3, v3, bias, sinks):
    outs = []
    for hk in range(2):
        kh = k3[:, hk * 128:(hk + 1) * 128]
        vh = v3[:, hk * 128:(hk + 1) * 128]
        for g in range(4):
            h = hk * 4 + g
            s = _bmm_nt(q[:, h * 128:(h + 1) * 128], kh) * 0.125 + bias[h]
            sink = _lane_pick(sinks, h)
            m = lax.stop_gradient(jnp.maximum(jnp.max(s, axis=-1, keepdims=True), sink))
            e = jnp.exp(s - m)
            den = jnp.sum(e, axis=-1, keepdims=True) + jnp.exp(sink - m)
            outs.append(_bmm(e / den, vh))
    return jnp.concatenate(outs, axis=1)


def _swa_in_specs():
    qs = pl.BlockSpec((BLK, 1024), lambda n: (n, E_QA // 1024))
    ks = [pl.BlockSpec((BLK, 256), lambda n: (0, E_KA // 256)),
          pl.BlockSpec((BLK, 256), lambda n: (jnp.maximum(n - 1, 0), E_KA // 256)),
          pl.BlockSpec((BLK, 256), lambda n: (n, E_KA // 256))]
    vs = [pl.BlockSpec((BLK, 256), lambda n: (0, E_VA // 256)),
          pl.BlockSpec((BLK, 256), lambda n: (jnp.maximum(n - 1, 0), E_VA // 256)),
          pl.BlockSpec((BLK, 256), lambda n: (n, E_VA // 256))]
    bs = pl.BlockSpec((None, 8, BLK, 3 * BLK), lambda n: (jnp.minimum(n, 2), 0, 0, 0))
    ss = pl.BlockSpec((1, LANE), lambda n: (0, 0))
    return [qs] + ks + vs + [bs, ss]


def swa_fwd(proj, bias, sinks):
    M = proj.shape[0]

    def body(q_ref, k0, k1, k2, v0, v1, v2, b_ref, s_ref, o_ref):
        k3 = jnp.concatenate([k0[...], k1[...], k2[...]], axis=0)
        v3 = jnp.concatenate([v0[...], v1[...], v2[...]], axis=0)
        o_ref[...] = _swa_block(q_ref[...], k3, v3, b_ref[...], s_ref[...]).astype(o_ref.dtype)

    return pl.pallas_call(
        body, name="swa_fwd", grid=(M // BLK,),
        in_specs=_swa_in_specs(),
        out_specs=pl.BlockSpec((BLK, 1024), lambda n: (n, 0)),
        out_shape=jax.ShapeDtypeStruct((M, 1024), BF16),
        compiler_params=_cparams(("parallel",)),
    )(proj, proj, proj, proj, proj, proj, proj, bias, sinks)


def swa_bwd(proj, bias, sinks, do):
    M = proj.shape[0]

    def body(q_ref, k0, k1, k2, v0, v1, v2, b_ref, s_ref, do_ref, dq_ref, dk_ref, dv_ref, db_ref, ds_ref):
        n = pl.program_id(0)

        @pl.when(n == 0)
        def _():
            dk_ref[...] = jnp.zeros_like(dk_ref)
            dv_ref[...] = jnp.zeros_like(dv_ref)
            ds_ref[...] = jnp.zeros_like(ds_ref)

        k3 = jnp.concatenate([k0[...], k1[...], k2[...]], axis=0)
        v3 = jnp.concatenate([v0[...], v1[...], v2[...]], axis=0)
        _, vjp = jax.vjp(_swa_block, q_ref[...], k3, v3, b_ref[...], s_ref[...])
        dq, dk3, dv3, dbias, dsink = vjp(do_ref[...].astype(F32))
        dq_ref[...] = dq
        prev = pl.multiple_of(jnp.maximum(n - 1, 0) * BLK, BLK)
        cur = pl.multiple_of(n * BLK, BLK)
        dk_ref[pl.ds(0, BLK), :] += dk3[0:BLK]
        dv_ref[pl.ds(0, BLK), :] += dv3[0:BLK]
        dk_ref[pl.ds(prev, BLK), :] += dk3[BLK:2 * BLK]
        dv_ref[pl.ds(prev, BLK), :] += dv3[BLK:2 * BLK]
        dk_ref[pl.ds(cur, BLK), :] += dk3[2 * BLK:]
        dv_ref[pl.ds(cur, BLK), :] += dv3[2 * BLK:]
        ds_ref[...] += dsink

        @pl.when(n <= 2)
        def _():
            db_ref[...] = dbias

        @pl.when(n > 2)
        def _():
            db_ref[...] += dbias

    return pl.pallas_call(
        body, name="swa_bwd", grid=(M // BLK,),
        in_specs=_swa_in_specs() + [pl.BlockSpec((BLK, 1024), lambda n: (n, 0))],
        out_specs=[pl.BlockSpec((BLK, 1024), lambda n: (n, 0)),
                   pl.BlockSpec((M, 256), lambda n: (0, 0)), pl.BlockSpec((M, 256), lambda n: (0, 0)),
                   pl.BlockSpec((None, 8, BLK, 3 * BLK), lambda n: (jnp.minimum(n, 2), 0, 0, 0)),
                   pl.BlockSpec((1, LANE), lambda n: (0, 0))],
        out_shape=[jax.ShapeDtypeStruct((M, 1024), F32), jax.ShapeDtypeStruct((M, 256), F32),
                   jax.ShapeDtypeStruct((M, 256), F32), jax.ShapeDtypeStruct((3, 8, BLK, 3 * BLK), F32),
                   jax.ShapeDtypeStruct((1, LANE), F32)],
        compiler_params=_cparams(("arbitrary",)),
    )(proj, proj, proj, proj, proj, proj, proj, bias, sinks, do)


def _shift_rows_impl(x, k):
    n = x.shape[0]
    rolled = pltpu.roll(x, k, 0)
    return jnp.where(_row_ids(0, n) >= k, rolled, 0.0)


def _unshift_rows_impl(g, k):
    n = g.shape[0]
    rolled = pltpu.roll(g, n - k, 0)
    return jnp.where(_row_ids(0, n) < n - k, rolled, 0.0)


@functools.partial(jax.custom_vjp, nondiff_argnums=(1,))
def _shift_rows(x, k):
    return _shift_rows_impl(x, k)


def _shift_rows_f(x, k):
    return _shift_rows_impl(x, k), None


def _shift_rows_b(k, _, g):
    return (_unshift_rows_impl(g, k),)


_shift_rows.defvjp(_shift_rows_f, _shift_rows_b)


def _conv_silu(x, w):
    rid = lax.broadcasted_iota(jnp.int32, w.shape, 0)
    y = x * jnp.sum(jnp.where(rid == 3, w, 0.0), axis=0, keepdims=True)
    for k in range(1, 4):
        y = y + _shift_rows(x, k) * jnp.sum(jnp.where(rid == 3 - k, w, 0.0), axis=0, keepdims=True)
    y = jnp.where(_row_ids(0, x.shape[0]) >= ZROWS, y, 0.0)
    return _silu(y)


def conv_fwd(proj, conv_w):
    M = proj.shape[0]
    nb = conv_w.shape[1] // LANE

    def body(x_ref, w_ref, o_ref):
        o_ref[...] = _conv_silu(x_ref[...], w_ref[...])

    return pl.pallas_call(
        body, name="conv_fwd", grid=(nb,),
        in_specs=[pl.BlockSpec((M, LANE), lambda c: (0, E_QB // LANE + c)), pl.BlockSpec((4, LANE), lambda c: (0, c))],
        out_specs=pl.BlockSpec((M, LANE), lambda c: (0, c)),
        out_shape=jax.ShapeDtypeStruct((M, conv_w.shape[1]), F32),
        compiler_params=_cparams(("parallel",)),
    )(proj, conv_w)


def conv_bwd(proj, conv_w, dy):
    M = proj.shape[0]
    nb = conv_w.shape[1] // LANE

    def body(x_ref, w_ref, dy_ref, dx_ref, dw_ref):
        _, vjp = jax.vjp(_conv_silu, x_ref[...], w_ref[...])
        dx, dw = vjp(dy_ref[...])
        dx_ref[...] = dx
        dw_ref[...] = dw

    return pl.pallas_call(
        body, name="conv_bwd", grid=(nb,),
        in_specs=[pl.BlockSpec((M, LANE), lambda c: (0, E_QB // LANE + c)), pl.BlockSpec((4, LANE), lambda c: (0, c)),
                  pl.BlockSpec((M, LANE), lambda c: (0, c))],
        out_specs=[pl.BlockSpec((M, LANE), lambda c: (0, c)), pl.BlockSpec((4, LANE), lambda c: (0, c))],
        out_shape=[jax.ShapeDtypeStruct((M, conv_w.shape[1]), F32), jax.ShapeDtypeStruct(conv_w.shape, F32)],
        compiler_params=_cparams(("parallel",)),
    )(proj, conv_w, dy)


def _fn_dn_prep(row0, yq, yk, ba, dnp):
    tm = yq.shape[0]
    real = _row_ids(row0, tm) >= ZROWS
    qs, ks, gs, bs = [], [], [], []
    for h in range(4):
        q = yq[:, h * 128:(h + 1) * 128]
        k = yk[:, h * 128:(h + 1) * 128]
        qs.append(q * lax.rsqrt(jnp.sum(q * q, axis=-1, keepdims=True) + 1e-6) * (128.0 ** -0.5))
        ks.append(k * lax.rsqrt(jnp.sum(k * k, axis=-1, keepdims=True) + 1e-6))
        beta = _sigmoid(_lane_pick(ba, h))
        g = -jnp.exp(_lane_pick(dnp, h)) * _softplus(_lane_pick(ba, 4 + h) + _lane_pick(dnp, 4 + h))
        g = jnp.where(real, g, 0.0)
        gs.append(jnp.broadcast_to(g, (tm, 128)))
        bs.append(jnp.broadcast_to(beta, (tm, 128)))
    cat = lambda xs: jnp.concatenate(xs, axis=1)
    return cat(qs), cat(ks), cat(gs), cat(bs)


def _unit_lower_inv_impl(a):
    n = a.shape[0]
    r = lax.broadcasted_iota(jnp.int32, (n, n), 0)
    c = lax.broadcasted_iota(jnp.int32, (n, n), 1)
    p = -a
    t = (r == c).astype(F32) + p
    for _ in range(int(math.log2(n)) - 1):
        p = _mm(p, p)
        t = t + _mm(t, p)
    return t


@jax.custom_vjp
def _unit_lower_inv(a):
    return _unit_lower_inv_impl(a)


def _unit_lower_inv_f(a):
    t = _unit_lower_inv_impl(a)
    return t, t


def _unit_lower_inv_b(t, g):
    return (-_mm_nt(_mm_tn(t, g), t),)


_unit_lower_inv.defvjp(_unit_lower_inv_f, _unit_lower_inv_b)


def _dn_chunk(q, k, v, gb, bb, S):
    r = lax.broadcasted_iota(jnp.int32, (CH, CH), 0)
    c = lax.broadcasted_iota(jnp.int32, (CH, CH), 1)
    tri_incl = r >= c
    gcb = _mm(tri_incl.astype(F32), gb)
    g1 = gcb[:, :CH]
    diff = g1 - g1.T
    gamma = jnp.where(tri_incl, jnp.exp(jnp.where(tri_incl, diff, 0.0)), 0.0)
    kb = k * bb
    vb = v * bb
    t = _unit_lower_inv(jnp.where(r > c, _bmm_nt(kb, k) * gamma, 0.0))
    eg = jnp.exp(gcb)
    u = _mm(t, vb)
    w = _mm(t, kb * eg)
    attn = _bmm_nt(q, k) * gamma
    gtot = jnp.sum(gb, axis=0, keepdims=True)
    k_dec = k * jnp.exp(gtot - gcb)
    g_last = jnp.exp(jnp.broadcast_to(gtot, S.shape))
    v_new = u - _bmm(w, S)
    o = _bmm(q * eg, S) + _bmm(attn, v_new)
    return o, S * g_last + _bmm_tn(k_dec, v_new)


def _gla_chunk(q, k, v, glog, S):
    r = lax.broadcasted_iota(jnp.int32, (CH, CH), 0)
    c = lax.broadcasted_iota(jnp.int32, (CH, CH), 1)
    tri = r >= c
    bcum = _mm(tri.astype(F32), glog)
    q_dec = q * (128.0 ** -0.5) * jnp.exp(bcum)
    attn = jnp.where(tri, _bmm_nt(q_dec, k * jnp.exp(-bcum)), 0.0)
    o = _bmm(attn, v) + _bmm(q_dec, S)
    k_dec = k * jnp.exp(jnp.sum(glog, axis=0, keepdims=True) - bcum)
    decay = jnp.exp(_mm_tn(glog, jnp.ones((CH, v.shape[1]), F32)))
    return o, S * decay + _bmm_tn(k_dec, v)


def chunk_fwd(name, chunk_fn, ins, dv):
    M = ins[0][0].shape[0]
    N = M // CH
    ni = len(ins)
    ws = [w for (_, w, _) in ins]

    def body(*refs):
        o_ref, sall_ref, s_ref = refs[ni:]

        @pl.when(pl.program_id(0) == 0)
        def _():
            s_ref[...] = jnp.zeros_like(s_ref)

        for h in range(4):
            S = s_ref[h]
            sall_ref[h] = S
            o, s_new = chunk_fn(*[r[:, h * w:(h + 1) * w] for r, w in zip(refs[:ni], ws)], S)
            o_ref[:, h * dv:(h + 1) * dv] = o
            s_ref[h] = s_new

    specs = [pl.BlockSpec((CH, 4 * w), functools.partial(lambda n, cb: (n, cb), cb=cb // 4)) for (_, w, cb) in ins]
    return pl.pallas_call(
        body, name=name, grid=(N,),
        in_specs=specs,
        out_specs=[pl.BlockSpec((CH, 4 * dv), lambda n: (n, 0)),
                   pl.BlockSpec((4, None, 128, dv), lambda n: (0, n, 0, 0))],
        out_shape=[jax.ShapeDtypeStruct((M, 4 * dv), F32), jax.ShapeDtypeStruct((4, N, 128, dv), F32)],
        scratch_shapes=[pltpu.VMEM((4, 128, dv), F32)],
        compiler_params=_cparams(("arbitrary",)),
    )(*[a for (a, _, _) in ins])


def chunk_bwd(name, chunk_fn, ins, dv, s_all, do):
    M = ins[0][0].shape[0]
    N = M // CH
    ni = len(ins)
    ws = [w for (_, w, _) in ins]

    def body(*refs):
        sall_ref, do_ref = refs[ni:ni + 2]
        d_refs = refs[ni + 2:2 * ni + 2]
        ds_ref = refs[2 * ni + 2]

        @pl.when(pl.program_id(0) == 0)
        def _():
            ds_ref[...] = jnp.zeros_like(ds_ref)

        for h in range(4):
            _, vjp = jax.vjp(chunk_fn, *[r[:, h * w:(h + 1) * w] for r, w in zip(refs[:ni], ws)], sall_ref[h])
            grads = vjp((do_ref[:, h * dv:(h + 1) * dv], ds_ref[h]))
            for d_ref, w, g in zip(d_refs, ws, grads[:ni]):
                d_ref[:, h * w:(h + 1) * w] = g
            ds_ref[h] = grads[ni]

    rev = lambda n: N - 1 - n
    specs = [pl.BlockSpec((CH, 4 * w), functools.partial(lambda n, cb: (rev(n), cb), cb=cb // 4)) for (_, w, cb) in ins]
    return pl.pallas_call(
        body, name=name, grid=(N,),
        in_specs=specs + [pl.BlockSpec((4, None, 128, dv), lambda n: (0, rev(n), 0, 0)),
                          pl.BlockSpec((CH, 4 * dv), lambda n: (rev(n), 0))],
        out_specs=[pl.BlockSpec((CH, 4 * w), lambda n: (rev(n), 0)) for w in ws],
        out_shape=[jax.ShapeDtypeStruct((M, 4 * w), F32) for w in ws],
        scratch_shapes=[pltpu.VMEM((4, 128, dv), F32)],
        compiler_params=_cparams(("arbitrary",)),
    )(*[a for (a, _, _) in ins], s_all, do)


def _fn_gate_out(hd, row0, o, z, w):
    outs = []
    for h in range(4):
        outs.append(_rms(o[:, h * hd:(h + 1) * hd], w) * _silu(z[:, h * hd:(h + 1) * hd]))
    return (jnp.concatenate(outs, axis=1),)


def _fn_gla_prep(row0, gk, wgu, bg):
    x = _mm(gk, wgu) + bg
    ls = jnp.minimum(x, 0.0) - jnp.log(1.0 + jnp.exp(-jnp.abs(x)))
    return (jnp.where(_row_ids(row0, gk.shape[0]) >= ZROWS, ls / 16.0, 0.0),)


def loss_call(y, tgt):
    M = y.shape[0]
    tm = _pick(M, 512)

    def body(y_ref, t_ref, l_ref, dy_ref):
        i = pl.program_id(0)
        e = y_ref[...] - t_ref[...]
        dy_ref[...] = e * (1.0 / D)
        part = 0.5 * jnp.sum(jnp.sum(e * e, axis=1, keepdims=True) * (1.0 / D), axis=0, keepdims=True)
        part = jnp.broadcast_to(part, (8, LANE))

        @pl.when(i == 0)
        def _():
            l_ref[...] = part

        @pl.when(i > 0)
        def _():
            l_ref[...] += part

    return pl.pallas_call(
        body, name="loss", grid=(M // tm,),
        in_specs=[pl.BlockSpec((tm, D), lambda i: (i, 0))] * 2,
        out_specs=[pl.BlockSpec((8, LANE), lambda i: (0, 0)), pl.BlockSpec((tm, D), lambda i: (i, 0))],
        out_shape=[jax.ShapeDtypeStruct((8, LANE), F32), jax.ShapeDtypeStruct((M, D), F32)],
        compiler_params=_cparams(("arbitrary",)),
    )(y, tgt)


def _bf(x):
    return x.astype(BF16)


def core_step(x, tgt, W):
    S = x.shape[0]
    M = S + PADR
    ids = jnp.asarray(_swa_bucket_ids())
    h0 = jnp.concatenate([jnp.zeros((ZROWS, D), F32), W["meta"], x], axis=0)
    nw = W["norm"]
    nrow = lambda l, k: nw[l, k][None, :]
    wg, wu, wd = W["ffn_g"], W["ffn_u"], W["ffn_d"]
    e_in, e_out = W["e_in"], W["e_out"]
    o_in, o_out = W["o_in"], W["o_out"]
    sinks = jnp.pad(W["sinks"], ((0, 0), (0, LANE - 8)))
    dnp = jnp.pad(jnp.concatenate([W["a_log"], W["dt_bias"]], axis=1), ((0, 0), (0, LANE - 8)))
    wgu = jnp.pad(W["gate_up"], ((0, LANE - 16), (0, 0)))
    bg = W["b_gate"]
    full = lambda a: (a, a.shape[1], 0)

    saved = []
    h = h0
    (hn,) = rowwise_fwd("prenorm_0", _fn_prenorm, [full(h)], [nrow(0, 0)], [(D, BF16)])
    bias = swa_bias_fwd(W["rel"], ids)
    for l in range(2):
        st = {"h_a": h, "hn_a": hn}
        f1, a1, b1 = ffn_fwd(f"ffn_fwd_{l}0", hn, wg, wu, wd, l, 0)
        h, hn = rowwise_fwd(f"resnorm_{l}1", functools.partial(_fn_resnorm, 0.5), [full(h), full(f1)],
                            [nrow(l, 1), nrow(l, 2)], [(D, F32), (D, BF16)])
        st.update(f1=f1, a1=a1, b1=b1, h_b=h, hn_b=hn)
        if l == 0:
            proj = mm_nn(hn, e_in, "e_proj")
            o_a = swa_fwd(proj, bias, sinks)
            y = conv_fwd(proj, W["conv"])
            qn, kn, gb, bb = rowwise_fwd(
                "dn_prep", _fn_dn_prep, [(y, 512, 0), (y, 512, 1), (proj, LANE, E_BA // LANE)], [dnp], [(512, F32)] * 4)
            ins = [(qn, 128, 0), (kn, 128, 0), (y, 128, 8), (gb, 128, 0), (bb, 128, 0)]
            o_dn, s_all = chunk_fwd("dn_fwd", _dn_chunk, ins, 128)
            (o_b,) = rowwise_fwd("dn_out", functools.partial(_fn_gate_out, 128),
                                 [full(o_dn), (proj, 512, E_ZB // 512)], [W["dn_norm"]], [(512, BF16)])
            omix = jnp.concatenate([o_a, o_b], axis=1)
            mix = mm_nn(omix, e_out, "e_mix")
            st.update(proj=proj, y=y, qn=qn, kn=kn, gb=gb, bb=bb, o_dn=o_dn, s_all=s_all, omix=omix)
        else:
            proj = mm_nn(hn, o_in, "o_proj")
            (glog,) = rowwise_fwd("gla_prep", _fn_gla_prep, [(proj, LANE, O_GK // LANE)], [wgu, bg], [(512, F32)])
            ins = [(proj, 128, O_Q // 128), (proj, 128, O_K // 128), (proj, 256, O_V // 256), (glog, 128, 0)]
            o_g, s_all = chunk_fwd("gla_fwd", _gla_chunk, ins, 256)
            (omix,) = rowwise_fwd("gla_out", functools.partial(_fn_gate_out, 256),
                                  [full(o_g), (proj, 1024, O_G // 1024)], [W["gla_norm"]], [(1024, BF16)])
            mix = mm_nn(omix, o_out, "o_mix")
            st.update(proj=proj, glog=glog, o_g=o_g, s_all=s_all, omix=omix)
        h, hn = rowwise_fwd(f"resnorm_{l}3", functools.partial(_fn_resnorm, 1.0), [full(h), full(mix)],
                            [nrow(l, 3), nrow(l, 4)], [(D, F32), (D, BF16)])
        st.update(mix=mix, h_c=h, hn_c=hn)
        f2, a2, b2 = ffn_fwd(f"ffn_fwd_{l}1", hn, wg, wu, wd, l, 1)
        st.update(f2=f2, a2=a2, b2=b2)
        if l == 0:
            h, hn = rowwise_fwd("resnorm_05", functools.partial(_fn_resnorm, 0.5), [full(h), full(f2)],
                                [nrow(0, 5), nrow(1, 0)], [(D, F32), (D, BF16)])
        else:
            (h,) = rowwise_fwd("res_last", functools.partial(_fn_res_last, 0.5), [full(h), full(f2)],
                               [nrow(1, 5)], [(D, F32)])
        saved.append(st)

    loss_blk, dy = loss_call(h[PADR:], tgt)
    dh = jnp.concatenate([jnp.zeros((PADR, D), F32), dy], axis=0)

    G = {}
    dnorm = [[None] * 6 for _ in range(2)]
    dWg = [[None, None], [None, None]]
    dWu = [[None, None], [None, None]]
    dWd = [[None, None], [None, None]]
    dhn = None
    for l in (1, 0):
        st = saved[l]
        if l == 1:
            (dh_, df), (dw5,) = rowwise_bwd(
                "res_last_b", functools.partial(_fn_res_last, 0.5), [full(st["h_c"]), full(st["f2"])], [nrow(1, 5)],
                [full(dh)], [F32, BF16])
            dnorm[1][5] = dw5
        else:
            (dh_, df), (dw5, dw0n) = rowwise_bwd(
                "resnorm_05_b", functools.partial(_fn_resnorm, 0.5), [full(st["h_c"]), full(st["f2"])],
                [nrow(0, 5), nrow(1, 0)], [full(dh), full(dhn)], [F32, BF16])
            dnorm[0][5] = dw5
            dnorm[1][0] = dw0n
        dh = dh_
        dxn, da, db, hm = ffn_bwd_x(f"ffn_bx_{l}1", df, st["a2"], st["b2"], wg, wu, wd, l, 1)
        dWg[l][1], dWu[l][1], dWd[l][1] = ffn_bwd_w(f"ffn_bw_{l}1", st["hn_c"], df, da, db, hm)
        (dh_, dmix), (dw3, dw4) = rowwise_bwd(
            f"resnorm_{l}3_b", functools.partial(_fn_resnorm, 1.0), [full(st["h_b"]), full(st["mix"])],
            [nrow(l, 3), nrow(l, 4)], [full(dh), full(dxn)], [F32, BF16])
        dnorm[l][3], dnorm[l][4] = dw3, dw4
        dh = dh_
        proj = st["proj"]
        if l == 1:
            G["o_out"] = mm_tn(st["omix"], dmix, "o_out_dw")
            domix = mm_nt(dmix, o_out, "o_mix_dx")
            (do_g, dgate), (dgn,) = rowwise_bwd(
                "gla_out_b", functools.partial(_fn_gate_out, 256), [full(st["o_g"]), (proj, 1024, O_G // 1024)],
                [W["gla_norm"]], [full(domix)], [F32, F32])
            G["gla_norm"] = dgn
            ins = [(proj, 128, O_Q // 128), (proj, 128, O_K // 128), (proj, 256, O_V // 256), (st["glog"], 128, 0)]
            dq, dk, dv, dglog = chunk_bwd("gla_bwd", _gla_chunk, ins, 256, st["s_all"], do_g)
            (dgk,), (dwgu, dbg) = rowwise_bwd("gla_prep_b", _fn_gla_prep, [(proj, LANE, O_GK // LANE)], [wgu, bg],
                                              [full(dglog)], [F32])
            G["gate_up"] = dwgu[:16]
            G["b_gate"] = dbg
            dproj = _bf(jnp.concatenate([dq, dk, dv, dgate, dgk, jnp.zeros((M, O_END - O_GK - LANE), F32)], axis=1))
            G["o_in"] = mm_tn(st["hn_b"], dproj, "o_in_dw")
            dhn_b = mm_nt(dproj, o_in, "o_proj_dx")
        else:
            G["e_out"] = mm_tn(st["omix"], dmix, "e_out_dw")
            domix = mm_nt(dmix, e_out, "e_mix_dx")
            (do_dn, dz), (ddn,) = rowwise_bwd(
                "dn_out_b", functools.partial(_fn_gate_out, 128), [full(st["o_dn"]), (proj, 512, E_ZB // 512)],
                [W["dn_norm"]], [(domix, 512, 2)], [F32, F32])
            G["dn_norm"] = ddn
            ins = [(st["qn"], 128, 0), (st["kn"], 128, 0), (st["y"], 128, 8), (st["gb"], 128, 0), (st["bb"], 128, 0)]
            dqn, dkn, dvv, dgb, dbb = chunk_bwd("dn_bwd", _dn_chunk, ins, 128, st["s_all"], do_dn)
            (dyq, dyk, dba), (ddnp,) = rowwise_bwd(
                "dn_prep_b", _fn_dn_prep, [(st["y"], 512, 0), (st["y"], 512, 1), (proj, LANE, E_BA // LANE)], [dnp],
                [full(dqn), full(dkn), full(dgb), full(dbb)], [F32, F32, F32])
            G["a_log"] = ddnp[:, 0:4]
            G["dt_bias"] = ddnp[:, 4:8]
            dyc = jnp.concatenate([dyq, dyk, dvv], axis=1)
            dxc, dconv = conv_bwd(proj, W["conv"], dyc)
            G["conv"] = dconv
            dq_a, dk_a, dv_a, dbias, dsink = swa_bwd(proj, bias, sinks, domix)
            G["sinks"] = dsink[:, :8]
            G["rel"] = swa_bias_bwd(dbias, ids)[:, :8]
            dproj = _bf(jnp.concatenate([dq_a, dk_a, dv_a, dxc, dz, dba, jnp.zeros((M, E_END - E_BA - LANE), F32)], axis=1))
            G["e_in"] = mm_tn(st["hn_b"], dproj, "e_in_dw")
            dhn_b = mm_nt(dproj, e_in, "e_proj_dx")
        (dh_, df), (dw1, dw2) = rowwise_bwd(
            f"resnorm_{l}1_b", functools.partial(_fn_resnorm, 0.5), [full(st["h_a"]), full(st["f1"])],
            [nrow(l, 1), nrow(l, 2)], [full(dh), full(dhn_b)], [F32, BF16])
        dnorm[l][1], dnorm[l][2] = dw1, dw2
        dh = dh_
        dxn, da, db, hm = ffn_bwd_x(f"ffn_bx_{l}0", df, st["a1"], st["b1"], wg, wu, wd, l, 0)
        dWg[l][0], dWu[l][0], dWd[l][0] = ffn_bwd_w(f"ffn_bw_{l}0", st["hn_a"], df, da, db, hm)
        dhn = dxn
    (dh0p,), (dw00,) = rowwise_bwd("prenorm_0_b", _fn_prenorm, [full(saved[0]["h_a"])], [nrow(0, 0)], [full(dhn)], [F32])
    dnorm[0][0] = dw00
    dh = dh + dh0p
    G["meta"] = dh[ZROWS:PADR]
    G["norm"] = jnp.stack([jnp.concatenate(r, axis=0) for r in dnorm], axis=0)
    G["ffn_g"], G["ffn_u"], G["ffn_d"] = dWg, dWu, dWd
    return loss_blk, dh[PADR:], G


NAMES = [("meta", "meta_tokens"), ("norm", "norm_w"), ("ffn_g", "ffn_w_gate"), ("ffn_u", "ffn_w_up"),
         ("ffn_d", "ffn_w_down"), ("rel", "rel_bias_table"), ("e_in", "even_w_in"), ("conv", "even_conv_w"),
         ("sinks", "swa_sinks"), ("a_log", "dn_a_log"), ("dt_bias", "dn_dt_bias"), ("dn_norm", "dn_norm_w"),
         ("e_out", "even_w_out"), ("o_in", "odd_w_in"), ("gate_up", "gla_w_gate_up"), ("b_gate", "gla_b_gate"),
         ("gla_norm", "gla_norm_w"), ("o_out", "odd_w_out")]
BIG = [("ffn_g", (2, 2, D, FSH)), ("ffn_u", (2, 2, D, FSH)), ("ffn_d", (2, 2, FSH, D)), ("e_in", (1, D, 706)),
       ("e_out", (1, 256, D)), ("o_in", (1, D, 772)), ("o_out", (1, 256, D))]
IN_COLS = 772
SMALL = [("meta", (16, 256)), ("norm", (2, 6, 256)), ("conv", (1, 4, 384)), ("gate_up", (1, 16, 128)),
         ("b_gate", (1, 128)), ("gla_norm", (1, 64))]
REPL = [("rel", (32, 8)), ("sinks", (1, 8)), ("a_log", (1, 4)), ("dt_bias", (1, 4)), ("dn_norm", (1, 128))]
SMALL_REP = 88 * LANE
SMALL_ROWS = 96


def pack_small(t):
    a = jnp.concatenate([t[n].reshape(-1) for n, _ in SMALL])
    b = jnp.concatenate([t[n].reshape(-1) for n, _ in REPL])
    flat = jnp.concatenate([a, jnp.zeros((SMALL_REP - a.shape[0],), F32), b,
                            jnp.zeros((SMALL_ROWS * LANE - SMALL_REP - b.shape[0],), F32)])
    return flat.reshape(SMALL_ROWS, LANE)


def unpack_small(p):
    flat = p.reshape(-1)
    out, r = {}, 0
    for n, shp in SMALL:
        k = int(np.prod(shp))
        out[n] = flat[r:r + k].reshape(shp)
        r += k
    r = SMALL_REP
    for n, shp in REPL:
        k = int(np.prod(shp))
        out[n] = flat[r:r + k].reshape(shp)
        r += k
    return out


def full_from_gathered(gb, gs):
    sm = [unpack_small(gs[s]) for s in range(NSH)]
    full = {n: gb[n] for n in ("ffn_g", "ffn_u", "ffn_d")}
    full["e_in"] = jnp.moveaxis(gb["e_in"], 0, 1).reshape(D, -1)
    full["o_in"] = jnp.moveaxis(gb["o_in"], 0, 1).reshape(D, -1)
    full["e_out"] = gb["e_out"].reshape(-1, D)
    full["o_out"] = gb["o_out"].reshape(-1, D)
    full["meta"] = jnp.concatenate([sm[s]["meta"] for s in range(NSH)], axis=1)
    full["norm"] = jnp.concatenate([sm[s]["norm"] for s in range(NSH)], axis=2)
    full["conv"] = jnp.concatenate([sm[s]["conv"][0] for s in range(NSH)], axis=1)
    full["gate_up"] = jnp.concatenate([sm[s]["gate_up"][0] for s in range(NSH)], axis=1)
    full["b_gate"] = jnp.concatenate([sm[s]["b_gate"] for s in range(NSH)], axis=1)
    full["gla_norm"] = jnp.concatenate([sm[s]["gla_norm"] for s in range(NSH)], axis=1)
    return full


def make_W(full, rep):
    W = dict(full)
    W.update(rep)
    W["e_in"] = _take_pad(full["e_in"], _even_in_map(), 1).astype(BF16)
    W["e_out"] = _take_pad(full["e_out"], _even_out_map(), 0).astype(BF16)
    W["o_in"] = _take_pad(full["o_in"], _odd_in_map(), 1).astype(BF16)
    W["o_out"] = full["o_out"].astype(BF16)
    for n in ("ffn_g", "ffn_u", "ffn_d"):
        W[n] = full[n].astype(BF16)
    for n in ("meta", "norm", "conv", "gate_up", "b_gate", "gla_norm"):
        W[n] = full[n].astype(F32)
    return W


def _col_sh(w):
    return jnp.moveaxis(w.reshape(w.shape[0], NSH, w.shape[1] // NSH), 1, 0)


def grad_items(G):
    ei = _col_sh(jnp.take(G["e_in"], jnp.asarray(_inverse(_even_in_map(), 2824)), axis=1))
    ei = jnp.pad(ei, ((0, 0), (0, 0), (0, IN_COLS - ei.shape[2])))
    oi = _col_sh(jnp.take(G["o_in"], jnp.asarray(_inverse(_odd_in_map(), 3088)), axis=1))
    eo = jnp.take(G["e_out"], jnp.asarray(_inverse(_even_out_map(), 1024)), axis=0).reshape(NSH, 256, D)
    oo = G["o_out"].reshape(NSH, 256, D)
    proj = [(_bf(ei), _bf(eo)), (_bf(oi), _bf(oo))]
    return [[G["ffn_g"][l][0], G["ffn_g"][l][1], G["ffn_u"][l][0], G["ffn_u"][l][1], G["ffn_d"][l][0], G["ffn_d"][l][1],
             proj[l][0], proj[l][1]] for l in range(2)]


def shard_grads(f0, f1):
    st = lambda a, b: jnp.stack([jnp.stack([f0[a], f0[b]]), jnp.stack([f1[a], f1[b]])])
    return {"ffn_g": st(0, 1), "ffn_u": st(2, 3), "ffn_d": st(4, 5), "e_in": f0[6][None, :, :706], "e_out": f0[7][None],
            "o_in": f1[6][None], "o_out": f1[7][None]}


def pack_small_grads(G):
    col_sh = _col_sh
    norm_sh = jnp.moveaxis(G["norm"].reshape(2, 6, NSH, 256), 2, 0)
    a = jnp.concatenate([col_sh(G["meta"]).reshape(NSH, -1), norm_sh.reshape(NSH, -1), col_sh(G["conv"]).reshape(NSH, -1),
                         col_sh(G["gate_up"]).reshape(NSH, -1), G["b_gate"].reshape(NSH, -1),
                         G["gla_norm"].reshape(NSH, -1)], axis=1)
    b = jnp.concatenate([G[n].reshape(-1) for n, _ in REPL])
    b = jnp.broadcast_to(b[None], (NSH, b.shape[0]))
    small = jnp.concatenate([a, jnp.zeros((NSH, SMALL_REP - a.shape[1]), F32), b,
                             jnp.zeros((NSH, SMALL_ROWS * LANE - SMALL_REP - b.shape[1]), F32)], axis=1)
    return small.reshape(NSH, SMALL_ROWS, LANE)


MESH = pl.DeviceIdType.MESH
ANY = pl.BlockSpec(memory_space=pl.ANY)
VMEM = pl.BlockSpec(memory_space=pltpu.VMEM)


def _place():
    return lax.axis_index("x"), lax.axis_index("y"), lax.axis_index("c")


def _other_chips(x, y):
    return [(1 - x, y), (x, 1 - y), (1 - x, 1 - y)]


def _rcopy(send_sems, recv_sems, k, src, dst, to):
    return pltpu.make_async_remote_copy(src_ref=src, dst_ref=dst, send_sem=send_sems.at[k], recv_sem=recv_sems.at[k],
                                        device_id=to, device_id_type=MESH)


_AG_ITEMS = [[(0, (0,)), (1, (0,)), (2, (0,)), (3, ()), (4, ())], [(0, (1,)), (1, (1,)), (2, (1,)), (5, ()), (6, ())]]


def ag_big(shards):
    n = len(shards)
    ni = len(_AG_ITEMS[0])

    def body(*refs):
        in_refs, out_refs = refs[:n], refs[n:2 * n]
        send_sems, recv_sems = refs[2 * n:]
        x, y, c = _place()
        s = 2 * x + y
        chips = _other_chips(x, y)
        copy = functools.partial(_rcopy, send_sems, recv_sems)
        src = lambda it: in_refs[it[0]].at[it[1]] if it[1] else in_refs[it[0]]
        dst = lambda it, slot: out_refs[it[0]].at[(slot,) + it[1]]
        for cc in range(2):
            @pl.when(c == cc)
            def _():
                mine, theirs = _AG_ITEMS[cc], _AG_ITEMS[1 - cc]
                first, passed = [], []
                for i, it in enumerate(mine):
                    for j, (cx, cy) in enumerate(chips):
                        first.append(copy(i * 3 + j, src(it), dst(it, s), (cx, cy, c)))
                for cp in first:
                    cp.start()
                for j, (cx, cy) in enumerate(chips):
                    for i, it in enumerate(mine):
                        blk = dst(it, 2 * cx + cy)
                        copy(i * 3 + j, blk, blk, (x, y, c)).wait_recv()
                        p = copy(ni * 3 + i * 3 + j, blk, blk, (x, y, 1 - c))
                        p.start()
                        passed.append(p)
                for j, (cx, cy) in enumerate(chips):
                    for i, it in enumerate(theirs):
                        blk = dst(it, 2 * cx + cy)
                        copy(ni * 3 + i * 3 + j, blk, blk, (x, y, c)).wait_recv()
                for cp in first + passed:
                    cp.wait_send()

    return pl.pallas_call(
        body, name="ag_big", in_specs=[ANY] * n, out_specs=[ANY] * n,
        out_shape=[jax.ShapeDtypeStruct((NSH,) + a.shape, a.dtype) for a in shards],
        scratch_shapes=[pltpu.SemaphoreType.DMA((6 * ni,)), pltpu.SemaphoreType.DMA((6 * ni,))],
    )(*shards)


def ag_small(pack):
    def body(x_ref, out_ref, send_sems, recv_sems):
        x, y, c = _place()
        s = 2 * x + y
        chips = _other_chips(x, y)

        def copy(k, src, dst, to):
            return pltpu.make_async_remote_copy(src_ref=src, dst_ref=dst, send_sem=send_sems.at[k], recv_sem=recv_sems.at[k],
                                                device_id=to, device_id_type=MESH)

        out_ref[s] = x_ref[...]
        sends = [copy(j, x_ref, out_ref.at[s], (cx, cy, c)) for j, (cx, cy) in enumerate(chips)]
        for cp in sends:
            cp.start()
        for j, (cx, cy) in enumerate(chips):
            blk = out_ref.at[2 * cx + cy]
            copy(j, blk, blk, (x, y, c)).wait_recv()
        for cp in sends:
            cp.wait_send()

    return pl.pallas_call(
        body, name="ag_small", in_specs=[VMEM], out_specs=VMEM,
        out_shape=jax.ShapeDtypeStruct((NSH,) + pack.shape, pack.dtype),
        scratch_shapes=[pltpu.SemaphoreType.DMA((3,)), pltpu.SemaphoreType.DMA((3,))],
    )(pack)


def rs_pair(items):
    ni = len(items[0])

    def body(*refs):
        in_refs = [refs[:ni], refs[ni:2 * ni]]
        recv_refs = refs[2 * ni:3 * ni]
        send_sems, recv_sems = refs[3 * ni:]
        x, y, c = _place()
        copy = functools.partial(_rcopy, send_sems, recv_sems)
        for cc in range(2):
            @pl.when(c == cc)
            def _():
                cps = [copy(i * NSH + s, in_refs[1 - cc][i].at[s], recv_refs[i].at[s], (x, y, 1 - c))
                       for i in range(ni) for s in range(NSH)]
                for cp in cps:
                    cp.start()
                for cp in cps:
                    cp.wait()

    return pl.pallas_call(
        body, name="rs_pair", in_specs=[ANY] * (2 * ni), out_specs=[ANY] * ni,
        out_shape=[jax.ShapeDtypeStruct(a.shape, a.dtype) for a in items[0]],
        scratch_shapes=[pltpu.SemaphoreType.DMA((ni * NSH,)), pltpu.SemaphoreType.DMA((ni * NSH,))],
    )(*items[0], *items[1])


def rs_chips(arrs):
    n = len(arrs)

    def body(*refs):
        a_refs, out_refs = refs[:n], refs[n:2 * n]
        send_sems, recv_sems = refs[2 * n:]
        x, y, c = _place()
        s = 2 * x + y
        chips = _other_chips(x, y)
        copy = functools.partial(_rcopy, send_sems, recv_sems)
        sends = [copy(i * 3 + j, a_refs[i].at[2 * cx + cy], out_refs[i].at[s], (cx, cy, c))
                 for i in range(n) for j, (cx, cy) in enumerate(chips)]
        for cp in sends:
            cp.start()
        for i in range(n):
            for j, (cx, cy) in enumerate(chips):
                blk = out_refs[i].at[2 * cx + cy]
                copy(i * 3 + j, blk, blk, (x, y, c)).wait_recv()
        for cp in sends:
            cp.wait_send()

    return pl.pallas_call(
        body, name="rs_chips", in_specs=[ANY] * n, out_specs=[ANY] * n,
        out_shape=[jax.ShapeDtypeStruct(a.shape, a.dtype) for a in arrs],
        scratch_shapes=[pltpu.SemaphoreType.DMA((3 * n,)), pltpu.SemaphoreType.DMA((3 * n,))],
    )(*arrs)


PAIR_CHUNKS = 4


def ag_pair(arrs):
    n = len(arrs)

    def body(*refs):
        g_refs, out_refs = refs[:n], refs[n:2 * n]
        send_sems, recv_sems = refs[2 * n:]
        x, y, c = _place()
        give = []
        for i, a in enumerate(arrs):
            rc = a.shape[0] // PAIR_CHUNKS
            for k in range(PAIR_CHUNKS):
                rows = pl.ds(k * rc, rc)
                give.append(_rcopy(send_sems, recv_sems, i * PAIR_CHUNKS + k, g_refs[i].at[rows], out_refs[i].at[rows],
                                   (x, y, 1 - c)))
        for cp in give:
            cp.start()
        for cp in give:
            cp.wait()

    return pl.pallas_call(
        body, name="ag_pair", in_specs=[ANY] * n, out_specs=[ANY] * n,
        out_shape=[jax.ShapeDtypeStruct(a.shape, a.dtype) for a in arrs],
        scratch_shapes=[pltpu.SemaphoreType.DMA((n * PAIR_CHUNKS,)), pltpu.SemaphoreType.DMA((n * PAIR_CHUNKS,))],
    )(*arrs)


def small_allreduce(p):
    def body(p_ref, out_ref, rbuf, send_sems, recv_sems):
        x, y, c = _place()
        me = 4 * x + 2 * y + c
        rbuf[me] = p_ref[2 * x + y]
        flip = lambda v, f: (1 - v) if f else v
        peers = [(flip(x, k >> 2 & 1), flip(y, k >> 1 & 1), flip(c, k & 1)) for k in range(1, 8)]

        def copy(k, src, dst, to):
            return pltpu.make_async_remote_copy(src_ref=src, dst_ref=dst, send_sem=send_sems.at[k], recv_sem=recv_sems.at[k],
                                                device_id=to, device_id_type=MESH)

        sends = [copy(k, p_ref.at[2 * px + py], rbuf.at[me], (px, py, pc)) for k, (px, py, pc) in enumerate(peers)]
        for cp in sends:
            cp.start()
        for k, (px, py, pc) in enumerate(peers):
            blk = rbuf.at[4 * px + 2 * py + pc]
            copy(k, blk, blk, (x, y, c)).wait_recv()
        for cp in sends:
            cp.wait_send()
        acc = rbuf[0]
        for d in range(1, 8):
            acc = acc + rbuf[d]
        out_ref[...] = acc

    return pl.pallas_call(
        body, name="small_allreduce", in_specs=[VMEM], out_specs=VMEM,
        out_shape=jax.ShapeDtypeStruct(p.shape[1:], F32),
        scratch_shapes=[pltpu.VMEM((8,) + p.shape[1:], F32), pltpu.SemaphoreType.DMA((7,)), pltpu.SemaphoreType.DMA((7,))],
    )(p)


def _rows_tile(rows, cap):
    return _pick(rows, cap) if rows % 128 == 0 else rows


def sum_pair(name, a0, a1, recv, cflag):
    n, r, d = recv.shape
    tr = _pick(r, 1024) if r % 64 == 0 else r

    def body(c_ref, a0_ref, a1_ref, b_ref, o_ref):
        own = jnp.where(c_ref[0] == 0, a0_ref[...].astype(F32), a1_ref[...].astype(F32))
        o_ref[...] = (own + b_ref[...].astype(F32)).astype(o_ref.dtype)

    spec = pl.BlockSpec((None, tr, d), lambda s, i: (s, i, 0))
    return pl.pallas_call(
        body, name=name, grid=(n, r // tr), in_specs=[pl.BlockSpec(memory_space=pltpu.SMEM), spec, spec, spec],
        out_specs=spec, out_shape=jax.ShapeDtypeStruct(recv.shape, BF16), compiler_params=_cparams(("parallel", "parallel")),
    )(cflag, a0, a1, recv)


def sum_chips(name, parts, own, sflag):
    n, r, d = parts.shape
    tr = _pick(r, 1024) if r % 64 == 0 else r

    def body(s_ref, p_ref, a_ref, o_ref):
        acc = None
        for t in range(n):
            term = jnp.where(s_ref[0] == t, a_ref[t].astype(F32), p_ref[t].astype(F32))
            acc = term if acc is None else acc + term
        o_ref[...] = acc

    spec = pl.BlockSpec((n, tr, d), lambda i: (0, i, 0))
    return pl.pallas_call(
        body, name=name, grid=(r // tr,), in_specs=[pl.BlockSpec(memory_space=pltpu.SMEM), spec, spec],
        out_specs=pl.BlockSpec((tr, d), lambda i: (i, 0)), out_shape=jax.ShapeDtypeStruct((r, d), F32),
        compiler_params=_cparams(("parallel",)),
    )(sflag, parts, own)


ADAM_LR, ADAM_B1, ADAM_B2, ADAM_EPS, ADAM_WD, ADAM_STEP = 0.001, 0.9, 0.999, 1e-08, 0.01, 10


def adamw_call(name, w, g, m, v):
    rows, cols = w.shape
    tr = _rows_tile(rows, 512)

    def body(w_ref, g_ref, m_ref, v_ref, d_ref, nm_ref, nv_ref):
        g_ = g_ref[...]
        m_ = ADAM_B1 * m_ref[...] + (1.0 - ADAM_B1) * g_
        v_ = ADAM_B2 * v_ref[...] + (1.0 - ADAM_B2) * (g_ * g_)
        m_hat = m_ / (1.0 - ADAM_B1 ** ADAM_STEP)
        v_hat = v_ / (1.0 - ADAM_B2 ** ADAM_STEP)
        d_ref[...] = -ADAM_LR * (m_hat / (jnp.sqrt(v_hat) + ADAM_EPS) + ADAM_WD * w_ref[...])
        nm_ref[...] = m_
        nv_ref[...] = v_

    spec = pl.BlockSpec((tr, cols), lambda i: (i, 0))
    sh = jax.ShapeDtypeStruct((rows, cols), F32)
    return pl.pallas_call(
        body, name=name, grid=(rows // tr,), in_specs=[spec] * 4, out_specs=[spec] * 3, out_shape=[sh] * 3,
        compiler_params=_cparams(("parallel",)),
    )(w, g, m, v)


def kernel(x, meta_tokens, norm_w, ffn_w_gate, ffn_w_up, ffn_w_down, rel_bias_table, even_w_in, even_conv_w, swa_sinks, dn_a_log, dn_dt_bias, dn_norm_w, even_w_out, odd_w_in, gla_w_gate_up, gla_b_gate, gla_norm_w, odd_w_out, loss_target, m_meta_tokens, m_norm_w, m_ffn_w_gate, m_ffn_w_up, m_ffn_w_down, m_rel_bias_table, m_even_w_in, m_even_conv_w, m_swa_sinks, m_dn_a_log, m_dn_dt_bias, m_dn_norm_w, m_even_w_out, m_odd_w_in, m_gla_w_gate_up, m_gla_b_gate, m_gla_norm_w, m_odd_w_out, v_meta_tokens, v_norm_w, v_ffn_w_gate, v_ffn_w_up, v_ffn_w_down, v_rel_bias_table, v_even_w_in, v_even_conv_w, v_swa_sinks, v_dn_a_log, v_dn_dt_bias, v_dn_norm_w, v_even_w_out, v_odd_w_in, v_gla_w_gate_up, v_gla_b_gate, v_gla_norm_w, v_odd_w_out):
    ws = [meta_tokens, norm_w, ffn_w_gate, ffn_w_up, ffn_w_down, rel_bias_table, even_w_in, even_conv_w, swa_sinks, dn_a_log,
          dn_dt_bias, dn_norm_w, even_w_out, odd_w_in, gla_w_gate_up, gla_b_gate, gla_norm_w, odd_w_out]
    ms = [m_meta_tokens, m_norm_w, m_ffn_w_gate, m_ffn_w_up, m_ffn_w_down, m_rel_bias_table, m_even_w_in, m_even_conv_w,
          m_swa_sinks, m_dn_a_log, m_dn_dt_bias, m_dn_norm_w, m_even_w_out, m_odd_w_in, m_gla_w_gate_up, m_gla_b_gate,
          m_gla_norm_w, m_odd_w_out]
    vs = [v_meta_tokens, v_norm_w, v_ffn_w_gate, v_ffn_w_up, v_ffn_w_down, v_rel_bias_table, v_even_w_in, v_even_conv_w,
          v_swa_sinks, v_dn_a_log, v_dn_dt_bias, v_dn_norm_w, v_even_w_out, v_odd_w_in, v_gla_w_gate_up, v_gla_b_gate,
          v_gla_norm_w, v_odd_w_out]
    short = [n for n, _ in NAMES]
    w = dict(zip(short, ws))
    m = dict(zip(short, ms))
    v = dict(zip(short, vs))

    big_names = ["ffn_g", "ffn_u", "ffn_d", "e_in", "e_out", "o_in", "o_out"]
    sq = lambda n, a: a if n.startswith("ffn") else a[0]
    own = [sq(n, w[n]).astype(BF16) for n in big_names]
    sflag = (2 * lax.axis_index("x") + lax.axis_index("y")).astype(jnp.int32).reshape(1)
    gb = {n: lax.dynamic_update_index_in_dim(g_, o_, sflag[0], 0) for n, g_, o_ in zip(big_names, ag_big(own), own)}
    gs = ag_small(pack_small(w))
    W = make_W(full_from_gathered(gb, gs), {n: w[n] for n, _ in REPL})

    loss_blk, gx, G = core_step(x[0], loss_target[0], W)

    items = grad_items(G)
    cflag = lax.axis_index("c").astype(jnp.int32).reshape(1)
    recv = rs_pair(items)
    mine = [sum_pair(f"sum_pair_{i}", items[0][i], items[1][i], recv[i], cflag) for i in range(len(recv))]
    parts = rs_chips(mine)
    red = [sum_chips(f"sum_chips_{i}", p, a, sflag) for i, (p, a) in enumerate(zip(parts, mine))]
    got = ag_pair(red)
    is0 = cflag[0] == 0
    f0 = [jnp.where(is0, r_, g_) for r_, g_ in zip(red, got)]
    f1 = [jnp.where(is0, g_, r_) for r_, g_ in zip(red, got)]
    g_small_pack = small_allreduce(pack_small_grads(G))
    g = {**shard_grads(f0, f1), **unpack_small(g_small_pack)}

    delta, new_m, new_v = {}, {}, {}
    for n, shp in BIG:
        two = lambda t: t.reshape(-1, shp[-1])
        d_, m_, v_ = adamw_call("adamw_" + n, two(w[n]), two(g[n]), two(m[n]), two(v[n]))
        delta[n], new_m[n], new_v[n] = d_.reshape(shp), m_.reshape(shp), v_.reshape(shp)
    d_, m_, v_ = adamw_call("adamw_small", pack_small(w), g_small_pack, pack_small(m), pack_small(v))
    delta.update(unpack_small(d_))
    new_m.update(unpack_small(m_))
    new_v.update(unpack_small(v_))

    loss = lax.psum(loss_blk[0, 0], ("x", "y", "c"))
    return (loss, gx[None], *[g[n] for n in short], *[delta[n] for n in short], *[new_m[n] for n in short],
            *[new_v[n] for n in short])
```

```python
import functools
import math

import numpy as np
import jax
import jax.numpy as jnp
from jax import lax
from jax.experimental import pallas as pl
from jax.experimental.pallas import tpu as pltpu

F32 = jnp.float32
BF16 = jnp.bfloat16
HI = lax.Precision.HIGHEST

D = 1024
N_META = 16
PADR = 128
ZROWS = PADR - N_META
D_FF = 2816
NSH = 4
FSH = D_FF // NSH
EPS = 1e-6
NEG = -1e30
CH = 64
BLK = 128
LANE = 128
VMEM_LIMIT = 56 * 1024 * 1024

E_QA, E_KA, E_VA, E_QB, E_KB, E_VB, E_ZB, E_BA, E_END = 0, 1024, 1280, 1536, 2048, 2560, 3072, 3584, 4096


def _even_in_map():
    m = np.full((E_END,), -1, np.int64)
    for h in range(8):
        m[E_QA + h * 128:E_QA + h * 128 + 64] = np.arange(h * 64, (h + 1) * 64)
    for h in range(2):
        m[E_KA + h * 128:E_KA + h * 128 + 64] = 512 + np.arange(h * 64, (h + 1) * 64)
        m[E_VA + h * 128:E_VA + h * 128 + 64] = 640 + np.arange(h * 64, (h + 1) * 64)
    m[E_QB:E_QB + 2048] = 768 + np.arange(2048)
    m[E_BA:E_BA + 8] = 2816 + np.arange(8)
    return m


def _even_out_map():
    m = np.full((1536,), -1, np.int64)
    for h in range(8):
        m[h * 128:h * 128 + 64] = np.arange(h * 64, (h + 1) * 64)
    m[1024:1536] = 512 + np.arange(512)
    return m


O_Q, O_K, O_V, O_G, O_GK, O_END = 0, 512, 1024, 2048, 3072, 3584


def _odd_in_map():
    m = np.full((O_END,), -1, np.int64)
    m[:3072] = np.arange(3072)
    m[O_GK:O_GK + 16] = 3072 + np.arange(16)
    return m


def _inverse(m, n):
    inv = np.zeros((n,), np.int64)
    for p, o in enumerate(m):
        if o >= 0:
            inv[o] = p
    return inv


def _take_pad(w, m, axis):
    t = jnp.take(w, jnp.asarray(np.maximum(m, 0)), axis=axis)
    shape = [1] * w.ndim
    shape[axis] = m.shape[0]
    return jnp.where(jnp.asarray(m >= 0).reshape(shape), t, jnp.zeros((), w.dtype))


def _mm(a, b, prec=HI):
    return lax.dot_general(a, b, (((1,), (0,)), ((), ())), precision=prec, preferred_element_type=F32)


def _mm_nt(a, b, prec=HI):
    return lax.dot_general(a, b, (((1,), (1,)), ((), ())), precision=prec, preferred_element_type=F32)


def _mm_tn(a, b, prec=HI):
    return lax.dot_general(a, b, (((0,), (0,)), ((), ())), precision=prec, preferred_element_type=F32)


def _bdot(a, b, dims):
    return lax.dot_general(a.astype(BF16), b.astype(BF16), (dims, ((), ())), preferred_element_type=F32)


@jax.custom_vjp
def _bmm(a, b):
    return _bdot(a, b, ((1,), (0,)))


@jax.custom_vjp
def _bmm_nt(a, b):
    return _bdot(a, b, ((1,), (1,)))


@jax.custom_vjp
def _bmm_tn(a, b):
    return _bdot(a, b, ((0,), (0,)))


_bmm.defvjp(lambda a, b: (_bmm(a, b), (a, b)), lambda r, g: (_bmm_nt(g, r[1]), _bmm_tn(r[0], g)))
_bmm_nt.defvjp(lambda a, b: (_bmm_nt(a, b), (a, b)), lambda r, g: (_bmm(g, r[1]), _bmm_tn(g, r[0])))
_bmm_tn.defvjp(lambda a, b: (_bmm_tn(a, b), (a, b)), lambda r, g: (_bmm_nt(r[1], g), _bmm(r[0], g)))


def _hi_lo(x):
    h = x.astype(BF16)
    return h, (x - h.astype(F32)).astype(BF16)


def _xdot(a, b, dims):
    ah, al = _hi_lo(a)
    bh, bl = _hi_lo(b)
    d = lambda p, q: lax.dot_general(p, q, (dims, ((), ())), preferred_element_type=F32)
    return d(ah, bh) + (d(ah, bl) + d(al, bh))


@jax.custom_vjp
def _xmm(a, b):
    return _xdot(a, b, ((1,), (0,)))


@jax.custom_vjp
def _xmm_nt(a, b):
    return _xdot(a, b, ((1,), (1,)))


@jax.custom_vjp
def _xmm_tn(a, b):
    return _xdot(a, b, ((0,), (0,)))


_xmm.defvjp(lambda a, b: (_xmm(a, b), (a, b)), lambda r, g: (_xmm_nt(g, r[1]), _xmm_tn(r[0], g)))
_xmm_nt.defvjp(lambda a, b: (_xmm_nt(a, b), (a, b)), lambda r, g: (_xmm(g, r[1]), _xmm_tn(g, r[0])))
_xmm_tn.defvjp(lambda a, b: (_xmm_tn(a, b), (a, b)), lambda r, g: (_xmm_nt(r[1], g), _xmm(r[0], g)))


def _sum01(m01, x, dims):
    h, l = _hi_lo(x)
    l2 = (x - h.astype(F32) - l.astype(F32)).astype(BF16)
    m = m01.astype(BF16)
    d = lambda q: lax.dot_general(m, q, (dims, ((), ())), preferred_element_type=F32)
    return d(h) + (d(l) + d(l2))


@jax.custom_vjp
def _cumsum_rows(x):
    n = x.shape[0]
    tri = lax.broadcasted_iota(jnp.int32, (n, n), 0) >= lax.broadcasted_iota(jnp.int32, (n, n), 1)
    return _sum01(tri, x, ((1,), (0,)))


def _cumsum_rows_b(_, g):
    n = g.shape[0]
    tri = lax.broadcasted_iota(jnp.int32, (n, n), 0) >= lax.broadcasted_iota(jnp.int32, (n, n), 1)
    return (_sum01(tri, g, ((0,), (0,))),)


_cumsum_rows.defvjp(lambda x: (_cumsum_rows(x), None), _cumsum_rows_b)


@functools.partial(jax.custom_vjp, nondiff_argnums=(1,))
def _colsum_as_rows(x, width):
    return _colsum_impl(x, width)


def _colsum_impl(x, width):
    h, l = _hi_lo(x)
    l2 = (x - h.astype(F32) - l.astype(F32)).astype(BF16)
    ones = jnp.ones((x.shape[0], width), BF16)
    d = lambda q: lax.dot_general(q, ones, (((0,), (0,)), ((), ())), preferred_element_type=F32)
    return d(h) + (d(l) + d(l2))


def _colsum_as_rows_f(x, width):
    return _colsum_impl(x, width), x.shape[0]


def _colsum_as_rows_b(width, n, g):
    return (_sum01(jnp.ones((n, width), F32), g, ((1,), (1,))),)


_colsum_as_rows.defvjp(_colsum_as_rows_f, _colsum_as_rows_b)


def _rms(x, w):
    return x * lax.rsqrt(jnp.mean(x * x, axis=-1, keepdims=True) + EPS) * w


def _sigmoid(x):
    return 1.0 / (1.0 + jnp.exp(-x))


def _silu(x):
    return x * _sigmoid(x)


def _softplus(x):
    return jnp.maximum(x, 0.0) + jnp.log(1.0 + jnp.exp(-jnp.abs(x)))


def _lane_pick(row, idx):
    lane = lax.broadcasted_iota(jnp.int32, row.shape, row.ndim - 1)
    return jnp.sum(jnp.where(lane == idx, row, 0.0), axis=-1, keepdims=True)


def _row_ids(row0, n):
    return row0 + lax.broadcasted_iota(jnp.int32, (n, 1), 0)


def _pick(m, cap):
    best = 64
    for t in range(64, min(m, cap) + 1, 64):
        if m % t == 0:
            best = t
    return best


def _cparams(sem):
    return pltpu.CompilerParams(dimension_semantics=sem, vmem_limit_bytes=VMEM_LIMIT)


def mm_nn(a, b, name, out_dtype=F32):
    M, K = a.shape
    N = b.shape[1]
    tm = _pick(M, 1408 if K <= 2048 else 704)
    tn = _pick(N, 512)

    def body(a_ref, b_ref, o_ref):
        o_ref[...] = _mm(a_ref[...], b_ref[...], None).astype(o_ref.dtype)

    return pl.pallas_call(
        body, name=name, grid=(N // tn, M // tm),
        in_specs=[pl.BlockSpec((tm, K), lambda j, i: (i, 0)), pl.BlockSpec((K, tn), lambda j, i: (0, j))],
        out_specs=pl.BlockSpec((tm, tn), lambda j, i: (i, j)),
        out_shape=jax.ShapeDtypeStruct((M, N), out_dtype),
        compiler_params=_cparams(("parallel", "parallel")),
    )(a, b)


def mm_nt(a, b, name, out_dtype=F32):
    M, K = a.shape
    N = b.shape[0]
    tm = _pick(M, 768)
    tn = _pick(N, 512)

    def body(a_ref, b_ref, o_ref):
        o_ref[...] = _mm_nt(a_ref[...], b_ref[...], None).astype(o_ref.dtype)

    return pl.pallas_call(
        body, name=name, grid=(N // tn, M // tm),
        in_specs=[pl.BlockSpec((tm, K), lambda j, i: (i, 0)), pl.BlockSpec((tn, K), lambda j, i: (j, 0))],
        out_specs=pl.BlockSpec((tm, tn), lambda j, i: (i, j)),
        out_shape=jax.ShapeDtypeStruct((M, N), out_dtype),
        compiler_params=_cparams(("parallel", "parallel")),
    )(a, b)


def mm_tn(a, b, name):
    M, K = a.shape
    N = b.shape[1]
    tk = _pick(K, 512)
    tn = _pick(N, 512)

    def body(a_ref, b_ref, o_ref):
        o_ref[...] = _mm_tn(a_ref[...], b_ref[...], None)

    return pl.pallas_call(
        body, name=name, grid=(K // tk, N // tn),
        in_specs=[pl.BlockSpec((M, tk), lambda i, j: (0, i)), pl.BlockSpec((M, tn), lambda i, j: (0, j))],
        out_specs=pl.BlockSpec((tk, tn), lambda i, j: (i, j)),
        out_shape=jax.ShapeDtypeStruct((K, N), F32),
        compiler_params=_cparams(("parallel", "parallel")),
    )(a, b)


def _row_specs(rows, tm):
    return [pl.BlockSpec((tm, w), functools.partial(lambda i, cb: (i, cb), cb=cb)) for (_, w, cb) in rows]


def _param_specs(params):
    return [pl.BlockSpec(p.shape, functools.partial(lambda i, nd: (0,) * nd, nd=p.ndim)) for p in params]


def rowwise_fwd(name, fn, rows, params, outs, tm=None):
    M = rows[0][0].shape[0]
    tm = tm or _pick(M, 704)
    nr, npar = len(rows), len(params)

    def body(*refs):
        row0 = pl.program_id(0) * tm
        vals = [r[...].astype(F32) for r in refs[:nr]] + [p[...] for p in refs[nr:nr + npar]]
        res = fn(row0, *vals)
        for o_ref, r in zip(refs[nr + npar:], res):
            o_ref[...] = r.astype(o_ref.dtype)

    return pl.pallas_call(
        body, name=name, grid=(M // tm,),
        in_specs=_row_specs(rows, tm) + _param_specs(params),
        out_specs=[pl.BlockSpec((tm, w), lambda i: (i, 0)) for (w, _) in outs],
        out_shape=[jax.ShapeDtypeStruct((M, w), dt) for (w, dt) in outs],
        compiler_params=_cparams(("parallel",)),
    )(*[r[0] for r in rows], *params)


def rowwise_bwd(name, fn, rows, params, douts, drow_dtypes, tm=None):
    M = rows[0][0].shape[0]
    tm = tm or _pick(M, 704)
    nr, npar, nd = len(rows), len(params), len(douts)
    want = [k for k, dt in enumerate(drow_dtypes) if dt is not None]

    def body(*refs):
        i = pl.program_id(0)
        row0 = i * tm
        vals = [r[...].astype(F32) for r in refs[:nr]] + [p[...] for p in refs[nr:nr + npar]]
        cots = tuple(d[...].astype(F32) for d in refs[nr + npar:nr + npar + nd])
        _, vjp = jax.vjp(functools.partial(fn, row0), *vals)
        grads = vjp(cots)
        o_refs = refs[nr + npar + nd:]
        for o_ref, k in zip(o_refs[:len(want)], want):
            o_ref[...] = grads[k].astype(o_ref.dtype)
        for o_ref, g in zip(o_refs[len(want):], grads[nr:]):
            @pl.when(i == 0)
            def _():
                o_ref[...] = g

            @pl.when(i > 0)
            def _():
                o_ref[...] += g

    res = pl.pallas_call(
        body, name=name, grid=(M // tm,),
        in_specs=_row_specs(rows, tm) + _param_specs(params) + _row_specs(douts, tm),
        out_specs=[pl.BlockSpec((tm, rows[k][1]), lambda i: (i, 0)) for k in want] + _param_specs(params),
        out_shape=[jax.ShapeDtypeStruct((M, rows[k][1]), drow_dtypes[k]) for k in want]
        + [jax.ShapeDtypeStruct(p.shape, F32) for p in params],
        compiler_params=_cparams(("arbitrary",)),
    )(*[r[0] for r in rows], *params, *[d[0] for d in douts])
    return res[:len(want)], res[len(want):]


def _fn_prenorm(row0, h, wpre):
    return (_rms(h, wpre),)


def _fn_resnorm(scale, row0, h, f, wpost, wpre):
    h2 = h + scale * _rms(f, wpost)
    return h2, _rms(h2, wpre)


def _fn_res_last(scale, row0, h, f, wpost):
    return (h + scale * _rms(f, wpost),)


def ffn_fwd(name, xn, wg, wu, wd, l, j):
    M = xn.shape[0]
    tm = _pick(M, 704)

    def body(x_ref, wg_ref, wu_ref, wd_ref, f_ref, a_ref, b_ref):
        s = pl.program_id(1)
        x = x_ref[...]
        a = _mm_nt(x, wg_ref[...], None)
        b = _mm_nt(x, wu_ref[...], None)
        hm = (_silu(a) * b).astype(BF16)
        c = _mm(hm, wd_ref[...], None)

        @pl.when(s == 0)
        def _():
            f_ref[...] = c

        @pl.when(s > 0)
        def _():
            f_ref[...] += c

        a_ref[...] = a.astype(BF16)
        b_ref[...] = b.astype(BF16)

    wspec = wdspec = pl.BlockSpec((None, None, None, FSH, D), lambda i, s: (s, l, j, 0, 0))
    abspec = pl.BlockSpec((None, tm, FSH), lambda i, s: (s, i, 0))
    return pl.pallas_call(
        body, name=name, grid=(M // tm, NSH),
        in_specs=[pl.BlockSpec((tm, D), lambda i, s: (i, 0)), wspec, wspec, wdspec],
        out_specs=[pl.BlockSpec((tm, D), lambda i, s: (i, 0)), abspec, abspec],
        out_shape=[jax.ShapeDtypeStruct((M, D), F32), jax.ShapeDtypeStruct((NSH, M, FSH), BF16),
                   jax.ShapeDtypeStruct((NSH, M, FSH), BF16)],
        compiler_params=_cparams(("parallel", "arbitrary")),
    )(xn, wg, wu, wd)


def ffn_bwd_x(name, df, a, b, wg, wu, wd, l, j):
    M = df.shape[0]
    tm = _pick(M, 704)

    def body(df_ref, a_ref, b_ref, wg_ref, wu_ref, wd_ref, dx_ref, da_ref, db_ref, hm_ref):
        s = pl.program_id(1)
        a_ = a_ref[...].astype(F32)
        b_ = b_ref[...].astype(F32)
        dh = _mm_nt(df_ref[...], wd_ref[...], None)
        sig = _sigmoid(a_)
        sil = a_ * sig
        da = (dh * b_ * (sig * (1.0 + a_ * (1.0 - sig)))).astype(BF16)
        db = (dh * sil).astype(BF16)
        c = _mm(da, wg_ref[...], None) + _mm(db, wu_ref[...], None)

        @pl.when(s == 0)
        def _():
            dx_ref[...] = c

        @pl.when(s > 0)
        def _():
            dx_ref[...] += c

        da_ref[...] = da
        db_ref[...] = db
        hm_ref[...] = (sil * b_).astype(BF16)

    wspec = wdspec = pl.BlockSpec((None, None, None, FSH, D), lambda i, s: (s, l, j, 0, 0))
    abspec = pl.BlockSpec((None, tm, FSH), lambda i, s: (s, i, 0))
    ab = jax.ShapeDtypeStruct((NSH, M, FSH), BF16)
    return pl.pallas_call(
        body, name=name, grid=(M // tm, NSH),
        in_specs=[pl.BlockSpec((tm, D), lambda i, s: (i, 0)), abspec, abspec, wspec, wspec, wdspec],
        out_specs=[pl.BlockSpec((tm, D), lambda i, s: (i, 0)), abspec, abspec, abspec],
        out_shape=[jax.ShapeDtypeStruct((M, D), F32), ab, ab, ab],
        compiler_params=_cparams(("parallel", "arbitrary")),
    )(df, a, b, wg, wu, wd)


def ffn_bwd_w(name, xn, df, da, db, hm):
    M = xn.shape[0]
    tm = _pick(M, 704)
    nt = M // tm

    def body(x_ref, df_ref, da_ref, db_ref, hm_ref, dwg_ref, dwu_ref, dwd_ref, ag, au, ad):
        i = pl.program_id(1)
        x = x_ref[...]
        g = _mm_tn(da_ref[...], x, None)
        u = _mm_tn(db_ref[...], x, None)
        d = _mm_tn(hm_ref[...], df_ref[...], None)

        @pl.when(i == 0)
        def _():
            ag[...] = g
            au[...] = u
            ad[...] = d

        @pl.when(i > 0)
        def _():
            ag[...] += g
            au[...] += u
            ad[...] += d

        @pl.when(i == nt - 1)
        def _():
            dwg_ref[...] = ag[...].astype(BF16)
            dwu_ref[...] = au[...].astype(BF16)
            dwd_ref[...] = ad[...].astype(BF16)

    xspec = pl.BlockSpec((tm, D), lambda s, i: (i, 0))
    abspec = pl.BlockSpec((None, tm, FSH), lambda s, i: (s, i, 0))
    return pl.pallas_call(
        body, name=name, grid=(NSH, nt),
        in_specs=[xspec, xspec, abspec, abspec, abspec],
        out_specs=[pl.BlockSpec((None, FSH, D), lambda s, i: (s, 0, 0))] * 3,
        out_shape=[jax.ShapeDtypeStruct((NSH, FSH, D), BF16)] * 3,
        scratch_shapes=[pltpu.VMEM((FSH, D), F32)] * 3,
        compiler_params=_cparams(("parallel", "arbitrary")),
    )(xn, df, da, db, hm)


def _t5_bucket_np(rel):
    n = np.maximum(rel, 0)
    n_f = np.maximum(n, 1).astype(np.float32)
    large = 16 + (np.log(n_f / np.float32(16)) / np.float32(math.log(8.0)) * np.float32(16)).astype(np.int32)
    large = np.minimum(large, 31)
    return np.where(n < 16, n, large).astype(np.int32)


def _swa_bucket_ids():
    qi = np.arange(BLK)[:, None]
    kj = np.arange(BLK)[None, :]
    out = np.full((3, BLK, 3 * BLK), -1, np.int32)
    for v in range(3):
        pos_q = v * BLK + qi - ZROWS
        rel_m = pos_q - (kj - ZROWS)
        ok_m = (kj >= ZROWS) & (rel_m >= 0) & (pos_q >= 0)
        out[v, :, 0:BLK] = np.where(ok_m, _t5_bucket_np(rel_m), -1)
        pos_kp = (v - 1) * BLK + kj - ZROWS
        rel_p = BLK + qi - kj
        ok_p = (pos_kp >= N_META) & (rel_p >= 0) & (rel_p < BLK) & np.full_like(ok_m, v >= 1)
        out[v, :, BLK:2 * BLK] = np.where(ok_p, _t5_bucket_np(rel_p), -1)
        pos_kc = v * BLK + kj - ZROWS
        rel_c = qi - kj
        ok_c = (pos_kc >= N_META) & (rel_c >= 0) & (rel_c < BLK)
        out[v, :, 2 * BLK:] = np.where(ok_c, _t5_bucket_np(rel_c), -1)
    return out


def swa_bias_fwd(table, ids):
    def body(t_ref, id_ref, o_ref):
        for v in range(3):
            for h in range(8):
                o_ref[v, h] = jnp.where(id_ref[v] < 0, NEG, 0.0)

            def step(b, carry):
                hit = id_ref[v] == b
                for h in range(8):
                    o_ref[v, h] += jnp.where(hit, t_ref[b, h], 0.0)
                return carry

            lax.fori_loop(0, 32, step, 0)

    return pl.pallas_call(
        body, name="swa_bias_fwd",
        in_specs=[pl.BlockSpec(memory_space=pltpu.SMEM), pl.BlockSpec(memory_space=pltpu.VMEM)],
        out_specs=pl.BlockSpec(memory_space=pltpu.VMEM),
        out_shape=jax.ShapeDtypeStruct((3, 8, BLK, 3 * BLK), F32),
        compiler_params=pltpu.CompilerParams(vmem_limit_bytes=VMEM_LIMIT),
    )(table, ids)


def swa_bias_bwd(dbias, ids):
    def body(d_ref, id_ref, o_ref):
        r = lax.broadcasted_iota(jnp.int32, (32, LANE), 0)
        c = lax.broadcasted_iota(jnp.int32, (32, LANE), 1)

        def step(b, acc):
            for v in range(3):
                hit = id_ref[v] == b
                for h in range(8):
                    m = jnp.where(hit, d_ref[v, h], 0.0)
                    s = jnp.sum(jnp.sum(m, axis=1, keepdims=True), axis=0, keepdims=True)
                    acc = acc + jnp.where((r == b) & (c == h), s, 0.0)
            return acc

        o_ref[...] = lax.fori_loop(0, 32, step, jnp.zeros((32, LANE), F32))

    return pl.pallas_call(
        body, name="swa_bias_bwd",
        in_specs=[pl.BlockSpec(memory_space=pltpu.VMEM), pl.BlockSpec(memory_space=pltpu.VMEM)],
        out_specs=pl.BlockSpec(memory_space=pltpu.VMEM),
        out_shape=jax.ShapeDtypeStruct((32, LANE), F32),
        compiler_params=pltpu.CompilerParams(vmem_limit_bytes=VMEM_LIMIT),
    )(dbias, ids)


def _swa_block(q, k3, v3, bias, sinks):
    outs = []
    for hk in range(2):
        kh = k3[:, hk * 128:(hk + 1) * 128]
        vh = v3[:, hk * 128:(hk + 1) * 128]
        for g in range(4):
            h = hk * 4 + g
            s = _bmm_nt(q[:, h * 128:(h + 1) * 128], kh) * 0.125 + bias[h]
            sink = _lane_pick(sinks, h)
            m = lax.stop_gradient(jnp.maximum(jnp.max(s, axis=-1, keepdims=True), sink))
            e = jnp.exp(s - m)
            den = jnp.sum(e, axis=-1, keepdims=True) + jnp.exp(sink - m)
            outs.append(_bmm(e / den, vh))
    return jnp.concatenate(outs, axis=1)


def _swa_in_specs():
    qs = pl.BlockSpec((BLK, 1024), lambda n: (n, E_QA // 1024))
    ks = [pl.BlockSpec((BLK, 256), lambda n: (0, E_KA // 256)),
          pl.BlockSpec((BLK, 256), lambda n: (jnp.maximum(n - 1, 0), E_KA // 256)),
          pl.BlockSpec((BLK, 256), lambda n: (n, E_KA // 256))]
    vs = [pl.BlockSpec((BLK, 256), lambda n: (0, E_VA // 256)),
          pl.BlockSpec((BLK, 256), lambda n: (jnp.maximum(n - 1, 0), E_VA // 256)),
          pl.BlockSpec((BLK, 256), lambda n: (n, E_VA // 256))]
    bs = pl.BlockSpec((None, 8, BLK, 3 * BLK), lambda n: (jnp.minimum(n, 2), 0, 0, 0))
    ss = pl.BlockSpec((1, LANE), lambda n: (0, 0))
    return [qs] + ks + vs + [bs, ss]


def swa_fwd(proj, bias, sinks):
    M = proj.shape[0]

    def body(q_ref, k0, k1, k2, v0, v1, v2, b_ref, s_ref, o_ref):
        k3 = jnp.concatenate([k0[...], k1[...], k2[...]], axis=0)
        v3 = jnp.concatenate([v0[...], v1[...], v2[...]], axis=0)
        o_ref[...] = _swa_block(q_ref[...], k3, v3, b_ref[...], s_ref[...]).astype(o_ref.dtype)

    return pl.pallas_call(
        body, name="swa_fwd", grid=(M // BLK,),
        in_specs=_swa_in_specs(),
        out_specs=pl.BlockSpec((BLK, 1024), lambda n: (n, 0)),
        out_shape=jax.ShapeDtypeStruct((M, 1024), BF16),
        compiler_params=_cparams(("parallel",)),
    )(proj, proj, proj, proj, proj, proj, proj, bias, sinks)


def swa_bwd(proj, bias, sinks, do):
    M = proj.shape[0]

    def body(q_ref, k0, k1, k2, v0, v1, v2, b_ref, s_ref, do_ref, dq_ref, dk_ref, dv_ref, db_ref, ds_ref):
        n = pl.program_id(0)

        @pl.when(n == 0)
        def _():
            dk_ref[...] = jnp.zeros_like(dk_ref)
            dv_ref[...] = jnp.zeros_like(dv_ref)
            ds_ref[...] = jnp.zeros_like(ds_ref)

        k3 = jnp.concatenate([k0[...], k1[...], k2[...]], axis=0)
        v3 = jnp.concatenate([v0[...], v1[...], v2[...]], axis=0)
        _, vjp = jax.vjp(_swa_block, q_ref[...], k3, v3, b_ref[...], s_ref[...])
        dq, dk3, dv3, dbias, dsink = vjp(do_ref[...].astype(F32))
        dq_ref[...] = dq
        prev = pl.multiple_of(jnp.maximum(n - 1, 0) * BLK, BLK)
        cur = pl.multiple_of(n * BLK, BLK)
        dk_ref[pl.ds(0, BLK), :] += dk3[0:BLK]
        dv_ref[pl.ds(0, BLK), :] += dv3[0:BLK]
        dk_ref[pl.ds(prev, BLK), :] += dk3[BLK:2 * BLK]
        dv_ref[pl.ds(prev, BLK), :] += dv3[BLK:2 * BLK]
        dk_ref[pl.ds(cur, BLK), :] += dk3[2 * BLK:]
        dv_ref[pl.ds(cur, BLK), :] += dv3[2 * BLK:]
        ds_ref[...] += dsink

        @pl.when(n <= 2)
        def _():
            db_ref[...] = dbias

        @pl.when(n > 2)
        def _():
            db_ref[...] += dbias

    return pl.pallas_call(
        body, name="swa_bwd", grid=(M // BLK,),
        in_specs=_swa_in_specs() + [pl.BlockSpec((BLK, 1024), lambda n: (n, 0))],
        out_specs=[pl.BlockSpec((BLK, 1024), lambda n: (n, 0)),
                   pl.BlockSpec((M, 256), lambda n: (0, 0)), pl.BlockSpec((M, 256), lambda n: (0, 0)),
                   pl.BlockSpec((None, 8, BLK, 3 * BLK), lambda n: (jnp.minimum(n, 2), 0, 0, 0)),
                   pl.BlockSpec((1, LANE), lambda n: (0, 0))],
        out_shape=[jax.ShapeDtypeStruct((M, 1024), F32), jax.ShapeDtypeStruct((M, 256), F32),
                   jax.ShapeDtypeStruct((M, 256), F32), jax.ShapeDtypeStruct((3, 8, BLK, 3 * BLK), F32),
                   jax.ShapeDtypeStruct((1, LANE), F32)],
        compiler_params=_cparams(("arbitrary",)),
    )(proj, proj, proj, proj, proj, proj, proj, bias, sinks, do)


def _shift_rows_impl(x, k):
    n = x.shape[0]
    rolled = pltpu.roll(x, k, 0)
    return jnp.where(_row_ids(0, n) >= k, rolled, 0.0)


def _unshift_rows_impl(g, k):
    n = g.shape[0]
    rolled = pltpu.roll(g, n - k, 0)
    return jnp.where(_row_ids(0, n) < n - k, rolled, 0.0)


@functools.partial(jax.custom_vjp, nondiff_argnums=(1,))
def _shift_rows(x, k):
    return _shift_rows_impl(x, k)


def _shift_rows_f(x, k):
    return _shift_rows_impl(x, k), None


def _shift_rows_b(k, _, g):
    return (_unshift_rows_impl(g, k),)


_shift_rows.defvjp(_shift_rows_f, _shift_rows_b)


def _conv_silu(x, w):
    rid = lax.broadcasted_iota(jnp.int32, w.shape, 0)
    y = x * jnp.sum(jnp.where(rid == 3, w, 0.0), axis=0, keepdims=True)
    for k in range(1, 4):
        y = y + _shift_rows(x, k) * jnp.sum(jnp.where(rid == 3 - k, w, 0.0), axis=0, keepdims=True)
    y = jnp.where(_row_ids(0, x.shape[0]) >= ZROWS, y, 0.0)
    return _silu(y)


def conv_fwd(proj, conv_w):
    M = proj.shape[0]
    nb = conv_w.shape[1] // LANE

    def body(x_ref, w_ref, o_ref):
        o_ref[...] = _conv_silu(x_ref[...], w_ref[...])

    return pl.pallas_call(
        body, name="conv_fwd", grid=(nb,),
        in_specs=[pl.BlockSpec((M, LANE), lambda c: (0, E_QB // LANE + c)), pl.BlockSpec((4, LANE), lambda c: (0, c))],
        out_specs=pl.BlockSpec((M, LANE), lambda c: (0, c)),
        out_shape=jax.ShapeDtypeStruct((M, conv_w.shape[1]), F32),
        compiler_params=_cparams(("parallel",)),
    )(proj, conv_w)


def conv_bwd(proj, conv_w, dy):
    M = proj.shape[0]
    nb = conv_w.shape[1] // LANE

    def body(x_ref, w_ref, dy_ref, dx_ref, dw_ref):
        _, vjp = jax.vjp(_conv_silu, x_ref[...], w_ref[...])
        dx, dw = vjp(dy_ref[...])
        dx_ref[...] = dx
        dw_ref[...] = dw

    return pl.pallas_call(
        body, name="conv_bwd", grid=(nb,),
        in_specs=[pl.BlockSpec((M, LANE), lambda c: (0, E_QB // LANE + c)), pl.BlockSpec((4, LANE), lambda c: (0, c)),
                  pl.BlockSpec((M, LANE), lambda c: (0, c))],
        out_specs=[pl.BlockSpec((M, LANE), lambda c: (0, c)), pl.BlockSpec((4, LANE), lambda c: (0, c))],
        out_shape=[jax.ShapeDtypeStruct((M, conv_w.shape[1]), F32), jax.ShapeDtypeStruct(conv_w.shape, F32)],
        compiler_params=_cparams(("parallel",)),
    )(proj, conv_w, dy)


def _fn_dn_prep(row0, yq, yk, ba, dnp):
    tm = yq.shape[0]
    real = _row_ids(row0, tm) >= ZROWS
    qs, ks, gs, bs = [], [], [], []
    for h in range(4):
        q = yq[:, h * 128:(h + 1) * 128]
        k = yk[:, h * 128:(h + 1) * 128]
        qs.append(q * lax.rsqrt(jnp.sum(q * q, axis=-1, keepdims=True) + 1e-6) * (128.0 ** -0.5))
        ks.append(k * lax.rsqrt(jnp.sum(k * k, axis=-1, keepdims=True) + 1e-6))
        beta = _sigmoid(_lane_pick(ba, h))
        g = -jnp.exp(_lane_pick(dnp, h)) * _softplus(_lane_pick(ba, 4 + h) + _lane_pick(dnp, 4 + h))
        g = jnp.where(real, g, 0.0)
        gs.append(jnp.broadcast_to(g, (tm, 128)))
        bs.append(jnp.broadcast_to(beta, (tm, 128)))
    cat = lambda xs: jnp.concatenate(xs, axis=1)
    return cat(qs), cat(ks), cat(gs), cat(bs)


def _unit_lower_inv_impl(a):
    n = a.shape[0]
    r = lax.broadcasted_iota(jnp.int32, (n, n), 0)
    c = lax.broadcasted_iota(jnp.int32, (n, n), 1)
    p = -a
    t = (r == c).astype(F32) + p
    nn = ((1,), (0,))
    for _ in range(int(math.log2(n)) - 1):
        p = _xdot(p, p, nn)
        t = t + _xdot(t, p, nn)
    return t


@jax.custom_vjp
def _unit_lower_inv(a):
    return _unit_lower_inv_impl(a)


def _unit_lower_inv_f(a):
    t = _unit_lower_inv_impl(a)
    return t, t


def _unit_lower_inv_b(t, g):
    return (-_xdot(_xdot(t, g, ((0,), (0,))), t, ((1,), (1,))),)


_unit_lower_inv.defvjp(_unit_lower_inv_f, _unit_lower_inv_b)


def _dn_chunk(q, k, v, gb, bb, S):
    r = lax.broadcasted_iota(jnp.int32, (CH, CH), 0)
    c = lax.broadcasted_iota(jnp.int32, (CH, CH), 1)
    tri_incl = r >= c
    gcb = _cumsum_rows(gb)
    g1 = gcb[:, :CH]
    diff = g1 - g1.T
    gamma = jnp.where(tri_incl, jnp.exp(jnp.where(tri_incl, diff, 0.0)), 0.0)
    kb = k * bb
    vb = v * bb
    t = _unit_lower_inv(jnp.where(r > c, _bmm_nt(kb, k) * gamma, 0.0))
    eg = jnp.exp(gcb)
    u = _xmm(t, vb)
    w = _xmm(t, kb * eg)
    attn = _bmm_nt(q, k) * gamma
    gtot = jnp.sum(gb, axis=0, keepdims=True)
    k_dec = k * jnp.exp(gtot - gcb)
    g_last = jnp.exp(jnp.broadcast_to(gtot, S.shape))
    v_new = u - _bmm(w, S)
    o = _bmm(q * eg, S) + _bmm(attn, v_new)
    return o, S * g_last + _bmm_tn(k_dec, v_new)


def _gla_chunk(q, k, v, glog, S):
    r = lax.broadcasted_iota(jnp.int32, (CH, CH), 0)
    c = lax.broadcasted_iota(jnp.int32, (CH, CH), 1)
    tri = r >= c
    bcum = _cumsum_rows(glog)
    q_dec = q * (128.0 ** -0.5) * jnp.exp(bcum)
    attn = jnp.where(tri, _bmm_nt(q_dec, k * jnp.exp(-bcum)), 0.0)
    o = _bmm(attn, v) + _bmm(q_dec, S)
    k_dec = k * jnp.exp(jnp.sum(glog, axis=0, keepdims=True) - bcum)
    decay = jnp.exp(_colsum_as_rows(glog, v.shape[1]))
    return o, S * decay + _bmm_tn(k_dec, v)


def chunk_fwd(name, chunk_fn, ins, dv):
    M = ins[0][0].shape[0]
    N = M // CH
    ni = len(ins)
    ws = [w for (_, w, _) in ins]

    def body(*refs):
        o_ref, sall_ref, s_ref = refs[ni:]

        @pl.when(pl.program_id(0) == 0)
        def _():
            s_ref[...] = jnp.zeros_like(s_ref)

        for h in range(4):
            S = s_ref[h]
            sall_ref[h] = S
            o, s_new = chunk_fn(*[r[:, h * w:(h + 1) * w] for r, w in zip(refs[:ni], ws)], S)
            o_ref[:, h * dv:(h + 1) * dv] = o
            s_ref[h] = s_new

    specs = [pl.BlockSpec((CH, 4 * w), functools.partial(lambda n, cb: (n, cb), cb=cb // 4)) for (_, w, cb) in ins]
    return pl.pallas_call(
        body, name=name, grid=(N,),
        in_specs=specs,
        out_specs=[pl.BlockSpec((CH, 4 * dv), lambda n: (n, 0)),
                   pl.BlockSpec((4, None, 128, dv), lambda n: (0, n, 0, 0))],
        out_shape=[jax.ShapeDtypeStruct((M, 4 * dv), F32), jax.ShapeDtypeStruct((4, N, 128, dv), F32)],
        scratch_shapes=[pltpu.VMEM((4, 128, dv), F32)],
        compiler_params=_cparams(("arbitrary",)),
    )(*[a for (a, _, _) in ins])


def chunk_bwd(name, chunk_fn, ins, dv, s_all, do):
    M = ins[0][0].shape[0]
    N = M // CH
    ni = len(ins)
    ws = [w for (_, w, _) in ins]

    def body(*refs):
        sall_ref, do_ref = refs[ni:ni + 2]
        d_refs = refs[ni + 2:2 * ni + 2]
        ds_ref = refs[2 * ni + 2]

        @pl.when(pl.program_id(0) == 0)
        def _():
            ds_ref[...] = jnp.zeros_like(ds_ref)

        for h in range(4):
            _, vjp = jax.vjp(chunk_fn, *[r[:, h * w:(h + 1) * w] for r, w in zip(refs[:ni], ws)], sall_ref[h])
            grads = vjp((do_ref[:, h * dv:(h + 1) * dv], ds_ref[h]))
            for d_ref, w, g in zip(d_refs, ws, grads[:ni]):
                d_ref[:, h * w:(h + 1) * w] = g
            ds_ref[h] = grads[ni]

    rev = lambda n: N - 1 - n
    specs = [pl.BlockSpec((CH, 4 * w), functools.partial(lambda n, cb: (rev(n), cb), cb=cb // 4)) for (_, w, cb) in ins]
    return pl.pallas_call(
        body, name=name, grid=(N,),
        in_specs=specs + [pl.BlockSpec((4, None, 128, dv), lambda n: (0, rev(n), 0, 0)),
                          pl.BlockSpec((CH, 4 * dv), lambda n: (rev(n), 0))],
        out_specs=[pl.BlockSpec((CH, 4 * w), lambda n: (rev(n), 0)) for w in ws],
        out_shape=[jax.ShapeDtypeStruct((M, 4 * w), F32) for w in ws],
        scratch_shapes=[pltpu.VMEM((4, 128, dv), F32)],
        compiler_params=_cparams(("arbitrary",)),
    )(*[a for (a, _, _) in ins], s_all, do)


def _fn_gate_out(hd, row0, o, z, w):
    outs = []
    for h in range(4):
        outs.append(_rms(o[:, h * hd:(h + 1) * hd], w) * _silu(z[:, h * hd:(h + 1) * hd]))
    return (jnp.concatenate(outs, axis=1),)


def _fn_gla_prep(row0, gk, wgu, bg):
    x = _mm(gk, wgu) + bg
    ls = jnp.minimum(x, 0.0) - jnp.log(1.0 + jnp.exp(-jnp.abs(x)))
    return (jnp.where(_row_ids(row0, gk.shape[0]) >= ZROWS, ls / 16.0, 0.0),)


def loss_call(y, tgt):
    M = y.shape[0]
    tm = _pick(M, 512)

    def body(y_ref, t_ref, l_ref, dy_ref):
        i = pl.program_id(0)
        e = y_ref[...] - t_ref[...]
        dy_ref[...] = e * (1.0 / D)
        part = 0.5 * jnp.sum(jnp.sum(e * e, axis=1, keepdims=True) * (1.0 / D), axis=0, keepdims=True)
        part = jnp.broadcast_to(part, (8, LANE))

        @pl.when(i == 0)
        def _():
            l_ref[...] = part

        @pl.when(i > 0)
        def _():
            l_ref[...] += part

    return pl.pallas_call(
        body, name="loss", grid=(M // tm,),
        in_specs=[pl.BlockSpec((tm, D), lambda i: (i, 0))] * 2,
        out_specs=[pl.BlockSpec((8, LANE), lambda i: (0, 0)), pl.BlockSpec((tm, D), lambda i: (i, 0))],
        out_shape=[jax.ShapeDtypeStruct((8, LANE), F32), jax.ShapeDtypeStruct((M, D), F32)],
        compiler_params=_cparams(("arbitrary",)),
    )(y, tgt)


def _bf(x):
    return x.astype(BF16)


def core_step(x, tgt, W):
    S = x.shape[0]
    M = S + PADR
    ids = jnp.asarray(_swa_bucket_ids())
    h0 = jnp.concatenate([jnp.zeros((ZROWS, D), F32), W["meta"], x], axis=0)
    nw = W["norm"]
    nrow = lambda l, k: nw[l, k][None, :]
    wg, wu, wd = W["ffn_g"], W["ffn_u"], W["ffn_d"]
    e_in, e_out = W["e_in"], W["e_out"]
    o_in, o_out = W["o_in"], W["o_out"]
    sinks = jnp.pad(W["sinks"], ((0, 0), (0, LANE - 8)))
    dnp = jnp.pad(jnp.concatenate([W["a_log"], W["dt_bias"]], axis=1), ((0, 0), (0, LANE - 8)))
    wgu = jnp.pad(W["gate_up"], ((0, LANE - 16), (0, 0)))
    bg = W["b_gate"]
    full = lambda a: (a, a.shape[1], 0)

    saved = []
    h = h0
    (hn,) = rowwise_fwd("prenorm_0", _fn_prenorm, [full(h)], [nrow(0, 0)], [(D, BF16)])
    bias = swa_bias_fwd(W["rel"], ids)
    for l in range(2):
        st = {"h_a": h, "hn_a": hn}
        f1, a1, b1 = ffn_fwd(f"ffn_fwd_{l}0", hn, wg, wu, wd, l, 0)
        h, hn = rowwise_fwd(f"resnorm_{l}1", functools.partial(_fn_resnorm, 0.5), [full(h), full(f1)],
                            [nrow(l, 1), nrow(l, 2)], [(D, F32), (D, BF16)])
        st.update(f1=f1, a1=a1, b1=b1, h_b=h, hn_b=hn)
        if l == 0:
            proj = mm_nt(hn, e_in, "e_proj")
            o_a = swa_fwd(proj, bias, sinks)
            y = conv_fwd(proj, W["conv"])
            qn, kn, gb, bb = rowwise_fwd(
                "dn_prep", _fn_dn_prep, [(y, 512, 0), (y, 512, 1), (proj, LANE, E_BA // LANE)], [dnp], [(512, F32)] * 4)
            ins = [(qn, 128, 0), (kn, 128, 0), (y, 128, 8), (gb, 128, 0), (bb, 128, 0)]
            o_dn, s_all = chunk_fwd("dn_fwd", _dn_chunk, ins, 128)
            (o_b,) = rowwise_fwd("dn_out", functools.partial(_fn_gate_out, 128),
                                 [full(o_dn), (proj, 512, E_ZB // 512)], [W["dn_norm"]], [(512, BF16)])
            omix = jnp.concatenate([o_a, o_b], axis=1)
            mix = mm_nn(omix, e_out, "e_mix")
            st.update(proj=proj, y=y, qn=qn, kn=kn, gb=gb, bb=bb, o_dn=o_dn, s_all=s_all, omix=omix)
        else:
            proj = mm_nt(hn, o_in, "o_proj")
            (glog,) = rowwise_fwd("gla_prep", _fn_gla_prep, [(proj, LANE, O_GK // LANE)], [wgu, bg], [(512, F32)])
            ins = [(proj, 128, O_Q // 128), (proj, 128, O_K // 128), (proj, 256, O_V // 256), (glog, 128, 0)]
            o_g, s_all = chunk_fwd("gla_fwd", _gla_chunk, ins, 256)
            (omix,) = rowwise_fwd("gla_out", functools.partial(_fn_gate_out, 256),
                                  [full(o_g), (proj, 1024, O_G // 1024)], [W["gla_norm"]], [(1024, BF16)])
            mix = mm_nn(omix, o_out, "o_mix")
            st.update(proj=proj, glog=glog, o_g=o_g, s_all=s_all, omix=omix)
        h, hn = rowwise_fwd(f"resnorm_{l}3", functools.partial(_fn_resnorm, 1.0), [full(h), full(mix)],
                            [nrow(l, 3), nrow(l, 4)], [(D, F32), (D, BF16)])
        st.update(mix=mix, h_c=h, hn_c=hn)
        f2, a2, b2 = ffn_fwd(f"ffn_fwd_{l}1", hn, wg, wu, wd, l, 1)
        st.update(f2=f2, a2=a2, b2=b2)
        if l == 0:
            h, hn = rowwise_fwd("resnorm_05", functools.partial(_fn_resnorm, 0.5), [full(h), full(f2)],
                                [nrow(0, 5), nrow(1, 0)], [(D, F32), (D, BF16)])
        else:
            (h,) = rowwise_fwd("res_last", functools.partial(_fn_res_last, 0.5), [full(h), full(f2)],
                               [nrow(1, 5)], [(D, F32)])
        saved.append(st)

    loss_blk, dy = loss_call(h[PADR:], tgt)
    dh = jnp.concatenate([jnp.zeros((PADR, D), F32), dy], axis=0)

    G = {}
    dnorm = [[None] * 6 for _ in range(2)]
    dWg = [[None, None], [None, None]]
    dWu = [[None, None], [None, None]]
    dWd = [[None, None], [None, None]]
    dhn = None
    for l in (1, 0):
        st = saved[l]
        if l == 1:
            (dh_, df), (dw5,) = rowwise_bwd(
                "res_last_b", functools.partial(_fn_res_last, 0.5), [full(st["h_c"]), full(st["f2"])], [nrow(1, 5)],
                [full(dh)], [F32, BF16])
            dnorm[1][5] = dw5
        else:
            (dh_, df), (dw5, dw0n) = rowwise_bwd(
                "resnorm_05_b", functools.partial(_fn_resnorm, 0.5), [full(st["h_c"]), full(st["f2"])],
                [nrow(0, 5), nrow(1, 0)], [full(dh), full(dhn)], [F32, BF16])
            dnorm[0][5] = dw5
            dnorm[1][0] = dw0n
        dh = dh_
        dxn, da, db, hm = ffn_bwd_x(f"ffn_bx_{l}1", df, st["a2"], st["b2"], wg, wu, wd, l, 1)
        dWg[l][1], dWu[l][1], dWd[l][1] = ffn_bwd_w(f"ffn_bw_{l}1", st["hn_c"], df, da, db, hm)
        (dh_, dmix), (dw3, dw4) = rowwise_bwd(
            f"resnorm_{l}3_b", functools.partial(_fn_resnorm, 1.0), [full(st["h_b"]), full(st["mix"])],
            [nrow(l, 3), nrow(l, 4)], [full(dh), full(dxn)], [F32, BF16])
        dnorm[l][3], dnorm[l][4] = dw3, dw4
        dh = dh_
        proj = st["proj"]
        if l == 1:
            G["o_out"] = mm_tn(st["omix"], dmix, "o_out_dw")
            domix = mm_nt(dmix, o_out, "o_mix_dx")
            (do_g, dgate), (dgn,) = rowwise_bwd(
                "gla_out_b", functools.partial(_fn_gate_out, 256), [full(st["o_g"]), (proj, 1024, O_G // 1024)],
                [W["gla_norm"]], [full(domix)], [F32, F32])
            G["gla_norm"] = dgn
            ins = [(proj, 128, O_Q // 128), (proj, 128, O_K // 128), (proj, 256, O_V // 256), (st["glog"], 128, 0)]
            dq, dk, dv, dglog = chunk_bwd("gla_bwd", _gla_chunk, ins, 256, st["s_all"], do_g)
            (dgk,), (dwgu, dbg) = rowwise_bwd("gla_prep_b", _fn_gla_prep, [(proj, LANE, O_GK // LANE)], [wgu, bg],
                                              [full(dglog)], [F32])
            G["gate_up"] = dwgu[:16]
            G["b_gate"] = dbg
            dproj = _bf(jnp.concatenate([dq, dk, dv, dgate, dgk, jnp.zeros((M, O_END - O_GK - LANE), F32)], axis=1))
            G["o_in"] = mm_tn(dproj, st["hn_b"], "o_in_dw")
            dhn_b = mm_nn(dproj, o_in, "o_proj_dx")
        else:
            G["e_out"] = mm_tn(st["omix"], dmix, "e_out_dw")
            domix = mm_nt(dmix, e_out, "e_mix_dx")
            (do_dn, dz), (ddn,) = rowwise_bwd(
                "dn_out_b", functools.partial(_fn_gate_out, 128), [full(st["o_dn"]), (proj, 512, E_ZB // 512)],
                [W["dn_norm"]], [(domix, 512, 2)], [F32, F32])
            G["dn_norm"] = ddn
            ins = [(st["qn"], 128, 0), (st["kn"], 128, 0), (st["y"], 128, 8), (st["gb"], 128, 0), (st["bb"], 128, 0)]
            dqn, dkn, dvv, dgb, dbb = chunk_bwd("dn_bwd", _dn_chunk, ins, 128, st["s_all"], do_dn)
            (dyq, dyk, dba), (ddnp,) = rowwise_bwd(
                "dn_prep_b", _fn_dn_prep, [(st["y"], 512, 0), (st["y"], 512, 1), (proj, LANE, E_BA // LANE)], [dnp],
                [full(dqn), full(dkn), full(dgb), full(dbb)], [F32, F32, F32])
            G["a_log"] = ddnp[:, 0:4]
            G["dt_bias"] = ddnp[:, 4:8]
            dyc = jnp.concatenate([dyq, dyk, dvv], axis=1)
            dxc, dconv = conv_bwd(proj, W["conv"], dyc)
            G["conv"] = dconv
            dq_a, dk_a, dv_a, dbias, dsink = swa_bwd(proj, bias, sinks, domix)
            G["sinks"] = dsink[:, :8]
            G["rel"] = swa_bias_bwd(dbias, ids)[:, :8]
            dproj = _bf(jnp.concatenate([dq_a, dk_a, dv_a, dxc, dz, dba, jnp.zeros((M, E_END - E_BA - LANE), F32)], axis=1))
            G["e_in"] = mm_tn(dproj, st["hn_b"], "e_in_dw")
            dhn_b = mm_nn(dproj, e_in, "e_proj_dx")
        (dh_, df), (dw1, dw2) = rowwise_bwd(
            f"resnorm_{l}1_b", functools.partial(_fn_resnorm, 0.5), [full(st["h_a"]), full(st["f1"])],
            [nrow(l, 1), nrow(l, 2)], [full(dh), full(dhn_b)], [F32, BF16])
        dnorm[l][1], dnorm[l][2] = dw1, dw2
        dh = dh_
        dxn, da, db, hm = ffn_bwd_x(f"ffn_bx_{l}0", df, st["a1"], st["b1"], wg, wu, wd, l, 0)
        dWg[l][0], dWu[l][0], dWd[l][0] = ffn_bwd_w(f"ffn_bw_{l}0", st["hn_a"], df, da, db, hm)
        dhn = dxn
    (dh0p,), (dw00,) = rowwise_bwd("prenorm_0_b", _fn_prenorm, [full(saved[0]["h_a"])], [nrow(0, 0)], [full(dhn)], [F32])
    dnorm[0][0] = dw00
    dh = dh + dh0p
    G["meta"] = dh[ZROWS:PADR]
    G["norm"] = jnp.stack([jnp.concatenate(r, axis=0) for r in dnorm], axis=0)
    G["ffn_g"], G["ffn_u"], G["ffn_d"] = dWg, dWu, dWd
    return loss_blk, dh[PADR:], G


NAMES = [("meta", "meta_tokens"), ("norm", "norm_w"), ("ffn_g", "ffn_w_gate"), ("ffn_u", "ffn_w_up"),
         ("ffn_d", "ffn_w_down"), ("rel", "rel_bias_table"), ("e_in", "even_w_in"), ("conv", "even_conv_w"),
         ("sinks", "swa_sinks"), ("a_log", "dn_a_log"), ("dt_bias", "dn_dt_bias"), ("dn_norm", "dn_norm_w"),
         ("e_out", "even_w_out"), ("o_in", "odd_w_in"), ("gate_up", "gla_w_gate_up"), ("b_gate", "gla_b_gate"),
         ("gla_norm", "gla_norm_w"), ("o_out", "odd_w_out")]
BIG = ["ffn_g", "ffn_u", "ffn_d", "e_in", "e_out", "o_in", "o_out"]
IN_ROWS = 800
SMALL = [("meta", (16, 256)), ("norm", (2, 6, 256)), ("conv", (1, 4, 384)), ("gate_up", (1, 16, 128)),
         ("b_gate", (1, 128)), ("gla_norm", (1, 64))]
REPL = [("rel", (32, 8)), ("sinks", (1, 8)), ("a_log", (1, 4)), ("dt_bias", (1, 4)), ("dn_norm", (1, 128))]
SMALL_REP = 88 * LANE
SMALL_ROWS = 96


def pack_small(t):
    a = jnp.concatenate([t[n].reshape(-1) for n, _ in SMALL])
    b = jnp.concatenate([t[n].reshape(-1) for n, _ in REPL])
    flat = jnp.concatenate([a, jnp.zeros((SMALL_REP - a.shape[0],), F32), b,
                            jnp.zeros((SMALL_ROWS * LANE - SMALL_REP - b.shape[0],), F32)])
    return flat.reshape(SMALL_ROWS, LANE)


def unpack_small(p):
    flat = p.reshape(-1)
    out, r = {}, 0
    for n, shp in SMALL:
        k = int(np.prod(shp))
        out[n] = flat[r:r + k].reshape(shp)
        r += k
    r = SMALL_REP
    for n, shp in REPL:
        k = int(np.prod(shp))
        out[n] = flat[r:r + k].reshape(shp)
        r += k
    return out


def full_from_gathered(gb, gs):
    sm = [unpack_small(gs[s]) for s in range(NSH)]
    full = {n: gb[n] for n in ("ffn_g", "ffn_u", "ffn_d")}
    for n in ("e_in", "o_in", "e_out", "o_out"):
        full[n] = gb[n].reshape(-1, D)
    full["meta"] = jnp.concatenate([sm[s]["meta"] for s in range(NSH)], axis=1)
    full["norm"] = jnp.concatenate([sm[s]["norm"] for s in range(NSH)], axis=2)
    full["conv"] = jnp.concatenate([sm[s]["conv"][0] for s in range(NSH)], axis=1)
    full["gate_up"] = jnp.concatenate([sm[s]["gate_up"][0] for s in range(NSH)], axis=1)
    full["b_gate"] = jnp.concatenate([sm[s]["b_gate"] for s in range(NSH)], axis=1)
    full["gla_norm"] = jnp.concatenate([sm[s]["gla_norm"] for s in range(NSH)], axis=1)
    return full


def make_W(full, rep):
    W = dict(full)
    W.update(rep)
    W["e_in"] = _take_pad(full["e_in"], _even_in_map(), 0).astype(BF16)
    W["e_out"] = _take_pad(full["e_out"], _even_out_map(), 0).astype(BF16)
    W["o_in"] = _take_pad(full["o_in"], _odd_in_map(), 0).astype(BF16)
    W["o_out"] = full["o_out"].astype(BF16)
    for n in ("ffn_g", "ffn_u", "ffn_d"):
        W[n] = full[n].astype(BF16)
    for n in ("meta", "norm", "conv", "gate_up", "b_gate", "gla_norm"):
        W[n] = full[n].astype(F32)
    return W


def _col_sh(w):
    return jnp.moveaxis(w.reshape(w.shape[0], NSH, w.shape[1] // NSH), 1, 0)


def grad_items(G):
    ei = jnp.take(G["e_in"], jnp.asarray(_inverse(_even_in_map(), 2824)), axis=0).reshape(NSH, -1, D)
    ei = jnp.pad(ei, ((0, 0), (0, IN_ROWS - ei.shape[1]), (0, 0)))
    oi = jnp.take(G["o_in"], jnp.asarray(_inverse(_odd_in_map(), 3088)), axis=0).reshape(NSH, -1, D)
    oi = jnp.pad(oi, ((0, 0), (0, IN_ROWS - oi.shape[1]), (0, 0)))
    eo = jnp.take(G["e_out"], jnp.asarray(_inverse(_even_out_map(), 1024)), axis=0).reshape(NSH, 256, D)
    oo = G["o_out"].reshape(NSH, 256, D)
    proj = [(_bf(ei), _bf(eo)), (_bf(oi), _bf(oo))]
    return [[G["ffn_g"][l][0], G["ffn_g"][l][1], G["ffn_u"][l][0], G["ffn_u"][l][1], G["ffn_d"][l][0], G["ffn_d"][l][1],
             proj[l][0], proj[l][1]] for l in range(2)]


def shard_grads(f0, f1):
    st = lambda a, b: jnp.stack([jnp.stack([f0[a], f0[b]]), jnp.stack([f1[a], f1[b]])])
    return {"ffn_g": st(0, 1), "ffn_u": st(2, 3), "ffn_d": st(4, 5), "e_in": f0[6][:706], "e_out": f0[7],
            "o_in": f1[6][:772], "o_out": f1[7]}


def _to_t(n, a):
    if n in ("ffn_g", "ffn_u"):
        return jnp.swapaxes(a, 2, 3)
    if n in ("e_in", "o_in"):
        return jnp.swapaxes(a[0], 0, 1)
    return a if n == "ffn_d" else a[0]


def _from_t(n, a):
    if n in ("ffn_g", "ffn_u"):
        return jnp.swapaxes(a, 2, 3)
    if n in ("e_in", "o_in"):
        return jnp.swapaxes(a, 0, 1)[None]
    return a if n == "ffn_d" else a[None]


def pack_small_grads(G):
    col_sh = _col_sh
    norm_sh = jnp.moveaxis(G["norm"].reshape(2, 6, NSH, 256), 2, 0)
    a = jnp.concatenate([col_sh(G["meta"]).reshape(NSH, -1), norm_sh.reshape(NSH, -1), col_sh(G["conv"]).reshape(NSH, -1),
                         col_sh(G["gate_up"]).reshape(NSH, -1), G["b_gate"].reshape(NSH, -1),
                         G["gla_norm"].reshape(NSH, -1)], axis=1)
    b = jnp.concatenate([G[n].reshape(-1) for n, _ in REPL])
    b = jnp.broadcast_to(b[None], (NSH, b.shape[0]))
    small = jnp.concatenate([a, jnp.zeros((NSH, SMALL_REP - a.shape[1]), F32), b,
                             jnp.zeros((NSH, SMALL_ROWS * LANE - SMALL_REP - b.shape[1]), F32)], axis=1)
    return small.reshape(NSH, SMALL_ROWS, LANE)


MESH = pl.DeviceIdType.MESH
ANY = pl.BlockSpec(memory_space=pl.ANY)
VMEM = pl.BlockSpec(memory_space=pltpu.VMEM)


def _place():
    return lax.axis_index("x"), lax.axis_index("y"), lax.axis_index("c")


def _other_chips(x, y):
    return [(1 - x, y), (x, 1 - y), (1 - x, 1 - y)]


def _rcopy(send_sems, recv_sems, k, src, dst, to):
    return pltpu.make_async_remote_copy(src_ref=src, dst_ref=dst, send_sem=send_sems.at[k], recv_sem=recv_sems.at[k],
                                        device_id=to, device_id_type=MESH)


_AG_ITEMS = [[(0, (0,)), (1, (0,)), (2, (0,)), (3, ()), (4, ())], [(0, (1,)), (1, (1,)), (2, (1,)), (5, ()), (6, ())]]


def ag_big(shards):
    n = len(shards)
    ni = len(_AG_ITEMS[0])

    def body(*refs):
        in_refs, out_refs = refs[:n], refs[n:2 * n]
        send_sems, recv_sems = refs[2 * n:]
        x, y, c = _place()
        s = 2 * x + y
        chips = _other_chips(x, y)
        copy = functools.partial(_rcopy, send_sems, recv_sems)
        src = lambda it: in_refs[it[0]].at[it[1]] if it[1] else in_refs[it[0]]
        dst = lambda it, slot: out_refs[it[0]].at[(slot,) + it[1]]
        for cc in range(2):
            @pl.when(c == cc)
            def _():
                mine, theirs = _AG_ITEMS[cc], _AG_ITEMS[1 - cc]
                first, passed = [], []
                for i, it in enumerate(mine):
                    for j, (cx, cy) in enumerate(chips):
                        first.append(copy(i * 3 + j, src(it), dst(it, s), (cx, cy, c)))
                for cp in first:
                    cp.start()
                for j, (cx, cy) in enumerate(chips):
                    for i, it in enumerate(mine):
                        blk = dst(it, 2 * cx + cy)
                        copy(i * 3 + j, blk, blk, (x, y, c)).wait_recv()
                        p = copy(ni * 3 + i * 3 + j, blk, blk, (x, y, 1 - c))
                        p.start()
                        passed.append(p)
                for j, (cx, cy) in enumerate(chips):
                    for i, it in enumerate(theirs):
                        blk = dst(it, 2 * cx + cy)
                        copy(ni * 3 + i * 3 + j, blk, blk, (x, y, c)).wait_recv()
                for cp in first + passed:
                    cp.wait_send()

    return pl.pallas_call(
        body, name="ag_big", in_specs=[ANY] * n, out_specs=[ANY] * n,
        out_shape=[jax.ShapeDtypeStruct((NSH,) + a.shape, a.dtype) for a in shards],
        scratch_shapes=[pltpu.SemaphoreType.DMA((6 * ni,)), pltpu.SemaphoreType.DMA((6 * ni,))],
    )(*shards)


def ag_small(pack):
    def body(x_ref, out_ref, send_sems, recv_sems):
        x, y, c = _place()
        s = 2 * x + y
        chips = _other_chips(x, y)

        def copy(k, src, dst, to):
            return pltpu.make_async_remote_copy(src_ref=src, dst_ref=dst, send_sem=send_sems.at[k], recv_sem=recv_sems.at[k],
                                                device_id=to, device_id_type=MESH)

        out_ref[s] = x_ref[...]
        sends = [copy(j, x_ref, out_ref.at[s], (cx, cy, c)) for j, (cx, cy) in enumerate(chips)]
        for cp in sends:
            cp.start()
        for j, (cx, cy) in enumerate(chips):
            blk = out_ref.at[2 * cx + cy]
            copy(j, blk, blk, (x, y, c)).wait_recv()
        for cp in sends:
            cp.wait_send()

    return pl.pallas_call(
        body, name="ag_small", in_specs=[VMEM], out_specs=VMEM,
        out_shape=jax.ShapeDtypeStruct((NSH,) + pack.shape, pack.dtype),
        scratch_shapes=[pltpu.SemaphoreType.DMA((3,)), pltpu.SemaphoreType.DMA((3,))],
    )(pack)


def rs_pair(items):
    ni = len(items[0])

    def body(*refs):
        in_refs = [refs[:ni], refs[ni:2 * ni]]
        recv_refs = refs[2 * ni:3 * ni]
        send_sems, recv_sems = refs[3 * ni:]
        x, y, c = _place()
        copy = functools.partial(_rcopy, send_sems, recv_sems)
        for cc in range(2):
            @pl.when(c == cc)
            def _():
                cps = [copy(i * NSH + s, in_refs[1 - cc][i].at[s], recv_refs[i].at[s], (x, y, 1 - c))
                       for i in range(ni) for s in range(NSH)]
                for cp in cps:
                    cp.start()
                for cp in cps:
                    cp.wait()

    return pl.pallas_call(
        body, name="rs_pair", in_specs=[ANY] * (2 * ni), out_specs=[ANY] * ni,
        out_shape=[jax.ShapeDtypeStruct(a.shape, a.dtype) for a in items[0]],
        scratch_shapes=[pltpu.SemaphoreType.DMA((ni * NSH,)), pltpu.SemaphoreType.DMA((ni * NSH,))],
    )(*items[0], *items[1])


def rs_chips(arrs):
    n = len(arrs)

    def body(*refs):
        a_refs, out_refs = refs[:n], refs[n:2 * n]
        send_sems, recv_sems = refs[2 * n:]
        x, y, c = _place()
        s = 2 * x + y
        chips = _other_chips(x, y)
        copy = functools.partial(_rcopy, send_sems, recv_sems)
        sends = [copy(i * 3 + j, a_refs[i].at[2 * cx + cy], out_refs[i].at[s], (cx, cy, c))
                 for i in range(n) for j, (cx, cy) in enumerate(chips)]
        for cp in sends:
            cp.start()
        for i in range(n):
            for j, (cx, cy) in enumerate(chips):
                blk = out_refs[i].at[2 * cx + cy]
                copy(i * 3 + j, blk, blk, (x, y, c)).wait_recv()
        for cp in sends:
            cp.wait_send()

    return pl.pallas_call(
        body, name="rs_chips", in_specs=[ANY] * n, out_specs=[ANY] * n,
        out_shape=[jax.ShapeDtypeStruct(a.shape, a.dtype) for a in arrs],
        scratch_shapes=[pltpu.SemaphoreType.DMA((3 * n,)), pltpu.SemaphoreType.DMA((3 * n,))],
    )(*arrs)


PAIR_CHUNKS = 4


def ag_pair(arrs):
    n = len(arrs)

    def body(*refs):
        g_refs, out_refs = refs[:n], refs[n:2 * n]
        send_sems, recv_sems = refs[2 * n:]
        x, y, c = _place()
        give = []
        for i, a in enumerate(arrs):
            rc = a.shape[0] // PAIR_CHUNKS
            for k in range(PAIR_CHUNKS):
                rows = pl.ds(k * rc, rc)
                give.append(_rcopy(send_sems, recv_sems, i * PAIR_CHUNKS + k, g_refs[i].at[rows], out_refs[i].at[rows],
                                   (x, y, 1 - c)))
        for cp in give:
            cp.start()
        for cp in give:
            cp.wait()

    return pl.pallas_call(
        body, name="ag_pair", in_specs=[ANY] * n, out_specs=[ANY] * n,
        out_shape=[jax.ShapeDtypeStruct(a.shape, a.dtype) for a in arrs],
        scratch_shapes=[pltpu.SemaphoreType.DMA((n * PAIR_CHUNKS,)), pltpu.SemaphoreType.DMA((n * PAIR_CHUNKS,))],
    )(*arrs)


def small_allreduce(p):
    def body(p_ref, out_ref, rbuf, send_sems, recv_sems):
        x, y, c = _place()
        me = 4 * x + 2 * y + c
        rbuf[me] = p_ref[2 * x + y]
        flip = lambda v, f: (1 - v) if f else v
        peers = [(flip(x, k >> 2 & 1), flip(y, k >> 1 & 1), flip(c, k & 1)) for k in range(1, 8)]

        def copy(k, src, dst, to):
            return pltpu.make_async_remote_copy(src_ref=src, dst_ref=dst, send_sem=send_sems.at[k], recv_sem=recv_sems.at[k],
                                                device_id=to, device_id_type=MESH)

        sends = [copy(k, p_ref.at[2 * px + py], rbuf.at[me], (px, py, pc)) for k, (px, py, pc) in enumerate(peers)]
        for cp in sends:
            cp.start()
        for k, (px, py, pc) in enumerate(peers):
            blk = rbuf.at[4 * px + 2 * py + pc]
            copy(k, blk, blk, (x, y, c)).wait_recv()
        for cp in sends:
            cp.wait_send()
        acc = rbuf[0]
        for d in range(1, 8):
            acc = acc + rbuf[d]
        out_ref[...] = acc

    return pl.pallas_call(
        body, name="small_allreduce", in_specs=[VMEM], out_specs=VMEM,
        out_shape=jax.ShapeDtypeStruct(p.shape[1:], F32),
        scratch_shapes=[pltpu.VMEM((8,) + p.shape[1:], F32), pltpu.SemaphoreType.DMA((7,)), pltpu.SemaphoreType.DMA((7,))],
    )(p)


def _rows_tile(rows, cap):
    return _pick(rows, cap) if rows % 128 == 0 else rows


def sum_pair(name, a0, a1, recv, cflag):
    n, r, d = recv.shape
    tr = _pick(r, 1024) if r % 64 == 0 else r

    def body(c_ref, a0_ref, a1_ref, b_ref, o_ref):
        own = jnp.where(c_ref[0] == 0, a0_ref[...].astype(F32), a1_ref[...].astype(F32))
        o_ref[...] = (own + b_ref[...].astype(F32)).astype(o_ref.dtype)

    spec = pl.BlockSpec((None, tr, d), lambda s, i: (s, i, 0))
    return pl.pallas_call(
        body, name=name, grid=(n, r // tr), in_specs=[pl.BlockSpec(memory_space=pltpu.SMEM), spec, spec, spec],
        out_specs=spec, out_shape=jax.ShapeDtypeStruct(recv.shape, BF16), compiler_params=_cparams(("parallel", "parallel")),
    )(cflag, a0, a1, recv)


def sum_chips(name, parts, own, sflag):
    n, r, d = parts.shape
    tr = _pick(r, 1024) if r % 64 == 0 else r

    def body(s_ref, p_ref, a_ref, o_ref):
        acc = None
        for t in range(n):
            term = jnp.where(s_ref[0] == t, a_ref[t].astype(F32), p_ref[t].astype(F32))
            acc = term if acc is None else acc + term
        o_ref[...] = acc

    spec = pl.BlockSpec((n, tr, d), lambda i: (0, i, 0))
    return pl.pallas_call(
        body, name=name, grid=(r // tr,), in_specs=[pl.BlockSpec(memory_space=pltpu.SMEM), spec, spec],
        out_specs=pl.BlockSpec((tr, d), lambda i: (i, 0)), out_shape=jax.ShapeDtypeStruct((r, d), F32),
        compiler_params=_cparams(("parallel",)),
    )(sflag, parts, own)


ADAM_LR, ADAM_B1, ADAM_B2, ADAM_EPS, ADAM_WD, ADAM_STEP = 0.001, 0.9, 0.999, 1e-08, 0.01, 10


def adamw_call(name, w, g, m, v):
    rows, cols = w.shape
    tr = _rows_tile(rows, 512)

    def body(w_ref, g_ref, m_ref, v_ref, d_ref, nm_ref, nv_ref):
        g_ = g_ref[...]
        m_ = ADAM_B1 * m_ref[...] + (1.0 - ADAM_B1) * g_
        v_ = ADAM_B2 * v_ref[...] + (1.0 - ADAM_B2) * (g_ * g_)
        m_hat = m_ / (1.0 - ADAM_B1 ** ADAM_STEP)
        v_hat = v_ / (1.0 - ADAM_B2 ** ADAM_STEP)
        d_ref[...] = -ADAM_LR * (m_hat / (jnp.sqrt(v_hat) + ADAM_EPS) + ADAM_WD * w_ref[...])
        nm_ref[...] = m_
        nv_ref[...] = v_

    spec = pl.BlockSpec((tr, cols), lambda i: (i, 0))
    sh = jax.ShapeDtypeStruct((rows, cols), F32)
    return pl.pallas_call(
        body, name=name, grid=(rows // tr,), in_specs=[spec] * 4, out_specs=[spec] * 3, out_shape=[sh] * 3,
        compiler_params=_cparams(("parallel",)),
    )(w, g, m, v)


def kernel(x, meta_tokens, norm_w, ffn_w_gate, ffn_w_up, ffn_w_down, rel_bias_table, even_w_in, even_conv_w, swa_sinks, dn_a_log, dn_dt_bias, dn_norm_w, even_w_out, odd_w_in, gla_w_gate_up, gla_b_gate, gla_norm_w, odd_w_out, loss_target, m_meta_tokens, m_norm_w, m_ffn_w_gate, m_ffn_w_up, m_ffn_w_down, m_rel_bias_table, m_even_w_in, m_even_conv_w, m_swa_sinks, m_dn_a_log, m_dn_dt_bias, m_dn_norm_w, m_even_w_out, m_odd_w_in, m_gla_w_gate_up, m_gla_b_gate, m_gla_norm_w, m_odd_w_out, v_meta_tokens, v_norm_w, v_ffn_w_gate, v_ffn_w_up, v_ffn_w_down, v_rel_bias_table, v_even_w_in, v_even_conv_w, v_swa_sinks, v_dn_a_log, v_dn_dt_bias, v_dn_norm_w, v_even_w_out, v_odd_w_in, v_gla_w_gate_up, v_gla_b_gate, v_gla_norm_w, v_odd_w_out):
    ws = [meta_tokens, norm_w, ffn_w_gate, ffn_w_up, ffn_w_down, rel_bias_table, even_w_in, even_conv_w, swa_sinks, dn_a_log,
          dn_dt_bias, dn_norm_w, even_w_out, odd_w_in, gla_w_gate_up, gla_b_gate, gla_norm_w, odd_w_out]
    ms = [m_meta_tokens, m_norm_w, m_ffn_w_gate, m_ffn_w_up, m_ffn_w_down, m_rel_bias_table, m_even_w_in, m_even_conv_w,
          m_swa_sinks, m_dn_a_log, m_dn_dt_bias, m_dn_norm_w, m_even_w_out, m_odd_w_in, m_gla_w_gate_up, m_gla_b_gate,
          m_gla_norm_w, m_odd_w_out]
    vs = [v_meta_tokens, v_norm_w, v_ffn_w_gate, v_ffn_w_up, v_ffn_w_down, v_rel_bias_table, v_even_w_in, v_even_conv_w,
          v_swa_sinks, v_dn_a_log, v_dn_dt_bias, v_dn_norm_w, v_even_w_out, v_odd_w_in, v_gla_w_gate_up, v_gla_b_gate,
          v_gla_norm_w, v_odd_w_out]
    short = [n for n, _ in NAMES]
    w = dict(zip(short, ws))
    m = dict(zip(short, ms))
    v = dict(zip(short, vs))

    wt = {n: _to_t(n, w[n]) for n in BIG}
    own = [wt[n].astype(BF16) for n in BIG]
    sflag = (2 * lax.axis_index("x") + lax.axis_index("y")).astype(jnp.int32).reshape(1)
    gb = {n: lax.dynamic_update_index_in_dim(g_, o_, sflag[0], 0) for n, g_, o_ in zip(BIG, ag_big(own), own)}
    gs = ag_small(pack_small(w))
    W = make_W(full_from_gathered(gb, gs), {n: w[n] for n, _ in REPL})

    loss_blk, gx, G = core_step(x[0], loss_target[0], W)

    items = grad_items(G)
    cflag = lax.axis_index("c").astype(jnp.int32).reshape(1)
    recv = rs_pair(items)
    mine = [sum_pair(f"sum_pair_{i}", items[0][i], items[1][i], recv[i], cflag) for i in range(len(recv))]
    parts = rs_chips(mine)
    red = [sum_chips(f"sum_chips_{i}", p, a, sflag) for i, (p, a) in enumerate(zip(parts, mine))]
    got = ag_pair(red)
    is0 = cflag[0] == 0
    f0 = [jnp.where(is0, r_, g_) for r_, g_ in zip(red, got)]
    f1 = [jnp.where(is0, g_, r_) for r_, g_ in zip(red, got)]
    g_small_pack = small_allreduce(pack_small_grads(G))
    gt = shard_grads(f0, f1)
    g = {**{n: _from_t(n, gt[n]) for n in BIG}, **unpack_small(g_small_pack)}

    delta, new_m, new_v = {}, {}, {}
    for n in BIG:
        shp = wt[n].shape
        two = lambda t: t.reshape(-1, D)
        d_, m_, v_ = adamw_call("adamw_" + n, two(wt[n]), two(gt[n]), two(_to_t(n, m[n])), two(_to_t(n, v[n])))
        delta[n], new_m[n], new_v[n] = (_from_t(n, t.reshape(shp)) for t in (d_, m_, v_))
    d_, m_, v_ = adamw_call("adamw_small", pack_small(w), g_small_pack, pack_small(m), pack_small(v))
    delta.update(unpack_small(d_))
    new_m.update(unpack_small(m_))
    new_v.update(unpack_small(v_))

    loss = lax.psum(loss_blk[0, 0], ("x", "y", "c"))
    return (loss, gx[None], *[g[n] for n in short], *[delta[n] for n in short], *[new_m[n] for n in short],
            *[new_v[n] for n in short])
```

```python
import functools
import math

import numpy as np
import jax
import jax.numpy as jnp
from jax import lax
from jax.experimental import pallas as pl
from jax.experimental.pallas import tpu as pltpu

F32 = jnp.float32
BF16 = jnp.bfloat16
HI = lax.Precision.HIGHEST

D = 1024
N_META = 16
PADR = 128
ZROWS = PADR - N_META
D_FF = 2816
NSH = 4
FSH = D_FF // NSH
EPS = 1e-6
NEG = -1e30
CH = 64
BLK = 128
LANE = 128
VMEM_LIMIT = 56 * 1024 * 1024

E_QA, E_KA, E_VA, E_QB, E_KB, E_VB, E_ZB, E_BA, E_END = 0, 1024, 1280, 1536, 2048, 2560, 3072, 3584, 4096


def _even_in_map():
    m = np.full((E_END,), -1, np.int64)
    for h in range(8):
        m[E_QA + h * 128:E_QA + h * 128 + 64] = np.arange(h * 64, (h + 1) * 64)
    for h in range(2):
        m[E_KA + h * 128:E_KA + h * 128 + 64] = 512 + np.arange(h * 64, (h + 1) * 64)
        m[E_VA + h * 128:E_VA + h * 128 + 64] = 640 + np.arange(h * 64, (h + 1) * 64)
    m[E_QB:E_QB + 2048] = 768 + np.arange(2048)
    m[E_BA:E_BA + 8] = 2816 + np.arange(8)
    return m


def _even_out_map():
    m = np.full((1536,), -1, np.int64)
    for h in range(8):
        m[h * 128:h * 128 + 64] = np.arange(h * 64, (h + 1) * 64)
    m[1024:1536] = 512 + np.arange(512)
    return m


O_Q, O_K, O_V, O_G, O_GK, O_END = 0, 512, 1024, 2048, 3072, 3584


def _odd_in_map():
    m = np.full((O_END,), -1, np.int64)
    m[:3072] = np.arange(3072)
    m[O_GK:O_GK + 16] = 3072 + np.arange(16)
    return m


def _inverse(m, n):
    inv = np.zeros((n,), np.int64)
    for p, o in enumerate(m):
        if o >= 0:
            inv[o] = p
    return inv


def _take_pad(w, m, axis):
    t = jnp.take(w, jnp.asarray(np.maximum(m, 0)), axis=axis)
    shape = [1] * w.ndim
    shape[axis] = m.shape[0]
    return jnp.where(jnp.asarray(m >= 0).reshape(shape), t, jnp.zeros((), w.dtype))


def _mm(a, b, prec=HI):
    return lax.dot_general(a, b, (((1,), (0,)), ((), ())), precision=prec, preferred_element_type=F32)


def _mm_nt(a, b, prec=HI):
    return lax.dot_general(a, b, (((1,), (1,)), ((), ())), precision=prec, preferred_element_type=F32)


def _mm_tn(a, b, prec=HI):
    return lax.dot_general(a, b, (((0,), (0,)), ((), ())), precision=prec, preferred_element_type=F32)


def _bdot(a, b, dims):
    return lax.dot_general(a.astype(BF16), b.astype(BF16), (dims, ((), ())), preferred_element_type=F32)


@jax.custom_vjp
def _bmm(a, b):
    return _bdot(a, b, ((1,), (0,)))


@jax.custom_vjp
def _bmm_nt(a, b):
    return _bdot(a, b, ((1,), (1,)))


@jax.custom_vjp
def _bmm_tn(a, b):
    return _bdot(a, b, ((0,), (0,)))


_bmm.defvjp(lambda a, b: (_bmm(a, b), (a, b)), lambda r, g: (_bmm_nt(g, r[1]), _bmm_tn(r[0], g)))
_bmm_nt.defvjp(lambda a, b: (_bmm_nt(a, b), (a, b)), lambda r, g: (_bmm(g, r[1]), _bmm_tn(g, r[0])))
_bmm_tn.defvjp(lambda a, b: (_bmm_tn(a, b), (a, b)), lambda r, g: (_bmm_nt(r[1], g), _bmm(r[0], g)))


def _hi_lo(x):
    h = x.astype(BF16)
    return h, (x - h.astype(F32)).astype(BF16)


def _xdot(a, b, dims):
    ah, al = _hi_lo(a)
    bh, bl = _hi_lo(b)
    d = lambda p, q: lax.dot_general(p, q, (dims, ((), ())), preferred_element_type=F32)
    return d(ah, bh) + (d(ah, bl) + d(al, bh))


@jax.custom_vjp
def _xmm(a, b):
    return _xdot(a, b, ((1,), (0,)))


@jax.custom_vjp
def _xmm_nt(a, b):
    return _xdot(a, b, ((1,), (1,)))


@jax.custom_vjp
def _xmm_tn(a, b):
    return _xdot(a, b, ((0,), (0,)))


_xmm.defvjp(lambda a, b: (_xmm(a, b), (a, b)), lambda r, g: (_xmm_nt(g, r[1]), _xmm_tn(r[0], g)))
_xmm_nt.defvjp(lambda a, b: (_xmm_nt(a, b), (a, b)), lambda r, g: (_xmm(g, r[1]), _xmm_tn(g, r[0])))
_xmm_tn.defvjp(lambda a, b: (_xmm_tn(a, b), (a, b)), lambda r, g: (_xmm_nt(r[1], g), _xmm(r[0], g)))


def _sum01(m01, x, dims):
    h, l = _hi_lo(x)
    l2 = (x - h.astype(F32) - l.astype(F32)).astype(BF16)
    m = m01.astype(BF16)
    d = lambda q: lax.dot_general(m, q, (dims, ((), ())), preferred_element_type=F32)
    return d(h) + (d(l) + d(l2))


@jax.custom_vjp
def _cumsum_rows(x):
    n = x.shape[0]
    tri = lax.broadcasted_iota(jnp.int32, (n, n), 0) >= lax.broadcasted_iota(jnp.int32, (n, n), 1)
    return _sum01(tri, x, ((1,), (0,)))


def _cumsum_rows_b(_, g):
    n = g.shape[0]
    tri = lax.broadcasted_iota(jnp.int32, (n, n), 0) >= lax.broadcasted_iota(jnp.int32, (n, n), 1)
    return (_sum01(tri, g, ((0,), (0,))),)


_cumsum_rows.defvjp(lambda x: (_cumsum_rows(x), None), _cumsum_rows_b)


@functools.partial(jax.custom_vjp, nondiff_argnums=(1,))
def _colsum_as_rows(x, width):
    return _colsum_impl(x, width)


def _colsum_impl(x, width):
    h, l = _hi_lo(x)
    l2 = (x - h.astype(F32) - l.astype(F32)).astype(BF16)
    ones = jnp.ones((x.shape[0], width), BF16)
    d = lambda q: lax.dot_general(q, ones, (((0,), (0,)), ((), ())), preferred_element_type=F32)
    return d(h) + (d(l) + d(l2))


def _colsum_as_rows_f(x, width):
    return _colsum_impl(x, width), x.shape[0]


def _colsum_as_rows_b(width, n, g):
    return (_sum01(jnp.ones((n, width), F32), g, ((1,), (1,))),)


_colsum_as_rows.defvjp(_colsum_as_rows_f, _colsum_as_rows_b)


def _rms(x, w):
    return x * lax.rsqrt(jnp.mean(x * x, axis=-1, keepdims=True) + EPS) * w


def _sigmoid(x):
    return 1.0 / (1.0 + jnp.exp(-x))


def _silu(x):
    return x * _sigmoid(x)


def _softplus(x):
    return jnp.maximum(x, 0.0) + jnp.log(1.0 + jnp.exp(-jnp.abs(x)))


def _lane_pick(row, idx):
    lane = lax.broadcasted_iota(jnp.int32, row.shape, row.ndim - 1)
    return jnp.sum(jnp.where(lane == idx, row, 0.0), axis=-1, keepdims=True)


def _row_ids(row0, n):
    return row0 + lax.broadcasted_iota(jnp.int32, (n, 1), 0)


def _pick(m, cap):
    best = 64
    for t in range(64, min(m, cap) + 1, 64):
        if m % t == 0:
            best = t
    return best


def _cparams(sem):
    return pltpu.CompilerParams(dimension_semantics=sem, vmem_limit_bytes=VMEM_LIMIT)


def mm_nn(a, b, name, out_dtype=F32):
    M, K = a.shape
    N = b.shape[1]
    tm = _pick(M, 1408 if K <= 2048 else 704)
    tn = _pick(N, 512)

    def body(a_ref, b_ref, o_ref):
        o_ref[...] = _mm(a_ref[...], b_ref[...], None).astype(o_ref.dtype)

    return pl.pallas_call(
        body, name=name, grid=(N // tn, M // tm),
        in_specs=[pl.BlockSpec((tm, K), lambda j, i: (i, 0)), pl.BlockSpec((K, tn), lambda j, i: (0, j))],
        out_specs=pl.BlockSpec((tm, tn), lambda j, i: (i, j)),
        out_shape=jax.ShapeDtypeStruct((M, N), out_dtype),
        compiler_params=_cparams(("parallel", "parallel")),
    )(a, b)


def mm_nt(a, b, name, out_dtype=F32):
    M, K = a.shape
    N = b.shape[0]
    tm = _pick(M, 768)
    tn = _pick(N, 512)

    def body(a_ref, b_ref, o_ref):
        o_ref[...] = _mm_nt(a_ref[...], b_ref[...], None).astype(o_ref.dtype)

    return pl.pallas_call(
        body, name=name, grid=(N // tn, M // tm),
        in_specs=[pl.BlockSpec((tm, K), lambda j, i: (i, 0)), pl.BlockSpec((tn, K), lambda j, i: (j, 0))],
        out_specs=pl.BlockSpec((tm, tn), lambda j, i: (i, j)),
        out_shape=jax.ShapeDtypeStruct((M, N), out_dtype),
        compiler_params=_cparams(("parallel", "parallel")),
    )(a, b)


def mm_tn(a, b, name):
    M, K = a.shape
    N = b.shape[1]
    tk = _pick(K, 512)
    tn = _pick(N, 512)

    def body(a_ref, b_ref, o_ref):
        o_ref[...] = _mm_tn(a_ref[...], b_ref[...], None)

    return pl.pallas_call(
        body, name=name, grid=(K // tk, N // tn),
        in_specs=[pl.BlockSpec((M, tk), lambda i, j: (0, i)), pl.BlockSpec((M, tn), lambda i, j: (0, j))],
        out_specs=pl.BlockSpec((tk, tn), lambda i, j: (i, j)),
        out_shape=jax.ShapeDtypeStruct((K, N), F32),
        compiler_params=_cparams(("parallel", "parallel")),
    )(a, b)


def _row_specs(rows, tm):
    return [pl.BlockSpec((tm, w), functools.partial(lambda i, cb: (i, cb), cb=cb)) for (_, w, cb) in rows]


def _param_specs(params):
    return [pl.BlockSpec(p.shape, functools.partial(lambda i, nd: (0,) * nd, nd=p.ndim)) for p in params]


def rowwise_fwd(name, fn, rows, params, outs, tm=None):
    M = rows[0][0].shape[0]
    tm = tm or _pick(M, 704)
    nr, npar = len(rows), len(params)

    def body(*refs):
        row0 = pl.program_id(0) * tm
        vals = [r[...].astype(F32) for r in refs[:nr]] + [p[...] for p in refs[nr:nr + npar]]
        res = fn(row0, *vals)
        for o_ref, r in zip(refs[nr + npar:], res):
            o_ref[...] = r.astype(o_ref.dtype)

    return pl.pallas_call(
        body, name=name, grid=(M // tm,),
        in_specs=_row_specs(rows, tm) + _param_specs(params),
        out_specs=[pl.BlockSpec((tm, w), lambda i: (i, 0)) for (w, _) in outs],
        out_shape=[jax.ShapeDtypeStruct((M, w), dt) for (w, dt) in outs],
        compiler_params=_cparams(("parallel",)),
    )(*[r[0] for r in rows], *params)


def rowwise_bwd(name, fn, rows, params, douts, drow_dtypes, tm=None):
    M = rows[0][0].shape[0]
    tm = tm or _pick(M, 704)
    nr, npar, nd = len(rows), len(params), len(douts)
    want = [k for k, dt in enumerate(drow_dtypes) if dt is not None]

    def body(*refs):
        i = pl.program_id(0)
        row0 = i * tm
        vals = [r[...].astype(F32) for r in refs[:nr]] + [p[...] for p in refs[nr:nr + npar]]
        cots = tuple(d[...].astype(F32) for d in refs[nr + npar:nr + npar + nd])
        _, vjp = jax.vjp(functools.partial(fn, row0), *vals)
        grads = vjp(cots)
        o_refs = refs[nr + npar + nd:]
        for o_ref, k in zip(o_refs[:len(want)], want):
            o_ref[...] = grads[k].astype(o_ref.dtype)
        for o_ref, g in zip(o_refs[len(want):], grads[nr:]):
            @pl.when(i == 0)
            def _():
                o_ref[...] = g

            @pl.when(i > 0)
            def _():
                o_ref[...] += g

    res = pl.pallas_call(
        body, name=name, grid=(M // tm,),
        in_specs=_row_specs(rows, tm) + _param_specs(params) + _row_specs(douts, tm),
        out_specs=[pl.BlockSpec((tm, rows[k][1]), lambda i: (i, 0)) for k in want] + _param_specs(params),
        out_shape=[jax.ShapeDtypeStruct((M, rows[k][1]), drow_dtypes[k]) for k in want]
        + [jax.ShapeDtypeStruct(p.shape, F32) for p in params],
        compiler_params=_cparams(("arbitrary",)),
    )(*[r[0] for r in rows], *params, *[d[0] for d in douts])
    return res[:len(want)], res[len(want):]


def _fn_prenorm(row0, h, wpre):
    return (_rms(h, wpre),)


def _fn_resnorm(scale, row0, h, f, wpost, wpre):
    h2 = h + scale * _rms(f, wpost)
    return h2, _rms(h2, wpre)


def _fn_res_last(scale, row0, h, f, wpost):
    return (h + scale * _rms(f, wpost),)


def ffn_fwd(name, xn, wg, wu, wd, j):
    M = xn.shape[0]
    tm = _pick(M, 704)

    def body(x_ref, wg_ref, wu_ref, wd_ref, f_ref, a_ref, b_ref):
        s = pl.program_id(1)
        x = x_ref[...]
        a = _mm_nt(x, wg_ref[...], None)
        b = _mm_nt(x, wu_ref[...], None)
        hm = (_silu(a) * b).astype(BF16)
        c = _mm(hm, wd_ref[...], None)

        @pl.when(s == 0)
        def _():
            f_ref[...] = c

        @pl.when(s > 0)
        def _():
            f_ref[...] += c

        a_ref[...] = a.astype(BF16)
        b_ref[...] = b.astype(BF16)

    wspec = wdspec = pl.BlockSpec((None, None, FSH, D), lambda i, s: (s, j, 0, 0))
    abspec = pl.BlockSpec((None, tm, FSH), lambda i, s: (s, i, 0))
    return pl.pallas_call(
        body, name=name, grid=(M // tm, NSH),
        in_specs=[pl.BlockSpec((tm, D), lambda i, s: (i, 0)), wspec, wspec, wdspec],
        out_specs=[pl.BlockSpec((tm, D), lambda i, s: (i, 0)), abspec, abspec],
        out_shape=[jax.ShapeDtypeStruct((M, D), F32), jax.ShapeDtypeStruct((NSH, M, FSH), BF16),
                   jax.ShapeDtypeStruct((NSH, M, FSH), BF16)],
        compiler_params=_cparams(("parallel", "arbitrary")),
    )(xn, wg, wu, wd)


def ffn_bwd_x(name, df, a, b, wg, wu, wd, j):
    M = df.shape[0]
    tm = _pick(M, 704)

    def body(df_ref, a_ref, b_ref, wg_ref, wu_ref, wd_ref, dx_ref, da_ref, db_ref, hm_ref):
        s = pl.program_id(1)
        a_ = a_ref[...].astype(F32)
        b_ = b_ref[...].astype(F32)
        dh = _mm_nt(df_ref[...], wd_ref[...], None)
        sig = _sigmoid(a_)
        sil = a_ * sig
        da = (dh * b_ * (sig * (1.0 + a_ * (1.0 - sig)))).astype(BF16)
        db = (dh * sil).astype(BF16)
        c = _mm(da, wg_ref[...], None) + _mm(db, wu_ref[...], None)

        @pl.when(s == 0)
        def _():
            dx_ref[...] = c

        @pl.when(s > 0)
        def _():
            dx_ref[...] += c

        da_ref[...] = da
        db_ref[...] = db
        hm_ref[...] = (sil * b_).astype(BF16)

    wspec = wdspec = pl.BlockSpec((None, None, FSH, D), lambda i, s: (s, j, 0, 0))
    abspec = pl.BlockSpec((None, tm, FSH), lambda i, s: (s, i, 0))
    ab = jax.ShapeDtypeStruct((NSH, M, FSH), BF16)
    return pl.pallas_call(
        body, name=name, grid=(M // tm, NSH),
        in_specs=[pl.BlockSpec((tm, D), lambda i, s: (i, 0)), abspec, abspec, wspec, wspec, wdspec],
        out_specs=[pl.BlockSpec((tm, D), lambda i, s: (i, 0)), abspec, abspec, abspec],
        out_shape=[jax.ShapeDtypeStruct((M, D), F32), ab, ab, ab],
        compiler_params=_cparams(("parallel", "arbitrary")),
    )(df, a, b, wg, wu, wd)


def ffn_bwd_w(name, xn, df, da, db, hm):
    M = xn.shape[0]
    tm = _pick(M, 704)
    nt = M // tm

    def body(x_ref, df_ref, da_ref, db_ref, hm_ref, dwg_ref, dwu_ref, dwd_ref, ag, au, ad):
        i = pl.program_id(1)
        x = x_ref[...]
        g = _mm_tn(da_ref[...], x, None)
        u = _mm_tn(db_ref[...], x, None)
        d = _mm_tn(hm_ref[...], df_ref[...], None)

        @pl.when(i == 0)
        def _():
            ag[...] = g
            au[...] = u
            ad[...] = d

        @pl.when(i > 0)
        def _():
            ag[...] += g
            au[...] += u
            ad[...] += d

        @pl.when(i == nt - 1)
        def _():
            dwg_ref[...] = ag[...].astype(BF16)
            dwu_ref[...] = au[...].astype(BF16)
            dwd_ref[...] = ad[...].astype(BF16)

    xspec = pl.BlockSpec((tm, D), lambda s, i: (i, 0))
    abspec = pl.BlockSpec((None, tm, FSH), lambda s, i: (s, i, 0))
    return pl.pallas_call(
        body, name=name, grid=(NSH, nt),
        in_specs=[xspec, xspec, abspec, abspec, abspec],
        out_specs=[pl.BlockSpec((None, FSH, D), lambda s, i: (s, 0, 0))] * 3,
        out_shape=[jax.ShapeDtypeStruct((NSH, FSH, D), BF16)] * 3,
        scratch_shapes=[pltpu.VMEM((FSH, D), F32)] * 3,
        compiler_params=_cparams(("parallel", "arbitrary")),
    )(xn, df, da, db, hm)


def _t5_bucket_np(rel):
    n = np.maximum(rel, 0)
    n_f = np.maximum(n, 1).astype(np.float32)
    large = 16 + (np.log(n_f / np.float32(16)) / np.float32(math.log(8.0)) * np.float32(16)).astype(np.int32)
    large = np.minimum(large, 31)
    return np.where(n < 16, n, large).astype(np.int32)


def _swa_bucket_ids():
    qi = np.arange(BLK)[:, None]
    kj = np.arange(BLK)[None, :]
    out = np.full((3, BLK, 3 * BLK), -1, np.int32)
    for v in range(3):
        pos_q = v * BLK + qi - ZROWS
        rel_m = pos_q - (kj - ZROWS)
        ok_m = (kj >= ZROWS) & (rel_m >= 0) & (pos_q >= 0)
        out[v, :, 0:BLK] = np.where(ok_m, _t5_bucket_np(rel_m), -1)
        pos_kp = (v - 1) * BLK + kj - ZROWS
        rel_p = BLK + qi - kj
        ok_p = (pos_kp >= N_META) & (rel_p >= 0) & (rel_p < BLK) & np.full_like(ok_m, v >= 1)
        out[v, :, BLK:2 * BLK] = np.where(ok_p, _t5_bucket_np(rel_p), -1)
        pos_kc = v * BLK + kj - ZROWS
        rel_c = qi - kj
        ok_c = (pos_kc >= N_META) & (rel_c >= 0) & (rel_c < BLK)
        out[v, :, 2 * BLK:] = np.where(ok_c, _t5_bucket_np(rel_c), -1)
    return out


def swa_bias_fwd(table, ids):
    def body(t_ref, id_ref, o_ref):
        for v in range(3):
            for h in range(8):
                o_ref[v, h] = jnp.where(id_ref[v] < 0, NEG, 0.0)

            def step(b, carry):
                hit = id_ref[v] == b
                for h in range(8):
                    o_ref[v, h] += jnp.where(hit, t_ref[b, h], 0.0)
                return carry

            lax.fori_loop(0, 32, step, 0)

    return pl.pallas_call(
        body, name="swa_bias_fwd",
        in_specs=[pl.BlockSpec(memory_space=pltpu.SMEM), pl.BlockSpec(memory_space=pltpu.VMEM)],
        out_specs=pl.BlockSpec(memory_space=pltpu.VMEM),
        out_shape=jax.ShapeDtypeStruct((3, 8, BLK, 3 * BLK), F32),
        compiler_params=pltpu.CompilerParams(vmem_limit_bytes=VMEM_LIMIT),
    )(table, ids)


def swa_bias_bwd(dbias, ids):
    def body(d_ref, id_ref, o_ref):
        r = lax.broadcasted_iota(jnp.int32, (32, LANE), 0)
        c = lax.broadcasted_iota(jnp.int32, (32, LANE), 1)

        def step(b, acc):
            for v in range(3):
                hit = id_ref[v] == b
                for h in range(8):
                    m = jnp.where(hit, d_ref[v, h], 0.0)
                    s = jnp.sum(jnp.sum(m, axis=1, keepdims=True), axis=0, keepdims=True)
                    acc = acc + jnp.where((r == b) & (c == h), s, 0.0)
            return acc

        o_ref[...] = lax.fori_loop(0, 32, step, jnp.zeros((32, LANE), F32))

    return pl.pallas_call(
        body, name="swa_bias_bwd",
        in_specs=[pl.BlockSpec(memory_space=pltpu.VMEM), pl.BlockSpec(memory_space=pltpu.VMEM)],
        out_specs=pl.BlockSpec(memory_space=pltpu.VMEM),
        out_shape=jax.ShapeDtypeStruct((32, LANE), F32),
        compiler_params=pltpu.CompilerParams(vmem_limit_bytes=VMEM_LIMIT),
    )(dbias, ids)


def _swa_block(q, k3, v3, bias, sinks):
    outs = []
    for hk in range(2):
        kh = k3[:, hk * 128:(hk + 1) * 128]
        vh = v3[:, hk * 128:(hk + 1) * 128]
        for g in range(4):
            h = hk * 4 + g
            s = _bmm_nt(q[:, h * 128:(h + 1) * 128], kh) * 0.125 + bias[h]
            sink = _lane_pick(sinks, h)
            m = lax.stop_gradient(jnp.maximum(jnp.max(s, axis=-1, keepdims=True), sink))
            e = jnp.exp(s - m)
            den = jnp.sum(e, axis=-1, keepdims=True) + jnp.exp(sink - m)
            outs.append(_bmm(e / den, vh))
    return jnp.concatenate(outs, axis=1)


def _swa_in_specs():
    qs = pl.BlockSpec((BLK, 1024), lambda n: (n, E_QA // 1024))
    ks = [pl.BlockSpec((BLK, 256), lambda n: (0, E_KA // 256)),
          pl.BlockSpec((BLK, 256), lambda n: (jnp.maximum(n - 1, 0), E_KA // 256)),
          pl.BlockSpec((BLK, 256), lambda n: (n, E_KA // 256))]
    vs = [pl.BlockSpec((BLK, 256), lambda n: (0, E_VA // 256)),
          pl.BlockSpec((BLK, 256), lambda n: (jnp.maximum(n - 1, 0), E_VA // 256)),
          pl.BlockSpec((BLK, 256), lambda n: (n, E_VA // 256))]
    bs = pl.BlockSpec((None, 8, BLK, 3 * BLK), lambda n: (jnp.minimum(n, 2), 0, 0, 0))
    ss = pl.BlockSpec((1, LANE), lambda n: (0, 0))
    return [qs] + ks + vs + [bs, ss]


def swa_fwd(proj, bias, sinks):
    M = proj.shape[0]

    def body(q_ref, k0, k1, k2, v0, v1, v2, b_ref, s_ref, o_ref):
        k3 = jnp.concatenate([k0[...], k1[...], k2[...]], axis=0)
        v3 = jnp.concatenate([v0[...], v1[...], v2[...]], axis=0)
        o_ref[...] = _swa_block(q_ref[...], k3, v3, b_ref[...], s_ref[...]).astype(o_ref.dtype)

    return pl.pallas_call(
        body, name="swa_fwd", grid=(M // BLK,),
        in_specs=_swa_in_specs(),
        out_specs=pl.BlockSpec((BLK, 1024), lambda n: (n, 0)),
        out_shape=jax.ShapeDtypeStruct((M, 1024), BF16),
        compiler_params=_cparams(("parallel",)),
    )(proj, proj, proj, proj, proj, proj, proj, bias, sinks)


def swa_bwd(proj, bias, sinks, do):
    M = proj.shape[0]

    def body(q_ref, k0, k1, k2, v0, v1, v2, b_ref, s_ref, do_ref, dq_ref, dk_ref, dv_ref, db_ref, ds_ref):
        n = pl.program_id(0)

        @pl.when(n == 0)
        def _():
            dk_ref[...] = jnp.zeros_like(dk_ref)
            dv_ref[...] = jnp.zeros_like(dv_ref)
            ds_ref[...] = jnp.zeros_like(ds_ref)

        k3 = jnp.concatenate([k0[...], k1[...], k2[...]], axis=0)
        v3 = jnp.concatenate([v0[...], v1[...], v2[...]], axis=0)
        _, vjp = jax.vjp(_swa_block, q_ref[...], k3, v3, b_ref[...], s_ref[...])
        dq, dk3, dv3, dbias, dsink = vjp(do_ref[...].astype(F32))
        dq_ref[...] = dq
        prev = pl.multiple_of(jnp.maximum(n - 1, 0) * BLK, BLK)
        cur = pl.multiple_of(n * BLK, BLK)
        dk_ref[pl.ds(0, BLK), :] += dk3[0:BLK]
        dv_ref[pl.ds(0, BLK), :] += dv3[0:BLK]
        dk_ref[pl.ds(prev, BLK), :] += dk3[BLK:2 * BLK]
        dv_ref[pl.ds(prev, BLK), :] += dv3[BLK:2 * BLK]
        dk_ref[pl.ds(cur, BLK), :] += dk3[2 * BLK:]
        dv_ref[pl.ds(cur, BLK), :] += dv3[2 * BLK:]
        ds_ref[...] += dsink

        @pl.when(n <= 2)
        def _():
            db_ref[...] = dbias

        @pl.when(n > 2)
        def _():
            db_ref[...] += dbias

    return pl.pallas_call(
        body, name="swa_bwd", grid=(M // BLK,),
        in_specs=_swa_in_specs() + [pl.BlockSpec((BLK, 1024), lambda n: (n, 0))],
        out_specs=[pl.BlockSpec((BLK, 1024), lambda n: (n, 0)),
                   pl.BlockSpec((M, 256), lambda n: (0, 0)), pl.BlockSpec((M, 256), lambda n: (0, 0)),
                   pl.BlockSpec((None, 8, BLK, 3 * BLK), lambda n: (jnp.minimum(n, 2), 0, 0, 0)),
                   pl.BlockSpec((1, LANE), lambda n: (0, 0))],
        out_shape=[jax.ShapeDtypeStruct((M, 1024), F32), jax.ShapeDtypeStruct((M, 256), F32),
                   jax.ShapeDtypeStruct((M, 256), F32), jax.ShapeDtypeStruct((3, 8, BLK, 3 * BLK), F32),
                   jax.ShapeDtypeStruct((1, LANE), F32)],
        compiler_params=_cparams(("arbitrary",)),
    )(proj, proj, proj, proj, proj, proj, proj, bias, sinks, do)


def _shift_rows_impl(x, k):
    n = x.shape[0]
    rolled = pltpu.roll(x, k, 0)
    return jnp.where(_row_ids(0, n) >= k, rolled, 0.0)


def _unshift_rows_impl(g, k):
    n = g.shape[0]
    rolled = pltpu.roll(g, n - k, 0)
    return jnp.where(_row_ids(0, n) < n - k, rolled, 0.0)


@functools.partial(jax.custom_vjp, nondiff_argnums=(1,))
def _shift_rows(x, k):
    return _shift_rows_impl(x, k)


def _shift_rows_f(x, k):
    return _shift_rows_impl(x, k), None


def _shift_rows_b(k, _, g):
    return (_unshift_rows_impl(g, k),)


_shift_rows.defvjp(_shift_rows_f, _shift_rows_b)


def _conv_silu(x, w):
    rid = lax.broadcasted_iota(jnp.int32, w.shape, 0)
    y = x * jnp.sum(jnp.where(rid == 3, w, 0.0), axis=0, keepdims=True)
    for k in range(1, 4):
        y = y + _shift_rows(x, k) * jnp.sum(jnp.where(rid == 3 - k, w, 0.0), axis=0, keepdims=True)
    y = jnp.where(_row_ids(0, x.shape[0]) >= ZROWS, y, 0.0)
    return _silu(y)


def conv_fwd(proj, conv_w):
    M = proj.shape[0]
    nb = conv_w.shape[1] // LANE

    def body(x_ref, w_ref, o_ref):
        o_ref[...] = _conv_silu(x_ref[...], w_ref[...])

    return pl.pallas_call(
        body, name="conv_fwd", grid=(nb,),
        in_specs=[pl.BlockSpec((M, LANE), lambda c: (0, E_QB // LANE + c)), pl.BlockSpec((4, LANE), lambda c: (0, c))],
        out_specs=pl.BlockSpec((M, LANE), lambda c: (0, c)),
        out_shape=jax.ShapeDtypeStruct((M, conv_w.shape[1]), F32),
        compiler_params=_cparams(("parallel",)),
    )(proj, conv_w)


def conv_bwd(proj, conv_w, dy):
    M = proj.shape[0]
    nb = conv_w.shape[1] // LANE

    def body(x_ref, w_ref, dy_ref, dx_ref, dw_ref):
        _, vjp = jax.vjp(_conv_silu, x_ref[...], w_ref[...])
        dx, dw = vjp(dy_ref[...])
        dx_ref[...] = dx
        dw_ref[...] = dw

    return pl.pallas_call(
        body, name="conv_bwd", grid=(nb,),
        in_specs=[pl.BlockSpec((M, LANE), lambda c: (0, E_QB // LANE + c)), pl.BlockSpec((4, LANE), lambda c: (0, c)),
                  pl.BlockSpec((M, LANE), lambda c: (0, c))],
        out_specs=[pl.BlockSpec((M, LANE), lambda c: (0, c)), pl.BlockSpec((4, LANE), lambda c: (0, c))],
        out_shape=[jax.ShapeDtypeStruct((M, conv_w.shape[1]), F32), jax.ShapeDtypeStruct(conv_w.shape, F32)],
        compiler_params=_cparams(("parallel",)),
    )(proj, conv_w, dy)


def _fn_dn_prep(row0, yq, yk, ba, dnp):
    tm = yq.shape[0]
    real = _row_ids(row0, tm) >= ZROWS
    qs, ks, gs, bs = [], [], [], []
    for h in range(4):
        q = yq[:, h * 128:(h + 1) * 128]
        k = yk[:, h * 128:(h + 1) * 128]
        qs.append(q * lax.rsqrt(jnp.sum(q * q, axis=-1, keepdims=True) + 1e-6) * (128.0 ** -0.5))
        ks.append(k * lax.rsqrt(jnp.sum(k * k, axis=-1, keepdims=True) + 1e-6))
        beta = _sigmoid(_lane_pick(ba, h))
        g = -jnp.exp(_lane_pick(dnp, h)) * _softplus(_lane_pick(ba, 4 + h) + _lane_pick(dnp, 4 + h))
        g = jnp.where(real, g, 0.0)
        gs.append(jnp.broadcast_to(g, (tm, 128)))
        bs.append(jnp.broadcast_to(beta, (tm, 128)))
    cat = lambda xs: jnp.concatenate(xs, axis=1)
    return cat(qs), cat(ks), cat(gs), cat(bs)


def _unit_lower_inv_impl(a):
    n = a.shape[0]
    r = lax.broadcasted_iota(jnp.int32, (n, n), 0)
    c = lax.broadcasted_iota(jnp.int32, (n, n), 1)
    p = -a
    t = (r == c).astype(F32) + p
    nn = ((1,), (0,))
    for _ in range(int(math.log2(n)) - 1):
        p = _xdot(p, p, nn)
        t = t + _xdot(t, p, nn)
    return t


@jax.custom_vjp
def _unit_lower_inv(a):
    return _unit_lower_inv_impl(a)


def _unit_lower_inv_f(a):
    t = _unit_lower_inv_impl(a)
    return t, t


def _unit_lower_inv_b(t, g):
    return (-_xdot(_xdot(t, g, ((0,), (0,))), t, ((1,), (1,))),)


_unit_lower_inv.defvjp(_unit_lower_inv_f, _unit_lower_inv_b)


def _dn_chunk(q, k, v, gb, bb, S):
    r = lax.broadcasted_iota(jnp.int32, (CH, CH), 0)
    c = lax.broadcasted_iota(jnp.int32, (CH, CH), 1)
    tri_incl = r >= c
    gcb = _cumsum_rows(gb)
    g1 = gcb[:, :CH]
    diff = g1 - g1.T
    gamma = jnp.where(tri_incl, jnp.exp(jnp.where(tri_incl, diff, 0.0)), 0.0)
    kb = k * bb
    vb = v * bb
    t = _unit_lower_inv(jnp.where(r > c, _bmm_nt(kb, k) * gamma, 0.0))
    eg = jnp.exp(gcb)
    u = _xmm(t, vb)
    w = _xmm(t, kb * eg)
    attn = _bmm_nt(q, k) * gamma
    gtot = jnp.sum(gb, axis=0, keepdims=True)
    k_dec = k * jnp.exp(gtot - gcb)
    g_last = jnp.exp(jnp.broadcast_to(gtot, S.shape))
    v_new = u - _bmm(w, S)
    o = _bmm(q * eg, S) + _bmm(attn, v_new)
    return o, S * g_last + _bmm_tn(k_dec, v_new)


def _gla_chunk(q, k, v, glog, S):
    r = lax.broadcasted_iota(jnp.int32, (CH, CH), 0)
    c = lax.broadcasted_iota(jnp.int32, (CH, CH), 1)
    tri = r >= c
    bcum = _cumsum_rows(glog)
    q_dec = q * (128.0 ** -0.5) * jnp.exp(bcum)
    attn = jnp.where(tri, _bmm_nt(q_dec, k * jnp.exp(-bcum)), 0.0)
    o = _bmm(attn, v) + _bmm(q_dec, S)
    k_dec = k * jnp.exp(jnp.sum(glog, axis=0, keepdims=True) - bcum)
    decay = jnp.exp(_colsum_as_rows(glog, v.shape[1]))
    return o, S * decay + _bmm_tn(k_dec, v)


class Side:
    def __init__(self, ins, out_shapes, nsem, events):
        self.ins, self.out_shapes, self.nsem, self.events = list(ins), list(out_shapes), nsem, events


def _side_parts(side):
    if side is None:
        return [], [], [], []
    anyspec = pl.BlockSpec(memory_space=pl.ANY)
    return (side.ins, [anyspec] * len(side.ins), side.out_shapes,
            [pltpu.SemaphoreType.DMA((side.nsem,)), pltpu.SemaphoreType.DMA((side.nsem,))])


def _side_run(side, n_steps, in_refs, out_refs, sems):
    if side is None:
        return
    for step, fn in side.events(n_steps, in_refs, out_refs, *sems):
        pl.when(pl.program_id(0) == step)(fn)


def chunk_fwd(name, chunk_fn, ins, dv, side=None):
    M = ins[0][0].shape[0]
    N = M // CH
    ni = len(ins)
    ws = [w for (_, w, _) in ins]
    s_ins, s_specs, s_shapes, s_sems = _side_parts(side)
    ns, nso = len(s_ins), len(s_shapes)

    def body(*refs):
        o_ref, sall_ref = refs[ni + ns:ni + ns + 2]
        s_ref = refs[ni + ns + 2 + nso]
        _side_run(side, N, refs[ni:ni + ns], refs[ni + ns + 2:ni + ns + 2 + nso], refs[ni + ns + 3 + nso:])

        @pl.when(pl.program_id(0) == 0)
        def _():
            s_ref[...] = jnp.zeros_like(s_ref)

        for h in range(4):
            S = s_ref[h]
            sall_ref[h] = S
            o, s_new = chunk_fn(*[r[:, h * w:(h + 1) * w] for r, w in zip(refs[:ni], ws)], S)
            o_ref[:, h * dv:(h + 1) * dv] = o
            s_ref[h] = s_new

    specs = [pl.BlockSpec((CH, 4 * w), functools.partial(lambda n, cb: (n, cb), cb=cb // 4)) for (_, w, cb) in ins]
    res = pl.pallas_call(
        body, name=name, grid=(N,),
        in_specs=specs + s_specs,
        out_specs=[pl.BlockSpec((CH, 4 * dv), lambda n: (n, 0)),
                   pl.BlockSpec((4, None, 128, dv), lambda n: (0, n, 0, 0))] + [pl.BlockSpec(memory_space=pl.ANY)] * nso,
        out_shape=[jax.ShapeDtypeStruct((M, 4 * dv), F32), jax.ShapeDtypeStruct((4, N, 128, dv), F32)] + s_shapes,
        scratch_shapes=[pltpu.VMEM((4, 128, dv), F32)] + s_sems,
        compiler_params=_cparams(("arbitrary",)),
    )(*[a for (a, _, _) in ins], *s_ins)
    return res[0], res[1], res[2:]


def chunk_bwd(name, chunk_fn, ins, dv, s_all, do, side=None):
    M = ins[0][0].shape[0]
    N = M // CH
    ni = len(ins)
    ws = [w for (_, w, _) in ins]
    s_ins, s_specs, s_shapes, s_sems = _side_parts(side)
    ns, nso = len(s_ins), len(s_shapes)

    def body(*refs):
        sall_ref, do_ref = refs[ni:ni + 2]
        o0 = ni + 2 + ns
        d_refs = refs[o0:o0 + ni]
        ds_ref = refs[o0 + ni + nso]
        _side_run(side, N, refs[ni + 2:ni + 2 + ns], refs[o0 + ni:o0 + ni + nso], refs[o0 + ni + nso + 1:])

        @pl.when(pl.program_id(0) == 0)
        def _():
            ds_ref[...] = jnp.zeros_like(ds_ref)

        for h in range(4):
            _, vjp = jax.vjp(chunk_fn, *[r[:, h * w:(h + 1) * w] for r, w in zip(refs[:ni], ws)], sall_ref[h])
            grads = vjp((do_ref[:, h * dv:(h + 1) * dv], ds_ref[h]))
            for d_ref, w, g in zip(d_refs, ws, grads[:ni]):
                d_ref[:, h * w:(h + 1) * w] = g
            ds_ref[h] = grads[ni]

    rev = lambda n: N - 1 - n
    specs = [pl.BlockSpec((CH, 4 * w), functools.partial(lambda n, cb: (rev(n), cb), cb=cb // 4)) for (_, w, cb) in ins]
    res = pl.pallas_call(
        body, name=name, grid=(N,),
        in_specs=specs + [pl.BlockSpec((4, None, 128, dv), lambda n: (0, rev(n), 0, 0)),
                          pl.BlockSpec((CH, 4 * dv), lambda n: (rev(n), 0))] + s_specs,
        out_specs=[pl.BlockSpec((CH, 4 * w), lambda n: (rev(n), 0)) for w in ws] + [pl.BlockSpec(memory_space=pl.ANY)] * nso,
        out_shape=[jax.ShapeDtypeStruct((M, 4 * w), F32) for w in ws] + s_shapes,
        scratch_shapes=[pltpu.VMEM((4, 128, dv), F32)] + s_sems,
        compiler_params=_cparams(("arbitrary",)),
    )(*[a for (a, _, _) in ins], s_all, do, *s_ins)
    return res[:ni], res[ni:]


def _fn_gate_out(hd, row0, o, z, w):
    outs = []
    for h in range(4):
        outs.append(_rms(o[:, h * hd:(h + 1) * hd], w) * _silu(z[:, h * hd:(h + 1) * hd]))
    return (jnp.concatenate(outs, axis=1),)


def _fn_gla_prep(row0, gk, wgu, bg):
    x = _mm(gk, wgu) + bg
    ls = jnp.minimum(x, 0.0) - jnp.log(1.0 + jnp.exp(-jnp.abs(x)))
    return (jnp.where(_row_ids(row0, gk.shape[0]) >= ZROWS, ls / 16.0, 0.0),)


def loss_call(y, tgt):
    M = y.shape[0]
    tm = _pick(M, 512)

    def body(y_ref, t_ref, l_ref, dy_ref):
        i = pl.program_id(0)
        e = y_ref[...] - t_ref[...]
        dy_ref[...] = e * (1.0 / D)
        part = 0.5 * jnp.sum(jnp.sum(e * e, axis=1, keepdims=True) * (1.0 / D), axis=0, keepdims=True)
        part = jnp.broadcast_to(part, (8, LANE))

        @pl.when(i == 0)
        def _():
            l_ref[...] = part

        @pl.when(i > 0)
        def _():
            l_ref[...] += part

    return pl.pallas_call(
        body, name="loss", grid=(M // tm,),
        in_specs=[pl.BlockSpec((tm, D), lambda i: (i, 0))] * 2,
        out_specs=[pl.BlockSpec((8, LANE), lambda i: (0, 0)), pl.BlockSpec((tm, D), lambda i: (i, 0))],
        out_shape=[jax.ShapeDtypeStruct((8, LANE), F32), jax.ShapeDtypeStruct((M, D), F32)],
        compiler_params=_cparams(("arbitrary",)),
    )(y, tgt)


def _bf(x):
    return x.astype(BF16)


def core_step(x, tgt, W, comm=None):
    S = x.shape[0]
    M = S + PADR
    ids = jnp.asarray(_swa_bucket_ids())
    h0 = jnp.concatenate([jnp.zeros((ZROWS, D), F32), W["meta"], x], axis=0)
    nw = W["norm"]
    nrow = lambda l, k: nw[l, k][None, :]
    layers = list(W["layers"])
    ffw = lambda l: (layers[l]["ffn_g"], layers[l]["ffn_u"], layers[l]["ffn_d"])
    sinks = jnp.pad(W["sinks"], ((0, 0), (0, LANE - 8)))
    dnp = jnp.pad(jnp.concatenate([W["a_log"], W["dt_bias"]], axis=1), ((0, 0), (0, LANE - 8)))
    wgu = jnp.pad(W["gate_up"], ((0, LANE - 16), (0, 0)))
    bg = W["b_gate"]
    full = lambda a: (a, a.shape[1], 0)

    saved = []
    h = h0
    (hn,) = rowwise_fwd("prenorm_0", _fn_prenorm, [full(h)], [nrow(0, 0)], [(D, BF16)])
    bias = swa_bias_fwd(W["rel"], ids)
    for l in range(2):
        st = {"h_a": h, "hn_a": hn}
        f1, a1, b1 = ffn_fwd(f"ffn_fwd_{l}0", hn, *ffw(l),0)
        h, hn = rowwise_fwd(f"resnorm_{l}1", functools.partial(_fn_resnorm, 0.5), [full(h), full(f1)],
                            [nrow(l, 1), nrow(l, 2)], [(D, F32), (D, BF16)])
        st.update(f1=f1, a1=a1, b1=b1, h_b=h, hn_b=hn)
        if l == 0:
            proj = mm_nt(hn, layers[0]["w_in"], "e_proj")
            o_a = swa_fwd(proj, bias, sinks)
            y = conv_fwd(proj, W["conv"])
            qn, kn, gb, bb = rowwise_fwd(
                "dn_prep", _fn_dn_prep, [(y, 512, 0), (y, 512, 1), (proj, LANE, E_BA // LANE)], [dnp], [(512, F32)] * 4)
            ins = [(qn, 128, 0), (kn, 128, 0), (y, 128, 8), (gb, 128, 0), (bb, 128, 0)]
            o_dn, s_all, got = chunk_fwd("dn_fwd", _dn_chunk, ins, 128, side=comm.gather_side() if comm else None)
            if comm:
                layers[1] = comm.gather_done(got)
            (o_b,) = rowwise_fwd("dn_out", functools.partial(_fn_gate_out, 128),
                                 [full(o_dn), (proj, 512, E_ZB // 512)], [W["dn_norm"]], [(512, BF16)])
            omix = jnp.concatenate([o_a, o_b], axis=1)
            mix = mm_nn(omix, layers[0]["w_out"], "e_mix")
            st.update(proj=proj, y=y, qn=qn, kn=kn, gb=gb, bb=bb, o_dn=o_dn, s_all=s_all, omix=omix)
        else:
            proj = mm_nt(hn, layers[1]["w_in"], "o_proj")
            (glog,) = rowwise_fwd("gla_prep", _fn_gla_prep, [(proj, LANE, O_GK // LANE)], [wgu, bg], [(512, F32)])
            ins = [(proj, 128, O_Q // 128), (proj, 128, O_K // 128), (proj, 256, O_V // 256), (glog, 128, 0)]
            o_g, s_all, _ = chunk_fwd("gla_fwd", _gla_chunk, ins, 256)
            (omix,) = rowwise_fwd("gla_out", functools.partial(_fn_gate_out, 256),
                                  [full(o_g), (proj, 1024, O_G // 1024)], [W["gla_norm"]], [(1024, BF16)])
            mix = mm_nn(omix, layers[1]["w_out"], "o_mix")
            st.update(proj=proj, glog=glog, o_g=o_g, s_all=s_all, omix=omix)
        h, hn = rowwise_fwd(f"resnorm_{l}3", functools.partial(_fn_resnorm, 1.0), [full(h), full(mix)],
                            [nrow(l, 3), nrow(l, 4)], [(D, F32), (D, BF16)])
        st.update(mix=mix, h_c=h, hn_c=hn)
        f2, a2, b2 = ffn_fwd(f"ffn_fwd_{l}1", hn, *ffw(l),1)
        st.update(f2=f2, a2=a2, b2=b2)
        if l == 0:
            h, hn = rowwise_fwd("resnorm_05", functools.partial(_fn_resnorm, 0.5), [full(h), full(f2)],
                                [nrow(0, 5), nrow(1, 0)], [(D, F32), (D, BF16)])
        else:
            (h,) = rowwise_fwd("res_last", functools.partial(_fn_res_last, 0.5), [full(h), full(f2)],
                               [nrow(1, 5)], [(D, F32)])
        saved.append(st)

    loss_blk, dy = loss_call(h[PADR:], tgt)
    dh = jnp.concatenate([jnp.zeros((PADR, D), F32), dy], axis=0)

    G = {}
    dnorm = [[None] * 6 for _ in range(2)]
    dWg = [[None, None], [None, None]]
    dWu = [[None, None], [None, None]]
    dWd = [[None, None], [None, None]]
    dhn = None
    for l in (1, 0):
        st = saved[l]
        if l == 1:
            (dh_, df), (dw5,) = rowwise_bwd(
                "res_last_b", functools.partial(_fn_res_last, 0.5), [full(st["h_c"]), full(st["f2"])], [nrow(1, 5)],
                [full(dh)], [F32, BF16])
            dnorm[1][5] = dw5
        else:
            (dh_, df), (dw5, dw0n) = rowwise_bwd(
                "resnorm_05_b", functools.partial(_fn_resnorm, 0.5), [full(st["h_c"]), full(st["f2"])],
                [nrow(0, 5), nrow(1, 0)], [full(dh), full(dhn)], [F32, BF16])
            dnorm[0][5] = dw5
            dnorm[1][0] = dw0n
        dh = dh_
        dxn, da, db, hm = ffn_bwd_x(f"ffn_bx_{l}1", df, st["a2"], st["b2"], *ffw(l),1)
        dWg[l][1], dWu[l][1], dWd[l][1] = ffn_bwd_w(f"ffn_bw_{l}1", st["hn_c"], df, da, db, hm)
        (dh_, dmix), (dw3, dw4) = rowwise_bwd(
            f"resnorm_{l}3_b", functools.partial(_fn_resnorm, 1.0), [full(st["h_b"]), full(st["mix"])],
            [nrow(l, 3), nrow(l, 4)], [full(dh), full(dxn)], [F32, BF16])
        dnorm[l][3], dnorm[l][4] = dw3, dw4
        dh = dh_
        proj = st["proj"]
        if l == 1:
            G["o_out"] = mm_tn(st["omix"], dmix, "o_out_dw")
            domix = mm_nt(dmix, layers[1]["w_out"], "o_mix_dx")
            (do_g, dgate), (dgn,) = rowwise_bwd(
                "gla_out_b", functools.partial(_fn_gate_out, 256), [full(st["o_g"]), (proj, 1024, O_G // 1024)],
                [W["gla_norm"]], [full(domix)], [F32, F32])
            G["gla_norm"] = dgn
            ins = [(proj, 128, O_Q // 128), (proj, 128, O_K // 128), (proj, 256, O_V // 256), (st["glog"], 128, 0)]
            (dq, dk, dv, dglog), _ = chunk_bwd("gla_bwd", _gla_chunk, ins, 256, st["s_all"], do_g)
            (dgk,), (dwgu, dbg) = rowwise_bwd("gla_prep_b", _fn_gla_prep, [(proj, LANE, O_GK // LANE)], [wgu, bg],
                                              [full(dglog)], [F32])
            G["gate_up"] = dwgu[:16]
            G["b_gate"] = dbg
            dproj = _bf(jnp.concatenate([dq, dk, dv, dgate, dgk, jnp.zeros((M, O_END - O_GK - LANE), F32)], axis=1))
            G["o_in"] = mm_tn(dproj, st["hn_b"], "o_in_dw")
            dhn_b = mm_nn(dproj, layers[1]["w_in"], "o_proj_dx")
        else:
            G["e_out"] = mm_tn(st["omix"], dmix, "e_out_dw")
            domix = mm_nt(dmix, layers[0]["w_out"], "e_mix_dx")
            (do_dn, dz), (ddn,) = rowwise_bwd(
                "dn_out_b", functools.partial(_fn_gate_out, 128), [full(st["o_dn"]), (proj, 512, E_ZB // 512)],
                [W["dn_norm"]], [(domix, 512, 2)], [F32, F32])
            G["dn_norm"] = ddn
            ins = [(st["qn"], 128, 0), (st["kn"], 128, 0), (st["y"], 128, 8), (st["gb"], 128, 0), (st["bb"], 128, 0)]
            side = comm.scatter_side(layer_grad_items(1, dWg, dWu, dWd, G["o_in"], G["o_out"])) if comm else None
            (dqn, dkn, dvv, dgb, dbb), got = chunk_bwd("dn_bwd", _dn_chunk, ins, 128, st["s_all"], do_dn, side=side)
            if comm:
                comm.scatter_done(got)
            (dyq, dyk, dba), (ddnp,) = rowwise_bwd(
                "dn_prep_b", _fn_dn_prep, [(st["y"], 512, 0), (st["y"], 512, 1), (proj, LANE, E_BA // LANE)], [dnp],
                [full(dqn), full(dkn), full(dgb), full(dbb)], [F32, F32, F32])
            G["a_log"] = ddnp[:, 0:4]
            G["dt_bias"] = ddnp[:, 4:8]
            dyc = jnp.concatenate([dyq, dyk, dvv], axis=1)
            dxc, dconv = conv_bwd(proj, W["conv"], dyc)
            G["conv"] = dconv
            dq_a, dk_a, dv_a, dbias, dsink = swa_bwd(proj, bias, sinks, domix)
            G["sinks"] = dsink[:, :8]
            G["rel"] = swa_bias_bwd(dbias, ids)[:, :8]
            dproj = _bf(jnp.concatenate([dq_a, dk_a, dv_a, dxc, dz, dba, jnp.zeros((M, E_END - E_BA - LANE), F32)], axis=1))
            G["e_in"] = mm_tn(dproj, st["hn_b"], "e_in_dw")
            dhn_b = mm_nn(dproj, layers[0]["w_in"], "e_proj_dx")
        (dh_, df), (dw1, dw2) = rowwise_bwd(
            f"resnorm_{l}1_b", functools.partial(_fn_resnorm, 0.5), [full(st["h_a"]), full(st["f1"])],
            [nrow(l, 1), nrow(l, 2)], [full(dh), full(dhn_b)], [F32, BF16])
        dnorm[l][1], dnorm[l][2] = dw1, dw2
        dh = dh_
        dxn, da, db, hm = ffn_bwd_x(f"ffn_bx_{l}0", df, st["a1"], st["b1"], *ffw(l),0)
        dWg[l][0], dWu[l][0], dWd[l][0] = ffn_bwd_w(f"ffn_bw_{l}0", st["hn_a"], df, da, db, hm)
        dhn = dxn
    (dh0p,), (dw00,) = rowwise_bwd("prenorm_0_b", _fn_prenorm, [full(saved[0]["h_a"])], [nrow(0, 0)], [full(dhn)], [F32])
    dnorm[0][0] = dw00
    dh = dh + dh0p
    G["meta"] = dh[ZROWS:PADR]
    G["norm"] = jnp.stack([jnp.concatenate(r, axis=0) for r in dnorm], axis=0)
    G["items"] = [layer_grad_items(0, dWg, dWu, dWd, G["e_in"], G["e_out"]),
                  None if comm else layer_grad_items(1, dWg, dWu, dWd, G["o_in"], G["o_out"])]
    return loss_blk, dh[PADR:], G


NAMES = [("meta", "meta_tokens"), ("norm", "norm_w"), ("ffn_g", "ffn_w_gate"), ("ffn_u", "ffn_w_up"),
         ("ffn_d", "ffn_w_down"), ("rel", "rel_bias_table"), ("e_in", "even_w_in"), ("conv", "even_conv_w"),
         ("sinks", "swa_sinks"), ("a_log", "dn_a_log"), ("dt_bias", "dn_dt_bias"), ("dn_norm", "dn_norm_w"),
         ("e_out", "even_w_out"), ("o_in", "odd_w_in"), ("gate_up", "gla_w_gate_up"), ("b_gate", "gla_b_gate"),
         ("gla_norm", "gla_norm_w"), ("o_out", "odd_w_out")]
BIG = ["ffn_g", "ffn_u", "ffn_d", "e_in", "e_out", "o_in", "o_out"]
IN_ROWS = 800
SMALL = [("meta", (16, 256)), ("norm", (2, 6, 256)), ("conv", (1, 4, 384)), ("gate_up", (1, 16, 128)),
         ("b_gate", (1, 128)), ("gla_norm", (1, 64))]
REPL = [("rel", (32, 8)), ("sinks", (1, 8)), ("a_log", (1, 4)), ("dt_bias", (1, 4)), ("dn_norm", (1, 128))]
SMALL_REP = 88 * LANE
SMALL_ROWS = 96


def pack_small(t):
    a = jnp.concatenate([t[n].reshape(-1) for n, _ in SMALL])
    b = jnp.concatenate([t[n].reshape(-1) for n, _ in REPL])
    flat = jnp.concatenate([a, jnp.zeros((SMALL_REP - a.shape[0],), F32), b,
                            jnp.zeros((SMALL_ROWS * LANE - SMALL_REP - b.shape[0],), F32)])
    return flat.reshape(SMALL_ROWS, LANE)


def unpack_small(p):
    flat = p.reshape(-1)
    out, r = {}, 0
    for n, shp in SMALL:
        k = int(np.prod(shp))
        out[n] = flat[r:r + k].reshape(shp)
        r += k
    r = SMALL_REP
    for n, shp in REPL:
        k = int(np.prod(shp))
        out[n] = flat[r:r + k].reshape(shp)
        r += k
    return out


IN_SRC = (706, 772)


def weight_pieces(wt, l):
    inn = wt["e_in" if l == 0 else "o_in"]
    inn = jnp.pad(inn, ((0, IN_ROWS - inn.shape[0]), (0, 0))).reshape(2, IN_ROWS // 2, D)
    out = wt["e_out" if l == 0 else "o_out"].reshape(2, 128, D)
    return [wt["ffn_g"][l], wt["ffn_u"][l], wt["ffn_d"][l], inn, out]


def layer_weights(l, q):
    m = _even_in_map() if l == 0 else _odd_in_map()
    src = np.where(m >= 0, (m // IN_SRC[l]) * IN_ROWS + m % IN_SRC[l], -1)
    w_out = q[4].reshape(NSH * 256, D)
    return {"ffn_g": q[0], "ffn_u": q[1], "ffn_d": q[2], "w_in": _take_pad(q[3].reshape(NSH * IN_ROWS, D), src, 0),
            "w_out": _take_pad(w_out, _even_out_map(), 0) if l == 0 else w_out}


def layer_grad_items(l, dwg, dwu, dwd, g_in, g_out):
    m = _even_in_map() if l == 0 else _odd_in_map()
    gi = jnp.take(g_in, jnp.asarray(_inverse(m, NSH * IN_SRC[l])), axis=0).reshape(NSH, IN_SRC[l], D)
    gi = _bf(jnp.pad(gi, ((0, 0), (0, IN_ROWS - IN_SRC[l]), (0, 0)))).reshape(NSH, 2, IN_ROWS // 2, D)
    if l == 0:
        g_out = jnp.take(g_out, jnp.asarray(_inverse(_even_out_map(), 1024)), axis=0)
    go = _bf(g_out).reshape(NSH, 2, 128, D)
    return [[dwg[l][j], dwu[l][j], dwd[l][j], gi[:, j], go[:, j]] for j in range(2)]


def assemble_layer(l, r0, r1):
    return {"ffn_g": jnp.stack([r0[0], r1[0]]), "ffn_u": jnp.stack([r0[1], r1[1]]), "ffn_d": jnp.stack([r0[2], r1[2]]),
            "in": jnp.concatenate([r0[3], r1[3]])[:IN_SRC[l]], "out": jnp.concatenate([r0[4], r1[4]])}


def big_grads(l0, l1):
    st = lambda n: jnp.stack([l0[n], l1[n]])
    return {"ffn_g": st("ffn_g"), "ffn_u": st("ffn_u"), "ffn_d": st("ffn_d"), "e_in": l0["in"], "e_out": l0["out"],
            "o_in": l1["in"], "o_out": l1["out"]}


def small_from_gathered(gs):
    sm = [unpack_small(gs[s]) for s in range(NSH)]
    full = {}
    full["meta"] = jnp.concatenate([sm[s]["meta"] for s in range(NSH)], axis=1)
    full["norm"] = jnp.concatenate([sm[s]["norm"] for s in range(NSH)], axis=2)
    full["conv"] = jnp.concatenate([sm[s]["conv"][0] for s in range(NSH)], axis=1)
    full["gate_up"] = jnp.concatenate([sm[s]["gate_up"][0] for s in range(NSH)], axis=1)
    full["b_gate"] = jnp.concatenate([sm[s]["b_gate"] for s in range(NSH)], axis=1)
    full["gla_norm"] = jnp.concatenate([sm[s]["gla_norm"] for s in range(NSH)], axis=1)
    return full


def _col_sh(w):
    return jnp.moveaxis(w.reshape(w.shape[0], NSH, w.shape[1] // NSH), 1, 0)


def _to_t(n, a):
    if n in ("ffn_g", "ffn_u"):
        return jnp.swapaxes(a, 2, 3)
    if n in ("e_in", "o_in"):
        return jnp.swapaxes(a[0], 0, 1)
    return a if n == "ffn_d" else a[0]


def _from_t(n, a):
    if n in ("ffn_g", "ffn_u"):
        return jnp.swapaxes(a, 2, 3)
    if n in ("e_in", "o_in"):
        return jnp.swapaxes(a, 0, 1)[None]
    return a if n == "ffn_d" else a[None]


def pack_small_grads(G):
    col_sh = _col_sh
    norm_sh = jnp.moveaxis(G["norm"].reshape(2, 6, NSH, 256), 2, 0)
    a = jnp.concatenate([col_sh(G["meta"]).reshape(NSH, -1), norm_sh.reshape(NSH, -1), col_sh(G["conv"]).reshape(NSH, -1),
                         col_sh(G["gate_up"]).reshape(NSH, -1), G["b_gate"].reshape(NSH, -1),
                         G["gla_norm"].reshape(NSH, -1)], axis=1)
    b = jnp.concatenate([G[n].reshape(-1) for n, _ in REPL])
    b = jnp.broadcast_to(b[None], (NSH, b.shape[0]))
    small = jnp.concatenate([a, jnp.zeros((NSH, SMALL_REP - a.shape[1]), F32), b,
                             jnp.zeros((NSH, SMALL_ROWS * LANE - SMALL_REP - b.shape[1]), F32)], axis=1)
    return small.reshape(NSH, SMALL_ROWS, LANE)


MESH = pl.DeviceIdType.MESH
ANY = pl.BlockSpec(memory_space=pl.ANY)
VMEM = pl.BlockSpec(memory_space=pltpu.VMEM)


def _place():
    return lax.axis_index("x"), lax.axis_index("y"), lax.axis_index("c")


def _other_chips(x, y):
    return [(1 - x, y), (x, 1 - y), (1 - x, 1 - y)]


def _rcopy(send_sems, recv_sems, k, src, dst, to):
    return pltpu.make_async_remote_copy(src_ref=src, dst_ref=dst, send_sem=send_sems.at[k], recv_sem=recv_sems.at[k],
                                        device_id=to, device_id_type=MESH)


def _gather_steps(in_refs, out_refs, send_sems, recv_sems):
    n = len(in_refs)
    x, y, c = _place()
    s = 2 * x + y
    chips = _other_chips(x, y)
    copy = functools.partial(_rcopy, send_sems, recv_sems)
    pairs = [(i, j, cx, cy) for i in range(n) for j, (cx, cy) in enumerate(chips)]
    pushes = lambda: [copy(i * 3 + j, in_refs[i].at[c], out_refs[i].at[s, c], (cx, cy, c)) for i, j, cx, cy in pairs]
    landed = lambda i, cx, cy, half: out_refs[i].at[2 * cx + cy, half]
    relays = lambda: [copy(3 * n + i * 3 + j, landed(i, cx, cy, c), landed(i, cx, cy, c), (x, y, 1 - c)) for i, j, cx, cy in pairs]

    def start():
        for cp in pushes():
            cp.start()

    def relay():
        for i, j, cx, cy in pairs:
            copy(i * 3 + j, landed(i, cx, cy, c), landed(i, cx, cy, c), (x, y, c)).wait_recv()
        for cp in relays():
            cp.start()

    def finish():
        for i, j, cx, cy in pairs:
            copy(3 * n + i * 3 + j, landed(i, cx, cy, 1 - c), landed(i, cx, cy, 1 - c), (x, y, c)).wait_recv()
        for cp in pushes() + relays():
            cp.wait_send()

    return start, relay, finish


def _gather_shapes(pieces):
    return [jax.ShapeDtypeStruct((NSH,) + a.shape, a.dtype) for a in pieces]


def ag_layer(name, pieces):
    n = len(pieces)

    def body(*refs):
        for fn in _gather_steps(refs[:n], refs[n:2 * n], *refs[2 * n:]):
            fn()

    return pl.pallas_call(
        body, name=name, in_specs=[ANY] * n, out_specs=[ANY] * n, out_shape=_gather_shapes(pieces),
        scratch_shapes=[pltpu.SemaphoreType.DMA((6 * n,)), pltpu.SemaphoreType.DMA((6 * n,))],
    )(*pieces)


def gather_side(pieces):
    def events(n_steps, in_refs, out_refs, send_sems, recv_sems):
        start, relay, finish = _gather_steps(in_refs, out_refs, send_sems, recv_sems)
        return [(0, start), (max(3 * n_steps // 4, 1), relay), (n_steps - 1, finish)]

    return Side(pieces, _gather_shapes(pieces), 6 * len(pieces), events)


def ag_small(pack):
    def body(x_ref, out_ref, send_sems, recv_sems):
        x, y, c = _place()
        s = 2 * x + y
        chips = _other_chips(x, y)

        def copy(k, src, dst, to):
            return pltpu.make_async_remote_copy(src_ref=src, dst_ref=dst, send_sem=send_sems.at[k], recv_sem=recv_sems.at[k],
                                                device_id=to, device_id_type=MESH)

        out_ref[s] = x_ref[...]
        sends = [copy(j, x_ref, out_ref.at[s], (cx, cy, c)) for j, (cx, cy) in enumerate(chips)]
        for cp in sends:
            cp.start()
        for j, (cx, cy) in enumerate(chips):
            blk = out_ref.at[2 * cx + cy]
            copy(j, blk, blk, (x, y, c)).wait_recv()
        for cp in sends:
            cp.wait_send()

    return pl.pallas_call(
        body, name="ag_small", in_specs=[VMEM], out_specs=VMEM,
        out_shape=jax.ShapeDtypeStruct((NSH,) + pack.shape, pack.dtype),
        scratch_shapes=[pltpu.SemaphoreType.DMA((3,)), pltpu.SemaphoreType.DMA((3,))],
    )(pack)


def rs_pair(name, items):
    ni = len(items[0])

    def body(*refs):
        in_refs = [refs[:ni], refs[ni:2 * ni]]
        recv_refs = refs[2 * ni:3 * ni]
        send_sems, recv_sems = refs[3 * ni:]
        x, y, c = _place()
        copy = functools.partial(_rcopy, send_sems, recv_sems)
        for cc in range(2):
            @pl.when(c == cc)
            def _():
                cps = [copy(i * NSH + s, in_refs[1 - cc][i].at[s], recv_refs[i].at[s], (x, y, 1 - c))
                       for i in range(ni) for s in range(NSH)]
                for cp in cps:
                    cp.start()
                for cp in cps:
                    cp.wait()

    return pl.pallas_call(
        body, name=name, in_specs=[ANY] * (2 * ni), out_specs=[ANY] * ni,
        out_shape=[jax.ShapeDtypeStruct(a.shape, a.dtype) for a in items[0]],
        scratch_shapes=[pltpu.SemaphoreType.DMA((ni * NSH,)), pltpu.SemaphoreType.DMA((ni * NSH,))],
    )(*items[0], *items[1])


def _scatter_steps(a_refs, out_refs, send_sems, recv_sems):
    n = len(a_refs)
    x, y, c = _place()
    s = 2 * x + y
    chips = _other_chips(x, y)
    copy = functools.partial(_rcopy, send_sems, recv_sems)
    pairs = [(i, j, cx, cy) for i in range(n) for j, (cx, cy) in enumerate(chips)]
    sends = lambda: [copy(i * 3 + j, a_refs[i].at[2 * cx + cy], out_refs[i].at[s], (cx, cy, c)) for i, j, cx, cy in pairs]

    def start():
        for cp in sends():
            cp.start()

    def finish():
        for i, j, cx, cy in pairs:
            blk = out_refs[i].at[2 * cx + cy]
            copy(i * 3 + j, blk, blk, (x, y, c)).wait_recv()
        for cp in sends():
            cp.wait_send()

    return start, finish


def rs_chips(name, arrs):
    n = len(arrs)

    def body(*refs):
        for fn in _scatter_steps(refs[:n], refs[n:2 * n], *refs[2 * n:]):
            fn()

    return pl.pallas_call(
        body, name=name, in_specs=[ANY] * n, out_specs=[ANY] * n,
        out_shape=[jax.ShapeDtypeStruct(a.shape, a.dtype) for a in arrs],
        scratch_shapes=[pltpu.SemaphoreType.DMA((3 * n,)), pltpu.SemaphoreType.DMA((3 * n,))],
    )(*arrs)


def scatter_side(arrs):
    def events(n_steps, in_refs, out_refs, send_sems, recv_sems):
        start, finish = _scatter_steps(in_refs, out_refs, send_sems, recv_sems)
        return [(0, start), (n_steps - 1, finish)]

    return Side(arrs, [jax.ShapeDtypeStruct(a.shape, a.dtype) for a in arrs], 3 * len(arrs), events)


def _pair_chunks(rows):
    return 4 if rows % 32 == 0 else (2 if rows % 16 == 0 else 1)


def ag_pair(name, arrs):
    n = len(arrs)
    chunks = [(i, k * (a.shape[0] // _pair_chunks(a.shape[0])), a.shape[0] // _pair_chunks(a.shape[0]))
              for i, a in enumerate(arrs) for k in range(_pair_chunks(a.shape[0]))]

    def body(*refs):
        g_refs, out_refs = refs[:n], refs[n:2 * n]
        send_sems, recv_sems = refs[2 * n:]
        x, y, c = _place()
        give = [_rcopy(send_sems, recv_sems, q, g_refs[i].at[pl.ds(r0, rc)], out_refs[i].at[pl.ds(r0, rc)], (x, y, 1 - c))
                for q, (i, r0, rc) in enumerate(chunks)]
        for cp in give:
            cp.start()
        for cp in give:
            cp.wait()

    return pl.pallas_call(
        body, name=name, in_specs=[ANY] * n, out_specs=[ANY] * n,
        out_shape=[jax.ShapeDtypeStruct(a.shape, a.dtype) for a in arrs],
        scratch_shapes=[pltpu.SemaphoreType.DMA((len(chunks),)), pltpu.SemaphoreType.DMA((len(chunks),))],
    )(*arrs)


def small_allreduce(p):
    def body(p_ref, out_ref, rbuf, send_sems, recv_sems):
        x, y, c = _place()
        me = 4 * x + 2 * y + c
        rbuf[me] = p_ref[2 * x + y]
        flip = lambda v, f: (1 - v) if f else v
        peers = [(flip(x, k >> 2 & 1), flip(y, k >> 1 & 1), flip(c, k & 1)) for k in range(1, 8)]

        def copy(k, src, dst, to):
            return pltpu.make_async_remote_copy(src_ref=src, dst_ref=dst, send_sem=send_sems.at[k], recv_sem=recv_sems.at[k],
                                                device_id=to, device_id_type=MESH)

        sends = [copy(k, p_ref.at[2 * px + py], rbuf.at[me], (px, py, pc)) for k, (px, py, pc) in enumerate(peers)]
        for cp in sends:
            cp.start()
        for k, (px, py, pc) in enumerate(peers):
            blk = rbuf.at[4 * px + 2 * py + pc]
            copy(k, blk, blk, (x, y, c)).wait_recv()
        for cp in sends:
            cp.wait_send()
        acc = rbuf[0]
        for d in range(1, 8):
            acc = acc + rbuf[d]
        out_ref[...] = acc

    return pl.pallas_call(
        body, name="small_allreduce", in_specs=[VMEM], out_specs=VMEM,
        out_shape=jax.ShapeDtypeStruct(p.shape[1:], F32),
        scratch_shapes=[pltpu.VMEM((8,) + p.shape[1:], F32), pltpu.SemaphoreType.DMA((7,)), pltpu.SemaphoreType.DMA((7,))],
    )(p)


def _rows_tile(rows, cap):
    return _pick(rows, cap) if rows % 128 == 0 else rows


def sum_pair(name, a0, a1, recv, cflag):
    n, r, d = recv.shape
    tr = _pick(r, 1024) if r % 64 == 0 else r

    def body(c_ref, a0_ref, a1_ref, b_ref, o_ref):
        own = jnp.where(c_ref[0] == 0, a0_ref[...].astype(F32), a1_ref[...].astype(F32))
        o_ref[...] = (own + b_ref[...].astype(F32)).astype(o_ref.dtype)

    spec = pl.BlockSpec((None, tr, d), lambda s, i: (s, i, 0))
    return pl.pallas_call(
        body, name=name, grid=(n, r // tr), in_specs=[pl.BlockSpec(memory_space=pltpu.SMEM), spec, spec, spec],
        out_specs=spec, out_shape=jax.ShapeDtypeStruct(recv.shape, BF16), compiler_params=_cparams(("parallel", "parallel")),
    )(cflag, a0, a1, recv)


def sum_chips(name, parts, own, sflag):
    n, r, d = parts.shape
    tr = _pick(r, 1024) if r % 64 == 0 else r

    def body(s_ref, p_ref, a_ref, o_ref):
        acc = None
        for t in range(n):
            term = jnp.where(s_ref[0] == t, a_ref[t].astype(F32), p_ref[t].astype(F32))
            acc = term if acc is None else acc + term
        o_ref[...] = acc

    spec = pl.BlockSpec((n, tr, d), lambda i: (0, i, 0))
    return pl.pallas_call(
        body, name=name, grid=(r // tr,), in_specs=[pl.BlockSpec(memory_space=pltpu.SMEM), spec, spec],
        out_specs=pl.BlockSpec((tr, d), lambda i: (i, 0)), out_shape=jax.ShapeDtypeStruct((r, d), F32),
        compiler_params=_cparams(("parallel",)),
    )(sflag, parts, own)


ADAM_LR, ADAM_B1, ADAM_B2, ADAM_EPS, ADAM_WD, ADAM_STEP = 0.001, 0.9, 0.999, 1e-08, 0.01, 10


def adamw_call(name, w, g, m, v):
    rows, cols = w.shape
    tr = _rows_tile(rows, 512)

    def body(w_ref, g_ref, m_ref, v_ref, d_ref, nm_ref, nv_ref):
        g_ = g_ref[...]
        m_ = ADAM_B1 * m_ref[...] + (1.0 - ADAM_B1) * g_
        v_ = ADAM_B2 * v_ref[...] + (1.0 - ADAM_B2) * (g_ * g_)
        m_hat = m_ / (1.0 - ADAM_B1 ** ADAM_STEP)
        v_hat = v_ / (1.0 - ADAM_B2 ** ADAM_STEP)
        d_ref[...] = -ADAM_LR * (m_hat / (jnp.sqrt(v_hat) + ADAM_EPS) + ADAM_WD * w_ref[...])
        nm_ref[...] = m_
        nv_ref[...] = v_

    spec = pl.BlockSpec((tr, cols), lambda i: (i, 0))
    sh = jax.ShapeDtypeStruct((rows, cols), F32)
    return pl.pallas_call(
        body, name=name, grid=(rows // tr,), in_specs=[spec] * 4, out_specs=[spec] * 3, out_shape=[sh] * 3,
        compiler_params=_cparams(("parallel",)),
    )(w, g, m, v)


def kernel(x, meta_tokens, norm_w, ffn_w_gate, ffn_w_up, ffn_w_down, rel_bias_table, even_w_in, even_conv_w, swa_sinks, dn_a_log, dn_dt_bias, dn_norm_w, even_w_out, odd_w_in, gla_w_gate_up, gla_b_gate, gla_norm_w, odd_w_out, loss_target, m_meta_tokens, m_norm_w, m_ffn_w_gate, m_ffn_w_up, m_ffn_w_down, m_rel_bias_table, m_even_w_in, m_even_conv_w, m_swa_sinks, m_dn_a_log, m_dn_dt_bias, m_dn_norm_w, m_even_w_out, m_odd_w_in, m_gla_w_gate_up, m_gla_b_gate, m_gla_norm_w, m_odd_w_out, v_meta_tokens, v_norm_w, v_ffn_w_gate, v_ffn_w_up, v_ffn_w_down, v_rel_bias_table, v_even_w_in, v_even_conv_w, v_swa_sinks, v_dn_a_log, v_dn_dt_bias, v_dn_norm_w, v_even_w_out, v_odd_w_in, v_gla_w_gate_up, v_gla_b_gate, v_gla_norm_w, v_odd_w_out):
    ws = [meta_tokens, norm_w, ffn_w_gate, ffn_w_up, ffn_w_down, rel_bias_table, even_w_in, even_conv_w, swa_sinks, dn_a_log,
          dn_dt_bias, dn_norm_w, even_w_out, odd_w_in, gla_w_gate_up, gla_b_gate, gla_norm_w, odd_w_out]
    ms = [m_meta_tokens, m_norm_w, m_ffn_w_gate, m_ffn_w_up, m_ffn_w_down, m_rel_bias_table, m_even_w_in, m_even_conv_w,
          m_swa_sinks, m_dn_a_log, m_dn_dt_bias, m_dn_norm_w, m_even_w_out, m_odd_w_in, m_gla_w_gate_up, m_gla_b_gate,
          m_gla_norm_w, m_odd_w_out]
    vs = [v_meta_tokens, v_norm_w, v_ffn_w_gate, v_ffn_w_up, v_ffn_w_down, v_rel_bias_table, v_even_w_in, v_even_conv_w,
          v_swa_sinks, v_dn_a_log, v_dn_dt_bias, v_dn_norm_w, v_even_w_out, v_odd_w_in, v_gla_w_gate_up, v_gla_b_gate,
          v_gla_norm_w, v_odd_w_out]
    short = [n for n, _ in NAMES]
    w = dict(zip(short, ws))
    m = dict(zip(short, ms))
    v = dict(zip(short, vs))

    wt = {n: _to_t(n, w[n]) for n in BIG}
    own = {n: wt[n].astype(BF16) for n in BIG}
    sflag = (2 * lax.axis_index("x") + lax.axis_index("y")).astype(jnp.int32).reshape(1)
    cflag = lax.axis_index("c").astype(jnp.int32).reshape(1)
    is0 = cflag[0] == 0
    fill = lambda got, pieces: [lax.dynamic_update_index_in_dim(g_, p_, sflag[0], 0) for g_, p_ in zip(got, pieces)]
    pieces = [weight_pieces(own, l) for l in range(2)]
    small = small_from_gathered(ag_small(pack_small(w)))
    W = {**small, **{n: w[n] for n, _ in REPL},
         "layers": [layer_weights(0, fill(ag_layer("ag_layer_0", pieces[0]), pieces[0])), None]}

    def reduce_start(l, items):
        recv = rs_pair(f"rs_pair_{l}", items)
        return [sum_pair(f"sum_pair_{l}{i}", items[0][i], items[1][i], recv[i], cflag) for i in range(len(recv))]

    def reduce_finish(l, mine, parts):
        red = [sum_chips(f"sum_chips_{l}{i}", p, a, sflag) for i, (p, a) in enumerate(zip(parts, mine))]
        got = ag_pair(f"ag_pair_{l}", red)
        return [jnp.where(is0, r_, g_) for r_, g_ in zip(red, got)], [jnp.where(is0, g_, r_) for r_, g_ in zip(red, got)]

    class Layer1Exchange:
        def gather_side(self):
            return gather_side(pieces[1])

        def gather_done(self, got):
            return layer_weights(1, fill(got, pieces[1]))

        def scatter_side(self, items):
            self.mine = reduce_start(1, items)
            return scatter_side(self.mine)

        def scatter_done(self, got):
            self.parts = got

    ex = Layer1Exchange()
    loss_blk, gx, G = core_step(x[0], loss_target[0], W, comm=ex)

    mine0 = reduce_start(0, G["items"][0])
    lay0 = assemble_layer(0, *reduce_finish(0, mine0, rs_chips("rs_chips_0", mine0)))
    lay1 = assemble_layer(1, *reduce_finish(1, ex.mine, ex.parts))
    gt = big_grads(lay0, lay1)
    g_small_pack = small_allreduce(pack_small_grads(G))
    g = {**{n: _from_t(n, gt[n]) for n in BIG}, **unpack_small(g_small_pack)}

    delta, new_m, new_v = {}, {}, {}
    for n in BIG:
        shp = wt[n].shape
        two = lambda t: t.reshape(-1, D)
        d_, m_, v_ = adamw_call("adamw_" + n, two(wt[n]), two(gt[n]), two(_to_t(n, m[n])), two(_to_t(n, v[n])))
        delta[n], new_m[n], new_v[n] = (_from_t(n, t.reshape(shp)) for t in (d_, m_, v_))
    d_, m_, v_ = adamw_call("adamw_small", pack_small(w), g_small_pack, pack_small(m), pack_small(v))
    delta.update(unpack_small(d_))
    new_m.update(unpack_small(m_))
    new_v.update(unpack_small(v_))

    loss = lax.psum(loss_blk[0, 0], ("x", "y", "c"))
    return (loss, gx[None], *[g[n] for n in short], *[delta[n] for n in short], *[new_m[n] for n in short],
            *[new_v[n] for n in short])
```

```python
import functools
import math

import numpy as np
import jax
import jax.numpy as jnp
from jax import lax
from jax.experimental import pallas as pl
from jax.experimental.pallas import tpu as pltpu

F32 = jnp.float32
BF16 = jnp.bfloat16
HI = lax.Precision.HIGHEST

D = 1024
N_META = 16
PADR = 128
ZROWS = PADR - N_META
D_FF = 2816
NSH = 4
FSH = D_FF // NSH
EPS = 1e-6
NEG = -1e30
CH = 64
CPS = 2
BLK = 128
LANE = 128
VMEM_LIMIT = 56 * 1024 * 1024
FFN_SUB = 4

E_QA, E_KA, E_VA, E_QB, E_KB, E_VB, E_ZB, E_BA, E_END = 0, 1024, 1280, 1536, 2048, 2560, 3072, 3584, 4096


def _even_in_map():
    m = np.full((E_END,), -1, np.int64)
    for h in range(8):
        m[E_QA + h * 128:E_QA + h * 128 + 64] = np.arange(h * 64, (h + 1) * 64)
    for h in range(2):
        m[E_KA + h * 128:E_KA + h * 128 + 64] = 512 + np.arange(h * 64, (h + 1) * 64)
        m[E_VA + h * 128:E_VA + h * 128 + 64] = 640 + np.arange(h * 64, (h + 1) * 64)
    m[E_QB:E_QB + 2048] = 768 + np.arange(2048)
    m[E_BA:E_BA + 8] = 2816 + np.arange(8)
    return m


def _even_out_map():
    m = np.full((1536,), -1, np.int64)
    for h in range(8):
        m[h * 128:h * 128 + 64] = np.arange(h * 64, (h + 1) * 64)
    m[1024:1536] = 512 + np.arange(512)
    return m


O_Q, O_K, O_V, O_G, O_GK, O_END = 0, 512, 1024, 2048, 3072, 3584


def _odd_in_map():
    m = np.full((O_END,), -1, np.int64)
    m[:3072] = np.arange(3072)
    m[O_GK:O_GK + 16] = 3072 + np.arange(16)
    return m


def _inverse(m, n):
    inv = np.zeros((n,), np.int64)
    for p, o in enumerate(m):
        if o >= 0:
            inv[o] = p
    return inv


def _take_pad(w, m, axis):
    t = jnp.take(w, jnp.asarray(np.maximum(m, 0)), axis=axis)
    shape = [1] * w.ndim
    shape[axis] = m.shape[0]
    return jnp.where(jnp.asarray(m >= 0).reshape(shape), t, jnp.zeros((), w.dtype))


def _mm(a, b, prec=HI):
    return lax.dot_general(a, b, (((1,), (0,)), ((), ())), precision=prec, preferred_element_type=F32)


def _mm_nt(a, b, prec=HI):
    return lax.dot_general(a, b, (((1,), (1,)), ((), ())), precision=prec, preferred_element_type=F32)


def _mm_tn(a, b, prec=HI):
    return lax.dot_general(a, b, (((0,), (0,)), ((), ())), precision=prec, preferred_element_type=F32)


def _bdot(a, b, dims):
    return lax.dot_general(a.astype(BF16), b.astype(BF16), (dims, ((), ())), preferred_element_type=F32)


@jax.custom_vjp
def _bmm(a, b):
    return _bdot(a, b, ((1,), (0,)))


@jax.custom_vjp
def _bmm_nt(a, b):
    return _bdot(a, b, ((1,), (1,)))


@jax.custom_vjp
def _bmm_tn(a, b):
    return _bdot(a, b, ((0,), (0,)))


_bmm.defvjp(lambda a, b: (_bmm(a, b), (a, b)), lambda r, g: (_bmm_nt(g, r[1]), _bmm_tn(r[0], g)))
_bmm_nt.defvjp(lambda a, b: (_bmm_nt(a, b), (a, b)), lambda r, g: (_bmm(g, r[1]), _bmm_tn(g, r[0])))
_bmm_tn.defvjp(lambda a, b: (_bmm_tn(a, b), (a, b)), lambda r, g: (_bmm_nt(r[1], g), _bmm(r[0], g)))


def _hi_lo(x):
    h = x.astype(BF16)
    return h, (x - h.astype(F32)).astype(BF16)


def _xdot(a, b, dims):
    ah, al = _hi_lo(a)
    bh, bl = _hi_lo(b)
    d = lambda p, q: lax.dot_general(p, q, (dims, ((), ())), preferred_element_type=F32)
    return d(ah, bh) + (d(ah, bl) + d(al, bh))


@jax.custom_vjp
def _xmm(a, b):
    return _xdot(a, b, ((1,), (0,)))


@jax.custom_vjp
def _xmm_nt(a, b):
    return _xdot(a, b, ((1,), (1,)))


@jax.custom_vjp
def _xmm_tn(a, b):
    return _xdot(a, b, ((0,), (0,)))


_xmm.defvjp(lambda a, b: (_xmm(a, b), (a, b)), lambda r, g: (_xmm_nt(g, r[1]), _xmm_tn(r[0], g)))
_xmm_nt.defvjp(lambda a, b: (_xmm_nt(a, b), (a, b)), lambda r, g: (_xmm(g, r[1]), _xmm_tn(g, r[0])))
_xmm_tn.defvjp(lambda a, b: (_xmm_tn(a, b), (a, b)), lambda r, g: (_xmm_nt(r[1], g), _xmm(r[0], g)))


def _sum01(m01, x, dims):
    h, l = _hi_lo(x)
    l2 = (x - h.astype(F32) - l.astype(F32)).astype(BF16)
    m = m01.astype(BF16)
    d = lambda q: lax.dot_general(m, q, (dims, ((), ())), preferred_element_type=F32)
    return d(h) + (d(l) + d(l2))


@jax.custom_vjp
def _cumsum_rows(x):
    n = x.shape[0]
    tri = lax.broadcasted_iota(jnp.int32, (n, n), 0) >= lax.broadcasted_iota(jnp.int32, (n, n), 1)
    return _sum01(tri, x, ((1,), (0,)))


def _cumsum_rows_b(_, g):
    n = g.shape[0]
    tri = lax.broadcasted_iota(jnp.int32, (n, n), 0) >= lax.broadcasted_iota(jnp.int32, (n, n), 1)
    return (_sum01(tri, g, ((0,), (0,))),)


_cumsum_rows.defvjp(lambda x: (_cumsum_rows(x), None), _cumsum_rows_b)


@functools.partial(jax.custom_vjp, nondiff_argnums=(1,))
def _colsum_as_rows(x, width):
    return _colsum_impl(x, width)


def _colsum_impl(x, width):
    h, l = _hi_lo(x)
    l2 = (x - h.astype(F32) - l.astype(F32)).astype(BF16)
    ones = jnp.ones((x.shape[0], width), BF16)
    d = lambda q: lax.dot_general(q, ones, (((0,), (0,)), ((), ())), preferred_element_type=F32)
    return d(h) + (d(l) + d(l2))


def _colsum_as_rows_f(x, width):
    return _colsum_impl(x, width), x.shape[0]


def _colsum_as_rows_b(width, n, g):
    return (_sum01(jnp.ones((n, width), F32), g, ((1,), (1,))),)


_colsum_as_rows.defvjp(_colsum_as_rows_f, _colsum_as_rows_b)


def _rms(x, w):
    return x * lax.rsqrt(jnp.mean(x * x, axis=-1, keepdims=True) + EPS) * w


def _sigmoid(x):
    return 1.0 / (1.0 + jnp.exp(-x))


def _silu(x):
    return x * _sigmoid(x)


def _softplus(x):
    return jnp.maximum(x, 0.0) + jnp.log(1.0 + jnp.exp(-jnp.abs(x)))


def _lane_pick(row, idx):
    lane = lax.broadcasted_iota(jnp.int32, row.shape, row.ndim - 1)
    return jnp.sum(jnp.where(lane == idx, row, 0.0), axis=-1, keepdims=True)


def _row_ids(row0, n):
    return row0 + lax.broadcasted_iota(jnp.int32, (n, 1), 0)


def _pick(m, cap):
    best = 64
    for t in range(64, min(m, cap) + 1, 64):
        if m % t == 0:
            best = t
    return best


def _cparams(sem):
    return pltpu.CompilerParams(dimension_semantics=sem, vmem_limit_bytes=VMEM_LIMIT)


def mm_nn(a, b, name, out_dtype=F32):
    M, K = a.shape
    N = b.shape[1]
    tm = _pick(M, 1408 if K <= 2048 else 704)
    tn = _pick(N, 512)

    def body(a_ref, b_ref, o_ref):
        o_ref[...] = _mm(a_ref[...], b_ref[...], None).astype(o_ref.dtype)

    return pl.pallas_call(
        body, name=name, grid=(N // tn, M // tm),
        in_specs=[pl.BlockSpec((tm, K), lambda j, i: (i, 0)), pl.BlockSpec((K, tn), lambda j, i: (0, j))],
        out_specs=pl.BlockSpec((tm, tn), lambda j, i: (i, j)),
        out_shape=jax.ShapeDtypeStruct((M, N), out_dtype),
        compiler_params=_cparams(("parallel", "parallel")),
    )(a, b)


def mm_nt(a, b, name, out_dtype=F32):
    M, K = a.shape
    N = b.shape[0]
    tm = _pick(M, 768)
    tn = _pick(N, 512)

    def body(a_ref, b_ref, o_ref):
        o_ref[...] = _mm_nt(a_ref[...], b_ref[...], None).astype(o_ref.dtype)

    return pl.pallas_call(
        body, name=name, grid=(N // tn, M // tm),
        in_specs=[pl.BlockSpec((tm, K), lambda j, i: (i, 0)), pl.BlockSpec((tn, K), lambda j, i: (j, 0))],
        out_specs=pl.BlockSpec((tm, tn), lambda j, i: (i, j)),
        out_shape=jax.ShapeDtypeStruct((M, N), out_dtype),
        compiler_params=_cparams(("parallel", "parallel")),
    )(a, b)


def mm_tn(a, b, name):
    M, K = a.shape
    N = b.shape[1]
    tk = _pick(K, 512)
    tn = _pick(N, 512)

    def body(a_ref, b_ref, o_ref):
        o_ref[...] = _mm_tn(a_ref[...], b_ref[...], None)

    return pl.pallas_call(
        body, name=name, grid=(K // tk, N // tn),
        in_specs=[pl.BlockSpec((M, tk), lambda i, j: (0, i)), pl.BlockSpec((M, tn), lambda i, j: (0, j))],
        out_specs=pl.BlockSpec((tk, tn), lambda i, j: (i, j)),
        out_shape=jax.ShapeDtypeStruct((K, N), F32),
        compiler_params=_cparams(("parallel", "parallel")),
    )(a, b)


def _row_specs(rows, tm):
    return [pl.BlockSpec((tm, w), functools.partial(lambda i, cb: (i, cb), cb=cb)) for (_, w, cb) in rows]


def _param_specs(params):
    return [pl.BlockSpec(p.shape, functools.partial(lambda i, nd: (0,) * nd, nd=p.ndim)) for p in params]


def rowwise_fwd(name, fn, rows, params, outs, tm=None):
    M = rows[0][0].shape[0]
    tm = tm or _pick(M, 704)
    nr, npar = len(rows), len(params)

    def body(*refs):
        row0 = pl.program_id(0) * tm
        vals = [r[...].astype(F32) for r in refs[:nr]] + [p[...] for p in refs[nr:nr + npar]]
        res = fn(row0, *vals)
        for o_ref, r in zip(refs[nr + npar:], res):
            o_ref[...] = r.astype(o_ref.dtype)

    return pl.pallas_call(
        body, name=name, grid=(M // tm,),
        in_specs=_row_specs(rows, tm) + _param_specs(params),
        out_specs=[pl.BlockSpec((tm, w), lambda i: (i, 0)) for (w, _) in outs],
        out_shape=[jax.ShapeDtypeStruct((M, w), dt) for (w, dt) in outs],
        compiler_params=_cparams(("parallel",)),
    )(*[r[0] for r in rows], *params)


def rowwise_bwd(name, fn, rows, params, douts, drow_dtypes, tm=None):
    M = rows[0][0].shape[0]
    tm = tm or _pick(M, 704)
    nr, npar, nd = len(rows), len(params), len(douts)
    want = [k for k, dt in enumerate(drow_dtypes) if dt is not None]

    def body(*refs):
        i = pl.program_id(0)
        row0 = i * tm
        vals = [r[...].astype(F32) for r in refs[:nr]] + [p[...] for p in refs[nr:nr + npar]]
        cots = tuple(d[...].astype(F32) for d in refs[nr + npar:nr + npar + nd])
        _, vjp = jax.vjp(functools.partial(fn, row0), *vals)
        grads = vjp(cots)
        o_refs = refs[nr + npar + nd:]
        for o_ref, k in zip(o_refs[:len(want)], want):
            o_ref[...] = grads[k].astype(o_ref.dtype)
        for o_ref, g in zip(o_refs[len(want):], grads[nr:]):
            @pl.when(i == 0)
            def _():
                o_ref[...] = g

            @pl.when(i > 0)
            def _():
                o_ref[...] += g

    res = pl.pallas_call(
        body, name=name, grid=(M // tm,),
        in_specs=_row_specs(rows, tm) + _param_specs(params) + _row_specs(douts, tm),
        out_specs=[pl.BlockSpec((tm, rows[k][1]), lambda i: (i, 0)) for k in want] + _param_specs(params),
        out_shape=[jax.ShapeDtypeStruct((M, rows[k][1]), drow_dtypes[k]) for k in want]
        + [jax.ShapeDtypeStruct(p.shape, F32) for p in params],
        compiler_params=_cparams(("arbitrary",)),
    )(*[r[0] for r in rows], *params, *[d[0] for d in douts])
    return res[:len(want)], res[len(want):]


def _fn_prenorm(row0, h, wpre):
    return (_rms(h, wpre),)


def _fn_resnorm(scale, row0, h, f, wpost, wpre):
    h2 = h + scale * _rms(f, wpost)
    return h2, _rms(h2, wpre)


def _fn_res_last(scale, row0, h, f, wpost):
    return (h + scale * _rms(f, wpost),)


def ffn_fwd(name, xn, wg, wu, wd, j):
    M = xn.shape[0]
    tm = _pick(M, 704)

    ts = tm // FFN_SUB

    def body(x_ref, wg_ref, wu_ref, wd_ref, f_ref, a_ref, b_ref):
        @pl.when(pl.program_id(1) == 0)
        def _():
            f_ref[...] = jnp.zeros_like(f_ref)

        for r in range(FFN_SUB):
            rows = pl.ds(r * ts, ts)
            x = x_ref[rows, :]
            a = _mm_nt(x, wg_ref[...], None)
            b = _mm_nt(x, wu_ref[...], None)
            f_ref[rows, :] += _mm((_silu(a) * b).astype(BF16), wd_ref[...], None)
            a_ref[rows, :] = a.astype(BF16)
            b_ref[rows, :] = b.astype(BF16)

    wspec = wdspec = pl.BlockSpec((None, None, FSH, D), lambda i, s: (s, j, 0, 0))
    abspec = pl.BlockSpec((None, tm, FSH), lambda i, s: (s, i, 0))
    return pl.pallas_call(
        body, name=name, grid=(M // tm, NSH),
        in_specs=[pl.BlockSpec((tm, D), lambda i, s: (i, 0)), wspec, wspec, wdspec],
        out_specs=[pl.BlockSpec((tm, D), lambda i, s: (i, 0)), abspec, abspec],
        out_shape=[jax.ShapeDtypeStruct((M, D), F32), jax.ShapeDtypeStruct((NSH, M, FSH), BF16),
                   jax.ShapeDtypeStruct((NSH, M, FSH), BF16)],
        compiler_params=_cparams(("parallel", "arbitrary")),
    )(xn, wg, wu, wd)


def ffn_bwd_x(name, df, a, b, wg, wu, wd, j):
    M = df.shape[0]
    tm = _pick(M, 704)

    ts = tm // FFN_SUB

    def body(df_ref, a_ref, b_ref, wg_ref, wu_ref, wd_ref, dx_ref, da_ref, db_ref, hm_ref):
        @pl.when(pl.program_id(1) == 0)
        def _():
            dx_ref[...] = jnp.zeros_like(dx_ref)

        for r in range(FFN_SUB):
            rows = pl.ds(r * ts, ts)
            a_ = a_ref[rows, :].astype(F32)
            b_ = b_ref[rows, :].astype(F32)
            dh = _mm_nt(df_ref[rows, :], wd_ref[...], None)
            sig = _sigmoid(a_)
            sil = a_ * sig
            da = (dh * b_ * (sig * (1.0 + a_ * (1.0 - sig)))).astype(BF16)
            db = (dh * sil).astype(BF16)
            dx_ref[rows, :] += _mm(da, wg_ref[...], None) + _mm(db, wu_ref[...], None)
            da_ref[rows, :] = da
            db_ref[rows, :] = db
            hm_ref[rows, :] = (sil * b_).astype(BF16)

    wspec = wdspec = pl.BlockSpec((None, None, FSH, D), lambda i, s: (s, j, 0, 0))
    abspec = pl.BlockSpec((None, tm, FSH), lambda i, s: (s, i, 0))
    ab = jax.ShapeDtypeStruct((NSH, M, FSH), BF16)
    return pl.pallas_call(
        body, name=name, grid=(M // tm, NSH),
        in_specs=[pl.BlockSpec((tm, D), lambda i, s: (i, 0)), abspec, abspec, wspec, wspec, wdspec],
        out_specs=[pl.BlockSpec((tm, D), lambda i, s: (i, 0)), abspec, abspec, abspec],
        out_shape=[jax.ShapeDtypeStruct((M, D), F32), ab, ab, ab],
        compiler_params=_cparams(("parallel", "arbitrary")),
    )(df, a, b, wg, wu, wd)


def ffn_bwd_w(name, xn, df, da, db, hm):
    M = xn.shape[0]
    tm = _pick(M, 704)
    nt = M // tm

    def body(x_ref, df_ref, da_ref, db_ref, hm_ref, dwg_ref, dwu_ref, dwd_ref, ag, au, ad):
        i = pl.program_id(1)
        x = x_ref[...]
        g = _mm_tn(da_ref[...], x, None)
        u = _mm_tn(db_ref[...], x, None)
        d = _mm_tn(hm_ref[...], df_ref[...], None)

        @pl.when(i == 0)
        def _():
            ag[...] = g
            au[...] = u
            ad[...] = d

        @pl.when(i > 0)
        def _():
            ag[...] += g
            au[...] += u
            ad[...] += d

        @pl.when(i == nt - 1)
        def _():
            dwg_ref[...] = ag[...].astype(BF16)
            dwu_ref[...] = au[...].astype(BF16)
            dwd_ref[...] = ad[...].astype(BF16)

    xspec = pl.BlockSpec((tm, D), lambda s, i: (i, 0))
    abspec = pl.BlockSpec((None, tm, FSH), lambda s, i: (s, i, 0))
    return pl.pallas_call(
        body, name=name, grid=(NSH, nt),
        in_specs=[xspec, xspec, abspec, abspec, abspec],
        out_specs=[pl.BlockSpec((None, FSH, D), lambda s, i: (s, 0, 0))] * 3,
        out_shape=[jax.ShapeDtypeStruct((NSH, FSH, D), BF16)] * 3,
        scratch_shapes=[pltpu.VMEM((FSH, D), F32)] * 3,
        compiler_params=_cparams(("parallel", "arbitrary")),
    )(xn, df, da, db, hm)


def _t5_bucket_np(rel):
    n = np.maximum(rel, 0)
    n_f = np.maximum(n, 1).astype(np.float32)
    large = 16 + (np.log(n_f / np.float32(16)) / np.float32(math.log(8.0)) * np.float32(16)).astype(np.int32)
    large = np.minimum(large, 31)
    return np.where(n < 16, n, large).astype(np.int32)


def _swa_bucket_ids():
    qi = np.arange(BLK)[:, None]
    kj = np.arange(BLK)[None, :]
    out = np.full((3, BLK, 3 * BLK), -1, np.int32)
    for v in range(3):
        pos_q = v * BLK + qi - ZROWS
        rel_m = pos_q - (kj - ZROWS)
        ok_m = (kj >= ZROWS) & (rel_m >= 0) & (pos_q >= 0)
        out[v, :, 0:BLK] = np.where(ok_m, _t5_bucket_np(rel_m), -1)
        pos_kp = (v - 1) * BLK + kj - ZROWS
        rel_p = BLK + qi - kj
        ok_p = (pos_kp >= N_META) & (rel_p >= 0) & (rel_p < BLK) & np.full_like(ok_m, v >= 1)
        out[v, :, BLK:2 * BLK] = np.where(ok_p, _t5_bucket_np(rel_p), -1)
        pos_kc = v * BLK + kj - ZROWS
        rel_c = qi - kj
        ok_c = (pos_kc >= N_META) & (rel_c >= 0) & (rel_c < BLK)
        out[v, :, 2 * BLK:] = np.where(ok_c, _t5_bucket_np(rel_c), -1)
    return out


def swa_bias_fwd(table, ids):
    def body(t_ref, id_ref, o_ref):
        for v in range(3):
            for h in range(8):
                o_ref[v, h] = jnp.where(id_ref[v] < 0, NEG, 0.0)

            def step(b, carry):
                hit = id_ref[v] == b
                for h in range(8):
                    o_ref[v, h] += jnp.where(hit, t_ref[b, h], 0.0)
                return carry

            lax.fori_loop(0, 32, step, 0)

    return pl.pallas_call(
        body, name="swa_bias_fwd",
        in_specs=[pl.BlockSpec(memory_space=pltpu.SMEM), pl.BlockSpec(memory_space=pltpu.VMEM)],
        out_specs=pl.BlockSpec(memory_space=pltpu.VMEM),
        out_shape=jax.ShapeDtypeStruct((3, 8, BLK, 3 * BLK), F32),
        compiler_params=pltpu.CompilerParams(vmem_limit_bytes=VMEM_LIMIT),
    )(table, ids)


def swa_bias_bwd(dbias, ids):
    def body(d_ref, id_ref, o_ref):
        r = lax.broadcasted_iota(jnp.int32, (32, LANE), 0)
        c = lax.broadcasted_iota(jnp.int32, (32, LANE), 1)

        def step(b, acc):
            for v in range(3):
                hit = id_ref[v] == b
                for h in range(8):
                    m = jnp.where(hit, d_ref[v, h], 0.0)
                    s = jnp.sum(jnp.sum(m, axis=1, keepdims=True), axis=0, keepdims=True)
                    acc = acc + jnp.where((r == b) & (c == h), s, 0.0)
            return acc

        o_ref[...] = lax.fori_loop(0, 32, step, jnp.zeros((32, LANE), F32))

    return pl.pallas_call(
        body, name="swa_bias_bwd",
        in_specs=[pl.BlockSpec(memory_space=pltpu.VMEM), pl.BlockSpec(memory_space=pltpu.VMEM)],
        out_specs=pl.BlockSpec(memory_space=pltpu.VMEM),
        out_shape=jax.ShapeDtypeStruct((32, LANE), F32),
        compiler_params=pltpu.CompilerParams(vmem_limit_bytes=VMEM_LIMIT),
    )(dbias, ids)


def _swa_block(q, k3, v3, bias, sinks):
    outs = []
    for hk in range(2):
        kh = k3[:, hk * 128:(hk + 1) * 128]
        vh = v3[:, hk * 128:(hk + 1) * 128]
        for g in range(4):
            h = hk * 4 + g
            s = _bmm_nt(q[:, h * 128:(h + 1) * 128], kh) * 0.125 + bias[h]
            sink = _lane_pick(sinks, h)
            m = lax.stop_gradient(jnp.maximum(jnp.max(s, axis=-1, keepdims=True), sink))
            e = jnp.exp(s - m)
            den = jnp.sum(e, axis=-1, keepdims=True) + jnp.exp(sink - m)
            outs.append(_bmm(e / den, vh))
    return jnp.concatenate(outs, axis=1)


def _swa_in_specs():
    qs = pl.BlockSpec((BLK, 1024), lambda n: (n, E_QA // 1024))
    ks = [pl.BlockSpec((BLK, 256), lambda n: (0, E_KA // 256)),
          pl.BlockSpec((BLK, 256), lambda n: (jnp.maximum(n - 1, 0), E_KA // 256)),
          pl.BlockSpec((BLK, 256), lambda n: (n, E_KA // 256))]
    vs = [pl.BlockSpec((BLK, 256), lambda n: (0, E_VA // 256)),
          pl.BlockSpec((BLK, 256), lambda n: (jnp.maximum(n - 1, 0), E_VA // 256)),
          pl.BlockSpec((BLK, 256), lambda n: (n, E_VA // 256))]
    bs = pl.BlockSpec((None, 8, BLK, 3 * BLK), lambda n: (jnp.minimum(n, 2), 0, 0, 0))
    ss = pl.BlockSpec((1, LANE), lambda n: (0, 0))
    return [qs] + ks + vs + [bs, ss]


def swa_fwd(proj, bias, sinks):
    M = proj.shape[0]

    def body(q_ref, k0, k1, k2, v0, v1, v2, b_ref, s_ref, o_ref):
        k3 = jnp.concatenate([k0[...], k1[...], k2[...]], axis=0)
        v3 = jnp.concatenate([v0[...], v1[...], v2[...]], axis=0)
        o_ref[...] = _swa_block(q_ref[...], k3, v3, b_ref[...], s_ref[...]).astype(o_ref.dtype)

    return pl.pallas_call(
        body, name="swa_fwd", grid=(M // BLK,),
        in_specs=_swa_in_specs(),
        out_specs=pl.BlockSpec((BLK, 1024), lambda n: (n, 0)),
        out_shape=jax.ShapeDtypeStruct((M, 1024), BF16),
        compiler_params=_cparams(("parallel",)),
    )(proj, proj, proj, proj, proj, proj, proj, bias, sinks)


def swa_bwd(proj, bias, sinks, do):
    M = proj.shape[0]

    def body(q_ref, k0, k1, k2, v0, v1, v2, b_ref, s_ref, do_ref, dq_ref, dk_ref, dv_ref, db_ref, ds_ref):
        n = pl.program_id(0)

        @pl.when(n == 0)
        def _():
            dk_ref[...] = jnp.zeros_like(dk_ref)
            dv_ref[...] = jnp.zeros_like(dv_ref)
            ds_ref[...] = jnp.zeros_like(ds_ref)

        k3 = jnp.concatenate([k0[...], k1[...], k2[...]], axis=0)
        v3 = jnp.concatenate([v0[...], v1[...], v2[...]], axis=0)
        _, vjp = jax.vjp(_swa_block, q_ref[...], k3, v3, b_ref[...], s_ref[...])
        dq, dk3, dv3, dbias, dsink = vjp(do_ref[...].astype(F32))
        dq_ref[...] = dq
        prev = pl.multiple_of(jnp.maximum(n - 1, 0) * BLK, BLK)
        cur = pl.multiple_of(n * BLK, BLK)
        dk_ref[pl.ds(0, BLK), :] += dk3[0:BLK]
        dv_ref[pl.ds(0, BLK), :] += dv3[0:BLK]
        dk_ref[pl.ds(prev, BLK), :] += dk3[BLK:2 * BLK]
        dv_ref[pl.ds(prev, BLK), :] += dv3[BLK:2 * BLK]
        dk_ref[pl.ds(cur, BLK), :] += dk3[2 * BLK:]
        dv_ref[pl.ds(cur, BLK), :] += dv3[2 * BLK:]
        ds_ref[...] += dsink

        @pl.when(n <= 2)
        def _():
            db_ref[...] = dbias

        @pl.when(n > 2)
        def _():
            db_ref[...] += dbias

    return pl.pallas_call(
        body, name="swa_bwd", grid=(M // BLK,),
        in_specs=_swa_in_specs() + [pl.BlockSpec((BLK, 1024), lambda n: (n, 0))],
        out_specs=[pl.BlockSpec((BLK, 1024), lambda n: (n, 0)),
                   pl.BlockSpec((M, 256), lambda n: (0, 0)), pl.BlockSpec((M, 256), lambda n: (0, 0)),
                   pl.BlockSpec((None, 8, BLK, 3 * BLK), lambda n: (jnp.minimum(n, 2), 0, 0, 0)),
                   pl.BlockSpec((1, LANE), lambda n: (0, 0))],
        out_shape=[jax.ShapeDtypeStruct((M, 1024), F32), jax.ShapeDtypeStruct((M, 256), F32),
                   jax.ShapeDtypeStruct((M, 256), F32), jax.ShapeDtypeStruct((3, 8, BLK, 3 * BLK), F32),
                   jax.ShapeDtypeStruct((1, LANE), F32)],
        compiler_params=_cparams(("arbitrary",)),
    )(proj, proj, proj, proj, proj, proj, proj, bias, sinks, do)


def _shift_rows_impl(x, k):
    n = x.shape[0]
    rolled = pltpu.roll(x, k, 0)
    return jnp.where(_row_ids(0, n) >= k, rolled, 0.0)


def _unshift_rows_impl(g, k):
    n = g.shape[0]
    rolled = pltpu.roll(g, n - k, 0)
    return jnp.where(_row_ids(0, n) < n - k, rolled, 0.0)


@functools.partial(jax.custom_vjp, nondiff_argnums=(1,))
def _shift_rows(x, k):
    return _shift_rows_impl(x, k)


def _shift_rows_f(x, k):
    return _shift_rows_impl(x, k), None


def _shift_rows_b(k, _, g):
    return (_unshift_rows_impl(g, k),)


_shift_rows.defvjp(_shift_rows_f, _shift_rows_b)


def _conv_silu(x, w):
    rid = lax.broadcasted_iota(jnp.int32, w.shape, 0)
    y = x * jnp.sum(jnp.where(rid == 3, w, 0.0), axis=0, keepdims=True)
    for k in range(1, 4):
        y = y + _shift_rows(x, k) * jnp.sum(jnp.where(rid == 3 - k, w, 0.0), axis=0, keepdims=True)
    y = jnp.where(_row_ids(0, x.shape[0]) >= ZROWS, y, 0.0)
    return _silu(y)


def conv_fwd(proj, conv_w):
    M = proj.shape[0]
    nb = conv_w.shape[1] // LANE

    def body(x_ref, w_ref, o_ref):
        o_ref[...] = _conv_silu(x_ref[...], w_ref[...])

    return pl.pallas_call(
        body, name="conv_fwd", grid=(nb,),
        in_specs=[pl.BlockSpec((M, LANE), lambda c: (0, E_QB // LANE + c)), pl.BlockSpec((4, LANE), lambda c: (0, c))],
        out_specs=pl.BlockSpec((M, LANE), lambda c: (0, c)),
        out_shape=jax.ShapeDtypeStruct((M, conv_w.shape[1]), F32),
        compiler_params=_cparams(("parallel",)),
    )(proj, conv_w)


def conv_bwd(proj, conv_w, dy):
    M = proj.shape[0]
    nb = conv_w.shape[1] // LANE

    def body(x_ref, w_ref, dy_ref, dx_ref, dw_ref):
        _, vjp = jax.vjp(_conv_silu, x_ref[...], w_ref[...])
        dx, dw = vjp(dy_ref[...])
        dx_ref[...] = dx
        dw_ref[...] = dw

    return pl.pallas_call(
        body, name="conv_bwd", grid=(nb,),
        in_specs=[pl.BlockSpec((M, LANE), lambda c: (0, E_QB // LANE + c)), pl.BlockSpec((4, LANE), lambda c: (0, c)),
                  pl.BlockSpec((M, LANE), lambda c: (0, c))],
        out_specs=[pl.BlockSpec((M, LANE), lambda c: (0, c)), pl.BlockSpec((4, LANE), lambda c: (0, c))],
        out_shape=[jax.ShapeDtypeStruct((M, conv_w.shape[1]), F32), jax.ShapeDtypeStruct(conv_w.shape, F32)],
        compiler_params=_cparams(("parallel",)),
    )(proj, conv_w, dy)


def _fn_dn_prep(row0, yq, yk, ba, dnp):
    tm = yq.shape[0]
    real = _row_ids(row0, tm) >= ZROWS
    qs, ks, gs, bs = [], [], [], []
    for h in range(4):
        q = yq[:, h * 128:(h + 1) * 128]
        k = yk[:, h * 128:(h + 1) * 128]
        qs.append(q * lax.rsqrt(jnp.sum(q * q, axis=-1, keepdims=True) + 1e-6) * (128.0 ** -0.5))
        ks.append(k * lax.rsqrt(jnp.sum(k * k, axis=-1, keepdims=True) + 1e-6))
        beta = _sigmoid(_lane_pick(ba, h))
        g = -jnp.exp(_lane_pick(dnp, h)) * _softplus(_lane_pick(ba, 4 + h) + _lane_pick(dnp, 4 + h))
        g = jnp.where(real, g, 0.0)
        gs.append(jnp.broadcast_to(g, (tm, 128)))
        bs.append(jnp.broadcast_to(beta, (tm, 128)))
    cat = lambda xs: jnp.concatenate(xs, axis=1)
    return cat(qs), cat(ks), cat(gs), cat(bs)


def _zip(f, *lists):
    return [f(*args) for args in zip(*lists)]


def _unit_lower_inv_impl(a):
    n = a[0].shape[0]
    eye = (lax.broadcasted_iota(jnp.int32, (n, n), 0) == lax.broadcasted_iota(jnp.int32, (n, n), 1)).astype(F32)
    nn = ((1,), (0,))
    p = [-x for x in a]
    t = [eye + x for x in p]
    for _ in range(int(math.log2(n)) - 1):
        p = _zip(lambda x: _xdot(x, x, nn), p)
        t = _zip(lambda x, y: x + _xdot(x, y, nn), t, p)
    return t


@jax.custom_vjp
def _unit_lower_inv(a):
    return _unit_lower_inv_impl(a)


def _unit_lower_inv_f(a):
    t = _unit_lower_inv_impl(a)
    return t, t


def _unit_lower_inv_b(t, g):
    tg = _zip(lambda x, y: _xdot(x, y, ((0,), (0,))), t, g)
    return (_zip(lambda x, y: -_xdot(x, y, ((1,), (1,))), tg, t),)


_unit_lower_inv.defvjp(_unit_lower_inv_f, _unit_lower_inv_b)


@jax.custom_vjp
def _known_inv(a, t):
    return t


_known_inv.defvjp(lambda a, t: (t, t), lambda t, g: (_unit_lower_inv_b(t, g)[0], [jnp.zeros_like(x) for x in t]))


def _dn_block(q, k, v, gb, bb, S, t_kept=None):
    nh = len(S)
    r = lax.broadcasted_iota(jnp.int32, (CH, CH), 0)
    c = lax.broadcasted_iota(jnp.int32, (CH, CH), 1)
    tri_incl = r >= c
    gcb = _zip(_cumsum_rows, gb)
    gamma = _zip(lambda x: jnp.where(tri_incl, jnp.exp(jnp.where(tri_incl, x[:, :CH] - x[:, :CH].T, 0.0)), 0.0), gcb)
    kb = _zip(jnp.multiply, k, bb)
    vb = _zip(jnp.multiply, v, bb)
    a = _zip(lambda m, g: jnp.where(r > c, m * g, 0.0), _zip(_bmm_nt, kb, k), gamma)
    t = _unit_lower_inv(a) if t_kept is None else _known_inv(a, list(t_kept))
    eg = _zip(jnp.exp, gcb)
    u = _zip(_xmm, t, vb)
    w = _zip(_xmm, t, _zip(jnp.multiply, kb, eg))
    attn = _zip(lambda m, g: m * g, _zip(_bmm_nt, q, k), gamma)
    gtot = _zip(lambda x: jnp.sum(x, axis=0, keepdims=True), gb)
    k_dec = _zip(lambda x, gt, gc: x * jnp.exp(gt - gc), k, gtot, gcb)
    q_dec = _zip(jnp.multiply, q, eg)
    S = list(S)
    o, starts = [], []
    for i0 in range(0, len(q), nh):
        idx = range(i0, i0 + nh)
        starts.append(list(S))
        v_new = [u[i] - m for i, m in zip(idx, [_bmm(w[i], S[h]) for h, i in enumerate(idx)])]
        oq = [_bmm(q_dec[i], S[h]) for h, i in enumerate(idx)]
        oa = [_bmm(attn[i], vn) for i, vn in zip(idx, v_new)]
        kv = [_bmm_tn(k_dec[i], vn) for i, vn in zip(idx, v_new)]
        o += _zip(jnp.add, oq, oa)
        S = [S[h] * jnp.exp(jnp.broadcast_to(gtot[i], S[h].shape)) + kv[h] for h, i in enumerate(idx)]
    return o, S, starts, t


def _gla_block(q, k, v, glog, S):
    nh = len(S)
    tri = lax.broadcasted_iota(jnp.int32, (CH, CH), 0) >= lax.broadcasted_iota(jnp.int32, (CH, CH), 1)
    bcum = _zip(_cumsum_rows, glog)
    q_dec = _zip(lambda x, b: x * (128.0 ** -0.5) * jnp.exp(b), q, bcum)
    attn = _zip(lambda m: jnp.where(tri, m, 0.0), _zip(_bmm_nt, q_dec, _zip(lambda x, b: x * jnp.exp(-b), k, bcum)))
    o_in = _zip(_bmm, attn, v)
    k_dec = _zip(lambda x, g, b: x * jnp.exp(jnp.sum(g, axis=0, keepdims=True) - b), k, glog, bcum)
    decay = _zip(lambda g, x: jnp.exp(_colsum_as_rows(g, x.shape[1])), glog, v)
    kv = _zip(_bmm_tn, k_dec, v)
    S = list(S)
    o, starts = [], []
    for i0 in range(0, len(q), nh):
        idx = range(i0, i0 + nh)
        starts.append(list(S))
        o += [o_in[i] + m for i, m in zip(idx, [_bmm(q_dec[i], S[h]) for h, i in enumerate(idx)])]
        S = [S[h] * decay[i] + kv[i] for h, i in enumerate(idx)]
    return o, S, starts


class Side:
    def __init__(self, ins, out_shapes, nsem, events):
        self.ins, self.out_shapes, self.nsem, self.events = list(ins), list(out_shapes), nsem, events


def _side_parts(side):
    if side is None:
        return [], [], [], []
    anyspec = pl.BlockSpec(memory_space=pl.ANY)
    return (side.ins, [anyspec] * len(side.ins), side.out_shapes,
            [pltpu.SemaphoreType.DMA((side.nsem,)), pltpu.SemaphoreType.DMA((side.nsem,))])


def _side_run(side, n_steps, in_refs, out_refs, sems):
    if side is None:
        return
    for step, fn in side.events(n_steps, in_refs, out_refs, *sems):
        pl.when(pl.program_id(0) == step)(fn)


def chunk_fwd(name, chunk_fn, ins, dv, side=None, keep=()):
    M = ins[0][0].shape[0]
    NC = M // CH
    N = NC // CPS
    ni, nk = len(ins), len(keep)
    ws = [w for (_, w, _) in ins]
    s_ins, s_specs, s_shapes, s_sems = _side_parts(side)
    ns, nso = len(s_ins), len(s_shapes)

    def body(*refs):
        o0 = ni + ns
        o_ref, sall_ref = refs[o0:o0 + 2]
        k_refs = refs[o0 + 2:o0 + 2 + nk]
        s_ref = refs[o0 + 2 + nk + nso]
        _side_run(side, N, refs[ni:o0], refs[o0 + 2 + nk:o0 + 2 + nk + nso], refs[o0 + 3 + nk + nso:])

        @pl.when(pl.program_id(0) == 0)
        def _():
            s_ref[...] = jnp.zeros_like(s_ref)

        problems = [(cc, h) for cc in range(CPS) for h in range(4)]
        lists = [[r[cc * CH:(cc + 1) * CH, h * w:(h + 1) * w] for cc, h in problems] for r, w in zip(refs[:ni], ws)]
        o, s_new, starts, *kept = chunk_fn(*lists, [s_ref[h] for h in range(4)])
        for b, (cc, h) in enumerate(problems):
            o_ref[cc * CH:(cc + 1) * CH, h * dv:(h + 1) * dv] = o[b]
            sall_ref[h, cc] = starts[cc][h]
            for k_ref, vals in zip(k_refs, kept):
                k_ref[h, cc] = vals[b]
        for h in range(4):
            s_ref[h] = s_new[h]

    per_chunk = lambda r, c: pl.BlockSpec((4, CPS, r, c), lambda n: (0, n, 0, 0))
    specs = [pl.BlockSpec((CPS * CH, 4 * w), functools.partial(lambda n, cb: (n, cb), cb=cb // 4)) for (_, w, cb) in ins]
    res = pl.pallas_call(
        body, name=name, grid=(N,),
        in_specs=specs + s_specs,
        out_specs=[pl.BlockSpec((CPS * CH, 4 * dv), lambda n: (n, 0)), per_chunk(128, dv)] + [per_chunk(r, c) for r, c in keep]
        + [pl.BlockSpec(memory_space=pl.ANY)] * nso,
        out_shape=[jax.ShapeDtypeStruct((M, 4 * dv), F32), jax.ShapeDtypeStruct((4, NC, 128, dv), F32)]
        + [jax.ShapeDtypeStruct((4, NC, r, c), F32) for r, c in keep] + s_shapes,
        scratch_shapes=[pltpu.VMEM((4, 128, dv), F32)] + s_sems,
        compiler_params=_cparams(("arbitrary",)),
    )(*[a for (a, _, _) in ins], *s_ins)
    return res[0], res[1], res[2:2 + nk], res[2 + nk:]


def chunk_bwd(name, chunk_fn, ins, dv, s_all, do, side=None, kept=()):
    M = ins[0][0].shape[0]
    N = M // CH // CPS
    ni, nk = len(ins), len(kept)
    ws = [w for (_, w, _) in ins]
    s_ins, s_specs, s_shapes, s_sems = _side_parts(side)
    ns, nso = len(s_ins), len(s_shapes)

    def body(*refs):
        sall_ref, do_ref = refs[ni:ni + 2]
        k_refs = refs[ni + 2:ni + 2 + nk]
        o0 = ni + 2 + nk + ns
        d_refs = refs[o0:o0 + ni]
        ds_ref = refs[o0 + ni + nso]
        _side_run(side, N, refs[ni + 2 + nk:o0], refs[o0 + ni:o0 + ni + nso], refs[o0 + ni + nso + 1:])

        @pl.when(pl.program_id(0) == 0)
        def _():
            ds_ref[...] = jnp.zeros_like(ds_ref)

        problems = [(cc, h) for cc in range(CPS) for h in range(4)]
        lists = [[r[cc * CH:(cc + 1) * CH, h * w:(h + 1) * w] for cc, h in problems] for r, w in zip(refs[:ni], ws)]
        kept_lists = [[k_ref[h, cc] for cc, h in problems] for k_ref in k_refs]
        _, vjp = jax.vjp(lambda *a: tuple(chunk_fn(*a)[:2]), *lists, [sall_ref[h, 0] for h in range(4)], *kept_lists)
        grads = vjp(([do_ref[cc * CH:(cc + 1) * CH, h * dv:(h + 1) * dv] for cc, h in problems],
                     [ds_ref[h] for h in range(4)]))
        for d_ref, w, g in zip(d_refs, ws, grads[:ni]):
            for b, (cc, h) in enumerate(problems):
                d_ref[cc * CH:(cc + 1) * CH, h * w:(h + 1) * w] = g[b]
        for h in range(4):
            ds_ref[h] = grads[ni][h]

    rev = lambda n: N - 1 - n
    per_chunk = lambda r, c: pl.BlockSpec((4, CPS, r, c), lambda n: (0, rev(n), 0, 0))
    specs = [pl.BlockSpec((CPS * CH, 4 * w), functools.partial(lambda n, cb: (rev(n), cb), cb=cb // 4)) for (_, w, cb) in ins]
    res = pl.pallas_call(
        body, name=name, grid=(N,),
        in_specs=specs + [per_chunk(128, dv), pl.BlockSpec((CPS * CH, 4 * dv), lambda n: (rev(n), 0))]
        + [per_chunk(*a.shape[2:]) for a in kept] + s_specs,
        out_specs=[pl.BlockSpec((CPS * CH, 4 * w), lambda n: (rev(n), 0)) for w in ws] + [pl.BlockSpec(memory_space=pl.ANY)] * nso,
        out_shape=[jax.ShapeDtypeStruct((M, 4 * w), F32) for w in ws] + s_shapes,
        scratch_shapes=[pltpu.VMEM((4, 128, dv), F32)] + s_sems,
        compiler_params=_cparams(("arbitrary",)),
    )(*[a for (a, _, _) in ins], s_all, do, *kept, *s_ins)
    return res[:ni], res[ni:]


def _fn_gate_out(hd, row0, o, z, w):
    outs = []
    for h in range(4):
        outs.append(_rms(o[:, h * hd:(h + 1) * hd], w) * _silu(z[:, h * hd:(h + 1) * hd]))
    return (jnp.concatenate(outs, axis=1),)


def _fn_gla_prep(row0, gk, wgu, bg):
    x = _mm(gk, wgu) + bg
    ls = jnp.minimum(x, 0.0) - jnp.log(1.0 + jnp.exp(-jnp.abs(x)))
    return (jnp.where(_row_ids(row0, gk.shape[0]) >= ZROWS, ls / 16.0, 0.0),)


def loss_call(y, tgt):
    M = y.shape[0]
    tm = _pick(M, 512)

    def body(y_ref, t_ref, l_ref, dy_ref):
        i = pl.program_id(0)
        e = y_ref[...] - t_ref[...]
        dy_ref[...] = e * (1.0 / D)
        part = 0.5 * jnp.sum(jnp.sum(e * e, axis=1, keepdims=True) * (1.0 / D), axis=0, keepdims=True)
        part = jnp.broadcast_to(part, (8, LANE))

        @pl.when(i == 0)
        def _():
            l_ref[...] = part

        @pl.when(i > 0)
        def _():
            l_ref[...] += part

    return pl.pallas_call(
        body, name="loss", grid=(M // tm,),
        in_specs=[pl.BlockSpec((tm, D), lambda i: (i, 0))] * 2,
        out_specs=[pl.BlockSpec((8, LANE), lambda i: (0, 0)), pl.BlockSpec((tm, D), lambda i: (i, 0))],
        out_shape=[jax.ShapeDtypeStruct((8, LANE), F32), jax.ShapeDtypeStruct((M, D), F32)],
        compiler_params=_cparams(("arbitrary",)),
    )(y, tgt)


def _bf(x):
    return x.astype(BF16)


def core_step(x, tgt, W, comm=None):
    S = x.shape[0]
    M = S + PADR
    ids = jnp.asarray(_swa_bucket_ids())
    h0 = jnp.concatenate([jnp.zeros((ZROWS, D), F32), W["meta"], x], axis=0)
    nw = W["norm"]
    nrow = lambda l, k: nw[l, k][None, :]
    layers = list(W["layers"])
    ffw = lambda l: (layers[l]["ffn_g"], layers[l]["ffn_u"], layers[l]["ffn_d"])
    sinks = jnp.pad(W["sinks"], ((0, 0), (0, LANE - 8)))
    dnp = jnp.pad(jnp.concatenate([W["a_log"], W["dt_bias"]], axis=1), ((0, 0), (0, LANE - 8)))
    wgu = jnp.pad(W["gate_up"], ((0, LANE - 16), (0, 0)))
    bg = W["b_gate"]
    full = lambda a: (a, a.shape[1], 0)

    saved = []
    h = h0
    (hn,) = rowwise_fwd("prenorm_0", _fn_prenorm, [full(h)], [nrow(0, 0)], [(D, BF16)])
    bias = swa_bias_fwd(W["rel"], ids)
    for l in range(2):
        st = {"h_a": h, "hn_a": hn}
        f1, a1, b1 = ffn_fwd(f"ffn_fwd_{l}0", hn, *ffw(l),0)
        h, hn = rowwise_fwd(f"resnorm_{l}1", functools.partial(_fn_resnorm, 0.5), [full(h), full(f1)],
                            [nrow(l, 1), nrow(l, 2)], [(D, F32), (D, BF16)])
        st.update(f1=f1, a1=a1, b1=b1, h_b=h, hn_b=hn)
        if l == 0:
            proj = mm_nt(hn, layers[0]["w_in"], "e_proj")
            o_a = swa_fwd(proj, bias, sinks)
            y = conv_fwd(proj, W["conv"])
            qn, kn, gb, bb = rowwise_fwd(
                "dn_prep", _fn_dn_prep, [(y, 512, 0), (y, 512, 1), (proj, LANE, E_BA // LANE)], [dnp], [(512, F32)] * 4)
            ins = [(qn, 128, 0), (kn, 128, 0), (y, 128, 8), (gb, 128, 0), (bb, 128, 0)]
            o_dn, s_all, (t_inv,), got = chunk_fwd("dn_fwd", _dn_block, ins, 128, keep=[(CH, CH)],
                                                   side=comm.gather_side() if comm else None)
            if comm:
                layers[1] = comm.gather_done(got)
            (o_b,) = rowwise_fwd("dn_out", functools.partial(_fn_gate_out, 128),
                                 [full(o_dn), (proj, 512, E_ZB // 512)], [W["dn_norm"]], [(512, BF16)])
            omix = jnp.concatenate([o_a, o_b], axis=1)
            mix = mm_nn(omix, layers[0]["w_out"], "e_mix")
            st.update(proj=proj, y=y, qn=qn, kn=kn, gb=gb, bb=bb, o_dn=o_dn, s_all=s_all, t_inv=t_inv, omix=omix)
        else:
            proj = mm_nt(hn, layers[1]["w_in"], "o_proj")
            (glog,) = rowwise_fwd("gla_prep", _fn_gla_prep, [(proj, LANE, O_GK // LANE)], [wgu, bg], [(512, F32)])
            ins = [(proj, 128, O_Q // 128), (proj, 128, O_K // 128), (proj, 256, O_V // 256), (glog, 128, 0)]
            o_g, s_all, _, _ = chunk_fwd("gla_fwd", _gla_block, ins, 256)
            (omix,) = rowwise_fwd("gla_out", functools.partial(_fn_gate_out, 256),
                                  [full(o_g), (proj, 1024, O_G // 1024)], [W["gla_norm"]], [(1024, BF16)])
            mix = mm_nn(omix, layers[1]["w_out"], "o_mix")
            st.update(proj=proj, glog=glog, o_g=o_g, s_all=s_all, omix=omix)
        h, hn = rowwise_fwd(f"resnorm_{l}3", functools.partial(_fn_resnorm, 1.0), [full(h), full(mix)],
                            [nrow(l, 3), nrow(l, 4)], [(D, F32), (D, BF16)])
        st.update(mix=mix, h_c=h, hn_c=hn)
        f2, a2, b2 = ffn_fwd(f"ffn_fwd_{l}1", hn, *ffw(l),1)
        st.update(f2=f2, a2=a2, b2=b2)
        if l == 0:
            h, hn = rowwise_fwd("resnorm_05", functools.partial(_fn_resnorm, 0.5), [full(h), full(f2)],
                                [nrow(0, 5), nrow(1, 0)], [(D, F32), (D, BF16)])
        else:
            (h,) = rowwise_fwd("res_last", functools.partial(_fn_res_last, 0.5), [full(h), full(f2)],
                               [nrow(1, 5)], [(D, F32)])
        saved.append(st)

    loss_blk, dy = loss_call(h[PADR:], tgt)
    dh = jnp.concatenate([jnp.zeros((PADR, D), F32), dy], axis=0)

    G = {}
    dnorm = [[None] * 6 for _ in range(2)]
    dWg = [[None, None], [None, None]]
    dWu = [[None, None], [None, None]]
    dWd = [[None, None], [None, None]]
    dhn = None
    for l in (1, 0):
        st = saved[l]
        if l == 1:
            (dh_, df), (dw5,) = rowwise_bwd(
                "res_last_b", functools.partial(_fn_res_last, 0.5), [full(st["h_c"]), full(st["f2"])], [nrow(1, 5)],
                [full(dh)], [F32, BF16])
            dnorm[1][5] = dw5
        else:
            (dh_, df), (dw5, dw0n) = rowwise_bwd(
                "resnorm_05_b", functools.partial(_fn_resnorm, 0.5), [full(st["h_c"]), full(st["f2"])],
                [nrow(0, 5), nrow(1, 0)], [full(dh), full(dhn)], [F32, BF16])
            dnorm[0][5] = dw5
            dnorm[1][0] = dw0n
        dh = dh_
        dxn, da, db, hm = ffn_bwd_x(f"ffn_bx_{l}1", df, st["a2"], st["b2"], *ffw(l),1)
        dWg[l][1], dWu[l][1], dWd[l][1] = ffn_bwd_w(f"ffn_bw_{l}1", st["hn_c"], df, da, db, hm)
        (dh_, dmix), (dw3, dw4) = rowwise_bwd(
            f"resnorm_{l}3_b", functools.partial(_fn_resnorm, 1.0), [full(st["h_b"]), full(st["mix"])],
            [nrow(l, 3), nrow(l, 4)], [full(dh), full(dxn)], [F32, BF16])
        dnorm[l][3], dnorm[l][4] = dw3, dw4
        dh = dh_
        proj = st["proj"]
        if l == 1:
            G["o_out"] = mm_tn(st["omix"], dmix, "o_out_dw")
            domix = mm_nt(dmix, layers[1]["w_out"], "o_mix_dx")
            (do_g, dgate), (dgn,) = rowwise_bwd(
                "gla_out_b", functools.partial(_fn_gate_out, 256), [full(st["o_g"]), (proj, 1024, O_G // 1024)],
                [W["gla_norm"]], [full(domix)], [F32, F32])
            G["gla_norm"] = dgn
            ins = [(proj, 128, O_Q // 128), (proj, 128, O_K // 128), (proj, 256, O_V // 256), (st["glog"], 128, 0)]
            (dq, dk, dv, dglog), _ = chunk_bwd("gla_bwd", _gla_block, ins, 256, st["s_all"], do_g)
            (dgk,), (dwgu, dbg) = rowwise_bwd("gla_prep_b", _fn_gla_prep, [(proj, LANE, O_GK // LANE)], [wgu, bg],
                                              [full(dglog)], [F32])
            G["gate_up"] = dwgu[:16]
            G["b_gate"] = dbg
            dproj = _bf(jnp.concatenate([dq, dk, dv, dgate, dgk, jnp.zeros((M, O_END - O_GK - LANE), F32)], axis=1))
            G["o_in"] = mm_tn(dproj, st["hn_b"], "o_in_dw")
            dhn_b = mm_nn(dproj, layers[1]["w_in"], "o_proj_dx")
        else:
            G["e_out"] = mm_tn(st["omix"], dmix, "e_out_dw")
            domix = mm_nt(dmix, layers[0]["w_out"], "e_mix_dx")
            (do_dn, dz), (ddn,) = rowwise_bwd(
                "dn_out_b", functools.partial(_fn_gate_out, 128), [full(st["o_dn"]), (proj, 512, E_ZB // 512)],
                [W["dn_norm"]], [(domix, 512, 2)], [F32, F32])
            G["dn_norm"] = ddn
            ins = [(st["qn"], 128, 0), (st["kn"], 128, 0), (st["y"], 128, 8), (st["gb"], 128, 0), (st["bb"], 128, 0)]
            side = comm.scatter_side(layer_grad_items(1, dWg, dWu, dWd, G["o_in"], G["o_out"])) if comm else None
            (dqn, dkn, dvv, dgb, dbb), got = chunk_bwd("dn_bwd", _dn_block, ins, 128, st["s_all"], do_dn, side=side,
                                                        kept=[st["t_inv"]])
            if comm:
                comm.scatter_done(got)
            (dyq, dyk, dba), (ddnp,) = rowwise_bwd(
                "dn_prep_b", _fn_dn_prep, [(st["y"], 512, 0), (st["y"], 512, 1), (proj, LANE, E_BA // LANE)], [dnp],
                [full(dqn), full(dkn), full(dgb), full(dbb)], [F32, F32, F32])
            G["a_log"] = ddnp[:, 0:4]
            G["dt_bias"] = ddnp[:, 4:8]
            dyc = jnp.concatenate([dyq, dyk, dvv], axis=1)
            dxc, dconv = conv_bwd(proj, W["conv"], dyc)
            G["conv"] = dconv
            dq_a, dk_a, dv_a, dbias, dsink = swa_bwd(proj, bias, sinks, domix)
            G["sinks"] = dsink[:, :8]
            G["rel"] = swa_bias_bwd(dbias, ids)[:, :8]
            dproj = _bf(jnp.concatenate([dq_a, dk_a, dv_a, dxc, dz, dba, jnp.zeros((M, E_END - E_BA - LANE), F32)], axis=1))
            G["e_in"] = mm_tn(dproj, st["hn_b"], "e_in_dw")
            dhn_b = mm_nn(dproj, layers[0]["w_in"], "e_proj_dx")
        (dh_, df), (dw1, dw2) = rowwise_bwd(
            f"resnorm_{l}1_b", functools.partial(_fn_resnorm, 0.5), [full(st["h_a"]), full(st["f1"])],
            [nrow(l, 1), nrow(l, 2)], [full(dh), full(dhn_b)], [F32, BF16])
        dnorm[l][1], dnorm[l][2] = dw1, dw2
        dh = dh_
        dxn, da, db, hm = ffn_bwd_x(f"ffn_bx_{l}0", df, st["a1"], st["b1"], *ffw(l),0)
        dWg[l][0], dWu[l][0], dWd[l][0] = ffn_bwd_w(f"ffn_bw_{l}0", st["hn_a"], df, da, db, hm)
        dhn = dxn
    (dh0p,), (dw00,) = rowwise_bwd("prenorm_0_b", _fn_prenorm, [full(saved[0]["h_a"])], [nrow(0, 0)], [full(dhn)], [F32])
    dnorm[0][0] = dw00
    dh = dh + dh0p
    G["meta"] = dh[ZROWS:PADR]
    G["norm"] = jnp.stack([jnp.concatenate(r, axis=0) for r in dnorm], axis=0)
    G["items"] = [layer_grad_items(0, dWg, dWu, dWd, G["e_in"], G["e_out"]),
                  None if comm else layer_grad_items(1, dWg, dWu, dWd, G["o_in"], G["o_out"])]
    return loss_blk, dh[PADR:], G


NAMES = [("meta", "meta_tokens"), ("norm", "norm_w"), ("ffn_g", "ffn_w_gate"), ("ffn_u", "ffn_w_up"),
         ("ffn_d", "ffn_w_down"), ("rel", "rel_bias_table"), ("e_in", "even_w_in"), ("conv", "even_conv_w"),
         ("sinks", "swa_sinks"), ("a_log", "dn_a_log"), ("dt_bias", "dn_dt_bias"), ("dn_norm", "dn_norm_w"),
         ("e_out", "even_w_out"), ("o_in", "odd_w_in"), ("gate_up", "gla_w_gate_up"), ("b_gate", "gla_b_gate"),
         ("gla_norm", "gla_norm_w"), ("o_out", "odd_w_out")]
BIG = ["ffn_g", "ffn_u", "ffn_d", "e_in", "e_out", "o_in", "o_out"]
IN_ROWS = 800
SMALL = [("meta", (16, 256)), ("norm", (2, 6, 256)), ("conv", (1, 4, 384)), ("gate_up", (1, 16, 128)),
         ("b_gate", (1, 128)), ("gla_norm", (1, 64))]
REPL = [("rel", (32, 8)), ("sinks", (1, 8)), ("a_log", (1, 4)), ("dt_bias", (1, 4)), ("dn_norm", (1, 128))]
SMALL_REP = 88 * LANE
SMALL_ROWS = 96


def pack_small(t):
    a = jnp.concatenate([t[n].reshape(-1) for n, _ in SMALL])
    b = jnp.concatenate([t[n].reshape(-1) for n, _ in REPL])
    flat = jnp.concatenate([a, jnp.zeros((SMALL_REP - a.shape[0],), F32), b,
                            jnp.zeros((SMALL_ROWS * LANE - SMALL_REP - b.shape[0],), F32)])
    return flat.reshape(SMALL_ROWS, LANE)


def unpack_small(p):
    flat = p.reshape(-1)
    out, r = {}, 0
    for n, shp in SMALL:
        k = int(np.prod(shp))
        out[n] = flat[r:r + k].reshape(shp)
        r += k
    r = SMALL_REP
    for n, shp in REPL:
        k = int(np.prod(shp))
        out[n] = flat[r:r + k].reshape(shp)
        r += k
    return out


IN_SRC = (706, 772)


def weight_pieces(wt, l):
    inn = wt["e_in" if l == 0 else "o_in"]
    inn = jnp.pad(inn, ((0, IN_ROWS - inn.shape[0]), (0, 0))).reshape(2, IN_ROWS // 2, D)
    out = wt["e_out" if l == 0 else "o_out"].reshape(2, 128, D)
    return [wt["ffn_g"][l], wt["ffn_u"][l], wt["ffn_d"][l], inn, out]


def layer_weights(l, q):
    m = _even_in_map() if l == 0 else _odd_in_map()
    src = np.where(m >= 0, (m // IN_SRC[l]) * IN_ROWS + m % IN_SRC[l], -1)
    w_out = q[4].reshape(NSH * 256, D)
    return {"ffn_g": q[0], "ffn_u": q[1], "ffn_d": q[2], "w_in": _take_pad(q[3].reshape(NSH * IN_ROWS, D), src, 0),
            "w_out": _take_pad(w_out, _even_out_map(), 0) if l == 0 else w_out}


def layer_grad_items(l, dwg, dwu, dwd, g_in, g_out):
    m = _even_in_map() if l == 0 else _odd_in_map()
    gi = jnp.take(g_in, jnp.asarray(_inverse(m, NSH * IN_SRC[l])), axis=0).reshape(NSH, IN_SRC[l], D)
    gi = _bf(jnp.pad(gi, ((0, 0), (0, IN_ROWS - IN_SRC[l]), (0, 0)))).reshape(NSH, 2, IN_ROWS // 2, D)
    if l == 0:
        g_out = jnp.take(g_out, jnp.asarray(_inverse(_even_out_map(), 1024)), axis=0)
    go = _bf(g_out).reshape(NSH, 2, 128, D)
    return [[dwg[l][j], dwu[l][j], dwd[l][j], gi[:, j], go[:, j]] for j in range(2)]


def assemble_layer(l, r0, r1):
    return {"ffn_g": jnp.stack([r0[0], r1[0]]), "ffn_u": jnp.stack([r0[1], r1[1]]), "ffn_d": jnp.stack([r0[2], r1[2]]),
            "in": jnp.concatenate([r0[3], r1[3]])[:IN_SRC[l]], "out": jnp.concatenate([r0[4], r1[4]])}


def big_grads(l0, l1):
    st = lambda n: jnp.stack([l0[n], l1[n]])
    return {"ffn_g": st("ffn_g"), "ffn_u": st("ffn_u"), "ffn_d": st("ffn_d"), "e_in": l0["in"], "e_out": l0["out"],
            "o_in": l1["in"], "o_out": l1["out"]}


def small_from_gathered(gs):
    sm = [unpack_small(gs[s]) for s in range(NSH)]
    full = {}
    full["meta"] = jnp.concatenate([sm[s]["meta"] for s in range(NSH)], axis=1)
    full["norm"] = jnp.concatenate([sm[s]["norm"] for s in range(NSH)], axis=2)
    full["conv"] = jnp.concatenate([sm[s]["conv"][0] for s in range(NSH)], axis=1)
    full["gate_up"] = jnp.concatenate([sm[s]["gate_up"][0] for s in range(NSH)], axis=1)
    full["b_gate"] = jnp.concatenate([sm[s]["b_gate"] for s in range(NSH)], axis=1)
    full["gla_norm"] = jnp.concatenate([sm[s]["gla_norm"] for s in range(NSH)], axis=1)
    return full


def _col_sh(w):
    return jnp.moveaxis(w.reshape(w.shape[0], NSH, w.shape[1] // NSH), 1, 0)


def _to_t(n, a):
    if n in ("ffn_g", "ffn_u"):
        return jnp.swapaxes(a, 2, 3)
    if n in ("e_in", "o_in"):
        return jnp.swapaxes(a[0], 0, 1)
    return a if n == "ffn_d" else a[0]


def _from_t(n, a):
    if n in ("ffn_g", "ffn_u"):
        return jnp.swapaxes(a, 2, 3)
    if n in ("e_in", "o_in"):
        return jnp.swapaxes(a, 0, 1)[None]
    return a if n == "ffn_d" else a[None]


def pack_small_grads(G):
    col_sh = _col_sh
    norm_sh = jnp.moveaxis(G["norm"].reshape(2, 6, NSH, 256), 2, 0)
    a = jnp.concatenate([col_sh(G["meta"]).reshape(NSH, -1), norm_sh.reshape(NSH, -1), col_sh(G["conv"]).reshape(NSH, -1),
                         col_sh(G["gate_up"]).reshape(NSH, -1), G["b_gate"].reshape(NSH, -1),
                         G["gla_norm"].reshape(NSH, -1)], axis=1)
    b = jnp.concatenate([G[n].reshape(-1) for n, _ in REPL])
    b = jnp.broadcast_to(b[None], (NSH, b.shape[0]))
    small = jnp.concatenate([a, jnp.zeros((NSH, SMALL_REP - a.shape[1]), F32), b,
                             jnp.zeros((NSH, SMALL_ROWS * LANE - SMALL_REP - b.shape[1]), F32)], axis=1)
    return small.reshape(NSH, SMALL_ROWS, LANE)


MESH = pl.DeviceIdType.MESH
ANY = pl.BlockSpec(memory_space=pl.ANY)
VMEM = pl.BlockSpec(memory_space=pltpu.VMEM)


def _place():
    return lax.axis_index("x"), lax.axis_index("y"), lax.axis_index("c")


def _other_chips(x, y):
    return [(1 - x, y), (x, 1 - y), (1 - x, 1 - y)]


def _rcopy(send_sems, recv_sems, k, src, dst, to):
    return pltpu.make_async_remote_copy(src_ref=src, dst_ref=dst, send_sem=send_sems.at[k], recv_sem=recv_sems.at[k],
                                        device_id=to, device_id_type=MESH)


def _gather_steps(in_refs, out_refs, send_sems, recv_sems):
    n = len(in_refs)
    x, y, c = _place()
    s = 2 * x + y
    chips = _other_chips(x, y)
    copy = functools.partial(_rcopy, send_sems, recv_sems)
    pairs = [(i, j, cx, cy) for i in range(n) for j, (cx, cy) in enumerate(chips)]
    pushes = lambda: [copy(i * 3 + j, in_refs[i].at[c], out_refs[i].at[s, c], (cx, cy, c)) for i, j, cx, cy in pairs]
    landed = lambda i, cx, cy, half: out_refs[i].at[2 * cx + cy, half]
    relays = lambda: [copy(3 * n + i * 3 + j, landed(i, cx, cy, c), landed(i, cx, cy, c), (x, y, 1 - c)) for i, j, cx, cy in pairs]

    def start():
        for cp in pushes():
            cp.start()

    def relay():
        for i, j, cx, cy in pairs:
            copy(i * 3 + j, landed(i, cx, cy, c), landed(i, cx, cy, c), (x, y, c)).wait_recv()
        for cp in relays():
            cp.start()

    def finish():
        for i, j, cx, cy in pairs:
            copy(3 * n + i * 3 + j, landed(i, cx, cy, 1 - c), landed(i, cx, cy, 1 - c), (x, y, c)).wait_recv()
        for cp in pushes() + relays():
            cp.wait_send()

    return start, relay, finish


def _gather_shapes(pieces):
    return [jax.ShapeDtypeStruct((NSH,) + a.shape, a.dtype) for a in pieces]


def ag_layer(name, pieces):
    n = len(pieces)

    def body(*refs):
        for fn in _gather_steps(refs[:n], refs[n:2 * n], *refs[2 * n:]):
            fn()

    return pl.pallas_call(
        body, name=name, in_specs=[ANY] * n, out_specs=[ANY] * n, out_shape=_gather_shapes(pieces),
        scratch_shapes=[pltpu.SemaphoreType.DMA((6 * n,)), pltpu.SemaphoreType.DMA((6 * n,))],
    )(*pieces)


def gather_side(pieces):
    def events(n_steps, in_refs, out_refs, send_sems, recv_sems):
        start, relay, finish = _gather_steps(in_refs, out_refs, send_sems, recv_sems)
        return [(0, start), (max(3 * n_steps // 4, 1), relay), (n_steps - 1, finish)]

    return Side(pieces, _gather_shapes(pieces), 6 * len(pieces), events)


def ag_small(pack):
    def body(x_ref, out_ref, send_sems, recv_sems):
        x, y, c = _place()
        s = 2 * x + y
        chips = _other_chips(x, y)

        def copy(k, src, dst, to):
            return pltpu.make_async_remote_copy(src_ref=src, dst_ref=dst, send_sem=send_sems.at[k], recv_sem=recv_sems.at[k],
                                                device_id=to, device_id_type=MESH)

        out_ref[s] = x_ref[...]
        sends = [copy(j, x_ref, out_ref.at[s], (cx, cy, c)) for j, (cx, cy) in enumerate(chips)]
        for cp in sends:
            cp.start()
        for j, (cx, cy) in enumerate(chips):
            blk = out_ref.at[2 * cx + cy]
            copy(j, blk, blk, (x, y, c)).wait_recv()
        for cp in sends:
            cp.wait_send()

    return pl.pallas_call(
        body, name="ag_small", in_specs=[VMEM], out_specs=VMEM,
        out_shape=jax.ShapeDtypeStruct((NSH,) + pack.shape, pack.dtype),
        scratch_shapes=[pltpu.SemaphoreType.DMA((3,)), pltpu.SemaphoreType.DMA((3,))],
    )(pack)


def rs_pair(name, items):
    ni = len(items[0])

    def body(*refs):
        in_refs = [refs[:ni], refs[ni:2 * ni]]
        recv_refs = refs[2 * ni:3 * ni]
        send_sems, recv_sems = refs[3 * ni:]
        x, y, c = _place()
        copy = functools.partial(_rcopy, send_sems, recv_sems)
        for cc in range(2):
            @pl.when(c == cc)
            def _():
                cps = [copy(i * NSH + s, in_refs[1 - cc][i].at[s], recv_refs[i].at[s], (x, y, 1 - c))
                       for i in range(ni) for s in range(NSH)]
                for cp in cps:
                    cp.start()
                for cp in cps:
                    cp.wait()

    return pl.pallas_call(
        body, name=name, in_specs=[ANY] * (2 * ni), out_specs=[ANY] * ni,
        out_shape=[jax.ShapeDtypeStruct(a.shape, a.dtype) for a in items[0]],
        scratch_shapes=[pltpu.SemaphoreType.DMA((ni * NSH,)), pltpu.SemaphoreType.DMA((ni * NSH,))],
    )(*items[0], *items[1])


def _scatter_steps(a_refs, out_refs, send_sems, recv_sems):
    n = len(a_refs)
    x, y, c = _place()
    s = 2 * x + y
    chips = _other_chips(x, y)
    copy = functools.partial(_rcopy, send_sems, recv_sems)
    pairs = [(i, j, cx, cy) for i in range(n) for j, (cx, cy) in enumerate(chips)]
    sends = lambda: [copy(i * 3 + j, a_refs[i].at[2 * cx + cy], out_refs[i].at[s], (cx, cy, c)) for i, j, cx, cy in pairs]

    def start():
        for cp in sends():
            cp.start()

    def finish():
        for i, j, cx, cy in pairs:
            blk = out_refs[i].at[2 * cx + cy]
            copy(i * 3 + j, blk, blk, (x, y, c)).wait_recv()
        for cp in sends():
            cp.wait_send()

    return start, finish


def rs_chips(name, arrs):
    n = len(arrs)

    def body(*refs):
        for fn in _scatter_steps(refs[:n], refs[n:2 * n], *refs[2 * n:]):
            fn()

    return pl.pallas_call(
        body, name=name, in_specs=[ANY] * n, out_specs=[ANY] * n,
        out_shape=[jax.ShapeDtypeStruct(a.shape, a.dtype) for a in arrs],
        scratch_shapes=[pltpu.SemaphoreType.DMA((3 * n,)), pltpu.SemaphoreType.DMA((3 * n,))],
    )(*arrs)


def scatter_side(arrs):
    def events(n_steps, in_refs, out_refs, send_sems, recv_sems):
        start, finish = _scatter_steps(in_refs, out_refs, send_sems, recv_sems)
        return [(0, start), (n_steps - 1, finish)]

    return Side(arrs, [jax.ShapeDtypeStruct(a.shape, a.dtype) for a in arrs], 3 * len(arrs), events)


def _pair_chunks(rows):
    return 4 if rows % 32 == 0 else (2 if rows % 16 == 0 else 1)


def ag_pair(name, arrs):
    n = len(arrs)
    chunks = [(i, k * (a.shape[0] // _pair_chunks(a.shape[0])), a.shape[0] // _pair_chunks(a.shape[0]))
              for i, a in enumerate(arrs) for k in range(_pair_chunks(a.shape[0]))]

    def body(*refs):
        g_refs, out_refs = refs[:n], refs[n:2 * n]
        send_sems, recv_sems = refs[2 * n:]
        x, y, c = _place()
        give = [_rcopy(send_sems, recv_sems, q, g_refs[i].at[pl.ds(r0, rc)], out_refs[i].at[pl.ds(r0, rc)], (x, y, 1 - c))
                for q, (i, r0, rc) in enumerate(chunks)]
        for cp in give:
            cp.start()
        for cp in give:
            cp.wait()

    return pl.pallas_call(
        body, name=name, in_specs=[ANY] * n, out_specs=[ANY] * n,
        out_shape=[jax.ShapeDtypeStruct(a.shape, a.dtype) for a in arrs],
        scratch_shapes=[pltpu.SemaphoreType.DMA((len(chunks),)), pltpu.SemaphoreType.DMA((len(chunks),))],
    )(*arrs)


def small_allreduce(p):
    def body(p_ref, out_ref, rbuf, send_sems, recv_sems):
        x, y, c = _place()
        me = 4 * x + 2 * y + c
        rbuf[me] = p_ref[2 * x + y]
        flip = lambda v, f: (1 - v) if f else v
        peers = [(flip(x, k >> 2 & 1), flip(y, k >> 1 & 1), flip(c, k & 1)) for k in range(1, 8)]

        def copy(k, src, dst, to):
            return pltpu.make_async_remote_copy(src_ref=src, dst_ref=dst, send_sem=send_sems.at[k], recv_sem=recv_sems.at[k],
                                                device_id=to, device_id_type=MESH)

        sends = [copy(k, p_ref.at[2 * px + py], rbuf.at[me], (px, py, pc)) for k, (px, py, pc) in enumerate(peers)]
        for cp in sends:
            cp.start()
        for k, (px, py, pc) in enumerate(peers):
            blk = rbuf.at[4 * px + 2 * py + pc]
            copy(k, blk, blk, (x, y, c)).wait_recv()
        for cp in sends:
            cp.wait_send()
        acc = rbuf[0]
        for d in range(1, 8):
            acc = acc + rbuf[d]
        out_ref[...] = acc

    return pl.pallas_call(
        body, name="small_allreduce", in_specs=[VMEM], out_specs=VMEM,
        out_shape=jax.ShapeDtypeStruct(p.shape[1:], F32),
        scratch_shapes=[pltpu.VMEM((8,) + p.shape[1:], F32), pltpu.SemaphoreType.DMA((7,)), pltpu.SemaphoreType.DMA((7,))],
    )(p)


def _rows_tile(rows, cap):
    return _pick(rows, cap) if rows % 128 == 0 else rows


def sum_pair(name, a0, a1, recv, cflag):
    n, r, d = recv.shape
    tr = _pick(r, 1024) if r % 64 == 0 else r

    def body(c_ref, a0_ref, a1_ref, b_ref, o_ref):
        own = jnp.where(c_ref[0] == 0, a0_ref[...].astype(F32), a1_ref[...].astype(F32))
        o_ref[...] = (own + b_ref[...].astype(F32)).astype(o_ref.dtype)

    spec = pl.BlockSpec((None, tr, d), lambda s, i: (s, i, 0))
    return pl.pallas_call(
        body, name=name, grid=(n, r // tr), in_specs=[pl.BlockSpec(memory_space=pltpu.SMEM), spec, spec, spec],
        out_specs=spec, out_shape=jax.ShapeDtypeStruct(recv.shape, BF16), compiler_params=_cparams(("parallel", "parallel")),
    )(cflag, a0, a1, recv)


def sum_chips(name, parts, own, sflag):
    n, r, d = parts.shape
    tr = _pick(r, 1024) if r % 64 == 0 else r

    def body(s_ref, p_ref, a_ref, o_ref):
        acc = None
        for t in range(n):
            term = jnp.where(s_ref[0] == t, a_ref[t].astype(F32), p_ref[t].astype(F32))
            acc = term if acc is None else acc + term
        o_ref[...] = acc

    spec = pl.BlockSpec((n, tr, d), lambda i: (0, i, 0))
    return pl.pallas_call(
        body, name=name, grid=(r // tr,), in_specs=[pl.BlockSpec(memory_space=pltpu.SMEM), spec, spec],
        out_specs=pl.BlockSpec((tr, d), lambda i: (i, 0)), out_shape=jax.ShapeDtypeStruct((r, d), F32),
        compiler_params=_cparams(("parallel",)),
    )(sflag, parts, own)


ADAM_LR, ADAM_B1, ADAM_B2, ADAM_EPS, ADAM_WD, ADAM_STEP = 0.001, 0.9, 0.999, 1e-08, 0.01, 10


def adamw_call(name, w, g, m, v):
    rows, cols = w.shape
    tr = _rows_tile(rows, 512)

    def body(w_ref, g_ref, m_ref, v_ref, d_ref, nm_ref, nv_ref):
        g_ = g_ref[...]
        m_ = ADAM_B1 * m_ref[...] + (1.0 - ADAM_B1) * g_
        v_ = ADAM_B2 * v_ref[...] + (1.0 - ADAM_B2) * (g_ * g_)
        m_hat = m_ / (1.0 - ADAM_B1 ** ADAM_STEP)
        v_hat = v_ / (1.0 - ADAM_B2 ** ADAM_STEP)
        d_ref[...] = -ADAM_LR * (m_hat / (jnp.sqrt(v_hat) + ADAM_EPS) + ADAM_WD * w_ref[...])
        nm_ref[...] = m_
        nv_ref[...] = v_

    spec = pl.BlockSpec((tr, cols), lambda i: (i, 0))
    sh = jax.ShapeDtypeStruct((rows, cols), F32)
    return pl.pallas_call(
        body, name=name, grid=(rows // tr,), in_specs=[spec] * 4, out_specs=[spec] * 3, out_shape=[sh] * 3,
        compiler_params=_cparams(("parallel",)),
    )(w, g, m, v)


def kernel(x, meta_tokens, norm_w, ffn_w_gate, ffn_w_up, ffn_w_down, rel_bias_table, even_w_in, even_conv_w, swa_sinks, dn_a_log, dn_dt_bias, dn_norm_w, even_w_out, odd_w_in, gla_w_gate_up, gla_b_gate, gla_norm_w, odd_w_out, loss_target, m_meta_tokens, m_norm_w, m_ffn_w_gate, m_ffn_w_up, m_ffn_w_down, m_rel_bias_table, m_even_w_in, m_even_conv_w, m_swa_sinks, m_dn_a_log, m_dn_dt_bias, m_dn_norm_w, m_even_w_out, m_odd_w_in, m_gla_w_gate_up, m_gla_b_gate, m_gla_norm_w, m_odd_w_out, v_meta_tokens, v_norm_w, v_ffn_w_gate, v_ffn_w_up, v_ffn_w_down, v_rel_bias_table, v_even_w_in, v_even_conv_w, v_swa_sinks, v_dn_a_log, v_dn_dt_bias, v_dn_norm_w, v_even_w_out, v_odd_w_in, v_gla_w_gate_up, v_gla_b_gate, v_gla_norm_w, v_odd_w_out):
    ws = [meta_tokens, norm_w, ffn_w_gate, ffn_w_up, ffn_w_down, rel_bias_table, even_w_in, even_conv_w, swa_sinks, dn_a_log,
          dn_dt_bias, dn_norm_w, even_w_out, odd_w_in, gla_w_gate_up, gla_b_gate, gla_norm_w, odd_w_out]
    ms = [m_meta_tokens, m_norm_w, m_ffn_w_gate, m_ffn_w_up, m_ffn_w_down, m_rel_bias_table, m_even_w_in, m_even_conv_w,
          m_swa_sinks, m_dn_a_log, m_dn_dt_bias, m_dn_norm_w, m_even_w_out, m_odd_w_in, m_gla_w_gate_up, m_gla_b_gate,
          m_gla_norm_w, m_odd_w_out]
    vs = [v_meta_tokens, v_norm_w, v_ffn_w_gate, v_ffn_w_up, v_ffn_w_down, v_rel_bias_table, v_even_w_in, v_even_conv_w,
          v_swa_sinks, v_dn_a_log, v_dn_dt_bias, v_dn_norm_w, v_even_w_out, v_odd_w_in, v_gla_w_gate_up, v_gla_b_gate,
          v_gla_norm_w, v_odd_w_out]
    short = [n for n, _ in NAMES]
    w = dict(zip(short, ws))
    m = dict(zip(short, ms))
    v = dict(zip(short, vs))

    wt = {n: _to_t(n, w[n]) for n in BIG}
    own = {n: wt[n].astype(BF16) for n in BIG}
    sflag = (2 * lax.axis_index("x") + lax.axis_index("y")).astype(jnp.int32).reshape(1)
    cflag = lax.axis_index("c").astype(jnp.int32).reshape(1)
    is0 = cflag[0] == 0
    fill = lambda got, pieces: [lax.dynamic_update_index_in_dim(g_, p_, sflag[0], 0) for g_, p_ in zip(got, pieces)]
    pieces = [weight_pieces(own, l) for l in range(2)]
    small = small_from_gathered(ag_small(pack_small(w)))
    W = {**small, **{n: w[n] for n, _ in REPL},
         "layers": [layer_weights(0, fill(ag_layer("ag_layer_0", pieces[0]), pieces[0])), None]}

    def reduce_start(l, items):
        recv = rs_pair(f"rs_pair_{l}", items)
        return [sum_pair(f"sum_pair_{l}{i}", items[0][i], items[1][i], recv[i], cflag) for i in range(len(recv))]

    def reduce_finish(l, mine, parts):
        red = [sum_chips(f"sum_chips_{l}{i}", p, a, sflag) for i, (p, a) in enumerate(zip(parts, mine))]
        got = ag_pair(f"ag_pair_{l}", red)
        return [jnp.where(is0, r_, g_) for r_, g_ in zip(red, got)], [jnp.where(is0, g_, r_) for r_, g_ in zip(red, got)]

    class Layer1Exchange:
        def gather_side(self):
            return gather_side(pieces[1])

        def gather_done(self, got):
            return layer_weights(1, fill(got, pieces[1]))

        def scatter_side(self, items):
            self.mine = reduce_start(1, items)
            return scatter_side(self.mine)

        def scatter_done(self, got):
            self.parts = got

    ex = Layer1Exchange()
    loss_blk, gx, G = core_step(x[0], loss_target[0], W, comm=ex)

    mine0 = reduce_start(0, G["items"][0])
    lay0 = assemble_layer(0, *reduce_finish(0, mine0, rs_chips("rs_chips_0", mine0)))
    lay1 = assemble_layer(1, *reduce_finish(1, ex.mine, ex.parts))
    gt = big_grads(lay0, lay1)
    g_small_pack = small_allreduce(pack_small_grads(G))
    g = {**{n: _from_t(n, gt[n]) for n in BIG}, **unpack_small(g_small_pack)}

    delta, new_m, new_v = {}, {}, {}
    for n in BIG:
        shp = wt[n].shape
        two = lambda t: t.reshape(-1, D)
        d_, m_, v_ = adamw_call("adamw_" + n, two(wt[n]), two(gt[n]), two(_to_t(n, m[n])), two(_to_t(n, v[n])))
        delta[n], new_m[n], new_v[n] = (_from_t(n, t.reshape(shp)) for t in (d_, m_, v_))
    d_, m_, v_ = adamw_call("adamw_small", pack_small(w), g_small_pack, pack_small(m), pack_small(v))
    delta.update(unpack_small(d_))
    new_m.update(unpack_small(m_))
    new_v.update(unpack_small(v_))

    loss = lax.psum(loss_blk[0, 0], ("x", "y", "c"))
    return (loss, gx[None], *[g[n] for n in short], *[delta[n] for n in short], *[new_m[n] for n in short],
            *[new_v[n] for n in short])
```

```python
import functools
import math

import numpy as np
import jax
import jax.numpy as jnp
from jax import lax
from jax.experimental import pallas as pl
from jax.experimental.pallas import tpu as pltpu

F32 = jnp.float32
BF16 = jnp.bfloat16
HI = lax.Precision.HIGHEST

D = 1024
N_META = 16
PADR = 128
ZROWS = PADR - N_META
D_FF = 2816
NSH = 4
FSH = D_FF // NSH
EPS = 1e-6
NEG = -1e30
CH = 64
CPS = 2
BLK = 128
LANE = 128
VMEM_LIMIT = 56 * 1024 * 1024
FFN_SUB = 4

E_QA, E_KA, E_VA, E_QB, E_KB, E_VB, E_ZB, E_BA, E_END = 0, 1024, 1280, 1536, 2048, 2560, 3072, 3584, 4096


def _even_in_map():
    m = np.full((E_END,), -1, np.int64)
    for h in range(8):
        m[E_QA + h * 128:E_QA + h * 128 + 64] = np.arange(h * 64, (h + 1) * 64)
    for h in range(2):
        m[E_KA + h * 128:E_KA + h * 128 + 64] = 512 + np.arange(h * 64, (h + 1) * 64)
        m[E_VA + h * 128:E_VA + h * 128 + 64] = 640 + np.arange(h * 64, (h + 1) * 64)
    m[E_QB:E_QB + 2048] = 768 + np.arange(2048)
    m[E_BA:E_BA + 8] = 2816 + np.arange(8)
    return m


def _even_out_map():
    m = np.full((1536,), -1, np.int64)
    for h in range(8):
        m[h * 128:h * 128 + 64] = np.arange(h * 64, (h + 1) * 64)
    m[1024:1536] = 512 + np.arange(512)
    return m


O_Q, O_K, O_V, O_G, O_GK, O_END = 0, 512, 1024, 2048, 3072, 3584


def _odd_in_map():
    m = np.full((O_END,), -1, np.int64)
    m[:3072] = np.arange(3072)
    m[O_GK:O_GK + 16] = 3072 + np.arange(16)
    return m


def _inverse(m, n):
    inv = np.zeros((n,), np.int64)
    for p, o in enumerate(m):
        if o >= 0:
            inv[o] = p
    return inv


def _take_pad(w, m, axis):
    t = jnp.take(w, jnp.asarray(np.maximum(m, 0)), axis=axis)
    shape = [1] * w.ndim
    shape[axis] = m.shape[0]
    return jnp.where(jnp.asarray(m >= 0).reshape(shape), t, jnp.zeros((), w.dtype))


def _mm(a, b, prec=HI):
    return lax.dot_general(a, b, (((1,), (0,)), ((), ())), precision=prec, preferred_element_type=F32)


def _mm_nt(a, b, prec=HI):
    return lax.dot_general(a, b, (((1,), (1,)), ((), ())), precision=prec, preferred_element_type=F32)


def _mm_tn(a, b, prec=HI):
    return lax.dot_general(a, b, (((0,), (0,)), ((), ())), precision=prec, preferred_element_type=F32)


def _bdot(a, b, dims):
    return lax.dot_general(a.astype(BF16), b.astype(BF16), (dims, ((), ())), preferred_element_type=F32)


@jax.custom_vjp
def _bmm(a, b):
    return _bdot(a, b, ((1,), (0,)))


@jax.custom_vjp
def _bmm_nt(a, b):
    return _bdot(a, b, ((1,), (1,)))


@jax.custom_vjp
def _bmm_tn(a, b):
    return _bdot(a, b, ((0,), (0,)))


_bmm.defvjp(lambda a, b: (_bmm(a, b), (a, b)), lambda r, g: (_bmm_nt(g, r[1]), _bmm_tn(r[0], g)))
_bmm_nt.defvjp(lambda a, b: (_bmm_nt(a, b), (a, b)), lambda r, g: (_bmm(g, r[1]), _bmm_tn(g, r[0])))
_bmm_tn.defvjp(lambda a, b: (_bmm_tn(a, b), (a, b)), lambda r, g: (_bmm_nt(r[1], g), _bmm(r[0], g)))


def _hi_lo(x):
    h = x.astype(BF16)
    return h, (x - h.astype(F32)).astype(BF16)


def _xdot(a, b, dims):
    ah, al = _hi_lo(a)
    bh, bl = _hi_lo(b)
    d = lambda p, q: lax.dot_general(p, q, (dims, ((), ())), preferred_element_type=F32)
    return d(ah, bh) + (d(ah, bl) + d(al, bh))


@jax.custom_vjp
def _xmm(a, b):
    return _xdot(a, b, ((1,), (0,)))


@jax.custom_vjp
def _xmm_nt(a, b):
    return _xdot(a, b, ((1,), (1,)))


@jax.custom_vjp
def _xmm_tn(a, b):
    return _xdot(a, b, ((0,), (0,)))


_xmm.defvjp(lambda a, b: (_xmm(a, b), (a, b)), lambda r, g: (_xmm_nt(g, r[1]), _xmm_tn(r[0], g)))
_xmm_nt.defvjp(lambda a, b: (_xmm_nt(a, b), (a, b)), lambda r, g: (_xmm(g, r[1]), _xmm_tn(g, r[0])))
_xmm_tn.defvjp(lambda a, b: (_xmm_tn(a, b), (a, b)), lambda r, g: (_xmm_nt(r[1], g), _xmm(r[0], g)))


def _sum01(m01, x, dims):
    h, l = _hi_lo(x)
    l2 = (x - h.astype(F32) - l.astype(F32)).astype(BF16)
    m = m01.astype(BF16)
    d = lambda q: lax.dot_general(m, q, (dims, ((), ())), preferred_element_type=F32)
    return d(h) + (d(l) + d(l2))


@jax.custom_vjp
def _cumsum_rows(x):
    n = x.shape[0]
    tri = lax.broadcasted_iota(jnp.int32, (n, n), 0) >= lax.broadcasted_iota(jnp.int32, (n, n), 1)
    return _sum01(tri, x, ((1,), (0,)))


def _cumsum_rows_b(_, g):
    n = g.shape[0]
    tri = lax.broadcasted_iota(jnp.int32, (n, n), 0) >= lax.broadcasted_iota(jnp.int32, (n, n), 1)
    return (_sum01(tri, g, ((0,), (0,))),)


_cumsum_rows.defvjp(lambda x: (_cumsum_rows(x), None), _cumsum_rows_b)


@functools.partial(jax.custom_vjp, nondiff_argnums=(1,))
def _colsum_as_rows(x, width):
    return _colsum_impl(x, width)


def _colsum_impl(x, width):
    h, l = _hi_lo(x)
    l2 = (x - h.astype(F32) - l.astype(F32)).astype(BF16)
    ones = jnp.ones((x.shape[0], width), BF16)
    d = lambda q: lax.dot_general(q, ones, (((0,), (0,)), ((), ())), preferred_element_type=F32)
    return d(h) + (d(l) + d(l2))


def _colsum_as_rows_f(x, width):
    return _colsum_impl(x, width), x.shape[0]


def _colsum_as_rows_b(width, n, g):
    return (_sum01(jnp.ones((n, width), F32), g, ((1,), (1,))),)


_colsum_as_rows.defvjp(_colsum_as_rows_f, _colsum_as_rows_b)


def _rms(x, w):
    return x * lax.rsqrt(jnp.mean(x * x, axis=-1, keepdims=True) + EPS) * w


def _sigmoid(x):
    return 1.0 / (1.0 + jnp.exp(-x))


def _silu(x):
    return x * _sigmoid(x)


def _softplus(x):
    return jnp.maximum(x, 0.0) + jnp.log(1.0 + jnp.exp(-jnp.abs(x)))


def _lane_pick(row, idx):
    lane = lax.broadcasted_iota(jnp.int32, row.shape, row.ndim - 1)
    return jnp.sum(jnp.where(lane == idx, row, 0.0), axis=-1, keepdims=True)


def _row_ids(row0, n):
    return row0 + lax.broadcasted_iota(jnp.int32, (n, 1), 0)


def _pick(m, cap):
    best = 64
    for t in range(64, min(m, cap) + 1, 64):
        if m % t == 0:
            best = t
    return best


def _cparams(sem):
    return pltpu.CompilerParams(dimension_semantics=sem, vmem_limit_bytes=VMEM_LIMIT)


def mm_nn(a, b, name, out_dtype=F32):
    M, K = a.shape
    N = b.shape[1]
    tm = _pick(M, 1408 if K <= 2048 else 704)
    tn = _pick(N, 512)

    def body(a_ref, b_ref, o_ref):
        o_ref[...] = _mm(a_ref[...], b_ref[...], None).astype(o_ref.dtype)

    return pl.pallas_call(
        body, name=name, grid=(N // tn, M // tm),
        in_specs=[pl.BlockSpec((tm, K), lambda j, i: (i, 0)), pl.BlockSpec((K, tn), lambda j, i: (0, j))],
        out_specs=pl.BlockSpec((tm, tn), lambda j, i: (i, j)),
        out_shape=jax.ShapeDtypeStruct((M, N), out_dtype),
        compiler_params=_cparams(("parallel", "parallel")),
    )(a, b)


def mm_nt(a, b, name, out_dtype=F32):
    M, K = a.shape
    N = b.shape[0]
    tm = _pick(M, 768)
    tn = _pick(N, 512)

    def body(a_ref, b_ref, o_ref):
        o_ref[...] = _mm_nt(a_ref[...], b_ref[...], None).astype(o_ref.dtype)

    return pl.pallas_call(
        body, name=name, grid=(N // tn, M // tm),
        in_specs=[pl.BlockSpec((tm, K), lambda j, i: (i, 0)), pl.BlockSpec((tn, K), lambda j, i: (j, 0))],
        out_specs=pl.BlockSpec((tm, tn), lambda j, i: (i, j)),
        out_shape=jax.ShapeDtypeStruct((M, N), out_dtype),
        compiler_params=_cparams(("parallel", "parallel")),
    )(a, b)


def mm_tn(a, b, name):
    M, K = a.shape
    N = b.shape[1]
    tk = _pick(K, 512)
    tn = _pick(N, 512)

    def body(a_ref, b_ref, o_ref):
        o_ref[...] = _mm_tn(a_ref[...], b_ref[...], None)

    return pl.pallas_call(
        body, name=name, grid=(K // tk, N // tn),
        in_specs=[pl.BlockSpec((M, tk), lambda i, j: (0, i)), pl.BlockSpec((M, tn), lambda i, j: (0, j))],
        out_specs=pl.BlockSpec((tk, tn), lambda i, j: (i, j)),
        out_shape=jax.ShapeDtypeStruct((K, N), F32),
        compiler_params=_cparams(("parallel", "parallel")),
    )(a, b)


def _row_specs(rows, tm):
    return [pl.BlockSpec((tm, w), functools.partial(lambda i, cb: (i, cb), cb=cb)) for (_, w, cb) in rows]


def _param_specs(params):
    return [pl.BlockSpec(p.shape, functools.partial(lambda i, nd: (0,) * nd, nd=p.ndim)) for p in params]


def rowwise_fwd(name, fn, rows, params, outs, tm=None):
    M = rows[0][0].shape[0]
    tm = tm or _pick(M, 704)
    nr, npar = len(rows), len(params)

    def body(*refs):
        row0 = pl.program_id(0) * tm
        vals = [r[...].astype(F32) for r in refs[:nr]] + [p[...] for p in refs[nr:nr + npar]]
        res = fn(row0, *vals)
        for o_ref, r in zip(refs[nr + npar:], res):
            o_ref[...] = r.astype(o_ref.dtype)

    return pl.pallas_call(
        body, name=name, grid=(M // tm,),
        in_specs=_row_specs(rows, tm) + _param_specs(params),
        out_specs=[pl.BlockSpec((tm, w), lambda i: (i, 0)) for (w, _) in outs],
        out_shape=[jax.ShapeDtypeStruct((M, w), dt) for (w, dt) in outs],
        compiler_params=_cparams(("parallel",)),
    )(*[r[0] for r in rows], *params)


def rowwise_bwd(name, fn, rows, params, douts, drow_dtypes, tm=None):
    M = rows[0][0].shape[0]
    tm = tm or _pick(M, 704)
    nr, npar, nd = len(rows), len(params), len(douts)
    want = [k for k, dt in enumerate(drow_dtypes) if dt is not None]

    def body(*refs):
        i = pl.program_id(0)
        row0 = i * tm
        vals = [r[...].astype(F32) for r in refs[:nr]] + [p[...] for p in refs[nr:nr + npar]]
        cots = tuple(d[...].astype(F32) for d in refs[nr + npar:nr + npar + nd])
        _, vjp = jax.vjp(functools.partial(fn, row0), *vals)
        grads = vjp(cots)
        o_refs = refs[nr + npar + nd:]
        for o_ref, k in zip(o_refs[:len(want)], want):
            o_ref[...] = grads[k].astype(o_ref.dtype)
        for o_ref, g in zip(o_refs[len(want):], grads[nr:]):
            @pl.when(i == 0)
            def _():
                o_ref[...] = g

            @pl.when(i > 0)
            def _():
                o_ref[...] += g

    res = pl.pallas_call(
        body, name=name, grid=(M // tm,),
        in_specs=_row_specs(rows, tm) + _param_specs(params) + _row_specs(douts, tm),
        out_specs=[pl.BlockSpec((tm, rows[k][1]), lambda i: (i, 0)) for k in want] + _param_specs(params),
        out_shape=[jax.ShapeDtypeStruct((M, rows[k][1]), drow_dtypes[k]) for k in want]
        + [jax.ShapeDtypeStruct(p.shape, F32) for p in params],
        compiler_params=_cparams(("arbitrary",)),
    )(*[r[0] for r in rows], *params, *[d[0] for d in douts])
    return res[:len(want)], res[len(want):]


def _fn_prenorm(row0, h, wpre):
    return (_rms(h, wpre),)


def _fn_resnorm(scale, row0, h, f, wpost, wpre):
    h2 = h + scale * _rms(f, wpost)
    return h2, _rms(h2, wpre)


def _fn_res_last(scale, row0, h, f, wpost):
    return (h + scale * _rms(f, wpost),)


def ffn_fwd(name, xn, wg, wu, wd, j):
    M = xn.shape[0]
    tm = _pick(M, 704)

    def body(x_ref, wg_ref, wu_ref, wd_ref, f_ref, a_ref, b_ref):
        s = pl.program_id(1)
        x = x_ref[...]
        a = _mm_nt(x, wg_ref[...], None)
        b = _mm_nt(x, wu_ref[...], None)
        c = _mm((_silu(a) * b).astype(BF16), wd_ref[...], None)

        @pl.when(s == 0)
        def _():
            f_ref[...] = c

        @pl.when(s > 0)
        def _():
            f_ref[...] += c

        a_ref[...] = a.astype(BF16)
        b_ref[...] = b.astype(BF16)

    wspec = wdspec = pl.BlockSpec((None, None, FSH, D), lambda i, s: (s, j, 0, 0))
    abspec = pl.BlockSpec((None, tm, FSH), lambda i, s: (s, i, 0))
    return pl.pallas_call(
        body, name=name, grid=(M // tm, NSH),
        in_specs=[pl.BlockSpec((tm, D), lambda i, s: (i, 0)), wspec, wspec, wdspec],
        out_specs=[pl.BlockSpec((tm, D), lambda i, s: (i, 0)), abspec, abspec],
        out_shape=[jax.ShapeDtypeStruct((M, D), F32), jax.ShapeDtypeStruct((NSH, M, FSH), BF16),
                   jax.ShapeDtypeStruct((NSH, M, FSH), BF16)],
        compiler_params=_cparams(("parallel", "arbitrary")),
    )(xn, wg, wu, wd)


def ffn_bwd_x(name, df, a, b, wg, wu, wd, j):
    M = df.shape[0]
    tm = _pick(M, 704)

    ts = tm // FFN_SUB

    def body(df_ref, a_ref, b_ref, wg_ref, wu_ref, wd_ref, dx_ref, da_ref, db_ref, hm_ref):
        @pl.when(pl.program_id(1) == 0)
        def _():
            dx_ref[...] = jnp.zeros_like(dx_ref)

        for r in range(FFN_SUB):
            rows = pl.ds(r * ts, ts)
            a_ = a_ref[rows, :].astype(F32)
            b_ = b_ref[rows, :].astype(F32)
            dh = _mm_nt(df_ref[rows, :], wd_ref[...], None)
            sig = _sigmoid(a_)
            sil = a_ * sig
            da = (dh * b_ * (sig * (1.0 + a_ * (1.0 - sig)))).astype(BF16)
            db = (dh * sil).astype(BF16)
            dx_ref[rows, :] += _mm(da, wg_ref[...], None) + _mm(db, wu_ref[...], None)
            da_ref[rows, :] = da
            db_ref[rows, :] = db
            hm_ref[rows, :] = (sil * b_).astype(BF16)

    wspec = wdspec = pl.BlockSpec((None, None, FSH, D), lambda i, s: (s, j, 0, 0))
    abspec = pl.BlockSpec((None, tm, FSH), lambda i, s: (s, i, 0))
    ab = jax.ShapeDtypeStruct((NSH, M, FSH), BF16)
    return pl.pallas_call(
        body, name=name, grid=(M // tm, NSH),
        in_specs=[pl.BlockSpec((tm, D), lambda i, s: (i, 0)), abspec, abspec, wspec, wspec, wdspec],
        out_specs=[pl.BlockSpec((tm, D), lambda i, s: (i, 0)), abspec, abspec, abspec],
        out_shape=[jax.ShapeDtypeStruct((M, D), F32), ab, ab, ab],
        compiler_params=_cparams(("parallel", "arbitrary")),
    )(df, a, b, wg, wu, wd)


def ffn_bwd_w(name, xn, df, da, db, hm):
    M = xn.shape[0]
    tm = _pick(M, 704)
    nt = M // tm

    def body(x_ref, df_ref, da_ref, db_ref, hm_ref, dwg_ref, dwu_ref, dwd_ref, ag, au, ad):
        i = pl.program_id(1)
        x = x_ref[...]
        g = _mm_tn(da_ref[...], x, None)
        u = _mm_tn(db_ref[...], x, None)
        d = _mm_tn(hm_ref[...], df_ref[...], None)

        @pl.when(i == 0)
        def _():
            ag[...] = g
            au[...] = u
            ad[...] = d

        @pl.when(i > 0)
        def _():
            ag[...] += g
            au[...] += u
            ad[...] += d

        @pl.when(i == nt - 1)
        def _():
            dwg_ref[...] = ag[...].astype(BF16)
            dwu_ref[...] = au[...].astype(BF16)
            dwd_ref[...] = ad[...].astype(BF16)

    xspec = pl.BlockSpec((tm, D), lambda s, i: (i, 0))
    abspec = pl.BlockSpec((None, tm, FSH), lambda s, i: (s, i, 0))
    return pl.pallas_call(
        body, name=name, grid=(NSH, nt),
        in_specs=[xspec, xspec, abspec, abspec, abspec],
        out_specs=[pl.BlockSpec((None, FSH, D), lambda s, i: (s, 0, 0))] * 3,
        out_shape=[jax.ShapeDtypeStruct((NSH, FSH, D), BF16)] * 3,
        scratch_shapes=[pltpu.VMEM((FSH, D), F32)] * 3,
        compiler_params=_cparams(("parallel", "arbitrary")),
    )(xn, df, da, db, hm)


def _t5_bucket_np(rel):
    n = np.maximum(rel, 0)
    n_f = np.maximum(n, 1).astype(np.float32)
    large = 16 + (np.log(n_f / np.float32(16)) / np.float32(math.log(8.0)) * np.float32(16)).astype(np.int32)
    large = np.minimum(large, 31)
    return np.where(n < 16, n, large).astype(np.int32)


def _swa_bucket_ids():
    qi = np.arange(BLK)[:, None]
    kj = np.arange(BLK)[None, :]
    out = np.full((3, BLK, 3 * BLK), -1, np.int32)
    for v in range(3):
        pos_q = v * BLK + qi - ZROWS
        rel_m = pos_q - (kj - ZROWS)
        ok_m = (kj >= ZROWS) & (rel_m >= 0) & (pos_q >= 0)
        out[v, :, 0:BLK] = np.where(ok_m, _t5_bucket_np(rel_m), -1)
        pos_kp = (v - 1) * BLK + kj - ZROWS
        rel_p = BLK + qi - kj
        ok_p = (pos_kp >= N_META) & (rel_p >= 0) & (rel_p < BLK) & np.full_like(ok_m, v >= 1)
        out[v, :, BLK:2 * BLK] = np.where(ok_p, _t5_bucket_np(rel_p), -1)
        pos_kc = v * BLK + kj - ZROWS
        rel_c = qi - kj
        ok_c = (pos_kc >= N_META) & (rel_c >= 0) & (rel_c < BLK)
        out[v, :, 2 * BLK:] = np.where(ok_c, _t5_bucket_np(rel_c), -1)
    return out


def swa_bias_fwd(table, ids):
    def body(t_ref, id_ref, o_ref):
        for v in range(3):
            for h in range(8):
                o_ref[v, h] = jnp.where(id_ref[v] < 0, NEG, 0.0)

            def step(b, carry):
                hit = id_ref[v] == b
                for h in range(8):
                    o_ref[v, h] += jnp.where(hit, t_ref[b, h], 0.0)
                return carry

            lax.fori_loop(0, 32, step, 0)

    return pl.pallas_call(
        body, name="swa_bias_fwd",
        in_specs=[pl.BlockSpec(memory_space=pltpu.SMEM), pl.BlockSpec(memory_space=pltpu.VMEM)],
        out_specs=pl.BlockSpec(memory_space=pltpu.VMEM),
        out_shape=jax.ShapeDtypeStruct((3, 8, BLK, 3 * BLK), F32),
        compiler_params=pltpu.CompilerParams(vmem_limit_bytes=VMEM_LIMIT),
    )(table, ids)


def swa_bias_bwd(dbias, ids):
    def body(d_ref, id_ref, o_ref):
        r = lax.broadcasted_iota(jnp.int32, (32, LANE), 0)
        c = lax.broadcasted_iota(jnp.int32, (32, LANE), 1)

        def step(b, acc):
            for v in range(3):
                hit = id_ref[v] == b
                for h in range(8):
                    m = jnp.where(hit, d_ref[v, h], 0.0)
                    s = jnp.sum(jnp.sum(m, axis=1, keepdims=True), axis=0, keepdims=True)
                    acc = acc + jnp.where((r == b) & (c == h), s, 0.0)
            return acc

        o_ref[...] = lax.fori_loop(0, 32, step, jnp.zeros((32, LANE), F32))

    return pl.pallas_call(
        body, name="swa_bias_bwd",
        in_specs=[pl.BlockSpec(memory_space=pltpu.VMEM), pl.BlockSpec(memory_space=pltpu.VMEM)],
        out_specs=pl.BlockSpec(memory_space=pltpu.VMEM),
        out_shape=jax.ShapeDtypeStruct((32, LANE), F32),
        compiler_params=pltpu.CompilerParams(vmem_limit_bytes=VMEM_LIMIT),
    )(dbias, ids)


def _swa_block(q, k3, v3, bias, sinks):
    heads = range(8)
    kh = [k3[:, (h // 4) * 128:(h // 4 + 1) * 128] for h in heads]
    vh = [v3[:, (h // 4) * 128:(h // 4 + 1) * 128] for h in heads]
    s = [_bmm_nt(q[:, h * 128:(h + 1) * 128], kh[h]) * 0.125 + bias[h] for h in heads]
    sink = [_lane_pick(sinks, h) for h in heads]
    m = [lax.stop_gradient(jnp.maximum(jnp.max(s[h], axis=-1, keepdims=True), sink[h])) for h in heads]
    e = [jnp.exp(s[h] - m[h]) for h in heads]
    p = [e[h] / (jnp.sum(e[h], axis=-1, keepdims=True) + jnp.exp(sink[h] - m[h])) for h in heads]
    return jnp.concatenate([_bmm(p[h], vh[h]) for h in heads], axis=1)


def _swa_in_specs():
    qs = pl.BlockSpec((BLK, 1024), lambda n: (n, E_QA // 1024))
    ks = [pl.BlockSpec((BLK, 256), lambda n: (0, E_KA // 256)),
          pl.BlockSpec((BLK, 256), lambda n: (jnp.maximum(n - 1, 0), E_KA // 256)),
          pl.BlockSpec((BLK, 256), lambda n: (n, E_KA // 256))]
    vs = [pl.BlockSpec((BLK, 256), lambda n: (0, E_VA // 256)),
          pl.BlockSpec((BLK, 256), lambda n: (jnp.maximum(n - 1, 0), E_VA // 256)),
          pl.BlockSpec((BLK, 256), lambda n: (n, E_VA // 256))]
    bs = pl.BlockSpec((None, 8, BLK, 3 * BLK), lambda n: (jnp.minimum(n, 2), 0, 0, 0))
    ss = pl.BlockSpec((1, LANE), lambda n: (0, 0))
    return [qs] + ks + vs + [bs, ss]


def swa_fwd(proj, bias, sinks, side=None):
    M = proj.shape[0]
    s_ins, s_specs, s_shapes, s_sems = _side_parts(side)
    ns, nso = len(s_ins), len(s_shapes)

    def body(q_ref, k0, k1, k2, v0, v1, v2, b_ref, s_ref, *rest):
        o_ref = rest[ns]
        _side_run(side, M // BLK, rest[:ns], rest[ns + 1:ns + 1 + nso], rest[ns + 1 + nso:])
        k3 = jnp.concatenate([k0[...], k1[...], k2[...]], axis=0)
        v3 = jnp.concatenate([v0[...], v1[...], v2[...]], axis=0)
        o_ref[...] = _swa_block(q_ref[...], k3, v3, b_ref[...], s_ref[...]).astype(o_ref.dtype)

    res = pl.pallas_call(
        body, name="swa_fwd", grid=(M // BLK,),
        in_specs=_swa_in_specs() + s_specs,
        out_specs=[pl.BlockSpec((BLK, 1024), lambda n: (n, 0))] + [pl.BlockSpec(memory_space=pl.ANY)] * nso,
        out_shape=[jax.ShapeDtypeStruct((M, 1024), BF16)] + s_shapes,
        scratch_shapes=s_sems,
        compiler_params=_cparams(("arbitrary",)),
    )(proj, proj, proj, proj, proj, proj, proj, bias, sinks, *s_ins)
    return res[0], res[1:]


def swa_bwd(proj, bias, sinks, do, side=None):
    M = proj.shape[0]
    s_ins, s_specs, s_shapes, s_sems = _side_parts(side)
    ns, nso = len(s_ins), len(s_shapes)

    def body(q_ref, k0, k1, k2, v0, v1, v2, b_ref, s_ref, do_ref, *rest):
        dq_ref, dk_ref, dv_ref, db_ref, ds_ref = rest[ns:ns + 5]
        _side_run(side, M // BLK, rest[:ns], rest[ns + 5:ns + 5 + nso], rest[ns + 5 + nso:])
        n = pl.program_id(0)

        @pl.when(n == 0)
        def _():
            dk_ref[...] = jnp.zeros_like(dk_ref)
            dv_ref[...] = jnp.zeros_like(dv_ref)
            ds_ref[...] = jnp.zeros_like(ds_ref)

        k3 = jnp.concatenate([k0[...], k1[...], k2[...]], axis=0)
        v3 = jnp.concatenate([v0[...], v1[...], v2[...]], axis=0)
        _, vjp = jax.vjp(_swa_block, q_ref[...], k3, v3, b_ref[...], s_ref[...])
        dq, dk3, dv3, dbias, dsink = vjp(do_ref[...].astype(F32))
        dq_ref[...] = dq
        prev = pl.multiple_of(jnp.maximum(n - 1, 0) * BLK, BLK)
        cur = pl.multiple_of(n * BLK, BLK)
        dk_ref[pl.ds(0, BLK), :] += dk3[0:BLK]
        dv_ref[pl.ds(0, BLK), :] += dv3[0:BLK]
        dk_ref[pl.ds(prev, BLK), :] += dk3[BLK:2 * BLK]
        dv_ref[pl.ds(prev, BLK), :] += dv3[BLK:2 * BLK]
        dk_ref[pl.ds(cur, BLK), :] += dk3[2 * BLK:]
        dv_ref[pl.ds(cur, BLK), :] += dv3[2 * BLK:]
        ds_ref[...] += dsink

        @pl.when(n <= 2)
        def _():
            db_ref[...] = dbias

        @pl.when(n > 2)
        def _():
            db_ref[...] += dbias

    res = pl.pallas_call(
        body, name="swa_bwd", grid=(M // BLK,),
        in_specs=_swa_in_specs() + [pl.BlockSpec((BLK, 1024), lambda n: (n, 0))] + s_specs,
        out_specs=[pl.BlockSpec((BLK, 1024), lambda n: (n, 0)),
                   pl.BlockSpec((M, 256), lambda n: (0, 0)), pl.BlockSpec((M, 256), lambda n: (0, 0)),
                   pl.BlockSpec((None, 8, BLK, 3 * BLK), lambda n: (jnp.minimum(n, 2), 0, 0, 0)),
                   pl.BlockSpec((1, LANE), lambda n: (0, 0))] + [pl.BlockSpec(memory_space=pl.ANY)] * nso,
        out_shape=[jax.ShapeDtypeStruct((M, 1024), F32), jax.ShapeDtypeStruct((M, 256), F32),
                   jax.ShapeDtypeStruct((M, 256), F32), jax.ShapeDtypeStruct((3, 8, BLK, 3 * BLK), F32),
                   jax.ShapeDtypeStruct((1, LANE), F32)] + s_shapes,
        scratch_shapes=s_sems,
        compiler_params=_cparams(("arbitrary",)),
    )(proj, proj, proj, proj, proj, proj, proj, bias, sinks, do, *s_ins)
    return res[:5], res[5:]


def _shift_rows_impl(x, k):
    n = x.shape[0]
    rolled = pltpu.roll(x, k, 0)
    return jnp.where(_row_ids(0, n) >= k, rolled, 0.0)


def _unshift_rows_impl(g, k):
    n = g.shape[0]
    rolled = pltpu.roll(g, n - k, 0)
    return jnp.where(_row_ids(0, n) < n - k, rolled, 0.0)


@functools.partial(jax.custom_vjp, nondiff_argnums=(1,))
def _shift_rows(x, k):
    return _shift_rows_impl(x, k)


def _shift_rows_f(x, k):
    return _shift_rows_impl(x, k), None


def _shift_rows_b(k, _, g):
    return (_unshift_rows_impl(g, k),)


_shift_rows.defvjp(_shift_rows_f, _shift_rows_b)


def _conv_silu(x, w):
    rid = lax.broadcasted_iota(jnp.int32, w.shape, 0)
    y = x * jnp.sum(jnp.where(rid == 3, w, 0.0), axis=0, keepdims=True)
    for k in range(1, 4):
        y = y + _shift_rows(x, k) * jnp.sum(jnp.where(rid == 3 - k, w, 0.0), axis=0, keepdims=True)
    y = jnp.where(_row_ids(0, x.shape[0]) >= ZROWS, y, 0.0)
    return _silu(y)


def conv_fwd(proj, conv_w):
    M = proj.shape[0]
    nb = conv_w.shape[1] // LANE

    def body(x_ref, w_ref, o_ref):
        o_ref[...] = _conv_silu(x_ref[...], w_ref[...])

    return pl.pallas_call(
        body, name="conv_fwd", grid=(nb,),
        in_specs=[pl.BlockSpec((M, LANE), lambda c: (0, E_QB // LANE + c)), pl.BlockSpec((4, LANE), lambda c: (0, c))],
        out_specs=pl.BlockSpec((M, LANE), lambda c: (0, c)),
        out_shape=jax.ShapeDtypeStruct((M, conv_w.shape[1]), F32),
        compiler_params=_cparams(("parallel",)),
    )(proj, conv_w)


def conv_bwd(proj, conv_w, dy):
    M = proj.shape[0]
    nb = conv_w.shape[1] // LANE

    def body(x_ref, w_ref, dy_ref, dx_ref, dw_ref):
        _, vjp = jax.vjp(_conv_silu, x_ref[...], w_ref[...])
        dx, dw = vjp(dy_ref[...])
        dx_ref[...] = dx
        dw_ref[...] = dw

    return pl.pallas_call(
        body, name="conv_bwd", grid=(nb,),
        in_specs=[pl.BlockSpec((M, LANE), lambda c: (0, E_QB // LANE + c)), pl.BlockSpec((4, LANE), lambda c: (0, c)),
                  pl.BlockSpec((M, LANE), lambda c: (0, c))],
        out_specs=[pl.BlockSpec((M, LANE), lambda c: (0, c)), pl.BlockSpec((4, LANE), lambda c: (0, c))],
        out_shape=[jax.ShapeDtypeStruct((M, conv_w.shape[1]), F32), jax.ShapeDtypeStruct(conv_w.shape, F32)],
        compiler_params=_cparams(("parallel",)),
    )(proj, conv_w, dy)


def _fn_dn_prep(row0, yq, yk, ba, dnp):
    tm = yq.shape[0]
    real = _row_ids(row0, tm) >= ZROWS
    qs, ks, gs, bs = [], [], [], []
    for h in range(4):
        q = yq[:, h * 128:(h + 1) * 128]
        k = yk[:, h * 128:(h + 1) * 128]
        qs.append(q * lax.rsqrt(jnp.sum(q * q, axis=-1, keepdims=True) + 1e-6) * (128.0 ** -0.5))
        ks.append(k * lax.rsqrt(jnp.sum(k * k, axis=-1, keepdims=True) + 1e-6))
        beta = _sigmoid(_lane_pick(ba, h))
        g = -jnp.exp(_lane_pick(dnp, h)) * _softplus(_lane_pick(ba, 4 + h) + _lane_pick(dnp, 4 + h))
        g = jnp.where(real, g, 0.0)
        gs.append(jnp.broadcast_to(g, (tm, 128)))
        bs.append(jnp.broadcast_to(beta, (tm, 128)))
    cat = lambda xs: jnp.concatenate(xs, axis=1)
    return cat(qs), cat(ks), cat(gs), cat(bs)


def _zip(f, *lists):
    return [f(*args) for args in zip(*lists)]


def _unit_lower_inv_impl(a):
    n = a[0].shape[0]
    eye = (lax.broadcasted_iota(jnp.int32, (n, n), 0) == lax.broadcasted_iota(jnp.int32, (n, n), 1)).astype(F32)
    nn = ((1,), (0,))
    p = [-x for x in a]
    t = [eye + x for x in p]
    for _ in range(int(math.log2(n)) - 1):
        p = _zip(lambda x: _xdot(x, x, nn), p)
        t = _zip(lambda x, y: x + _xdot(x, y, nn), t, p)
    return t


@jax.custom_vjp
def _unit_lower_inv(a):
    return _unit_lower_inv_impl(a)


def _unit_lower_inv_f(a):
    t = _unit_lower_inv_impl(a)
    return t, t


def _unit_lower_inv_b(t, g):
    tg = _zip(lambda x, y: _xdot(x, y, ((0,), (0,))), t, g)
    return (_zip(lambda x, y: -_xdot(x, y, ((1,), (1,))), tg, t),)


_unit_lower_inv.defvjp(_unit_lower_inv_f, _unit_lower_inv_b)


@jax.custom_vjp
def _known_inv(a, t):
    return t


_known_inv.defvjp(lambda a, t: (t, t), lambda t, g: (_unit_lower_inv_b(t, g)[0], [jnp.zeros_like(x) for x in t]))


def _dn_block(q, k, v, gb, bb, S, t_kept=None):
    nh = len(S)
    r = lax.broadcasted_iota(jnp.int32, (CH, CH), 0)
    c = lax.broadcasted_iota(jnp.int32, (CH, CH), 1)
    tri_incl = r >= c
    gcb = _zip(_cumsum_rows, gb)
    gamma = _zip(lambda x: jnp.where(tri_incl, jnp.exp(jnp.where(tri_incl, x[:, :CH] - x[:, :CH].T, 0.0)), 0.0), gcb)
    kb = _zip(jnp.multiply, k, bb)
    vb = _zip(jnp.multiply, v, bb)
    a = _zip(lambda m, g: jnp.where(r > c, m * g, 0.0), _zip(_bmm_nt, kb, k), gamma)
    t = _unit_lower_inv(a) if t_kept is None else _known_inv(a, list(t_kept))
    eg = _zip(jnp.exp, gcb)
    u = _zip(_xmm, t, vb)
    w = _zip(_xmm, t, _zip(jnp.multiply, kb, eg))
    attn = _zip(lambda m, g: m * g, _zip(_bmm_nt, q, k), gamma)
    gtot = _zip(lambda x: jnp.sum(x, axis=0, keepdims=True), gb)
    k_dec = _zip(lambda x, gt, gc: x * jnp.exp(gt - gc), k, gtot, gcb)
    q_dec = _zip(jnp.multiply, q, eg)
    S = list(S)
    o, starts = [], []
    for i0 in range(0, len(q), nh):
        idx = range(i0, i0 + nh)
        starts.append(list(S))
        v_new = [u[i] - m for i, m in zip(idx, [_bmm(w[i], S[h]) for h, i in enumerate(idx)])]
        oq = [_bmm(q_dec[i], S[h]) for h, i in enumerate(idx)]
        oa = [_bmm(attn[i], vn) for i, vn in zip(idx, v_new)]
        kv = [_bmm_tn(k_dec[i], vn) for i, vn in zip(idx, v_new)]
        o += _zip(jnp.add, oq, oa)
        S = [S[h] * jnp.exp(jnp.broadcast_to(gtot[i], S[h].shape)) + kv[h] for h, i in enumerate(idx)]
    return o, S, starts, t


def _gla_block(q, k, v, glog, S):
    nh = len(S)
    tri = lax.broadcasted_iota(jnp.int32, (CH, CH), 0) >= lax.broadcasted_iota(jnp.int32, (CH, CH), 1)
    bcum = _zip(_cumsum_rows, glog)
    q_dec = _zip(lambda x, b: x * (128.0 ** -0.5) * jnp.exp(b), q, bcum)
    attn = _zip(lambda m: jnp.where(tri, m, 0.0), _zip(_bmm_nt, q_dec, _zip(lambda x, b: x * jnp.exp(-b), k, bcum)))
    o_in = _zip(_bmm, attn, v)
    k_dec = _zip(lambda x, g, b: x * jnp.exp(jnp.sum(g, axis=0, keepdims=True) - b), k, glog, bcum)
    decay = _zip(lambda g, x: jnp.exp(_colsum_as_rows(g, x.shape[1])), glog, v)
    kv = _zip(_bmm_tn, k_dec, v)
    S = list(S)
    o, starts = [], []
    for i0 in range(0, len(q), nh):
        idx = range(i0, i0 + nh)
        starts.append(list(S))
        o += [o_in[i] + m for i, m in zip(idx, [_bmm(q_dec[i], S[h]) for h, i in enumerate(idx)])]
        S = [S[h] * decay[i] + kv[i] for h, i in enumerate(idx)]
    return o, S, starts


class Side:
    def __init__(self, ins, out_shapes, nsem, events):
        self.ins, self.out_shapes, self.nsem, self.events = list(ins), list(out_shapes), nsem, events


def _side_parts(side):
    if side is None:
        return [], [], [], []
    anyspec = pl.BlockSpec(memory_space=pl.ANY)
    return (side.ins, [anyspec] * len(side.ins), side.out_shapes,
            [pltpu.SemaphoreType.DMA((side.nsem,)), pltpu.SemaphoreType.DMA((side.nsem,))])


def _side_run(side, n_steps, in_refs, out_refs, sems):
    if side is None:
        return
    for step, fn in side.events(n_steps, in_refs, out_refs, *sems):
        pl.when(pl.program_id(0) == step)(fn)


def chunk_fwd(name, chunk_fn, ins, dv, side=None, keep=()):
    M = ins[0][0].shape[0]
    NC = M // CH
    N = NC // CPS
    ni, nk = len(ins), len(keep)
    ws = [w for (_, w, _) in ins]
    s_ins, s_specs, s_shapes, s_sems = _side_parts(side)
    ns, nso = len(s_ins), len(s_shapes)

    def body(*refs):
        o0 = ni + ns
        o_ref, sall_ref = refs[o0:o0 + 2]
        k_refs = refs[o0 + 2:o0 + 2 + nk]
        s_ref = refs[o0 + 2 + nk + nso]
        _side_run(side, N, refs[ni:o0], refs[o0 + 2 + nk:o0 + 2 + nk + nso], refs[o0 + 3 + nk + nso:])

        @pl.when(pl.program_id(0) == 0)
        def _():
            s_ref[...] = jnp.zeros_like(s_ref)

        problems = [(cc, h) for cc in range(CPS) for h in range(4)]
        lists = [[r[cc * CH:(cc + 1) * CH, h * w:(h + 1) * w] for cc, h in problems] for r, w in zip(refs[:ni], ws)]
        o, s_new, starts, *kept = chunk_fn(*lists, [s_ref[h] for h in range(4)])
        for b, (cc, h) in enumerate(problems):
            o_ref[cc * CH:(cc + 1) * CH, h * dv:(h + 1) * dv] = o[b]
            sall_ref[h, cc] = starts[cc][h]
            for k_ref, vals in zip(k_refs, kept):
                k_ref[h, cc] = vals[b]
        for h in range(4):
            s_ref[h] = s_new[h]

    per_chunk = lambda r, c: pl.BlockSpec((4, CPS, r, c), lambda n: (0, n, 0, 0))
    specs = [pl.BlockSpec((CPS * CH, 4 * w), functools.partial(lambda n, cb: (n, cb), cb=cb // 4)) for (_, w, cb) in ins]
    res = pl.pallas_call(
        body, name=name, grid=(N,),
        in_specs=specs + s_specs,
        out_specs=[pl.BlockSpec((CPS * CH, 4 * dv), lambda n: (n, 0)), per_chunk(128, dv)] + [per_chunk(r, c) for r, c in keep]
        + [pl.BlockSpec(memory_space=pl.ANY)] * nso,
        out_shape=[jax.ShapeDtypeStruct((M, 4 * dv), F32), jax.ShapeDtypeStruct((4, NC, 128, dv), F32)]
        + [jax.ShapeDtypeStruct((4, NC, r, c), F32) for r, c in keep] + s_shapes,
        scratch_shapes=[pltpu.VMEM((4, 128, dv), F32)] + s_sems,
        compiler_params=_cparams(("arbitrary",)),
    )(*[a for (a, _, _) in ins], *s_ins)
    return res[0], res[1], res[2:2 + nk], res[2 + nk:]


def chunk_bwd(name, chunk_fn, ins, dv, s_all, do, side=None, kept=()):
    M = ins[0][0].shape[0]
    N = M // CH // CPS
    ni, nk = len(ins), len(kept)
    ws = [w for (_, w, _) in ins]
    s_ins, s_specs, s_shapes, s_sems = _side_parts(side)
    ns, nso = len(s_ins), len(s_shapes)

    def body(*refs):
        sall_ref, do_ref = refs[ni:ni + 2]
        k_refs = refs[ni + 2:ni + 2 + nk]
        o0 = ni + 2 + nk + ns
        d_refs = refs[o0:o0 + ni]
        ds_ref = refs[o0 + ni + nso]
        _side_run(side, N, refs[ni + 2 + nk:o0], refs[o0 + ni:o0 + ni + nso], refs[o0 + ni + nso + 1:])

        @pl.when(pl.program_id(0) == 0)
        def _():
            ds_ref[...] = jnp.zeros_like(ds_ref)

        problems = [(cc, h) for cc in range(CPS) for h in range(4)]
        lists = [[r[cc * CH:(cc + 1) * CH, h * w:(h + 1) * w] for cc, h in problems] for r, w in zip(refs[:ni], ws)]
        kept_lists = [[k_ref[h, cc] for cc, h in problems] for k_ref in k_refs]
        _, vjp = jax.vjp(lambda *a: tuple(chunk_fn(*a)[:2]), *lists, [sall_ref[h, 0] for h in range(4)], *kept_lists)
        grads = vjp(([do_ref[cc * CH:(cc + 1) * CH, h * dv:(h + 1) * dv] for cc, h in problems],
                     [ds_ref[h] for h in range(4)]))
        for d_ref, w, g in zip(d_refs, ws, grads[:ni]):
            for b, (cc, h) in enumerate(problems):
                d_ref[cc * CH:(cc + 1) * CH, h * w:(h + 1) * w] = g[b]
        for h in range(4):
            ds_ref[h] = grads[ni][h]

    rev = lambda n: N - 1 - n
    per_chunk = lambda r, c: pl.BlockSpec((4, CPS, r, c), lambda n: (0, rev(n), 0, 0))
    specs = [pl.BlockSpec((CPS * CH, 4 * w), functools.partial(lambda n, cb: (rev(n), cb), cb=cb // 4)) for (_, w, cb) in ins]
    res = pl.pallas_call(
        body, name=name, grid=(N,),
        in_specs=specs + [per_chunk(128, dv), pl.BlockSpec((CPS * CH, 4 * dv), lambda n: (rev(n), 0))]
        + [per_chunk(*a.shape[2:]) for a in kept] + s_specs,
        out_specs=[pl.BlockSpec((CPS * CH, 4 * w), lambda n: (rev(n), 0)) for w in ws] + [pl.BlockSpec(memory_space=pl.ANY)] * nso,
        out_shape=[jax.ShapeDtypeStruct((M, 4 * w), F32) for w in ws] + s_shapes,
        scratch_shapes=[pltpu.VMEM((4, 128, dv), F32)] + s_sems,
        compiler_params=_cparams(("arbitrary",)),
    )(*[a for (a, _, _) in ins], s_all, do, *kept, *s_ins)
    return res[:ni], res[ni:]


def _fn_gate_out(hd, row0, o, z, w):
    outs = []
    for h in range(4):
        outs.append(_rms(o[:, h * hd:(h + 1) * hd], w) * _silu(z[:, h * hd:(h + 1) * hd]))
    return (jnp.concatenate(outs, axis=1),)


def _fn_gla_prep(row0, gk, wgu, bg):
    x = _mm(gk, wgu) + bg
    ls = jnp.minimum(x, 0.0) - jnp.log(1.0 + jnp.exp(-jnp.abs(x)))
    return (jnp.where(_row_ids(row0, gk.shape[0]) >= ZROWS, ls / 16.0, 0.0),)


def loss_call(y, tgt):
    M = y.shape[0]
    tm = _pick(M, 512)

    def body(y_ref, t_ref, l_ref, dy_ref):
        i = pl.program_id(0)
        e = y_ref[...] - t_ref[...]
        dy_ref[...] = e * (1.0 / D)
        part = 0.5 * jnp.sum(jnp.sum(e * e, axis=1, keepdims=True) * (1.0 / D), axis=0, keepdims=True)
        part = jnp.broadcast_to(part, (8, LANE))

        @pl.when(i == 0)
        def _():
            l_ref[...] = part

        @pl.when(i > 0)
        def _():
            l_ref[...] += part

    return pl.pallas_call(
        body, name="loss", grid=(M // tm,),
        in_specs=[pl.BlockSpec((tm, D), lambda i: (i, 0))] * 2,
        out_specs=[pl.BlockSpec((8, LANE), lambda i: (0, 0)), pl.BlockSpec((tm, D), lambda i: (i, 0))],
        out_shape=[jax.ShapeDtypeStruct((8, LANE), F32), jax.ShapeDtypeStruct((M, D), F32)],
        compiler_params=_cparams(("arbitrary",)),
    )(y, tgt)


def _bf(x):
    return x.astype(BF16)


def core_step(x, tgt, W, comm=None):
    S = x.shape[0]
    M = S + PADR
    ids = jnp.asarray(_swa_bucket_ids())
    h0 = jnp.concatenate([jnp.zeros((ZROWS, D), F32), W["meta"], x], axis=0)
    nw = W["norm"]
    nrow = lambda l, k: nw[l, k][None, :]
    layers = list(W["layers"])
    ffw = lambda l: (layers[l]["ffn_g"], layers[l]["ffn_u"], layers[l]["ffn_d"])
    sinks = jnp.pad(W["sinks"], ((0, 0), (0, LANE - 8)))
    dnp = jnp.pad(jnp.concatenate([W["a_log"], W["dt_bias"]], axis=1), ((0, 0), (0, LANE - 8)))
    wgu = jnp.pad(W["gate_up"], ((0, LANE - 16), (0, 0)))
    bg = W["b_gate"]
    full = lambda a: (a, a.shape[1], 0)

    saved = []
    h = h0
    (hn,) = rowwise_fwd("prenorm_0", _fn_prenorm, [full(h)], [nrow(0, 0)], [(D, BF16)])
    bias = swa_bias_fwd(W["rel"], ids)
    for l in range(2):
        st = {"h_a": h, "hn_a": hn}
        f1, a1, b1 = ffn_fwd(f"ffn_fwd_{l}0", hn, *ffw(l),0)
        h, hn = rowwise_fwd(f"resnorm_{l}1", functools.partial(_fn_resnorm, 0.5), [full(h), full(f1)],
                            [nrow(l, 1), nrow(l, 2)], [(D, F32), (D, BF16)])
        st.update(f1=f1, a1=a1, b1=b1, h_b=h, hn_b=hn)
        if l == 0:
            proj = mm_nt(hn, layers[0]["w_in"], "e_proj")
            o_a, got_a = swa_fwd(proj, bias, sinks, side=comm.gather_side(0) if comm else None)
            y = conv_fwd(proj, W["conv"])
            qn, kn, gb, bb = rowwise_fwd(
                "dn_prep", _fn_dn_prep, [(y, 512, 0), (y, 512, 1), (proj, LANE, E_BA // LANE)], [dnp], [(512, F32)] * 4)
            ins = [(qn, 128, 0), (kn, 128, 0), (y, 128, 8), (gb, 128, 0), (bb, 128, 0)]
            o_dn, s_all, (t_inv,), got = chunk_fwd("dn_fwd", _dn_block, ins, 128, keep=[(CH, CH)],
                                                   side=comm.gather_side(1) if comm else None)
            if comm:
                layers[1] = comm.gather_done(got_a, got)
            (o_b,) = rowwise_fwd("dn_out", functools.partial(_fn_gate_out, 128),
                                 [full(o_dn), (proj, 512, E_ZB // 512)], [W["dn_norm"]], [(512, BF16)])
            omix = jnp.concatenate([o_a, o_b], axis=1)
            mix = mm_nn(omix, layers[0]["w_out"], "e_mix")
            st.update(proj=proj, y=y, qn=qn, kn=kn, gb=gb, bb=bb, o_dn=o_dn, s_all=s_all, t_inv=t_inv, omix=omix)
        else:
            proj = mm_nt(hn, layers[1]["w_in"], "o_proj")
            (glog,) = rowwise_fwd("gla_prep", _fn_gla_prep, [(proj, LANE, O_GK // LANE)], [wgu, bg], [(512, F32)])
            ins = [(proj, 128, O_Q // 128), (proj, 128, O_K // 128), (proj, 256, O_V // 256), (glog, 128, 0)]
            o_g, s_all, _, _ = chunk_fwd("gla_fwd", _gla_block, ins, 256)
            (omix,) = rowwise_fwd("gla_out", functools.partial(_fn_gate_out, 256),
                                  [full(o_g), (proj, 1024, O_G // 1024)], [W["gla_norm"]], [(1024, BF16)])
            mix = mm_nn(omix, layers[1]["w_out"], "o_mix")
            st.update(proj=proj, glog=glog, o_g=o_g, s_all=s_all, omix=omix)
        h, hn = rowwise_fwd(f"resnorm_{l}3", functools.partial(_fn_resnorm, 1.0), [full(h), full(mix)],
                            [nrow(l, 3), nrow(l, 4)], [(D, F32), (D, BF16)])
        st.update(mix=mix, h_c=h, hn_c=hn)
        f2, a2, b2 = ffn_fwd(f"ffn_fwd_{l}1", hn, *ffw(l),1)
        st.update(f2=f2, a2=a2, b2=b2)
        if l == 0:
            h, hn = rowwise_fwd("resnorm_05", functools.partial(_fn_resnorm, 0.5), [full(h), full(f2)],
                                [nrow(0, 5), nrow(1, 0)], [(D, F32), (D, BF16)])
        else:
            (h,) = rowwise_fwd("res_last", functools.partial(_fn_res_last, 0.5), [full(h), full(f2)],
                               [nrow(1, 5)], [(D, F32)])
        saved.append(st)

    loss_blk, dy = loss_call(h[PADR:], tgt)
    dh = jnp.concatenate([jnp.zeros((PADR, D), F32), dy], axis=0)

    G = {}
    dnorm = [[None] * 6 for _ in range(2)]
    dWg = [[None, None], [None, None]]
    dWu = [[None, None], [None, None]]
    dWd = [[None, None], [None, None]]
    dhn = None
    for l in (1, 0):
        st = saved[l]
        if l == 1:
            (dh_, df), (dw5,) = rowwise_bwd(
                "res_last_b", functools.partial(_fn_res_last, 0.5), [full(st["h_c"]), full(st["f2"])], [nrow(1, 5)],
                [full(dh)], [F32, BF16])
            dnorm[1][5] = dw5
        else:
            (dh_, df), (dw5, dw0n) = rowwise_bwd(
                "resnorm_05_b", functools.partial(_fn_resnorm, 0.5), [full(st["h_c"]), full(st["f2"])],
                [nrow(0, 5), nrow(1, 0)], [full(dh), full(dhn)], [F32, BF16])
            dnorm[0][5] = dw5
            dnorm[1][0] = dw0n
        dh = dh_
        dxn, da, db, hm = ffn_bwd_x(f"ffn_bx_{l}1", df, st["a2"], st["b2"], *ffw(l),1)
        dWg[l][1], dWu[l][1], dWd[l][1] = ffn_bwd_w(f"ffn_bw_{l}1", st["hn_c"], df, da, db, hm)
        (dh_, dmix), (dw3, dw4) = rowwise_bwd(
            f"resnorm_{l}3_b", functools.partial(_fn_resnorm, 1.0), [full(st["h_b"]), full(st["mix"])],
            [nrow(l, 3), nrow(l, 4)], [full(dh), full(dxn)], [F32, BF16])
        dnorm[l][3], dnorm[l][4] = dw3, dw4
        dh = dh_
        proj = st["proj"]
        if l == 1:
            G["o_out"] = mm_tn(st["omix"], dmix, "o_out_dw")
            domix = mm_nt(dmix, layers[1]["w_out"], "o_mix_dx")
            (do_g, dgate), (dgn,) = rowwise_bwd(
                "gla_out_b", functools.partial(_fn_gate_out, 256), [full(st["o_g"]), (proj, 1024, O_G // 1024)],
                [W["gla_norm"]], [full(domix)], [F32, F32])
            G["gla_norm"] = dgn
            ins = [(proj, 128, O_Q // 128), (proj, 128, O_K // 128), (proj, 256, O_V // 256), (st["glog"], 128, 0)]
            (dq, dk, dv, dglog), _ = chunk_bwd("gla_bwd", _gla_block, ins, 256, st["s_all"], do_g)
            (dgk,), (dwgu, dbg) = rowwise_bwd("gla_prep_b", _fn_gla_prep, [(proj, LANE, O_GK // LANE)], [wgu, bg],
                                              [full(dglog)], [F32])
            G["gate_up"] = dwgu[:16]
            G["b_gate"] = dbg
            dproj = _bf(jnp.concatenate([dq, dk, dv, dgate, dgk, jnp.zeros((M, O_END - O_GK - LANE), F32)], axis=1))
            G["o_in"] = mm_tn(dproj, st["hn_b"], "o_in_dw")
            dhn_b = mm_nn(dproj, layers[1]["w_in"], "o_proj_dx")
        else:
            G["e_out"] = mm_tn(st["omix"], dmix, "e_out_dw")
            domix = mm_nt(dmix, layers[0]["w_out"], "e_mix_dx")
            (do_dn, dz), (ddn,) = rowwise_bwd(
                "dn_out_b", functools.partial(_fn_gate_out, 128), [full(st["o_dn"]), (proj, 512, E_ZB // 512)],
                [W["dn_norm"]], [(domix, 512, 2)], [F32, F32])
            G["dn_norm"] = ddn
            ins = [(st["qn"], 128, 0), (st["kn"], 128, 0), (st["y"], 128, 8), (st["gb"], 128, 0), (st["bb"], 128, 0)]
            sides = comm.scatter_sides(layer_grad_items(1, dWg, dWu, dWd, G["o_in"], G["o_out"])) if comm else (None, None)
            (dqn, dkn, dvv, dgb, dbb), got = chunk_bwd("dn_bwd", _dn_block, ins, 128, st["s_all"], do_dn, side=sides[0],
                                                        kept=[st["t_inv"]])
            (dyq, dyk, dba), (ddnp,) = rowwise_bwd(
                "dn_prep_b", _fn_dn_prep, [(st["y"], 512, 0), (st["y"], 512, 1), (proj, LANE, E_BA // LANE)], [dnp],
                [full(dqn), full(dkn), full(dgb), full(dbb)], [F32, F32, F32])
            G["a_log"] = ddnp[:, 0:4]
            G["dt_bias"] = ddnp[:, 4:8]
            dyc = jnp.concatenate([dyq, dyk, dvv], axis=1)
            dxc, dconv = conv_bwd(proj, W["conv"], dyc)
            G["conv"] = dconv
            (dq_a, dk_a, dv_a, dbias, dsink), got_b = swa_bwd(proj, bias, sinks, domix, side=sides[1])
            if comm:
                comm.scatter_done(got, got_b)
            G["sinks"] = dsink[:, :8]
            G["rel"] = swa_bias_bwd(dbias, ids)[:, :8]
            dproj = _bf(jnp.concatenate([dq_a, dk_a, dv_a, dxc, dz, dba, jnp.zeros((M, E_END - E_BA - LANE), F32)], axis=1))
            G["e_in"] = mm_tn(dproj, st["hn_b"], "e_in_dw")
            dhn_b = mm_nn(dproj, layers[0]["w_in"], "e_proj_dx")
        (dh_, df), (dw1, dw2) = rowwise_bwd(
            f"resnorm_{l}1_b", functools.partial(_fn_resnorm, 0.5), [full(st["h_a"]), full(st["f1"])],
            [nrow(l, 1), nrow(l, 2)], [full(dh), full(dhn_b)], [F32, BF16])
        dnorm[l][1], dnorm[l][2] = dw1, dw2
        dh = dh_
        dxn, da, db, hm = ffn_bwd_x(f"ffn_bx_{l}0", df, st["a1"], st["b1"], *ffw(l),0)
        dWg[l][0], dWu[l][0], dWd[l][0] = ffn_bwd_w(f"ffn_bw_{l}0", st["hn_a"], df, da, db, hm)
        dhn = dxn
    (dh0p,), (dw00,) = rowwise_bwd("prenorm_0_b", _fn_prenorm, [full(saved[0]["h_a"])], [nrow(0, 0)], [full(dhn)], [F32])
    dnorm[0][0] = dw00
    dh = dh + dh0p
    G["meta"] = dh[ZROWS:PADR]
    G["norm"] = jnp.stack([jnp.concatenate(r, axis=0) for r in dnorm], axis=0)
    G["items"] = [layer_grad_items(0, dWg, dWu, dWd, G["e_in"], G["e_out"]),
                  None if comm else layer_grad_items(1, dWg, dWu, dWd, G["o_in"], G["o_out"])]
    return loss_blk, dh[PADR:], G


NAMES = [("meta", "meta_tokens"), ("norm", "norm_w"), ("ffn_g", "ffn_w_gate"), ("ffn_u", "ffn_w_up"),
         ("ffn_d", "ffn_w_down"), ("rel", "rel_bias_table"), ("e_in", "even_w_in"), ("conv", "even_conv_w"),
         ("sinks", "swa_sinks"), ("a_log", "dn_a_log"), ("dt_bias", "dn_dt_bias"), ("dn_norm", "dn_norm_w"),
         ("e_out", "even_w_out"), ("o_in", "odd_w_in"), ("gate_up", "gla_w_gate_up"), ("b_gate", "gla_b_gate"),
         ("gla_norm", "gla_norm_w"), ("o_out", "odd_w_out")]
BIG = ["ffn_g", "ffn_u", "ffn_d", "e_in", "e_out", "o_in", "o_out"]
IN_ROWS = 800
SMALL = [("meta", (16, 256)), ("norm", (2, 6, 256)), ("conv", (1, 4, 384)), ("gate_up", (1, 16, 128)),
         ("b_gate", (1, 128)), ("gla_norm", (1, 64))]
REPL = [("rel", (32, 8)), ("sinks", (1, 8)), ("a_log", (1, 4)), ("dt_bias", (1, 4)), ("dn_norm", (1, 128))]
SMALL_REP = 88 * LANE
SMALL_ROWS = 96


def pack_small(t):
    a = jnp.concatenate([t[n].reshape(-1) for n, _ in SMALL])
    b = jnp.concatenate([t[n].reshape(-1) for n, _ in REPL])
    flat = jnp.concatenate([a, jnp.zeros((SMALL_REP - a.shape[0],), F32), b,
                            jnp.zeros((SMALL_ROWS * LANE - SMALL_REP - b.shape[0],), F32)])
    return flat.reshape(SMALL_ROWS, LANE)


def unpack_small(p):
    flat = p.reshape(-1)
    out, r = {}, 0
    for n, shp in SMALL:
        k = int(np.prod(shp))
        out[n] = flat[r:r + k].reshape(shp)
        r += k
    r = SMALL_REP
    for n, shp in REPL:
        k = int(np.prod(shp))
        out[n] = flat[r:r + k].reshape(shp)
        r += k
    return out


IN_SRC = (706, 772)


def weight_pieces(wt, l):
    inn = wt["e_in" if l == 0 else "o_in"]
    inn = jnp.pad(inn, ((0, IN_ROWS - inn.shape[0]), (0, 0))).reshape(2, IN_ROWS // 2, D)
    out = wt["e_out" if l == 0 else "o_out"].reshape(2, 128, D)
    return [wt["ffn_g"][l], wt["ffn_u"][l], wt["ffn_d"][l], inn, out]


def layer_weights(l, q):
    m = _even_in_map() if l == 0 else _odd_in_map()
    src = np.where(m >= 0, (m // IN_SRC[l]) * IN_ROWS + m % IN_SRC[l], -1)
    w_out = q[4].reshape(NSH * 256, D)
    return {"ffn_g": q[0], "ffn_u": q[1], "ffn_d": q[2], "w_in": _take_pad(q[3].reshape(NSH * IN_ROWS, D), src, 0),
            "w_out": _take_pad(w_out, _even_out_map(), 0) if l == 0 else w_out}


def layer_grad_items(l, dwg, dwu, dwd, g_in, g_out):
    m = _even_in_map() if l == 0 else _odd_in_map()
    gi = jnp.take(g_in, jnp.asarray(_inverse(m, NSH * IN_SRC[l])), axis=0).reshape(NSH, IN_SRC[l], D)
    gi = _bf(jnp.pad(gi, ((0, 0), (0, IN_ROWS - IN_SRC[l]), (0, 0)))).reshape(NSH, 2, IN_ROWS // 2, D)
    if l == 0:
        g_out = jnp.take(g_out, jnp.asarray(_inverse(_even_out_map(), 1024)), axis=0)
    go = _bf(g_out).reshape(NSH, 2, 128, D)
    return [[dwg[l][j], dwu[l][j], dwd[l][j], gi[:, j], go[:, j]] for j in range(2)]


def assemble_layer(l, r0, r1):
    return {"ffn_g": jnp.stack([r0[0], r1[0]]), "ffn_u": jnp.stack([r0[1], r1[1]]), "ffn_d": jnp.stack([r0[2], r1[2]]),
            "in": jnp.concatenate([r0[3], r1[3]])[:IN_SRC[l]], "out": jnp.concatenate([r0[4], r1[4]])}


def big_grads(l0, l1):
    st = lambda n: jnp.stack([l0[n], l1[n]])
    return {"ffn_g": st("ffn_g"), "ffn_u": st("ffn_u"), "ffn_d": st("ffn_d"), "e_in": l0["in"], "e_out": l0["out"],
            "o_in": l1["in"], "o_out": l1["out"]}


def small_from_gathered(gs):
    sm = [unpack_small(gs[s]) for s in range(NSH)]
    full = {}
    full["meta"] = jnp.concatenate([sm[s]["meta"] for s in range(NSH)], axis=1)
    full["norm"] = jnp.concatenate([sm[s]["norm"] for s in range(NSH)], axis=2)
    full["conv"] = jnp.concatenate([sm[s]["conv"][0] for s in range(NSH)], axis=1)
    full["gate_up"] = jnp.concatenate([sm[s]["gate_up"][0] for s in range(NSH)], axis=1)
    full["b_gate"] = jnp.concatenate([sm[s]["b_gate"] for s in range(NSH)], axis=1)
    full["gla_norm"] = jnp.concatenate([sm[s]["gla_norm"] for s in range(NSH)], axis=1)
    return full


def _col_sh(w):
    return jnp.moveaxis(w.reshape(w.shape[0], NSH, w.shape[1] // NSH), 1, 0)


def _to_t(n, a):
    if n in ("ffn_g", "ffn_u"):
        return jnp.swapaxes(a, 2, 3)
    if n in ("e_in", "o_in"):
        return jnp.swapaxes(a[0], 0, 1)
    return a if n == "ffn_d" else a[0]


def _from_t(n, a):
    if n in ("ffn_g", "ffn_u"):
        return jnp.swapaxes(a, 2, 3)
    if n in ("e_in", "o_in"):
        return jnp.swapaxes(a, 0, 1)[None]
    return a if n == "ffn_d" else a[None]


def pack_small_grads(G):
    col_sh = _col_sh
    norm_sh = jnp.moveaxis(G["norm"].reshape(2, 6, NSH, 256), 2, 0)
    a = jnp.concatenate([col_sh(G["meta"]).reshape(NSH, -1), norm_sh.reshape(NSH, -1), col_sh(G["conv"]).reshape(NSH, -1),
                         col_sh(G["gate_up"]).reshape(NSH, -1), G["b_gate"].reshape(NSH, -1),
                         G["gla_norm"].reshape(NSH, -1)], axis=1)
    b = jnp.concatenate([G[n].reshape(-1) for n, _ in REPL])
    b = jnp.broadcast_to(b[None], (NSH, b.shape[0]))
    small = jnp.concatenate([a, jnp.zeros((NSH, SMALL_REP - a.shape[1]), F32), b,
                             jnp.zeros((NSH, SMALL_ROWS * LANE - SMALL_REP - b.shape[1]), F32)], axis=1)
    return small.reshape(NSH, SMALL_ROWS, LANE)


MESH = pl.DeviceIdType.MESH
ANY = pl.BlockSpec(memory_space=pl.ANY)
VMEM = pl.BlockSpec(memory_space=pltpu.VMEM)


def _place():
    return lax.axis_index("x"), lax.axis_index("y"), lax.axis_index("c")


def _other_chips(x, y):
    return [(1 - x, y), (x, 1 - y), (1 - x, 1 - y)]


def _rcopy(send_sems, recv_sems, k, src, dst, to):
    return pltpu.make_async_remote_copy(src_ref=src, dst_ref=dst, send_sem=send_sems.at[k], recv_sem=recv_sems.at[k],
                                        device_id=to, device_id_type=MESH)


def _gather_steps(in_refs, out_refs, send_sems, recv_sems):
    n = len(in_refs)
    x, y, c = _place()
    s = 2 * x + y
    chips = _other_chips(x, y)
    copy = functools.partial(_rcopy, send_sems, recv_sems)
    pairs = [(i, j, cx, cy) for i in range(n) for j, (cx, cy) in enumerate(chips)]
    pushes = lambda: [copy(i * 3 + j, in_refs[i].at[c], out_refs[i].at[s, c], (cx, cy, c)) for i, j, cx, cy in pairs]
    landed = lambda i, cx, cy, half: out_refs[i].at[2 * cx + cy, half]
    relays = lambda: [copy(3 * n + i * 3 + j, landed(i, cx, cy, c), landed(i, cx, cy, c), (x, y, 1 - c)) for i, j, cx, cy in pairs]

    def start():
        for cp in pushes():
            cp.start()

    def relay():
        for i, j, cx, cy in pairs:
            copy(i * 3 + j, landed(i, cx, cy, c), landed(i, cx, cy, c), (x, y, c)).wait_recv()
        for cp in relays():
            cp.start()

    def finish():
        for i, j, cx, cy in pairs:
            copy(3 * n + i * 3 + j, landed(i, cx, cy, 1 - c), landed(i, cx, cy, 1 - c), (x, y, c)).wait_recv()
        for cp in pushes() + relays():
            cp.wait_send()

    return start, relay, finish


def _gather_shapes(pieces):
    return [jax.ShapeDtypeStruct((NSH,) + a.shape, a.dtype) for a in pieces]


def ag_layer(name, pieces):
    n = len(pieces)

    def body(*refs):
        for fn in _gather_steps(refs[:n], refs[n:2 * n], *refs[2 * n:]):
            fn()

    return pl.pallas_call(
        body, name=name, in_specs=[ANY] * n, out_specs=[ANY] * n, out_shape=_gather_shapes(pieces),
        scratch_shapes=[pltpu.SemaphoreType.DMA((6 * n,)), pltpu.SemaphoreType.DMA((6 * n,))],
    )(*pieces)


def gather_side(pieces):
    def events(n_steps, in_refs, out_refs, send_sems, recv_sems):
        start, relay, finish = _gather_steps(in_refs, out_refs, send_sems, recv_sems)
        return [(0, start), (max(3 * n_steps // 4, 1), relay), (n_steps - 1, finish)]

    return Side(pieces, _gather_shapes(pieces), 6 * len(pieces), events)


def ag_small(pack):
    def body(x_ref, out_ref, send_sems, recv_sems):
        x, y, c = _place()
        s = 2 * x + y
        chips = _other_chips(x, y)

        def copy(k, src, dst, to):
            return pltpu.make_async_remote_copy(src_ref=src, dst_ref=dst, send_sem=send_sems.at[k], recv_sem=recv_sems.at[k],
                                                device_id=to, device_id_type=MESH)

        out_ref[s] = x_ref[...]
        sends = [copy(j, x_ref, out_ref.at[s], (cx, cy, c)) for j, (cx, cy) in enumerate(chips)]
        for cp in sends:
            cp.start()
        for j, (cx, cy) in enumerate(chips):
            blk = out_ref.at[2 * cx + cy]
            copy(j, blk, blk, (x, y, c)).wait_recv()
        for cp in sends:
            cp.wait_send()

    return pl.pallas_call(
        body, name="ag_small", in_specs=[VMEM], out_specs=VMEM,
        out_shape=jax.ShapeDtypeStruct((NSH,) + pack.shape, pack.dtype),
        scratch_shapes=[pltpu.SemaphoreType.DMA((3,)), pltpu.SemaphoreType.DMA((3,))],
    )(pack)


def rs_pair(name, items):
    ni = len(items[0])

    def body(*refs):
        in_refs = [refs[:ni], refs[ni:2 * ni]]
        recv_refs = refs[2 * ni:3 * ni]
        send_sems, recv_sems = refs[3 * ni:]
        x, y, c = _place()
        copy = functools.partial(_rcopy, send_sems, recv_sems)
        for cc in range(2):
            @pl.when(c == cc)
            def _():
                cps = [copy(i * NSH + s, in_refs[1 - cc][i].at[s], recv_refs[i].at[s], (x, y, 1 - c))
                       for i in range(ni) for s in range(NSH)]
                for cp in cps:
                    cp.start()
                for cp in cps:
                    cp.wait()

    return pl.pallas_call(
        body, name=name, in_specs=[ANY] * (2 * ni), out_specs=[ANY] * ni,
        out_shape=[jax.ShapeDtypeStruct(a.shape, a.dtype) for a in items[0]],
        scratch_shapes=[pltpu.SemaphoreType.DMA((ni * NSH,)), pltpu.SemaphoreType.DMA((ni * NSH,))],
    )(*items[0], *items[1])


def _scatter_steps(a_refs, out_refs, send_sems, recv_sems):
    n = len(a_refs)
    x, y, c = _place()
    s = 2 * x + y
    chips = _other_chips(x, y)
    copy = functools.partial(_rcopy, send_sems, recv_sems)
    pairs = [(i, j, cx, cy) for i in range(n) for j, (cx, cy) in enumerate(chips)]
    sends = lambda: [copy(i * 3 + j, a_refs[i].at[2 * cx + cy], out_refs[i].at[s], (cx, cy, c)) for i, j, cx, cy in pairs]

    def start():
        for cp in sends():
            cp.start()

    def finish():
        for i, j, cx, cy in pairs:
            blk = out_refs[i].at[2 * cx + cy]
            copy(i * 3 + j, blk, blk, (x, y, c)).wait_recv()
        for cp in sends():
            cp.wait_send()

    return start, finish


def rs_chips(name, arrs):
    n = len(arrs)

    def body(*refs):
        for fn in _scatter_steps(refs[:n], refs[n:2 * n], *refs[2 * n:]):
            fn()

    return pl.pallas_call(
        body, name=name, in_specs=[ANY] * n, out_specs=[ANY] * n,
        out_shape=[jax.ShapeDtypeStruct(a.shape, a.dtype) for a in arrs],
        scratch_shapes=[pltpu.SemaphoreType.DMA((3 * n,)), pltpu.SemaphoreType.DMA((3 * n,))],
    )(*arrs)


def scatter_side(arrs):
    def events(n_steps, in_refs, out_refs, send_sems, recv_sems):
        start, finish = _scatter_steps(in_refs, out_refs, send_sems, recv_sems)
        return [(0, start), (n_steps - 1, finish)]

    return Side(arrs, [jax.ShapeDtypeStruct(a.shape, a.dtype) for a in arrs], 3 * len(arrs), events)


def _pair_chunks(rows):
    return 4 if rows % 32 == 0 else (2 if rows % 16 == 0 else 1)


def ag_pair(name, arrs):
    n = len(arrs)
    chunks = [(i, k * (a.shape[0] // _pair_chunks(a.shape[0])), a.shape[0] // _pair_chunks(a.shape[0]))
              for i, a in enumerate(arrs) for k in range(_pair_chunks(a.shape[0]))]

    def body(*refs):
        g_refs, out_refs = refs[:n], refs[n:2 * n]
        send_sems, recv_sems = refs[2 * n:]
        x, y, c = _place()
        give = [_rcopy(send_sems, recv_sems, q, g_refs[i].at[pl.ds(r0, rc)], out_refs[i].at[pl.ds(r0, rc)], (x, y, 1 - c))
                for q, (i, r0, rc) in enumerate(chunks)]
        for cp in give:
            cp.start()
        for cp in give:
            cp.wait()

    return pl.pallas_call(
        body, name=name, in_specs=[ANY] * n, out_specs=[ANY] * n,
        out_shape=[jax.ShapeDtypeStruct(a.shape, a.dtype) for a in arrs],
        scratch_shapes=[pltpu.SemaphoreType.DMA((len(chunks),)), pltpu.SemaphoreType.DMA((len(chunks),))],
    )(*arrs)


def small_allreduce(p):
    def body(p_ref, out_ref, rbuf, send_sems, recv_sems):
        x, y, c = _place()
        me = 4 * x + 2 * y + c
        rbuf[me] = p_ref[2 * x + y]
        flip = lambda v, f: (1 - v) if f else v
        peers = [(flip(x, k >> 2 & 1), flip(y, k >> 1 & 1), flip(c, k & 1)) for k in range(1, 8)]

        def copy(k, src, dst, to):
            return pltpu.make_async_remote_copy(src_ref=src, dst_ref=dst, send_sem=send_sems.at[k], recv_sem=recv_sems.at[k],
                                                device_id=to, device_id_type=MESH)

        sends = [copy(k, p_ref.at[2 * px + py], rbuf.at[me], (px, py, pc)) for k, (px, py, pc) in enumerate(peers)]
        for cp in sends:
            cp.start()
        for k, (px, py, pc) in enumerate(peers):
            blk = rbuf.at[4 * px + 2 * py + pc]
            copy(k, blk, blk, (x, y, c)).wait_recv()
        for cp in sends:
            cp.wait_send()
        acc = rbuf[0]
        for d in range(1, 8):
            acc = acc + rbuf[d]
        out_ref[...] = acc

    return pl.pallas_call(
        body, name="small_allreduce", in_specs=[VMEM], out_specs=VMEM,
        out_shape=jax.ShapeDtypeStruct(p.shape[1:], F32),
        scratch_shapes=[pltpu.VMEM((8,) + p.shape[1:], F32), pltpu.SemaphoreType.DMA((7,)), pltpu.SemaphoreType.DMA((7,))],
    )(p)


def _rows_tile(rows, cap):
    return _pick(rows, cap) if rows % 128 == 0 else rows


def sum_pair(name, a0, a1, recv, cflag):
    n, r, d = recv.shape
    tr = _pick(r, 1024) if r % 64 == 0 else r

    def body(c_ref, a0_ref, a1_ref, b_ref, o_ref):
        own = jnp.where(c_ref[0] == 0, a0_ref[...].astype(F32), a1_ref[...].astype(F32))
        o_ref[...] = (own + b_ref[...].astype(F32)).astype(o_ref.dtype)

    spec = pl.BlockSpec((None, tr, d), lambda s, i: (s, i, 0))
    return pl.pallas_call(
        body, name=name, grid=(n, r // tr), in_specs=[pl.BlockSpec(memory_space=pltpu.SMEM), spec, spec, spec],
        out_specs=spec, out_shape=jax.ShapeDtypeStruct(recv.shape, BF16), compiler_params=_cparams(("parallel", "parallel")),
    )(cflag, a0, a1, recv)


def sum_chips(name, parts, own, sflag):
    n, r, d = parts.shape
    tr = _pick(r, 1024) if r % 64 == 0 else r

    def body(s_ref, p_ref, a_ref, o_ref):
        acc = None
        for t in range(n):
            term = jnp.where(s_ref[0] == t, a_ref[t].astype(F32), p_ref[t].astype(F32))
            acc = term if acc is None else acc + term
        o_ref[...] = acc

    spec = pl.BlockSpec((n, tr, d), lambda i: (0, i, 0))
    return pl.pallas_call(
        body, name=name, grid=(r // tr,), in_specs=[pl.BlockSpec(memory_space=pltpu.SMEM), spec, spec],
        out_specs=pl.BlockSpec((tr, d), lambda i: (i, 0)), out_shape=jax.ShapeDtypeStruct((r, d), F32),
        compiler_params=_cparams(("parallel",)),
    )(sflag, parts, own)


ADAM_LR, ADAM_B1, ADAM_B2, ADAM_EPS, ADAM_WD, ADAM_STEP = 0.001, 0.9, 0.999, 1e-08, 0.01, 10


def adamw_call(name, w, g, m, v):
    rows, cols = w.shape
    tr = _rows_tile(rows, 512)

    def body(w_ref, g_ref, m_ref, v_ref, d_ref, nm_ref, nv_ref):
        g_ = g_ref[...]
        m_ = ADAM_B1 * m_ref[...] + (1.0 - ADAM_B1) * g_
        v_ = ADAM_B2 * v_ref[...] + (1.0 - ADAM_B2) * (g_ * g_)
        m_hat = m_ / (1.0 - ADAM_B1 ** ADAM_STEP)
        v_hat = v_ / (1.0 - ADAM_B2 ** ADAM_STEP)
        d_ref[...] = -ADAM_LR * (m_hat / (jnp.sqrt(v_hat) + ADAM_EPS) + ADAM_WD * w_ref[...])
        nm_ref[...] = m_
        nv_ref[...] = v_

    spec = pl.BlockSpec((tr, cols), lambda i: (i, 0))
    sh = jax.ShapeDtypeStruct((rows, cols), F32)
    return pl.pallas_call(
        body, name=name, grid=(rows // tr,), in_specs=[spec] * 4, out_specs=[spec] * 3, out_shape=[sh] * 3,
        compiler_params=_cparams(("parallel",)),
    )(w, g, m, v)


def kernel(x, meta_tokens, norm_w, ffn_w_gate, ffn_w_up, ffn_w_down, rel_bias_table, even_w_in, even_conv_w, swa_sinks, dn_a_log, dn_dt_bias, dn_norm_w, even_w_out, odd_w_in, gla_w_gate_up, gla_b_gate, gla_norm_w, odd_w_out, loss_target, m_meta_tokens, m_norm_w, m_ffn_w_gate, m_ffn_w_up, m_ffn_w_down, m_rel_bias_table, m_even_w_in, m_even_conv_w, m_swa_sinks, m_dn_a_log, m_dn_dt_bias, m_dn_norm_w, m_even_w_out, m_odd_w_in, m_gla_w_gate_up, m_gla_b_gate, m_gla_norm_w, m_odd_w_out, v_meta_tokens, v_norm_w, v_ffn_w_gate, v_ffn_w_up, v_ffn_w_down, v_rel_bias_table, v_even_w_in, v_even_conv_w, v_swa_sinks, v_dn_a_log, v_dn_dt_bias, v_dn_norm_w, v_even_w_out, v_odd_w_in, v_gla_w_gate_up, v_gla_b_gate, v_gla_norm_w, v_odd_w_out):
    ws = [meta_tokens, norm_w, ffn_w_gate, ffn_w_up, ffn_w_down, rel_bias_table, even_w_in, even_conv_w, swa_sinks, dn_a_log,
          dn_dt_bias, dn_norm_w, even_w_out, odd_w_in, gla_w_gate_up, gla_b_gate, gla_norm_w, odd_w_out]
    ms = [m_meta_tokens, m_norm_w, m_ffn_w_gate, m_ffn_w_up, m_ffn_w_down, m_rel_bias_table, m_even_w_in, m_even_conv_w,
          m_swa_sinks, m_dn_a_log, m_dn_dt_bias, m_dn_norm_w, m_even_w_out, m_odd_w_in, m_gla_w_gate_up, m_gla_b_gate,
          m_gla_norm_w, m_odd_w_out]
    vs = [v_meta_tokens, v_norm_w, v_ffn_w_gate, v_ffn_w_up, v_ffn_w_down, v_rel_bias_table, v_even_w_in, v_even_conv_w,
          v_swa_sinks, v_dn_a_log, v_dn_dt_bias, v_dn_norm_w, v_even_w_out, v_odd_w_in, v_gla_w_gate_up, v_gla_b_gate,
          v_gla_norm_w, v_odd_w_out]
    short = [n for n, _ in NAMES]
    w = dict(zip(short, ws))
    m = dict(zip(short, ms))
    v = dict(zip(short, vs))

    wt = {n: _to_t(n, w[n]) for n in BIG}
    own = {n: wt[n].astype(BF16) for n in BIG}
    sflag = (2 * lax.axis_index("x") + lax.axis_index("y")).astype(jnp.int32).reshape(1)
    cflag = lax.axis_index("c").astype(jnp.int32).reshape(1)
    is0 = cflag[0] == 0
    fill = lambda got, pieces: [lax.dynamic_update_index_in_dim(g_, p_, sflag[0], 0) for g_, p_ in zip(got, pieces)]
    pieces = [weight_pieces(own, l) for l in range(2)]
    small = small_from_gathered(ag_small(pack_small(w)))
    W = {**small, **{n: w[n] for n, _ in REPL},
         "layers": [layer_weights(0, fill(ag_layer("ag_layer_0", pieces[0]), pieces[0])), None]}

    def reduce_start(l, items):
        recv = rs_pair(f"rs_pair_{l}", items)
        return [sum_pair(f"sum_pair_{l}{i}", items[0][i], items[1][i], recv[i], cflag) for i in range(len(recv))]

    def reduce_finish(l, mine, parts):
        red = [sum_chips(f"sum_chips_{l}{i}", p, a, sflag) for i, (p, a) in enumerate(zip(parts, mine))]
        got = ag_pair(f"ag_pair_{l}", red)
        return [jnp.where(is0, r_, g_) for r_, g_ in zip(red, got)], [jnp.where(is0, g_, r_) for r_, g_ in zip(red, got)]

    class Layer1Exchange:
        gather_split = ([0, 3], [1, 2, 4])
        scatter_split = ([1, 2], [0, 3, 4])

        def gather_side(self, part):
            return gather_side([pieces[1][i] for i in self.gather_split[part]])

        def gather_done(self, *got):
            q = [None] * len(pieces[1])
            for idx, arrs in zip(self.gather_split, got):
                for i, a in zip(idx, arrs):
                    q[i] = a
            return layer_weights(1, fill(q, pieces[1]))

        def scatter_sides(self, items):
            self.mine = reduce_start(1, items)
            return [scatter_side([self.mine[i] for i in idx]) for idx in self.scatter_split]

        def scatter_done(self, *got):
            self.parts = [None] * len(self.mine)
            for idx, arrs in zip(self.scatter_split, got):
                for i, a in zip(idx, arrs):
                    self.parts[i] = a

    ex = Layer1Exchange()
    loss_blk, gx, G = core_step(x[0], loss_target[0], W, comm=ex)

    mine0 = reduce_start(0, G["items"][0])
    lay0 = assemble_layer(0, *reduce_finish(0, mine0, rs_chips("rs_chips_0", mine0)))
    lay1 = assemble_layer(1, *reduce_finish(1, ex.mine, ex.parts))
    gt = big_grads(lay0, lay1)
    g_small_pack = small_allreduce(pack_small_grads(G))
    g = {**{n: _from_t(n, gt[n]) for n in BIG}, **unpack_small(g_small_pack)}

    delta, new_m, new_v = {}, {}, {}
    for n in BIG:
        shp = wt[n].shape
        two = lambda t: t.reshape(-1, D)
        d_, m_, v_ = adamw_call("adamw_" + n, two(wt[n]), two(gt[n]), two(_to_t(n, m[n])), two(_to_t(n, v[n])))
        delta[n], new_m[n], new_v[n] = (_from_t(n, t.reshape(shp)) for t in (d_, m_, v_))
    d_, m_, v_ = adamw_call("adamw_small", pack_small(w), g_small_pack, pack_small(m), pack_small(v))
    delta.update(unpack_small(d_))
    new_m.update(unpack_small(m_))
    new_v.update(unpack_small(v_))

    loss = lax.psum(loss_blk[0, 0], ("x", "y", "c"))
    return (loss, gx[None], *[g[n] for n in short], *[delta[n] for n in short], *[new_m[n] for n in short],
            *[new_v[n] for n in short])
```

```python
import functools
import math

import numpy as np
import jax
import jax.numpy as jnp
from jax import lax
from jax.experimental import pallas as pl
from jax.experimental.pallas import tpu as pltpu

F32 = jnp.float32
BF16 = jnp.bfloat16
HI = lax.Precision.HIGHEST

D = 1024
N_META = 16
PADR = 128
ZROWS = PADR - N_META
D_FF = 2816
NSH = 4
FSH = D_FF // NSH
EPS = 1e-6
NEG = -1e30
CH = 64
CPS = 2
BLK = 128
LANE = 128
VMEM_LIMIT = 56 * 1024 * 1024
FFN_SUB = 4

E_QA, E_KA, E_VA, E_QB, E_KB, E_VB, E_ZB, E_BA, E_END = 0, 1024, 1280, 1536, 2048, 2560, 3072, 3584, 4096


def _even_in_map():
    m = np.full((E_END,), -1, np.int64)
    for h in range(8):
        m[E_QA + h * 128:E_QA + h * 128 + 64] = np.arange(h * 64, (h + 1) * 64)
    for h in range(2):
        m[E_KA + h * 128:E_KA + h * 128 + 64] = 512 + np.arange(h * 64, (h + 1) * 64)
        m[E_VA + h * 128:E_VA + h * 128 + 64] = 640 + np.arange(h * 64, (h + 1) * 64)
    m[E_QB:E_QB + 2048] = 768 + np.arange(2048)
    m[E_BA:E_BA + 8] = 2816 + np.arange(8)
    return m


def _even_out_map():
    m = np.full((1536,), -1, np.int64)
    for h in range(8):
        m[h * 128:h * 128 + 64] = np.arange(h * 64, (h + 1) * 64)
    m[1024:1536] = 512 + np.arange(512)
    return m


O_Q, O_K, O_V, O_G, O_GK, O_END = 0, 512, 1024, 2048, 3072, 3584


def _odd_in_map():
    m = np.full((O_END,), -1, np.int64)
    m[:3072] = np.arange(3072)
    m[O_GK:O_GK + 16] = 3072 + np.arange(16)
    return m


def _inverse(m, n):
    inv = np.zeros((n,), np.int64)
    for p, o in enumerate(m):
        if o >= 0:
            inv[o] = p
    return inv


def _take_pad(w, m, axis):
    t = jnp.take(w, jnp.asarray(np.maximum(m, 0)), axis=axis)
    shape = [1] * w.ndim
    shape[axis] = m.shape[0]
    return jnp.where(jnp.asarray(m >= 0).reshape(shape), t, jnp.zeros((), w.dtype))


def _mm(a, b, prec=HI):
    return lax.dot_general(a, b, (((1,), (0,)), ((), ())), precision=prec, preferred_element_type=F32)


def _mm_nt(a, b, prec=HI):
    return lax.dot_general(a, b, (((1,), (1,)), ((), ())), precision=prec, preferred_element_type=F32)


def _mm_tn(a, b, prec=HI):
    return lax.dot_general(a, b, (((0,), (0,)), ((), ())), precision=prec, preferred_element_type=F32)


def _bdot(a, b, dims):
    return lax.dot_general(a.astype(BF16), b.astype(BF16), (dims, ((), ())), preferred_element_type=F32)


@jax.custom_vjp
def _bmm(a, b):
    return _bdot(a, b, ((1,), (0,)))


@jax.custom_vjp
def _bmm_nt(a, b):
    return _bdot(a, b, ((1,), (1,)))


@jax.custom_vjp
def _bmm_tn(a, b):
    return _bdot(a, b, ((0,), (0,)))


_bmm.defvjp(lambda a, b: (_bmm(a, b), (a, b)), lambda r, g: (_bmm_nt(g, r[1]), _bmm_tn(r[0], g)))
_bmm_nt.defvjp(lambda a, b: (_bmm_nt(a, b), (a, b)), lambda r, g: (_bmm(g, r[1]), _bmm_tn(g, r[0])))
_bmm_tn.defvjp(lambda a, b: (_bmm_tn(a, b), (a, b)), lambda r, g: (_bmm_nt(r[1], g), _bmm(r[0], g)))


def _hi_lo(x):
    h = x.astype(BF16)
    return h, (x - h.astype(F32)).astype(BF16)


def _xdot(a, b, dims):
    ah, al = _hi_lo(a)
    bh, bl = _hi_lo(b)
    d = lambda p, q: lax.dot_general(p, q, (dims, ((), ())), preferred_element_type=F32)
    return d(ah, bh) + (d(ah, bl) + d(al, bh))


@jax.custom_vjp
def _xmm(a, b):
    return _xdot(a, b, ((1,), (0,)))


@jax.custom_vjp
def _xmm_nt(a, b):
    return _xdot(a, b, ((1,), (1,)))


@jax.custom_vjp
def _xmm_tn(a, b):
    return _xdot(a, b, ((0,), (0,)))


_xmm.defvjp(lambda a, b: (_xmm(a, b), (a, b)), lambda r, g: (_xmm_nt(g, r[1]), _xmm_tn(r[0], g)))
_xmm_nt.defvjp(lambda a, b: (_xmm_nt(a, b), (a, b)), lambda r, g: (_xmm(g, r[1]), _xmm_tn(g, r[0])))
_xmm_tn.defvjp(lambda a, b: (_xmm_tn(a, b), (a, b)), lambda r, g: (_xmm_nt(r[1], g), _xmm(r[0], g)))


def _sum01(m01, x, dims):
    h, l = _hi_lo(x)
    l2 = (x - h.astype(F32) - l.astype(F32)).astype(BF16)
    m = m01.astype(BF16)
    d = lambda q: lax.dot_general(m, q, (dims, ((), ())), preferred_element_type=F32)
    return d(h) + (d(l) + d(l2))


@jax.custom_vjp
def _cumsum_rows(x):
    n = x.shape[0]
    tri = lax.broadcasted_iota(jnp.int32, (n, n), 0) >= lax.broadcasted_iota(jnp.int32, (n, n), 1)
    return _sum01(tri, x, ((1,), (0,)))


def _cumsum_rows_b(_, g):
    n = g.shape[0]
    tri = lax.broadcasted_iota(jnp.int32, (n, n), 0) >= lax.broadcasted_iota(jnp.int32, (n, n), 1)
    return (_sum01(tri, g, ((0,), (0,))),)


_cumsum_rows.defvjp(lambda x: (_cumsum_rows(x), None), _cumsum_rows_b)


@functools.partial(jax.custom_vjp, nondiff_argnums=(1,))
def _colsum_as_rows(x, width):
    return _colsum_impl(x, width)


def _colsum_impl(x, width):
    h, l = _hi_lo(x)
    l2 = (x - h.astype(F32) - l.astype(F32)).astype(BF16)
    ones = jnp.ones((x.shape[0], width), BF16)
    d = lambda q: lax.dot_general(q, ones, (((0,), (0,)), ((), ())), preferred_element_type=F32)
    return d(h) + (d(l) + d(l2))


def _colsum_as_rows_f(x, width):
    return _colsum_impl(x, width), x.shape[0]


def _colsum_as_rows_b(width, n, g):
    return (_sum01(jnp.ones((n, width), F32), g, ((1,), (1,))),)


_colsum_as_rows.defvjp(_colsum_as_rows_f, _colsum_as_rows_b)


def _rms(x, w):
    return x * lax.rsqrt(jnp.mean(x * x, axis=-1, keepdims=True) + EPS) * w


def _sigmoid(x):
    return 1.0 / (1.0 + jnp.exp(-x))


def _silu(x):
    return x * _sigmoid(x)


def _softplus(x):
    return jnp.maximum(x, 0.0) + jnp.log(1.0 + jnp.exp(-jnp.abs(x)))


def _lane_pick(row, idx):
    lane = lax.broadcasted_iota(jnp.int32, row.shape, row.ndim - 1)
    return jnp.sum(jnp.where(lane == idx, row, 0.0), axis=-1, keepdims=True)


def _row_ids(row0, n):
    return row0 + lax.broadcasted_iota(jnp.int32, (n, 1), 0)


def _pick(m, cap):
    best = 64
    for t in range(64, min(m, cap) + 1, 64):
        if m % t == 0:
            best = t
    return best


def _cparams(sem):
    return pltpu.CompilerParams(dimension_semantics=sem, vmem_limit_bytes=VMEM_LIMIT)


def mm_nn(a, b, name, out_dtype=F32):
    M, K = a.shape
    N = b.shape[1]
    tm = _pick(M, 1408 if K <= 2048 else 704)
    tn = _pick(N, 512)

    def body(a_ref, b_ref, o_ref):
        o_ref[...] = _mm(a_ref[...], b_ref[...], None).astype(o_ref.dtype)

    return pl.pallas_call(
        body, name=name, grid=(N // tn, M // tm),
        in_specs=[pl.BlockSpec((tm, K), lambda j, i: (i, 0)), pl.BlockSpec((K, tn), lambda j, i: (0, j))],
        out_specs=pl.BlockSpec((tm, tn), lambda j, i: (i, j)),
        out_shape=jax.ShapeDtypeStruct((M, N), out_dtype),
        compiler_params=_cparams(("parallel", "parallel")),
    )(a, b)


def mm_nt(a, b, name, out_dtype=F32):
    M, K = a.shape
    N = b.shape[0]
    tm = _pick(M, 768)
    tn = _pick(N, 512)

    def body(a_ref, b_ref, o_ref):
        o_ref[...] = _mm_nt(a_ref[...], b_ref[...], None).astype(o_ref.dtype)

    return pl.pallas_call(
        body, name=name, grid=(N // tn, M // tm),
        in_specs=[pl.BlockSpec((tm, K), lambda j, i: (i, 0)), pl.BlockSpec((tn, K), lambda j, i: (j, 0))],
        out_specs=pl.BlockSpec((tm, tn), lambda j, i: (i, j)),
        out_shape=jax.ShapeDtypeStruct((M, N), out_dtype),
        compiler_params=_cparams(("parallel", "parallel")),
    )(a, b)


def mm_tn(a, b, name):
    M, K = a.shape
    N = b.shape[1]
    tk = _pick(K, 512)
    tn = _pick(N, 512)

    def body(a_ref, b_ref, o_ref):
        o_ref[...] = _mm_tn(a_ref[...], b_ref[...], None)

    return pl.pallas_call(
        body, name=name, grid=(K // tk, N // tn),
        in_specs=[pl.BlockSpec((M, tk), lambda i, j: (0, i)), pl.BlockSpec((M, tn), lambda i, j: (0, j))],
        out_specs=pl.BlockSpec((tk, tn), lambda i, j: (i, j)),
        out_shape=jax.ShapeDtypeStruct((K, N), F32),
        compiler_params=_cparams(("parallel", "parallel")),
    )(a, b)


def _row_specs(rows, tm):
    return [pl.BlockSpec((tm, w), functools.partial(lambda i, cb: (i, cb), cb=cb)) for (_, w, cb) in rows]


def _param_specs(params):
    return [pl.BlockSpec(p.shape, functools.partial(lambda i, nd: (0,) * nd, nd=p.ndim)) for p in params]


def rowwise_fwd(name, fn, rows, params, outs, tm=None):
    M = rows[0][0].shape[0]
    tm = tm or _pick(M, 704)
    nr, npar = len(rows), len(params)

    def body(*refs):
        row0 = pl.program_id(0) * tm
        vals = [r[...].astype(F32) for r in refs[:nr]] + [p[...] for p in refs[nr:nr + npar]]
        res = fn(row0, *vals)
        for o_ref, r in zip(refs[nr + npar:], res):
            o_ref[...] = r.astype(o_ref.dtype)

    return pl.pallas_call(
        body, name=name, grid=(M // tm,),
        in_specs=_row_specs(rows, tm) + _param_specs(params),
        out_specs=[pl.BlockSpec((tm, w), lambda i: (i, 0)) for (w, _) in outs],
        out_shape=[jax.ShapeDtypeStruct((M, w), dt) for (w, dt) in outs],
        compiler_params=_cparams(("parallel",)),
    )(*[r[0] for r in rows], *params)


def rowwise_bwd(name, fn, rows, params, douts, drow_dtypes, tm=None):
    M = rows[0][0].shape[0]
    tm = tm or _pick(M, 704)
    nr, npar, nd = len(rows), len(params), len(douts)
    want = [k for k, dt in enumerate(drow_dtypes) if dt is not None]

    def body(*refs):
        i = pl.program_id(0)
        row0 = i * tm
        vals = [r[...].astype(F32) for r in refs[:nr]] + [p[...] for p in refs[nr:nr + npar]]
        cots = tuple(d[...].astype(F32) for d in refs[nr + npar:nr + npar + nd])
        _, vjp = jax.vjp(functools.partial(fn, row0), *vals)
        grads = vjp(cots)
        o_refs = refs[nr + npar + nd:]
        for o_ref, k in zip(o_refs[:len(want)], want):
            o_ref[...] = grads[k].astype(o_ref.dtype)
        for o_ref, g in zip(o_refs[len(want):], grads[nr:]):
            @pl.when(i == 0)
            def _():
                o_ref[...] = g

            @pl.when(i > 0)
            def _():
                o_ref[...] += g

    res = pl.pallas_call(
        body, name=name, grid=(M // tm,),
        in_specs=_row_specs(rows, tm) + _param_specs(params) + _row_specs(douts, tm),
        out_specs=[pl.BlockSpec((tm, rows[k][1]), lambda i: (i, 0)) for k in want] + _param_specs(params),
        out_shape=[jax.ShapeDtypeStruct((M, rows[k][1]), drow_dtypes[k]) for k in want]
        + [jax.ShapeDtypeStruct(p.shape, F32) for p in params],
        compiler_params=_cparams(("arbitrary",)),
    )(*[r[0] for r in rows], *params, *[d[0] for d in douts])
    return res[:len(want)], res[len(want):]


def _fn_prenorm(row0, h, wpre):
    return (_rms(h, wpre),)


def _fn_resnorm(scale, row0, h, f, wpost, wpre):
    h2 = h + scale * _rms(f, wpost)
    return h2, _rms(h2, wpre)


def _fn_res_last(scale, row0, h, f, wpost):
    return (h + scale * _rms(f, wpost),)


def ffn_fwd(name, xn, wg, wu, wd, j, side=None):
    M = xn.shape[0]
    tm = _pick(M, 704)
    s_ins, s_specs, s_shapes, s_sems = _side_parts(side)
    ns, nso = len(s_ins), len(s_shapes)

    def body(x_ref, wg_ref, wu_ref, wd_ref, *rest):
        f_ref, a_ref, b_ref = rest[ns:ns + 3]
        s = pl.program_id(1)
        _side_run(side, (M // tm) * NSH, rest[:ns], rest[ns + 3:ns + 3 + nso], rest[ns + 3 + nso:],
                  step=pl.program_id(0) * NSH + s)
        x = x_ref[...]
        a = _mm_nt(x, wg_ref[...], None)
        b = _mm_nt(x, wu_ref[...], None)
        c = _mm((_silu(a) * b).astype(BF16), wd_ref[...], None)

        @pl.when(s == 0)
        def _():
            f_ref[...] = c

        @pl.when(s > 0)
        def _():
            f_ref[...] += c

        a_ref[...] = a.astype(BF16)
        b_ref[...] = b.astype(BF16)

    wspec = wdspec = pl.BlockSpec((None, None, FSH, D), lambda i, s: (s, j, 0, 0))
    abspec = pl.BlockSpec((None, tm, FSH), lambda i, s: (s, i, 0))
    res = pl.pallas_call(
        body, name=name, grid=(M // tm, NSH),
        in_specs=[pl.BlockSpec((tm, D), lambda i, s: (i, 0)), wspec, wspec, wdspec] + s_specs,
        out_specs=[pl.BlockSpec((tm, D), lambda i, s: (i, 0)), abspec, abspec] + [pl.BlockSpec(memory_space=pl.ANY)] * nso,
        out_shape=[jax.ShapeDtypeStruct((M, D), F32), jax.ShapeDtypeStruct((NSH, M, FSH), BF16),
                   jax.ShapeDtypeStruct((NSH, M, FSH), BF16)] + s_shapes,
        scratch_shapes=s_sems,
        compiler_params=_cparams(("arbitrary", "arbitrary")),
    )(xn, wg, wu, wd, *s_ins)
    return res[:3], res[3:]


def ffn_bwd_x(name, df, a, b, wg, wu, wd, j, side=None):
    M = df.shape[0]
    tm = _pick(M, 704)
    ts = tm // FFN_SUB
    s_ins, s_specs, s_shapes, s_sems = _side_parts(side)
    ns, nso = len(s_ins), len(s_shapes)

    def body(df_ref, a_ref, b_ref, wg_ref, wu_ref, wd_ref, *rest):
        dx_ref, da_ref, db_ref, hm_ref = rest[ns:ns + 4]
        _side_run(side, (M // tm) * NSH, rest[:ns], rest[ns + 4:ns + 4 + nso], rest[ns + 4 + nso:],
                  step=pl.program_id(0) * NSH + pl.program_id(1))

        @pl.when(pl.program_id(1) == 0)
        def _():
            dx_ref[...] = jnp.zeros_like(dx_ref)

        for r in range(FFN_SUB):
            rows = pl.ds(r * ts, ts)
            a_ = a_ref[rows, :].astype(F32)
            b_ = b_ref[rows, :].astype(F32)
            dh = _mm_nt(df_ref[rows, :], wd_ref[...], None)
            sig = _sigmoid(a_)
            sil = a_ * sig
            da = (dh * b_ * (sig * (1.0 + a_ * (1.0 - sig)))).astype(BF16)
            db = (dh * sil).astype(BF16)
            dx_ref[rows, :] += _mm(da, wg_ref[...], None) + _mm(db, wu_ref[...], None)
            da_ref[rows, :] = da
            db_ref[rows, :] = db
            hm_ref[rows, :] = (sil * b_).astype(BF16)

    wspec = wdspec = pl.BlockSpec((None, None, FSH, D), lambda i, s: (s, j, 0, 0))
    abspec = pl.BlockSpec((None, tm, FSH), lambda i, s: (s, i, 0))
    ab = jax.ShapeDtypeStruct((NSH, M, FSH), BF16)
    res = pl.pallas_call(
        body, name=name, grid=(M // tm, NSH),
        in_specs=[pl.BlockSpec((tm, D), lambda i, s: (i, 0)), abspec, abspec, wspec, wspec, wdspec] + s_specs,
        out_specs=[pl.BlockSpec((tm, D), lambda i, s: (i, 0)), abspec, abspec, abspec] + [pl.BlockSpec(memory_space=pl.ANY)] * nso,
        out_shape=[jax.ShapeDtypeStruct((M, D), F32), ab, ab, ab] + s_shapes,
        scratch_shapes=s_sems,
        compiler_params=_cparams(("arbitrary", "arbitrary")),
    )(df, a, b, wg, wu, wd, *s_ins)
    return res[:4], res[4:]


def ffn_bwd_w(name, xn, df, da, db, hm):
    M = xn.shape[0]
    tm = _pick(M, 704)
    nt = M // tm

    def body(x_ref, df_ref, da_ref, db_ref, hm_ref, dwg_ref, dwu_ref, dwd_ref, ag, au, ad):
        i = pl.program_id(1)
        x = x_ref[...]
        g = _mm_tn(da_ref[...], x, None)
        u = _mm_tn(db_ref[...], x, None)
        d = _mm_tn(hm_ref[...], df_ref[...], None)

        @pl.when(i == 0)
        def _():
            ag[...] = g
            au[...] = u
            ad[...] = d

        @pl.when(i > 0)
        def _():
            ag[...] += g
            au[...] += u
            ad[...] += d

        @pl.when(i == nt - 1)
        def _():
            dwg_ref[...] = ag[...].astype(BF16)
            dwu_ref[...] = au[...].astype(BF16)
            dwd_ref[...] = ad[...].astype(BF16)

    xspec = pl.BlockSpec((tm, D), lambda s, i: (i, 0))
    abspec = pl.BlockSpec((None, tm, FSH), lambda s, i: (s, i, 0))
    return pl.pallas_call(
        body, name=name, grid=(NSH, nt),
        in_specs=[xspec, xspec, abspec, abspec, abspec],
        out_specs=[pl.BlockSpec((None, FSH, D), lambda s, i: (s, 0, 0))] * 3,
        out_shape=[jax.ShapeDtypeStruct((NSH, FSH, D), BF16)] * 3,
        scratch_shapes=[pltpu.VMEM((FSH, D), F32)] * 3,
        compiler_params=_cparams(("parallel", "arbitrary")),
    )(xn, df, da, db, hm)


def _t5_bucket_np(rel):
    n = np.maximum(rel, 0)
    n_f = np.maximum(n, 1).astype(np.float32)
    large = 16 + (np.log(n_f / np.float32(16)) / np.float32(math.log(8.0)) * np.float32(16)).astype(np.int32)
    large = np.minimum(large, 31)
    return np.where(n < 16, n, large).astype(np.int32)


def _swa_bucket_ids():
    qi = np.arange(BLK)[:, None]
    kj = np.arange(BLK)[None, :]
    out = np.full((3, BLK, 3 * BLK), -1, np.int32)
    for v in range(3):
        pos_q = v * BLK + qi - ZROWS
        rel_m = pos_q - (kj - ZROWS)
        ok_m = (kj >= ZROWS) & (rel_m >= 0) & (pos_q >= 0)
        out[v, :, 0:BLK] = np.where(ok_m, _t5_bucket_np(rel_m), -1)
        pos_kp = (v - 1) * BLK + kj - ZROWS
        rel_p = BLK + qi - kj
        ok_p = (pos_kp >= N_META) & (rel_p >= 0) & (rel_p < BLK) & np.full_like(ok_m, v >= 1)
        out[v, :, BLK:2 * BLK] = np.where(ok_p, _t5_bucket_np(rel_p), -1)
        pos_kc = v * BLK + kj - ZROWS
        rel_c = qi - kj
        ok_c = (pos_kc >= N_META) & (rel_c >= 0) & (rel_c < BLK)
        out[v, :, 2 * BLK:] = np.where(ok_c, _t5_bucket_np(rel_c), -1)
    return out


def swa_bias_fwd(table, ids):
    def body(t_ref, id_ref, o_ref):
        for v in range(3):
            for h in range(8):
                o_ref[v, h] = jnp.where(id_ref[v] < 0, NEG, 0.0)

            def step(b, carry):
                hit = id_ref[v] == b
                for h in range(8):
                    o_ref[v, h] += jnp.where(hit, t_ref[b, h], 0.0)
                return carry

            lax.fori_loop(0, 32, step, 0)

    return pl.pallas_call(
        body, name="swa_bias_fwd",
        in_specs=[pl.BlockSpec(memory_space=pltpu.SMEM), pl.BlockSpec(memory_space=pltpu.VMEM)],
        out_specs=pl.BlockSpec(memory_space=pltpu.VMEM),
        out_shape=jax.ShapeDtypeStruct((3, 8, BLK, 3 * BLK), F32),
        compiler_params=pltpu.CompilerParams(vmem_limit_bytes=VMEM_LIMIT),
    )(table, ids)


def swa_bias_bwd(dbias, ids):
    def body(d_ref, id_ref, o_ref):
        r = lax.broadcasted_iota(jnp.int32, (32, LANE), 0)
        c = lax.broadcasted_iota(jnp.int32, (32, LANE), 1)

        def step(b, acc):
            for v in range(3):
                hit = id_ref[v] == b
                for h in range(8):
                    m = jnp.where(hit, d_ref[v, h], 0.0)
                    s = jnp.sum(jnp.sum(m, axis=1, keepdims=True), axis=0, keepdims=True)
                    acc = acc + jnp.where((r == b) & (c == h), s, 0.0)
            return acc

        o_ref[...] = lax.fori_loop(0, 32, step, jnp.zeros((32, LANE), F32))

    return pl.pallas_call(
        body, name="swa_bias_bwd",
        in_specs=[pl.BlockSpec(memory_space=pltpu.VMEM), pl.BlockSpec(memory_space=pltpu.VMEM)],
        out_specs=pl.BlockSpec(memory_space=pltpu.VMEM),
        out_shape=jax.ShapeDtypeStruct((32, LANE), F32),
        compiler_params=pltpu.CompilerParams(vmem_limit_bytes=VMEM_LIMIT),
    )(dbias, ids)


def _swa_block(q, k3, v3, bias, sinks):
    heads = range(8)
    kh = [k3[:, (h // 4) * 128:(h // 4 + 1) * 128] for h in heads]
    vh = [v3[:, (h // 4) * 128:(h // 4 + 1) * 128] for h in heads]
    s = [_bmm_nt(q[:, h * 128:(h + 1) * 128], kh[h]) * 0.125 + bias[h] for h in heads]
    sink = [_lane_pick(sinks, h) for h in heads]
    m = [lax.stop_gradient(jnp.maximum(jnp.max(s[h], axis=-1, keepdims=True), sink[h])) for h in heads]
    e = [jnp.exp(s[h] - m[h]) for h in heads]
    p = [e[h] / (jnp.sum(e[h], axis=-1, keepdims=True) + jnp.exp(sink[h] - m[h])) for h in heads]
    return jnp.concatenate([_bmm(p[h], vh[h]) for h in heads], axis=1)


def _swa_in_specs():
    qs = pl.BlockSpec((BLK, 1024), lambda n: (n, E_QA // 1024))
    ks = [pl.BlockSpec((BLK, 256), lambda n: (0, E_KA // 256)),
          pl.BlockSpec((BLK, 256), lambda n: (jnp.maximum(n - 1, 0), E_KA // 256)),
          pl.BlockSpec((BLK, 256), lambda n: (n, E_KA // 256))]
    vs = [pl.BlockSpec((BLK, 256), lambda n: (0, E_VA // 256)),
          pl.BlockSpec((BLK, 256), lambda n: (jnp.maximum(n - 1, 0), E_VA // 256)),
          pl.BlockSpec((BLK, 256), lambda n: (n, E_VA // 256))]
    bs = pl.BlockSpec((None, 8, BLK, 3 * BLK), lambda n: (jnp.minimum(n, 2), 0, 0, 0))
    ss = pl.BlockSpec((1, LANE), lambda n: (0, 0))
    return [qs] + ks + vs + [bs, ss]


def swa_fwd(proj, bias, sinks, side=None):
    M = proj.shape[0]
    s_ins, s_specs, s_shapes, s_sems = _side_parts(side)
    ns, nso = len(s_ins), len(s_shapes)

    def body(q_ref, k0, k1, k2, v0, v1, v2, b_ref, s_ref, *rest):
        o_ref = rest[ns]
        _side_run(side, M // BLK, rest[:ns], rest[ns + 1:ns + 1 + nso], rest[ns + 1 + nso:])
        k3 = jnp.concatenate([k0[...], k1[...], k2[...]], axis=0)
        v3 = jnp.concatenate([v0[...], v1[...], v2[...]], axis=0)
        o_ref[...] = _swa_block(q_ref[...], k3, v3, b_ref[...], s_ref[...]).astype(o_ref.dtype)

    res = pl.pallas_call(
        body, name="swa_fwd", grid=(M // BLK,),
        in_specs=_swa_in_specs() + s_specs,
        out_specs=[pl.BlockSpec((BLK, 1024), lambda n: (n, 0))] + [pl.BlockSpec(memory_space=pl.ANY)] * nso,
        out_shape=[jax.ShapeDtypeStruct((M, 1024), BF16)] + s_shapes,
        scratch_shapes=s_sems,
        compiler_params=_cparams(("arbitrary",)),
    )(proj, proj, proj, proj, proj, proj, proj, bias, sinks, *s_ins)
    return res[0], res[1:]


def swa_bwd(proj, bias, sinks, do, side=None):
    M = proj.shape[0]
    s_ins, s_specs, s_shapes, s_sems = _side_parts(side)
    ns, nso = len(s_ins), len(s_shapes)

    def body(q_ref, k0, k1, k2, v0, v1, v2, b_ref, s_ref, do_ref, *rest):
        dq_ref, dk_ref, dv_ref, db_ref, ds_ref = rest[ns:ns + 5]
        _side_run(side, M // BLK, rest[:ns], rest[ns + 5:ns + 5 + nso], rest[ns + 5 + nso:])
        n = pl.program_id(0)

        @pl.when(n == 0)
        def _():
            dk_ref[...] = jnp.zeros_like(dk_ref)
            dv_ref[...] = jnp.zeros_like(dv_ref)
            ds_ref[...] = jnp.zeros_like(ds_ref)

        k3 = jnp.concatenate([k0[...], k1[...], k2[...]], axis=0)
        v3 = jnp.concatenate([v0[...], v1[...], v2[...]], axis=0)
        _, vjp = jax.vjp(_swa_block, q_ref[...], k3, v3, b_ref[...], s_ref[...])
        dq, dk3, dv3, dbias, dsink = vjp(do_ref[...].astype(F32))
        dq_ref[...] = dq
        prev = pl.multiple_of(jnp.maximum(n - 1, 0) * BLK, BLK)
        cur = pl.multiple_of(n * BLK, BLK)
        dk_ref[pl.ds(0, BLK), :] += dk3[0:BLK]
        dv_ref[pl.ds(0, BLK), :] += dv3[0:BLK]
        dk_ref[pl.ds(prev, BLK), :] += dk3[BLK:2 * BLK]
        dv_ref[pl.ds(prev, BLK), :] += dv3[BLK:2 * BLK]
        dk_ref[pl.ds(cur, BLK), :] += dk3[2 * BLK:]
        dv_ref[pl.ds(cur, BLK), :] += dv3[2 * BLK:]
        ds_ref[...] += dsink

        @pl.when(n <= 2)
        def _():
            db_ref[...] = dbias

        @pl.when(n > 2)
        def _():
            db_ref[...] += dbias

    res = pl.pallas_call(
        body, name="swa_bwd", grid=(M // BLK,),
        in_specs=_swa_in_specs() + [pl.BlockSpec((BLK, 1024), lambda n: (n, 0))] + s_specs,
        out_specs=[pl.BlockSpec((BLK, 1024), lambda n: (n, 0)),
                   pl.BlockSpec((M, 256), lambda n: (0, 0)), pl.BlockSpec((M, 256), lambda n: (0, 0)),
                   pl.BlockSpec((None, 8, BLK, 3 * BLK), lambda n: (jnp.minimum(n, 2), 0, 0, 0)),
                   pl.BlockSpec((1, LANE), lambda n: (0, 0))] + [pl.BlockSpec(memory_space=pl.ANY)] * nso,
        out_shape=[jax.ShapeDtypeStruct((M, 1024), F32), jax.ShapeDtypeStruct((M, 256), F32),
                   jax.ShapeDtypeStruct((M, 256), F32), jax.ShapeDtypeStruct((3, 8, BLK, 3 * BLK), F32),
                   jax.ShapeDtypeStruct((1, LANE), F32)] + s_shapes,
        scratch_shapes=s_sems,
        compiler_params=_cparams(("arbitrary",)),
    )(proj, proj, proj, proj, proj, proj, proj, bias, sinks, do, *s_ins)
    return res[:5], res[5:]


def _shift_rows_impl(x, k):
    n = x.shape[0]
    rolled = pltpu.roll(x, k, 0)
    return jnp.where(_row_ids(0, n) >= k, rolled, 0.0)


def _unshift_rows_impl(g, k):
    n = g.shape[0]
    rolled = pltpu.roll(g, n - k, 0)
    return jnp.where(_row_ids(0, n) < n - k, rolled, 0.0)


@functools.partial(jax.custom_vjp, nondiff_argnums=(1,))
def _shift_rows(x, k):
    return _shift_rows_impl(x, k)


def _shift_rows_f(x, k):
    return _shift_rows_impl(x, k), None


def _shift_rows_b(k, _, g):
    return (_unshift_rows_impl(g, k),)


_shift_rows.defvjp(_shift_rows_f, _shift_rows_b)


def _conv_silu(x, w):
    rid = lax.broadcasted_iota(jnp.int32, w.shape, 0)
    y = x * jnp.sum(jnp.where(rid == 3, w, 0.0), axis=0, keepdims=True)
    for k in range(1, 4):
        y = y + _shift_rows(x, k) * jnp.sum(jnp.where(rid == 3 - k, w, 0.0), axis=0, keepdims=True)
    y = jnp.where(_row_ids(0, x.shape[0]) >= ZROWS, y, 0.0)
    return _silu(y)


def conv_fwd(proj, conv_w):
    M = proj.shape[0]
    nb = conv_w.shape[1] // LANE

    def body(x_ref, w_ref, o_ref):
        o_ref[...] = _conv_silu(x_ref[...], w_ref[...])

    return pl.pallas_call(
        body, name="conv_fwd", grid=(nb,),
        in_specs=[pl.BlockSpec((M, LANE), lambda c: (0, E_QB // LANE + c)), pl.BlockSpec((4, LANE), lambda c: (0, c))],
        out_specs=pl.BlockSpec((M, LANE), lambda c: (0, c)),
        out_shape=jax.ShapeDtypeStruct((M, conv_w.shape[1]), F32),
        compiler_params=_cparams(("parallel",)),
    )(proj, conv_w)


def conv_bwd(proj, conv_w, dy):
    M = proj.shape[0]
    nb = conv_w.shape[1] // LANE

    def body(x_ref, w_ref, dy_ref, dx_ref, dw_ref):
        _, vjp = jax.vjp(_conv_silu, x_ref[...], w_ref[...])
        dx, dw = vjp(dy_ref[...])
        dx_ref[...] = dx
        dw_ref[...] = dw

    return pl.pallas_call(
        body, name="conv_bwd", grid=(nb,),
        in_specs=[pl.BlockSpec((M, LANE), lambda c: (0, E_QB // LANE + c)), pl.BlockSpec((4, LANE), lambda c: (0, c)),
                  pl.BlockSpec((M, LANE), lambda c: (0, c))],
        out_specs=[pl.BlockSpec((M, LANE), lambda c: (0, c)), pl.BlockSpec((4, LANE), lambda c: (0, c))],
        out_shape=[jax.ShapeDtypeStruct((M, conv_w.shape[1]), F32), jax.ShapeDtypeStruct(conv_w.shape, F32)],
        compiler_params=_cparams(("parallel",)),
    )(proj, conv_w, dy)


def _fn_dn_prep(row0, yq, yk, ba, dnp):
    tm = yq.shape[0]
    real = _row_ids(row0, tm) >= ZROWS
    qs, ks, gs, bs = [], [], [], []
    for h in range(4):
        q = yq[:, h * 128:(h + 1) * 128]
        k = yk[:, h * 128:(h + 1) * 128]
        qs.append(q * lax.rsqrt(jnp.sum(q * q, axis=-1, keepdims=True) + 1e-6) * (128.0 ** -0.5))
        ks.append(k * lax.rsqrt(jnp.sum(k * k, axis=-1, keepdims=True) + 1e-6))
        beta = _sigmoid(_lane_pick(ba, h))
        g = -jnp.exp(_lane_pick(dnp, h)) * _softplus(_lane_pick(ba, 4 + h) + _lane_pick(dnp, 4 + h))
        g = jnp.where(real, g, 0.0)
        gs.append(jnp.broadcast_to(g, (tm, 128)))
        bs.append(jnp.broadcast_to(beta, (tm, 128)))
    cat = lambda xs: jnp.concatenate(xs, axis=1)
    return cat(qs), cat(ks), cat(gs), cat(bs)


def _zip(f, *lists):
    return [f(*args) for args in zip(*lists)]


def _unit_lower_inv_impl(a):
    n = a[0].shape[0]
    eye = (lax.broadcasted_iota(jnp.int32, (n, n), 0) == lax.broadcasted_iota(jnp.int32, (n, n), 1)).astype(F32)
    nn = ((1,), (0,))
    p = [-x for x in a]
    t = [eye + x for x in p]
    for _ in range(int(math.log2(n)) - 1):
        p = _zip(lambda x: _xdot(x, x, nn), p)
        t = _zip(lambda x, y: x + _xdot(x, y, nn), t, p)
    return t


@jax.custom_vjp
def _unit_lower_inv(a):
    return _unit_lower_inv_impl(a)


def _unit_lower_inv_f(a):
    t = _unit_lower_inv_impl(a)
    return t, t


def _unit_lower_inv_b(t, g):
    tg = _zip(lambda x, y: _xdot(x, y, ((0,), (0,))), t, g)
    return (_zip(lambda x, y: -_xdot(x, y, ((1,), (1,))), tg, t),)


_unit_lower_inv.defvjp(_unit_lower_inv_f, _unit_lower_inv_b)


@jax.custom_vjp
def _known_inv(a, t):
    return t


_known_inv.defvjp(lambda a, t: (t, t), lambda t, g: (_unit_lower_inv_b(t, g)[0], [jnp.zeros_like(x) for x in t]))


def _dn_block(q, k, v, gb, bb, S, t_kept=None):
    nh = len(S)
    r = lax.broadcasted_iota(jnp.int32, (CH, CH), 0)
    c = lax.broadcasted_iota(jnp.int32, (CH, CH), 1)
    tri_incl = r >= c
    gcb = _zip(_cumsum_rows, gb)
    gamma = _zip(lambda x: jnp.where(tri_incl, jnp.exp(jnp.where(tri_incl, x[:, :CH] - x[:, :CH].T, 0.0)), 0.0), gcb)
    kb = _zip(jnp.multiply, k, bb)
    vb = _zip(jnp.multiply, v, bb)
    a = _zip(lambda m, g: jnp.where(r > c, m * g, 0.0), _zip(_bmm_nt, kb, k), gamma)
    t = _unit_lower_inv(a) if t_kept is None else _known_inv(a, list(t_kept))
    eg = _zip(jnp.exp, gcb)
    u = _zip(_xmm, t, vb)
    w = _zip(_xmm, t, _zip(jnp.multiply, kb, eg))
    attn = _zip(lambda m, g: m * g, _zip(_bmm_nt, q, k), gamma)
    gtot = _zip(lambda x: jnp.sum(x, axis=0, keepdims=True), gb)
    k_dec = _zip(lambda x, gt, gc: x * jnp.exp(gt - gc), k, gtot, gcb)
    q_dec = _zip(jnp.multiply, q, eg)
    S = list(S)
    o, starts = [], []
    for i0 in range(0, len(q), nh):
        idx = range(i0, i0 + nh)
        starts.append(list(S))
        v_new = [u[i] - m for i, m in zip(idx, [_bmm(w[i], S[h]) for h, i in enumerate(idx)])]
        oq = [_bmm(q_dec[i], S[h]) for h, i in enumerate(idx)]
        oa = [_bmm(attn[i], vn) for i, vn in zip(idx, v_new)]
        kv = [_bmm_tn(k_dec[i], vn) for i, vn in zip(idx, v_new)]
        o += _zip(jnp.add, oq, oa)
        S = [S[h] * jnp.exp(jnp.broadcast_to(gtot[i], S[h].shape)) + kv[h] for h, i in enumerate(idx)]
    return o, S, starts, t


def _gla_block(q, k, v, glog, S):
    nh = len(S)
    tri = lax.broadcasted_iota(jnp.int32, (CH, CH), 0) >= lax.broadcasted_iota(jnp.int32, (CH, CH), 1)
    bcum = _zip(_cumsum_rows, glog)
    q_dec = _zip(lambda x, b: x * (128.0 ** -0.5) * jnp.exp(b), q, bcum)
    attn = _zip(lambda m: jnp.where(tri, m, 0.0), _zip(_bmm_nt, q_dec, _zip(lambda x, b: x * jnp.exp(-b), k, bcum)))
    o_in = _zip(_bmm, attn, v)
    k_dec = _zip(lambda x, g, b: x * jnp.exp(jnp.sum(g, axis=0, keepdims=True) - b), k, glog, bcum)
    decay = _zip(lambda g, x: jnp.exp(_colsum_as_rows(g, x.shape[1])), glog, v)
    kv = _zip(_bmm_tn, k_dec, v)
    S = list(S)
    o, starts = [], []
    for i0 in range(0, len(q), nh):
        idx = range(i0, i0 + nh)
        starts.append(list(S))
        o += [o_in[i] + m for i, m in zip(idx, [_bmm(q_dec[i], S[h]) for h, i in enumerate(idx)])]
        S = [S[h] * decay[i] + kv[i] for h, i in enumerate(idx)]
    return o, S, starts


class Side:
    def __init__(self, ins, out_shapes, nsem, events):
        self.ins, self.out_shapes, self.nsem, self.events = list(ins), list(out_shapes), nsem, events


def _side_parts(side):
    if side is None:
        return [], [], [], []
    anyspec = pl.BlockSpec(memory_space=pl.ANY)
    return (side.ins, [anyspec] * len(side.ins), side.out_shapes,
            [pltpu.SemaphoreType.DMA((side.nsem,)), pltpu.SemaphoreType.DMA((side.nsem,))])


def _side_run(side, n_steps, in_refs, out_refs, sems, step=None):
    if side is None:
        return
    step = pl.program_id(0) if step is None else step
    for at, fn in side.events(n_steps, in_refs, out_refs, *sems):
        pl.when(step == at)(fn)


def chunk_fwd(name, chunk_fn, ins, dv, side=None, keep=()):
    M = ins[0][0].shape[0]
    NC = M // CH
    N = NC // CPS
    ni, nk = len(ins), len(keep)
    ws = [w for (_, w, _) in ins]
    s_ins, s_specs, s_shapes, s_sems = _side_parts(side)
    ns, nso = len(s_ins), len(s_shapes)

    def body(*refs):
        o0 = ni + ns
        o_ref, sall_ref = refs[o0:o0 + 2]
        k_refs = refs[o0 + 2:o0 + 2 + nk]
        s_ref = refs[o0 + 2 + nk + nso]
        _side_run(side, N, refs[ni:o0], refs[o0 + 2 + nk:o0 + 2 + nk + nso], refs[o0 + 3 + nk + nso:])

        @pl.when(pl.program_id(0) == 0)
        def _():
            s_ref[...] = jnp.zeros_like(s_ref)

        problems = [(cc, h) for cc in range(CPS) for h in range(4)]
        lists = [[r[cc * CH:(cc + 1) * CH, h * w:(h + 1) * w] for cc, h in problems] for r, w in zip(refs[:ni], ws)]
        o, s_new, starts, *kept = chunk_fn(*lists, [s_ref[h] for h in range(4)])
        for b, (cc, h) in enumerate(problems):
            o_ref[cc * CH:(cc + 1) * CH, h * dv:(h + 1) * dv] = o[b]
            sall_ref[h, cc] = starts[cc][h]
            for k_ref, vals in zip(k_refs, kept):
                k_ref[h, cc] = vals[b]
        for h in range(4):
            s_ref[h] = s_new[h]

    per_chunk = lambda r, c: pl.BlockSpec((4, CPS, r, c), lambda n: (0, n, 0, 0))
    specs = [pl.BlockSpec((CPS * CH, 4 * w), functools.partial(lambda n, cb: (n, cb), cb=cb // 4)) for (_, w, cb) in ins]
    res = pl.pallas_call(
        body, name=name, grid=(N,),
        in_specs=specs + s_specs,
        out_specs=[pl.BlockSpec((CPS * CH, 4 * dv), lambda n: (n, 0)), per_chunk(128, dv)] + [per_chunk(r, c) for r, c in keep]
        + [pl.BlockSpec(memory_space=pl.ANY)] * nso,
        out_shape=[jax.ShapeDtypeStruct((M, 4 * dv), F32), jax.ShapeDtypeStruct((4, NC, 128, dv), F32)]
        + [jax.ShapeDtypeStruct((4, NC, r, c), F32) for r, c in keep] + s_shapes,
        scratch_shapes=[pltpu.VMEM((4, 128, dv), F32)] + s_sems,
        compiler_params=_cparams(("arbitrary",)),
    )(*[a for (a, _, _) in ins], *s_ins)
    return res[0], res[1], res[2:2 + nk], res[2 + nk:]


def chunk_bwd(name, chunk_fn, ins, dv, s_all, do, side=None, kept=()):
    M = ins[0][0].shape[0]
    N = M // CH // CPS
    ni, nk = len(ins), len(kept)
    ws = [w for (_, w, _) in ins]
    s_ins, s_specs, s_shapes, s_sems = _side_parts(side)
    ns, nso = len(s_ins), len(s_shapes)

    def body(*refs):
        sall_ref, do_ref = refs[ni:ni + 2]
        k_refs = refs[ni + 2:ni + 2 + nk]
        o0 = ni + 2 + nk + ns
        d_refs = refs[o0:o0 + ni]
        ds_ref = refs[o0 + ni + nso]
        _side_run(side, N, refs[ni + 2 + nk:o0], refs[o0 + ni:o0 + ni + nso], refs[o0 + ni + nso + 1:])

        @pl.when(pl.program_id(0) == 0)
        def _():
            ds_ref[...] = jnp.zeros_like(ds_ref)

        problems = [(cc, h) for cc in range(CPS) for h in range(4)]
        lists = [[r[cc * CH:(cc + 1) * CH, h * w:(h + 1) * w] for cc, h in problems] for r, w in zip(refs[:ni], ws)]
        kept_lists = [[k_ref[h, cc] for cc, h in problems] for k_ref in k_refs]
        _, vjp = jax.vjp(lambda *a: tuple(chunk_fn(*a)[:2]), *lists, [sall_ref[h, 0] for h in range(4)], *kept_lists)
        grads = vjp(([do_ref[cc * CH:(cc + 1) * CH, h * dv:(h + 1) * dv] for cc, h in problems],
                     [ds_ref[h] for h in range(4)]))
        for d_ref, w, g in zip(d_refs, ws, grads[:ni]):
            for b, (cc, h) in enumerate(problems):
                d_ref[cc * CH:(cc + 1) * CH, h * w:(h + 1) * w] = g[b]
        for h in range(4):
            ds_ref[h] = grads[ni][h]

    rev = lambda n: N - 1 - n
    per_chunk = lambda r, c: pl.BlockSpec((4, CPS, r, c), lambda n: (0, rev(n), 0, 0))
    specs = [pl.BlockSpec((CPS * CH, 4 * w), functools.partial(lambda n, cb: (rev(n), cb), cb=cb // 4)) for (_, w, cb) in ins]
    res = pl.pallas_call(
        body, name=name, grid=(N,),
        in_specs=specs + [per_chunk(128, dv), pl.BlockSpec((CPS * CH, 4 * dv), lambda n: (rev(n), 0))]
        + [per_chunk(*a.shape[2:]) for a in kept] + s_specs,
        out_specs=[pl.BlockSpec((CPS * CH, 4 * w), lambda n: (rev(n), 0)) for w in ws] + [pl.BlockSpec(memory_space=pl.ANY)] * nso,
        out_shape=[jax.ShapeDtypeStruct((M, 4 * w), F32) for w in ws] + s_shapes,
        scratch_shapes=[pltpu.VMEM((4, 128, dv), F32)] + s_sems,
        compiler_params=_cparams(("arbitrary",)),
    )(*[a for (a, _, _) in ins], s_all, do, *kept, *s_ins)
    return res[:ni], res[ni:]


def _fn_gate_out(hd, row0, o, z, w):
    outs = []
    for h in range(4):
        outs.append(_rms(o[:, h * hd:(h + 1) * hd], w) * _silu(z[:, h * hd:(h + 1) * hd]))
    return (jnp.concatenate(outs, axis=1),)


def _fn_gla_prep(row0, gk, wgu, bg):
    x = _mm(gk, wgu) + bg
    ls = jnp.minimum(x, 0.0) - jnp.log(1.0 + jnp.exp(-jnp.abs(x)))
    return (jnp.where(_row_ids(row0, gk.shape[0]) >= ZROWS, ls / 16.0, 0.0),)


def loss_call(y, tgt):
    M = y.shape[0]
    tm = _pick(M, 512)

    def body(y_ref, t_ref, l_ref, dy_ref):
        i = pl.program_id(0)
        e = y_ref[...] - t_ref[...]
        dy_ref[...] = e * (1.0 / D)
        part = 0.5 * jnp.sum(jnp.sum(e * e, axis=1, keepdims=True) * (1.0 / D), axis=0, keepdims=True)
        part = jnp.broadcast_to(part, (8, LANE))

        @pl.when(i == 0)
        def _():
            l_ref[...] = part

        @pl.when(i > 0)
        def _():
            l_ref[...] += part

    return pl.pallas_call(
        body, name="loss", grid=(M // tm,),
        in_specs=[pl.BlockSpec((tm, D), lambda i: (i, 0))] * 2,
        out_specs=[pl.BlockSpec((8, LANE), lambda i: (0, 0)), pl.BlockSpec((tm, D), lambda i: (i, 0))],
        out_shape=[jax.ShapeDtypeStruct((8, LANE), F32), jax.ShapeDtypeStruct((M, D), F32)],
        compiler_params=_cparams(("arbitrary",)),
    )(y, tgt)


def _bf(x):
    return x.astype(BF16)


def core_step(x, tgt, W, comm=None):
    S = x.shape[0]
    M = S + PADR
    ids = jnp.asarray(_swa_bucket_ids())
    h0 = jnp.concatenate([jnp.zeros((ZROWS, D), F32), W["meta"], x], axis=0)
    nw = W["norm"]
    nrow = lambda l, k: nw[l, k][None, :]
    layers = list(W["layers"])
    ffw = lambda l: (layers[l]["ffn_g"], layers[l]["ffn_u"], layers[l]["ffn_d"])
    site = (lambda name: comm.side(name)) if comm else (lambda name: None)
    landed = (lambda name, outs: comm.done(name, outs)) if comm else (lambda name, outs: None)
    sinks = jnp.pad(W["sinks"], ((0, 0), (0, LANE - 8)))
    dnp = jnp.pad(jnp.concatenate([W["a_log"], W["dt_bias"]], axis=1), ((0, 0), (0, LANE - 8)))
    wgu = jnp.pad(W["gate_up"], ((0, LANE - 16), (0, 0)))
    bg = W["b_gate"]
    full = lambda a: (a, a.shape[1], 0)

    saved = []
    h = h0
    (hn,) = rowwise_fwd("prenorm_0", _fn_prenorm, [full(h)], [nrow(0, 0)], [(D, BF16)])
    bias = swa_bias_fwd(W["rel"], ids)
    for l in range(2):
        st = {"h_a": h, "hn_a": hn}
        (f1, a1, b1), got = ffn_fwd(f"ffn_fwd_{l}0", hn, *ffw(l), 0, side=site(f"ffn_fwd_{l}0"))
        landed(f"ffn_fwd_{l}0", got)
        if comm and l == 0:
            layers[0] = {**layers[0], **comm.layer0_proj()}
        h, hn = rowwise_fwd(f"resnorm_{l}1", functools.partial(_fn_resnorm, 0.5), [full(h), full(f1)],
                            [nrow(l, 1), nrow(l, 2)], [(D, F32), (D, BF16)])
        st.update(f1=f1, a1=a1, b1=b1, h_b=h, hn_b=hn)
        if l == 0:
            proj = mm_nt(hn, layers[0]["w_in"], "e_proj")
            o_a, got = swa_fwd(proj, bias, sinks, side=site("swa_fwd"))
            landed("swa_fwd", got)
            y = conv_fwd(proj, W["conv"])
            qn, kn, gb, bb = rowwise_fwd(
                "dn_prep", _fn_dn_prep, [(y, 512, 0), (y, 512, 1), (proj, LANE, E_BA // LANE)], [dnp], [(512, F32)] * 4)
            ins = [(qn, 128, 0), (kn, 128, 0), (y, 128, 8), (gb, 128, 0), (bb, 128, 0)]
            o_dn, s_all, (t_inv,), got = chunk_fwd("dn_fwd", _dn_block, ins, 128, keep=[(CH, CH)],
                                                   side=site("dn_fwd"))
            landed("dn_fwd", got)
            (o_b,) = rowwise_fwd("dn_out", functools.partial(_fn_gate_out, 128),
                                 [full(o_dn), (proj, 512, E_ZB // 512)], [W["dn_norm"]], [(512, BF16)])
            omix = jnp.concatenate([o_a, o_b], axis=1)
            mix = mm_nn(omix, layers[0]["w_out"], "e_mix")
            st.update(proj=proj, y=y, qn=qn, kn=kn, gb=gb, bb=bb, o_dn=o_dn, s_all=s_all, t_inv=t_inv, omix=omix)
        else:
            proj = mm_nt(hn, layers[1]["w_in"], "o_proj")
            (glog,) = rowwise_fwd("gla_prep", _fn_gla_prep, [(proj, LANE, O_GK // LANE)], [wgu, bg], [(512, F32)])
            ins = [(proj, 128, O_Q // 128), (proj, 128, O_K // 128), (proj, 256, O_V // 256), (glog, 128, 0)]
            o_g, s_all, _, _ = chunk_fwd("gla_fwd", _gla_block, ins, 256)
            (omix,) = rowwise_fwd("gla_out", functools.partial(_fn_gate_out, 256),
                                  [full(o_g), (proj, 1024, O_G // 1024)], [W["gla_norm"]], [(1024, BF16)])
            mix = mm_nn(omix, layers[1]["w_out"], "o_mix")
            st.update(proj=proj, glog=glog, o_g=o_g, s_all=s_all, omix=omix)
        h, hn = rowwise_fwd(f"resnorm_{l}3", functools.partial(_fn_resnorm, 1.0), [full(h), full(mix)],
                            [nrow(l, 3), nrow(l, 4)], [(D, F32), (D, BF16)])
        st.update(mix=mix, h_c=h, hn_c=hn)
        (f2, a2, b2), got = ffn_fwd(f"ffn_fwd_{l}1", hn, *ffw(l), 1, side=site(f"ffn_fwd_{l}1"))
        landed(f"ffn_fwd_{l}1", got)
        if comm and l == 0:
            layers[1] = comm.layer1()
        st.update(f2=f2, a2=a2, b2=b2)
        if l == 0:
            h, hn = rowwise_fwd("resnorm_05", functools.partial(_fn_resnorm, 0.5), [full(h), full(f2)],
                                [nrow(0, 5), nrow(1, 0)], [(D, F32), (D, BF16)])
        else:
            (h,) = rowwise_fwd("res_last", functools.partial(_fn_res_last, 0.5), [full(h), full(f2)],
                               [nrow(1, 5)], [(D, F32)])
        saved.append(st)

    loss_blk, dy = loss_call(h[PADR:], tgt)
    dh = jnp.concatenate([jnp.zeros((PADR, D), F32), dy], axis=0)

    G = {}
    dnorm = [[None] * 6 for _ in range(2)]
    dWg = [[None, None], [None, None]]
    dWu = [[None, None], [None, None]]
    dWd = [[None, None], [None, None]]
    dhn = None
    for l in (1, 0):
        st = saved[l]
        if l == 1:
            (dh_, df), (dw5,) = rowwise_bwd(
                "res_last_b", functools.partial(_fn_res_last, 0.5), [full(st["h_c"]), full(st["f2"])], [nrow(1, 5)],
                [full(dh)], [F32, BF16])
            dnorm[1][5] = dw5
        else:
            (dh_, df), (dw5, dw0n) = rowwise_bwd(
                "resnorm_05_b", functools.partial(_fn_resnorm, 0.5), [full(st["h_c"]), full(st["f2"])],
                [nrow(0, 5), nrow(1, 0)], [full(dh), full(dhn)], [F32, BF16])
            dnorm[0][5] = dw5
            dnorm[1][0] = dw0n
        dh = dh_
        if comm and l == 0:
            comm.grads(1, range(5), layer_grad_items(1, dWg, dWu, dWd, G["o_in"], G["o_out"]))
        (dxn, da, db, hm), got = ffn_bwd_x(f"ffn_bx_{l}1", df, st["a2"], st["b2"], *ffw(l), 1, side=site(f"ffn_bx_{l}1"))
        landed(f"ffn_bx_{l}1", got)
        dWg[l][1], dWu[l][1], dWd[l][1] = ffn_bwd_w(f"ffn_bw_{l}1", st["hn_c"], df, da, db, hm)
        (dh_, dmix), (dw3, dw4) = rowwise_bwd(
            f"resnorm_{l}3_b", functools.partial(_fn_resnorm, 1.0), [full(st["h_b"]), full(st["mix"])],
            [nrow(l, 3), nrow(l, 4)], [full(dh), full(dxn)], [F32, BF16])
        dnorm[l][3], dnorm[l][4] = dw3, dw4
        dh = dh_
        proj = st["proj"]
        if l == 1:
            G["o_out"] = mm_tn(st["omix"], dmix, "o_out_dw")
            domix = mm_nt(dmix, layers[1]["w_out"], "o_mix_dx")
            (do_g, dgate), (dgn,) = rowwise_bwd(
                "gla_out_b", functools.partial(_fn_gate_out, 256), [full(st["o_g"]), (proj, 1024, O_G // 1024)],
                [W["gla_norm"]], [full(domix)], [F32, F32])
            G["gla_norm"] = dgn
            ins = [(proj, 128, O_Q // 128), (proj, 128, O_K // 128), (proj, 256, O_V // 256), (st["glog"], 128, 0)]
            (dq, dk, dv, dglog), _ = chunk_bwd("gla_bwd", _gla_block, ins, 256, st["s_all"], do_g)
            (dgk,), (dwgu, dbg) = rowwise_bwd("gla_prep_b", _fn_gla_prep, [(proj, LANE, O_GK // LANE)], [wgu, bg],
                                              [full(dglog)], [F32])
            G["gate_up"] = dwgu[:16]
            G["b_gate"] = dbg
            dproj = _bf(jnp.concatenate([dq, dk, dv, dgate, dgk, jnp.zeros((M, O_END - O_GK - LANE), F32)], axis=1))
            G["o_in"] = mm_tn(dproj, st["hn_b"], "o_in_dw")
            dhn_b = mm_nn(dproj, layers[1]["w_in"], "o_proj_dx")
        else:
            G["e_out"] = mm_tn(st["omix"], dmix, "e_out_dw")
            domix = mm_nt(dmix, layers[0]["w_out"], "e_mix_dx")
            (do_dn, dz), (ddn,) = rowwise_bwd(
                "dn_out_b", functools.partial(_fn_gate_out, 128), [full(st["o_dn"]), (proj, 512, E_ZB // 512)],
                [W["dn_norm"]], [(domix, 512, 2)], [F32, F32])
            G["dn_norm"] = ddn
            ins = [(st["qn"], 128, 0), (st["kn"], 128, 0), (st["y"], 128, 8), (st["gb"], 128, 0), (st["bb"], 128, 0)]
            (dqn, dkn, dvv, dgb, dbb), got = chunk_bwd("dn_bwd", _dn_block, ins, 128, st["s_all"], do_dn, side=site("dn_bwd"),
                                                        kept=[st["t_inv"]])
            landed("dn_bwd", got)
            (dyq, dyk, dba), (ddnp,) = rowwise_bwd(
                "dn_prep_b", _fn_dn_prep, [(st["y"], 512, 0), (st["y"], 512, 1), (proj, LANE, E_BA // LANE)], [dnp],
                [full(dqn), full(dkn), full(dgb), full(dbb)], [F32, F32, F32])
            G["a_log"] = ddnp[:, 0:4]
            G["dt_bias"] = ddnp[:, 4:8]
            dyc = jnp.concatenate([dyq, dyk, dvv], axis=1)
            dxc, dconv = conv_bwd(proj, W["conv"], dyc)
            G["conv"] = dconv
            (dq_a, dk_a, dv_a, dbias, dsink), got = swa_bwd(proj, bias, sinks, domix, side=site("swa_bwd"))
            landed("swa_bwd", got)
            G["sinks"] = dsink[:, :8]
            G["rel"] = swa_bias_bwd(dbias, ids)[:, :8]
            dproj = _bf(jnp.concatenate([dq_a, dk_a, dv_a, dxc, dz, dba, jnp.zeros((M, E_END - E_BA - LANE), F32)], axis=1))
            G["e_in"] = mm_tn(dproj, st["hn_b"], "e_in_dw")
            dhn_b = mm_nn(dproj, layers[0]["w_in"], "e_proj_dx")
        (dh_, df), (dw1, dw2) = rowwise_bwd(
            f"resnorm_{l}1_b", functools.partial(_fn_resnorm, 0.5), [full(st["h_a"]), full(st["f1"])],
            [nrow(l, 1), nrow(l, 2)], [full(dh), full(dhn_b)], [F32, BF16])
        dnorm[l][1], dnorm[l][2] = dw1, dw2
        dh = dh_
        if comm and l == 0:
            comm.grads(0, (3, 4), layer_grad_items(0, dWg, dWu, dWd, G["e_in"], G["e_out"], only=(3, 4)))
        (dxn, da, db, hm), got = ffn_bwd_x(f"ffn_bx_{l}0", df, st["a1"], st["b1"], *ffw(l), 0, side=site(f"ffn_bx_{l}0"))
        landed(f"ffn_bx_{l}0", got)
        dWg[l][0], dWu[l][0], dWd[l][0] = ffn_bwd_w(f"ffn_bw_{l}0", st["hn_a"], df, da, db, hm)
        dhn = dxn
    (dh0p,), (dw00,) = rowwise_bwd("prenorm_0_b", _fn_prenorm, [full(saved[0]["h_a"])], [nrow(0, 0)], [full(dhn)], [F32])
    dnorm[0][0] = dw00
    dh = dh + dh0p
    G["meta"] = dh[ZROWS:PADR]
    G["norm"] = jnp.stack([jnp.concatenate(r, axis=0) for r in dnorm], axis=0)
    G["items"] = [layer_grad_items(0, dWg, dWu, dWd, G["e_in"], G["e_out"], only=(0, 1, 2) if comm else range(5)),
                  None if comm else layer_grad_items(1, dWg, dWu, dWd, G["o_in"], G["o_out"])]
    return loss_blk, dh[PADR:], G


NAMES = [("meta", "meta_tokens"), ("norm", "norm_w"), ("ffn_g", "ffn_w_gate"), ("ffn_u", "ffn_w_up"),
         ("ffn_d", "ffn_w_down"), ("rel", "rel_bias_table"), ("e_in", "even_w_in"), ("conv", "even_conv_w"),
         ("sinks", "swa_sinks"), ("a_log", "dn_a_log"), ("dt_bias", "dn_dt_bias"), ("dn_norm", "dn_norm_w"),
         ("e_out", "even_w_out"), ("o_in", "odd_w_in"), ("gate_up", "gla_w_gate_up"), ("b_gate", "gla_b_gate"),
         ("gla_norm", "gla_norm_w"), ("o_out", "odd_w_out")]
BIG = ["ffn_g", "ffn_u", "ffn_d", "e_in", "e_out", "o_in", "o_out"]
IN_ROWS = 800
SMALL = [("meta", (16, 256)), ("norm", (2, 6, 256)), ("conv", (1, 4, 384)), ("gate_up", (1, 16, 128)),
         ("b_gate", (1, 128)), ("gla_norm", (1, 64))]
REPL = [("rel", (32, 8)), ("sinks", (1, 8)), ("a_log", (1, 4)), ("dt_bias", (1, 4)), ("dn_norm", (1, 128))]
SMALL_REP = 88 * LANE
SMALL_ROWS = 96


def pack_small(t):
    a = jnp.concatenate([t[n].reshape(-1) for n, _ in SMALL])
    b = jnp.concatenate([t[n].reshape(-1) for n, _ in REPL])
    flat = jnp.concatenate([a, jnp.zeros((SMALL_REP - a.shape[0],), F32), b,
                            jnp.zeros((SMALL_ROWS * LANE - SMALL_REP - b.shape[0],), F32)])
    return flat.reshape(SMALL_ROWS, LANE)


def unpack_small(p):
    flat = p.reshape(-1)
    out, r = {}, 0
    for n, shp in SMALL:
        k = int(np.prod(shp))
        out[n] = flat[r:r + k].reshape(shp)
        r += k
    r = SMALL_REP
    for n, shp in REPL:
        k = int(np.prod(shp))
        out[n] = flat[r:r + k].reshape(shp)
        r += k
    return out


IN_SRC = (706, 772)


def weight_pieces(wt, l):
    inn = wt["e_in" if l == 0 else "o_in"]
    inn = jnp.pad(inn, ((0, IN_ROWS - inn.shape[0]), (0, 0))).reshape(2, IN_ROWS // 2, D)
    out = wt["e_out" if l == 0 else "o_out"].reshape(2, 128, D)
    return [wt["ffn_g"][l], wt["ffn_u"][l], wt["ffn_d"][l], inn, out]


def proj_weights(l, q_in, q_out):
    m = _even_in_map() if l == 0 else _odd_in_map()
    src = np.where(m >= 0, (m // IN_SRC[l]) * IN_ROWS + m % IN_SRC[l], -1)
    w_out = q_out.reshape(NSH * 256, D)
    return {"w_in": _take_pad(q_in.reshape(NSH * IN_ROWS, D), src, 0),
            "w_out": _take_pad(w_out, _even_out_map(), 0) if l == 0 else w_out}


def layer_weights(l, q):
    return {"ffn_g": q[0], "ffn_u": q[1], "ffn_d": q[2], **proj_weights(l, q[3], q[4])}


def layer_grad_items(l, dwg, dwu, dwd, g_in, g_out, only=range(5)):
    units = [[None] * 5 for _ in range(2)]
    for j in range(2):
        for i, t in enumerate((dwg, dwu, dwd)):
            if i in only:
                units[j][i] = t[l][j]
    if 3 in only:
        m = _even_in_map() if l == 0 else _odd_in_map()
        gi = jnp.take(g_in, jnp.asarray(_inverse(m, NSH * IN_SRC[l])), axis=0).reshape(NSH, IN_SRC[l], D)
        gi = _bf(jnp.pad(gi, ((0, 0), (0, IN_ROWS - IN_SRC[l]), (0, 0)))).reshape(NSH, 2, IN_ROWS // 2, D)
        units[0][3], units[1][3] = gi[:, 0], gi[:, 1]
    if 4 in only:
        if l == 0:
            g_out = jnp.take(g_out, jnp.asarray(_inverse(_even_out_map(), 1024)), axis=0)
        go = _bf(g_out).reshape(NSH, 2, 128, D)
        units[0][4], units[1][4] = go[:, 0], go[:, 1]
    return units


def assemble_layer(l, r0, r1):
    return {"ffn_g": jnp.stack([r0[0], r1[0]]), "ffn_u": jnp.stack([r0[1], r1[1]]), "ffn_d": jnp.stack([r0[2], r1[2]]),
            "in": jnp.concatenate([r0[3], r1[3]])[:IN_SRC[l]], "out": jnp.concatenate([r0[4], r1[4]])}


def big_grads(l0, l1):
    st = lambda n: jnp.stack([l0[n], l1[n]])
    return {"ffn_g": st("ffn_g"), "ffn_u": st("ffn_u"), "ffn_d": st("ffn_d"), "e_in": l0["in"], "e_out": l0["out"],
            "o_in": l1["in"], "o_out": l1["out"]}


def small_from_gathered(gs):
    sm = [unpack_small(gs[s]) for s in range(NSH)]
    full = {}
    full["meta"] = jnp.concatenate([sm[s]["meta"] for s in range(NSH)], axis=1)
    full["norm"] = jnp.concatenate([sm[s]["norm"] for s in range(NSH)], axis=2)
    full["conv"] = jnp.concatenate([sm[s]["conv"][0] for s in range(NSH)], axis=1)
    full["gate_up"] = jnp.concatenate([sm[s]["gate_up"][0] for s in range(NSH)], axis=1)
    full["b_gate"] = jnp.concatenate([sm[s]["b_gate"] for s in range(NSH)], axis=1)
    full["gla_norm"] = jnp.concatenate([sm[s]["gla_norm"] for s in range(NSH)], axis=1)
    return full


def _col_sh(w):
    return jnp.moveaxis(w.reshape(w.shape[0], NSH, w.shape[1] // NSH), 1, 0)


def _to_t(n, a):
    if n in ("ffn_g", "ffn_u"):
        return jnp.swapaxes(a, 2, 3)
    if n in ("e_in", "o_in"):
        return jnp.swapaxes(a[0], 0, 1)
    return a if n == "ffn_d" else a[0]


def _from_t(n, a):
    if n in ("ffn_g", "ffn_u"):
        return jnp.swapaxes(a, 2, 3)
    if n in ("e_in", "o_in"):
        return jnp.swapaxes(a, 0, 1)[None]
    return a if n == "ffn_d" else a[None]


def pack_small_grads(G):
    col_sh = _col_sh
    norm_sh = jnp.moveaxis(G["norm"].reshape(2, 6, NSH, 256), 2, 0)
    a = jnp.concatenate([col_sh(G["meta"]).reshape(NSH, -1), norm_sh.reshape(NSH, -1), col_sh(G["conv"]).reshape(NSH, -1),
                         col_sh(G["gate_up"]).reshape(NSH, -1), G["b_gate"].reshape(NSH, -1),
                         G["gla_norm"].reshape(NSH, -1)], axis=1)
    b = jnp.concatenate([G[n].reshape(-1) for n, _ in REPL])
    b = jnp.broadcast_to(b[None], (NSH, b.shape[0]))
    small = jnp.concatenate([a, jnp.zeros((NSH, SMALL_REP - a.shape[1]), F32), b,
                             jnp.zeros((NSH, SMALL_ROWS * LANE - SMALL_REP - b.shape[1]), F32)], axis=1)
    return small.reshape(NSH, SMALL_ROWS, LANE)


MESH = pl.DeviceIdType.MESH
ANY = pl.BlockSpec(memory_space=pl.ANY)
VMEM = pl.BlockSpec(memory_space=pltpu.VMEM)


def _place():
    return lax.axis_index("x"), lax.axis_index("y"), lax.axis_index("c")


def _other_chips(x, y):
    return [(1 - x, y), (x, 1 - y), (1 - x, 1 - y)]


def _rcopy(send_sems, recv_sems, k, src, dst, to):
    return pltpu.make_async_remote_copy(src_ref=src, dst_ref=dst, send_sem=send_sems.at[k], recv_sem=recv_sems.at[k],
                                        device_id=to, device_id_type=MESH)


def _gather_steps(in_refs, out_refs, send_sems, recv_sems):
    n = len(in_refs)
    x, y, c = _place()
    s = 2 * x + y
    chips = _other_chips(x, y)
    copy = functools.partial(_rcopy, send_sems, recv_sems)
    pairs = [(i, j, cx, cy) for i in range(n) for j, (cx, cy) in enumerate(chips)]
    pushes = lambda: [copy(i * 3 + j, in_refs[i].at[c], out_refs[i].at[s, c], (cx, cy, c)) for i, j, cx, cy in pairs]
    landed = lambda i, cx, cy, half: out_refs[i].at[2 * cx + cy, half]
    relays = lambda: [copy(3 * n + i * 3 + j, landed(i, cx, cy, c), landed(i, cx, cy, c), (x, y, 1 - c)) for i, j, cx, cy in pairs]

    def start():
        for cp in pushes():
            cp.start()

    def relay():
        for i, j, cx, cy in pairs:
            copy(i * 3 + j, landed(i, cx, cy, c), landed(i, cx, cy, c), (x, y, c)).wait_recv()
        for cp in relays():
            cp.start()

    def finish():
        for i, j, cx, cy in pairs:
            copy(3 * n + i * 3 + j, landed(i, cx, cy, 1 - c), landed(i, cx, cy, 1 - c), (x, y, c)).wait_recv()
        for cp in pushes() + relays():
            cp.wait_send()

    return start, relay, finish


def _gather_shapes(pieces):
    return [jax.ShapeDtypeStruct((NSH,) + a.shape, a.dtype) for a in pieces]


def ag_layer(name, pieces):
    n = len(pieces)

    def body(*refs):
        for fn in _gather_steps(refs[:n], refs[n:2 * n], *refs[2 * n:]):
            fn()

    return pl.pallas_call(
        body, name=name, in_specs=[ANY] * n, out_specs=[ANY] * n, out_shape=_gather_shapes(pieces),
        scratch_shapes=[pltpu.SemaphoreType.DMA((6 * n,)), pltpu.SemaphoreType.DMA((6 * n,))],
    )(*pieces)


def gather_side(pieces):
    def events(n_steps, in_refs, out_refs, send_sems, recv_sems):
        start, relay, finish = _gather_steps(in_refs, out_refs, send_sems, recv_sems)
        return [(0, start), (max(3 * n_steps // 4, 1), relay), (n_steps - 1, finish)]

    return Side(pieces, _gather_shapes(pieces), 6 * len(pieces), events)


def ag_small(pack):
    def body(x_ref, out_ref, send_sems, recv_sems):
        x, y, c = _place()
        s = 2 * x + y
        chips = _other_chips(x, y)

        def copy(k, src, dst, to):
            return pltpu.make_async_remote_copy(src_ref=src, dst_ref=dst, send_sem=send_sems.at[k], recv_sem=recv_sems.at[k],
                                                device_id=to, device_id_type=MESH)

        out_ref[s] = x_ref[...]
        sends = [copy(j, x_ref, out_ref.at[s], (cx, cy, c)) for j, (cx, cy) in enumerate(chips)]
        for cp in sends:
            cp.start()
        for j, (cx, cy) in enumerate(chips):
            blk = out_ref.at[2 * cx + cy]
            copy(j, blk, blk, (x, y, c)).wait_recv()
        for cp in sends:
            cp.wait_send()

    return pl.pallas_call(
        body, name="ag_small", in_specs=[VMEM], out_specs=VMEM,
        out_shape=jax.ShapeDtypeStruct((NSH,) + pack.shape, pack.dtype),
        scratch_shapes=[pltpu.SemaphoreType.DMA((3,)), pltpu.SemaphoreType.DMA((3,))],
    )(pack)


def rs_pair(name, items):
    ni = len(items[0])

    def body(*refs):
        in_refs = [refs[:ni], refs[ni:2 * ni]]
        recv_refs = refs[2 * ni:3 * ni]
        send_sems, recv_sems = refs[3 * ni:]
        x, y, c = _place()
        copy = functools.partial(_rcopy, send_sems, recv_sems)
        for cc in range(2):
            @pl.when(c == cc)
            def _():
                cps = [copy(i * NSH + s, in_refs[1 - cc][i].at[s], recv_refs[i].at[s], (x, y, 1 - c))
                       for i in range(ni) for s in range(NSH)]
                for cp in cps:
                    cp.start()
                for cp in cps:
                    cp.wait()

    return pl.pallas_call(
        body, name=name, in_specs=[ANY] * (2 * ni), out_specs=[ANY] * ni,
        out_shape=[jax.ShapeDtypeStruct(a.shape, a.dtype) for a in items[0]],
        scratch_shapes=[pltpu.SemaphoreType.DMA((ni * NSH,)), pltpu.SemaphoreType.DMA((ni * NSH,))],
    )(*items[0], *items[1])


def _scatter_steps(a_refs, out_refs, send_sems, recv_sems):
    n = len(a_refs)
    x, y, c = _place()
    s = 2 * x + y
    chips = _other_chips(x, y)
    copy = functools.partial(_rcopy, send_sems, recv_sems)
    pairs = [(i, j, cx, cy) for i in range(n) for j, (cx, cy) in enumerate(chips)]
    sends = lambda: [copy(i * 3 + j, a_refs[i].at[2 * cx + cy], out_refs[i].at[s], (cx, cy, c)) for i, j, cx, cy in pairs]

    def start():
        for cp in sends():
            cp.start()

    def finish():
        for i, j, cx, cy in pairs:
            blk = out_refs[i].at[2 * cx + cy]
            copy(i * 3 + j, blk, blk, (x, y, c)).wait_recv()
        for cp in sends():
            cp.wait_send()

    return start, finish


def rs_chips(name, arrs):
    n = len(arrs)

    def body(*refs):
        for fn in _scatter_steps(refs[:n], refs[n:2 * n], *refs[2 * n:]):
            fn()

    return pl.pallas_call(
        body, name=name, in_specs=[ANY] * n, out_specs=[ANY] * n,
        out_shape=[jax.ShapeDtypeStruct(a.shape, a.dtype) for a in arrs],
        scratch_shapes=[pltpu.SemaphoreType.DMA((3 * n,)), pltpu.SemaphoreType.DMA((3 * n,))],
    )(*arrs)


def scatter_side(arrs):
    def events(n_steps, in_refs, out_refs, send_sems, recv_sems):
        start, finish = _scatter_steps(in_refs, out_refs, send_sems, recv_sems)
        return [(0, start), (n_steps - 1, finish)]

    return Side(arrs, [jax.ShapeDtypeStruct(a.shape, a.dtype) for a in arrs], 3 * len(arrs), events)


def _pair_chunks(rows):
    return 4 if rows % 32 == 0 else (2 if rows % 16 == 0 else 1)


def ag_pair(name, arrs):
    n = len(arrs)
    chunks = [(i, k * (a.shape[0] // _pair_chunks(a.shape[0])), a.shape[0] // _pair_chunks(a.shape[0]))
              for i, a in enumerate(arrs) for k in range(_pair_chunks(a.shape[0]))]

    def body(*refs):
        g_refs, out_refs = refs[:n], refs[n:2 * n]
        send_sems, recv_sems = refs[2 * n:]
        x, y, c = _place()
        give = [_rcopy(send_sems, recv_sems, q, g_refs[i].at[pl.ds(r0, rc)], out_refs[i].at[pl.ds(r0, rc)], (x, y, 1 - c))
                for q, (i, r0, rc) in enumerate(chunks)]
        for cp in give:
            cp.start()
        for cp in give:
            cp.wait()

    return pl.pallas_call(
        body, name=name, in_specs=[ANY] * n, out_specs=[ANY] * n,
        out_shape=[jax.ShapeDtypeStruct(a.shape, a.dtype) for a in arrs],
        scratch_shapes=[pltpu.SemaphoreType.DMA((len(chunks),)), pltpu.SemaphoreType.DMA((len(chunks),))],
    )(*arrs)


def small_allreduce(p):
    def body(p_ref, out_ref, rbuf, send_sems, recv_sems):
        x, y, c = _place()
        me = 4 * x + 2 * y + c
        rbuf[me] = p_ref[2 * x + y]
        flip = lambda v, f: (1 - v) if f else v
        peers = [(flip(x, k >> 2 & 1), flip(y, k >> 1 & 1), flip(c, k & 1)) for k in range(1, 8)]

        def copy(k, src, dst, to):
            return pltpu.make_async_remote_copy(src_ref=src, dst_ref=dst, send_sem=send_sems.at[k], recv_sem=recv_sems.at[k],
                                                device_id=to, device_id_type=MESH)

        sends = [copy(k, p_ref.at[2 * px + py], rbuf.at[me], (px, py, pc)) for k, (px, py, pc) in enumerate(peers)]
        for cp in sends:
            cp.start()
        for k, (px, py, pc) in enumerate(peers):
            blk = rbuf.at[4 * px + 2 * py + pc]
            copy(k, blk, blk, (x, y, c)).wait_recv()
        for cp in sends:
            cp.wait_send()
        acc = rbuf[0]
        for d in range(1, 8):
            acc = acc + rbuf[d]
        out_ref[...] = acc

    return pl.pallas_call(
        body, name="small_allreduce", in_specs=[VMEM], out_specs=VMEM,
        out_shape=jax.ShapeDtypeStruct(p.shape[1:], F32),
        scratch_shapes=[pltpu.VMEM((8,) + p.shape[1:], F32), pltpu.SemaphoreType.DMA((7,)), pltpu.SemaphoreType.DMA((7,))],
    )(p)


def _rows_tile(rows, cap):
    return _pick(rows, cap) if rows % 128 == 0 else rows


def sum_pair(name, a0, a1, recv, cflag):
    n, r, d = recv.shape
    tr = _pick(r, 1024) if r % 64 == 0 else r

    def body(c_ref, a0_ref, a1_ref, b_ref, o_ref):
        own = jnp.where(c_ref[0] == 0, a0_ref[...].astype(F32), a1_ref[...].astype(F32))
        o_ref[...] = (own + b_ref[...].astype(F32)).astype(o_ref.dtype)

    spec = pl.BlockSpec((None, tr, d), lambda s, i: (s, i, 0))
    return pl.pallas_call(
        body, name=name, grid=(n, r // tr), in_specs=[pl.BlockSpec(memory_space=pltpu.SMEM), spec, spec, spec],
        out_specs=spec, out_shape=jax.ShapeDtypeStruct(recv.shape, BF16), compiler_params=_cparams(("parallel", "parallel")),
    )(cflag, a0, a1, recv)


def sum_chips(name, parts, own, sflag):
    n, r, d = parts.shape
    tr = _pick(r, 1024) if r % 64 == 0 else r

    def body(s_ref, p_ref, a_ref, o_ref):
        acc = None
        for t in range(n):
            term = jnp.where(s_ref[0] == t, a_ref[t].astype(F32), p_ref[t].astype(F32))
            acc = term if acc is None else acc + term
        o_ref[...] = acc

    spec = pl.BlockSpec((n, tr, d), lambda i: (0, i, 0))
    return pl.pallas_call(
        body, name=name, grid=(r // tr,), in_specs=[pl.BlockSpec(memory_space=pltpu.SMEM), spec, spec],
        out_specs=pl.BlockSpec((tr, d), lambda i: (i, 0)), out_shape=jax.ShapeDtypeStruct((r, d), F32),
        compiler_params=_cparams(("parallel",)),
    )(sflag, parts, own)


ADAM_LR, ADAM_B1, ADAM_B2, ADAM_EPS, ADAM_WD, ADAM_STEP = 0.001, 0.9, 0.999, 1e-08, 0.01, 10


def adamw_call(name, w, g, m, v):
    rows, cols = w.shape
    tr = _rows_tile(rows, 512)

    def body(w_ref, g_ref, m_ref, v_ref, d_ref, nm_ref, nv_ref):
        g_ = g_ref[...]
        m_ = ADAM_B1 * m_ref[...] + (1.0 - ADAM_B1) * g_
        v_ = ADAM_B2 * v_ref[...] + (1.0 - ADAM_B2) * (g_ * g_)
        m_hat = m_ / (1.0 - ADAM_B1 ** ADAM_STEP)
        v_hat = v_ / (1.0 - ADAM_B2 ** ADAM_STEP)
        d_ref[...] = -ADAM_LR * (m_hat / (jnp.sqrt(v_hat) + ADAM_EPS) + ADAM_WD * w_ref[...])
        nm_ref[...] = m_
        nv_ref[...] = v_

    spec = pl.BlockSpec((tr, cols), lambda i: (i, 0))
    sh = jax.ShapeDtypeStruct((rows, cols), F32)
    return pl.pallas_call(
        body, name=name, grid=(rows // tr,), in_specs=[spec] * 4, out_specs=[spec] * 3, out_shape=[sh] * 3,
        compiler_params=_cparams(("parallel",)),
    )(w, g, m, v)


def kernel(x, meta_tokens, norm_w, ffn_w_gate, ffn_w_up, ffn_w_down, rel_bias_table, even_w_in, even_conv_w, swa_sinks, dn_a_log, dn_dt_bias, dn_norm_w, even_w_out, odd_w_in, gla_w_gate_up, gla_b_gate, gla_norm_w, odd_w_out, loss_target, m_meta_tokens, m_norm_w, m_ffn_w_gate, m_ffn_w_up, m_ffn_w_down, m_rel_bias_table, m_even_w_in, m_even_conv_w, m_swa_sinks, m_dn_a_log, m_dn_dt_bias, m_dn_norm_w, m_even_w_out, m_odd_w_in, m_gla_w_gate_up, m_gla_b_gate, m_gla_norm_w, m_odd_w_out, v_meta_tokens, v_norm_w, v_ffn_w_gate, v_ffn_w_up, v_ffn_w_down, v_rel_bias_table, v_even_w_in, v_even_conv_w, v_swa_sinks, v_dn_a_log, v_dn_dt_bias, v_dn_norm_w, v_even_w_out, v_odd_w_in, v_gla_w_gate_up, v_gla_b_gate, v_gla_norm_w, v_odd_w_out):
    ws = [meta_tokens, norm_w, ffn_w_gate, ffn_w_up, ffn_w_down, rel_bias_table, even_w_in, even_conv_w, swa_sinks, dn_a_log,
          dn_dt_bias, dn_norm_w, even_w_out, odd_w_in, gla_w_gate_up, gla_b_gate, gla_norm_w, odd_w_out]
    ms = [m_meta_tokens, m_norm_w, m_ffn_w_gate, m_ffn_w_up, m_ffn_w_down, m_rel_bias_table, m_even_w_in, m_even_conv_w,
          m_swa_sinks, m_dn_a_log, m_dn_dt_bias, m_dn_norm_w, m_even_w_out, m_odd_w_in, m_gla_w_gate_up, m_gla_b_gate,
          m_gla_norm_w, m_odd_w_out]
    vs = [v_meta_tokens, v_norm_w, v_ffn_w_gate, v_ffn_w_up, v_ffn_w_down, v_rel_bias_table, v_even_w_in, v_even_conv_w,
          v_swa_sinks, v_dn_a_log, v_dn_dt_bias, v_dn_norm_w, v_even_w_out, v_odd_w_in, v_gla_w_gate_up, v_gla_b_gate,
          v_gla_norm_w, v_odd_w_out]
    short = [n for n, _ in NAMES]
    w = dict(zip(short, ws))
    m = dict(zip(short, ms))
    v = dict(zip(short, vs))

    wt = {n: _to_t(n, w[n]) for n in BIG}
    own = {n: wt[n].astype(BF16) for n in BIG}
    sflag = (2 * lax.axis_index("x") + lax.axis_index("y")).astype(jnp.int32).reshape(1)
    cflag = lax.axis_index("c").astype(jnp.int32).reshape(1)
    is0 = cflag[0] == 0
    fill = lambda got, pieces: [lax.dynamic_update_index_in_dim(g_, p_, sflag[0], 0) for g_, p_ in zip(got, pieces)]
    pieces = [weight_pieces(own, l) for l in range(2)]
    small = small_from_gathered(ag_small(pack_small(w)))
    ffn0 = fill(ag_layer("ag_layer_0", pieces[0][:3]), pieces[0][:3])
    W = {**small, **{n: w[n] for n, _ in REPL},
         "layers": [{"ffn_g": ffn0[0], "ffn_u": ffn0[1], "ffn_d": ffn0[2]}, None]}

    def reduce_finish(l, mine, parts):
        red = [sum_chips(f"sum_chips_{l}{i}", p, a, sflag) for i, (p, a) in enumerate(zip(parts, mine))]
        got = ag_pair(f"ag_pair_{l}", red)
        return [jnp.where(is0, r_, g_) for r_, g_ in zip(red, got)], [jnp.where(is0, g_, r_) for r_, g_ in zip(red, got)]

    class Exchanges:
        gathers = {"ffn_fwd_00": (0, (3, 4)), "swa_fwd": (1, (0,)), "dn_fwd": (1, (1, 4)), "ffn_fwd_01": (1, (2, 3))}
        scatters = {"ffn_bx_01": (1, (0,)), "dn_bwd": (1, (1,)), "swa_bwd": (1, (2, 3, 4)), "ffn_bx_00": (0, (3, 4))}

        def __init__(self):
            self.q = [[None] * 5, [None] * 5]
            self.mine = [[None] * 5, [None] * 5]
            self.parts = [[None] * 5, [None] * 5]

        def side(self, name):
            if name in self.gathers:
                l, units = self.gathers[name]
                return gather_side([pieces[l][i] for i in units])
            if name in self.scatters:
                l, units = self.scatters[name]
                return scatter_side([self.mine[l][i] for i in units])
            return None

        def done(self, name, outs):
            if name in self.gathers:
                l, units = self.gathers[name]
                for i, a in zip(units, fill(outs, [pieces[l][i] for i in units])):
                    self.q[l][i] = a
            if name in self.scatters:
                l, units = self.scatters[name]
                for i, a in zip(units, outs):
                    self.parts[l][i] = a

        def layer0_proj(self):
            return proj_weights(0, self.q[0][3], self.q[0][4])

        def layer1(self):
            return layer_weights(1, self.q[1])

        def grads(self, l, units, items):
            units = list(units)
            recv = rs_pair(f"rs_pair_{l}{units[0]}", [[items[j][i] for i in units] for j in range(2)])
            for k, i in enumerate(units):
                self.mine[l][i] = sum_pair(f"sum_pair_{l}{i}", items[0][i], items[1][i], recv[k], cflag)

    ex = Exchanges()
    loss_blk, gx, G = core_step(x[0], loss_target[0], W, comm=ex)

    ex.grads(0, (0, 1, 2), G["items"][0])
    ex.parts[0][:3] = rs_chips("rs_chips_0", ex.mine[0][:3])
    lay0 = assemble_layer(0, *reduce_finish(0, ex.mine[0], ex.parts[0]))
    lay1 = assemble_layer(1, *reduce_finish(1, ex.mine[1], ex.parts[1]))
    gt = big_grads(lay0, lay1)
    g_small_pack = small_allreduce(pack_small_grads(G))
    g = {**{n: _from_t(n, gt[n]) for n in BIG}, **unpack_small(g_small_pack)}

    delta, new_m, new_v = {}, {}, {}
    for n in BIG:
        shp = wt[n].shape
        two = lambda t: t.reshape(-1, D)
        d_, m_, v_ = adamw_call("adamw_" + n, two(wt[n]), two(gt[n]), two(_to_t(n, m[n])), two(_to_t(n, v[n])))
        delta[n], new_m[n], new_v[n] = (_from_t(n, t.reshape(shp)) for t in (d_, m_, v_))
    d_, m_, v_ = adamw_call("adamw_small", pack_small(w), g_small_pack, pack_small(m), pack_small(v))
    delta.update(unpack_small(d_))
    new_m.update(unpack_small(m_))
    new_v.update(unpack_small(v_))

    loss = lax.psum(loss_blk[0, 0], ("x", "y", "c"))
    return (loss, gx[None], *[g[n] for n in short], *[delta[n] for n in short], *[new_m[n] for n in short],
            *[new_v[n] for n in short])
```

```python
import functools
import math

import numpy as np
import jax
import jax.numpy as jnp
from jax import lax
from jax.experimental import pallas as pl
from jax.experimental.pallas import tpu as pltpu

F32 = jnp.float32
BF16 = jnp.bfloat16
HI = lax.Precision.HIGHEST

D = 1024
N_META = 16
PADR = 128
ZROWS = PADR - N_META
D_FF = 2816
NSH = 4
FSH = D_FF // NSH
EPS = 1e-6
NEG = -1e30
CH = 64
CPS = 2
BLK = 128
LANE = 128
VMEM_LIMIT = 56 * 1024 * 1024
FFN_SUB = 4

E_QA, E_KA, E_VA, E_QB, E_KB, E_VB, E_ZB, E_BA, E_END = 0, 1024, 1280, 1536, 2048, 2560, 3072, 3584, 4096


def _even_in_map():
    m = np.full((E_END,), -1, np.int64)
    for h in range(8):
        m[E_QA + h * 128:E_QA + h * 128 + 64] = np.arange(h * 64, (h + 1) * 64)
    for h in range(2):
        m[E_KA + h * 128:E_KA + h * 128 + 64] = 512 + np.arange(h * 64, (h + 1) * 64)
        m[E_VA + h * 128:E_VA + h * 128 + 64] = 640 + np.arange(h * 64, (h + 1) * 64)
    m[E_QB:E_QB + 2048] = 768 + np.arange(2048)
    m[E_BA:E_BA + 8] = 2816 + np.arange(8)
    return m


def _even_out_map():
    m = np.full((1536,), -1, np.int64)
    for h in range(8):
        m[h * 128:h * 128 + 64] = np.arange(h * 64, (h + 1) * 64)
    m[1024:1536] = 512 + np.arange(512)
    return m


O_Q, O_K, O_V, O_G, O_GK, O_END = 0, 512, 1024, 2048, 3072, 3584


def _odd_in_map():
    m = np.full((O_END,), -1, np.int64)
    m[:3072] = np.arange(3072)
    m[O_GK:O_GK + 16] = 3072 + np.arange(16)
    return m


def _inverse(m, n):
    inv = np.zeros((n,), np.int64)
    for p, o in enumerate(m):
        if o >= 0:
            inv[o] = p
    return inv


def _take_pad(w, m, axis):
    t = jnp.take(w, jnp.asarray(np.maximum(m, 0)), axis=axis)
    shape = [1] * w.ndim
    shape[axis] = m.shape[0]
    return jnp.where(jnp.asarray(m >= 0).reshape(shape), t, jnp.zeros((), w.dtype))


def _mm(a, b, prec=HI):
    return lax.dot_general(a, b, (((1,), (0,)), ((), ())), precision=prec, preferred_element_type=F32)


def _mm_nt(a, b, prec=HI):
    return lax.dot_general(a, b, (((1,), (1,)), ((), ())), precision=prec, preferred_element_type=F32)


def _mm_tn(a, b, prec=HI):
    return lax.dot_general(a, b, (((0,), (0,)), ((), ())), precision=prec, preferred_element_type=F32)


def _bdot(a, b, dims):
    return lax.dot_general(a.astype(BF16), b.astype(BF16), (dims, ((), ())), preferred_element_type=F32)


@jax.custom_vjp
def _bmm(a, b):
    return _bdot(a, b, ((1,), (0,)))


@jax.custom_vjp
def _bmm_nt(a, b):
    return _bdot(a, b, ((1,), (1,)))


@jax.custom_vjp
def _bmm_tn(a, b):
    return _bdot(a, b, ((0,), (0,)))


_bmm.defvjp(lambda a, b: (_bmm(a, b), (a, b)), lambda r, g: (_bmm_nt(g, r[1]), _bmm_tn(r[0], g)))
_bmm_nt.defvjp(lambda a, b: (_bmm_nt(a, b), (a, b)), lambda r, g: (_bmm(g, r[1]), _bmm_tn(g, r[0])))
_bmm_tn.defvjp(lambda a, b: (_bmm_tn(a, b), (a, b)), lambda r, g: (_bmm_nt(r[1], g), _bmm(r[0], g)))


def _hi_lo(x):
    h = x.astype(BF16)
    return h, (x - h.astype(F32)).astype(BF16)


def _xdot(a, b, dims):
    ah, al = _hi_lo(a)
    bh, bl = _hi_lo(b)
    d = lambda p, q: lax.dot_general(p, q, (dims, ((), ())), preferred_element_type=F32)
    return d(ah, bh) + (d(ah, bl) + d(al, bh))


@jax.custom_vjp
def _xmm(a, b):
    return _xdot(a, b, ((1,), (0,)))


@jax.custom_vjp
def _xmm_nt(a, b):
    return _xdot(a, b, ((1,), (1,)))


@jax.custom_vjp
def _xmm_tn(a, b):
    return _xdot(a, b, ((0,), (0,)))


_xmm.defvjp(lambda a, b: (_xmm(a, b), (a, b)), lambda r, g: (_xmm_nt(g, r[1]), _xmm_tn(r[0], g)))
_xmm_nt.defvjp(lambda a, b: (_xmm_nt(a, b), (a, b)), lambda r, g: (_xmm(g, r[1]), _xmm_tn(g, r[0])))
_xmm_tn.defvjp(lambda a, b: (_xmm_tn(a, b), (a, b)), lambda r, g: (_xmm_nt(r[1], g), _xmm(r[0], g)))


def _sum01(m01, x, dims):
    h, l = _hi_lo(x)
    l2 = (x - h.astype(F32) - l.astype(F32)).astype(BF16)
    m = m01.astype(BF16)
    d = lambda q: lax.dot_general(m, q, (dims, ((), ())), preferred_element_type=F32)
    return d(h) + (d(l) + d(l2))


@jax.custom_vjp
def _cumsum_rows(x):
    n = x.shape[0]
    tri = lax.broadcasted_iota(jnp.int32, (n, n), 0) >= lax.broadcasted_iota(jnp.int32, (n, n), 1)
    return _sum01(tri, x, ((1,), (0,)))


def _cumsum_rows_b(_, g):
    n = g.shape[0]
    tri = lax.broadcasted_iota(jnp.int32, (n, n), 0) >= lax.broadcasted_iota(jnp.int32, (n, n), 1)
    return (_sum01(tri, g, ((0,), (0,))),)


_cumsum_rows.defvjp(lambda x: (_cumsum_rows(x), None), _cumsum_rows_b)


@functools.partial(jax.custom_vjp, nondiff_argnums=(1,))
def _colsum_as_rows(x, width):
    return _colsum_impl(x, width)


def _colsum_impl(x, width):
    h, l = _hi_lo(x)
    l2 = (x - h.astype(F32) - l.astype(F32)).astype(BF16)
    ones = jnp.ones((x.shape[0], width), BF16)
    d = lambda q: lax.dot_general(q, ones, (((0,), (0,)), ((), ())), preferred_element_type=F32)
    return d(h) + (d(l) + d(l2))


def _colsum_as_rows_f(x, width):
    return _colsum_impl(x, width), x.shape[0]


def _colsum_as_rows_b(width, n, g):
    return (_sum01(jnp.ones((n, width), F32), g, ((1,), (1,))),)


_colsum_as_rows.defvjp(_colsum_as_rows_f, _colsum_as_rows_b)


def _rms(x, w):
    return x * lax.rsqrt(jnp.mean(x * x, axis=-1, keepdims=True) + EPS) * w


def _sigmoid(x):
    return 1.0 / (1.0 + jnp.exp(-x))


def _silu(x):
    return x * _sigmoid(x)


def _softplus(x):
    return jnp.maximum(x, 0.0) + jnp.log(1.0 + jnp.exp(-jnp.abs(x)))


def _lane_pick(row, idx):
    lane = lax.broadcasted_iota(jnp.int32, row.shape, row.ndim - 1)
    return jnp.sum(jnp.where(lane == idx, row, 0.0), axis=-1, keepdims=True)


def _row_ids(row0, n):
    return row0 + lax.broadcasted_iota(jnp.int32, (n, 1), 0)


def _pick(m, cap):
    best = 64
    for t in range(64, min(m, cap) + 1, 64):
        if m % t == 0:
            best = t
    return best


def _cparams(sem):
    return pltpu.CompilerParams(dimension_semantics=sem, vmem_limit_bytes=VMEM_LIMIT)


def mm_nn(a, b, name, out_dtype=F32):
    M, K = a.shape
    N = b.shape[1]
    tm = _pick(M, 1408 if K <= 2048 else 704)
    tn = _pick(N, 512)

    def body(a_ref, b_ref, o_ref):
        o_ref[...] = _mm(a_ref[...], b_ref[...], None).astype(o_ref.dtype)

    return pl.pallas_call(
        body, name=name, grid=(N // tn, M // tm),
        in_specs=[pl.BlockSpec((tm, K), lambda j, i: (i, 0)), pl.BlockSpec((K, tn), lambda j, i: (0, j))],
        out_specs=pl.BlockSpec((tm, tn), lambda j, i: (i, j)),
        out_shape=jax.ShapeDtypeStruct((M, N), out_dtype),
        compiler_params=_cparams(("parallel", "parallel")),
    )(a, b)


def mm_nt(a, b, name, out_dtype=F32):
    M, K = a.shape
    N = b.shape[0]
    tm = _pick(M, 768)
    tn = _pick(N, 512)

    def body(a_ref, b_ref, o_ref):
        o_ref[...] = _mm_nt(a_ref[...], b_ref[...], None).astype(o_ref.dtype)

    return pl.pallas_call(
        body, name=name, grid=(N // tn, M // tm),
        in_specs=[pl.BlockSpec((tm, K), lambda j, i: (i, 0)), pl.BlockSpec((tn, K), lambda j, i: (j, 0))],
        out_specs=pl.BlockSpec((tm, tn), lambda j, i: (i, j)),
        out_shape=jax.ShapeDtypeStruct((M, N), out_dtype),
        compiler_params=_cparams(("parallel", "parallel")),
    )(a, b)


def mm_tn(a, b, name):
    M, K = a.shape
    N = b.shape[1]
    tk = _pick(K, 512)
    tn = _pick(N, 512)

    def body(a_ref, b_ref, o_ref):
        o_ref[...] = _mm_tn(a_ref[...], b_ref[...], None)

    return pl.pallas_call(
        body, name=name, grid=(K // tk, N // tn),
        in_specs=[pl.BlockSpec((M, tk), lambda i, j: (0, i)), pl.BlockSpec((M, tn), lambda i, j: (0, j))],
        out_specs=pl.BlockSpec((tk, tn), lambda i, j: (i, j)),
        out_shape=jax.ShapeDtypeStruct((K, N), F32),
        compiler_params=_cparams(("parallel", "parallel")),
    )(a, b)


def _row_specs(rows, tm):
    return [pl.BlockSpec((tm, w), functools.partial(lambda i, cb: (i, cb), cb=cb)) for (_, w, cb) in rows]


def _param_specs(params):
    return [pl.BlockSpec(p.shape, functools.partial(lambda i, nd: (0,) * nd, nd=p.ndim)) for p in params]


def rowwise_fwd(name, fn, rows, params, outs, tm=None):
    M = rows[0][0].shape[0]
    tm = tm or _pick(M, 704)
    nr, npar = len(rows), len(params)

    def body(*refs):
        row0 = pl.program_id(0) * tm
        vals = [r[...].astype(F32) for r in refs[:nr]] + [p[...] for p in refs[nr:nr + npar]]
        res = fn(row0, *vals)
        for o_ref, r in zip(refs[nr + npar:], res):
            o_ref[...] = r.astype(o_ref.dtype)

    return pl.pallas_call(
        body, name=name, grid=(M // tm,),
        in_specs=_row_specs(rows, tm) + _param_specs(params),
        out_specs=[pl.BlockSpec((tm, w), lambda i: (i, 0)) for (w, _) in outs],
        out_shape=[jax.ShapeDtypeStruct((M, w), dt) for (w, dt) in outs],
        compiler_params=_cparams(("parallel",)),
    )(*[r[0] for r in rows], *params)


def rowwise_bwd(name, fn, rows, params, douts, drow_dtypes, tm=None):
    M = rows[0][0].shape[0]
    tm = tm or _pick(M, 704)
    nr, npar, nd = len(rows), len(params), len(douts)
    want = [k for k, dt in enumerate(drow_dtypes) if dt is not None]

    def body(*refs):
        i = pl.program_id(0)
        row0 = i * tm
        vals = [r[...].astype(F32) for r in refs[:nr]] + [p[...] for p in refs[nr:nr + npar]]
        cots = tuple(d[...].astype(F32) for d in refs[nr + npar:nr + npar + nd])
        _, vjp = jax.vjp(functools.partial(fn, row0), *vals)
        grads = vjp(cots)
        o_refs = refs[nr + npar + nd:]
        for o_ref, k in zip(o_refs[:len(want)], want):
            o_ref[...] = grads[k].astype(o_ref.dtype)
        for o_ref, g in zip(o_refs[len(want):], grads[nr:]):
            @pl.when(i == 0)
            def _():
                o_ref[...] = g

            @pl.when(i > 0)
            def _():
                o_ref[...] += g

    res = pl.pallas_call(
        body, name=name, grid=(M // tm,),
        in_specs=_row_specs(rows, tm) + _param_specs(params) + _row_specs(douts, tm),
        out_specs=[pl.BlockSpec((tm, rows[k][1]), lambda i: (i, 0)) for k in want] + _param_specs(params),
        out_shape=[jax.ShapeDtypeStruct((M, rows[k][1]), drow_dtypes[k]) for k in want]
        + [jax.ShapeDtypeStruct(p.shape, F32) for p in params],
        compiler_params=_cparams(("arbitrary",)),
    )(*[r[0] for r in rows], *params, *[d[0] for d in douts])
    return res[:len(want)], res[len(want):]


def _fn_prenorm(row0, h, wpre):
    return (_rms(h, wpre),)


def _fn_resnorm(scale, row0, h, f, wpost, wpre):
    h2 = h + scale * _rms(f, wpost)
    return h2, _rms(h2, wpre)


def _fn_res_last(scale, row0, h, f, wpost):
    return (h + scale * _rms(f, wpost),)


def ffn_fwd(name, xn, wg, wu, wd, side=None):
    M = xn.shape[0]
    tm = _pick(M, 704)
    s_ins, s_specs, s_shapes, s_sems = _side_parts(side)
    ns, nso = len(s_ins), len(s_shapes)

    def body(x_ref, wg_ref, wu_ref, wd_ref, *rest):
        f_ref, a_ref, b_ref = rest[ns:ns + 3]
        s = pl.program_id(1)
        _side_run(side, (M // tm) * NSH, rest[:ns], rest[ns + 3:ns + 3 + nso], rest[ns + 3 + nso:],
                  step=pl.program_id(0) * NSH + s)
        x = x_ref[...]
        a = _mm_nt(x, wg_ref[...], None)
        b = _mm_nt(x, wu_ref[...], None)
        c = _mm((_silu(a) * b).astype(BF16), wd_ref[...], None)

        @pl.when(s == 0)
        def _():
            f_ref[...] = c

        @pl.when(s > 0)
        def _():
            f_ref[...] += c

        a_ref[...] = a.astype(BF16)
        b_ref[...] = b.astype(BF16)

    wspec = wdspec = pl.BlockSpec((None, FSH, D), lambda i, s: (s, 0, 0))
    abspec = pl.BlockSpec((None, tm, FSH), lambda i, s: (s, i, 0))
    res = pl.pallas_call(
        body, name=name, grid=(M // tm, NSH),
        in_specs=[pl.BlockSpec((tm, D), lambda i, s: (i, 0)), wspec, wspec, wdspec] + s_specs,
        out_specs=[pl.BlockSpec((tm, D), lambda i, s: (i, 0)), abspec, abspec] + [pl.BlockSpec(memory_space=pl.ANY)] * nso,
        out_shape=[jax.ShapeDtypeStruct((M, D), F32), jax.ShapeDtypeStruct((NSH, M, FSH), BF16),
                   jax.ShapeDtypeStruct((NSH, M, FSH), BF16)] + s_shapes,
        scratch_shapes=s_sems,
        compiler_params=_cparams(("arbitrary", "arbitrary")),
    )(xn, wg, wu, wd, *s_ins)
    return res[:3], res[3:]


def ffn_bwd_x(name, df, a, b, wg, wu, wd, side=None):
    M = df.shape[0]
    tm = _pick(M, 704)
    ts = tm // FFN_SUB
    s_ins, s_specs, s_shapes, s_sems = _side_parts(side)
    ns, nso = len(s_ins), len(s_shapes)

    def body(df_ref, a_ref, b_ref, wg_ref, wu_ref, wd_ref, *rest):
        dx_ref, da_ref, db_ref, hm_ref = rest[ns:ns + 4]
        _side_run(side, (M // tm) * NSH, rest[:ns], rest[ns + 4:ns + 4 + nso], rest[ns + 4 + nso:],
                  step=pl.program_id(0) * NSH + pl.program_id(1))

        @pl.when(pl.program_id(1) == 0)
        def _():
            dx_ref[...] = jnp.zeros_like(dx_ref)

        for r in range(FFN_SUB):
            rows = pl.ds(r * ts, ts)
            a_ = a_ref[rows, :].astype(F32)
            b_ = b_ref[rows, :].astype(F32)
            dh = _mm_nt(df_ref[rows, :], wd_ref[...], None)
            sig = _sigmoid(a_)
            sil = a_ * sig
            da = (dh * b_ * (sig * (1.0 + a_ * (1.0 - sig)))).astype(BF16)
            db = (dh * sil).astype(BF16)
            dx_ref[rows, :] += _mm(da, wg_ref[...], None) + _mm(db, wu_ref[...], None)
            da_ref[rows, :] = da
            db_ref[rows, :] = db
            hm_ref[rows, :] = (sil * b_).astype(BF16)

    wspec = wdspec = pl.BlockSpec((None, FSH, D), lambda i, s: (s, 0, 0))
    abspec = pl.BlockSpec((None, tm, FSH), lambda i, s: (s, i, 0))
    ab = jax.ShapeDtypeStruct((NSH, M, FSH), BF16)
    res = pl.pallas_call(
        body, name=name, grid=(M // tm, NSH),
        in_specs=[pl.BlockSpec((tm, D), lambda i, s: (i, 0)), abspec, abspec, wspec, wspec, wdspec] + s_specs,
        out_specs=[pl.BlockSpec((tm, D), lambda i, s: (i, 0)), abspec, abspec, abspec] + [pl.BlockSpec(memory_space=pl.ANY)] * nso,
        out_shape=[jax.ShapeDtypeStruct((M, D), F32), ab, ab, ab] + s_shapes,
        scratch_shapes=s_sems,
        compiler_params=_cparams(("arbitrary", "arbitrary")),
    )(df, a, b, wg, wu, wd, *s_ins)
    return res[:4], res[4:]


def ffn_bwd_w(name, xn, df, da, db, hm):
    M = xn.shape[0]
    tm = _pick(M, 704)
    nt = M // tm

    def body(x_ref, df_ref, da_ref, db_ref, hm_ref, dwg_ref, dwu_ref, dwd_ref, ag, au, ad):
        i = pl.program_id(1)
        x = x_ref[...]
        g = _mm_tn(da_ref[...], x, None)
        u = _mm_tn(db_ref[...], x, None)
        d = _mm_tn(hm_ref[...], df_ref[...], None)

        @pl.when(i == 0)
        def _():
            ag[...] = g
            au[...] = u
            ad[...] = d

        @pl.when(i > 0)
        def _():
            ag[...] += g
            au[...] += u
            ad[...] += d

        @pl.when(i == nt - 1)
        def _():
            dwg_ref[...] = ag[...].astype(BF16)
            dwu_ref[...] = au[...].astype(BF16)
            dwd_ref[...] = ad[...].astype(BF16)

    xspec = pl.BlockSpec((tm, D), lambda s, i: (i, 0))
    abspec = pl.BlockSpec((None, tm, FSH), lambda s, i: (s, i, 0))
    return pl.pallas_call(
        body, name=name, grid=(NSH, nt),
        in_specs=[xspec, xspec, abspec, abspec, abspec],
        out_specs=[pl.BlockSpec((None, FSH, D), lambda s, i: (s, 0, 0))] * 3,
        out_shape=[jax.ShapeDtypeStruct((NSH, FSH, D), BF16)] * 3,
        scratch_shapes=[pltpu.VMEM((FSH, D), F32)] * 3,
        compiler_params=_cparams(("parallel", "arbitrary")),
    )(xn, df, da, db, hm)


def _t5_bucket_np(rel):
    n = np.maximum(rel, 0)
    n_f = np.maximum(n, 1).astype(np.float32)
    large = 16 + (np.log(n_f / np.float32(16)) / np.float32(math.log(8.0)) * np.float32(16)).astype(np.int32)
    large = np.minimum(large, 31)
    return np.where(n < 16, n, large).astype(np.int32)


def _swa_bucket_ids():
    qi = np.arange(BLK)[:, None]
    kj = np.arange(BLK)[None, :]
    out = np.full((3, BLK, 3 * BLK), -1, np.int32)
    for v in range(3):
        pos_q = v * BLK + qi - ZROWS
        rel_m = pos_q - (kj - ZROWS)
        ok_m = (kj >= ZROWS) & (rel_m >= 0) & (pos_q >= 0)
        out[v, :, 0:BLK] = np.where(ok_m, _t5_bucket_np(rel_m), -1)
        pos_kp = (v - 1) * BLK + kj - ZROWS
        rel_p = BLK + qi - kj
        ok_p = (pos_kp >= N_META) & (rel_p >= 0) & (rel_p < BLK) & np.full_like(ok_m, v >= 1)
        out[v, :, BLK:2 * BLK] = np.where(ok_p, _t5_bucket_np(rel_p), -1)
        pos_kc = v * BLK + kj - ZROWS
        rel_c = qi - kj
        ok_c = (pos_kc >= N_META) & (rel_c >= 0) & (rel_c < BLK)
        out[v, :, 2 * BLK:] = np.where(ok_c, _t5_bucket_np(rel_c), -1)
    return out


def swa_bias_fwd(table, ids):
    def body(t_ref, id_ref, o_ref):
        for v in range(3):
            for h in range(8):
                o_ref[v, h] = jnp.where(id_ref[v] < 0, NEG, 0.0)

            def step(b, carry):
                hit = id_ref[v] == b
                for h in range(8):
                    o_ref[v, h] += jnp.where(hit, t_ref[b, h], 0.0)
                return carry

            lax.fori_loop(0, 32, step, 0)

    return pl.pallas_call(
        body, name="swa_bias_fwd",
        in_specs=[pl.BlockSpec(memory_space=pltpu.SMEM), pl.BlockSpec(memory_space=pltpu.VMEM)],
        out_specs=pl.BlockSpec(memory_space=pltpu.VMEM),
        out_shape=jax.ShapeDtypeStruct((3, 8, BLK, 3 * BLK), F32),
        compiler_params=pltpu.CompilerParams(vmem_limit_bytes=VMEM_LIMIT),
    )(table, ids)


def swa_bias_bwd(dbias, ids):
    def body(d_ref, id_ref, o_ref):
        r = lax.broadcasted_iota(jnp.int32, (32, LANE), 0)
        c = lax.broadcasted_iota(jnp.int32, (32, LANE), 1)

        def step(b, acc):
            for v in range(3):
                hit = id_ref[v] == b
                for h in range(8):
                    m = jnp.where(hit, d_ref[v, h], 0.0)
                    s = jnp.sum(jnp.sum(m, axis=1, keepdims=True), axis=0, keepdims=True)
                    acc = acc + jnp.where((r == b) & (c == h), s, 0.0)
            return acc

        o_ref[...] = lax.fori_loop(0, 32, step, jnp.zeros((32, LANE), F32))

    return pl.pallas_call(
        body, name="swa_bias_bwd",
        in_specs=[pl.BlockSpec(memory_space=pltpu.VMEM), pl.BlockSpec(memory_space=pltpu.VMEM)],
        out_specs=pl.BlockSpec(memory_space=pltpu.VMEM),
        out_shape=jax.ShapeDtypeStruct((32, LANE), F32),
        compiler_params=pltpu.CompilerParams(vmem_limit_bytes=VMEM_LIMIT),
    )(dbias, ids)


def _swa_block(q, k3, v3, bias, sinks):
    heads = range(8)
    kh = [k3[:, (h // 4) * 128:(h // 4 + 1) * 128] for h in heads]
    vh = [v3[:, (h // 4) * 128:(h // 4 + 1) * 128] for h in heads]
    s = [_bmm_nt(q[:, h * 128:(h + 1) * 128], kh[h]) * 0.125 + bias[h] for h in heads]
    sink = [_lane_pick(sinks, h) for h in heads]
    m = [lax.stop_gradient(jnp.maximum(jnp.max(s[h], axis=-1, keepdims=True), sink[h])) for h in heads]
    e = [jnp.exp(s[h] - m[h]) for h in heads]
    p = [e[h] / (jnp.sum(e[h], axis=-1, keepdims=True) + jnp.exp(sink[h] - m[h])) for h in heads]
    return jnp.concatenate([_bmm(p[h], vh[h]) for h in heads], axis=1)


def _swa_in_specs():
    qs = pl.BlockSpec((BLK, 1024), lambda n: (n, E_QA // 1024))
    ks = [pl.BlockSpec((BLK, 256), lambda n: (0, E_KA // 256)),
          pl.BlockSpec((BLK, 256), lambda n: (jnp.maximum(n - 1, 0), E_KA // 256)),
          pl.BlockSpec((BLK, 256), lambda n: (n, E_KA // 256))]
    vs = [pl.BlockSpec((BLK, 256), lambda n: (0, E_VA // 256)),
          pl.BlockSpec((BLK, 256), lambda n: (jnp.maximum(n - 1, 0), E_VA // 256)),
          pl.BlockSpec((BLK, 256), lambda n: (n, E_VA // 256))]
    bs = pl.BlockSpec((None, 8, BLK, 3 * BLK), lambda n: (jnp.minimum(n, 2), 0, 0, 0))
    ss = pl.BlockSpec((1, LANE), lambda n: (0, 0))
    return [qs] + ks + vs + [bs, ss]


def swa_fwd(proj, bias, sinks, side=None):
    M = proj.shape[0]
    s_ins, s_specs, s_shapes, s_sems = _side_parts(side)
    ns, nso = len(s_ins), len(s_shapes)

    def body(q_ref, k0, k1, k2, v0, v1, v2, b_ref, s_ref, *rest):
        o_ref = rest[ns]
        _side_run(side, M // BLK, rest[:ns], rest[ns + 1:ns + 1 + nso], rest[ns + 1 + nso:])
        k3 = jnp.concatenate([k0[...], k1[...], k2[...]], axis=0)
        v3 = jnp.concatenate([v0[...], v1[...], v2[...]], axis=0)
        o_ref[...] = _swa_block(q_ref[...], k3, v3, b_ref[...], s_ref[...]).astype(o_ref.dtype)

    res = pl.pallas_call(
        body, name="swa_fwd", grid=(M // BLK,),
        in_specs=_swa_in_specs() + s_specs,
        out_specs=[pl.BlockSpec((BLK, 1024), lambda n: (n, 0))] + [pl.BlockSpec(memory_space=pl.ANY)] * nso,
        out_shape=[jax.ShapeDtypeStruct((M, 1024), BF16)] + s_shapes,
        scratch_shapes=s_sems,
        compiler_params=_cparams(("arbitrary",)),
    )(proj, proj, proj, proj, proj, proj, proj, bias, sinks, *s_ins)
    return res[0], res[1:]


def swa_bwd(proj, bias, sinks, do, side=None):
    M = proj.shape[0]
    s_ins, s_specs, s_shapes, s_sems = _side_parts(side)
    ns, nso = len(s_ins), len(s_shapes)

    def body(q_ref, k0, k1, k2, v0, v1, v2, b_ref, s_ref, do_ref, *rest):
        dq_ref, dk_ref, dv_ref, db_ref, ds_ref = rest[ns:ns + 5]
        _side_run(side, M // BLK, rest[:ns], rest[ns + 5:ns + 5 + nso], rest[ns + 5 + nso:])
        n = pl.program_id(0)

        @pl.when(n == 0)
        def _():
            dk_ref[...] = jnp.zeros_like(dk_ref)
            dv_ref[...] = jnp.zeros_like(dv_ref)
            ds_ref[...] = jnp.zeros_like(ds_ref)

        k3 = jnp.concatenate([k0[...], k1[...], k2[...]], axis=0)
        v3 = jnp.concatenate([v0[...], v1[...], v2[...]], axis=0)
        _, vjp = jax.vjp(_swa_block, q_ref[...], k3, v3, b_ref[...], s_ref[...])
        dq, dk3, dv3, dbias, dsink = vjp(do_ref[...].astype(F32))
        dq_ref[...] = dq
        prev = pl.multiple_of(jnp.maximum(n - 1, 0) * BLK, BLK)
        cur = pl.multiple_of(n * BLK, BLK)
        dk_ref[pl.ds(0, BLK), :] += dk3[0:BLK]
        dv_ref[pl.ds(0, BLK), :] += dv3[0:BLK]
        dk_ref[pl.ds(prev, BLK), :] += dk3[BLK:2 * BLK]
        dv_ref[pl.ds(prev, BLK), :] += dv3[BLK:2 * BLK]
        dk_ref[pl.ds(cur, BLK), :] += dk3[2 * BLK:]
        dv_ref[pl.ds(cur, BLK), :] += dv3[2 * BLK:]
        ds_ref[...] += dsink

        @pl.when(n <= 2)
        def _():
            db_ref[...] = dbias

        @pl.when(n > 2)
        def _():
            db_ref[...] += dbias

    res = pl.pallas_call(
        body, name="swa_bwd", grid=(M // BLK,),
        in_specs=_swa_in_specs() + [pl.BlockSpec((BLK, 1024), lambda n: (n, 0))] + s_specs,
        out_specs=[pl.BlockSpec((BLK, 1024), lambda n: (n, 0)),
                   pl.BlockSpec((M, 256), lambda n: (0, 0)), pl.BlockSpec((M, 256), lambda n: (0, 0)),
                   pl.BlockSpec((None, 8, BLK, 3 * BLK), lambda n: (jnp.minimum(n, 2), 0, 0, 0)),
                   pl.BlockSpec((1, LANE), lambda n: (0, 0))] + [pl.BlockSpec(memory_space=pl.ANY)] * nso,
        out_shape=[jax.ShapeDtypeStruct((M, 1024), F32), jax.ShapeDtypeStruct((M, 256), F32),
                   jax.ShapeDtypeStruct((M, 256), F32), jax.ShapeDtypeStruct((3, 8, BLK, 3 * BLK), F32),
                   jax.ShapeDtypeStruct((1, LANE), F32)] + s_shapes,
        scratch_shapes=s_sems,
        compiler_params=_cparams(("arbitrary",)),
    )(proj, proj, proj, proj, proj, proj, proj, bias, sinks, do, *s_ins)
    return res[:5], res[5:]


def _shift_rows_impl(x, k):
    n = x.shape[0]
    rolled = pltpu.roll(x, k, 0)
    return jnp.where(_row_ids(0, n) >= k, rolled, 0.0)


def _unshift_rows_impl(g, k):
    n = g.shape[0]
    rolled = pltpu.roll(g, n - k, 0)
    return jnp.where(_row_ids(0, n) < n - k, rolled, 0.0)


@functools.partial(jax.custom_vjp, nondiff_argnums=(1,))
def _shift_rows(x, k):
    return _shift_rows_impl(x, k)


def _shift_rows_f(x, k):
    return _shift_rows_impl(x, k), None


def _shift_rows_b(k, _, g):
    return (_unshift_rows_impl(g, k),)


_shift_rows.defvjp(_shift_rows_f, _shift_rows_b)


def _conv_silu(x, w):
    rid = lax.broadcasted_iota(jnp.int32, w.shape, 0)
    y = x * jnp.sum(jnp.where(rid == 3, w, 0.0), axis=0, keepdims=True)
    for k in range(1, 4):
        y = y + _shift_rows(x, k) * jnp.sum(jnp.where(rid == 3 - k, w, 0.0), axis=0, keepdims=True)
    y = jnp.where(_row_ids(0, x.shape[0]) >= ZROWS, y, 0.0)
    return _silu(y)


def conv_fwd(proj, conv_w):
    M = proj.shape[0]
    nb = conv_w.shape[1] // LANE

    def body(x_ref, w_ref, o_ref):
        o_ref[...] = _conv_silu(x_ref[...], w_ref[...])

    return pl.pallas_call(
        body, name="conv_fwd", grid=(nb,),
        in_specs=[pl.BlockSpec((M, LANE), lambda c: (0, E_QB // LANE + c)), pl.BlockSpec((4, LANE), lambda c: (0, c))],
        out_specs=pl.BlockSpec((M, LANE), lambda c: (0, c)),
        out_shape=jax.ShapeDtypeStruct((M, conv_w.shape[1]), F32),
        compiler_params=_cparams(("parallel",)),
    )(proj, conv_w)


def conv_bwd(proj, conv_w, dy):
    M = proj.shape[0]
    nb = conv_w.shape[1] // LANE

    def body(x_ref, w_ref, dy_ref, dx_ref, dw_ref):
        _, vjp = jax.vjp(_conv_silu, x_ref[...], w_ref[...])
        dx, dw = vjp(dy_ref[...])
        dx_ref[...] = dx
        dw_ref[...] = dw

    return pl.pallas_call(
        body, name="conv_bwd", grid=(nb,),
        in_specs=[pl.BlockSpec((M, LANE), lambda c: (0, E_QB // LANE + c)), pl.BlockSpec((4, LANE), lambda c: (0, c)),
                  pl.BlockSpec((M, LANE), lambda c: (0, c))],
        out_specs=[pl.BlockSpec((M, LANE), lambda c: (0, c)), pl.BlockSpec((4, LANE), lambda c: (0, c))],
        out_shape=[jax.ShapeDtypeStruct((M, conv_w.shape[1]), F32), jax.ShapeDtypeStruct(conv_w.shape, F32)],
        compiler_params=_cparams(("parallel",)),
    )(proj, conv_w, dy)


def _fn_dn_prep(row0, yq, yk, ba, dnp):
    tm = yq.shape[0]
    real = _row_ids(row0, tm) >= ZROWS
    qs, ks, gs, bs = [], [], [], []
    for h in range(4):
        q = yq[:, h * 128:(h + 1) * 128]
        k = yk[:, h * 128:(h + 1) * 128]
        qs.append(q * lax.rsqrt(jnp.sum(q * q, axis=-1, keepdims=True) + 1e-6) * (128.0 ** -0.5))
        ks.append(k * lax.rsqrt(jnp.sum(k * k, axis=-1, keepdims=True) + 1e-6))
        beta = _sigmoid(_lane_pick(ba, h))
        g = -jnp.exp(_lane_pick(dnp, h)) * _softplus(_lane_pick(ba, 4 + h) + _lane_pick(dnp, 4 + h))
        g = jnp.where(real, g, 0.0)
        gs.append(jnp.broadcast_to(g, (tm, 128)))
        bs.append(jnp.broadcast_to(beta, (tm, 128)))
    cat = lambda xs: jnp.concatenate(xs, axis=1)
    return cat(qs), cat(ks), cat(gs), cat(bs)


def _zip(f, *lists):
    return [f(*args) for args in zip(*lists)]


def _unit_lower_inv_impl(a):
    n = a[0].shape[0]
    eye = (lax.broadcasted_iota(jnp.int32, (n, n), 0) == lax.broadcasted_iota(jnp.int32, (n, n), 1)).astype(F32)
    nn = ((1,), (0,))
    p = [-x for x in a]
    t = [eye + x for x in p]
    for _ in range(int(math.log2(n)) - 1):
        p = _zip(lambda x: _xdot(x, x, nn), p)
        t = _zip(lambda x, y: x + _xdot(x, y, nn), t, p)
    return t


@jax.custom_vjp
def _unit_lower_inv(a):
    return _unit_lower_inv_impl(a)


def _unit_lower_inv_f(a):
    t = _unit_lower_inv_impl(a)
    return t, t


def _unit_lower_inv_b(t, g):
    tg = _zip(lambda x, y: _xdot(x, y, ((0,), (0,))), t, g)
    return (_zip(lambda x, y: -_xdot(x, y, ((1,), (1,))), tg, t),)


_unit_lower_inv.defvjp(_unit_lower_inv_f, _unit_lower_inv_b)


@jax.custom_vjp
def _known_inv(a, t):
    return t


_known_inv.defvjp(lambda a, t: (t, t), lambda t, g: (_unit_lower_inv_b(t, g)[0], [jnp.zeros_like(x) for x in t]))


def _dn_block(q, k, v, gb, bb, S, t_kept=None):
    nh = len(S)
    r = lax.broadcasted_iota(jnp.int32, (CH, CH), 0)
    c = lax.broadcasted_iota(jnp.int32, (CH, CH), 1)
    tri_incl = r >= c
    gcb = _zip(_cumsum_rows, gb)
    gamma = _zip(lambda x: jnp.where(tri_incl, jnp.exp(jnp.where(tri_incl, x[:, :CH] - x[:, :CH].T, 0.0)), 0.0), gcb)
    kb = _zip(jnp.multiply, k, bb)
    vb = _zip(jnp.multiply, v, bb)
    a = _zip(lambda m, g: jnp.where(r > c, m * g, 0.0), _zip(_bmm_nt, kb, k), gamma)
    t = _unit_lower_inv(a) if t_kept is None else _known_inv(a, list(t_kept))
    eg = _zip(jnp.exp, gcb)
    u = _zip(_xmm, t, vb)
    w = _zip(_xmm, t, _zip(jnp.multiply, kb, eg))
    attn = _zip(lambda m, g: m * g, _zip(_bmm_nt, q, k), gamma)
    gtot = _zip(lambda x: jnp.sum(x, axis=0, keepdims=True), gb)
    k_dec = _zip(lambda x, gt, gc: x * jnp.exp(gt - gc), k, gtot, gcb)
    q_dec = _zip(jnp.multiply, q, eg)
    S = list(S)
    o, starts = [], []
    for i0 in range(0, len(q), nh):
        idx = range(i0, i0 + nh)
        starts.append(list(S))
        v_new = [u[i] - m for i, m in zip(idx, [_bmm(w[i], S[h]) for h, i in enumerate(idx)])]
        oq = [_bmm(q_dec[i], S[h]) for h, i in enumerate(idx)]
        oa = [_bmm(attn[i], vn) for i, vn in zip(idx, v_new)]
        kv = [_bmm_tn(k_dec[i], vn) for i, vn in zip(idx, v_new)]
        o += _zip(jnp.add, oq, oa)
        S = [S[h] * jnp.exp(jnp.broadcast_to(gtot[i], S[h].shape)) + kv[h] for h, i in enumerate(idx)]
    return o, S, starts, t


def _gla_block(q, k, v, glog, S):
    nh = len(S)
    tri = lax.broadcasted_iota(jnp.int32, (CH, CH), 0) >= lax.broadcasted_iota(jnp.int32, (CH, CH), 1)
    bcum = _zip(_cumsum_rows, glog)
    q_dec = _zip(lambda x, b: x * (128.0 ** -0.5) * jnp.exp(b), q, bcum)
    attn = _zip(lambda m: jnp.where(tri, m, 0.0), _zip(_bmm_nt, q_dec, _zip(lambda x, b: x * jnp.exp(-b), k, bcum)))
    o_in = _zip(_bmm, attn, v)
    k_dec = _zip(lambda x, g, b: x * jnp.exp(jnp.sum(g, axis=0, keepdims=True) - b), k, glog, bcum)
    decay = _zip(lambda g, x: jnp.exp(_colsum_as_rows(g, x.shape[1])), glog, v)
    kv = _zip(_bmm_tn, k_dec, v)
    S = list(S)
    o, starts = [], []
    for i0 in range(0, len(q), nh):
        idx = range(i0, i0 + nh)
        starts.append(list(S))
        o += [o_in[i] + m for i, m in zip(idx, [_bmm(q_dec[i], S[h]) for h, i in enumerate(idx)])]
        S = [S[h] * decay[i] + kv[i] for h, i in enumerate(idx)]
    return o, S, starts


class Side:
    def __init__(self, ins, out_shapes, nsem, events):
        self.ins, self.out_shapes, self.nsem, self.events = list(ins), list(out_shapes), nsem, events


def _side_parts(side):
    if side is None:
        return [], [], [], []
    anyspec = pl.BlockSpec(memory_space=pl.ANY)
    return (side.ins, [anyspec] * len(side.ins), side.out_shapes,
            [pltpu.SemaphoreType.DMA((side.nsem,)), pltpu.SemaphoreType.DMA((side.nsem,))])


def _side_run(side, n_steps, in_refs, out_refs, sems, step=None):
    if side is None:
        return
    step = pl.program_id(0) if step is None else step
    for at, fn in side.events(n_steps, in_refs, out_refs, *sems):
        pl.when(step == at)(fn)


def chunk_fwd(name, chunk_fn, ins, dv, side=None, keep=()):
    M = ins[0][0].shape[0]
    NC = M // CH
    N = NC // CPS
    ni, nk = len(ins), len(keep)
    ws = [w for (_, w, _) in ins]
    s_ins, s_specs, s_shapes, s_sems = _side_parts(side)
    ns, nso = len(s_ins), len(s_shapes)

    def body(*refs):
        o0 = ni + ns
        o_ref, sall_ref = refs[o0:o0 + 2]
        k_refs = refs[o0 + 2:o0 + 2 + nk]
        s_ref = refs[o0 + 2 + nk + nso]
        _side_run(side, N, refs[ni:o0], refs[o0 + 2 + nk:o0 + 2 + nk + nso], refs[o0 + 3 + nk + nso:])

        @pl.when(pl.program_id(0) == 0)
        def _():
            s_ref[...] = jnp.zeros_like(s_ref)

        problems = [(cc, h) for cc in range(CPS) for h in range(4)]
        lists = [[r[cc * CH:(cc + 1) * CH, h * w:(h + 1) * w] for cc, h in problems] for r, w in zip(refs[:ni], ws)]
        o, s_new, starts, *kept = chunk_fn(*lists, [s_ref[h] for h in range(4)])
        for b, (cc, h) in enumerate(problems):
            o_ref[cc * CH:(cc + 1) * CH, h * dv:(h + 1) * dv] = o[b]
            sall_ref[h, cc] = starts[cc][h]
            for k_ref, vals in zip(k_refs, kept):
                k_ref[h, cc] = vals[b]
        for h in range(4):
            s_ref[h] = s_new[h]

    per_chunk = lambda r, c: pl.BlockSpec((4, CPS, r, c), lambda n: (0, n, 0, 0))
    specs = [pl.BlockSpec((CPS * CH, 4 * w), functools.partial(lambda n, cb: (n, cb), cb=cb // 4)) for (_, w, cb) in ins]
    res = pl.pallas_call(
        body, name=name, grid=(N,),
        in_specs=specs + s_specs,
        out_specs=[pl.BlockSpec((CPS * CH, 4 * dv), lambda n: (n, 0)), per_chunk(128, dv)] + [per_chunk(r, c) for r, c in keep]
        + [pl.BlockSpec(memory_space=pl.ANY)] * nso,
        out_shape=[jax.ShapeDtypeStruct((M, 4 * dv), F32), jax.ShapeDtypeStruct((4, NC, 128, dv), F32)]
        + [jax.ShapeDtypeStruct((4, NC, r, c), F32) for r, c in keep] + s_shapes,
        scratch_shapes=[pltpu.VMEM((4, 128, dv), F32)] + s_sems,
        compiler_params=_cparams(("arbitrary",)),
    )(*[a for (a, _, _) in ins], *s_ins)
    return res[0], res[1], res[2:2 + nk], res[2 + nk:]


def chunk_bwd(name, chunk_fn, ins, dv, s_all, do, side=None, kept=()):
    M = ins[0][0].shape[0]
    N = M // CH // CPS
    ni, nk = len(ins), len(kept)
    ws = [w for (_, w, _) in ins]
    s_ins, s_specs, s_shapes, s_sems = _side_parts(side)
    ns, nso = len(s_ins), len(s_shapes)

    def body(*refs):
        sall_ref, do_ref = refs[ni:ni + 2]
        k_refs = refs[ni + 2:ni + 2 + nk]
        o0 = ni + 2 + nk + ns
        d_refs = refs[o0:o0 + ni]
        ds_ref = refs[o0 + ni + nso]
        _side_run(side, N, refs[ni + 2 + nk:o0], refs[o0 + ni:o0 + ni + nso], refs[o0 + ni + nso + 1:])

        @pl.when(pl.program_id(0) == 0)
        def _():
            ds_ref[...] = jnp.zeros_like(ds_ref)

        problems = [(cc, h) for cc in range(CPS) for h in range(4)]
        lists = [[r[cc * CH:(cc + 1) * CH, h * w:(h + 1) * w] for cc, h in problems] for r, w in zip(refs[:ni], ws)]
        kept_lists = [[k_ref[h, cc] for cc, h in problems] for k_ref in k_refs]
        _, vjp = jax.vjp(lambda *a: tuple(chunk_fn(*a)[:2]), *lists, [sall_ref[h, 0] for h in range(4)], *kept_lists)
        grads = vjp(([do_ref[cc * CH:(cc + 1) * CH, h * dv:(h + 1) * dv] for cc, h in problems],
                     [ds_ref[h] for h in range(4)]))
        for d_ref, w, g in zip(d_refs, ws, grads[:ni]):
            for b, (cc, h) in enumerate(problems):
                d_ref[cc * CH:(cc + 1) * CH, h * w:(h + 1) * w] = g[b]
        for h in range(4):
            ds_ref[h] = grads[ni][h]

    rev = lambda n: N - 1 - n
    per_chunk = lambda r, c: pl.BlockSpec((4, CPS, r, c), lambda n: (0, rev(n), 0, 0))
    specs = [pl.BlockSpec((CPS * CH, 4 * w), functools.partial(lambda n, cb: (rev(n), cb), cb=cb // 4)) for (_, w, cb) in ins]
    res = pl.pallas_call(
        body, name=name, grid=(N,),
        in_specs=specs + [per_chunk(128, dv), pl.BlockSpec((CPS * CH, 4 * dv), lambda n: (rev(n), 0))]
        + [per_chunk(*a.shape[2:]) for a in kept] + s_specs,
        out_specs=[pl.BlockSpec((CPS * CH, 4 * w), lambda n: (rev(n), 0)) for w in ws] + [pl.BlockSpec(memory_space=pl.ANY)] * nso,
        out_shape=[jax.ShapeDtypeStruct((M, 4 * w), F32) for w in ws] + s_shapes,
        scratch_shapes=[pltpu.VMEM((4, 128, dv), F32)] + s_sems,
        compiler_params=_cparams(("arbitrary",)),
    )(*[a for (a, _, _) in ins], s_all, do, *kept, *s_ins)
    return res[:ni], res[ni:]


def _fn_gate_out(hd, row0, o, z, w):
    outs = []
    for h in range(4):
        outs.append(_rms(o[:, h * hd:(h + 1) * hd], w) * _silu(z[:, h * hd:(h + 1) * hd]))
    return (jnp.concatenate(outs, axis=1),)


def _fn_gla_prep(row0, gk, wgu, bg):
    x = _mm(gk, wgu) + bg
    ls = jnp.minimum(x, 0.0) - jnp.log(1.0 + jnp.exp(-jnp.abs(x)))
    return (jnp.where(_row_ids(row0, gk.shape[0]) >= ZROWS, ls / 16.0, 0.0),)


def loss_call(y, tgt):
    M = y.shape[0]
    tm = _pick(M, 512)

    def body(y_ref, t_ref, l_ref, dy_ref):
        i = pl.program_id(0)
        e = y_ref[...] - t_ref[...]
        dy_ref[...] = e * (1.0 / D)
        part = 0.5 * jnp.sum(jnp.sum(e * e, axis=1, keepdims=True) * (1.0 / D), axis=0, keepdims=True)
        part = jnp.broadcast_to(part, (8, LANE))

        @pl.when(i == 0)
        def _():
            l_ref[...] = part

        @pl.when(i > 0)
        def _():
            l_ref[...] += part

    return pl.pallas_call(
        body, name="loss", grid=(M // tm,),
        in_specs=[pl.BlockSpec((tm, D), lambda i: (i, 0))] * 2,
        out_specs=[pl.BlockSpec((8, LANE), lambda i: (0, 0)), pl.BlockSpec((tm, D), lambda i: (i, 0))],
        out_shape=[jax.ShapeDtypeStruct((8, LANE), F32), jax.ShapeDtypeStruct((M, D), F32)],
        compiler_params=_cparams(("arbitrary",)),
    )(y, tgt)


def _bf(x):
    return x.astype(BF16)


def core_step(x, tgt, W, comm=None):
    S = x.shape[0]
    M = S + PADR
    ids = jnp.asarray(_swa_bucket_ids())
    h0 = jnp.concatenate([jnp.zeros((ZROWS, D), F32), W["meta"], x], axis=0)
    nw = W["norm"]
    nrow = lambda l, k: nw[l, k][None, :]
    ffw = W["ffn"]
    projs = {}

    def projw(l):
        if l not in projs:
            projs[l] = W["proj"](l)
        return projs[l]

    site = (lambda name: comm.side(name)) if comm else (lambda name: None)
    landed = (lambda name, outs: comm.done(name, outs)) if comm else (lambda name, outs: None)
    sinks = jnp.pad(W["sinks"], ((0, 0), (0, LANE - 8)))
    dnp = jnp.pad(jnp.concatenate([W["a_log"], W["dt_bias"]], axis=1), ((0, 0), (0, LANE - 8)))
    wgu = jnp.pad(W["gate_up"], ((0, LANE - 16), (0, 0)))
    bg = W["b_gate"]
    full = lambda a: (a, a.shape[1], 0)

    saved = []
    h = h0
    (hn,) = rowwise_fwd("prenorm_0", _fn_prenorm, [full(h)], [nrow(0, 0)], [(D, BF16)])
    bias = swa_bias_fwd(W["rel"], ids)
    for l in range(2):
        st = {"h_a": h, "hn_a": hn}
        (f1, a1, b1), got = ffn_fwd(f"ffn_fwd_{l}0", hn, *ffw(l, 0), side=site(f"ffn_fwd_{l}0"))
        landed(f"ffn_fwd_{l}0", got)
        h, hn = rowwise_fwd(f"resnorm_{l}1", functools.partial(_fn_resnorm, 0.5), [full(h), full(f1)],
                            [nrow(l, 1), nrow(l, 2)], [(D, F32), (D, BF16)])
        st.update(f1=f1, a1=a1, b1=b1, h_b=h, hn_b=hn)
        if l == 0:
            proj = mm_nt(hn, projw(0)["w_in"], "e_proj")
            o_a, got = swa_fwd(proj, bias, sinks, side=site("swa_fwd"))
            landed("swa_fwd", got)
            y = conv_fwd(proj, W["conv"])
            qn, kn, gb, bb = rowwise_fwd(
                "dn_prep", _fn_dn_prep, [(y, 512, 0), (y, 512, 1), (proj, LANE, E_BA // LANE)], [dnp], [(512, F32)] * 4)
            ins = [(qn, 128, 0), (kn, 128, 0), (y, 128, 8), (gb, 128, 0), (bb, 128, 0)]
            o_dn, s_all, (t_inv,), got = chunk_fwd("dn_fwd", _dn_block, ins, 128, keep=[(CH, CH)],
                                                   side=site("dn_fwd"))
            landed("dn_fwd", got)
            (o_b,) = rowwise_fwd("dn_out", functools.partial(_fn_gate_out, 128),
                                 [full(o_dn), (proj, 512, E_ZB // 512)], [W["dn_norm"]], [(512, BF16)])
            omix = jnp.concatenate([o_a, o_b], axis=1)
            mix = mm_nn(omix, projw(0)["w_out"], "e_mix")
            st.update(proj=proj, y=y, qn=qn, kn=kn, gb=gb, bb=bb, o_dn=o_dn, s_all=s_all, t_inv=t_inv, omix=omix)
        else:
            proj = mm_nt(hn, projw(1)["w_in"], "o_proj")
            (glog,) = rowwise_fwd("gla_prep", _fn_gla_prep, [(proj, LANE, O_GK // LANE)], [wgu, bg], [(512, F32)])
            ins = [(proj, 128, O_Q // 128), (proj, 128, O_K // 128), (proj, 256, O_V // 256), (glog, 128, 0)]
            o_g, s_all, _, _ = chunk_fwd("gla_fwd", _gla_block, ins, 256)
            (omix,) = rowwise_fwd("gla_out", functools.partial(_fn_gate_out, 256),
                                  [full(o_g), (proj, 1024, O_G // 1024)], [W["gla_norm"]], [(1024, BF16)])
            mix = mm_nn(omix, projw(1)["w_out"], "o_mix")
            st.update(proj=proj, glog=glog, o_g=o_g, s_all=s_all, omix=omix)
        h, hn = rowwise_fwd(f"resnorm_{l}3", functools.partial(_fn_resnorm, 1.0), [full(h), full(mix)],
                            [nrow(l, 3), nrow(l, 4)], [(D, F32), (D, BF16)])
        st.update(mix=mix, h_c=h, hn_c=hn)
        (f2, a2, b2), got = ffn_fwd(f"ffn_fwd_{l}1", hn, *ffw(l, 1), side=site(f"ffn_fwd_{l}1"))
        landed(f"ffn_fwd_{l}1", got)
        st.update(f2=f2, a2=a2, b2=b2)
        if l == 0:
            h, hn = rowwise_fwd("resnorm_05", functools.partial(_fn_resnorm, 0.5), [full(h), full(f2)],
                                [nrow(0, 5), nrow(1, 0)], [(D, F32), (D, BF16)])
        else:
            (h,) = rowwise_fwd("res_last", functools.partial(_fn_res_last, 0.5), [full(h), full(f2)],
                               [nrow(1, 5)], [(D, F32)])
        saved.append(st)

    loss_blk, dy = loss_call(h[PADR:], tgt)
    dh = jnp.concatenate([jnp.zeros((PADR, D), F32), dy], axis=0)

    G = {}
    dnorm = [[None] * 6 for _ in range(2)]
    dWg = [[None, None], [None, None]]
    dWu = [[None, None], [None, None]]
    dWd = [[None, None], [None, None]]
    dhn = None
    for l in (1, 0):
        st = saved[l]
        if l == 1:
            (dh_, df), (dw5,) = rowwise_bwd(
                "res_last_b", functools.partial(_fn_res_last, 0.5), [full(st["h_c"]), full(st["f2"])], [nrow(1, 5)],
                [full(dh)], [F32, BF16])
            dnorm[1][5] = dw5
        else:
            (dh_, df), (dw5, dw0n) = rowwise_bwd(
                "resnorm_05_b", functools.partial(_fn_resnorm, 0.5), [full(st["h_c"]), full(st["f2"])],
                [nrow(0, 5), nrow(1, 0)], [full(dh), full(dhn)], [F32, BF16])
            dnorm[0][5] = dw5
            dnorm[1][0] = dw0n
        dh = dh_
        if comm and l == 0:
            units = (0, 1, 2, U_IN, U_OUT)
            comm.grads(1, units, layer_grad_items(1, dWg, dWu, dWd, G["o_in"], G["o_out"], only=units))
        (dxn, da, db, hm), got = ffn_bwd_x(f"ffn_bx_{l}1", df, st["a2"], st["b2"], *ffw(l, 1), side=site(f"ffn_bx_{l}1"))
        landed(f"ffn_bx_{l}1", got)
        dWg[l][1], dWu[l][1], dWd[l][1] = ffn_bwd_w(f"ffn_bw_{l}1", st["hn_c"], df, da, db, hm)
        if comm:
            comm.grads(l, (3, 4, 5), layer_grad_items(l, dWg, dWu, dWd, None, None, only=(3, 4, 5)))
        (dh_, dmix), (dw3, dw4) = rowwise_bwd(
            f"resnorm_{l}3_b", functools.partial(_fn_resnorm, 1.0), [full(st["h_b"]), full(st["mix"])],
            [nrow(l, 3), nrow(l, 4)], [full(dh), full(dxn)], [F32, BF16])
        dnorm[l][3], dnorm[l][4] = dw3, dw4
        dh = dh_
        proj = st["proj"]
        if l == 1:
            G["o_out"] = mm_tn(st["omix"], dmix, "o_out_dw")
            domix = mm_nt(dmix, projw(1)["w_out"], "o_mix_dx")
            (do_g, dgate), (dgn,) = rowwise_bwd(
                "gla_out_b", functools.partial(_fn_gate_out, 256), [full(st["o_g"]), (proj, 1024, O_G // 1024)],
                [W["gla_norm"]], [full(domix)], [F32, F32])
            G["gla_norm"] = dgn
            ins = [(proj, 128, O_Q // 128), (proj, 128, O_K // 128), (proj, 256, O_V // 256), (st["glog"], 128, 0)]
            (dq, dk, dv, dglog), _ = chunk_bwd("gla_bwd", _gla_block, ins, 256, st["s_all"], do_g)
            (dgk,), (dwgu, dbg) = rowwise_bwd("gla_prep_b", _fn_gla_prep, [(proj, LANE, O_GK // LANE)], [wgu, bg],
                                              [full(dglog)], [F32])
            G["gate_up"] = dwgu[:16]
            G["b_gate"] = dbg
            dproj = _bf(jnp.concatenate([dq, dk, dv, dgate, dgk, jnp.zeros((M, O_END - O_GK - LANE), F32)], axis=1))
            G["o_in"] = mm_tn(dproj, st["hn_b"], "o_in_dw")
            dhn_b = mm_nn(dproj, projw(1)["w_in"], "o_proj_dx")
        else:
            G["e_out"] = mm_tn(st["omix"], dmix, "e_out_dw")
            domix = mm_nt(dmix, projw(0)["w_out"], "e_mix_dx")
            (do_dn, dz), (ddn,) = rowwise_bwd(
                "dn_out_b", functools.partial(_fn_gate_out, 128), [full(st["o_dn"]), (proj, 512, E_ZB // 512)],
                [W["dn_norm"]], [(domix, 512, 2)], [F32, F32])
            G["dn_norm"] = ddn
            ins = [(st["qn"], 128, 0), (st["kn"], 128, 0), (st["y"], 128, 8), (st["gb"], 128, 0), (st["bb"], 128, 0)]
            (dqn, dkn, dvv, dgb, dbb), got = chunk_bwd("dn_bwd", _dn_block, ins, 128, st["s_all"], do_dn, side=site("dn_bwd"),
                                                        kept=[st["t_inv"]])
            landed("dn_bwd", got)
            (dyq, dyk, dba), (ddnp,) = rowwise_bwd(
                "dn_prep_b", _fn_dn_prep, [(st["y"], 512, 0), (st["y"], 512, 1), (proj, LANE, E_BA // LANE)], [dnp],
                [full(dqn), full(dkn), full(dgb), full(dbb)], [F32, F32, F32])
            G["a_log"] = ddnp[:, 0:4]
            G["dt_bias"] = ddnp[:, 4:8]
            dyc = jnp.concatenate([dyq, dyk, dvv], axis=1)
            dxc, dconv = conv_bwd(proj, W["conv"], dyc)
            G["conv"] = dconv
            (dq_a, dk_a, dv_a, dbias, dsink), got = swa_bwd(proj, bias, sinks, domix, side=site("swa_bwd"))
            landed("swa_bwd", got)
            G["sinks"] = dsink[:, :8]
            G["rel"] = swa_bias_bwd(dbias, ids)[:, :8]
            dproj = _bf(jnp.concatenate([dq_a, dk_a, dv_a, dxc, dz, dba, jnp.zeros((M, E_END - E_BA - LANE), F32)], axis=1))
            G["e_in"] = mm_tn(dproj, st["hn_b"], "e_in_dw")
            dhn_b = mm_nn(dproj, projw(0)["w_in"], "e_proj_dx")
        (dh_, df), (dw1, dw2) = rowwise_bwd(
            f"resnorm_{l}1_b", functools.partial(_fn_resnorm, 0.5), [full(st["h_a"]), full(st["f1"])],
            [nrow(l, 1), nrow(l, 2)], [full(dh), full(dhn_b)], [F32, BF16])
        dnorm[l][1], dnorm[l][2] = dw1, dw2
        dh = dh_
        if comm and l == 0:
            units = (U_IN, U_OUT)
            comm.grads(0, units, layer_grad_items(0, dWg, dWu, dWd, G["e_in"], G["e_out"], only=units))
        (dxn, da, db, hm), got = ffn_bwd_x(f"ffn_bx_{l}0", df, st["a1"], st["b1"], *ffw(l, 0), side=site(f"ffn_bx_{l}0"))
        landed(f"ffn_bx_{l}0", got)
        dWg[l][0], dWu[l][0], dWd[l][0] = ffn_bwd_w(f"ffn_bw_{l}0", st["hn_a"], df, da, db, hm)
        dhn = dxn
    (dh0p,), (dw00,) = rowwise_bwd("prenorm_0_b", _fn_prenorm, [full(saved[0]["h_a"])], [nrow(0, 0)], [full(dhn)], [F32])
    dnorm[0][0] = dw00
    dh = dh + dh0p
    G["meta"] = dh[ZROWS:PADR]
    G["norm"] = jnp.stack([jnp.concatenate(r, axis=0) for r in dnorm], axis=0)
    G["items"] = [layer_grad_items(0, dWg, dWu, dWd, G["e_in"], G["e_out"], only=(0, 1, 2) if comm else range(NUNITS)),
                  None if comm else layer_grad_items(1, dWg, dWu, dWd, G["o_in"], G["o_out"])]
    return loss_blk, dh[PADR:], G


NAMES = [("meta", "meta_tokens"), ("norm", "norm_w"), ("ffn_g", "ffn_w_gate"), ("ffn_u", "ffn_w_up"),
         ("ffn_d", "ffn_w_down"), ("rel", "rel_bias_table"), ("e_in", "even_w_in"), ("conv", "even_conv_w"),
         ("sinks", "swa_sinks"), ("a_log", "dn_a_log"), ("dt_bias", "dn_dt_bias"), ("dn_norm", "dn_norm_w"),
         ("e_out", "even_w_out"), ("o_in", "odd_w_in"), ("gate_up", "gla_w_gate_up"), ("b_gate", "gla_b_gate"),
         ("gla_norm", "gla_norm_w"), ("o_out", "odd_w_out")]
BIG = ["ffn_g", "ffn_u", "ffn_d", "e_in", "e_out", "o_in", "o_out"]
IN_ROWS = 800
SMALL = [("meta", (16, 256)), ("norm", (2, 6, 256)), ("conv", (1, 4, 384)), ("gate_up", (1, 16, 128)),
         ("b_gate", (1, 128)), ("gla_norm", (1, 64))]
REPL = [("rel", (32, 8)), ("sinks", (1, 8)), ("a_log", (1, 4)), ("dt_bias", (1, 4)), ("dn_norm", (1, 128))]
SMALL_REP = 88 * LANE
SMALL_ROWS = 96


def pack_small(t):
    a = jnp.concatenate([t[n].reshape(-1) for n, _ in SMALL])
    b = jnp.concatenate([t[n].reshape(-1) for n, _ in REPL])
    flat = jnp.concatenate([a, jnp.zeros((SMALL_REP - a.shape[0],), F32), b,
                            jnp.zeros((SMALL_ROWS * LANE - SMALL_REP - b.shape[0],), F32)])
    return flat.reshape(SMALL_ROWS, LANE)


def unpack_small(p):
    flat = p.reshape(-1)
    out, r = {}, 0
    for n, shp in SMALL:
        k = int(np.prod(shp))
        out[n] = flat[r:r + k].reshape(shp)
        r += k
    r = SMALL_REP
    for n, shp in REPL:
        k = int(np.prod(shp))
        out[n] = flat[r:r + k].reshape(shp)
        r += k
    return out


IN_SRC = (706, 772)


NUNITS = 8
U_IN, U_OUT = 6, 7


def _halves(a):
    return a.reshape(2, a.shape[0] // 2, D)


def weight_pieces(wt, l):
    inn = wt["e_in" if l == 0 else "o_in"]
    inn = jnp.pad(inn, ((0, IN_ROWS - inn.shape[0]), (0, 0)))
    ffn = [_halves(wt[n][l][j]) for j in range(2) for n in ("ffn_g", "ffn_u", "ffn_d")]
    return ffn + [_halves(inn), _halves(wt["e_out" if l == 0 else "o_out"])]


def ffn_weights(q, j):
    return tuple(q[3 * j + k].reshape(NSH, FSH, D) for k in range(3))


def proj_weights(l, q_in, q_out):
    m = _even_in_map() if l == 0 else _odd_in_map()
    src = np.where(m >= 0, (m // IN_SRC[l]) * IN_ROWS + m % IN_SRC[l], -1)
    w_out = q_out.reshape(NSH * 256, D)
    return {"w_in": _take_pad(q_in.reshape(NSH * IN_ROWS, D), src, 0),
            "w_out": _take_pad(w_out, _even_out_map(), 0) if l == 0 else w_out}


def layer_grad_items(l, dwg, dwu, dwd, g_in, g_out, only=range(NUNITS)):
    units = [[None] * NUNITS for _ in range(2)]

    def put(i, a):
        a = a.reshape(NSH, 2, a.shape[1] // 2, D)
        units[0][i], units[1][i] = a[:, 0], a[:, 1]

    for j in range(2):
        for k, t in enumerate((dwg, dwu, dwd)):
            if 3 * j + k in only:
                put(3 * j + k, t[l][j])
    if U_IN in only:
        m = _even_in_map() if l == 0 else _odd_in_map()
        gi = jnp.take(g_in, jnp.asarray(_inverse(m, NSH * IN_SRC[l])), axis=0).reshape(NSH, IN_SRC[l], D)
        put(U_IN, _bf(jnp.pad(gi, ((0, 0), (0, IN_ROWS - IN_SRC[l]), (0, 0)))))
    if U_OUT in only:
        if l == 0:
            g_out = jnp.take(g_out, jnp.asarray(_inverse(_even_out_map(), 1024)), axis=0)
        put(U_OUT, _bf(g_out).reshape(NSH, 256, D))
    return units


def assemble_layer(l, r0, r1):
    whole = lambda i: jnp.concatenate([r0[i], r1[i]])
    return {"ffn_g": jnp.stack([whole(0), whole(3)]), "ffn_u": jnp.stack([whole(1), whole(4)]),
            "ffn_d": jnp.stack([whole(2), whole(5)]), "in": whole(U_IN)[:IN_SRC[l]], "out": whole(U_OUT)}


def big_grads(l0, l1):
    st = lambda n: jnp.stack([l0[n], l1[n]])
    return {"ffn_g": st("ffn_g"), "ffn_u": st("ffn_u"), "ffn_d": st("ffn_d"), "e_in": l0["in"], "e_out": l0["out"],
            "o_in": l1["in"], "o_out": l1["out"]}


def small_from_gathered(gs):
    sm = [unpack_small(gs[s]) for s in range(NSH)]
    full = {}
    full["meta"] = jnp.concatenate([sm[s]["meta"] for s in range(NSH)], axis=1)
    full["norm"] = jnp.concatenate([sm[s]["norm"] for s in range(NSH)], axis=2)
    full["conv"] = jnp.concatenate([sm[s]["conv"][0] for s in range(NSH)], axis=1)
    full["gate_up"] = jnp.concatenate([sm[s]["gate_up"][0] for s in range(NSH)], axis=1)
    full["b_gate"] = jnp.concatenate([sm[s]["b_gate"] for s in range(NSH)], axis=1)
    full["gla_norm"] = jnp.concatenate([sm[s]["gla_norm"] for s in range(NSH)], axis=1)
    return full


def _col_sh(w):
    return jnp.moveaxis(w.reshape(w.shape[0], NSH, w.shape[1] // NSH), 1, 0)


def _to_t(n, a):
    if n in ("ffn_g", "ffn_u"):
        return jnp.swapaxes(a, 2, 3)
    if n in ("e_in", "o_in"):
        return jnp.swapaxes(a[0], 0, 1)
    return a if n == "ffn_d" else a[0]


def _from_t(n, a):
    if n in ("ffn_g", "ffn_u"):
        return jnp.swapaxes(a, 2, 3)
    if n in ("e_in", "o_in"):
        return jnp.swapaxes(a, 0, 1)[None]
    return a if n == "ffn_d" else a[None]


def pack_small_grads(G):
    col_sh = _col_sh
    norm_sh = jnp.moveaxis(G["norm"].reshape(2, 6, NSH, 256), 2, 0)
    a = jnp.concatenate([col_sh(G["meta"]).reshape(NSH, -1), norm_sh.reshape(NSH, -1), col_sh(G["conv"]).reshape(NSH, -1),
                         col_sh(G["gate_up"]).reshape(NSH, -1), G["b_gate"].reshape(NSH, -1),
                         G["gla_norm"].reshape(NSH, -1)], axis=1)
    b = jnp.concatenate([G[n].reshape(-1) for n, _ in REPL])
    b = jnp.broadcast_to(b[None], (NSH, b.shape[0]))
    small = jnp.concatenate([a, jnp.zeros((NSH, SMALL_REP - a.shape[1]), F32), b,
                             jnp.zeros((NSH, SMALL_ROWS * LANE - SMALL_REP - b.shape[1]), F32)], axis=1)
    return small.reshape(NSH, SMALL_ROWS, LANE)


MESH = pl.DeviceIdType.MESH
ANY = pl.BlockSpec(memory_space=pl.ANY)
VMEM = pl.BlockSpec(memory_space=pltpu.VMEM)


def _place():
    return lax.axis_index("x"), lax.axis_index("y"), lax.axis_index("c")


def _other_chips(x, y):
    return [(1 - x, y), (x, 1 - y), (1 - x, 1 - y)]


def _rcopy(send_sems, recv_sems, k, src, dst, to):
    return pltpu.make_async_remote_copy(src_ref=src, dst_ref=dst, send_sem=send_sems.at[k], recv_sem=recv_sems.at[k],
                                        device_id=to, device_id_type=MESH)


def _gather_steps(in_refs, out_refs, send_sems, recv_sems):
    n = len(in_refs)
    x, y, c = _place()
    s = 2 * x + y
    chips = _other_chips(x, y)
    copy = functools.partial(_rcopy, send_sems, recv_sems)
    pairs = [(i, j, cx, cy) for i in range(n) for j, (cx, cy) in enumerate(chips)]
    pushes = lambda: [copy(i * 3 + j, in_refs[i].at[c], out_refs[i].at[s, c], (cx, cy, c)) for i, j, cx, cy in pairs]
    landed = lambda i, cx, cy, half: out_refs[i].at[2 * cx + cy, half]
    relays = lambda: [copy(3 * n + i * 3 + j, landed(i, cx, cy, c), landed(i, cx, cy, c), (x, y, 1 - c)) for i, j, cx, cy in pairs]

    def start():
        for cp in pushes():
            cp.start()

    def relay():
        for i, j, cx, cy in pairs:
            copy(i * 3 + j, landed(i, cx, cy, c), landed(i, cx, cy, c), (x, y, c)).wait_recv()
        for cp in relays():
            cp.start()

    def finish():
        for i, j, cx, cy in pairs:
            copy(3 * n + i * 3 + j, landed(i, cx, cy, 1 - c), landed(i, cx, cy, 1 - c), (x, y, c)).wait_recv()
        for cp in pushes() + relays():
            cp.wait_send()

    return start, relay, finish


def _gather_shapes(pieces):
    return [jax.ShapeDtypeStruct((NSH,) + a.shape, a.dtype) for a in pieces]


def ag_layer(name, pieces):
    n = len(pieces)

    def body(*refs):
        for fn in _gather_steps(refs[:n], refs[n:2 * n], *refs[2 * n:]):
            fn()

    return pl.pallas_call(
        body, name=name, in_specs=[ANY] * n, out_specs=[ANY] * n, out_shape=_gather_shapes(pieces),
        scratch_shapes=[pltpu.SemaphoreType.DMA((6 * n,)), pltpu.SemaphoreType.DMA((6 * n,))],
    )(*pieces)


def gather_side(pieces):
    def events(n_steps, in_refs, out_refs, send_sems, recv_sems):
        start, relay, finish = _gather_steps(in_refs, out_refs, send_sems, recv_sems)
        return [(0, start), (max(3 * n_steps // 4, 1), relay), (n_steps - 1, finish)]

    return Side(pieces, _gather_shapes(pieces), 6 * len(pieces), events)


def ag_small(pack):
    def body(x_ref, out_ref, send_sems, recv_sems):
        x, y, c = _place()
        s = 2 * x + y
        chips = _other_chips(x, y)

        def copy(k, src, dst, to):
            return pltpu.make_async_remote_copy(src_ref=src, dst_ref=dst, send_sem=send_sems.at[k], recv_sem=recv_sems.at[k],
                                                device_id=to, device_id_type=MESH)

        out_ref[s] = x_ref[...]
        sends = [copy(j, x_ref, out_ref.at[s], (cx, cy, c)) for j, (cx, cy) in enumerate(chips)]
        for cp in sends:
            cp.start()
        for j, (cx, cy) in enumerate(chips):
            blk = out_ref.at[2 * cx + cy]
            copy(j, blk, blk, (x, y, c)).wait_recv()
        for cp in sends:
            cp.wait_send()

    return pl.pallas_call(
        body, name="ag_small", in_specs=[VMEM], out_specs=VMEM,
        out_shape=jax.ShapeDtypeStruct((NSH,) + pack.shape, pack.dtype),
        scratch_shapes=[pltpu.SemaphoreType.DMA((3,)), pltpu.SemaphoreType.DMA((3,))],
    )(pack)


def rs_pair(name, items):
    ni = len(items[0])

    def body(*refs):
        in_refs = [refs[:ni], refs[ni:2 * ni]]
        recv_refs = refs[2 * ni:3 * ni]
        send_sems, recv_sems = refs[3 * ni:]
        x, y, c = _place()
        copy = functools.partial(_rcopy, send_sems, recv_sems)
        for cc in range(2):
            @pl.when(c == cc)
            def _():
                cps = [copy(i * NSH + s, in_refs[1 - cc][i].at[s], recv_refs[i].at[s], (x, y, 1 - c))
                       for i in range(ni) for s in range(NSH)]
                for cp in cps:
                    cp.start()
                for cp in cps:
                    cp.wait()

    return pl.pallas_call(
        body, name=name, in_specs=[ANY] * (2 * ni), out_specs=[ANY] * ni,
        out_shape=[jax.ShapeDtypeStruct(a.shape, a.dtype) for a in items[0]],
        scratch_shapes=[pltpu.SemaphoreType.DMA((ni * NSH,)), pltpu.SemaphoreType.DMA((ni * NSH,))],
    )(*items[0], *items[1])


def _scatter_steps(a_refs, out_refs, send_sems, recv_sems):
    n = len(a_refs)
    x, y, c = _place()
    s = 2 * x + y
    chips = _other_chips(x, y)
    copy = functools.partial(_rcopy, send_sems, recv_sems)
    pairs = [(i, j, cx, cy) for i in range(n) for j, (cx, cy) in enumerate(chips)]
    sends = lambda: [copy(i * 3 + j, a_refs[i].at[2 * cx + cy], out_refs[i].at[s], (cx, cy, c)) for i, j, cx, cy in pairs]

    def start():
        for cp in sends():
            cp.start()

    def finish():
        for i, j, cx, cy in pairs:
            blk = out_refs[i].at[2 * cx + cy]
            copy(i * 3 + j, blk, blk, (x, y, c)).wait_recv()
        for cp in sends():
            cp.wait_send()

    return start, finish


def rs_chips(name, arrs):
    n = len(arrs)

    def body(*refs):
        for fn in _scatter_steps(refs[:n], refs[n:2 * n], *refs[2 * n:]):
            fn()

    return pl.pallas_call(
        body, name=name, in_specs=[ANY] * n, out_specs=[ANY] * n,
        out_shape=[jax.ShapeDtypeStruct(a.shape, a.dtype) for a in arrs],
        scratch_shapes=[pltpu.SemaphoreType.DMA((3 * n,)), pltpu.SemaphoreType.DMA((3 * n,))],
    )(*arrs)


def scatter_side(arrs):
    def events(n_steps, in_refs, out_refs, send_sems, recv_sems):
        start, finish = _scatter_steps(in_refs, out_refs, send_sems, recv_sems)
        return [(0, start), (n_steps - 1, finish)]

    return Side(arrs, [jax.ShapeDtypeStruct(a.shape, a.dtype) for a in arrs], 3 * len(arrs), events)


def _pair_chunks(rows):
    return 4 if rows % 32 == 0 else (2 if rows % 16 == 0 else 1)


def ag_pair(name, arrs):
    n = len(arrs)
    chunks = [(i, k * (a.shape[0] // _pair_chunks(a.shape[0])), a.shape[0] // _pair_chunks(a.shape[0]))
              for i, a in enumerate(arrs) for k in range(_pair_chunks(a.shape[0]))]

    def body(*refs):
        g_refs, out_refs = refs[:n], refs[n:2 * n]
        send_sems, recv_sems = refs[2 * n:]
        x, y, c = _place()
        give = [_rcopy(send_sems, recv_sems, q, g_refs[i].at[pl.ds(r0, rc)], out_refs[i].at[pl.ds(r0, rc)], (x, y, 1 - c))
                for q, (i, r0, rc) in enumerate(chunks)]
        for cp in give:
            cp.start()
        for cp in give:
            cp.wait()

    return pl.pallas_call(
        body, name=name, in_specs=[ANY] * n, out_specs=[ANY] * n,
        out_shape=[jax.ShapeDtypeStruct(a.shape, a.dtype) for a in arrs],
        scratch_shapes=[pltpu.SemaphoreType.DMA((len(chunks),)), pltpu.SemaphoreType.DMA((len(chunks),))],
    )(*arrs)


def small_allreduce(p):
    def body(p_ref, out_ref, rbuf, send_sems, recv_sems):
        x, y, c = _place()
        me = 4 * x + 2 * y + c
        rbuf[me] = p_ref[2 * x + y]
        flip = lambda v, f: (1 - v) if f else v
        peers = [(flip(x, k >> 2 & 1), flip(y, k >> 1 & 1), flip(c, k & 1)) for k in range(1, 8)]

        def copy(k, src, dst, to):
            return pltpu.make_async_remote_copy(src_ref=src, dst_ref=dst, send_sem=send_sems.at[k], recv_sem=recv_sems.at[k],
                                                device_id=to, device_id_type=MESH)

        sends = [copy(k, p_ref.at[2 * px + py], rbuf.at[me], (px, py, pc)) for k, (px, py, pc) in enumerate(peers)]
        for cp in sends:
            cp.start()
        for k, (px, py, pc) in enumerate(peers):
            blk = rbuf.at[4 * px + 2 * py + pc]
            copy(k, blk, blk, (x, y, c)).wait_recv()
        for cp in sends:
            cp.wait_send()
        acc = rbuf[0]
        for d in range(1, 8):
            acc = acc + rbuf[d]
        out_ref[...] = acc

    return pl.pallas_call(
        body, name="small_allreduce", in_specs=[VMEM], out_specs=VMEM,
        out_shape=jax.ShapeDtypeStruct(p.shape[1:], F32),
        scratch_shapes=[pltpu.VMEM((8,) + p.shape[1:], F32), pltpu.SemaphoreType.DMA((7,)), pltpu.SemaphoreType.DMA((7,))],
    )(p)


def _rows_tile(rows, cap):
    return _pick(rows, cap) if rows % 128 == 0 else rows


def sum_pair(name, a0s, a1s, recvs, cflag):
    n = len(recvs)

    def body(c_ref, *refs):
        for i in range(n):
            a0_ref, a1_ref, b_ref, o_ref = refs[i], refs[n + i], refs[2 * n + i], refs[3 * n + i]
            own = jnp.where(c_ref[0] == 0, a0_ref[...].astype(F32), a1_ref[...].astype(F32))
            o_ref[...] = (own + b_ref[...].astype(F32)).astype(o_ref.dtype)

    specs = [pl.BlockSpec((None,) + a.shape[1:], lambda s: (s, 0, 0)) for a in recvs]
    return pl.pallas_call(
        body, name=name, grid=(NSH,), in_specs=[pl.BlockSpec(memory_space=pltpu.SMEM)] + specs * 3, out_specs=specs,
        out_shape=[jax.ShapeDtypeStruct(a.shape, BF16) for a in recvs], compiler_params=_cparams(("parallel",)),
    )(cflag, *a0s, *a1s, *recvs)


def sum_chips(name, parts, owns, sflag):
    n = len(parts)

    def body(s_ref, *refs):
        for i in range(n):
            p_ref, a_ref, o_ref = refs[i], refs[n + i], refs[2 * n + i]
            acc = None
            for t in range(NSH):
                term = jnp.where(s_ref[0] == t, a_ref[t].astype(F32), p_ref[t].astype(F32))
                acc = term if acc is None else acc + term
            o_ref[...] = acc

    specs = [pl.BlockSpec(a.shape, lambda i: (0, 0, 0)) for a in parts]
    return pl.pallas_call(
        body, name=name, grid=(1,), in_specs=[pl.BlockSpec(memory_space=pltpu.SMEM)] + specs * 2,
        out_specs=[pl.BlockSpec(a.shape[1:], lambda i: (0, 0)) for a in parts],
        out_shape=[jax.ShapeDtypeStruct(a.shape[1:], F32) for a in parts], compiler_params=_cparams(("arbitrary",)),
    )(sflag, *parts, *owns)


ADAM_LR, ADAM_B1, ADAM_B2, ADAM_EPS, ADAM_WD, ADAM_STEP = 0.001, 0.9, 0.999, 1e-08, 0.01, 10


def adamw_call(name, w, g, m, v):
    rows, cols = w.shape
    tr = _rows_tile(rows, 512)

    def body(w_ref, g_ref, m_ref, v_ref, d_ref, nm_ref, nv_ref):
        g_ = g_ref[...]
        m_ = ADAM_B1 * m_ref[...] + (1.0 - ADAM_B1) * g_
        v_ = ADAM_B2 * v_ref[...] + (1.0 - ADAM_B2) * (g_ * g_)
        m_hat = m_ / (1.0 - ADAM_B1 ** ADAM_STEP)
        v_hat = v_ / (1.0 - ADAM_B2 ** ADAM_STEP)
        d_ref[...] = -ADAM_LR * (m_hat / (jnp.sqrt(v_hat) + ADAM_EPS) + ADAM_WD * w_ref[...])
        nm_ref[...] = m_
        nv_ref[...] = v_

    spec = pl.BlockSpec((tr, cols), lambda i: (i, 0))
    sh = jax.ShapeDtypeStruct((rows, cols), F32)
    return pl.pallas_call(
        body, name=name, grid=(rows // tr,), in_specs=[spec] * 4, out_specs=[spec] * 3, out_shape=[sh] * 3,
        compiler_params=_cparams(("parallel",)),
    )(w, g, m, v)


def kernel(x, meta_tokens, norm_w, ffn_w_gate, ffn_w_up, ffn_w_down, rel_bias_table, even_w_in, even_conv_w, swa_sinks, dn_a_log, dn_dt_bias, dn_norm_w, even_w_out, odd_w_in, gla_w_gate_up, gla_b_gate, gla_norm_w, odd_w_out, loss_target, m_meta_tokens, m_norm_w, m_ffn_w_gate, m_ffn_w_up, m_ffn_w_down, m_rel_bias_table, m_even_w_in, m_even_conv_w, m_swa_sinks, m_dn_a_log, m_dn_dt_bias, m_dn_norm_w, m_even_w_out, m_odd_w_in, m_gla_w_gate_up, m_gla_b_gate, m_gla_norm_w, m_odd_w_out, v_meta_tokens, v_norm_w, v_ffn_w_gate, v_ffn_w_up, v_ffn_w_down, v_rel_bias_table, v_even_w_in, v_even_conv_w, v_swa_sinks, v_dn_a_log, v_dn_dt_bias, v_dn_norm_w, v_even_w_out, v_odd_w_in, v_gla_w_gate_up, v_gla_b_gate, v_gla_norm_w, v_odd_w_out):
    ws = [meta_tokens, norm_w, ffn_w_gate, ffn_w_up, ffn_w_down, rel_bias_table, even_w_in, even_conv_w, swa_sinks, dn_a_log,
          dn_dt_bias, dn_norm_w, even_w_out, odd_w_in, gla_w_gate_up, gla_b_gate, gla_norm_w, odd_w_out]
    ms = [m_meta_tokens, m_norm_w, m_ffn_w_gate, m_ffn_w_up, m_ffn_w_down, m_rel_bias_table, m_even_w_in, m_even_conv_w,
          m_swa_sinks, m_dn_a_log, m_dn_dt_bias, m_dn_norm_w, m_even_w_out, m_odd_w_in, m_gla_w_gate_up, m_gla_b_gate,
          m_gla_norm_w, m_odd_w_out]
    vs = [v_meta_tokens, v_norm_w, v_ffn_w_gate, v_ffn_w_up, v_ffn_w_down, v_rel_bias_table, v_even_w_in, v_even_conv_w,
          v_swa_sinks, v_dn_a_log, v_dn_dt_bias, v_dn_norm_w, v_even_w_out, v_odd_w_in, v_gla_w_gate_up, v_gla_b_gate,
          v_gla_norm_w, v_odd_w_out]
    short = [n for n, _ in NAMES]
    w = dict(zip(short, ws))
    m = dict(zip(short, ms))
    v = dict(zip(short, vs))

    wt = {n: _to_t(n, w[n]) for n in BIG}
    own = {n: wt[n].astype(BF16) for n in BIG}
    sflag = (2 * lax.axis_index("x") + lax.axis_index("y")).astype(jnp.int32).reshape(1)
    cflag = lax.axis_index("c").astype(jnp.int32).reshape(1)
    is0 = cflag[0] == 0
    fill = lambda got, pieces: [lax.dynamic_update_index_in_dim(g_, p_, sflag[0], 0) for g_, p_ in zip(got, pieces)]
    pieces = [weight_pieces(own, l) for l in range(2)]
    small = small_from_gathered(ag_small(pack_small(w)))
    def reduce_finish(l, mine, parts):
        red = [None] * NUNITS
        for grp in ((0, 1, 2), (3, 4, 5), (U_IN, U_OUT)):
            outs = sum_chips(f"sum_chips_{l}{grp[0]}", [parts[i] for i in grp], [mine[i] for i in grp], sflag)
            for i, o in zip(grp, outs):
                red[i] = o
        got = ag_pair(f"ag_pair_{l}", red)
        return [jnp.where(is0, r_, g_) for r_, g_ in zip(red, got)], [jnp.where(is0, g_, r_) for r_, g_ in zip(red, got)]

    class Exchanges:
        gathers = {"ffn_fwd_00": [(0, 3), (0, U_IN), (0, U_OUT)], "swa_fwd": [(0, 4)], "dn_fwd": [(0, 5), (1, 0), (1, 1)],
                   "ffn_fwd_01": [(1, 2), (1, U_IN), (1, U_OUT)], "ffn_fwd_10": [(1, 3), (1, 4), (1, 5)]}
        scatters = {"ffn_bx_10": [(1, 3), (1, 4), (1, 5)], "ffn_bx_01": [(1, 0), (1, 1), (1, 2)],
                    "dn_bwd": [(1, U_IN), (1, U_OUT), (0, 3)], "swa_bwd": [(0, 4), (0, 5)], "ffn_bx_00": [(0, U_IN), (0, U_OUT)]}

        def __init__(self):
            self.q = [[None] * NUNITS for _ in range(2)]
            self.mine = [[None] * NUNITS for _ in range(2)]
            self.parts = [[None] * NUNITS for _ in range(2)]

        def side(self, name):
            if name in self.gathers:
                return gather_side([pieces[l][i] for l, i in self.gathers[name]])
            if name in self.scatters:
                return scatter_side([self.mine[l][i] for l, i in self.scatters[name]])
            return None

        def done(self, name, outs):
            if name in self.gathers:
                units = self.gathers[name]
                for (l, i), a in zip(units, fill(outs, [pieces[l][i] for l, i in units])):
                    self.q[l][i] = a
            if name in self.scatters:
                for (l, i), a in zip(self.scatters[name], outs):
                    self.parts[l][i] = a

        def ffn(self, l, j):
            return ffn_weights(self.q[l], j)

        def proj(self, l):
            return proj_weights(l, self.q[l][U_IN], self.q[l][U_OUT])

        def grads(self, l, units, items):
            units = list(units)
            half0, half1 = [[items[c][i] for i in units] for c in range(2)]
            recv = rs_pair(f"rs_pair_{l}{units[0]}", [half0, half1])
            for i, a in zip(units, sum_pair(f"sum_pair_{l}{units[0]}", half0, half1, recv, cflag)):
                self.mine[l][i] = a

    ex = Exchanges()
    first = [0, 1, 2]
    for i, a in zip(first, fill(ag_layer("ag_layer_0", [pieces[0][i] for i in first]), [pieces[0][i] for i in first])):
        ex.q[0][i] = a
    W = {**small, **{n: w[n] for n, _ in REPL}, "ffn": ex.ffn, "proj": ex.proj}
    loss_blk, gx, G = core_step(x[0], loss_target[0], W, comm=ex)

    ex.grads(0, first, G["items"][0])
    for i, a in zip(first, rs_chips("rs_chips_0", [ex.mine[0][i] for i in first])):
        ex.parts[0][i] = a
    lay0 = assemble_layer(0, *reduce_finish(0, ex.mine[0], ex.parts[0]))
    lay1 = assemble_layer(1, *reduce_finish(1, ex.mine[1], ex.parts[1]))
    gt = big_grads(lay0, lay1)
    g_small_pack = small_allreduce(pack_small_grads(G))
    g = {**{n: _from_t(n, gt[n]) for n in BIG}, **unpack_small(g_small_pack)}

    delta, new_m, new_v = {}, {}, {}
    for n in BIG:
        shp = wt[n].shape
        two = lambda t: t.reshape(-1, D)
        d_, m_, v_ = adamw_call("adamw_" + n, two(wt[n]), two(gt[n]), two(_to_t(n, m[n])), two(_to_t(n, v[n])))
        delta[n], new_m[n], new_v[n] = (_from_t(n, t.reshape(shp)) for t in (d_, m_, v_))
    d_, m_, v_ = adamw_call("adamw_small", pack_small(w), g_small_pack, pack_small(m), pack_small(v))
    delta.update(unpack_small(d_))
    new_m.update(unpack_small(m_))
    new_v.update(unpack_small(v_))

    loss = lax.psum(loss_blk[0, 0], ("x", "y", "c"))
    return (loss, gx[None], *[g[n] for n in short], *[delta[n] for n in short], *[new_m[n] for n in short],
            *[new_v[n] for n in short])
```

```python
import functools
import math

import numpy as np
import jax
import jax.numpy as jnp
from jax import lax
from jax.experimental import pallas as pl
from jax.experimental.pallas import tpu as pltpu

F32 = jnp.float32
BF16 = jnp.bfloat16
HI = lax.Precision.HIGHEST

D = 1024
N_META = 16
PADR = 128
ZROWS = PADR - N_META
D_FF = 2816
NSH = 4
FSH = D_FF // NSH
EPS = 1e-6
NEG = -1e30
CH = 64
CPS = 2
BLK = 128
LANE = 128
VMEM_LIMIT = 56 * 1024 * 1024
FFN_SUB = 4

E_QA, E_KA, E_VA, E_QB, E_KB, E_VB, E_ZB, E_BA, E_END = 0, 1024, 1280, 1536, 2048, 2560, 3072, 3584, 4096


def _even_in_map():
    m = np.full((E_END,), -1, np.int64)
    for h in range(8):
        m[E_QA + h * 128:E_QA + h * 128 + 64] = np.arange(h * 64, (h + 1) * 64)
    for h in range(2):
        m[E_KA + h * 128:E_KA + h * 128 + 64] = 512 + np.arange(h * 64, (h + 1) * 64)
        m[E_VA + h * 128:E_VA + h * 128 + 64] = 640 + np.arange(h * 64, (h + 1) * 64)
    m[E_QB:E_QB + 2048] = 768 + np.arange(2048)
    m[E_BA:E_BA + 8] = 2816 + np.arange(8)
    return m


def _even_out_map():
    m = np.full((1536,), -1, np.int64)
    for h in range(8):
        m[h * 128:h * 128 + 64] = np.arange(h * 64, (h + 1) * 64)
    m[1024:1536] = 512 + np.arange(512)
    return m


O_Q, O_K, O_V, O_G, O_GK, O_END = 0, 512, 1024, 2048, 3072, 3584


def _odd_in_map():
    m = np.full((O_END,), -1, np.int64)
    m[:3072] = np.arange(3072)
    m[O_GK:O_GK + 16] = 3072 + np.arange(16)
    return m


def _inverse(m, n):
    inv = np.zeros((n,), np.int64)
    for p, o in enumerate(m):
        if o >= 0:
            inv[o] = p
    return inv


def _rows(w, m):
    parts, i, n = [], 0, len(m)
    while i < n:
        j = i + 1
        if m[i] < 0:
            while j < n and m[j] < 0:
                j += 1
            parts.append(jnp.zeros((j - i,) + w.shape[1:], w.dtype))
        else:
            while j < n and m[j] == m[j - 1] + 1:
                j += 1
            parts.append(lax.slice_in_dim(w, int(m[i]), int(m[i]) + j - i, axis=0))
        i = j
    return jnp.concatenate(parts, axis=0)


def _mm(a, b, prec=HI):
    return lax.dot_general(a, b, (((1,), (0,)), ((), ())), precision=prec, preferred_element_type=F32)


def _mm_nt(a, b, prec=HI):
    return lax.dot_general(a, b, (((1,), (1,)), ((), ())), precision=prec, preferred_element_type=F32)


def _mm_tn(a, b, prec=HI):
    return lax.dot_general(a, b, (((0,), (0,)), ((), ())), precision=prec, preferred_element_type=F32)


def _bdot(a, b, dims):
    return lax.dot_general(a.astype(BF16), b.astype(BF16), (dims, ((), ())), preferred_element_type=F32)


@jax.custom_vjp
def _bmm(a, b):
    return _bdot(a, b, ((1,), (0,)))


@jax.custom_vjp
def _bmm_nt(a, b):
    return _bdot(a, b, ((1,), (1,)))


@jax.custom_vjp
def _bmm_tn(a, b):
    return _bdot(a, b, ((0,), (0,)))


_bmm.defvjp(lambda a, b: (_bmm(a, b), (a, b)), lambda r, g: (_bmm_nt(g, r[1]), _bmm_tn(r[0], g)))
_bmm_nt.defvjp(lambda a, b: (_bmm_nt(a, b), (a, b)), lambda r, g: (_bmm(g, r[1]), _bmm_tn(g, r[0])))
_bmm_tn.defvjp(lambda a, b: (_bmm_tn(a, b), (a, b)), lambda r, g: (_bmm_nt(r[1], g), _bmm(r[0], g)))


def _hi_lo(x):
    h = x.astype(BF16)
    return h, (x - h.astype(F32)).astype(BF16)


def _xdot(a, b, dims):
    ah, al = _hi_lo(a)
    bh, bl = _hi_lo(b)
    d = lambda p, q: lax.dot_general(p, q, (dims, ((), ())), preferred_element_type=F32)
    return d(ah, bh) + (d(ah, bl) + d(al, bh))


@jax.custom_vjp
def _xmm(a, b):
    return _xdot(a, b, ((1,), (0,)))


@jax.custom_vjp
def _xmm_nt(a, b):
    return _xdot(a, b, ((1,), (1,)))


@jax.custom_vjp
def _xmm_tn(a, b):
    return _xdot(a, b, ((0,), (0,)))


_xmm.defvjp(lambda a, b: (_xmm(a, b), (a, b)), lambda r, g: (_xmm_nt(g, r[1]), _xmm_tn(r[0], g)))
_xmm_nt.defvjp(lambda a, b: (_xmm_nt(a, b), (a, b)), lambda r, g: (_xmm(g, r[1]), _xmm_tn(g, r[0])))
_xmm_tn.defvjp(lambda a, b: (_xmm_tn(a, b), (a, b)), lambda r, g: (_xmm_nt(r[1], g), _xmm(r[0], g)))


def _sum01(m01, x, dims):
    h, l = _hi_lo(x)
    l2 = (x - h.astype(F32) - l.astype(F32)).astype(BF16)
    m = m01.astype(BF16)
    d = lambda q: lax.dot_general(m, q, (dims, ((), ())), preferred_element_type=F32)
    return d(h) + (d(l) + d(l2))


@jax.custom_vjp
def _cumsum_rows(x):
    n = x.shape[0]
    tri = lax.broadcasted_iota(jnp.int32, (n, n), 0) >= lax.broadcasted_iota(jnp.int32, (n, n), 1)
    return _sum01(tri, x, ((1,), (0,)))


def _cumsum_rows_b(_, g):
    n = g.shape[0]
    tri = lax.broadcasted_iota(jnp.int32, (n, n), 0) >= lax.broadcasted_iota(jnp.int32, (n, n), 1)
    return (_sum01(tri, g, ((0,), (0,))),)


_cumsum_rows.defvjp(lambda x: (_cumsum_rows(x), None), _cumsum_rows_b)


@functools.partial(jax.custom_vjp, nondiff_argnums=(1,))
def _colsum_as_rows(x, width):
    return _colsum_impl(x, width)


def _colsum_impl(x, width):
    h, l = _hi_lo(x)
    l2 = (x - h.astype(F32) - l.astype(F32)).astype(BF16)
    ones = jnp.ones((x.shape[0], width), BF16)
    d = lambda q: lax.dot_general(q, ones, (((0,), (0,)), ((), ())), preferred_element_type=F32)
    return d(h) + (d(l) + d(l2))


def _colsum_as_rows_f(x, width):
    return _colsum_impl(x, width), x.shape[0]


def _colsum_as_rows_b(width, n, g):
    return (_sum01(jnp.ones((n, width), F32), g, ((1,), (1,))),)


_colsum_as_rows.defvjp(_colsum_as_rows_f, _colsum_as_rows_b)


def _rms(x, w):
    return x * lax.rsqrt(jnp.mean(x * x, axis=-1, keepdims=True) + EPS) * w


def _sigmoid(x):
    return 1.0 / (1.0 + jnp.exp(-x))


def _silu(x):
    return x * _sigmoid(x)


def _softplus(x):
    return jnp.maximum(x, 0.0) + jnp.log(1.0 + jnp.exp(-jnp.abs(x)))


def _lane_pick(row, idx):
    lane = lax.broadcasted_iota(jnp.int32, row.shape, row.ndim - 1)
    return jnp.sum(jnp.where(lane == idx, row, 0.0), axis=-1, keepdims=True)


def _row_ids(row0, n):
    return row0 + lax.broadcasted_iota(jnp.int32, (n, 1), 0)


def _pick(m, cap):
    best = 64
    for t in range(64, min(m, cap) + 1, 64):
        if m % t == 0:
            best = t
    return best


def _cparams(sem):
    return pltpu.CompilerParams(dimension_semantics=sem, vmem_limit_bytes=VMEM_LIMIT)


def mm_nn(a, b, name, out_dtype=F32):
    M, K = a.shape
    N = b.shape[1]
    tm = _pick(M, 1408 if K <= 2048 else 704)
    tn = _pick(N, 512)

    def body(a_ref, b_ref, o_ref):
        o_ref[...] = _mm(a_ref[...], b_ref[...], None).astype(o_ref.dtype)

    return pl.pallas_call(
        body, name=name, grid=(N // tn, M // tm),
        in_specs=[pl.BlockSpec((tm, K), lambda j, i: (i, 0)), pl.BlockSpec((K, tn), lambda j, i: (0, j))],
        out_specs=pl.BlockSpec((tm, tn), lambda j, i: (i, j)),
        out_shape=jax.ShapeDtypeStruct((M, N), out_dtype),
        compiler_params=_cparams(("parallel", "parallel")),
    )(a, b)


def mm_nt(a, b, name, out_dtype=F32):
    M, K = a.shape
    N = b.shape[0]
    tm = _pick(M, 1408)
    tn = _pick(N, 512 if K > 2048 else 1024)

    def body(a_ref, b_ref, o_ref):
        o_ref[...] = _mm_nt(a_ref[...], b_ref[...], None).astype(o_ref.dtype)

    return pl.pallas_call(
        body, name=name, grid=(N // tn, M // tm),
        in_specs=[pl.BlockSpec((tm, K), lambda j, i: (i, 0)), pl.BlockSpec((tn, K), lambda j, i: (j, 0))],
        out_specs=pl.BlockSpec((tm, tn), lambda j, i: (i, j)),
        out_shape=jax.ShapeDtypeStruct((M, N), out_dtype),
        compiler_params=_cparams(("parallel", "parallel")),
    )(a, b)


def mm_tn(a, b, name):
    M, K = a.shape
    N = b.shape[1]
    tk = _pick(K, 512)
    tn = _pick(N, 512)

    def body(a_ref, b_ref, o_ref):
        o_ref[...] = _mm_tn(a_ref[...], b_ref[...], None)

    return pl.pallas_call(
        body, name=name, grid=(K // tk, N // tn),
        in_specs=[pl.BlockSpec((M, tk), lambda i, j: (0, i)), pl.BlockSpec((M, tn), lambda i, j: (0, j))],
        out_specs=pl.BlockSpec((tk, tn), lambda i, j: (i, j)),
        out_shape=jax.ShapeDtypeStruct((K, N), F32),
        compiler_params=_cparams(("parallel", "parallel")),
    )(a, b)


def _row_specs(rows, tm):
    return [pl.BlockSpec((tm, w), functools.partial(lambda i, cb: (i, cb), cb=cb)) for (_, w, cb) in rows]


def _param_specs(params):
    return [pl.BlockSpec(p.shape, functools.partial(lambda i, nd: (0,) * nd, nd=p.ndim)) for p in params]


def rowwise_fwd(name, fn, rows, params, outs, tm=None):
    M = rows[0][0].shape[0]
    tm = tm or _pick(M, 704)
    nr, npar = len(rows), len(params)

    def body(*refs):
        row0 = pl.program_id(0) * tm
        vals = [r[...].astype(F32) for r in refs[:nr]] + [p[...] for p in refs[nr:nr + npar]]
        res = fn(row0, *vals)
        for o_ref, r in zip(refs[nr + npar:], res):
            o_ref[...] = r.astype(o_ref.dtype)

    return pl.pallas_call(
        body, name=name, grid=(M // tm,),
        in_specs=_row_specs(rows, tm) + _param_specs(params),
        out_specs=[pl.BlockSpec((tm, w), lambda i: (i, 0)) for (w, _) in outs],
        out_shape=[jax.ShapeDtypeStruct((M, w), dt) for (w, dt) in outs],
        compiler_params=_cparams(("parallel",)),
    )(*[r[0] for r in rows], *params)


def rowwise_bwd(name, fn, rows, params, douts, drow_dtypes, tm=None):
    M = rows[0][0].shape[0]
    tm = tm or _pick(M, 704)
    nr, npar, nd = len(rows), len(params), len(douts)
    want = [k for k, dt in enumerate(drow_dtypes) if dt is not None]

    def body(*refs):
        i = pl.program_id(0)
        row0 = i * tm
        vals = [r[...].astype(F32) for r in refs[:nr]] + [p[...] for p in refs[nr:nr + npar]]
        cots = tuple(d[...].astype(F32) for d in refs[nr + npar:nr + npar + nd])
        _, vjp = jax.vjp(functools.partial(fn, row0), *vals)
        grads = vjp(cots)
        o_refs = refs[nr + npar + nd:]
        for o_ref, k in zip(o_refs[:len(want)], want):
            o_ref[...] = grads[k].astype(o_ref.dtype)
        for o_ref, g in zip(o_refs[len(want):], grads[nr:]):
            @pl.when(i == 0)
            def _():
                o_ref[...] = g

            @pl.when(i > 0)
            def _():
                o_ref[...] += g

    res = pl.pallas_call(
        body, name=name, grid=(M // tm,),
        in_specs=_row_specs(rows, tm) + _param_specs(params) + _row_specs(douts, tm),
        out_specs=[pl.BlockSpec((tm, rows[k][1]), lambda i: (i, 0)) for k in want] + _param_specs(params),
        out_shape=[jax.ShapeDtypeStruct((M, rows[k][1]), drow_dtypes[k]) for k in want]
        + [jax.ShapeDtypeStruct(p.shape, F32) for p in params],
        compiler_params=_cparams(("arbitrary",)),
    )(*[r[0] for r in rows], *params, *[d[0] for d in douts])
    return res[:len(want)], res[len(want):]


def _fn_prenorm(row0, h, wpre):
    return (_rms(h, wpre),)


def _fn_resnorm(scale, row0, h, f, wpost, wpre):
    h2 = h + scale * _rms(f, wpost)
    return h2, _rms(h2, wpre)


def _fn_res_last(scale, row0, h, f, wpost):
    return (h + scale * _rms(f, wpost),)


def ffn_fwd(name, xn, wg, wu, wd, side=None):
    M = xn.shape[0]
    tm = _pick(M, 704)
    s_ins, s_specs, s_shapes, s_sems = _side_parts(side)
    ns, nso = len(s_ins), len(s_shapes)

    def body(x_ref, wg_ref, wu_ref, wd_ref, *rest):
        f_ref, a_ref, b_ref = rest[ns:ns + 3]
        s = pl.program_id(1)
        _side_run(side, (M // tm) * NSH, rest[:ns], rest[ns + 3:ns + 3 + nso], rest[ns + 3 + nso:],
                  step=pl.program_id(0) * NSH + s)
        x = x_ref[...]
        a = _mm_nt(x, wg_ref[...], None)
        b = _mm_nt(x, wu_ref[...], None)
        c = _mm((_silu(a) * b).astype(BF16), wd_ref[...], None)

        @pl.when(s == 0)
        def _():
            f_ref[...] = c

        @pl.when(s > 0)
        def _():
            f_ref[...] += c

        a_ref[...] = a.astype(BF16)
        b_ref[...] = b.astype(BF16)

    wspec = wdspec = pl.BlockSpec((None, FSH, D), lambda i, s: (s, 0, 0))
    abspec = pl.BlockSpec((None, tm, FSH), lambda i, s: (s, i, 0))
    res = pl.pallas_call(
        body, name=name, grid=(M // tm, NSH),
        in_specs=[pl.BlockSpec((tm, D), lambda i, s: (i, 0)), wspec, wspec, wdspec] + s_specs,
        out_specs=[pl.BlockSpec((tm, D), lambda i, s: (i, 0)), abspec, abspec] + [pl.BlockSpec(memory_space=pl.ANY)] * nso,
        out_shape=[jax.ShapeDtypeStruct((M, D), F32), jax.ShapeDtypeStruct((NSH, M, FSH), BF16),
                   jax.ShapeDtypeStruct((NSH, M, FSH), BF16)] + s_shapes,
        scratch_shapes=s_sems,
        compiler_params=_cparams(("arbitrary", "arbitrary")),
    )(xn, wg, wu, wd, *s_ins)
    return res[:3], res[3:]


def ffn_bwd_x(name, df, a, b, wg, wu, wd, side=None):
    M = df.shape[0]
    tm = _pick(M, 704)
    ts = tm // FFN_SUB
    s_ins, s_specs, s_shapes, s_sems = _side_parts(side)
    ns, nso = len(s_ins), len(s_shapes)

    def body(df_ref, a_ref, b_ref, wg_ref, wu_ref, wd_ref, *rest):
        dx_ref, da_ref, db_ref, hm_ref = rest[ns:ns + 4]
        _side_run(side, (M // tm) * NSH, rest[:ns], rest[ns + 4:ns + 4 + nso], rest[ns + 4 + nso:],
                  step=pl.program_id(0) * NSH + pl.program_id(1))

        @pl.when(pl.program_id(1) == 0)
        def _():
            dx_ref[...] = jnp.zeros_like(dx_ref)

        for r in range(FFN_SUB):
            rows = pl.ds(r * ts, ts)
            a_ = a_ref[rows, :].astype(F32)
            b_ = b_ref[rows, :].astype(F32)
            dh = _mm_nt(df_ref[rows, :], wd_ref[...], None)
            sig = _sigmoid(a_)
            sil = a_ * sig
            da = (dh * b_ * (sig * (1.0 + a_ * (1.0 - sig)))).astype(BF16)
            db = (dh * sil).astype(BF16)
            dx_ref[rows, :] += _mm(da, wg_ref[...], None) + _mm(db, wu_ref[...], None)
            da_ref[rows, :] = da
            db_ref[rows, :] = db
            hm_ref[rows, :] = (sil * b_).astype(BF16)

    wspec = wdspec = pl.BlockSpec((None, FSH, D), lambda i, s: (s, 0, 0))
    abspec = pl.BlockSpec((None, tm, FSH), lambda i, s: (s, i, 0))
    ab = jax.ShapeDtypeStruct((NSH, M, FSH), BF16)
    res = pl.pallas_call(
        body, name=name, grid=(M // tm, NSH),
        in_specs=[pl.BlockSpec((tm, D), lambda i, s: (i, 0)), abspec, abspec, wspec, wspec, wdspec] + s_specs,
        out_specs=[pl.BlockSpec((tm, D), lambda i, s: (i, 0)), abspec, abspec, abspec] + [pl.BlockSpec(memory_space=pl.ANY)] * nso,
        out_shape=[jax.ShapeDtypeStruct((M, D), F32), ab, ab, ab] + s_shapes,
        scratch_shapes=s_sems,
        compiler_params=_cparams(("arbitrary", "arbitrary")),
    )(df, a, b, wg, wu, wd, *s_ins)
    return res[:4], res[4:]


def ffn_bwd_w(name, xn, df, da, db, hm):
    M = xn.shape[0]
    tm = _pick(M, 704)
    nt = M // tm

    def body(x_ref, df_ref, da_ref, db_ref, hm_ref, dwg_ref, dwu_ref, dwd_ref, ag, au, ad):
        i = pl.program_id(1)
        x = x_ref[...]
        g = _mm_tn(da_ref[...], x, None)
        u = _mm_tn(db_ref[...], x, None)
        d = _mm_tn(hm_ref[...], df_ref[...], None)

        @pl.when(i == 0)
        def _():
            ag[...] = g
            au[...] = u
            ad[...] = d

        @pl.when(i > 0)
        def _():
            ag[...] += g
            au[...] += u
            ad[...] += d

        @pl.when(i == nt - 1)
        def _():
            dwg_ref[...] = ag[...].astype(BF16)
            dwu_ref[...] = au[...].astype(BF16)
            dwd_ref[...] = ad[...].astype(BF16)

    xspec = pl.BlockSpec((tm, D), lambda s, i: (i, 0))
    abspec = pl.BlockSpec((None, tm, FSH), lambda s, i: (s, i, 0))
    return pl.pallas_call(
        body, name=name, grid=(NSH, nt),
        in_specs=[xspec, xspec, abspec, abspec, abspec],
        out_specs=[pl.BlockSpec((None, FSH, D), lambda s, i: (s, 0, 0))] * 3,
        out_shape=[jax.ShapeDtypeStruct((NSH, FSH, D), BF16)] * 3,
        scratch_shapes=[pltpu.VMEM((FSH, D), F32)] * 3,
        compiler_params=_cparams(("parallel", "arbitrary")),
    )(xn, df, da, db, hm)


def _t5_bucket_np(rel):
    n = np.maximum(rel, 0)
    n_f = np.maximum(n, 1).astype(np.float32)
    large = 16 + (np.log(n_f / np.float32(16)) / np.float32(math.log(8.0)) * np.float32(16)).astype(np.int32)
    large = np.minimum(large, 31)
    return np.where(n < 16, n, large).astype(np.int32)


def _swa_bucket_ids():
    qi = np.arange(BLK)[:, None]
    kj = np.arange(BLK)[None, :]
    out = np.full((3, BLK, 3 * BLK), -1, np.int32)
    for v in range(3):
        pos_q = v * BLK + qi - ZROWS
        rel_m = pos_q - (kj - ZROWS)
        ok_m = (kj >= ZROWS) & (rel_m >= 0) & (pos_q >= 0)
        out[v, :, 0:BLK] = np.where(ok_m, _t5_bucket_np(rel_m), -1)
        pos_kp = (v - 1) * BLK + kj - ZROWS
        rel_p = BLK + qi - kj
        ok_p = (pos_kp >= N_META) & (rel_p >= 0) & (rel_p < BLK) & np.full_like(ok_m, v >= 1)
        out[v, :, BLK:2 * BLK] = np.where(ok_p, _t5_bucket_np(rel_p), -1)
        pos_kc = v * BLK + kj - ZROWS
        rel_c = qi - kj
        ok_c = (pos_kc >= N_META) & (rel_c >= 0) & (rel_c < BLK)
        out[v, :, 2 * BLK:] = np.where(ok_c, _t5_bucket_np(rel_c), -1)
    return out


def swa_bias_fwd(table, ids):
    def body(t_ref, id_ref, o_ref):
        for v in range(3):
            for h in range(8):
                o_ref[v, h] = jnp.where(id_ref[v] < 0, NEG, 0.0)

            def step(b, carry):
                hit = id_ref[v] == b
                for h in range(8):
                    o_ref[v, h] += jnp.where(hit, t_ref[b, h], 0.0)
                return carry

            lax.fori_loop(0, 32, step, 0)

    return pl.pallas_call(
        body, name="swa_bias_fwd",
        in_specs=[pl.BlockSpec(memory_space=pltpu.SMEM), pl.BlockSpec(memory_space=pltpu.VMEM)],
        out_specs=pl.BlockSpec(memory_space=pltpu.VMEM),
        out_shape=jax.ShapeDtypeStruct((3, 8, BLK, 3 * BLK), F32),
        compiler_params=pltpu.CompilerParams(vmem_limit_bytes=VMEM_LIMIT),
    )(table, ids)


def swa_bias_bwd(dbias, ids):
    def body(d_ref, id_ref, o_ref):
        r = lax.broadcasted_iota(jnp.int32, (32, LANE), 0)
        c = lax.broadcasted_iota(jnp.int32, (32, LANE), 1)

        def step(b, acc):
            for v in range(3):
                hit = id_ref[v] == b
                for h in range(8):
                    m = jnp.where(hit, d_ref[v, h], 0.0)
                    s = jnp.sum(jnp.sum(m, axis=1, keepdims=True), axis=0, keepdims=True)
                    acc = acc + jnp.where((r == b) & (c == h), s, 0.0)
            return acc

        o_ref[...] = lax.fori_loop(0, 32, step, jnp.zeros((32, LANE), F32))

    return pl.pallas_call(
        body, name="swa_bias_bwd",
        in_specs=[pl.BlockSpec(memory_space=pltpu.VMEM), pl.BlockSpec(memory_space=pltpu.VMEM)],
        out_specs=pl.BlockSpec(memory_space=pltpu.VMEM),
        out_shape=jax.ShapeDtypeStruct((32, LANE), F32),
        compiler_params=pltpu.CompilerParams(vmem_limit_bytes=VMEM_LIMIT),
    )(dbias, ids)


def _swa_block(q, k3, v3, bias, sinks):
    heads = range(8)
    kh = [k3[:, (h // 4) * 128:(h // 4 + 1) * 128] for h in heads]
    vh = [v3[:, (h // 4) * 128:(h // 4 + 1) * 128] for h in heads]
    s = [_bmm_nt(q[:, h * 128:(h + 1) * 128], kh[h]) * 0.125 + bias[h] for h in heads]
    sink = [_lane_pick(sinks, h) for h in heads]
    m = [lax.stop_gradient(jnp.maximum(jnp.max(s[h], axis=-1, keepdims=True), sink[h])) for h in heads]
    e = [jnp.exp(s[h] - m[h]) for h in heads]
    p = [e[h] / (jnp.sum(e[h], axis=-1, keepdims=True) + jnp.exp(sink[h] - m[h])) for h in heads]
    return jnp.concatenate([_bmm(p[h], vh[h]) for h in heads], axis=1)


def _swa_in_specs():
    qs = pl.BlockSpec((BLK, 1024), lambda n: (n, E_QA // 1024))
    ks = [pl.BlockSpec((BLK, 256), lambda n: (0, E_KA // 256)),
          pl.BlockSpec((BLK, 256), lambda n: (jnp.maximum(n - 1, 0), E_KA // 256)),
          pl.BlockSpec((BLK, 256), lambda n: (n, E_KA // 256))]
    vs = [pl.BlockSpec((BLK, 256), lambda n: (0, E_VA // 256)),
          pl.BlockSpec((BLK, 256), lambda n: (jnp.maximum(n - 1, 0), E_VA // 256)),
          pl.BlockSpec((BLK, 256), lambda n: (n, E_VA // 256))]
    bs = pl.BlockSpec((None, 8, BLK, 3 * BLK), lambda n: (jnp.minimum(n, 2), 0, 0, 0))
    ss = pl.BlockSpec((1, LANE), lambda n: (0, 0))
    return [qs] + ks + vs + [bs, ss]


def swa_fwd(proj, bias, sinks, side=None):
    M = proj.shape[0]
    s_ins, s_specs, s_shapes, s_sems = _side_parts(side)
    ns, nso = len(s_ins), len(s_shapes)

    def body(q_ref, k0, k1, k2, v0, v1, v2, b_ref, s_ref, *rest):
        o_ref = rest[ns]
        _side_run(side, M // BLK, rest[:ns], rest[ns + 1:ns + 1 + nso], rest[ns + 1 + nso:])
        k3 = jnp.concatenate([k0[...], k1[...], k2[...]], axis=0)
        v3 = jnp.concatenate([v0[...], v1[...], v2[...]], axis=0)
        o_ref[...] = _swa_block(q_ref[...], k3, v3, b_ref[...], s_ref[...]).astype(o_ref.dtype)

    res = pl.pallas_call(
        body, name="swa_fwd", grid=(M // BLK,),
        in_specs=_swa_in_specs() + s_specs,
        out_specs=[pl.BlockSpec((BLK, 1024), lambda n: (n, 0))] + [pl.BlockSpec(memory_space=pl.ANY)] * nso,
        out_shape=[jax.ShapeDtypeStruct((M, 1024), BF16)] + s_shapes,
        scratch_shapes=s_sems,
        compiler_params=_cparams(("arbitrary",)),
    )(proj, proj, proj, proj, proj, proj, proj, bias, sinks, *s_ins)
    return res[0], res[1:]


def swa_bwd(proj, bias, sinks, do, side=None):
    M = proj.shape[0]
    s_ins, s_specs, s_shapes, s_sems = _side_parts(side)
    ns, nso = len(s_ins), len(s_shapes)

    def body(q_ref, k0, k1, k2, v0, v1, v2, b_ref, s_ref, do_ref, *rest):
        dq_ref, dk_ref, dv_ref, db_ref, ds_ref = rest[ns:ns + 5]
        _side_run(side, M // BLK, rest[:ns], rest[ns + 5:ns + 5 + nso], rest[ns + 5 + nso:])
        n = pl.program_id(0)

        @pl.when(n == 0)
        def _():
            dk_ref[...] = jnp.zeros_like(dk_ref)
            dv_ref[...] = jnp.zeros_like(dv_ref)
            ds_ref[...] = jnp.zeros_like(ds_ref)

        k3 = jnp.concatenate([k0[...], k1[...], k2[...]], axis=0)
        v3 = jnp.concatenate([v0[...], v1[...], v2[...]], axis=0)
        _, vjp = jax.vjp(_swa_block, q_ref[...], k3, v3, b_ref[...], s_ref[...])
        dq, dk3, dv3, dbias, dsink = vjp(do_ref[...].astype(F32))
        dq_ref[...] = dq
        prev = pl.multiple_of(jnp.maximum(n - 1, 0) * BLK, BLK)
        cur = pl.multiple_of(n * BLK, BLK)
        dk_ref[pl.ds(0, BLK), :] += dk3[0:BLK]
        dv_ref[pl.ds(0, BLK), :] += dv3[0:BLK]
        dk_ref[pl.ds(prev, BLK), :] += dk3[BLK:2 * BLK]
        dv_ref[pl.ds(prev, BLK), :] += dv3[BLK:2 * BLK]
        dk_ref[pl.ds(cur, BLK), :] += dk3[2 * BLK:]
        dv_ref[pl.ds(cur, BLK), :] += dv3[2 * BLK:]
        ds_ref[...] += dsink

        @pl.when(n <= 2)
        def _():
            db_ref[...] = dbias

        @pl.when(n > 2)
        def _():
            db_ref[...] += dbias

    res = pl.pallas_call(
        body, name="swa_bwd", grid=(M // BLK,),
        in_specs=_swa_in_specs() + [pl.BlockSpec((BLK, 1024), lambda n: (n, 0))] + s_specs,
        out_specs=[pl.BlockSpec((BLK, 1024), lambda n: (n, 0)),
                   pl.BlockSpec((M, 256), lambda n: (0, 0)), pl.BlockSpec((M, 256), lambda n: (0, 0)),
                   pl.BlockSpec((None, 8, BLK, 3 * BLK), lambda n: (jnp.minimum(n, 2), 0, 0, 0)),
                   pl.BlockSpec((1, LANE), lambda n: (0, 0))] + [pl.BlockSpec(memory_space=pl.ANY)] * nso,
        out_shape=[jax.ShapeDtypeStruct((M, 1024), F32), jax.ShapeDtypeStruct((M, 256), F32),
                   jax.ShapeDtypeStruct((M, 256), F32), jax.ShapeDtypeStruct((3, 8, BLK, 3 * BLK), F32),
                   jax.ShapeDtypeStruct((1, LANE), F32)] + s_shapes,
        scratch_shapes=s_sems,
        compiler_params=_cparams(("arbitrary",)),
    )(proj, proj, proj, proj, proj, proj, proj, bias, sinks, do, *s_ins)
    return res[:5], res[5:]


def _shift_rows_impl(x, k):
    n = x.shape[0]
    rolled = pltpu.roll(x, k, 0)
    return jnp.where(_row_ids(0, n) >= k, rolled, 0.0)


def _unshift_rows_impl(g, k):
    n = g.shape[0]
    rolled = pltpu.roll(g, n - k, 0)
    return jnp.where(_row_ids(0, n) < n - k, rolled, 0.0)


@functools.partial(jax.custom_vjp, nondiff_argnums=(1,))
def _shift_rows(x, k):
    return _shift_rows_impl(x, k)


def _shift_rows_f(x, k):
    return _shift_rows_impl(x, k), None


def _shift_rows_b(k, _, g):
    return (_unshift_rows_impl(g, k),)


_shift_rows.defvjp(_shift_rows_f, _shift_rows_b)


def _conv_silu(x, w):
    rid = lax.broadcasted_iota(jnp.int32, w.shape, 0)
    y = x * jnp.sum(jnp.where(rid == 3, w, 0.0), axis=0, keepdims=True)
    for k in range(1, 4):
        y = y + _shift_rows(x, k) * jnp.sum(jnp.where(rid == 3 - k, w, 0.0), axis=0, keepdims=True)
    y = jnp.where(_row_ids(0, x.shape[0]) >= ZROWS, y, 0.0)
    return _silu(y)


def conv_fwd(proj, conv_w):
    M = proj.shape[0]
    nb = conv_w.shape[1] // LANE

    def body(x_ref, w_ref, o_ref):
        o_ref[...] = _conv_silu(x_ref[...], w_ref[...])

    return pl.pallas_call(
        body, name="conv_fwd", grid=(nb,),
        in_specs=[pl.BlockSpec((M, LANE), lambda c: (0, E_QB // LANE + c)), pl.BlockSpec((4, LANE), lambda c: (0, c))],
        out_specs=pl.BlockSpec((M, LANE), lambda c: (0, c)),
        out_shape=jax.ShapeDtypeStruct((M, conv_w.shape[1]), F32),
        compiler_params=_cparams(("parallel",)),
    )(proj, conv_w)


def conv_bwd(proj, conv_w, dy):
    M = proj.shape[0]
    nb = conv_w.shape[1] // LANE

    def body(x_ref, w_ref, dy_ref, dx_ref, dw_ref):
        _, vjp = jax.vjp(_conv_silu, x_ref[...], w_ref[...])
        dx, dw = vjp(dy_ref[...])
        dx_ref[...] = dx
        dw_ref[...] = dw

    return pl.pallas_call(
        body, name="conv_bwd", grid=(nb,),
        in_specs=[pl.BlockSpec((M, LANE), lambda c: (0, E_QB // LANE + c)), pl.BlockSpec((4, LANE), lambda c: (0, c)),
                  pl.BlockSpec((M, LANE), lambda c: (0, c))],
        out_specs=[pl.BlockSpec((M, LANE), lambda c: (0, c)), pl.BlockSpec((4, LANE), lambda c: (0, c))],
        out_shape=[jax.ShapeDtypeStruct((M, conv_w.shape[1]), F32), jax.ShapeDtypeStruct(conv_w.shape, F32)],
        compiler_params=_cparams(("parallel",)),
    )(proj, conv_w, dy)


def _fn_dn_prep(row0, yq, yk, ba, dnp):
    tm = yq.shape[0]
    real = _row_ids(row0, tm) >= ZROWS
    qs, ks, gs, bs = [], [], [], []
    for h in range(4):
        q = yq[:, h * 128:(h + 1) * 128]
        k = yk[:, h * 128:(h + 1) * 128]
        qs.append(q * lax.rsqrt(jnp.sum(q * q, axis=-1, keepdims=True) + 1e-6) * (128.0 ** -0.5))
        ks.append(k * lax.rsqrt(jnp.sum(k * k, axis=-1, keepdims=True) + 1e-6))
        beta = _sigmoid(_lane_pick(ba, h))
        g = -jnp.exp(_lane_pick(dnp, h)) * _softplus(_lane_pick(ba, 4 + h) + _lane_pick(dnp, 4 + h))
        g = jnp.where(real, g, 0.0)
        gs.append(jnp.broadcast_to(g, (tm, 128)))
        bs.append(jnp.broadcast_to(beta, (tm, 128)))
    cat = lambda xs: jnp.concatenate(xs, axis=1)
    return cat(qs), cat(ks), cat(gs), cat(bs)


def _zip(f, *lists):
    return [f(*args) for args in zip(*lists)]


def _unit_lower_inv_impl(a):
    n = a[0].shape[0]
    eye = (lax.broadcasted_iota(jnp.int32, (n, n), 0) == lax.broadcasted_iota(jnp.int32, (n, n), 1)).astype(F32)
    nn = ((1,), (0,))
    p = [-x for x in a]
    t = [eye + x for x in p]
    for _ in range(int(math.log2(n)) - 1):
        p = _zip(lambda x: _xdot(x, x, nn), p)
        t = _zip(lambda x, y: x + _xdot(x, y, nn), t, p)
    return t


@jax.custom_vjp
def _unit_lower_inv(a):
    return _unit_lower_inv_impl(a)


def _unit_lower_inv_f(a):
    t = _unit_lower_inv_impl(a)
    return t, t


def _unit_lower_inv_b(t, g):
    tg = _zip(lambda x, y: _xdot(x, y, ((0,), (0,))), t, g)
    return (_zip(lambda x, y: -_xdot(x, y, ((1,), (1,))), tg, t),)


_unit_lower_inv.defvjp(_unit_lower_inv_f, _unit_lower_inv_b)


@jax.custom_vjp
def _known_inv(a, t):
    return t


_known_inv.defvjp(lambda a, t: (t, t), lambda t, g: (_unit_lower_inv_b(t, g)[0], [jnp.zeros_like(x) for x in t]))


def _dn_block(q, k, v, gb, bb, S, t_kept=None):
    nh = len(S)
    r = lax.broadcasted_iota(jnp.int32, (CH, CH), 0)
    c = lax.broadcasted_iota(jnp.int32, (CH, CH), 1)
    tri_incl = r >= c
    gcb = _zip(_cumsum_rows, gb)
    gamma = _zip(lambda x: jnp.where(tri_incl, jnp.exp(jnp.where(tri_incl, x[:, :CH] - x[:, :CH].T, 0.0)), 0.0), gcb)
    kb = _zip(jnp.multiply, k, bb)
    vb = _zip(jnp.multiply, v, bb)
    a = _zip(lambda m, g: jnp.where(r > c, m * g, 0.0), _zip(_bmm_nt, kb, k), gamma)
    t = _unit_lower_inv(a) if t_kept is None else _known_inv(a, list(t_kept))
    eg = _zip(jnp.exp, gcb)
    u = _zip(_xmm, t, vb)
    w = _zip(_xmm, t, _zip(jnp.multiply, kb, eg))
    attn = _zip(lambda m, g: m * g, _zip(_bmm_nt, q, k), gamma)
    gtot = _zip(lambda x: jnp.sum(x, axis=0, keepdims=True), gb)
    k_dec = _zip(lambda x, gt, gc: x * jnp.exp(gt - gc), k, gtot, gcb)
    q_dec = _zip(jnp.multiply, q, eg)
    S = list(S)
    o, starts = [], []
    for i0 in range(0, len(q), nh):
        idx = range(i0, i0 + nh)
        starts.append(list(S))
        v_new = [u[i] - m for i, m in zip(idx, [_bmm(w[i], S[h]) for h, i in enumerate(idx)])]
        oq = [_bmm(q_dec[i], S[h]) for h, i in enumerate(idx)]
        oa = [_bmm(attn[i], vn) for i, vn in zip(idx, v_new)]
        kv = [_bmm_tn(k_dec[i], vn) for i, vn in zip(idx, v_new)]
        o += _zip(jnp.add, oq, oa)
        S = [S[h] * jnp.exp(jnp.broadcast_to(gtot[i], S[h].shape)) + kv[h] for h, i in enumerate(idx)]
    return o, S, starts, t


def _gla_block(q, k, v, glog, S):
    nh = len(S)
    tri = lax.broadcasted_iota(jnp.int32, (CH, CH), 0) >= lax.broadcasted_iota(jnp.int32, (CH, CH), 1)
    bcum = _zip(_cumsum_rows, glog)
    q_dec = _zip(lambda x, b: x * (128.0 ** -0.5) * jnp.exp(b), q, bcum)
    attn = _zip(lambda m: jnp.where(tri, m, 0.0), _zip(_bmm_nt, q_dec, _zip(lambda x, b: x * jnp.exp(-b), k, bcum)))
    o_in = _zip(_bmm, attn, v)
    k_dec = _zip(lambda x, g, b: x * jnp.exp(jnp.sum(g, axis=0, keepdims=True) - b), k, glog, bcum)
    decay = _zip(lambda g, x: jnp.exp(_colsum_as_rows(g, x.shape[1])), glog, v)
    kv = _zip(_bmm_tn, k_dec, v)
    S = list(S)
    o, starts = [], []
    for i0 in range(0, len(q), nh):
        idx = range(i0, i0 + nh)
        starts.append(list(S))
        o += [o_in[i] + m for i, m in zip(idx, [_bmm(q_dec[i], S[h]) for h, i in enumerate(idx)])]
        S = [S[h] * decay[i] + kv[i] for h, i in enumerate(idx)]
    return o, S, starts


class Side:
    def __init__(self, ins, out_shapes, nsem, events):
        self.ins, self.out_shapes, self.nsem, self.events = list(ins), list(out_shapes), nsem, events


def _side_parts(side):
    if side is None:
        return [], [], [], []
    anyspec = pl.BlockSpec(memory_space=pl.ANY)
    return (side.ins, [anyspec] * len(side.ins), side.out_shapes,
            [pltpu.SemaphoreType.DMA((side.nsem,)), pltpu.SemaphoreType.DMA((side.nsem,))])


def _side_run(side, n_steps, in_refs, out_refs, sems, step=None):
    if side is None:
        return
    step = pl.program_id(0) if step is None else step
    for at, fn in side.events(n_steps, in_refs, out_refs, *sems):
        pl.when(step == at)(fn)


def chunk_fwd(name, chunk_fn, ins, dv, side=None, keep=()):
    M = ins[0][0].shape[0]
    NC = M // CH
    N = NC // CPS
    ni, nk = len(ins), len(keep)
    ws = [w for (_, w, _) in ins]
    s_ins, s_specs, s_shapes, s_sems = _side_parts(side)
    ns, nso = len(s_ins), len(s_shapes)

    def body(*refs):
        o0 = ni + ns
        o_ref, sall_ref = refs[o0:o0 + 2]
        k_refs = refs[o0 + 2:o0 + 2 + nk]
        s_ref = refs[o0 + 2 + nk + nso]
        _side_run(side, N, refs[ni:o0], refs[o0 + 2 + nk:o0 + 2 + nk + nso], refs[o0 + 3 + nk + nso:])

        @pl.when(pl.program_id(0) == 0)
        def _():
            s_ref[...] = jnp.zeros_like(s_ref)

        problems = [(cc, h) for cc in range(CPS) for h in range(4)]
        lists = [[r[cc * CH:(cc + 1) * CH, h * w:(h + 1) * w] for cc, h in problems] for r, w in zip(refs[:ni], ws)]
        o, s_new, starts, *kept = chunk_fn(*lists, [s_ref[h] for h in range(4)])
        for b, (cc, h) in enumerate(problems):
            o_ref[cc * CH:(cc + 1) * CH, h * dv:(h + 1) * dv] = o[b]
            sall_ref[h, cc] = starts[cc][h]
            for k_ref, vals in zip(k_refs, kept):
                k_ref[h, cc] = vals[b]
        for h in range(4):
            s_ref[h] = s_new[h]

    per_chunk = lambda r, c: pl.BlockSpec((4, CPS, r, c), lambda n: (0, n, 0, 0))
    specs = [pl.BlockSpec((CPS * CH, 4 * w), functools.partial(lambda n, cb: (n, cb), cb=cb // 4)) for (_, w, cb) in ins]
    res = pl.pallas_call(
        body, name=name, grid=(N,),
        in_specs=specs + s_specs,
        out_specs=[pl.BlockSpec((CPS * CH, 4 * dv), lambda n: (n, 0)), per_chunk(128, dv)] + [per_chunk(r, c) for r, c in keep]
        + [pl.BlockSpec(memory_space=pl.ANY)] * nso,
        out_shape=[jax.ShapeDtypeStruct((M, 4 * dv), F32), jax.ShapeDtypeStruct((4, NC, 128, dv), F32)]
        + [jax.ShapeDtypeStruct((4, NC, r, c), F32) for r, c in keep] + s_shapes,
        scratch_shapes=[pltpu.VMEM((4, 128, dv), F32)] + s_sems,
        compiler_params=_cparams(("arbitrary",)),
    )(*[a for (a, _, _) in ins], *s_ins)
    return res[0], res[1], res[2:2 + nk], res[2 + nk:]


def chunk_bwd(name, chunk_fn, ins, dv, s_all, do, side=None, kept=()):
    M = ins[0][0].shape[0]
    N = M // CH // CPS
    ni, nk = len(ins), len(kept)
    ws = [w for (_, w, _) in ins]
    s_ins, s_specs, s_shapes, s_sems = _side_parts(side)
    ns, nso = len(s_ins), len(s_shapes)

    def body(*refs):
        sall_ref, do_ref = refs[ni:ni + 2]
        k_refs = refs[ni + 2:ni + 2 + nk]
        o0 = ni + 2 + nk + ns
        d_refs = refs[o0:o0 + ni]
        ds_ref = refs[o0 + ni + nso]
        _side_run(side, N, refs[ni + 2 + nk:o0], refs[o0 + ni:o0 + ni + nso], refs[o0 + ni + nso + 1:])

        @pl.when(pl.program_id(0) == 0)
        def _():
            ds_ref[...] = jnp.zeros_like(ds_ref)

        problems = [(cc, h) for cc in range(CPS) for h in range(4)]
        lists = [[r[cc * CH:(cc + 1) * CH, h * w:(h + 1) * w] for cc, h in problems] for r, w in zip(refs[:ni], ws)]
        kept_lists = [[k_ref[h, cc] for cc, h in problems] for k_ref in k_refs]
        _, vjp = jax.vjp(lambda *a: tuple(chunk_fn(*a)[:2]), *lists, [sall_ref[h, 0] for h in range(4)], *kept_lists)
        grads = vjp(([do_ref[cc * CH:(cc + 1) * CH, h * dv:(h + 1) * dv] for cc, h in problems],
                     [ds_ref[h] for h in range(4)]))
        for d_ref, w, g in zip(d_refs, ws, grads[:ni]):
            for b, (cc, h) in enumerate(problems):
                d_ref[cc * CH:(cc + 1) * CH, h * w:(h + 1) * w] = g[b]
        for h in range(4):
            ds_ref[h] = grads[ni][h]

    rev = lambda n: N - 1 - n
    per_chunk = lambda r, c: pl.BlockSpec((4, CPS, r, c), lambda n: (0, rev(n), 0, 0))
    specs = [pl.BlockSpec((CPS * CH, 4 * w), functools.partial(lambda n, cb: (rev(n), cb), cb=cb // 4)) for (_, w, cb) in ins]
    res = pl.pallas_call(
        body, name=name, grid=(N,),
        in_specs=specs + [per_chunk(128, dv), pl.BlockSpec((CPS * CH, 4 * dv), lambda n: (rev(n), 0))]
        + [per_chunk(*a.shape[2:]) for a in kept] + s_specs,
        out_specs=[pl.BlockSpec((CPS * CH, 4 * w), lambda n: (rev(n), 0)) for w in ws] + [pl.BlockSpec(memory_space=pl.ANY)] * nso,
        out_shape=[jax.ShapeDtypeStruct((M, 4 * w), F32) for w in ws] + s_shapes,
        scratch_shapes=[pltpu.VMEM((4, 128, dv), F32)] + s_sems,
        compiler_params=_cparams(("arbitrary",)),
    )(*[a for (a, _, _) in ins], s_all, do, *kept, *s_ins)
    return res[:ni], res[ni:]


def _fn_gate_out(hd, row0, o, z, w):
    outs = []
    for h in range(4):
        outs.append(_rms(o[:, h * hd:(h + 1) * hd], w) * _silu(z[:, h * hd:(h + 1) * hd]))
    return (jnp.concatenate(outs, axis=1),)


def _fn_gla_prep(row0, gk, wgu, bg):
    x = _mm(gk, wgu) + bg
    ls = jnp.minimum(x, 0.0) - jnp.log(1.0 + jnp.exp(-jnp.abs(x)))
    return (jnp.where(_row_ids(row0, gk.shape[0]) >= ZROWS, ls / 16.0, 0.0),)


def loss_call(y, tgt):
    M = y.shape[0]
    tm = _pick(M, 512)

    def body(y_ref, t_ref, l_ref, dy_ref):
        i = pl.program_id(0)
        e = y_ref[...] - t_ref[...]
        dy_ref[...] = e * (1.0 / D)
        part = 0.5 * jnp.sum(jnp.sum(e * e, axis=1, keepdims=True) * (1.0 / D), axis=0, keepdims=True)
        part = jnp.broadcast_to(part, (8, LANE))

        @pl.when(i == 0)
        def _():
            l_ref[...] = part

        @pl.when(i > 0)
        def _():
            l_ref[...] += part

    return pl.pallas_call(
        body, name="loss", grid=(M // tm,),
        in_specs=[pl.BlockSpec((tm, D), lambda i: (i, 0))] * 2,
        out_specs=[pl.BlockSpec((8, LANE), lambda i: (0, 0)), pl.BlockSpec((tm, D), lambda i: (i, 0))],
        out_shape=[jax.ShapeDtypeStruct((8, LANE), F32), jax.ShapeDtypeStruct((M, D), F32)],
        compiler_params=_cparams(("arbitrary",)),
    )(y, tgt)


def _bf(x):
    return x.astype(BF16)


def core_step(x, tgt, W, comm=None):
    S = x.shape[0]
    M = S + PADR
    ids = jnp.asarray(_swa_bucket_ids())
    h0 = jnp.concatenate([jnp.zeros((ZROWS, D), F32), W["meta"], x], axis=0)
    nw = W["norm"]
    nrow = lambda l, k: nw[l, k][None, :]
    ffw = W["ffn"]
    projs = {}

    def projw(l):
        if l not in projs:
            projs[l] = W["proj"](l)
        return projs[l]

    site = (lambda name: comm.side(name)) if comm else (lambda name: None)
    landed = (lambda name, outs: comm.done(name, outs)) if comm else (lambda name, outs: None)
    sinks = jnp.pad(W["sinks"], ((0, 0), (0, LANE - 8)))
    dnp = jnp.pad(jnp.concatenate([W["a_log"], W["dt_bias"]], axis=1), ((0, 0), (0, LANE - 8)))
    wgu = jnp.pad(W["gate_up"], ((0, LANE - 16), (0, 0)))
    bg = W["b_gate"]
    full = lambda a: (a, a.shape[1], 0)

    saved = []
    h = h0
    (hn,) = rowwise_fwd("prenorm_0", _fn_prenorm, [full(h)], [nrow(0, 0)], [(D, BF16)])
    bias = swa_bias_fwd(W["rel"], ids)
    for l in range(2):
        st = {"h_a": h, "hn_a": hn}
        (f1, a1, b1), got = ffn_fwd(f"ffn_fwd_{l}0", hn, *ffw(l, 0), side=site(f"ffn_fwd_{l}0"))
        landed(f"ffn_fwd_{l}0", got)
        h, hn = rowwise_fwd(f"resnorm_{l}1", functools.partial(_fn_resnorm, 0.5), [full(h), full(f1)],
                            [nrow(l, 1), nrow(l, 2)], [(D, F32), (D, BF16)])
        st.update(f1=f1, a1=a1, b1=b1, h_b=h, hn_b=hn)
        if l == 0:
            proj = mm_nt(hn, projw(0)["w_in"], "e_proj")
            o_a, got = swa_fwd(proj, bias, sinks, side=site("swa_fwd"))
            landed("swa_fwd", got)
            y = conv_fwd(proj, W["conv"])
            qn, kn, gb, bb = rowwise_fwd(
                "dn_prep", _fn_dn_prep, [(y, 512, 0), (y, 512, 1), (proj, LANE, E_BA // LANE)], [dnp], [(512, F32)] * 4)
            ins = [(qn, 128, 0), (kn, 128, 0), (y, 128, 8), (gb, 128, 0), (bb, 128, 0)]
            o_dn, s_all, (t_inv,), got = chunk_fwd("dn_fwd", _dn_block, ins, 128, keep=[(CH, CH)],
                                                   side=site("dn_fwd"))
            landed("dn_fwd", got)
            (o_b,) = rowwise_fwd("dn_out", functools.partial(_fn_gate_out, 128),
                                 [full(o_dn), (proj, 512, E_ZB // 512)], [W["dn_norm"]], [(512, BF16)])
            omix = jnp.concatenate([o_a, o_b], axis=1)
            mix = mm_nn(omix, projw(0)["w_out"], "e_mix")
            st.update(proj=proj, y=y, qn=qn, kn=kn, gb=gb, bb=bb, o_dn=o_dn, s_all=s_all, t_inv=t_inv, omix=omix)
        else:
            proj = mm_nt(hn, projw(1)["w_in"], "o_proj")
            (glog,) = rowwise_fwd("gla_prep", _fn_gla_prep, [(proj, LANE, O_GK // LANE)], [wgu, bg], [(512, F32)])
            ins = [(proj, 128, O_Q // 128), (proj, 128, O_K // 128), (proj, 256, O_V // 256), (glog, 128, 0)]
            o_g, s_all, _, _ = chunk_fwd("gla_fwd", _gla_block, ins, 256)
            (omix,) = rowwise_fwd("gla_out", functools.partial(_fn_gate_out, 256),
                                  [full(o_g), (proj, 1024, O_G // 1024)], [W["gla_norm"]], [(1024, BF16)])
            mix = mm_nn(omix, projw(1)["w_out"], "o_mix")
            st.update(proj=proj, glog=glog, o_g=o_g, s_all=s_all, omix=omix)
        h, hn = rowwise_fwd(f"resnorm_{l}3", functools.partial(_fn_resnorm, 1.0), [full(h), full(mix)],
                            [nrow(l, 3), nrow(l, 4)], [(D, F32), (D, BF16)])
        st.update(mix=mix, h_c=h, hn_c=hn)
        (f2, a2, b2), got = ffn_fwd(f"ffn_fwd_{l}1", hn, *ffw(l, 1), side=site(f"ffn_fwd_{l}1"))
        landed(f"ffn_fwd_{l}1", got)
        st.update(f2=f2, a2=a2, b2=b2)
        if l == 0:
            h, hn = rowwise_fwd("resnorm_05", functools.partial(_fn_resnorm, 0.5), [full(h), full(f2)],
                                [nrow(0, 5), nrow(1, 0)], [(D, F32), (D, BF16)])
        else:
            (h,) = rowwise_fwd("res_last", functools.partial(_fn_res_last, 0.5), [full(h), full(f2)],
                               [nrow(1, 5)], [(D, F32)])
        saved.append(st)

    loss_blk, dy = loss_call(h[PADR:], tgt)
    dh = jnp.concatenate([jnp.zeros((PADR, D), F32), dy], axis=0)

    G = {}
    dnorm = [[None] * 6 for _ in range(2)]
    dWg = [[None, None], [None, None]]
    dWu = [[None, None], [None, None]]
    dWd = [[None, None], [None, None]]
    dhn = None
    for l in (1, 0):
        st = saved[l]
        if l == 1:
            (dh_, df), (dw5,) = rowwise_bwd(
                "res_last_b", functools.partial(_fn_res_last, 0.5), [full(st["h_c"]), full(st["f2"])], [nrow(1, 5)],
                [full(dh)], [F32, BF16])
            dnorm[1][5] = dw5
        else:
            (dh_, df), (dw5, dw0n) = rowwise_bwd(
                "resnorm_05_b", functools.partial(_fn_resnorm, 0.5), [full(st["h_c"]), full(st["f2"])],
                [nrow(0, 5), nrow(1, 0)], [full(dh), full(dhn)], [F32, BF16])
            dnorm[0][5] = dw5
            dnorm[1][0] = dw0n
        dh = dh_
        if comm and l == 0:
            units = (0, 1, 2, U_IN, U_OUT)
            comm.grads(1, units, layer_grad_items(1, dWg, dWu, dWd, G["o_in"], G["o_out"], only=units))
        (dxn, da, db, hm), got = ffn_bwd_x(f"ffn_bx_{l}1", df, st["a2"], st["b2"], *ffw(l, 1), side=site(f"ffn_bx_{l}1"))
        landed(f"ffn_bx_{l}1", got)
        dWg[l][1], dWu[l][1], dWd[l][1] = ffn_bwd_w(f"ffn_bw_{l}1", st["hn_c"], df, da, db, hm)
        if comm:
            comm.grads(l, (3, 4, 5), layer_grad_items(l, dWg, dWu, dWd, None, None, only=(3, 4, 5)))
        (dh_, dmix), (dw3, dw4) = rowwise_bwd(
            f"resnorm_{l}3_b", functools.partial(_fn_resnorm, 1.0), [full(st["h_b"]), full(st["mix"])],
            [nrow(l, 3), nrow(l, 4)], [full(dh), full(dxn)], [F32, BF16])
        dnorm[l][3], dnorm[l][4] = dw3, dw4
        dh = dh_
        proj = st["proj"]
        if l == 1:
            G["o_out"] = mm_tn(st["omix"], dmix, "o_out_dw")
            domix = mm_nt(dmix, projw(1)["w_out"], "o_mix_dx")
            (do_g, dgate), (dgn,) = rowwise_bwd(
                "gla_out_b", functools.partial(_fn_gate_out, 256), [full(st["o_g"]), (proj, 1024, O_G // 1024)],
                [W["gla_norm"]], [full(domix)], [F32, F32])
            G["gla_norm"] = dgn
            ins = [(proj, 128, O_Q // 128), (proj, 128, O_K // 128), (proj, 256, O_V // 256), (st["glog"], 128, 0)]
            (dq, dk, dv, dglog), _ = chunk_bwd("gla_bwd", _gla_block, ins, 256, st["s_all"], do_g)
            (dgk,), (dwgu, dbg) = rowwise_bwd("gla_prep_b", _fn_gla_prep, [(proj, LANE, O_GK // LANE)], [wgu, bg],
                                              [full(dglog)], [F32])
            G["gate_up"] = dwgu[:16]
            G["b_gate"] = dbg
            dproj = _bf(jnp.concatenate([dq, dk, dv, dgate, dgk, jnp.zeros((M, O_END - O_GK - LANE), F32)], axis=1))
            G["o_in"] = mm_tn(dproj, st["hn_b"], "o_in_dw")
            dhn_b = mm_nn(dproj, projw(1)["w_in"], "o_proj_dx")
        else:
            G["e_out"] = mm_tn(st["omix"], dmix, "e_out_dw")
            domix = mm_nt(dmix, projw(0)["w_out"], "e_mix_dx")
            (do_dn, dz), (ddn,) = rowwise_bwd(
                "dn_out_b", functools.partial(_fn_gate_out, 128), [full(st["o_dn"]), (proj, 512, E_ZB // 512)],
                [W["dn_norm"]], [(domix, 512, 2)], [F32, F32])
            G["dn_norm"] = ddn
            ins = [(st["qn"], 128, 0), (st["kn"], 128, 0), (st["y"], 128, 8), (st["gb"], 128, 0), (st["bb"], 128, 0)]
            (dqn, dkn, dvv, dgb, dbb), got = chunk_bwd("dn_bwd", _dn_block, ins, 128, st["s_all"], do_dn, side=site("dn_bwd"),
                                                        kept=[st["t_inv"]])
            landed("dn_bwd", got)
            (dyq, dyk, dba), (ddnp,) = rowwise_bwd(
                "dn_prep_b", _fn_dn_prep, [(st["y"], 512, 0), (st["y"], 512, 1), (proj, LANE, E_BA // LANE)], [dnp],
                [full(dqn), full(dkn), full(dgb), full(dbb)], [F32, F32, F32])
            G["a_log"] = ddnp[:, 0:4]
            G["dt_bias"] = ddnp[:, 4:8]
            dyc = jnp.concatenate([dyq, dyk, dvv], axis=1)
            dxc, dconv = conv_bwd(proj, W["conv"], dyc)
            G["conv"] = dconv
            (dq_a, dk_a, dv_a, dbias, dsink), got = swa_bwd(proj, bias, sinks, domix, side=site("swa_bwd"))
            landed("swa_bwd", got)
            G["sinks"] = dsink[:, :8]
            G["rel"] = swa_bias_bwd(dbias, ids)[:, :8]
            dproj = _bf(jnp.concatenate([dq_a, dk_a, dv_a, dxc, dz, dba, jnp.zeros((M, E_END - E_BA - LANE), F32)], axis=1))
            G["e_in"] = mm_tn(dproj, st["hn_b"], "e_in_dw")
            dhn_b = mm_nn(dproj, projw(0)["w_in"], "e_proj_dx")
        (dh_, df), (dw1, dw2) = rowwise_bwd(
            f"resnorm_{l}1_b", functools.partial(_fn_resnorm, 0.5), [full(st["h_a"]), full(st["f1"])],
            [nrow(l, 1), nrow(l, 2)], [full(dh), full(dhn_b)], [F32, BF16])
        dnorm[l][1], dnorm[l][2] = dw1, dw2
        dh = dh_
        if comm and l == 0:
            units = (U_IN, U_OUT)
            comm.grads(0, units, layer_grad_items(0, dWg, dWu, dWd, G["e_in"], G["e_out"], only=units))
        (dxn, da, db, hm), got = ffn_bwd_x(f"ffn_bx_{l}0", df, st["a1"], st["b1"], *ffw(l, 0), side=site(f"ffn_bx_{l}0"))
        landed(f"ffn_bx_{l}0", got)
        dWg[l][0], dWu[l][0], dWd[l][0] = ffn_bwd_w(f"ffn_bw_{l}0", st["hn_a"], df, da, db, hm)
        dhn = dxn
    (dh0p,), (dw00,) = rowwise_bwd("prenorm_0_b", _fn_prenorm, [full(saved[0]["h_a"])], [nrow(0, 0)], [full(dhn)], [F32])
    dnorm[0][0] = dw00
    dh = dh + dh0p
    G["meta"] = dh[ZROWS:PADR]
    G["norm"] = jnp.stack([jnp.concatenate(r, axis=0) for r in dnorm], axis=0)
    G["items"] = [layer_grad_items(0, dWg, dWu, dWd, G["e_in"], G["e_out"], only=(0, 1, 2) if comm else range(NUNITS)),
                  None if comm else layer_grad_items(1, dWg, dWu, dWd, G["o_in"], G["o_out"])]
    return loss_blk, dh[PADR:], G


NAMES = [("meta", "meta_tokens"), ("norm", "norm_w"), ("ffn_g", "ffn_w_gate"), ("ffn_u", "ffn_w_up"),
         ("ffn_d", "ffn_w_down"), ("rel", "rel_bias_table"), ("e_in", "even_w_in"), ("conv", "even_conv_w"),
         ("sinks", "swa_sinks"), ("a_log", "dn_a_log"), ("dt_bias", "dn_dt_bias"), ("dn_norm", "dn_norm_w"),
         ("e_out", "even_w_out"), ("o_in", "odd_w_in"), ("gate_up", "gla_w_gate_up"), ("b_gate", "gla_b_gate"),
         ("gla_norm", "gla_norm_w"), ("o_out", "odd_w_out")]
BIG = ["ffn_g", "ffn_u", "ffn_d", "e_in", "e_out", "o_in", "o_out"]
IN_ROWS = 800
SMALL = [("meta", (16, 256)), ("norm", (2, 6, 256)), ("conv", (1, 4, 384)), ("gate_up", (1, 16, 128)),
         ("b_gate", (1, 128)), ("gla_norm", (1, 64))]
REPL = [("rel", (32, 8)), ("sinks", (1, 8)), ("a_log", (1, 4)), ("dt_bias", (1, 4)), ("dn_norm", (1, 128))]
SMALL_REP = 88 * LANE
SMALL_ROWS = 96


def pack_small(t):
    a = jnp.concatenate([t[n].reshape(-1) for n, _ in SMALL])
    b = jnp.concatenate([t[n].reshape(-1) for n, _ in REPL])
    flat = jnp.concatenate([a, jnp.zeros((SMALL_REP - a.shape[0],), F32), b,
                            jnp.zeros((SMALL_ROWS * LANE - SMALL_REP - b.shape[0],), F32)])
    return flat.reshape(SMALL_ROWS, LANE)


def unpack_small(p):
    flat = p.reshape(-1)
    out, r = {}, 0
    for n, shp in SMALL:
        k = int(np.prod(shp))
        out[n] = flat[r:r + k].reshape(shp)
        r += k
    r = SMALL_REP
    for n, shp in REPL:
        k = int(np.prod(shp))
        out[n] = flat[r:r + k].reshape(shp)
        r += k
    return out


IN_SRC = (706, 772)


NUNITS = 8
U_IN, U_OUT = 6, 7


def _halves(a):
    return a.reshape(2, a.shape[0] // 2, D)


def weight_pieces(wt, l):
    inn = wt["e_in" if l == 0 else "o_in"]
    inn = jnp.pad(inn, ((0, IN_ROWS - inn.shape[0]), (0, 0)))
    ffn = [_halves(wt[n][l][j]) for j in range(2) for n in ("ffn_g", "ffn_u", "ffn_d")]
    return ffn + [_halves(inn), _halves(wt["e_out" if l == 0 else "o_out"])]


def ffn_weights(q, j):
    return tuple(q[3 * j + k].reshape(NSH, FSH, D) for k in range(3))


def proj_weights(l, q_in, q_out):
    m = _even_in_map() if l == 0 else _odd_in_map()
    src = np.where(m >= 0, (m // IN_SRC[l]) * IN_ROWS + m % IN_SRC[l], -1)
    w_out = q_out.reshape(NSH * 256, D)
    return {"w_in": _rows(q_in.reshape(NSH * IN_ROWS, D), src),
            "w_out": _rows(w_out, _even_out_map()) if l == 0 else w_out}


def layer_grad_items(l, dwg, dwu, dwd, g_in, g_out, only=range(NUNITS)):
    units = [[None] * NUNITS for _ in range(2)]

    def put(i, a):
        a = a.reshape(NSH, 2, a.shape[1] // 2, D)
        units[0][i], units[1][i] = a[:, 0], a[:, 1]

    for j in range(2):
        for k, t in enumerate((dwg, dwu, dwd)):
            if 3 * j + k in only:
                put(3 * j + k, t[l][j])
    if U_IN in only:
        m = _even_in_map() if l == 0 else _odd_in_map()
        gi = _rows(g_in, _inverse(m, NSH * IN_SRC[l])).reshape(NSH, IN_SRC[l], D)
        put(U_IN, _bf(jnp.pad(gi, ((0, 0), (0, IN_ROWS - IN_SRC[l]), (0, 0)))))
    if U_OUT in only:
        if l == 0:
            g_out = _rows(g_out, _inverse(_even_out_map(), 1024))
        put(U_OUT, _bf(g_out).reshape(NSH, 256, D))
    return units


def assemble_layer(l, r0, r1):
    whole = lambda i: jnp.concatenate([r0[i], r1[i]])
    return {"ffn_g": jnp.stack([whole(0), whole(3)]), "ffn_u": jnp.stack([whole(1), whole(4)]),
            "ffn_d": jnp.stack([whole(2), whole(5)]), "in": whole(U_IN)[:IN_SRC[l]], "out": whole(U_OUT)}


def big_grads(l0, l1):
    st = lambda n: jnp.stack([l0[n], l1[n]])
    return {"ffn_g": st("ffn_g"), "ffn_u": st("ffn_u"), "ffn_d": st("ffn_d"), "e_in": l0["in"], "e_out": l0["out"],
            "o_in": l1["in"], "o_out": l1["out"]}


def small_from_gathered(gs):
    sm = [unpack_small(gs[s]) for s in range(NSH)]
    full = {}
    full["meta"] = jnp.concatenate([sm[s]["meta"] for s in range(NSH)], axis=1)
    full["norm"] = jnp.concatenate([sm[s]["norm"] for s in range(NSH)], axis=2)
    full["conv"] = jnp.concatenate([sm[s]["conv"][0] for s in range(NSH)], axis=1)
    full["gate_up"] = jnp.concatenate([sm[s]["gate_up"][0] for s in range(NSH)], axis=1)
    full["b_gate"] = jnp.concatenate([sm[s]["b_gate"] for s in range(NSH)], axis=1)
    full["gla_norm"] = jnp.concatenate([sm[s]["gla_norm"] for s in range(NSH)], axis=1)
    return full


def _col_sh(w):
    return jnp.moveaxis(w.reshape(w.shape[0], NSH, w.shape[1] // NSH), 1, 0)


def _to_t(n, a):
    if n in ("ffn_g", "ffn_u"):
        return jnp.swapaxes(a, 2, 3)
    if n in ("e_in", "o_in"):
        return jnp.swapaxes(a[0], 0, 1)
    return a if n == "ffn_d" else a[0]


def _from_t(n, a):
    if n in ("ffn_g", "ffn_u"):
        return jnp.swapaxes(a, 2, 3)
    if n in ("e_in", "o_in"):
        return jnp.swapaxes(a, 0, 1)[None]
    return a if n == "ffn_d" else a[None]


def pack_small_grads(G):
    col_sh = _col_sh
    norm_sh = jnp.moveaxis(G["norm"].reshape(2, 6, NSH, 256), 2, 0)
    a = jnp.concatenate([col_sh(G["meta"]).reshape(NSH, -1), norm_sh.reshape(NSH, -1), col_sh(G["conv"]).reshape(NSH, -1),
                         col_sh(G["gate_up"]).reshape(NSH, -1), G["b_gate"].reshape(NSH, -1),
                         G["gla_norm"].reshape(NSH, -1)], axis=1)
    b = jnp.concatenate([G[n].reshape(-1) for n, _ in REPL])
    b = jnp.broadcast_to(b[None], (NSH, b.shape[0]))
    small = jnp.concatenate([a, jnp.zeros((NSH, SMALL_REP - a.shape[1]), F32), b,
                             jnp.zeros((NSH, SMALL_ROWS * LANE - SMALL_REP - b.shape[1]), F32)], axis=1)
    return small.reshape(NSH, SMALL_ROWS, LANE)


MESH = pl.DeviceIdType.MESH
ANY = pl.BlockSpec(memory_space=pl.ANY)
VMEM = pl.BlockSpec(memory_space=pltpu.VMEM)


def _place():
    return lax.axis_index("x"), lax.axis_index("y"), lax.axis_index("c")


def _other_chips(x, y):
    return [(1 - x, y), (x, 1 - y), (1 - x, 1 - y)]


def _rcopy(send_sems, recv_sems, k, src, dst, to):
    return pltpu.make_async_remote_copy(src_ref=src, dst_ref=dst, send_sem=send_sems.at[k], recv_sem=recv_sems.at[k],
                                        device_id=to, device_id_type=MESH)


def _gather_steps(in_refs, out_refs, send_sems, recv_sems):
    n = len(in_refs)
    x, y, c = _place()
    s = 2 * x + y
    chips = _other_chips(x, y)
    copy = functools.partial(_rcopy, send_sems, recv_sems)
    pairs = [(i, j, cx, cy) for i in range(n) for j, (cx, cy) in enumerate(chips)]
    pushes = lambda: [copy(i * 3 + j, in_refs[i].at[c], out_refs[i].at[s, c], (cx, cy, c)) for i, j, cx, cy in pairs]
    landed = lambda i, cx, cy, half: out_refs[i].at[2 * cx + cy, half]
    relays = lambda: [copy(3 * n + i * 3 + j, landed(i, cx, cy, c), landed(i, cx, cy, c), (x, y, 1 - c)) for i, j, cx, cy in pairs]

    def start():
        for cp in pushes():
            cp.start()

    def relay():
        for i, j, cx, cy in pairs:
            copy(i * 3 + j, landed(i, cx, cy, c), landed(i, cx, cy, c), (x, y, c)).wait_recv()
        for cp in relays():
            cp.start()

    def finish():
        for i, j, cx, cy in pairs:
            copy(3 * n + i * 3 + j, landed(i, cx, cy, 1 - c), landed(i, cx, cy, 1 - c), (x, y, c)).wait_recv()
        for cp in pushes() + relays():
            cp.wait_send()

    return start, relay, finish


def _gather_shapes(pieces):
    return [jax.ShapeDtypeStruct((NSH,) + a.shape, a.dtype) for a in pieces]


def ag_layer(name, pieces):
    n = len(pieces)

    def body(*refs):
        for fn in _gather_steps(refs[:n], refs[n:2 * n], *refs[2 * n:]):
            fn()

    return pl.pallas_call(
        body, name=name, in_specs=[ANY] * n, out_specs=[ANY] * n, out_shape=_gather_shapes(pieces),
        scratch_shapes=[pltpu.SemaphoreType.DMA((6 * n,)), pltpu.SemaphoreType.DMA((6 * n,))],
    )(*pieces)


def gather_side(pieces):
    def events(n_steps, in_refs, out_refs, send_sems, recv_sems):
        start, relay, finish = _gather_steps(in_refs, out_refs, send_sems, recv_sems)
        return [(0, start), (max(3 * n_steps // 4, 1), relay), (n_steps - 1, finish)]

    return Side(pieces, _gather_shapes(pieces), 6 * len(pieces), events)


def ag_small(pack):
    def body(x_ref, out_ref, send_sems, recv_sems):
        x, y, c = _place()
        s = 2 * x + y
        chips = _other_chips(x, y)

        def copy(k, src, dst, to):
            return pltpu.make_async_remote_copy(src_ref=src, dst_ref=dst, send_sem=send_sems.at[k], recv_sem=recv_sems.at[k],
                                                device_id=to, device_id_type=MESH)

        out_ref[s] = x_ref[...]
        sends = [copy(j, x_ref, out_ref.at[s], (cx, cy, c)) for j, (cx, cy) in enumerate(chips)]
        for cp in sends:
            cp.start()
        for j, (cx, cy) in enumerate(chips):
            blk = out_ref.at[2 * cx + cy]
            copy(j, blk, blk, (x, y, c)).wait_recv()
        for cp in sends:
            cp.wait_send()

    return pl.pallas_call(
        body, name="ag_small", in_specs=[VMEM], out_specs=VMEM,
        out_shape=jax.ShapeDtypeStruct((NSH,) + pack.shape, pack.dtype),
        scratch_shapes=[pltpu.SemaphoreType.DMA((3,)), pltpu.SemaphoreType.DMA((3,))],
    )(pack)


def rs_pair(name, items):
    ni = len(items[0])

    def body(*refs):
        in_refs = [refs[:ni], refs[ni:2 * ni]]
        recv_refs = refs[2 * ni:3 * ni]
        send_sems, recv_sems = refs[3 * ni:]
        x, y, c = _place()
        copy = functools.partial(_rcopy, send_sems, recv_sems)
        for cc in range(2):
            @pl.when(c == cc)
            def _():
                cps = [copy(i * NSH + s, in_refs[1 - cc][i].at[s], recv_refs[i].at[s], (x, y, 1 - c))
                       for i in range(ni) for s in range(NSH)]
                for cp in cps:
                    cp.start()
                for cp in cps:
                    cp.wait()

    return pl.pallas_call(
        body, name=name, in_specs=[ANY] * (2 * ni), out_specs=[ANY] * ni,
        out_shape=[jax.ShapeDtypeStruct(a.shape, a.dtype) for a in items[0]],
        scratch_shapes=[pltpu.SemaphoreType.DMA((ni * NSH,)), pltpu.SemaphoreType.DMA((ni * NSH,))],
    )(*items[0], *items[1])


def _scatter_steps(a_refs, out_refs, send_sems, recv_sems):
    n = len(a_refs)
    x, y, c = _place()
    s = 2 * x + y
    chips = _other_chips(x, y)
    copy = functools.partial(_rcopy, send_sems, recv_sems)
    pairs = [(i, j, cx, cy) for i in range(n) for j, (cx, cy) in enumerate(chips)]
    sends = lambda: [copy(i * 3 + j, a_refs[i].at[2 * cx + cy], out_refs[i].at[s], (cx, cy, c)) for i, j, cx, cy in pairs]

    def start():
        for cp in sends():
            cp.start()

    def finish():
        for i, j, cx, cy in pairs:
            blk = out_refs[i].at[2 * cx + cy]
            copy(i * 3 + j, blk, blk, (x, y, c)).wait_recv()
        for cp in sends():
            cp.wait_send()

    return start, finish


def rs_chips(name, arrs):
    n = len(arrs)

    def body(*refs):
        for fn in _scatter_steps(refs[:n], refs[n:2 * n], *refs[2 * n:]):
            fn()

    return pl.pallas_call(
        body, name=name, in_specs=[ANY] * n, out_specs=[ANY] * n,
        out_shape=[jax.ShapeDtypeStruct(a.shape, a.dtype) for a in arrs],
        scratch_shapes=[pltpu.SemaphoreType.DMA((3 * n,)), pltpu.SemaphoreType.DMA((3 * n,))],
    )(*arrs)


def scatter_side(arrs):
    def events(n_steps, in_refs, out_refs, send_sems, recv_sems):
        start, finish = _scatter_steps(in_refs, out_refs, send_sems, recv_sems)
        return [(0, start), (n_steps - 1, finish)]

    return Side(arrs, [jax.ShapeDtypeStruct(a.shape, a.dtype) for a in arrs], 3 * len(arrs), events)


def _pair_chunks(rows):
    return 4 if rows % 32 == 0 else (2 if rows % 16 == 0 else 1)


def ag_pair(name, arrs):
    n = len(arrs)
    chunks = [(i, k * (a.shape[0] // _pair_chunks(a.shape[0])), a.shape[0] // _pair_chunks(a.shape[0]))
              for i, a in enumerate(arrs) for k in range(_pair_chunks(a.shape[0]))]

    def body(*refs):
        g_refs, out_refs = refs[:n], refs[n:2 * n]
        send_sems, recv_sems = refs[2 * n:]
        x, y, c = _place()
        give = [_rcopy(send_sems, recv_sems, q, g_refs[i].at[pl.ds(r0, rc)], out_refs[i].at[pl.ds(r0, rc)], (x, y, 1 - c))
                for q, (i, r0, rc) in enumerate(chunks)]
        for cp in give:
            cp.start()
        for cp in give:
            cp.wait()

    return pl.pallas_call(
        body, name=name, in_specs=[ANY] * n, out_specs=[ANY] * n,
        out_shape=[jax.ShapeDtypeStruct(a.shape, a.dtype) for a in arrs],
        scratch_shapes=[pltpu.SemaphoreType.DMA((len(chunks),)), pltpu.SemaphoreType.DMA((len(chunks),))],
    )(*arrs)


def small_allreduce(p):
    def body(p_ref, out_ref, rbuf, send_sems, recv_sems):
        x, y, c = _place()
        me = 4 * x + 2 * y + c
        rbuf[me] = p_ref[2 * x + y]
        flip = lambda v, f: (1 - v) if f else v
        peers = [(flip(x, k >> 2 & 1), flip(y, k >> 1 & 1), flip(c, k & 1)) for k in range(1, 8)]

        def copy(k, src, dst, to):
            return pltpu.make_async_remote_copy(src_ref=src, dst_ref=dst, send_sem=send_sems.at[k], recv_sem=recv_sems.at[k],
                                                device_id=to, device_id_type=MESH)

        sends = [copy(k, p_ref.at[2 * px + py], rbuf.at[me], (px, py, pc)) for k, (px, py, pc) in enumerate(peers)]
        for cp in sends:
            cp.start()
        for k, (px, py, pc) in enumerate(peers):
            blk = rbuf.at[4 * px + 2 * py + pc]
            copy(k, blk, blk, (x, y, c)).wait_recv()
        for cp in sends:
            cp.wait_send()
        acc = rbuf[0]
        for d in range(1, 8):
            acc = acc + rbuf[d]
        out_ref[...] = acc

    return pl.pallas_call(
        body, name="small_allreduce", in_specs=[VMEM], out_specs=VMEM,
        out_shape=jax.ShapeDtypeStruct(p.shape[1:], F32),
        scratch_shapes=[pltpu.VMEM((8,) + p.shape[1:], F32), pltpu.SemaphoreType.DMA((7,)), pltpu.SemaphoreType.DMA((7,))],
    )(p)


def _rows_tile(rows, cap):
    return _pick(rows, cap) if rows % 128 == 0 else rows


def sum_pair(name, a0s, a1s, recvs, cflag):
    n = len(recvs)

    def body(c_ref, *refs):
        for i in range(n):
            a0_ref, a1_ref, b_ref, o_ref = refs[i], refs[n + i], refs[2 * n + i], refs[3 * n + i]
            own = jnp.where(c_ref[0] == 0, a0_ref[...].astype(F32), a1_ref[...].astype(F32))
            o_ref[...] = (own + b_ref[...].astype(F32)).astype(o_ref.dtype)

    specs = [pl.BlockSpec((None,) + a.shape[1:], lambda s: (s, 0, 0)) for a in recvs]
    return pl.pallas_call(
        body, name=name, grid=(NSH,), in_specs=[pl.BlockSpec(memory_space=pltpu.SMEM)] + specs * 3, out_specs=specs,
        out_shape=[jax.ShapeDtypeStruct(a.shape, BF16) for a in recvs], compiler_params=_cparams(("parallel",)),
    )(cflag, *a0s, *a1s, *recvs)


def sum_chips(name, parts, owns, sflag):
    n = len(parts)

    def body(s_ref, *refs):
        for i in range(n):
            p_ref, a_ref, o_ref = refs[i], refs[n + i], refs[2 * n + i]
            acc = None
            for t in range(NSH):
                term = jnp.where(s_ref[0] == t, a_ref[t].astype(F32), p_ref[t].astype(F32))
                acc = term if acc is None else acc + term
            o_ref[...] = acc

    specs = [pl.BlockSpec(a.shape, lambda i: (0, 0, 0)) for a in parts]
    return pl.pallas_call(
        body, name=name, grid=(1,), in_specs=[pl.BlockSpec(memory_space=pltpu.SMEM)] + specs * 2,
        out_specs=[pl.BlockSpec(a.shape[1:], lambda i: (0, 0)) for a in parts],
        out_shape=[jax.ShapeDtypeStruct(a.shape[1:], F32) for a in parts], compiler_params=_cparams(("arbitrary",)),
    )(sflag, *parts, *owns)


ADAM_LR, ADAM_B1, ADAM_B2, ADAM_EPS, ADAM_WD, ADAM_STEP = 0.001, 0.9, 0.999, 1e-08, 0.01, 10


def adamw_call(name, w, g, m, v):
    rows, cols = w.shape
    tr = _rows_tile(rows, 512)

    def body(w_ref, g_ref, m_ref, v_ref, d_ref, nm_ref, nv_ref):
        g_ = g_ref[...]
        m_ = ADAM_B1 * m_ref[...] + (1.0 - ADAM_B1) * g_
        v_ = ADAM_B2 * v_ref[...] + (1.0 - ADAM_B2) * (g_ * g_)
        m_hat = m_ / (1.0 - ADAM_B1 ** ADAM_STEP)
        v_hat = v_ / (1.0 - ADAM_B2 ** ADAM_STEP)
        d_ref[...] = -ADAM_LR * (m_hat / (jnp.sqrt(v_hat) + ADAM_EPS) + ADAM_WD * w_ref[...])
        nm_ref[...] = m_
        nv_ref[...] = v_

    spec = pl.BlockSpec((tr, cols), lambda i: (i, 0))
    sh = jax.ShapeDtypeStruct((rows, cols), F32)
    return pl.pallas_call(
        body, name=name, grid=(rows // tr,), in_specs=[spec] * 4, out_specs=[spec] * 3, out_shape=[sh] * 3,
        compiler_params=_cparams(("parallel",)),
    )(w, g, m, v)


def kernel(x, meta_tokens, norm_w, ffn_w_gate, ffn_w_up, ffn_w_down, rel_bias_table, even_w_in, even_conv_w, swa_sinks, dn_a_log, dn_dt_bias, dn_norm_w, even_w_out, odd_w_in, gla_w_gate_up, gla_b_gate, gla_norm_w, odd_w_out, loss_target, m_meta_tokens, m_norm_w, m_ffn_w_gate, m_ffn_w_up, m_ffn_w_down, m_rel_bias_table, m_even_w_in, m_even_conv_w, m_swa_sinks, m_dn_a_log, m_dn_dt_bias, m_dn_norm_w, m_even_w_out, m_odd_w_in, m_gla_w_gate_up, m_gla_b_gate, m_gla_norm_w, m_odd_w_out, v_meta_tokens, v_norm_w, v_ffn_w_gate, v_ffn_w_up, v_ffn_w_down, v_rel_bias_table, v_even_w_in, v_even_conv_w, v_swa_sinks, v_dn_a_log, v_dn_dt_bias, v_dn_norm_w, v_even_w_out, v_odd_w_in, v_gla_w_gate_up, v_gla_b_gate, v_gla_norm_w, v_odd_w_out):
    ws = [meta_tokens, norm_w, ffn_w_gate, ffn_w_up, ffn_w_down, rel_bias_table, even_w_in, even_conv_w, swa_sinks, dn_a_log,
          dn_dt_bias, dn_norm_w, even_w_out, odd_w_in, gla_w_gate_up, gla_b_gate, gla_norm_w, odd_w_out]
    ms = [m_meta_tokens, m_norm_w, m_ffn_w_gate, m_ffn_w_up, m_ffn_w_down, m_rel_bias_table, m_even_w_in, m_even_conv_w,
          m_swa_sinks, m_dn_a_log, m_dn_dt_bias, m_dn_norm_w, m_even_w_out, m_odd_w_in, m_gla_w_gate_up, m_gla_b_gate,
          m_gla_norm_w, m_odd_w_out]
    vs = [v_meta_tokens, v_norm_w, v_ffn_w_gate, v_ffn_w_up, v_ffn_w_down, v_rel_bias_table, v_even_w_in, v_even_conv_w,
          v_swa_sinks, v_dn_a_log, v_dn_dt_bias, v_dn_norm_w, v_even_w_out, v_odd_w_in, v_gla_w_gate_up, v_gla_b_gate,
          v_gla_norm_w, v_odd_w_out]
    short = [n for n, _ in NAMES]
    w = dict(zip(short, ws))
    m = dict(zip(short, ms))
    v = dict(zip(short, vs))

    wt = {n: _to_t(n, w[n]) for n in BIG}
    own = {n: wt[n].astype(BF16) for n in BIG}
    sflag = (2 * lax.axis_index("x") + lax.axis_index("y")).astype(jnp.int32).reshape(1)
    cflag = lax.axis_index("c").astype(jnp.int32).reshape(1)
    is0 = cflag[0] == 0
    fill = lambda got, pieces: [lax.dynamic_update_index_in_dim(g_, p_, sflag[0], 0) for g_, p_ in zip(got, pieces)]
    pieces = [weight_pieces(own, l) for l in range(2)]
    small = small_from_gathered(ag_small(pack_small(w)))
    def reduce_finish(l, mine, parts):
        red = [None] * NUNITS
        for grp in ((0, 1, 2), (3, 4, 5), (U_IN, U_OUT)):
            outs = sum_chips(f"sum_chips_{l}{grp[0]}", [parts[i] for i in grp], [mine[i] for i in grp], sflag)
            for i, o in zip(grp, outs):
                red[i] = o
        got = ag_pair(f"ag_pair_{l}", red)
        return [jnp.where(is0, r_, g_) for r_, g_ in zip(red, got)], [jnp.where(is0, g_, r_) for r_, g_ in zip(red, got)]

    class Exchanges:
        gathers = {"ffn_fwd_00": [(0, 3), (0, U_IN), (0, U_OUT)], "swa_fwd": [(0, 4)], "dn_fwd": [(0, 5), (1, 0), (1, 1)],
                   "ffn_fwd_01": [(1, 2), (1, U_IN), (1, U_OUT)], "ffn_fwd_10": [(1, 3), (1, 4), (1, 5)]}
        scatters = {"ffn_bx_10": [(1, 3), (1, 4), (1, 5)], "ffn_bx_01": [(1, 0), (1, 1), (1, 2)],
                    "dn_bwd": [(1, U_IN), (1, U_OUT), (0, 3)], "swa_bwd": [(0, 4), (0, 5)], "ffn_bx_00": [(0, U_IN), (0, U_OUT)]}

        def __init__(self):
            self.q = [[None] * NUNITS for _ in range(2)]
            self.mine = [[None] * NUNITS for _ in range(2)]
            self.parts = [[None] * NUNITS for _ in range(2)]

        def side(self, name):
            if name in self.gathers:
                return gather_side([pieces[l][i] for l, i in self.gathers[name]])
            if name in self.scatters:
                return scatter_side([self.mine[l][i] for l, i in self.scatters[name]])
            return None

        def done(self, name, outs):
            if name in self.gathers:
                units = self.gathers[name]
                for (l, i), a in zip(units, fill(outs, [pieces[l][i] for l, i in units])):
                    self.q[l][i] = a
            if name in self.scatters:
                for (l, i), a in zip(self.scatters[name], outs):
                    self.parts[l][i] = a

        def ffn(self, l, j):
            return ffn_weights(self.q[l], j)

        def proj(self, l):
            return proj_weights(l, self.q[l][U_IN], self.q[l][U_OUT])

        def grads(self, l, units, items):
            units = list(units)
            half0, half1 = [[items[c][i] for i in units] for c in range(2)]
            recv = rs_pair(f"rs_pair_{l}{units[0]}", [half0, half1])
            for i, a in zip(units, sum_pair(f"sum_pair_{l}{units[0]}", half0, half1, recv, cflag)):
                self.mine[l][i] = a

    ex = Exchanges()
    first = [0, 1, 2]
    for i, a in zip(first, fill(ag_layer("ag_layer_0", [pieces[0][i] for i in first]), [pieces[0][i] for i in first])):
        ex.q[0][i] = a
    W = {**small, **{n: w[n] for n, _ in REPL}, "ffn": ex.ffn, "proj": ex.proj}
    loss_blk, gx, G = core_step(x[0], loss_target[0], W, comm=ex)

    ex.grads(0, first, G["items"][0])
    for i, a in zip(first, rs_chips("rs_chips_0", [ex.mine[0][i] for i in first])):
        ex.parts[0][i] = a
    lay0 = assemble_layer(0, *reduce_finish(0, ex.mine[0], ex.parts[0]))
    lay1 = assemble_layer(1, *reduce_finish(1, ex.mine[1], ex.parts[1]))
    gt = big_grads(lay0, lay1)
    g_small_pack = small_allreduce(pack_small_grads(G))
    g = {**{n: _from_t(n, gt[n]) for n in BIG}, **unpack_small(g_small_pack)}

    delta, new_m, new_v = {}, {}, {}
    for n in BIG:
        shp = wt[n].shape
        two = lambda t: t.reshape(-1, D)
        d_, m_, v_ = adamw_call("adamw_" + n, two(wt[n]), two(gt[n]), two(_to_t(n, m[n])), two(_to_t(n, v[n])))
        delta[n], new_m[n], new_v[n] = (_from_t(n, t.reshape(shp)) for t in (d_, m_, v_))
    d_, m_, v_ = adamw_call("adamw_small", pack_small(w), g_small_pack, pack_small(m), pack_small(v))
    delta.update(unpack_small(d_))
    new_m.update(unpack_small(m_))
    new_v.update(unpack_small(v_))

    loss = lax.psum(loss_blk[0, 0], ("x", "y", "c"))
    return (loss, gx[None], *[g[n] for n in short], *[delta[n] for n in short], *[new_m[n] for n in short],
            *[new_v[n] for n in short])
```

```python
import functools
import math

import numpy as np
import jax
import jax.numpy as jnp
from jax import lax
from jax.experimental import pallas as pl
from jax.experimental.pallas import tpu as pltpu

F32 = jnp.float32
BF16 = jnp.bfloat16
HI = lax.Precision.HIGHEST

D = 1024
N_META = 16
PADR = 128
ZROWS = PADR - N_META
D_FF = 2816
NSH = 4
FSH = D_FF // NSH
EPS = 1e-6
NEG = -1e30
CH = 64
CPS = 2
BLK = 128
LANE = 128
VMEM_LIMIT = 56 * 1024 * 1024
FFN_SUB = 4

E_QA, E_KA, E_VA, E_QB, E_KB, E_VB, E_ZB, E_BA, E_END = 0, 1024, 1280, 1536, 2048, 2560, 3072, 3584, 4096


def _even_in_map():
    m = np.full((E_END,), -1, np.int64)
    for h in range(8):
        m[E_QA + h * 128:E_QA + h * 128 + 64] = np.arange(h * 64, (h + 1) * 64)
    for h in range(2):
        m[E_KA + h * 128:E_KA + h * 128 + 64] = 512 + np.arange(h * 64, (h + 1) * 64)
        m[E_VA + h * 128:E_VA + h * 128 + 64] = 640 + np.arange(h * 64, (h + 1) * 64)
    m[E_QB:E_QB + 2048] = 768 + np.arange(2048)
    m[E_BA:E_BA + 8] = 2816 + np.arange(8)
    return m


def _even_out_map():
    m = np.full((1536,), -1, np.int64)
    for h in range(8):
        m[h * 128:h * 128 + 64] = np.arange(h * 64, (h + 1) * 64)
    m[1024:1536] = 512 + np.arange(512)
    return m


O_Q, O_K, O_V, O_G, O_GK, O_END = 0, 512, 1024, 2048, 3072, 3584


def _odd_in_map():
    m = np.full((O_END,), -1, np.int64)
    m[:3072] = np.arange(3072)
    m[O_GK:O_GK + 16] = 3072 + np.arange(16)
    return m


def _inverse(m, n):
    inv = np.zeros((n,), np.int64)
    for p, o in enumerate(m):
        if o >= 0:
            inv[o] = p
    return inv


def _rows(w, m):
    parts, i, n = [], 0, len(m)
    while i < n:
        j = i + 1
        if m[i] < 0:
            while j < n and m[j] < 0:
                j += 1
            parts.append(jnp.zeros((j - i,) + w.shape[1:], w.dtype))
        else:
            while j < n and m[j] == m[j - 1] + 1:
                j += 1
            parts.append(lax.slice_in_dim(w, int(m[i]), int(m[i]) + j - i, axis=0))
        i = j
    return jnp.concatenate(parts, axis=0)


def _mm(a, b, prec=HI):
    return lax.dot_general(a, b, (((1,), (0,)), ((), ())), precision=prec, preferred_element_type=F32)


def _mm_nt(a, b, prec=HI):
    return lax.dot_general(a, b, (((1,), (1,)), ((), ())), precision=prec, preferred_element_type=F32)


def _mm_tn(a, b, prec=HI):
    return lax.dot_general(a, b, (((0,), (0,)), ((), ())), precision=prec, preferred_element_type=F32)


def _bdot(a, b, dims):
    return lax.dot_general(a.astype(BF16), b.astype(BF16), (dims, ((), ())), preferred_element_type=F32)


@jax.custom_vjp
def _bmm(a, b):
    return _bdot(a, b, ((1,), (0,)))


@jax.custom_vjp
def _bmm_nt(a, b):
    return _bdot(a, b, ((1,), (1,)))


@jax.custom_vjp
def _bmm_tn(a, b):
    return _bdot(a, b, ((0,), (0,)))


_bmm.defvjp(lambda a, b: (_bmm(a, b), (a, b)), lambda r, g: (_bmm_nt(g, r[1]), _bmm_tn(r[0], g)))
_bmm_nt.defvjp(lambda a, b: (_bmm_nt(a, b), (a, b)), lambda r, g: (_bmm(g, r[1]), _bmm_tn(g, r[0])))
_bmm_tn.defvjp(lambda a, b: (_bmm_tn(a, b), (a, b)), lambda r, g: (_bmm_nt(r[1], g), _bmm(r[0], g)))


def _hi_lo(x):
    h = x.astype(BF16)
    return h, (x - h.astype(F32)).astype(BF16)


def _xdot(a, b, dims):
    ah, al = _hi_lo(a)
    bh, bl = _hi_lo(b)
    d = lambda p, q: lax.dot_general(p, q, (dims, ((), ())), preferred_element_type=F32)
    return d(ah, bh) + (d(ah, bl) + d(al, bh))


@jax.custom_vjp
def _xmm(a, b):
    return _xdot(a, b, ((1,), (0,)))


@jax.custom_vjp
def _xmm_nt(a, b):
    return _xdot(a, b, ((1,), (1,)))


@jax.custom_vjp
def _xmm_tn(a, b):
    return _xdot(a, b, ((0,), (0,)))


_xmm.defvjp(lambda a, b: (_xmm(a, b), (a, b)), lambda r, g: (_xmm_nt(g, r[1]), _xmm_tn(r[0], g)))
_xmm_nt.defvjp(lambda a, b: (_xmm_nt(a, b), (a, b)), lambda r, g: (_xmm(g, r[1]), _xmm_tn(g, r[0])))
_xmm_tn.defvjp(lambda a, b: (_xmm_tn(a, b), (a, b)), lambda r, g: (_xmm_nt(r[1], g), _xmm(r[0], g)))


def _sum01(m01, x, dims):
    h, l = _hi_lo(x)
    l2 = (x - h.astype(F32) - l.astype(F32)).astype(BF16)
    m = m01.astype(BF16)
    d = lambda q: lax.dot_general(m, q, (dims, ((), ())), preferred_element_type=F32)
    return d(h) + (d(l) + d(l2))


@jax.custom_vjp
def _cumsum_rows(x):
    n = x.shape[0]
    tri = lax.broadcasted_iota(jnp.int32, (n, n), 0) >= lax.broadcasted_iota(jnp.int32, (n, n), 1)
    return _sum01(tri, x, ((1,), (0,)))


def _cumsum_rows_b(_, g):
    n = g.shape[0]
    tri = lax.broadcasted_iota(jnp.int32, (n, n), 0) >= lax.broadcasted_iota(jnp.int32, (n, n), 1)
    return (_sum01(tri, g, ((0,), (0,))),)


_cumsum_rows.defvjp(lambda x: (_cumsum_rows(x), None), _cumsum_rows_b)


@functools.partial(jax.custom_vjp, nondiff_argnums=(1,))
def _colsum_as_rows(x, width):
    return _colsum_impl(x, width)


def _colsum_impl(x, width):
    h, l = _hi_lo(x)
    l2 = (x - h.astype(F32) - l.astype(F32)).astype(BF16)
    ones = jnp.ones((x.shape[0], width), BF16)
    d = lambda q: lax.dot_general(q, ones, (((0,), (0,)), ((), ())), preferred_element_type=F32)
    return d(h) + (d(l) + d(l2))


def _colsum_as_rows_f(x, width):
    return _colsum_impl(x, width), x.shape[0]


def _colsum_as_rows_b(width, n, g):
    return (_sum01(jnp.ones((n, width), F32), g, ((1,), (1,))),)


_colsum_as_rows.defvjp(_colsum_as_rows_f, _colsum_as_rows_b)


def _rms(x, w):
    return x * lax.rsqrt(jnp.mean(x * x, axis=-1, keepdims=True) + EPS) * w


def _sigmoid(x):
    return 1.0 / (1.0 + jnp.exp(-x))


def _silu(x):
    return x * _sigmoid(x)


def _softplus(x):
    return jnp.maximum(x, 0.0) + jnp.log(1.0 + jnp.exp(-jnp.abs(x)))


def _lane_pick(row, idx):
    lane = lax.broadcasted_iota(jnp.int32, row.shape, row.ndim - 1)
    return jnp.sum(jnp.where(lane == idx, row, 0.0), axis=-1, keepdims=True)


def _row_ids(row0, n):
    return row0 + lax.broadcasted_iota(jnp.int32, (n, 1), 0)


def _pick(m, cap):
    best = 64
    for t in range(64, min(m, cap) + 1, 64):
        if m % t == 0:
            best = t
    return best


def _cparams(sem):
    return pltpu.CompilerParams(dimension_semantics=sem, vmem_limit_bytes=VMEM_LIMIT)


def mm_nn(a, b, name, out_dtype=F32):
    M, K = a.shape
    N = b.shape[1]
    tm = _pick(M, 1408 if K <= 2048 else 704)
    tn = _pick(N, 512)

    def body(a_ref, b_ref, o_ref):
        o_ref[...] = _mm(a_ref[...], b_ref[...], None).astype(o_ref.dtype)

    return pl.pallas_call(
        body, name=name, grid=(N // tn, M // tm),
        in_specs=[pl.BlockSpec((tm, K), lambda j, i: (i, 0)), pl.BlockSpec((K, tn), lambda j, i: (0, j))],
        out_specs=pl.BlockSpec((tm, tn), lambda j, i: (i, j)),
        out_shape=jax.ShapeDtypeStruct((M, N), out_dtype),
        compiler_params=_cparams(("parallel", "parallel")),
    )(a, b)


def mm_nt(a, b, name, out_dtype=F32):
    M, K = a.shape
    N = b.shape[0]
    tm = _pick(M, 1408)
    tn = _pick(N, 512 if K > 2048 else 1024)

    def body(a_ref, b_ref, o_ref):
        o_ref[...] = _mm_nt(a_ref[...], b_ref[...], None).astype(o_ref.dtype)

    return pl.pallas_call(
        body, name=name, grid=(N // tn, M // tm),
        in_specs=[pl.BlockSpec((tm, K), lambda j, i: (i, 0)), pl.BlockSpec((tn, K), lambda j, i: (j, 0))],
        out_specs=pl.BlockSpec((tm, tn), lambda j, i: (i, j)),
        out_shape=jax.ShapeDtypeStruct((M, N), out_dtype),
        compiler_params=_cparams(("parallel", "parallel")),
    )(a, b)


def mm_tn(a, b, name):
    M, K = a.shape
    N = b.shape[1]
    tk = _pick(K, 512)
    tn = _pick(N, 512)

    def body(a_ref, b_ref, o_ref):
        o_ref[...] = _mm_tn(a_ref[...], b_ref[...], None)

    return pl.pallas_call(
        body, name=name, grid=(K // tk, N // tn),
        in_specs=[pl.BlockSpec((M, tk), lambda i, j: (0, i)), pl.BlockSpec((M, tn), lambda i, j: (0, j))],
        out_specs=pl.BlockSpec((tk, tn), lambda i, j: (i, j)),
        out_shape=jax.ShapeDtypeStruct((K, N), F32),
        compiler_params=_cparams(("parallel", "parallel")),
    )(a, b)


def _row_specs(rows, tm):
    return [pl.BlockSpec((tm, w), functools.partial(lambda i, cb: (i, cb), cb=cb)) for (_, w, cb) in rows]


def _param_specs(params):
    return [pl.BlockSpec(p.shape, functools.partial(lambda i, nd: (0,) * nd, nd=p.ndim)) for p in params]


def rowwise_fwd(name, fn, rows, params, outs, tm=None):
    M = rows[0][0].shape[0]
    tm = tm or _pick(M, 704)
    nr, npar = len(rows), len(params)

    def body(*refs):
        row0 = pl.program_id(0) * tm
        vals = [r[...].astype(F32) for r in refs[:nr]] + [p[...] for p in refs[nr:nr + npar]]
        res = fn(row0, *vals)
        for o_ref, r in zip(refs[nr + npar:], res):
            o_ref[...] = r.astype(o_ref.dtype)

    return pl.pallas_call(
        body, name=name, grid=(M // tm,),
        in_specs=_row_specs(rows, tm) + _param_specs(params),
        out_specs=[pl.BlockSpec((tm, w), lambda i: (i, 0)) for (w, _) in outs],
        out_shape=[jax.ShapeDtypeStruct((M, w), dt) for (w, dt) in outs],
        compiler_params=_cparams(("parallel",)),
    )(*[r[0] for r in rows], *params)


def rowwise_bwd(name, fn, rows, params, douts, drow_dtypes, tm=None):
    M = rows[0][0].shape[0]
    tm = tm or _pick(M, 704)
    nr, npar, nd = len(rows), len(params), len(douts)
    want = [k for k, dt in enumerate(drow_dtypes) if dt is not None]

    def body(*refs):
        i = pl.program_id(0)
        row0 = i * tm
        vals = [r[...].astype(F32) for r in refs[:nr]] + [p[...] for p in refs[nr:nr + npar]]
        cots = tuple(d[...].astype(F32) for d in refs[nr + npar:nr + npar + nd])
        _, vjp = jax.vjp(functools.partial(fn, row0), *vals)
        grads = vjp(cots)
        o_refs = refs[nr + npar + nd:]
        for o_ref, k in zip(o_refs[:len(want)], want):
            o_ref[...] = grads[k].astype(o_ref.dtype)
        for o_ref, g in zip(o_refs[len(want):], grads[nr:]):
            @pl.when(i == 0)
            def _():
                o_ref[...] = g

            @pl.when(i > 0)
            def _():
                o_ref[...] += g

    res = pl.pallas_call(
        body, name=name, grid=(M // tm,),
        in_specs=_row_specs(rows, tm) + _param_specs(params) + _row_specs(douts, tm),
        out_specs=[pl.BlockSpec((tm, rows[k][1]), lambda i: (i, 0)) for k in want] + _param_specs(params),
        out_shape=[jax.ShapeDtypeStruct((M, rows[k][1]), drow_dtypes[k]) for k in want]
        + [jax.ShapeDtypeStruct(p.shape, F32) for p in params],
        compiler_params=_cparams(("arbitrary",)),
    )(*[r[0] for r in rows], *params, *[d[0] for d in douts])
    return res[:len(want)], res[len(want):]


def _fn_prenorm(row0, h, wpre):
    return (_rms(h, wpre),)


def _fn_resnorm(scale, row0, h, f, wpost, wpre):
    h2 = h + scale * _rms(f, wpost)
    return h2, _rms(h2, wpre)


def _fn_res_last(scale, row0, h, f, wpost):
    return (h + scale * _rms(f, wpost),)


def ffn_fwd(name, xn, wg, wu, wd, side=None):
    M = xn.shape[0]
    tm = _pick(M, 704)
    s_ins, s_specs, s_shapes, s_sems = _side_parts(side)
    ns, nso = len(s_ins), len(s_shapes)

    def body(x_ref, wg_ref, wu_ref, wd_ref, *rest):
        f_ref, a_ref, b_ref = rest[ns:ns + 3]
        s = pl.program_id(1)
        _side_run(side, (M // tm) * NSH, rest[:ns], rest[ns + 3:ns + 3 + nso], rest[ns + 3 + nso:],
                  step=pl.program_id(0) * NSH + s)
        x = x_ref[...]
        a = _mm_nt(x, wg_ref[...], None)
        b = _mm_nt(x, wu_ref[...], None)
        c = _mm((_silu(a) * b).astype(BF16), wd_ref[...], None)

        @pl.when(s == 0)
        def _():
            f_ref[...] = c

        @pl.when(s > 0)
        def _():
            f_ref[...] += c

        a_ref[...] = a.astype(BF16)
        b_ref[...] = b.astype(BF16)

    wspec = wdspec = pl.BlockSpec((None, FSH, D), lambda i, s: (s, 0, 0))
    abspec = pl.BlockSpec((None, tm, FSH), lambda i, s: (s, i, 0))
    res = pl.pallas_call(
        body, name=name, grid=(M // tm, NSH),
        in_specs=[pl.BlockSpec((tm, D), lambda i, s: (i, 0)), wspec, wspec, wdspec] + s_specs,
        out_specs=[pl.BlockSpec((tm, D), lambda i, s: (i, 0)), abspec, abspec] + [pl.BlockSpec(memory_space=pl.ANY)] * nso,
        out_shape=[jax.ShapeDtypeStruct((M, D), F32), jax.ShapeDtypeStruct((NSH, M, FSH), BF16),
                   jax.ShapeDtypeStruct((NSH, M, FSH), BF16)] + s_shapes,
        scratch_shapes=s_sems,
        compiler_params=_cparams(("arbitrary", "arbitrary")),
    )(xn, wg, wu, wd, *s_ins)
    return res[:3], res[3:]


def ffn_bwd_x(name, df, a, b, wg, wu, wd, side=None):
    M = df.shape[0]
    tm = _pick(M, 704)
    ts = tm // FFN_SUB
    s_ins, s_specs, s_shapes, s_sems = _side_parts(side)
    ns, nso = len(s_ins), len(s_shapes)

    def body(df_ref, a_ref, b_ref, wg_ref, wu_ref, wd_ref, *rest):
        dx_ref, da_ref, db_ref, hm_ref = rest[ns:ns + 4]
        _side_run(side, (M // tm) * NSH, rest[:ns], rest[ns + 4:ns + 4 + nso], rest[ns + 4 + nso:],
                  step=pl.program_id(0) * NSH + pl.program_id(1))

        @pl.when(pl.program_id(1) == 0)
        def _():
            dx_ref[...] = jnp.zeros_like(dx_ref)

        for r in range(FFN_SUB):
            rows = pl.ds(r * ts, ts)
            a_ = a_ref[rows, :].astype(F32)
            b_ = b_ref[rows, :].astype(F32)
            dh = _mm_nt(df_ref[rows, :], wd_ref[...], None)
            sig = _sigmoid(a_)
            sil = a_ * sig
            da = (dh * b_ * (sig * (1.0 + a_ * (1.0 - sig)))).astype(BF16)
            db = (dh * sil).astype(BF16)
            dx_ref[rows, :] += _mm(da, wg_ref[...], None) + _mm(db, wu_ref[...], None)
            da_ref[rows, :] = da
            db_ref[rows, :] = db
            hm_ref[rows, :] = (sil * b_).astype(BF16)

    wspec = wdspec = pl.BlockSpec((None, FSH, D), lambda i, s: (s, 0, 0))
    abspec = pl.BlockSpec((None, tm, FSH), lambda i, s: (s, i, 0))
    ab = jax.ShapeDtypeStruct((NSH, M, FSH), BF16)
    res = pl.pallas_call(
        body, name=name, grid=(M // tm, NSH),
        in_specs=[pl.BlockSpec((tm, D), lambda i, s: (i, 0)), abspec, abspec, wspec, wspec, wdspec] + s_specs,
        out_specs=[pl.BlockSpec((tm, D), lambda i, s: (i, 0)), abspec, abspec, abspec] + [pl.BlockSpec(memory_space=pl.ANY)] * nso,
        out_shape=[jax.ShapeDtypeStruct((M, D), F32), ab, ab, ab] + s_shapes,
        scratch_shapes=s_sems,
        compiler_params=_cparams(("arbitrary", "arbitrary")),
    )(df, a, b, wg, wu, wd, *s_ins)
    return res[:4], res[4:]


def ffn_bwd_w(name, xn, df, da, db, hm):
    M = xn.shape[0]
    tm = _pick(M, 704)
    nt = M // tm

    def body(x_ref, df_ref, da_ref, db_ref, hm_ref, dwg_ref, dwu_ref, dwd_ref, ag, au, ad):
        i = pl.program_id(1)
        x = x_ref[...]
        g = _mm_tn(da_ref[...], x, None)
        u = _mm_tn(db_ref[...], x, None)
        d = _mm_tn(hm_ref[...], df_ref[...], None)

        @pl.when(i == 0)
        def _():
            ag[...] = g
            au[...] = u
            ad[...] = d

        @pl.when(i > 0)
        def _():
            ag[...] += g
            au[...] += u
            ad[...] += d

        @pl.when(i == nt - 1)
        def _():
            dwg_ref[...] = ag[...].astype(BF16)
            dwu_ref[...] = au[...].astype(BF16)
            dwd_ref[...] = ad[...].astype(BF16)

    xspec = pl.BlockSpec((tm, D), lambda s, i: (i, 0))
    abspec = pl.BlockSpec((None, tm, FSH), lambda s, i: (s, i, 0))
    return pl.pallas_call(
        body, name=name, grid=(NSH, nt),
        in_specs=[xspec, xspec, abspec, abspec, abspec],
        out_specs=[pl.BlockSpec((None, FSH, D), lambda s, i: (s, 0, 0))] * 3,
        out_shape=[jax.ShapeDtypeStruct((NSH, FSH, D), BF16)] * 3,
        scratch_shapes=[pltpu.VMEM((FSH, D), F32)] * 3,
        compiler_params=_cparams(("parallel", "arbitrary")),
    )(xn, df, da, db, hm)


def _t5_bucket_np(rel):
    n = np.maximum(rel, 0)
    n_f = np.maximum(n, 1).astype(np.float32)
    large = 16 + (np.log(n_f / np.float32(16)) / np.float32(math.log(8.0)) * np.float32(16)).astype(np.int32)
    large = np.minimum(large, 31)
    return np.where(n < 16, n, large).astype(np.int32)


def _swa_bucket_ids():
    qi = np.arange(BLK)[:, None]
    kj = np.arange(BLK)[None, :]
    out = np.full((3, BLK, 3 * BLK), -1, np.int32)
    for v in range(3):
        pos_q = v * BLK + qi - ZROWS
        rel_m = pos_q - (kj - ZROWS)
        ok_m = (kj >= ZROWS) & (rel_m >= 0) & (pos_q >= 0)
        out[v, :, 0:BLK] = np.where(ok_m, _t5_bucket_np(rel_m), -1)
        pos_kp = (v - 1) * BLK + kj - ZROWS
        rel_p = BLK + qi - kj
        ok_p = (pos_kp >= N_META) & (rel_p >= 0) & (rel_p < BLK) & np.full_like(ok_m, v >= 1)
        out[v, :, BLK:2 * BLK] = np.where(ok_p, _t5_bucket_np(rel_p), -1)
        pos_kc = v * BLK + kj - ZROWS
        rel_c = qi - kj
        ok_c = (pos_kc >= N_META) & (rel_c >= 0) & (rel_c < BLK)
        out[v, :, 2 * BLK:] = np.where(ok_c, _t5_bucket_np(rel_c), -1)
    return out


def swa_bias_fwd(table, ids):
    def body(t_ref, id_ref, o_ref):
        for v in range(3):
            for h in range(8):
                o_ref[v, h] = jnp.where(id_ref[v] < 0, NEG, 0.0)

            def step(b, carry):
                hit = id_ref[v] == b
                for h in range(8):
                    o_ref[v, h] += jnp.where(hit, t_ref[b, h], 0.0)
                return carry

            lax.fori_loop(0, 32, step, 0)

    return pl.pallas_call(
        body, name="swa_bias_fwd",
        in_specs=[pl.BlockSpec(memory_space=pltpu.SMEM), pl.BlockSpec(memory_space=pltpu.VMEM)],
        out_specs=pl.BlockSpec(memory_space=pltpu.VMEM),
        out_shape=jax.ShapeDtypeStruct((3, 8, BLK, 3 * BLK), F32),
        compiler_params=pltpu.CompilerParams(vmem_limit_bytes=VMEM_LIMIT),
    )(table, ids)


def swa_bias_bwd(dbias, ids):
    def body(d_ref, id_ref, o_ref):
        r = lax.broadcasted_iota(jnp.int32, (32, LANE), 0)
        c = lax.broadcasted_iota(jnp.int32, (32, LANE), 1)

        def step(b, acc):
            for v in range(3):
                hit = id_ref[v] == b
                for h in range(8):
                    m = jnp.where(hit, d_ref[v, h], 0.0)
                    s = jnp.sum(jnp.sum(m, axis=1, keepdims=True), axis=0, keepdims=True)
                    acc = acc + jnp.where((r == b) & (c == h), s, 0.0)
            return acc

        o_ref[...] = lax.fori_loop(0, 32, step, jnp.zeros((32, LANE), F32))

    return pl.pallas_call(
        body, name="swa_bias_bwd",
        in_specs=[pl.BlockSpec(memory_space=pltpu.VMEM), pl.BlockSpec(memory_space=pltpu.VMEM)],
        out_specs=pl.BlockSpec(memory_space=pltpu.VMEM),
        out_shape=jax.ShapeDtypeStruct((32, LANE), F32),
        compiler_params=pltpu.CompilerParams(vmem_limit_bytes=VMEM_LIMIT),
    )(dbias, ids)


def _swa_block(q, k3, v3, bias, sinks):
    heads = range(8)
    kh = [k3[:, (h // 4) * 128:(h // 4 + 1) * 128] for h in heads]
    vh = [v3[:, (h // 4) * 128:(h // 4 + 1) * 128] for h in heads]
    s = [_bmm_nt(q[:, h * 128:(h + 1) * 128], kh[h]) * 0.125 + bias[h] for h in heads]
    sink = [_lane_pick(sinks, h) for h in heads]
    m = [lax.stop_gradient(jnp.maximum(jnp.max(s[h], axis=-1, keepdims=True), sink[h])) for h in heads]
    e = [jnp.exp(s[h] - m[h]) for h in heads]
    p = [e[h] / (jnp.sum(e[h], axis=-1, keepdims=True) + jnp.exp(sink[h] - m[h])) for h in heads]
    return jnp.concatenate([_bmm(p[h], vh[h]) for h in heads], axis=1)


def _swa_in_specs():
    qs = pl.BlockSpec((BLK, 1024), lambda n: (n, E_QA // 1024))
    ks = [pl.BlockSpec((BLK, 256), lambda n: (0, E_KA // 256)),
          pl.BlockSpec((BLK, 256), lambda n: (jnp.maximum(n - 1, 0), E_KA // 256)),
          pl.BlockSpec((BLK, 256), lambda n: (n, E_KA // 256))]
    vs = [pl.BlockSpec((BLK, 256), lambda n: (0, E_VA // 256)),
          pl.BlockSpec((BLK, 256), lambda n: (jnp.maximum(n - 1, 0), E_VA // 256)),
          pl.BlockSpec((BLK, 256), lambda n: (n, E_VA // 256))]
    bs = pl.BlockSpec((None, 8, BLK, 3 * BLK), lambda n: (jnp.minimum(n, 2), 0, 0, 0))
    ss = pl.BlockSpec((1, LANE), lambda n: (0, 0))
    return [qs] + ks + vs + [bs, ss]


def swa_fwd(proj, bias, sinks, side=None):
    M = proj.shape[0]
    s_ins, s_specs, s_shapes, s_sems = _side_parts(side)
    ns, nso = len(s_ins), len(s_shapes)

    def body(q_ref, k0, k1, k2, v0, v1, v2, b_ref, s_ref, *rest):
        o_ref = rest[ns]
        _side_run(side, M // BLK, rest[:ns], rest[ns + 1:ns + 1 + nso], rest[ns + 1 + nso:])
        k3 = jnp.concatenate([k0[...], k1[...], k2[...]], axis=0)
        v3 = jnp.concatenate([v0[...], v1[...], v2[...]], axis=0)
        o_ref[...] = _swa_block(q_ref[...], k3, v3, b_ref[...], s_ref[...]).astype(o_ref.dtype)

    res = pl.pallas_call(
        body, name="swa_fwd", grid=(M // BLK,),
        in_specs=_swa_in_specs() + s_specs,
        out_specs=[pl.BlockSpec((BLK, 1024), lambda n: (n, 0))] + [pl.BlockSpec(memory_space=pl.ANY)] * nso,
        out_shape=[jax.ShapeDtypeStruct((M, 1024), BF16)] + s_shapes,
        scratch_shapes=s_sems,
        compiler_params=_cparams(("arbitrary",)),
    )(proj, proj, proj, proj, proj, proj, proj, bias, sinks, *s_ins)
    return res[0], res[1:]


def swa_bwd(proj, bias, sinks, do, side=None):
    M = proj.shape[0]
    s_ins, s_specs, s_shapes, s_sems = _side_parts(side)
    ns, nso = len(s_ins), len(s_shapes)

    def body(q_ref, k0, k1, k2, v0, v1, v2, b_ref, s_ref, do_ref, *rest):
        dq_ref, dk_ref, dv_ref, db_ref, ds_ref = rest[ns:ns + 5]
        _side_run(side, M // BLK, rest[:ns], rest[ns + 5:ns + 5 + nso], rest[ns + 5 + nso:])
        n = pl.program_id(0)

        @pl.when(n == 0)
        def _():
            dk_ref[...] = jnp.zeros_like(dk_ref)
            dv_ref[...] = jnp.zeros_like(dv_ref)
            ds_ref[...] = jnp.zeros_like(ds_ref)

        k3 = jnp.concatenate([k0[...], k1[...], k2[...]], axis=0)
        v3 = jnp.concatenate([v0[...], v1[...], v2[...]], axis=0)
        _, vjp = jax.vjp(_swa_block, q_ref[...], k3, v3, b_ref[...], s_ref[...])
        dq, dk3, dv3, dbias, dsink = vjp(do_ref[...].astype(F32))
        dq_ref[...] = dq
        prev = pl.multiple_of(jnp.maximum(n - 1, 0) * BLK, BLK)
        cur = pl.multiple_of(n * BLK, BLK)
        dk_ref[pl.ds(0, BLK), :] += dk3[0:BLK]
        dv_ref[pl.ds(0, BLK), :] += dv3[0:BLK]
        dk_ref[pl.ds(prev, BLK), :] += dk3[BLK:2 * BLK]
        dv_ref[pl.ds(prev, BLK), :] += dv3[BLK:2 * BLK]
        dk_ref[pl.ds(cur, BLK), :] += dk3[2 * BLK:]
        dv_ref[pl.ds(cur, BLK), :] += dv3[2 * BLK:]
        ds_ref[...] += dsink

        @pl.when(n <= 2)
        def _():
            db_ref[...] = dbias

        @pl.when(n > 2)
        def _():
            db_ref[...] += dbias

    res = pl.pallas_call(
        body, name="swa_bwd", grid=(M // BLK,),
        in_specs=_swa_in_specs() + [pl.BlockSpec((BLK, 1024), lambda n: (n, 0))] + s_specs,
        out_specs=[pl.BlockSpec((BLK, 1024), lambda n: (n, 0)),
                   pl.BlockSpec((M, 256), lambda n: (0, 0)), pl.BlockSpec((M, 256), lambda n: (0, 0)),
                   pl.BlockSpec((None, 8, BLK, 3 * BLK), lambda n: (jnp.minimum(n, 2), 0, 0, 0)),
                   pl.BlockSpec((1, LANE), lambda n: (0, 0))] + [pl.BlockSpec(memory_space=pl.ANY)] * nso,
        out_shape=[jax.ShapeDtypeStruct((M, 1024), F32), jax.ShapeDtypeStruct((M, 256), F32),
                   jax.ShapeDtypeStruct((M, 256), F32), jax.ShapeDtypeStruct((3, 8, BLK, 3 * BLK), F32),
                   jax.ShapeDtypeStruct((1, LANE), F32)] + s_shapes,
        scratch_shapes=s_sems,
        compiler_params=_cparams(("arbitrary",)),
    )(proj, proj, proj, proj, proj, proj, proj, bias, sinks, do, *s_ins)
    return res[:5], res[5:]


def _shift_rows_impl(x, k):
    n = x.shape[0]
    rolled = pltpu.roll(x, k, 0)
    return jnp.where(_row_ids(0, n) >= k, rolled, 0.0)


def _unshift_rows_impl(g, k):
    n = g.shape[0]
    rolled = pltpu.roll(g, n - k, 0)
    return jnp.where(_row_ids(0, n) < n - k, rolled, 0.0)


@functools.partial(jax.custom_vjp, nondiff_argnums=(1,))
def _shift_rows(x, k):
    return _shift_rows_impl(x, k)


def _shift_rows_f(x, k):
    return _shift_rows_impl(x, k), None


def _shift_rows_b(k, _, g):
    return (_unshift_rows_impl(g, k),)


_shift_rows.defvjp(_shift_rows_f, _shift_rows_b)


def _conv_silu(x, w):
    rid = lax.broadcasted_iota(jnp.int32, w.shape, 0)
    y = x * jnp.sum(jnp.where(rid == 3, w, 0.0), axis=0, keepdims=True)
    for k in range(1, 4):
        y = y + _shift_rows(x, k) * jnp.sum(jnp.where(rid == 3 - k, w, 0.0), axis=0, keepdims=True)
    y = jnp.where(_row_ids(0, x.shape[0]) >= ZROWS, y, 0.0)
    return _silu(y)


def conv_fwd(proj, conv_w):
    M = proj.shape[0]
    nb = conv_w.shape[1] // LANE

    def body(x_ref, w_ref, o_ref):
        o_ref[...] = _conv_silu(x_ref[...], w_ref[...])

    return pl.pallas_call(
        body, name="conv_fwd", grid=(nb,),
        in_specs=[pl.BlockSpec((M, LANE), lambda c: (0, E_QB // LANE + c)), pl.BlockSpec((4, LANE), lambda c: (0, c))],
        out_specs=pl.BlockSpec((M, LANE), lambda c: (0, c)),
        out_shape=jax.ShapeDtypeStruct((M, conv_w.shape[1]), F32),
        compiler_params=_cparams(("parallel",)),
    )(proj, conv_w)


def conv_bwd(proj, conv_w, dy):
    M = proj.shape[0]
    nb = conv_w.shape[1] // LANE

    def body(x_ref, w_ref, dy_ref, dx_ref, dw_ref):
        _, vjp = jax.vjp(_conv_silu, x_ref[...], w_ref[...])
        dx, dw = vjp(dy_ref[...])
        dx_ref[...] = dx
        dw_ref[...] = dw

    return pl.pallas_call(
        body, name="conv_bwd", grid=(nb,),
        in_specs=[pl.BlockSpec((M, LANE), lambda c: (0, E_QB // LANE + c)), pl.BlockSpec((4, LANE), lambda c: (0, c)),
                  pl.BlockSpec((M, LANE), lambda c: (0, c))],
        out_specs=[pl.BlockSpec((M, LANE), lambda c: (0, c)), pl.BlockSpec((4, LANE), lambda c: (0, c))],
        out_shape=[jax.ShapeDtypeStruct((M, conv_w.shape[1]), F32), jax.ShapeDtypeStruct(conv_w.shape, F32)],
        compiler_params=_cparams(("parallel",)),
    )(proj, conv_w, dy)


def _fn_dn_prep(row0, yq, yk, ba, dnp):
    tm = yq.shape[0]
    real = _row_ids(row0, tm) >= ZROWS
    qs, ks, gs, bs = [], [], [], []
    for h in range(4):
        q = yq[:, h * 128:(h + 1) * 128]
        k = yk[:, h * 128:(h + 1) * 128]
        qs.append(q * lax.rsqrt(jnp.sum(q * q, axis=-1, keepdims=True) + 1e-6) * (128.0 ** -0.5))
        ks.append(k * lax.rsqrt(jnp.sum(k * k, axis=-1, keepdims=True) + 1e-6))
        beta = _sigmoid(_lane_pick(ba, h))
        g = -jnp.exp(_lane_pick(dnp, h)) * _softplus(_lane_pick(ba, 4 + h) + _lane_pick(dnp, 4 + h))
        g = jnp.where(real, g, 0.0)
        gs.append(jnp.broadcast_to(g, (tm, 128)))
        bs.append(jnp.broadcast_to(beta, (tm, 128)))
    cat = lambda xs: jnp.concatenate(xs, axis=1)
    return cat(qs), cat(ks), cat(gs), cat(bs)


def _zip(f, *lists):
    return [f(*args) for args in zip(*lists)]


def _unit_lower_inv_impl(a):
    n = a[0].shape[0]
    eye = (lax.broadcasted_iota(jnp.int32, (n, n), 0) == lax.broadcasted_iota(jnp.int32, (n, n), 1)).astype(F32)
    nn = ((1,), (0,))
    p = [-x for x in a]
    t = [eye + x for x in p]
    for _ in range(int(math.log2(n)) - 1):
        p = _zip(lambda x: _xdot(x, x, nn), p)
        t = _zip(lambda x, y: x + _xdot(x, y, nn), t, p)
    return t


@jax.custom_vjp
def _unit_lower_inv(a):
    return _unit_lower_inv_impl(a)


def _unit_lower_inv_f(a):
    t = _unit_lower_inv_impl(a)
    return t, t


def _unit_lower_inv_b(t, g):
    tg = _zip(lambda x, y: _xdot(x, y, ((0,), (0,))), t, g)
    return (_zip(lambda x, y: -_xdot(x, y, ((1,), (1,))), tg, t),)


_unit_lower_inv.defvjp(_unit_lower_inv_f, _unit_lower_inv_b)


@jax.custom_vjp
def _known_inv(a, t):
    return t


_known_inv.defvjp(lambda a, t: (t, t), lambda t, g: (_unit_lower_inv_b(t, g)[0], [jnp.zeros_like(x) for x in t]))


def _dn_block(q, k, v, gb, bb, S, t_kept=None):
    nh = len(S)
    r = lax.broadcasted_iota(jnp.int32, (CH, CH), 0)
    c = lax.broadcasted_iota(jnp.int32, (CH, CH), 1)
    tri_incl = r >= c
    gcb = _zip(_cumsum_rows, gb)
    gamma = _zip(lambda x: jnp.where(tri_incl, jnp.exp(jnp.where(tri_incl, x[:, :CH] - x[:, :CH].T, 0.0)), 0.0), gcb)
    kb = _zip(jnp.multiply, k, bb)
    vb = _zip(jnp.multiply, v, bb)
    a = _zip(lambda m, g: jnp.where(r > c, m * g, 0.0), _zip(_bmm_nt, kb, k), gamma)
    t = _unit_lower_inv(a) if t_kept is None else _known_inv(a, list(t_kept))
    eg = _zip(jnp.exp, gcb)
    u = _zip(_xmm, t, vb)
    w = _zip(_xmm, t, _zip(jnp.multiply, kb, eg))
    attn = _zip(lambda m, g: m * g, _zip(_bmm_nt, q, k), gamma)
    gtot = _zip(lambda x: jnp.sum(x, axis=0, keepdims=True), gb)
    k_dec = _zip(lambda x, gt, gc: x * jnp.exp(gt - gc), k, gtot, gcb)
    q_dec = _zip(jnp.multiply, q, eg)
    S = list(S)
    o, starts = [], []
    for i0 in range(0, len(q), nh):
        idx = range(i0, i0 + nh)
        starts.append(list(S))
        v_new = [u[i] - m for i, m in zip(idx, [_bmm(w[i], S[h]) for h, i in enumerate(idx)])]
        oq = [_bmm(q_dec[i], S[h]) for h, i in enumerate(idx)]
        oa = [_bmm(attn[i], vn) for i, vn in zip(idx, v_new)]
        kv = [_bmm_tn(k_dec[i], vn) for i, vn in zip(idx, v_new)]
        o += _zip(jnp.add, oq, oa)
        S = [S[h] * jnp.exp(jnp.broadcast_to(gtot[i], S[h].shape)) + kv[h] for h, i in enumerate(idx)]
    return o, S, starts, t


def _gla_block(q, k, v, glog, S):
    nh = len(S)
    tri = lax.broadcasted_iota(jnp.int32, (CH, CH), 0) >= lax.broadcasted_iota(jnp.int32, (CH, CH), 1)
    bcum = _zip(_cumsum_rows, glog)
    q_dec = _zip(lambda x, b: x * (128.0 ** -0.5) * jnp.exp(b), q, bcum)
    attn = _zip(lambda m: jnp.where(tri, m, 0.0), _zip(_bmm_nt, q_dec, _zip(lambda x, b: x * jnp.exp(-b), k, bcum)))
    o_in = _zip(_bmm, attn, v)
    k_dec = _zip(lambda x, g, b: x * jnp.exp(jnp.sum(g, axis=0, keepdims=True) - b), k, glog, bcum)
    decay = _zip(lambda g, x: jnp.exp(_colsum_as_rows(g, x.shape[1])), glog, v)
    kv = _zip(_bmm_tn, k_dec, v)
    S = list(S)
    o, starts = [], []
    for i0 in range(0, len(q), nh):
        idx = range(i0, i0 + nh)
        starts.append(list(S))
        o += [o_in[i] + m for i, m in zip(idx, [_bmm(q_dec[i], S[h]) for h, i in enumerate(idx)])]
        S = [S[h] * decay[i] + kv[i] for h, i in enumerate(idx)]
    return o, S, starts


class Side:
    def __init__(self, ins, out_shapes, nsem, events):
        self.ins, self.out_shapes, self.nsem, self.events = list(ins), list(out_shapes), nsem, events


def _side_parts(side):
    if side is None:
        return [], [], [], []
    anyspec = pl.BlockSpec(memory_space=pl.ANY)
    return (side.ins, [anyspec] * len(side.ins), side.out_shapes,
            [pltpu.SemaphoreType.DMA((side.nsem,)), pltpu.SemaphoreType.DMA((side.nsem,))])


def _side_run(side, n_steps, in_refs, out_refs, sems, step=None):
    if side is None:
        return
    step = pl.program_id(0) if step is None else step
    for at, fn in side.events(n_steps, in_refs, out_refs, *sems):
        pl.when(step == at)(fn)


def chunk_fwd(name, chunk_fn, ins, dv, side=None, keep=()):
    M = ins[0][0].shape[0]
    NC = M // CH
    N = NC // CPS
    ni, nk = len(ins), len(keep)
    ws = [w for (_, w, _) in ins]
    s_ins, s_specs, s_shapes, s_sems = _side_parts(side)
    ns, nso = len(s_ins), len(s_shapes)

    def body(*refs):
        o0 = ni + ns
        o_ref, sall_ref = refs[o0:o0 + 2]
        k_refs = refs[o0 + 2:o0 + 2 + nk]
        s_ref = refs[o0 + 2 + nk + nso]
        _side_run(side, N, refs[ni:o0], refs[o0 + 2 + nk:o0 + 2 + nk + nso], refs[o0 + 3 + nk + nso:])

        @pl.when(pl.program_id(0) == 0)
        def _():
            s_ref[...] = jnp.zeros_like(s_ref)

        problems = [(cc, h) for cc in range(CPS) for h in range(4)]
        lists = [[r[cc * CH:(cc + 1) * CH, h * w:(h + 1) * w] for cc, h in problems] for r, w in zip(refs[:ni], ws)]
        o, s_new, starts, *kept = chunk_fn(*lists, [s_ref[h] for h in range(4)])
        for b, (cc, h) in enumerate(problems):
            o_ref[cc * CH:(cc + 1) * CH, h * dv:(h + 1) * dv] = o[b]
            sall_ref[h, cc] = starts[cc][h]
            for k_ref, vals in zip(k_refs, kept):
                k_ref[h, cc] = vals[b]
        for h in range(4):
            s_ref[h] = s_new[h]

    per_chunk = lambda r, c: pl.BlockSpec((4, CPS, r, c), lambda n: (0, n, 0, 0))
    specs = [pl.BlockSpec((CPS * CH, 4 * w), functools.partial(lambda n, cb: (n, cb), cb=cb // 4)) for (_, w, cb) in ins]
    res = pl.pallas_call(
        body, name=name, grid=(N,),
        in_specs=specs + s_specs,
        out_specs=[pl.BlockSpec((CPS * CH, 4 * dv), lambda n: (n, 0)), per_chunk(128, dv)] + [per_chunk(r, c) for r, c in keep]
        + [pl.BlockSpec(memory_space=pl.ANY)] * nso,
        out_shape=[jax.ShapeDtypeStruct((M, 4 * dv), F32), jax.ShapeDtypeStruct((4, NC, 128, dv), F32)]
        + [jax.ShapeDtypeStruct((4, NC, r, c), F32) for r, c in keep] + s_shapes,
        scratch_shapes=[pltpu.VMEM((4, 128, dv), F32)] + s_sems,
        compiler_params=_cparams(("arbitrary",)),
    )(*[a for (a, _, _) in ins], *s_ins)
    return res[0], res[1], res[2:2 + nk], res[2 + nk:]


def chunk_bwd(name, chunk_fn, ins, dv, s_all, do, side=None, kept=()):
    M = ins[0][0].shape[0]
    N = M // CH // CPS
    ni, nk = len(ins), len(kept)
    ws = [w for (_, w, _) in ins]
    s_ins, s_specs, s_shapes, s_sems = _side_parts(side)
    ns, nso = len(s_ins), len(s_shapes)

    def body(*refs):
        sall_ref, do_ref = refs[ni:ni + 2]
        k_refs = refs[ni + 2:ni + 2 + nk]
        o0 = ni + 2 + nk + ns
        d_refs = refs[o0:o0 + ni]
        ds_ref = refs[o0 + ni + nso]
        _side_run(side, N, refs[ni + 2 + nk:o0], refs[o0 + ni:o0 + ni + nso], refs[o0 + ni + nso + 1:])

        @pl.when(pl.program_id(0) == 0)
        def _():
            ds_ref[...] = jnp.zeros_like(ds_ref)

        problems = [(cc, h) for cc in range(CPS) for h in range(4)]
        lists = [[r[cc * CH:(cc + 1) * CH, h * w:(h + 1) * w] for cc, h in problems] for r, w in zip(refs[:ni], ws)]
        kept_lists = [[k_ref[h, cc] for cc, h in problems] for k_ref in k_refs]
        _, vjp = jax.vjp(lambda *a: tuple(chunk_fn(*a)[:2]), *lists, [sall_ref[h, 0] for h in range(4)], *kept_lists)
        grads = vjp(([do_ref[cc * CH:(cc + 1) * CH, h * dv:(h + 1) * dv] for cc, h in problems],
                     [ds_ref[h] for h in range(4)]))
        for d_ref, w, g in zip(d_refs, ws, grads[:ni]):
            for b, (cc, h) in enumerate(problems):
                d_ref[cc * CH:(cc + 1) * CH, h * w:(h + 1) * w] = g[b]
        for h in range(4):
            ds_ref[h] = grads[ni][h]

    rev = lambda n: N - 1 - n
    per_chunk = lambda r, c: pl.BlockSpec((4, CPS, r, c), lambda n: (0, rev(n), 0, 0))
    specs = [pl.BlockSpec((CPS * CH, 4 * w), functools.partial(lambda n, cb: (rev(n), cb), cb=cb // 4)) for (_, w, cb) in ins]
    res = pl.pallas_call(
        body, name=name, grid=(N,),
        in_specs=specs + [per_chunk(128, dv), pl.BlockSpec((CPS * CH, 4 * dv), lambda n: (rev(n), 0))]
        + [per_chunk(*a.shape[2:]) for a in kept] + s_specs,
        out_specs=[pl.BlockSpec((CPS * CH, 4 * w), lambda n: (rev(n), 0)) for w in ws] + [pl.BlockSpec(memory_space=pl.ANY)] * nso,
        out_shape=[jax.ShapeDtypeStruct((M, 4 * w), F32) for w in ws] + s_shapes,
        scratch_shapes=[pltpu.VMEM((4, 128, dv), F32)] + s_sems,
        compiler_params=_cparams(("arbitrary",)),
    )(*[a for (a, _, _) in ins], s_all, do, *kept, *s_ins)
    return res[:ni], res[ni:]


def _fn_gate_out(hd, row0, o, z, w):
    outs = []
    for h in range(4):
        outs.append(_rms(o[:, h * hd:(h + 1) * hd], w) * _silu(z[:, h * hd:(h + 1) * hd]))
    return (jnp.concatenate(outs, axis=1),)


def _fn_gla_prep(row0, gk, wgu, bg):
    x = _mm(gk, wgu) + bg
    ls = jnp.minimum(x, 0.0) - jnp.log(1.0 + jnp.exp(-jnp.abs(x)))
    return (jnp.where(_row_ids(row0, gk.shape[0]) >= ZROWS, ls / 16.0, 0.0),)


def loss_call(y, tgt):
    M = y.shape[0]
    tm = _pick(M, 512)

    def body(y_ref, t_ref, l_ref, dy_ref):
        i = pl.program_id(0)
        e = y_ref[...] - t_ref[...]
        dy_ref[...] = e * (1.0 / D)
        part = 0.5 * jnp.sum(jnp.sum(e * e, axis=1, keepdims=True) * (1.0 / D), axis=0, keepdims=True)
        part = jnp.broadcast_to(part, (8, LANE))

        @pl.when(i == 0)
        def _():
            l_ref[...] = part

        @pl.when(i > 0)
        def _():
            l_ref[...] += part

    return pl.pallas_call(
        body, name="loss", grid=(M // tm,),
        in_specs=[pl.BlockSpec((tm, D), lambda i: (i, 0))] * 2,
        out_specs=[pl.BlockSpec((8, LANE), lambda i: (0, 0)), pl.BlockSpec((tm, D), lambda i: (i, 0))],
        out_shape=[jax.ShapeDtypeStruct((8, LANE), F32), jax.ShapeDtypeStruct((M, D), F32)],
        compiler_params=_cparams(("arbitrary",)),
    )(y, tgt)


def _bf(x):
    return x.astype(BF16)


def core_step(x, tgt, W, comm=None):
    S = x.shape[0]
    M = S + PADR
    ids = jnp.asarray(_swa_bucket_ids())
    h0 = jnp.concatenate([jnp.zeros((ZROWS, D), F32), W["meta"], x], axis=0)
    nw = W["norm"]
    nrow = lambda l, k: nw[l, k][None, :]
    ffw = W["ffn"]
    projs = {}

    def projw(l):
        if l not in projs:
            projs[l] = W["proj"](l)
        return projs[l]

    site = (lambda name: comm.side(name)) if comm else (lambda name: None)
    landed = (lambda name, outs: comm.done(name, outs)) if comm else (lambda name, outs: None)
    sinks = jnp.pad(W["sinks"], ((0, 0), (0, LANE - 8)))
    dnp = jnp.pad(jnp.concatenate([W["a_log"], W["dt_bias"]], axis=1), ((0, 0), (0, LANE - 8)))
    wgu = jnp.pad(W["gate_up"], ((0, LANE - 16), (0, 0)))
    bg = W["b_gate"]
    full = lambda a: (a, a.shape[1], 0)

    saved = []
    h = h0
    (hn,) = rowwise_fwd("prenorm_0", _fn_prenorm, [full(h)], [nrow(0, 0)], [(D, BF16)])
    bias = swa_bias_fwd(W["rel"], ids)
    for l in range(2):
        st = {"h_a": h, "hn_a": hn}
        (f1, a1, b1), got = ffn_fwd(f"ffn_fwd_{l}0", hn, *ffw(l, 0), side=site(f"ffn_fwd_{l}0"))
        landed(f"ffn_fwd_{l}0", got)
        h, hn = rowwise_fwd(f"resnorm_{l}1", functools.partial(_fn_resnorm, 0.5), [full(h), full(f1)],
                            [nrow(l, 1), nrow(l, 2)], [(D, F32), (D, BF16)])
        st.update(f1=f1, a1=a1, b1=b1, h_b=h, hn_b=hn)
        if l == 0:
            proj = mm_nt(hn, projw(0)["w_in"], "e_proj")
            o_a, got = swa_fwd(proj, bias, sinks, side=site("swa_fwd"))
            landed("swa_fwd", got)
            y = conv_fwd(proj, W["conv"])
            qn, kn, gb, bb = rowwise_fwd(
                "dn_prep", _fn_dn_prep, [(y, 512, 0), (y, 512, 1), (proj, LANE, E_BA // LANE)], [dnp], [(512, F32)] * 4)
            ins = [(qn, 128, 0), (kn, 128, 0), (y, 128, 8), (gb, 128, 0), (bb, 128, 0)]
            o_dn, s_all, (t_inv,), got = chunk_fwd("dn_fwd", _dn_block, ins, 128, keep=[(CH, CH)],
                                                   side=site("dn_fwd"))
            landed("dn_fwd", got)
            (o_b,) = rowwise_fwd("dn_out", functools.partial(_fn_gate_out, 128),
                                 [full(o_dn), (proj, 512, E_ZB // 512)], [W["dn_norm"]], [(512, BF16)])
            omix = jnp.concatenate([o_a, o_b], axis=1)
            mix = mm_nn(omix, projw(0)["w_out"], "e_mix")
            st.update(proj=proj, y=y, qn=qn, kn=kn, gb=gb, bb=bb, o_dn=o_dn, s_all=s_all, t_inv=t_inv, omix=omix)
        else:
            proj = mm_nt(hn, projw(1)["w_in"], "o_proj")
            (glog,) = rowwise_fwd("gla_prep", _fn_gla_prep, [(proj, LANE, O_GK // LANE)], [wgu, bg], [(512, F32)])
            ins = [(proj, 128, O_Q // 128), (proj, 128, O_K // 128), (proj, 256, O_V // 256), (glog, 128, 0)]
            o_g, s_all, _, _ = chunk_fwd("gla_fwd", _gla_block, ins, 256)
            (omix,) = rowwise_fwd("gla_out", functools.partial(_fn_gate_out, 256),
                                  [full(o_g), (proj, 1024, O_G // 1024)], [W["gla_norm"]], [(1024, BF16)])
            mix = mm_nn(omix, projw(1)["w_out"], "o_mix")
            st.update(proj=proj, glog=glog, o_g=o_g, s_all=s_all, omix=omix)
        h, hn = rowwise_fwd(f"resnorm_{l}3", functools.partial(_fn_resnorm, 1.0), [full(h), full(mix)],
                            [nrow(l, 3), nrow(l, 4)], [(D, F32), (D, BF16)])
        st.update(mix=mix, h_c=h, hn_c=hn)
        (f2, a2, b2), got = ffn_fwd(f"ffn_fwd_{l}1", hn, *ffw(l, 1), side=site(f"ffn_fwd_{l}1"))
        landed(f"ffn_fwd_{l}1", got)
        st.update(f2=f2, a2=a2, b2=b2)
        if l == 0:
            h, hn = rowwise_fwd("resnorm_05", functools.partial(_fn_resnorm, 0.5), [full(h), full(f2)],
                                [nrow(0, 5), nrow(1, 0)], [(D, F32), (D, BF16)])
        else:
            (h,) = rowwise_fwd("res_last", functools.partial(_fn_res_last, 0.5), [full(h), full(f2)],
                               [nrow(1, 5)], [(D, F32)])
        saved.append(st)

    loss_blk, dy = loss_call(h[PADR:], tgt)
    dh = jnp.concatenate([jnp.zeros((PADR, D), F32), dy], axis=0)

    G = {}
    dnorm = [[None] * 6 for _ in range(2)]
    dWg = [[None, None], [None, None]]
    dWu = [[None, None], [None, None]]
    dWd = [[None, None], [None, None]]
    dhn = None
    for l in (1, 0):
        st = saved[l]
        if l == 1:
            (dh_, df), (dw5,) = rowwise_bwd(
                "res_last_b", functools.partial(_fn_res_last, 0.5), [full(st["h_c"]), full(st["f2"])], [nrow(1, 5)],
                [full(dh)], [F32, BF16])
            dnorm[1][5] = dw5
        else:
            (dh_, df), (dw5, dw0n) = rowwise_bwd(
                "resnorm_05_b", functools.partial(_fn_resnorm, 0.5), [full(st["h_c"]), full(st["f2"])],
                [nrow(0, 5), nrow(1, 0)], [full(dh), full(dhn)], [F32, BF16])
            dnorm[0][5] = dw5
            dnorm[1][0] = dw0n
        dh = dh_
        if comm and l == 0:
            units = (0, 1, 2, U_IN, U_OUT)
            comm.grads(1, units, layer_grad_items(1, dWg, dWu, dWd, G["o_in"], G["o_out"], only=units))
        (dxn, da, db, hm), got = ffn_bwd_x(f"ffn_bx_{l}1", df, st["a2"], st["b2"], *ffw(l, 1), side=site(f"ffn_bx_{l}1"))
        landed(f"ffn_bx_{l}1", got)
        dWg[l][1], dWu[l][1], dWd[l][1] = ffn_bwd_w(f"ffn_bw_{l}1", st["hn_c"], df, da, db, hm)
        if comm:
            comm.grads(l, (3, 4, 5), layer_grad_items(l, dWg, dWu, dWd, None, None, only=(3, 4, 5)))
        (dh_, dmix), (dw3, dw4) = rowwise_bwd(
            f"resnorm_{l}3_b", functools.partial(_fn_resnorm, 1.0), [full(st["h_b"]), full(st["mix"])],
            [nrow(l, 3), nrow(l, 4)], [full(dh), full(dxn)], [F32, BF16])
        dnorm[l][3], dnorm[l][4] = dw3, dw4
        dh = dh_
        proj = st["proj"]
        if l == 1:
            G["o_out"] = mm_tn(st["omix"], dmix, "o_out_dw")
            domix = mm_nt(dmix, projw(1)["w_out"], "o_mix_dx")
            (do_g, dgate), (dgn,) = rowwise_bwd(
                "gla_out_b", functools.partial(_fn_gate_out, 256), [full(st["o_g"]), (proj, 1024, O_G // 1024)],
                [W["gla_norm"]], [full(domix)], [F32, F32])
            G["gla_norm"] = dgn
            ins = [(proj, 128, O_Q // 128), (proj, 128, O_K // 128), (proj, 256, O_V // 256), (st["glog"], 128, 0)]
            (dq, dk, dv, dglog), _ = chunk_bwd("gla_bwd", _gla_block, ins, 256, st["s_all"], do_g)
            (dgk,), (dwgu, dbg) = rowwise_bwd("gla_prep_b", _fn_gla_prep, [(proj, LANE, O_GK // LANE)], [wgu, bg],
                                              [full(dglog)], [F32])
            G["gate_up"] = dwgu[:16]
            G["b_gate"] = dbg
            dproj = _bf(jnp.concatenate([dq, dk, dv, dgate, dgk, jnp.zeros((M, O_END - O_GK - LANE), F32)], axis=1))
            G["o_in"] = mm_tn(dproj, st["hn_b"], "o_in_dw")
            dhn_b = mm_nn(dproj, projw(1)["w_in"], "o_proj_dx")
        else:
            G["e_out"] = mm_tn(st["omix"], dmix, "e_out_dw")
            domix = mm_nt(dmix, projw(0)["w_out"], "e_mix_dx")
            (do_dn, dz), (ddn,) = rowwise_bwd(
                "dn_out_b", functools.partial(_fn_gate_out, 128), [full(st["o_dn"]), (proj, 512, E_ZB // 512)],
                [W["dn_norm"]], [(domix, 512, 2)], [F32, F32])
            G["dn_norm"] = ddn
            ins = [(st["qn"], 128, 0), (st["kn"], 128, 0), (st["y"], 128, 8), (st["gb"], 128, 0), (st["bb"], 128, 0)]
            (dqn, dkn, dvv, dgb, dbb), got = chunk_bwd("dn_bwd", _dn_block, ins, 128, st["s_all"], do_dn, side=site("dn_bwd"),
                                                        kept=[st["t_inv"]])
            landed("dn_bwd", got)
            (dyq, dyk, dba), (ddnp,) = rowwise_bwd(
                "dn_prep_b", _fn_dn_prep, [(st["y"], 512, 0), (st["y"], 512, 1), (proj, LANE, E_BA // LANE)], [dnp],
                [full(dqn), full(dkn), full(dgb), full(dbb)], [F32, F32, F32])
            G["a_log"] = ddnp[:, 0:4]
            G["dt_bias"] = ddnp[:, 4:8]
            dyc = jnp.concatenate([dyq, dyk, dvv], axis=1)
            dxc, dconv = conv_bwd(proj, W["conv"], dyc)
            G["conv"] = dconv
            (dq_a, dk_a, dv_a, dbias, dsink), got = swa_bwd(proj, bias, sinks, domix, side=site("swa_bwd"))
            landed("swa_bwd", got)
            G["sinks"] = dsink[:, :8]
            G["rel"] = swa_bias_bwd(dbias, ids)[:, :8]
            dproj = _bf(jnp.concatenate([dq_a, dk_a, dv_a, dxc, dz, dba, jnp.zeros((M, E_END - E_BA - LANE), F32)], axis=1))
            G["e_in"] = mm_tn(dproj, st["hn_b"], "e_in_dw")
            dhn_b = mm_nn(dproj, projw(0)["w_in"], "e_proj_dx")
        (dh_, df), (dw1, dw2) = rowwise_bwd(
            f"resnorm_{l}1_b", functools.partial(_fn_resnorm, 0.5), [full(st["h_a"]), full(st["f1"])],
            [nrow(l, 1), nrow(l, 2)], [full(dh), full(dhn_b)], [F32, BF16])
        dnorm[l][1], dnorm[l][2] = dw1, dw2
        dh = dh_
        if comm and l == 0:
            units = (U_IN, U_OUT)
            comm.grads(0, units, layer_grad_items(0, dWg, dWu, dWd, G["e_in"], G["e_out"], only=units))
        (dxn, da, db, hm), got = ffn_bwd_x(f"ffn_bx_{l}0", df, st["a1"], st["b1"], *ffw(l, 0), side=site(f"ffn_bx_{l}0"))
        landed(f"ffn_bx_{l}0", got)
        dWg[l][0], dWu[l][0], dWd[l][0] = ffn_bwd_w(f"ffn_bw_{l}0", st["hn_a"], df, da, db, hm)
        dhn = dxn
    (dh0p,), (dw00,) = rowwise_bwd("prenorm_0_b", _fn_prenorm, [full(saved[0]["h_a"])], [nrow(0, 0)], [full(dhn)], [F32])
    dnorm[0][0] = dw00
    dh = dh + dh0p
    G["meta"] = dh[ZROWS:PADR]
    G["norm"] = jnp.stack([jnp.concatenate(r, axis=0) for r in dnorm], axis=0)
    G["items"] = [layer_grad_items(0, dWg, dWu, dWd, G["e_in"], G["e_out"], only=(0, 1, 2) if comm else range(NUNITS)),
                  None if comm else layer_grad_items(1, dWg, dWu, dWd, G["o_in"], G["o_out"])]
    return loss_blk, dh[PADR:], G


NAMES = [("meta", "meta_tokens"), ("norm", "norm_w"), ("ffn_g", "ffn_w_gate"), ("ffn_u", "ffn_w_up"),
         ("ffn_d", "ffn_w_down"), ("rel", "rel_bias_table"), ("e_in", "even_w_in"), ("conv", "even_conv_w"),
         ("sinks", "swa_sinks"), ("a_log", "dn_a_log"), ("dt_bias", "dn_dt_bias"), ("dn_norm", "dn_norm_w"),
         ("e_out", "even_w_out"), ("o_in", "odd_w_in"), ("gate_up", "gla_w_gate_up"), ("b_gate", "gla_b_gate"),
         ("gla_norm", "gla_norm_w"), ("o_out", "odd_w_out")]
BIG = ["ffn_g", "ffn_u", "ffn_d", "e_in", "e_out", "o_in", "o_out"]
IN_ROWS = 800
SMALL = [("meta", (16, 256)), ("norm", (2, 6, 256)), ("conv", (1, 4, 384)), ("gate_up", (1, 16, 128)),
         ("b_gate", (1, 128)), ("gla_norm", (1, 64))]
REPL = [("rel", (32, 8)), ("sinks", (1, 8)), ("a_log", (1, 4)), ("dt_bias", (1, 4)), ("dn_norm", (1, 128))]
SMALL_REP = 88 * LANE
SMALL_ROWS = 96


def pack_small(t):
    a = jnp.concatenate([t[n].reshape(-1) for n, _ in SMALL])
    b = jnp.concatenate([t[n].reshape(-1) for n, _ in REPL])
    flat = jnp.concatenate([a, jnp.zeros((SMALL_REP - a.shape[0],), F32), b,
                            jnp.zeros((SMALL_ROWS * LANE - SMALL_REP - b.shape[0],), F32)])
    return flat.reshape(SMALL_ROWS, LANE)


def unpack_small(p):
    flat = p.reshape(-1)
    out, r = {}, 0
    for n, shp in SMALL:
        k = int(np.prod(shp))
        out[n] = flat[r:r + k].reshape(shp)
        r += k
    r = SMALL_REP
    for n, shp in REPL:
        k = int(np.prod(shp))
        out[n] = flat[r:r + k].reshape(shp)
        r += k
    return out


IN_SRC = (706, 772)


NUNITS = 8
U_IN, U_OUT = 6, 7


def _halves(a):
    return a.reshape(2, a.shape[0] // 2, D)


def weight_pieces(wt, l):
    inn = wt["e_in" if l == 0 else "o_in"]
    inn = jnp.pad(inn, ((0, IN_ROWS - inn.shape[0]), (0, 0)))
    ffn = [_halves(wt[n][l][j]) for j in range(2) for n in ("ffn_g", "ffn_u", "ffn_d")]
    return ffn + [_halves(inn), _halves(wt["e_out" if l == 0 else "o_out"])]


def ffn_weights(q, j):
    return tuple(q[3 * j + k].reshape(NSH, FSH, D) for k in range(3))


def proj_weights(l, q_in, q_out):
    m = _even_in_map() if l == 0 else _odd_in_map()
    src = np.where(m >= 0, (m // IN_SRC[l]) * IN_ROWS + m % IN_SRC[l], -1)
    w_out = q_out.reshape(NSH * 256, D)
    w_in = q_in.reshape(NSH * IN_ROWS, D)
    w_in = jnp.where(jnp.asarray(src >= 0)[:, None], jnp.take(w_in, jnp.asarray(np.maximum(src, 0)), axis=0), 0)
    return {"w_in": w_in.astype(q_in.dtype),
            "w_out": _rows(w_out, _even_out_map()) if l == 0 else w_out}


def layer_grad_items(l, dwg, dwu, dwd, g_in, g_out, only=range(NUNITS)):
    units = [[None] * NUNITS for _ in range(2)]

    def put(i, a):
        a = a.reshape(NSH, 2, a.shape[1] // 2, D)
        units[0][i], units[1][i] = a[:, 0], a[:, 1]

    for j in range(2):
        for k, t in enumerate((dwg, dwu, dwd)):
            if 3 * j + k in only:
                put(3 * j + k, t[l][j])
    if U_IN in only:
        m = _even_in_map() if l == 0 else _odd_in_map()
        gi = _rows(g_in, _inverse(m, NSH * IN_SRC[l])).reshape(NSH, IN_SRC[l], D)
        put(U_IN, _bf(jnp.pad(gi, ((0, 0), (0, IN_ROWS - IN_SRC[l]), (0, 0)))))
    if U_OUT in only:
        if l == 0:
            g_out = _rows(g_out, _inverse(_even_out_map(), 1024))
        put(U_OUT, _bf(g_out).reshape(NSH, 256, D))
    return units


def assemble_layer(l, r0, r1):
    whole = lambda i: jnp.concatenate([r0[i], r1[i]])
    return {"ffn_g": jnp.stack([whole(0), whole(3)]), "ffn_u": jnp.stack([whole(1), whole(4)]),
            "ffn_d": jnp.stack([whole(2), whole(5)]), "in": whole(U_IN)[:IN_SRC[l]], "out": whole(U_OUT)}


def big_grads(l0, l1):
    st = lambda n: jnp.stack([l0[n], l1[n]])
    return {"ffn_g": st("ffn_g"), "ffn_u": st("ffn_u"), "ffn_d": st("ffn_d"), "e_in": l0["in"], "e_out": l0["out"],
            "o_in": l1["in"], "o_out": l1["out"]}


def small_from_gathered(gs):
    sm = [unpack_small(gs[s]) for s in range(NSH)]
    full = {}
    full["meta"] = jnp.concatenate([sm[s]["meta"] for s in range(NSH)], axis=1)
    full["norm"] = jnp.concatenate([sm[s]["norm"] for s in range(NSH)], axis=2)
    full["conv"] = jnp.concatenate([sm[s]["conv"][0] for s in range(NSH)], axis=1)
    full["gate_up"] = jnp.concatenate([sm[s]["gate_up"][0] for s in range(NSH)], axis=1)
    full["b_gate"] = jnp.concatenate([sm[s]["b_gate"] for s in range(NSH)], axis=1)
    full["gla_norm"] = jnp.concatenate([sm[s]["gla_norm"] for s in range(NSH)], axis=1)
    return full


def _col_sh(w):
    return jnp.moveaxis(w.reshape(w.shape[0], NSH, w.shape[1] // NSH), 1, 0)


def _to_t(n, a):
    if n in ("ffn_g", "ffn_u"):
        return jnp.swapaxes(a, 2, 3)
    if n in ("e_in", "o_in"):
        return jnp.swapaxes(a[0], 0, 1)
    return a if n == "ffn_d" else a[0]


def _from_t(n, a):
    if n in ("ffn_g", "ffn_u"):
        return jnp.swapaxes(a, 2, 3)
    if n in ("e_in", "o_in"):
        return jnp.swapaxes(a, 0, 1)[None]
    return a if n == "ffn_d" else a[None]


def pack_small_grads(G):
    col_sh = _col_sh
    norm_sh = jnp.moveaxis(G["norm"].reshape(2, 6, NSH, 256), 2, 0)
    a = jnp.concatenate([col_sh(G["meta"]).reshape(NSH, -1), norm_sh.reshape(NSH, -1), col_sh(G["conv"]).reshape(NSH, -1),
                         col_sh(G["gate_up"]).reshape(NSH, -1), G["b_gate"].reshape(NSH, -1),
                         G["gla_norm"].reshape(NSH, -1)], axis=1)
    b = jnp.concatenate([G[n].reshape(-1) for n, _ in REPL])
    b = jnp.broadcast_to(b[None], (NSH, b.shape[0]))
    small = jnp.concatenate([a, jnp.zeros((NSH, SMALL_REP - a.shape[1]), F32), b,
                             jnp.zeros((NSH, SMALL_ROWS * LANE - SMALL_REP - b.shape[1]), F32)], axis=1)
    return small.reshape(NSH, SMALL_ROWS, LANE)


MESH = pl.DeviceIdType.MESH
ANY = pl.BlockSpec(memory_space=pl.ANY)
VMEM = pl.BlockSpec(memory_space=pltpu.VMEM)


def _place():
    return lax.axis_index("x"), lax.axis_index("y"), lax.axis_index("c")


def _other_chips(x, y):
    return [(1 - x, y), (x, 1 - y), (1 - x, 1 - y)]


def _rcopy(send_sems, recv_sems, k, src, dst, to):
    return pltpu.make_async_remote_copy(src_ref=src, dst_ref=dst, send_sem=send_sems.at[k], recv_sem=recv_sems.at[k],
                                        device_id=to, device_id_type=MESH)


def _gather_steps(in_refs, out_refs, send_sems, recv_sems):
    n = len(in_refs)
    x, y, c = _place()
    s = 2 * x + y
    chips = _other_chips(x, y)
    copy = functools.partial(_rcopy, send_sems, recv_sems)
    pairs = [(i, j, cx, cy) for i in range(n) for j, (cx, cy) in enumerate(chips)]
    pushes = lambda: [copy(i * 3 + j, in_refs[i].at[c], out_refs[i].at[s, c], (cx, cy, c)) for i, j, cx, cy in pairs]
    landed = lambda i, cx, cy, half: out_refs[i].at[2 * cx + cy, half]
    relays = lambda: [copy(3 * n + i * 3 + j, landed(i, cx, cy, c), landed(i, cx, cy, c), (x, y, 1 - c)) for i, j, cx, cy in pairs]

    def start():
        for cp in pushes():
            cp.start()

    def relay():
        for i, j, cx, cy in pairs:
            copy(i * 3 + j, landed(i, cx, cy, c), landed(i, cx, cy, c), (x, y, c)).wait_recv()
        for cp in relays():
            cp.start()

    def finish():
        for i, j, cx, cy in pairs:
            copy(3 * n + i * 3 + j, landed(i, cx, cy, 1 - c), landed(i, cx, cy, 1 - c), (x, y, c)).wait_recv()
        for cp in pushes() + relays():
            cp.wait_send()

    return start, relay, finish


def _gather_shapes(pieces):
    return [jax.ShapeDtypeStruct((NSH,) + a.shape, a.dtype) for a in pieces]


def ag_layer(name, pieces):
    n = len(pieces)

    def body(*refs):
        for fn in _gather_steps(refs[:n], refs[n:2 * n], *refs[2 * n:]):
            fn()

    return pl.pallas_call(
        body, name=name, in_specs=[ANY] * n, out_specs=[ANY] * n, out_shape=_gather_shapes(pieces),
        scratch_shapes=[pltpu.SemaphoreType.DMA((6 * n,)), pltpu.SemaphoreType.DMA((6 * n,))],
    )(*pieces)


def gather_side(pieces):
    def events(n_steps, in_refs, out_refs, send_sems, recv_sems):
        start, relay, finish = _gather_steps(in_refs, out_refs, send_sems, recv_sems)
        return [(0, start), (max(3 * n_steps // 4, 1), relay), (n_steps - 1, finish)]

    return Side(pieces, _gather_shapes(pieces), 6 * len(pieces), events)


def ag_small(pack):
    def body(x_ref, out_ref, send_sems, recv_sems):
        x, y, c = _place()
        s = 2 * x + y
        chips = _other_chips(x, y)

        def copy(k, src, dst, to):
            return pltpu.make_async_remote_copy(src_ref=src, dst_ref=dst, send_sem=send_sems.at[k], recv_sem=recv_sems.at[k],
                                                device_id=to, device_id_type=MESH)

        out_ref[s] = x_ref[...]
        sends = [copy(j, x_ref, out_ref.at[s], (cx, cy, c)) for j, (cx, cy) in enumerate(chips)]
        for cp in sends:
            cp.start()
        for j, (cx, cy) in enumerate(chips):
            blk = out_ref.at[2 * cx + cy]
            copy(j, blk, blk, (x, y, c)).wait_recv()
        for cp in sends:
            cp.wait_send()

    return pl.pallas_call(
        body, name="ag_small", in_specs=[VMEM], out_specs=VMEM,
        out_shape=jax.ShapeDtypeStruct((NSH,) + pack.shape, pack.dtype),
        scratch_shapes=[pltpu.SemaphoreType.DMA((3,)), pltpu.SemaphoreType.DMA((3,))],
    )(pack)


def rs_pair(name, items):
    ni = len(items[0])

    def body(*refs):
        in_refs = [refs[:ni], refs[ni:2 * ni]]
        recv_refs = refs[2 * ni:3 * ni]
        send_sems, recv_sems = refs[3 * ni:]
        x, y, c = _place()
        copy = functools.partial(_rcopy, send_sems, recv_sems)
        for cc in range(2):
            @pl.when(c == cc)
            def _():
                cps = [copy(i * NSH + s, in_refs[1 - cc][i].at[s], recv_refs[i].at[s], (x, y, 1 - c))
                       for i in range(ni) for s in range(NSH)]
                for cp in cps:
                    cp.start()
                for cp in cps:
                    cp.wait()

    return pl.pallas_call(
        body, name=name, in_specs=[ANY] * (2 * ni), out_specs=[ANY] * ni,
        out_shape=[jax.ShapeDtypeStruct(a.shape, a.dtype) for a in items[0]],
        scratch_shapes=[pltpu.SemaphoreType.DMA((ni * NSH,)), pltpu.SemaphoreType.DMA((ni * NSH,))],
    )(*items[0], *items[1])


def _scatter_steps(a_refs, out_refs, send_sems, recv_sems):
    n = len(a_refs)
    x, y, c = _place()
    s = 2 * x + y
    chips = _other_chips(x, y)
    copy = functools.partial(_rcopy, send_sems, recv_sems)
    pairs = [(i, j, cx, cy) for i in range(n) for j, (cx, cy) in enumerate(chips)]
    sends = lambda: [copy(i * 3 + j, a_refs[i].at[2 * cx + cy], out_refs[i].at[s], (cx, cy, c)) for i, j, cx, cy in pairs]

    def start():
        for cp in sends():
            cp.start()

    def finish():
        for i, j, cx, cy in pairs:
            blk = out_refs[i].at[2 * cx + cy]
            copy(i * 3 + j, blk, blk, (x, y, c)).wait_recv()
        for cp in sends():
            cp.wait_send()

    return start, finish


def rs_chips(name, arrs):
    n = len(arrs)

    def body(*refs):
        for fn in _scatter_steps(refs[:n], refs[n:2 * n], *refs[2 * n:]):
            fn()

    return pl.pallas_call(
        body, name=name, in_specs=[ANY] * n, out_specs=[ANY] * n,
        out_shape=[jax.ShapeDtypeStruct(a.shape, a.dtype) for a in arrs],
        scratch_shapes=[pltpu.SemaphoreType.DMA((3 * n,)), pltpu.SemaphoreType.DMA((3 * n,))],
    )(*arrs)


def scatter_side(arrs):
    def events(n_steps, in_refs, out_refs, send_sems, recv_sems):
        start, finish = _scatter_steps(in_refs, out_refs, send_sems, recv_sems)
        return [(0, start), (n_steps - 1, finish)]

    return Side(arrs, [jax.ShapeDtypeStruct(a.shape, a.dtype) for a in arrs], 3 * len(arrs), events)


def _pair_chunks(rows):
    return 4 if rows % 32 == 0 else (2 if rows % 16 == 0 else 1)


def ag_pair(name, arrs):
    n = len(arrs)
    chunks = [(i, k * (a.shape[0] // _pair_chunks(a.shape[0])), a.shape[0] // _pair_chunks(a.shape[0]))
              for i, a in enumerate(arrs) for k in range(_pair_chunks(a.shape[0]))]

    def body(*refs):
        g_refs, out_refs = refs[:n], refs[n:2 * n]
        send_sems, recv_sems = refs[2 * n:]
        x, y, c = _place()
        give = [_rcopy(send_sems, recv_sems, q, g_refs[i].at[pl.ds(r0, rc)], out_refs[i].at[pl.ds(r0, rc)], (x, y, 1 - c))
                for q, (i, r0, rc) in enumerate(chunks)]
        for cp in give:
            cp.start()
        for cp in give:
            cp.wait()

    return pl.pallas_call(
        body, name=name, in_specs=[ANY] * n, out_specs=[ANY] * n,
        out_shape=[jax.ShapeDtypeStruct(a.shape, a.dtype) for a in arrs],
        scratch_shapes=[pltpu.SemaphoreType.DMA((len(chunks),)), pltpu.SemaphoreType.DMA((len(chunks),))],
    )(*arrs)


def small_allreduce(p):
    def body(p_ref, out_ref, rbuf, send_sems, recv_sems):
        x, y, c = _place()
        me = 4 * x + 2 * y + c
        rbuf[me] = p_ref[2 * x + y]
        flip = lambda v, f: (1 - v) if f else v
        peers = [(flip(x, k >> 2 & 1), flip(y, k >> 1 & 1), flip(c, k & 1)) for k in range(1, 8)]

        def copy(k, src, dst, to):
            return pltpu.make_async_remote_copy(src_ref=src, dst_ref=dst, send_sem=send_sems.at[k], recv_sem=recv_sems.at[k],
                                                device_id=to, device_id_type=MESH)

        sends = [copy(k, p_ref.at[2 * px + py], rbuf.at[me], (px, py, pc)) for k, (px, py, pc) in enumerate(peers)]
        for cp in sends:
            cp.start()
        for k, (px, py, pc) in enumerate(peers):
            blk = rbuf.at[4 * px + 2 * py + pc]
            copy(k, blk, blk, (x, y, c)).wait_recv()
        for cp in sends:
            cp.wait_send()
        acc = rbuf[0]
        for d in range(1, 8):
            acc = acc + rbuf[d]
        out_ref[...] = acc

    return pl.pallas_call(
        body, name="small_allreduce", in_specs=[VMEM], out_specs=VMEM,
        out_shape=jax.ShapeDtypeStruct(p.shape[1:], F32),
        scratch_shapes=[pltpu.VMEM((8,) + p.shape[1:], F32), pltpu.SemaphoreType.DMA((7,)), pltpu.SemaphoreType.DMA((7,))],
    )(p)


def _rows_tile(rows, cap):
    return _pick(rows, cap) if rows % 128 == 0 else rows


def sum_pair(name, a0s, a1s, recvs, cflag):
    n = len(recvs)

    def body(c_ref, *refs):
        for i in range(n):
            a0_ref, a1_ref, b_ref, o_ref = refs[i], refs[n + i], refs[2 * n + i], refs[3 * n + i]
            own = jnp.where(c_ref[0] == 0, a0_ref[...].astype(F32), a1_ref[...].astype(F32))
            o_ref[...] = (own + b_ref[...].astype(F32)).astype(o_ref.dtype)

    specs = [pl.BlockSpec((None,) + a.shape[1:], lambda s: (s, 0, 0)) for a in recvs]
    return pl.pallas_call(
        body, name=name, grid=(NSH,), in_specs=[pl.BlockSpec(memory_space=pltpu.SMEM)] + specs * 3, out_specs=specs,
        out_shape=[jax.ShapeDtypeStruct(a.shape, BF16) for a in recvs], compiler_params=_cparams(("parallel",)),
    )(cflag, *a0s, *a1s, *recvs)


def sum_chips(name, parts, owns, sflag):
    n = len(parts)

    def body(s_ref, *refs):
        for i in range(n):
            p_ref, a_ref, o_ref = refs[i], refs[n + i], refs[2 * n + i]
            acc = None
            for t in range(NSH):
                term = jnp.where(s_ref[0] == t, a_ref[t].astype(F32), p_ref[t].astype(F32))
                acc = term if acc is None else acc + term
            o_ref[...] = acc

    specs = [pl.BlockSpec(a.shape, lambda i: (0, 0, 0)) for a in parts]
    return pl.pallas_call(
        body, name=name, grid=(1,), in_specs=[pl.BlockSpec(memory_space=pltpu.SMEM)] + specs * 2,
        out_specs=[pl.BlockSpec(a.shape[1:], lambda i: (0, 0)) for a in parts],
        out_shape=[jax.ShapeDtypeStruct(a.shape[1:], F32) for a in parts], compiler_params=_cparams(("arbitrary",)),
    )(sflag, *parts, *owns)


ADAM_LR, ADAM_B1, ADAM_B2, ADAM_EPS, ADAM_WD, ADAM_STEP = 0.001, 0.9, 0.999, 1e-08, 0.01, 10


def adamw_call(name, w, g, m, v):
    rows, cols = w.shape
    tr = _rows_tile(rows, 512)

    def body(w_ref, g_ref, m_ref, v_ref, d_ref, nm_ref, nv_ref):
        g_ = g_ref[...]
        m_ = ADAM_B1 * m_ref[...] + (1.0 - ADAM_B1) * g_
        v_ = ADAM_B2 * v_ref[...] + (1.0 - ADAM_B2) * (g_ * g_)
        m_hat = m_ / (1.0 - ADAM_B1 ** ADAM_STEP)
        v_hat = v_ / (1.0 - ADAM_B2 ** ADAM_STEP)
        d_ref[...] = -ADAM_LR * (m_hat / (jnp.sqrt(v_hat) + ADAM_EPS) + ADAM_WD * w_ref[...])
        nm_ref[...] = m_
        nv_ref[...] = v_

    spec = pl.BlockSpec((tr, cols), lambda i: (i, 0))
    sh = jax.ShapeDtypeStruct((rows, cols), F32)
    return pl.pallas_call(
        body, name=name, grid=(rows // tr,), in_specs=[spec] * 4, out_specs=[spec] * 3, out_shape=[sh] * 3,
        compiler_params=_cparams(("parallel",)),
    )(w, g, m, v)


def kernel(x, meta_tokens, norm_w, ffn_w_gate, ffn_w_up, ffn_w_down, rel_bias_table, even_w_in, even_conv_w, swa_sinks, dn_a_log, dn_dt_bias, dn_norm_w, even_w_out, odd_w_in, gla_w_gate_up, gla_b_gate, gla_norm_w, odd_w_out, loss_target, m_meta_tokens, m_norm_w, m_ffn_w_gate, m_ffn_w_up, m_ffn_w_down, m_rel_bias_table, m_even_w_in, m_even_conv_w, m_swa_sinks, m_dn_a_log, m_dn_dt_bias, m_dn_norm_w, m_even_w_out, m_odd_w_in, m_gla_w_gate_up, m_gla_b_gate, m_gla_norm_w, m_odd_w_out, v_meta_tokens, v_norm_w, v_ffn_w_gate, v_ffn_w_up, v_ffn_w_down, v_rel_bias_table, v_even_w_in, v_even_conv_w, v_swa_sinks, v_dn_a_log, v_dn_dt_bias, v_dn_norm_w, v_even_w_out, v_odd_w_in, v_gla_w_gate_up, v_gla_b_gate, v_gla_norm_w, v_odd_w_out):
    ws = [meta_tokens, norm_w, ffn_w_gate, ffn_w_up, ffn_w_down, rel_bias_table, even_w_in, even_conv_w, swa_sinks, dn_a_log,
          dn_dt_bias, dn_norm_w, even_w_out, odd_w_in, gla_w_gate_up, gla_b_gate, gla_norm_w, odd_w_out]
    ms = [m_meta_tokens, m_norm_w, m_ffn_w_gate, m_ffn_w_up, m_ffn_w_down, m_rel_bias_table, m_even_w_in, m_even_conv_w,
          m_swa_sinks, m_dn_a_log, m_dn_dt_bias, m_dn_norm_w, m_even_w_out, m_odd_w_in, m_gla_w_gate_up, m_gla_b_gate,
          m_gla_norm_w, m_odd_w_out]
    vs = [v_meta_tokens, v_norm_w, v_ffn_w_gate, v_ffn_w_up, v_ffn_w_down, v_rel_bias_table, v_even_w_in, v_even_conv_w,
          v_swa_sinks, v_dn_a_log, v_dn_dt_bias, v_dn_norm_w, v_even_w_out, v_odd_w_in, v_gla_w_gate_up, v_gla_b_gate,
          v_gla_norm_w, v_odd_w_out]
    short = [n for n, _ in NAMES]
    w = dict(zip(short, ws))
    m = dict(zip(short, ms))
    v = dict(zip(short, vs))

    wt = {n: _to_t(n, w[n]) for n in BIG}
    own = {n: wt[n].astype(BF16) for n in BIG}
    sflag = (2 * lax.axis_index("x") + lax.axis_index("y")).astype(jnp.int32).reshape(1)
    cflag = lax.axis_index("c").astype(jnp.int32).reshape(1)
    is0 = cflag[0] == 0
    fill = lambda got, pieces: [lax.dynamic_update_index_in_dim(g_, p_, sflag[0], 0) for g_, p_ in zip(got, pieces)]
    pieces = [weight_pieces(own, l) for l in range(2)]
    small = small_from_gathered(ag_small(pack_small(w)))
    def reduce_finish(l, mine, parts):
        red = [None] * NUNITS
        for grp in ((0, 1, 2), (3, 4, 5), (U_IN, U_OUT)):
            outs = sum_chips(f"sum_chips_{l}{grp[0]}", [parts[i] for i in grp], [mine[i] for i in grp], sflag)
            for i, o in zip(grp, outs):
                red[i] = o
        got = ag_pair(f"ag_pair_{l}", red)
        return [jnp.where(is0, r_, g_) for r_, g_ in zip(red, got)], [jnp.where(is0, g_, r_) for r_, g_ in zip(red, got)]

    class Exchanges:
        gathers = {"ffn_fwd_00": [(0, 3), (0, U_IN), (0, U_OUT)], "swa_fwd": [(0, 4)], "dn_fwd": [(0, 5), (1, 0), (1, 1)],
                   "ffn_fwd_01": [(1, 2), (1, U_IN), (1, U_OUT)], "ffn_fwd_10": [(1, 3), (1, 4), (1, 5)]}
        scatters = {"ffn_bx_10": [(1, 3), (1, 4), (1, 5)], "ffn_bx_01": [(1, 0), (1, 1), (1, 2)],
                    "dn_bwd": [(1, U_IN), (1, U_OUT), (0, 3)], "swa_bwd": [(0, 4), (0, 5)], "ffn_bx_00": [(0, U_IN), (0, U_OUT)]}

        def __init__(self):
            self.q = [[None] * NUNITS for _ in range(2)]
            self.mine = [[None] * NUNITS for _ in range(2)]
            self.parts = [[None] * NUNITS for _ in range(2)]

        def side(self, name):
            if name in self.gathers:
                return gather_side([pieces[l][i] for l, i in self.gathers[name]])
            if name in self.scatters:
                return scatter_side([self.mine[l][i] for l, i in self.scatters[name]])
            return None

        def done(self, name, outs):
            if name in self.gathers:
                units = self.gathers[name]
                for (l, i), a in zip(units, fill(outs, [pieces[l][i] for l, i in units])):
                    self.q[l][i] = a
            if name in self.scatters:
                for (l, i), a in zip(self.scatters[name], outs):
                    self.parts[l][i] = a

        def ffn(self, l, j):
            return ffn_weights(self.q[l], j)

        def proj(self, l):
            return proj_weights(l, self.q[l][U_IN], self.q[l][U_OUT])

        def grads(self, l, units, items):
            units = list(units)
            half0, half1 = [[items[c][i] for i in units] for c in range(2)]
            recv = rs_pair(f"rs_pair_{l}{units[0]}", [half0, half1])
            for i, a in zip(units, sum_pair(f"sum_pair_{l}{units[0]}", half0, half1, recv, cflag)):
                self.mine[l][i] = a

    ex = Exchanges()
    first = [0, 1, 2]
    for i, a in zip(first, fill(ag_layer("ag_layer_0", [pieces[0][i] for i in first]), [pieces[0][i] for i in first])):
        ex.q[0][i] = a
    W = {**small, **{n: w[n] for n, _ in REPL}, "ffn": ex.ffn, "proj": ex.proj}
    loss_blk, gx, G = core_step(x[0], loss_target[0], W, comm=ex)

    ex.grads(0, first, G["items"][0])
    for i, a in zip(first, rs_chips("rs_chips_0", [ex.mine[0][i] for i in first])):
        ex.parts[0][i] = a
    lay0 = assemble_layer(0, *reduce_finish(0, ex.mine[0], ex.parts[0]))
    lay1 = assemble_layer(1, *reduce_finish(1, ex.mine[1], ex.parts[1]))
    gt = big_grads(lay0, lay1)
    g_small_pack = small_allreduce(pack_small_grads(G))
    g = {**{n: _from_t(n, gt[n]) for n in BIG}, **unpack_small(g_small_pack)}

    delta, new_m, new_v = {}, {}, {}
    for n in BIG:
        shp = wt[n].shape
        two = lambda t: t.reshape(-1, D)
        d_, m_, v_ = adamw_call("adamw_" + n, two(wt[n]), two(gt[n]), two(_to_t(n, m[n])), two(_to_t(n, v[n])))
        delta[n], new_m[n], new_v[n] = (_from_t(n, t.reshape(shp)) for t in (d_, m_, v_))
    d_, m_, v_ = adamw_call("adamw_small", pack_small(w), g_small_pack, pack_small(m), pack_small(v))
    delta.update(unpack_small(d_))
    new_m.update(unpack_small(m_))
    new_v.update(unpack_small(v_))

    loss = lax.psum(loss_blk[0, 0], ("x", "y", "c"))
    return (loss, gx[None], *[g[n] for n in short], *[delta[n] for n in short], *[new_m[n] for n in short],
            *[new_v[n] for n in short])
```

```python
import functools
import math

import numpy as np
import jax
import jax.numpy as jnp
from jax import lax
from jax.experimental import pallas as pl
from jax.experimental.pallas import tpu as pltpu

F32 = jnp.float32
BF16 = jnp.bfloat16
HI = lax.Precision.HIGHEST

D = 1024
N_META = 16
PADR = 128
ZROWS = PADR - N_META
D_FF = 2816
NSH = 4
FSH = D_FF // NSH
EPS = 1e-6
NEG = -1e30
CH = 64
CPS = 3
BLK = 128
LANE = 128
VMEM_LIMIT = 56 * 1024 * 1024
FFN_SUB = 4

E_QA, E_KA, E_VA, E_QB, E_KB, E_VB, E_ZB, E_BA, E_END = 0, 1024, 1280, 1536, 2048, 2560, 3072, 3584, 4096


def _even_in_map():
    m = np.full((E_END,), -1, np.int64)
    for h in range(8):
        m[E_QA + h * 128:E_QA + h * 128 + 64] = np.arange(h * 64, (h + 1) * 64)
    for h in range(2):
        m[E_KA + h * 128:E_KA + h * 128 + 64] = 512 + np.arange(h * 64, (h + 1) * 64)
        m[E_VA + h * 128:E_VA + h * 128 + 64] = 640 + np.arange(h * 64, (h + 1) * 64)
    m[E_QB:E_QB + 2048] = 768 + np.arange(2048)
    m[E_BA:E_BA + 8] = 2816 + np.arange(8)
    return m


def _even_out_map():
    m = np.full((1536,), -1, np.int64)
    for h in range(8):
        m[h * 128:h * 128 + 64] = np.arange(h * 64, (h + 1) * 64)
    m[1024:1536] = 512 + np.arange(512)
    return m


O_Q, O_K, O_V, O_G, O_GK, O_END = 0, 512, 1024, 2048, 3072, 3584


def _odd_in_map():
    m = np.full((O_END,), -1, np.int64)
    m[:3072] = np.arange(3072)
    m[O_GK:O_GK + 16] = 3072 + np.arange(16)
    return m


def _inverse(m, n):
    inv = np.zeros((n,), np.int64)
    for p, o in enumerate(m):
        if o >= 0:
            inv[o] = p
    return inv


def _rows(w, m):
    parts, i, n = [], 0, len(m)
    while i < n:
        j = i + 1
        if m[i] < 0:
            while j < n and m[j] < 0:
                j += 1
            parts.append(jnp.zeros((j - i,) + w.shape[1:], w.dtype))
        else:
            while j < n and m[j] == m[j - 1] + 1:
                j += 1
            parts.append(lax.slice_in_dim(w, int(m[i]), int(m[i]) + j - i, axis=0))
        i = j
    return jnp.concatenate(parts, axis=0)


def _mm(a, b, prec=HI):
    return lax.dot_general(a, b, (((1,), (0,)), ((), ())), precision=prec, preferred_element_type=F32)


def _mm_nt(a, b, prec=HI):
    return lax.dot_general(a, b, (((1,), (1,)), ((), ())), precision=prec, preferred_element_type=F32)


def _mm_tn(a, b, prec=HI):
    return lax.dot_general(a, b, (((0,), (0,)), ((), ())), precision=prec, preferred_element_type=F32)


def _bdot(a, b, dims):
    return lax.dot_general(a.astype(BF16), b.astype(BF16), (dims, ((), ())), preferred_element_type=F32)


@jax.custom_vjp
def _bmm(a, b):
    return _bdot(a, b, ((1,), (0,)))


@jax.custom_vjp
def _bmm_nt(a, b):
    return _bdot(a, b, ((1,), (1,)))


@jax.custom_vjp
def _bmm_tn(a, b):
    return _bdot(a, b, ((0,), (0,)))


_bmm.defvjp(lambda a, b: (_bmm(a, b), (a, b)), lambda r, g: (_bmm_nt(g, r[1]), _bmm_tn(r[0], g)))
_bmm_nt.defvjp(lambda a, b: (_bmm_nt(a, b), (a, b)), lambda r, g: (_bmm(g, r[1]), _bmm_tn(g, r[0])))
_bmm_tn.defvjp(lambda a, b: (_bmm_tn(a, b), (a, b)), lambda r, g: (_bmm_nt(r[1], g), _bmm(r[0], g)))


def _hi_lo(x):
    h = x.astype(BF16)
    return h, (x - h.astype(F32)).astype(BF16)


def _xdot(a, b, dims):
    ah, al = _hi_lo(a)
    bh, bl = _hi_lo(b)
    d = lambda p, q: lax.dot_general(p, q, (dims, ((), ())), preferred_element_type=F32)
    return d(ah, bh) + (d(ah, bl) + d(al, bh))


@jax.custom_vjp
def _xmm(a, b):
    return _xdot(a, b, ((1,), (0,)))


@jax.custom_vjp
def _xmm_nt(a, b):
    return _xdot(a, b, ((1,), (1,)))


@jax.custom_vjp
def _xmm_tn(a, b):
    return _xdot(a, b, ((0,), (0,)))


_xmm.defvjp(lambda a, b: (_xmm(a, b), (a, b)), lambda r, g: (_xmm_nt(g, r[1]), _xmm_tn(r[0], g)))
_xmm_nt.defvjp(lambda a, b: (_xmm_nt(a, b), (a, b)), lambda r, g: (_xmm(g, r[1]), _xmm_tn(g, r[0])))
_xmm_tn.defvjp(lambda a, b: (_xmm_tn(a, b), (a, b)), lambda r, g: (_xmm_nt(r[1], g), _xmm(r[0], g)))


def _sum01(m01, x, dims):
    h, l = _hi_lo(x)
    l2 = (x - h.astype(F32) - l.astype(F32)).astype(BF16)
    m = m01.astype(BF16)
    d = lambda q: lax.dot_general(m, q, (dims, ((), ())), preferred_element_type=F32)
    return d(h) + (d(l) + d(l2))


@jax.custom_vjp
def _cumsum_rows(x):
    n = x.shape[0]
    tri = lax.broadcasted_iota(jnp.int32, (n, n), 0) >= lax.broadcasted_iota(jnp.int32, (n, n), 1)
    return _sum01(tri, x, ((1,), (0,)))


def _cumsum_rows_b(_, g):
    n = g.shape[0]
    tri = lax.broadcasted_iota(jnp.int32, (n, n), 0) >= lax.broadcasted_iota(jnp.int32, (n, n), 1)
    return (_sum01(tri, g, ((0,), (0,))),)


_cumsum_rows.defvjp(lambda x: (_cumsum_rows(x), None), _cumsum_rows_b)


@functools.partial(jax.custom_vjp, nondiff_argnums=(1,))
def _colsum_as_rows(x, width):
    return _colsum_impl(x, width)


def _colsum_impl(x, width):
    h, l = _hi_lo(x)
    l2 = (x - h.astype(F32) - l.astype(F32)).astype(BF16)
    ones = jnp.ones((x.shape[0], width), BF16)
    d = lambda q: lax.dot_general(q, ones, (((0,), (0,)), ((), ())), preferred_element_type=F32)
    return d(h) + (d(l) + d(l2))


def _colsum_as_rows_f(x, width):
    return _colsum_impl(x, width), x.shape[0]


def _colsum_as_rows_b(width, n, g):
    return (_sum01(jnp.ones((n, width), F32), g, ((1,), (1,))),)


_colsum_as_rows.defvjp(_colsum_as_rows_f, _colsum_as_rows_b)


def _rms(x, w):
    return x * lax.rsqrt(jnp.mean(x * x, axis=-1, keepdims=True) + EPS) * w


def _sigmoid(x):
    return 1.0 / (1.0 + jnp.exp(-x))


def _silu(x):
    return x * _sigmoid(x)


def _softplus(x):
    return jnp.maximum(x, 0.0) + jnp.log(1.0 + jnp.exp(-jnp.abs(x)))


def _lane_pick(row, idx):
    lane = lax.broadcasted_iota(jnp.int32, row.shape, row.ndim - 1)
    return jnp.sum(jnp.where(lane == idx, row, 0.0), axis=-1, keepdims=True)


def _row_ids(row0, n):
    return row0 + lax.broadcasted_iota(jnp.int32, (n, 1), 0)


def _pick(m, cap):
    best = 64
    for t in range(64, min(m, cap) + 1, 64):
        if m % t == 0:
            best = t
    return best


def _cparams(sem):
    return pltpu.CompilerParams(dimension_semantics=sem, vmem_limit_bytes=VMEM_LIMIT)


def mm_nn(a, b, name, out_dtype=F32):
    M, K = a.shape
    N = b.shape[1]
    tm = _pick(M, 1408 if K <= 2048 else 704)
    tn = _pick(N, 512)

    def body(a_ref, b_ref, o_ref):
        o_ref[...] = _mm(a_ref[...], b_ref[...], None).astype(o_ref.dtype)

    return pl.pallas_call(
        body, name=name, grid=(N // tn, M // tm),
        in_specs=[pl.BlockSpec((tm, K), lambda j, i: (i, 0)), pl.BlockSpec((K, tn), lambda j, i: (0, j))],
        out_specs=pl.BlockSpec((tm, tn), lambda j, i: (i, j)),
        out_shape=jax.ShapeDtypeStruct((M, N), out_dtype),
        compiler_params=_cparams(("parallel", "parallel")),
    )(a, b)


def mm_nt(a, b, name, out_dtype=F32):
    M, K = a.shape
    N = b.shape[0]
    tm = _pick(M, 1408)
    tn = _pick(N, 512 if K > 2048 else 1024)

    def body(a_ref, b_ref, o_ref):
        o_ref[...] = _mm_nt(a_ref[...], b_ref[...], None).astype(o_ref.dtype)

    return pl.pallas_call(
        body, name=name, grid=(N // tn, M // tm),
        in_specs=[pl.BlockSpec((tm, K), lambda j, i: (i, 0)), pl.BlockSpec((tn, K), lambda j, i: (j, 0))],
        out_specs=pl.BlockSpec((tm, tn), lambda j, i: (i, j)),
        out_shape=jax.ShapeDtypeStruct((M, N), out_dtype),
        compiler_params=_cparams(("parallel", "parallel")),
    )(a, b)


def mm_tn(a, b, name):
    M, K = a.shape
    N = b.shape[1]
    tk = _pick(K, 512)
    tn = _pick(N, 512)

    def body(a_ref, b_ref, o_ref):
        o_ref[...] = _mm_tn(a_ref[...], b_ref[...], None)

    return pl.pallas_call(
        body, name=name, grid=(K // tk, N // tn),
        in_specs=[pl.BlockSpec((M, tk), lambda i, j: (0, i)), pl.BlockSpec((M, tn), lambda i, j: (0, j))],
        out_specs=pl.BlockSpec((tk, tn), lambda i, j: (i, j)),
        out_shape=jax.ShapeDtypeStruct((K, N), F32),
        compiler_params=_cparams(("parallel", "parallel")),
    )(a, b)


def _row_specs(rows, tm):
    return [pl.BlockSpec((tm, w), functools.partial(lambda i, cb: (i, cb), cb=cb)) for (_, w, cb) in rows]


def _param_specs(params):
    return [pl.BlockSpec(p.shape, functools.partial(lambda i, nd: (0,) * nd, nd=p.ndim)) for p in params]


def rowwise_fwd(name, fn, rows, params, outs, tm=None):
    M = rows[0][0].shape[0]
    tm = tm or _pick(M, 704)
    nr, npar = len(rows), len(params)

    def body(*refs):
        row0 = pl.program_id(0) * tm
        vals = [r[...].astype(F32) for r in refs[:nr]] + [p[...] for p in refs[nr:nr + npar]]
        res = fn(row0, *vals)
        for o_ref, r in zip(refs[nr + npar:], res):
            o_ref[...] = r.astype(o_ref.dtype)

    return pl.pallas_call(
        body, name=name, grid=(M // tm,),
        in_specs=_row_specs(rows, tm) + _param_specs(params),
        out_specs=[pl.BlockSpec((tm, w), lambda i: (i, 0)) for (w, _) in outs],
        out_shape=[jax.ShapeDtypeStruct((M, w), dt) for (w, dt) in outs],
        compiler_params=_cparams(("parallel",)),
    )(*[r[0] for r in rows], *params)


def rowwise_bwd(name, fn, rows, params, douts, drow_dtypes, tm=None):
    M = rows[0][0].shape[0]
    tm = tm or _pick(M, 704)
    nr, npar, nd = len(rows), len(params), len(douts)
    want = [k for k, dt in enumerate(drow_dtypes) if dt is not None]

    def body(*refs):
        i = pl.program_id(0)
        row0 = i * tm
        vals = [r[...].astype(F32) for r in refs[:nr]] + [p[...] for p in refs[nr:nr + npar]]
        cots = tuple(d[...].astype(F32) for d in refs[nr + npar:nr + npar + nd])
        _, vjp = jax.vjp(functools.partial(fn, row0), *vals)
        grads = vjp(cots)
        o_refs = refs[nr + npar + nd:]
        for o_ref, k in zip(o_refs[:len(want)], want):
            o_ref[...] = grads[k].astype(o_ref.dtype)
        for o_ref, g in zip(o_refs[len(want):], grads[nr:]):
            @pl.when(i == 0)
            def _():
                o_ref[...] = g

            @pl.when(i > 0)
            def _():
                o_ref[...] += g

    res = pl.pallas_call(
        body, name=name, grid=(M // tm,),
        in_specs=_row_specs(rows, tm) + _param_specs(params) + _row_specs(douts, tm),
        out_specs=[pl.BlockSpec((tm, rows[k][1]), lambda i: (i, 0)) for k in want] + _param_specs(params),
        out_shape=[jax.ShapeDtypeStruct((M, rows[k][1]), drow_dtypes[k]) for k in want]
        + [jax.ShapeDtypeStruct(p.shape, F32) for p in params],
        compiler_params=_cparams(("arbitrary",)),
    )(*[r[0] for r in rows], *params, *[d[0] for d in douts])
    return res[:len(want)], res[len(want):]


def _fn_prenorm(row0, h, wpre):
    return (_rms(h, wpre),)


def _fn_resnorm(scale, row0, h, f, wpost, wpre):
    h2 = h + scale * _rms(f, wpost)
    return h2, _rms(h2, wpre)


def _fn_res_last(scale, row0, h, f, wpost):
    return (h + scale * _rms(f, wpost),)


def ffn_fwd(name, xn, wg, wu, wd, side=None):
    M = xn.shape[0]
    tm = _pick(M, 704)
    s_ins, s_specs, s_shapes, s_sems = _side_parts(side)
    ns, nso = len(s_ins), len(s_shapes)

    def body(x_ref, wg_ref, wu_ref, wd_ref, *rest):
        f_ref, a_ref, b_ref = rest[ns:ns + 3]
        s = pl.program_id(1)
        _side_run(side, (M // tm) * NSH, rest[:ns], rest[ns + 3:ns + 3 + nso], rest[ns + 3 + nso:],
                  step=pl.program_id(0) * NSH + s)
        x = x_ref[...]
        a = _mm_nt(x, wg_ref[...], None)
        b = _mm_nt(x, wu_ref[...], None)
        c = _mm((_silu(a) * b).astype(BF16), wd_ref[...], None)

        @pl.when(s == 0)
        def _():
            f_ref[...] = c

        @pl.when(s > 0)
        def _():
            f_ref[...] += c

        a_ref[...] = a.astype(BF16)
        b_ref[...] = b.astype(BF16)

    wspec = wdspec = pl.BlockSpec((None, FSH, D), lambda i, s: (s, 0, 0))
    abspec = pl.BlockSpec((None, tm, FSH), lambda i, s: (s, i, 0))
    res = pl.pallas_call(
        body, name=name, grid=(M // tm, NSH),
        in_specs=[pl.BlockSpec((tm, D), lambda i, s: (i, 0)), wspec, wspec, wdspec] + s_specs,
        out_specs=[pl.BlockSpec((tm, D), lambda i, s: (i, 0)), abspec, abspec] + [pl.BlockSpec(memory_space=pl.ANY)] * nso,
        out_shape=[jax.ShapeDtypeStruct((M, D), F32), jax.ShapeDtypeStruct((NSH, M, FSH), BF16),
                   jax.ShapeDtypeStruct((NSH, M, FSH), BF16)] + s_shapes,
        scratch_shapes=s_sems,
        compiler_params=_cparams(("arbitrary", "arbitrary")),
    )(xn, wg, wu, wd, *s_ins)
    return res[:3], res[3:]


def ffn_bwd_x(name, df, a, b, wg, wu, wd, side=None):
    M = df.shape[0]
    tm = _pick(M, 704)
    ts = tm // FFN_SUB
    s_ins, s_specs, s_shapes, s_sems = _side_parts(side)
    ns, nso = len(s_ins), len(s_shapes)

    def body(df_ref, a_ref, b_ref, wg_ref, wu_ref, wd_ref, *rest):
        dx_ref, da_ref, db_ref, hm_ref = rest[ns:ns + 4]
        _side_run(side, (M // tm) * NSH, rest[:ns], rest[ns + 4:ns + 4 + nso], rest[ns + 4 + nso:],
                  step=pl.program_id(0) * NSH + pl.program_id(1))

        @pl.when(pl.program_id(1) == 0)
        def _():
            dx_ref[...] = jnp.zeros_like(dx_ref)

        for r in range(FFN_SUB):
            rows = pl.ds(r * ts, ts)
            a_ = a_ref[rows, :].astype(F32)
            b_ = b_ref[rows, :].astype(F32)
            dh = _mm_nt(df_ref[rows, :], wd_ref[...], None)
            sig = _sigmoid(a_)
            sil = a_ * sig
            da = (dh * b_ * (sig * (1.0 + a_ * (1.0 - sig)))).astype(BF16)
            db = (dh * sil).astype(BF16)
            dx_ref[rows, :] += _mm(da, wg_ref[...], None) + _mm(db, wu_ref[...], None)
            da_ref[rows, :] = da
            db_ref[rows, :] = db
            hm_ref[rows, :] = (sil * b_).astype(BF16)

    wspec = wdspec = pl.BlockSpec((None, FSH, D), lambda i, s: (s, 0, 0))
    abspec = pl.BlockSpec((None, tm, FSH), lambda i, s: (s, i, 0))
    ab = jax.ShapeDtypeStruct((NSH, M, FSH), BF16)
    res = pl.pallas_call(
        body, name=name, grid=(M // tm, NSH),
        in_specs=[pl.BlockSpec((tm, D), lambda i, s: (i, 0)), abspec, abspec, wspec, wspec, wdspec] + s_specs,
        out_specs=[pl.BlockSpec((tm, D), lambda i, s: (i, 0)), abspec, abspec, abspec] + [pl.BlockSpec(memory_space=pl.ANY)] * nso,
        out_shape=[jax.ShapeDtypeStruct((M, D), F32), ab, ab, ab] + s_shapes,
        scratch_shapes=s_sems,
        compiler_params=_cparams(("arbitrary", "arbitrary")),
    )(df, a, b, wg, wu, wd, *s_ins)
    return res[:4], res[4:]


def ffn_bwd_w(name, xn, df, da, db, hm):
    M = xn.shape[0]
    tm = _pick(M, 704)
    nt = M // tm

    def body(x_ref, df_ref, da_ref, db_ref, hm_ref, dwg_ref, dwu_ref, dwd_ref, ag, au, ad):
        i = pl.program_id(1)
        x = x_ref[...]
        g = _mm_tn(da_ref[...], x, None)
        u = _mm_tn(db_ref[...], x, None)
        d = _mm_tn(hm_ref[...], df_ref[...], None)

        @pl.when(i == 0)
        def _():
            ag[...] = g
            au[...] = u
            ad[...] = d

        @pl.when(i > 0)
        def _():
            ag[...] += g
            au[...] += u
            ad[...] += d

        @pl.when(i == nt - 1)
        def _():
            dwg_ref[...] = ag[...].astype(BF16)
            dwu_ref[...] = au[...].astype(BF16)
            dwd_ref[...] = ad[...].astype(BF16)

    xspec = pl.BlockSpec((tm, D), lambda s, i: (i, 0))
    abspec = pl.BlockSpec((None, tm, FSH), lambda s, i: (s, i, 0))
    return pl.pallas_call(
        body, name=name, grid=(NSH, nt),
        in_specs=[xspec, xspec, abspec, abspec, abspec],
        out_specs=[pl.BlockSpec((None, FSH, D), lambda s, i: (s, 0, 0))] * 3,
        out_shape=[jax.ShapeDtypeStruct((NSH, FSH, D), BF16)] * 3,
        scratch_shapes=[pltpu.VMEM((FSH, D), F32)] * 3,
        compiler_params=_cparams(("parallel", "arbitrary")),
    )(xn, df, da, db, hm)


def _t5_bucket_np(rel):
    n = np.maximum(rel, 0)
    n_f = np.maximum(n, 1).astype(np.float32)
    large = 16 + (np.log(n_f / np.float32(16)) / np.float32(math.log(8.0)) * np.float32(16)).astype(np.int32)
    large = np.minimum(large, 31)
    return np.where(n < 16, n, large).astype(np.int32)


def _swa_bucket_ids():
    qi = np.arange(BLK)[:, None]
    kj = np.arange(BLK)[None, :]
    out = np.full((3, BLK, 3 * BLK), -1, np.int32)
    for v in range(3):
        pos_q = v * BLK + qi - ZROWS
        rel_m = pos_q - (kj - ZROWS)
        ok_m = (kj >= ZROWS) & (rel_m >= 0) & (pos_q >= 0)
        out[v, :, 0:BLK] = np.where(ok_m, _t5_bucket_np(rel_m), -1)
        pos_kp = (v - 1) * BLK + kj - ZROWS
        rel_p = BLK + qi - kj
        ok_p = (pos_kp >= N_META) & (rel_p >= 0) & (rel_p < BLK) & np.full_like(ok_m, v >= 1)
        out[v, :, BLK:2 * BLK] = np.where(ok_p, _t5_bucket_np(rel_p), -1)
        pos_kc = v * BLK + kj - ZROWS
        rel_c = qi - kj
        ok_c = (pos_kc >= N_META) & (rel_c >= 0) & (rel_c < BLK)
        out[v, :, 2 * BLK:] = np.where(ok_c, _t5_bucket_np(rel_c), -1)
    return out


def swa_bias_fwd(table, ids):
    def body(t_ref, id_ref, o_ref):
        for v in range(3):
            for h in range(8):
                o_ref[v, h] = jnp.where(id_ref[v] < 0, NEG, 0.0)

            def step(b, carry):
                hit = id_ref[v] == b
                for h in range(8):
                    o_ref[v, h] += jnp.where(hit, t_ref[b, h], 0.0)
                return carry

            lax.fori_loop(0, 32, step, 0)

    return pl.pallas_call(
        body, name="swa_bias_fwd",
        in_specs=[pl.BlockSpec(memory_space=pltpu.SMEM), pl.BlockSpec(memory_space=pltpu.VMEM)],
        out_specs=pl.BlockSpec(memory_space=pltpu.VMEM),
        out_shape=jax.ShapeDtypeStruct((3, 8, BLK, 3 * BLK), F32),
        compiler_params=pltpu.CompilerParams(vmem_limit_bytes=VMEM_LIMIT),
    )(table, ids)


def swa_bias_bwd(dbias, ids):
    def body(d_ref, id_ref, o_ref):
        r = lax.broadcasted_iota(jnp.int32, (32, LANE), 0)
        c = lax.broadcasted_iota(jnp.int32, (32, LANE), 1)

        def step(b, acc):
            for v in range(3):
                hit = id_ref[v] == b
                for h in range(8):
                    m = jnp.where(hit, d_ref[v, h], 0.0)
                    s = jnp.sum(jnp.sum(m, axis=1, keepdims=True), axis=0, keepdims=True)
                    acc = acc + jnp.where((r == b) & (c == h), s, 0.0)
            return acc

        o_ref[...] = lax.fori_loop(0, 32, step, jnp.zeros((32, LANE), F32))

    return pl.pallas_call(
        body, name="swa_bias_bwd",
        in_specs=[pl.BlockSpec(memory_space=pltpu.VMEM), pl.BlockSpec(memory_space=pltpu.VMEM)],
        out_specs=pl.BlockSpec(memory_space=pltpu.VMEM),
        out_shape=jax.ShapeDtypeStruct((32, LANE), F32),
        compiler_params=pltpu.CompilerParams(vmem_limit_bytes=VMEM_LIMIT),
    )(dbias, ids)


def _swa_block(q, k3, v3, bias, sinks):
    heads = range(8)
    kh = [k3[:, (h // 4) * 128:(h // 4 + 1) * 128] for h in heads]
    vh = [v3[:, (h // 4) * 128:(h // 4 + 1) * 128] for h in heads]
    s = [_bmm_nt(q[:, h * 128:(h + 1) * 128], kh[h]) * 0.125 + bias[h] for h in heads]
    sink = [_lane_pick(sinks, h) for h in heads]
    m = [lax.stop_gradient(jnp.maximum(jnp.max(s[h], axis=-1, keepdims=True), sink[h])) for h in heads]
    e = [jnp.exp(s[h] - m[h]) for h in heads]
    p = [e[h] / (jnp.sum(e[h], axis=-1, keepdims=True) + jnp.exp(sink[h] - m[h])) for h in heads]
    return jnp.concatenate([_bmm(p[h], vh[h]) for h in heads], axis=1)


def _swa_in_specs():
    qs = pl.BlockSpec((BLK, 1024), lambda n: (n, E_QA // 1024))
    ks = [pl.BlockSpec((BLK, 256), lambda n: (0, E_KA // 256)),
          pl.BlockSpec((BLK, 256), lambda n: (jnp.maximum(n - 1, 0), E_KA // 256)),
          pl.BlockSpec((BLK, 256), lambda n: (n, E_KA // 256))]
    vs = [pl.BlockSpec((BLK, 256), lambda n: (0, E_VA // 256)),
          pl.BlockSpec((BLK, 256), lambda n: (jnp.maximum(n - 1, 0), E_VA // 256)),
          pl.BlockSpec((BLK, 256), lambda n: (n, E_VA // 256))]
    bs = pl.BlockSpec((None, 8, BLK, 3 * BLK), lambda n: (jnp.minimum(n, 2), 0, 0, 0))
    ss = pl.BlockSpec((1, LANE), lambda n: (0, 0))
    return [qs] + ks + vs + [bs, ss]


def swa_fwd(proj, bias, sinks, side=None):
    M = proj.shape[0]
    s_ins, s_specs, s_shapes, s_sems = _side_parts(side)
    ns, nso = len(s_ins), len(s_shapes)

    def body(q_ref, k0, k1, k2, v0, v1, v2, b_ref, s_ref, *rest):
        o_ref = rest[ns]
        _side_run(side, M // BLK, rest[:ns], rest[ns + 1:ns + 1 + nso], rest[ns + 1 + nso:])
        k3 = jnp.concatenate([k0[...], k1[...], k2[...]], axis=0)
        v3 = jnp.concatenate([v0[...], v1[...], v2[...]], axis=0)
        o_ref[...] = _swa_block(q_ref[...], k3, v3, b_ref[...], s_ref[...]).astype(o_ref.dtype)

    res = pl.pallas_call(
        body, name="swa_fwd", grid=(M // BLK,),
        in_specs=_swa_in_specs() + s_specs,
        out_specs=[pl.BlockSpec((BLK, 1024), lambda n: (n, 0))] + [pl.BlockSpec(memory_space=pl.ANY)] * nso,
        out_shape=[jax.ShapeDtypeStruct((M, 1024), BF16)] + s_shapes,
        scratch_shapes=s_sems,
        compiler_params=_cparams(("arbitrary",)),
    )(proj, proj, proj, proj, proj, proj, proj, bias, sinks, *s_ins)
    return res[0], res[1:]


def swa_bwd(proj, bias, sinks, do, side=None):
    M = proj.shape[0]
    s_ins, s_specs, s_shapes, s_sems = _side_parts(side)
    ns, nso = len(s_ins), len(s_shapes)

    def body(q_ref, k0, k1, k2, v0, v1, v2, b_ref, s_ref, do_ref, *rest):
        dq_ref, dk_ref, dv_ref, db_ref, ds_ref = rest[ns:ns + 5]
        _side_run(side, M // BLK, rest[:ns], rest[ns + 5:ns + 5 + nso], rest[ns + 5 + nso:])
        n = pl.program_id(0)

        @pl.when(n == 0)
        def _():
            dk_ref[...] = jnp.zeros_like(dk_ref)
            dv_ref[...] = jnp.zeros_like(dv_ref)
            ds_ref[...] = jnp.zeros_like(ds_ref)

        k3 = jnp.concatenate([k0[...], k1[...], k2[...]], axis=0)
        v3 = jnp.concatenate([v0[...], v1[...], v2[...]], axis=0)
        _, vjp = jax.vjp(_swa_block, q_ref[...], k3, v3, b_ref[...], s_ref[...])
        dq, dk3, dv3, dbias, dsink = vjp(do_ref[...].astype(F32))
        dq_ref[...] = dq
        prev = pl.multiple_of(jnp.maximum(n - 1, 0) * BLK, BLK)
        cur = pl.multiple_of(n * BLK, BLK)
        dk_ref[pl.ds(0, BLK), :] += dk3[0:BLK]
        dv_ref[pl.ds(0, BLK), :] += dv3[0:BLK]
        dk_ref[pl.ds(prev, BLK), :] += dk3[BLK:2 * BLK]
        dv_ref[pl.ds(prev, BLK), :] += dv3[BLK:2 * BLK]
        dk_ref[pl.ds(cur, BLK), :] += dk3[2 * BLK:]
        dv_ref[pl.ds(cur, BLK), :] += dv3[2 * BLK:]
        ds_ref[...] += dsink

        @pl.when(n <= 2)
        def _():
            db_ref[...] = dbias

        @pl.when(n > 2)
        def _():
            db_ref[...] += dbias

    res = pl.pallas_call(
        body, name="swa_bwd", grid=(M // BLK,),
        in_specs=_swa_in_specs() + [pl.BlockSpec((BLK, 1024), lambda n: (n, 0))] + s_specs,
        out_specs=[pl.BlockSpec((BLK, 1024), lambda n: (n, 0)),
                   pl.BlockSpec((M, 256), lambda n: (0, 0)), pl.BlockSpec((M, 256), lambda n: (0, 0)),
                   pl.BlockSpec((None, 8, BLK, 3 * BLK), lambda n: (jnp.minimum(n, 2), 0, 0, 0)),
                   pl.BlockSpec((1, LANE), lambda n: (0, 0))] + [pl.BlockSpec(memory_space=pl.ANY)] * nso,
        out_shape=[jax.ShapeDtypeStruct((M, 1024), F32), jax.ShapeDtypeStruct((M, 256), F32),
                   jax.ShapeDtypeStruct((M, 256), F32), jax.ShapeDtypeStruct((3, 8, BLK, 3 * BLK), F32),
                   jax.ShapeDtypeStruct((1, LANE), F32)] + s_shapes,
        scratch_shapes=s_sems,
        compiler_params=_cparams(("arbitrary",)),
    )(proj, proj, proj, proj, proj, proj, proj, bias, sinks, do, *s_ins)
    return res[:5], res[5:]


def _shift_rows_impl(x, k):
    n = x.shape[0]
    rolled = pltpu.roll(x, k, 0)
    return jnp.where(_row_ids(0, n) >= k, rolled, 0.0)


def _unshift_rows_impl(g, k):
    n = g.shape[0]
    rolled = pltpu.roll(g, n - k, 0)
    return jnp.where(_row_ids(0, n) < n - k, rolled, 0.0)


@functools.partial(jax.custom_vjp, nondiff_argnums=(1,))
def _shift_rows(x, k):
    return _shift_rows_impl(x, k)


def _shift_rows_f(x, k):
    return _shift_rows_impl(x, k), None


def _shift_rows_b(k, _, g):
    return (_unshift_rows_impl(g, k),)


_shift_rows.defvjp(_shift_rows_f, _shift_rows_b)


def _conv_silu(x, w):
    rid = lax.broadcasted_iota(jnp.int32, w.shape, 0)
    y = x * jnp.sum(jnp.where(rid == 3, w, 0.0), axis=0, keepdims=True)
    for k in range(1, 4):
        y = y + _shift_rows(x, k) * jnp.sum(jnp.where(rid == 3 - k, w, 0.0), axis=0, keepdims=True)
    y = jnp.where(_row_ids(0, x.shape[0]) >= ZROWS, y, 0.0)
    return _silu(y)


def conv_fwd(proj, conv_w):
    M = proj.shape[0]
    nb = conv_w.shape[1] // LANE

    def body(x_ref, w_ref, o_ref):
        o_ref[...] = _conv_silu(x_ref[...], w_ref[...])

    return pl.pallas_call(
        body, name="conv_fwd", grid=(nb,),
        in_specs=[pl.BlockSpec((M, LANE), lambda c: (0, E_QB // LANE + c)), pl.BlockSpec((4, LANE), lambda c: (0, c))],
        out_specs=pl.BlockSpec((M, LANE), lambda c: (0, c)),
        out_shape=jax.ShapeDtypeStruct((M, conv_w.shape[1]), F32),
        compiler_params=_cparams(("parallel",)),
    )(proj, conv_w)


def conv_bwd(proj, conv_w, dy):
    M = proj.shape[0]
    nb = conv_w.shape[1] // LANE

    def body(x_ref, w_ref, dy_ref, dx_ref, dw_ref):
        _, vjp = jax.vjp(_conv_silu, x_ref[...], w_ref[...])
        dx, dw = vjp(dy_ref[...])
        dx_ref[...] = dx
        dw_ref[...] = dw

    return pl.pallas_call(
        body, name="conv_bwd", grid=(nb,),
        in_specs=[pl.BlockSpec((M, LANE), lambda c: (0, E_QB // LANE + c)), pl.BlockSpec((4, LANE), lambda c: (0, c)),
                  pl.BlockSpec((M, LANE), lambda c: (0, c))],
        out_specs=[pl.BlockSpec((M, LANE), lambda c: (0, c)), pl.BlockSpec((4, LANE), lambda c: (0, c))],
        out_shape=[jax.ShapeDtypeStruct((M, conv_w.shape[1]), F32), jax.ShapeDtypeStruct(conv_w.shape, F32)],
        compiler_params=_cparams(("parallel",)),
    )(proj, conv_w, dy)


def _fn_dn_prep(row0, yq, yk, ba, dnp):
    tm = yq.shape[0]
    real = _row_ids(row0, tm) >= ZROWS
    qs, ks, gs, bs = [], [], [], []
    for h in range(4):
        q = yq[:, h * 128:(h + 1) * 128]
        k = yk[:, h * 128:(h + 1) * 128]
        qs.append(q * lax.rsqrt(jnp.sum(q * q, axis=-1, keepdims=True) + 1e-6) * (128.0 ** -0.5))
        ks.append(k * lax.rsqrt(jnp.sum(k * k, axis=-1, keepdims=True) + 1e-6))
        beta = _sigmoid(_lane_pick(ba, h))
        g = -jnp.exp(_lane_pick(dnp, h)) * _softplus(_lane_pick(ba, 4 + h) + _lane_pick(dnp, 4 + h))
        g = jnp.where(real, g, 0.0)
        gs.append(jnp.broadcast_to(g, (tm, 128)))
        bs.append(jnp.broadcast_to(beta, (tm, 128)))
    cat = lambda xs: jnp.concatenate(xs, axis=1)
    return cat(qs), cat(ks), cat(gs), cat(bs)


def _zip(f, *lists):
    return [f(*args) for args in zip(*lists)]


def _unit_lower_inv_impl(a):
    n = a[0].shape[0]
    eye = (lax.broadcasted_iota(jnp.int32, (n, n), 0) == lax.broadcasted_iota(jnp.int32, (n, n), 1)).astype(F32)
    nn = ((1,), (0,))
    p = [-x for x in a]
    t = [eye + x for x in p]
    for _ in range(int(math.log2(n)) - 1):
        p = _zip(lambda x: _xdot(x, x, nn), p)
        t = _zip(lambda x, y: x + _xdot(x, y, nn), t, p)
    return t


@jax.custom_vjp
def _unit_lower_inv(a):
    return _unit_lower_inv_impl(a)


def _unit_lower_inv_f(a):
    t = _unit_lower_inv_impl(a)
    return t, t


def _unit_lower_inv_b(t, g):
    tg = _zip(lambda x, y: _xdot(x, y, ((0,), (0,))), t, g)
    return (_zip(lambda x, y: -_xdot(x, y, ((1,), (1,))), tg, t),)


_unit_lower_inv.defvjp(_unit_lower_inv_f, _unit_lower_inv_b)


@jax.custom_vjp
def _known_inv(a, t):
    return t


_known_inv.defvjp(lambda a, t: (t, t), lambda t, g: (_unit_lower_inv_b(t, g)[0], [jnp.zeros_like(x) for x in t]))


def _dn_block(q, k, v, gb, bb, S, t_kept=None):
    nh = len(S)
    r = lax.broadcasted_iota(jnp.int32, (CH, CH), 0)
    c = lax.broadcasted_iota(jnp.int32, (CH, CH), 1)
    tri_incl = r >= c
    gcb = _zip(_cumsum_rows, gb)
    gamma = _zip(lambda x: jnp.where(tri_incl, jnp.exp(jnp.where(tri_incl, x[:, :CH] - x[:, :CH].T, 0.0)), 0.0), gcb)
    kb = _zip(jnp.multiply, k, bb)
    vb = _zip(jnp.multiply, v, bb)
    a = _zip(lambda m, g: jnp.where(r > c, m * g, 0.0), _zip(_bmm_nt, kb, k), gamma)
    t = _unit_lower_inv(a) if t_kept is None else _known_inv(a, list(t_kept))
    eg = _zip(jnp.exp, gcb)
    u = _zip(_xmm, t, vb)
    w = _zip(_xmm, t, _zip(jnp.multiply, kb, eg))
    attn = _zip(lambda m, g: m * g, _zip(_bmm_nt, q, k), gamma)
    gtot = _zip(lambda x: jnp.sum(x, axis=0, keepdims=True), gb)
    k_dec = _zip(lambda x, gt, gc: x * jnp.exp(gt - gc), k, gtot, gcb)
    q_dec = _zip(jnp.multiply, q, eg)
    S = list(S)
    o, starts = [], []
    for i0 in range(0, len(q), nh):
        idx = range(i0, i0 + nh)
        starts.append(list(S))
        v_new = [u[i] - m for i, m in zip(idx, [_bmm(w[i], S[h]) for h, i in enumerate(idx)])]
        oq = [_bmm(q_dec[i], S[h]) for h, i in enumerate(idx)]
        oa = [_bmm(attn[i], vn) for i, vn in zip(idx, v_new)]
        kv = [_bmm_tn(k_dec[i], vn) for i, vn in zip(idx, v_new)]
        o += _zip(jnp.add, oq, oa)
        S = [S[h] * jnp.exp(jnp.broadcast_to(gtot[i], S[h].shape)) + kv[h] for h, i in enumerate(idx)]
    return o, S, starts, t


def _gla_block(q, k, v, glog, S):
    nh = len(S)
    tri = lax.broadcasted_iota(jnp.int32, (CH, CH), 0) >= lax.broadcasted_iota(jnp.int32, (CH, CH), 1)
    bcum = _zip(_cumsum_rows, glog)
    q_dec = _zip(lambda x, b: x * (128.0 ** -0.5) * jnp.exp(b), q, bcum)
    attn = _zip(lambda m: jnp.where(tri, m, 0.0), _zip(_bmm_nt, q_dec, _zip(lambda x, b: x * jnp.exp(-b), k, bcum)))
    o_in = _zip(_bmm, attn, v)
    k_dec = _zip(lambda x, g, b: x * jnp.exp(jnp.sum(g, axis=0, keepdims=True) - b), k, glog, bcum)
    decay = _zip(lambda g, x: jnp.exp(_colsum_as_rows(g, x.shape[1])), glog, v)
    kv = _zip(_bmm_tn, k_dec, v)
    S = list(S)
    o, starts = [], []
    for i0 in range(0, len(q), nh):
        idx = range(i0, i0 + nh)
        starts.append(list(S))
        o += [o_in[i] + m for i, m in zip(idx, [_bmm(q_dec[i], S[h]) for h, i in enumerate(idx)])]
        S = [S[h] * decay[i] + kv[i] for h, i in enumerate(idx)]
    return o, S, starts


class Side:
    def __init__(self, ins, out_shapes, nsem, events):
        self.ins, self.out_shapes, self.nsem, self.events = list(ins), list(out_shapes), nsem, events


def _side_parts(side):
    if side is None:
        return [], [], [], []
    anyspec = pl.BlockSpec(memory_space=pl.ANY)
    return (side.ins, [anyspec] * len(side.ins), side.out_shapes,
            [pltpu.SemaphoreType.DMA((side.nsem,)), pltpu.SemaphoreType.DMA((side.nsem,))])


def _side_run(side, n_steps, in_refs, out_refs, sems, step=None):
    if side is None:
        return
    step = pl.program_id(0) if step is None else step
    for at, fn in side.events(n_steps, in_refs, out_refs, *sems):
        pl.when(step == at)(fn)


def _chunks_per_step(n_chunks):
    return next(c for c in (CPS, 2, 1) if n_chunks % c == 0)


def chunk_fwd(name, chunk_fn, ins, dv, side=None, keep=()):
    M = ins[0][0].shape[0]
    NC = M // CH
    cps = _chunks_per_step(NC)
    N = NC // cps
    ni, nk = len(ins), len(keep)
    ws = [w for (_, w, _) in ins]
    s_ins, s_specs, s_shapes, s_sems = _side_parts(side)
    ns, nso = len(s_ins), len(s_shapes)

    def body(*refs):
        o0 = ni + ns
        o_ref, sall_ref = refs[o0:o0 + 2]
        k_refs = refs[o0 + 2:o0 + 2 + nk]
        s_ref = refs[o0 + 2 + nk + nso]
        _side_run(side, N, refs[ni:o0], refs[o0 + 2 + nk:o0 + 2 + nk + nso], refs[o0 + 3 + nk + nso:])

        @pl.when(pl.program_id(0) == 0)
        def _():
            s_ref[...] = jnp.zeros_like(s_ref)

        problems = [(cc, h) for cc in range(cps) for h in range(4)]
        lists = [[r[cc * CH:(cc + 1) * CH, h * w:(h + 1) * w] for cc, h in problems] for r, w in zip(refs[:ni], ws)]
        o, s_new, starts, *kept = chunk_fn(*lists, [s_ref[h] for h in range(4)])
        for b, (cc, h) in enumerate(problems):
            o_ref[cc * CH:(cc + 1) * CH, h * dv:(h + 1) * dv] = o[b]
            sall_ref[h, cc] = starts[cc][h]
            for k_ref, vals in zip(k_refs, kept):
                k_ref[h, cc] = vals[b]
        for h in range(4):
            s_ref[h] = s_new[h]

    per_chunk = lambda r, c: pl.BlockSpec((4, cps, r, c), lambda n: (0, n, 0, 0))
    specs = [pl.BlockSpec((cps * CH, 4 * w), functools.partial(lambda n, cb: (n, cb), cb=cb // 4)) for (_, w, cb) in ins]
    res = pl.pallas_call(
        body, name=name, grid=(N,),
        in_specs=specs + s_specs,
        out_specs=[pl.BlockSpec((cps * CH, 4 * dv), lambda n: (n, 0)), per_chunk(128, dv)] + [per_chunk(r, c) for r, c in keep]
        + [pl.BlockSpec(memory_space=pl.ANY)] * nso,
        out_shape=[jax.ShapeDtypeStruct((M, 4 * dv), F32), jax.ShapeDtypeStruct((4, NC, 128, dv), F32)]
        + [jax.ShapeDtypeStruct((4, NC, r, c), F32) for r, c in keep] + s_shapes,
        scratch_shapes=[pltpu.VMEM((4, 128, dv), F32)] + s_sems,
        compiler_params=_cparams(("arbitrary",)),
    )(*[a for (a, _, _) in ins], *s_ins)
    return res[0], res[1], res[2:2 + nk], res[2 + nk:]


def chunk_bwd(name, chunk_fn, ins, dv, s_all, do, side=None, kept=()):
    M = ins[0][0].shape[0]
    cps = _chunks_per_step(M // CH)
    N = M // CH // cps
    ni, nk = len(ins), len(kept)
    ws = [w for (_, w, _) in ins]
    s_ins, s_specs, s_shapes, s_sems = _side_parts(side)
    ns, nso = len(s_ins), len(s_shapes)

    def body(*refs):
        sall_ref, do_ref = refs[ni:ni + 2]
        k_refs = refs[ni + 2:ni + 2 + nk]
        o0 = ni + 2 + nk + ns
        d_refs = refs[o0:o0 + ni]
        ds_ref = refs[o0 + ni + nso]
        _side_run(side, N, refs[ni + 2 + nk:o0], refs[o0 + ni:o0 + ni + nso], refs[o0 + ni + nso + 1:])

        @pl.when(pl.program_id(0) == 0)
        def _():
            ds_ref[...] = jnp.zeros_like(ds_ref)

        problems = [(cc, h) for cc in range(cps) for h in range(4)]
        lists = [[r[cc * CH:(cc + 1) * CH, h * w:(h + 1) * w] for cc, h in problems] for r, w in zip(refs[:ni], ws)]
        kept_lists = [[k_ref[h, cc] for cc, h in problems] for k_ref in k_refs]
        _, vjp = jax.vjp(lambda *a: tuple(chunk_fn(*a)[:2]), *lists, [sall_ref[h, 0] for h in range(4)], *kept_lists)
        grads = vjp(([do_ref[cc * CH:(cc + 1) * CH, h * dv:(h + 1) * dv] for cc, h in problems],
                     [ds_ref[h] for h in range(4)]))
        for d_ref, w, g in zip(d_refs, ws, grads[:ni]):
            for b, (cc, h) in enumerate(problems):
                d_ref[cc * CH:(cc + 1) * CH, h * w:(h + 1) * w] = g[b]
        for h in range(4):
            ds_ref[h] = grads[ni][h]

    rev = lambda n: N - 1 - n
    per_chunk = lambda r, c: pl.BlockSpec((4, cps, r, c), lambda n: (0, rev(n), 0, 0))
    specs = [pl.BlockSpec((cps * CH, 4 * w), functools.partial(lambda n, cb: (rev(n), cb), cb=cb // 4)) for (_, w, cb) in ins]
    res = pl.pallas_call(
        body, name=name, grid=(N,),
        in_specs=specs + [per_chunk(128, dv), pl.BlockSpec((cps * CH, 4 * dv), lambda n: (rev(n), 0))]
        + [per_chunk(*a.shape[2:]) for a in kept] + s_specs,
        out_specs=[pl.BlockSpec((cps * CH, 4 * w), lambda n: (rev(n), 0)) for w in ws] + [pl.BlockSpec(memory_space=pl.ANY)] * nso,
        out_shape=[jax.ShapeDtypeStruct((M, 4 * w), F32) for w in ws] + s_shapes,
        scratch_shapes=[pltpu.VMEM((4, 128, dv), F32)] + s_sems,
        compiler_params=_cparams(("arbitrary",)),
    )(*[a for (a, _, _) in ins], s_all, do, *kept, *s_ins)
    return res[:ni], res[ni:]


def _fn_gate_out(hd, row0, o, z, w):
    outs = []
    for h in range(4):
        outs.append(_rms(o[:, h * hd:(h + 1) * hd], w) * _silu(z[:, h * hd:(h + 1) * hd]))
    return (jnp.concatenate(outs, axis=1),)


def _fn_gla_prep(row0, gk, wgu, bg):
    x = _mm(gk, wgu) + bg
    ls = jnp.minimum(x, 0.0) - jnp.log(1.0 + jnp.exp(-jnp.abs(x)))
    return (jnp.where(_row_ids(row0, gk.shape[0]) >= ZROWS, ls / 16.0, 0.0),)


def loss_call(y, tgt):
    M = y.shape[0]
    tm = _pick(M, 512)

    def body(y_ref, t_ref, l_ref, dy_ref):
        i = pl.program_id(0)
        e = y_ref[...] - t_ref[...]
        dy_ref[...] = e * (1.0 / D)
        part = 0.5 * jnp.sum(jnp.sum(e * e, axis=1, keepdims=True) * (1.0 / D), axis=0, keepdims=True)
        part = jnp.broadcast_to(part, (8, LANE))

        @pl.when(i == 0)
        def _():
            l_ref[...] = part

        @pl.when(i > 0)
        def _():
            l_ref[...] += part

    return pl.pallas_call(
        body, name="loss", grid=(M // tm,),
        in_specs=[pl.BlockSpec((tm, D), lambda i: (i, 0))] * 2,
        out_specs=[pl.BlockSpec((8, LANE), lambda i: (0, 0)), pl.BlockSpec((tm, D), lambda i: (i, 0))],
        out_shape=[jax.ShapeDtypeStruct((8, LANE), F32), jax.ShapeDtypeStruct((M, D), F32)],
        compiler_params=_cparams(("arbitrary",)),
    )(y, tgt)


def _bf(x):
    return x.astype(BF16)


def core_step(x, tgt, W, comm=None):
    S = x.shape[0]
    M = S + PADR
    ids = jnp.asarray(_swa_bucket_ids())
    h0 = jnp.concatenate([jnp.zeros((ZROWS, D), F32), W["meta"], x], axis=0)
    nw = W["norm"]
    nrow = lambda l, k: nw[l, k][None, :]
    ffw = W["ffn"]
    projs = {}

    def projw(l):
        if l not in projs:
            projs[l] = W["proj"](l)
        return projs[l]

    site = (lambda name: comm.side(name)) if comm else (lambda name: None)
    landed = (lambda name, outs: comm.done(name, outs)) if comm else (lambda name, outs: None)
    sinks = jnp.pad(W["sinks"], ((0, 0), (0, LANE - 8)))
    dnp = jnp.pad(jnp.concatenate([W["a_log"], W["dt_bias"]], axis=1), ((0, 0), (0, LANE - 8)))
    wgu = jnp.pad(W["gate_up"], ((0, LANE - 16), (0, 0)))
    bg = W["b_gate"]
    full = lambda a: (a, a.shape[1], 0)

    saved = []
    h = h0
    (hn,) = rowwise_fwd("prenorm_0", _fn_prenorm, [full(h)], [nrow(0, 0)], [(D, BF16)])
    bias = swa_bias_fwd(W["rel"], ids)
    for l in range(2):
        st = {"h_a": h, "hn_a": hn}
        (f1, a1, b1), got = ffn_fwd(f"ffn_fwd_{l}0", hn, *ffw(l, 0), side=site(f"ffn_fwd_{l}0"))
        landed(f"ffn_fwd_{l}0", got)
        h, hn = rowwise_fwd(f"resnorm_{l}1", functools.partial(_fn_resnorm, 0.5), [full(h), full(f1)],
                            [nrow(l, 1), nrow(l, 2)], [(D, F32), (D, BF16)])
        st.update(f1=f1, a1=a1, b1=b1, h_b=h, hn_b=hn)
        if l == 0:
            proj = mm_nt(hn, projw(0)["w_in"], "e_proj")
            o_a, got = swa_fwd(proj, bias, sinks, side=site("swa_fwd"))
            landed("swa_fwd", got)
            y = conv_fwd(proj, W["conv"])
            qn, kn, gb, bb = rowwise_fwd(
                "dn_prep", _fn_dn_prep, [(y, 512, 0), (y, 512, 1), (proj, LANE, E_BA // LANE)], [dnp], [(512, F32)] * 4)
            ins = [(qn, 128, 0), (kn, 128, 0), (y, 128, 8), (gb, 128, 0), (bb, 128, 0)]
            o_dn, s_all, (t_inv,), got = chunk_fwd("dn_fwd", _dn_block, ins, 128, keep=[(CH, CH)],
                                                   side=site("dn_fwd"))
            landed("dn_fwd", got)
            (o_b,) = rowwise_fwd("dn_out", functools.partial(_fn_gate_out, 128),
                                 [full(o_dn), (proj, 512, E_ZB // 512)], [W["dn_norm"]], [(512, BF16)])
            omix = jnp.concatenate([o_a, o_b], axis=1)
            mix = mm_nn(omix, projw(0)["w_out"], "e_mix")
            st.update(proj=proj, y=y, qn=qn, kn=kn, gb=gb, bb=bb, o_dn=o_dn, s_all=s_all, t_inv=t_inv, omix=omix)
        else:
            proj = mm_nt(hn, projw(1)["w_in"], "o_proj")
            (glog,) = rowwise_fwd("gla_prep", _fn_gla_prep, [(proj, LANE, O_GK // LANE)], [wgu, bg], [(512, F32)])
            ins = [(proj, 128, O_Q // 128), (proj, 128, O_K // 128), (proj, 256, O_V // 256), (glog, 128, 0)]
            o_g, s_all, _, _ = chunk_fwd("gla_fwd", _gla_block, ins, 256)
            (omix,) = rowwise_fwd("gla_out", functools.partial(_fn_gate_out, 256),
                                  [full(o_g), (proj, 1024, O_G // 1024)], [W["gla_norm"]], [(1024, BF16)])
            mix = mm_nn(omix, projw(1)["w_out"], "o_mix")
            st.update(proj=proj, glog=glog, o_g=o_g, s_all=s_all, omix=omix)
        h, hn = rowwise_fwd(f"resnorm_{l}3", functools.partial(_fn_resnorm, 1.0), [full(h), full(mix)],
                            [nrow(l, 3), nrow(l, 4)], [(D, F32), (D, BF16)])
        st.update(mix=mix, h_c=h, hn_c=hn)
        (f2, a2, b2), got = ffn_fwd(f"ffn_fwd_{l}1", hn, *ffw(l, 1), side=site(f"ffn_fwd_{l}1"))
        landed(f"ffn_fwd_{l}1", got)
        st.update(f2=f2, a2=a2, b2=b2)
        if l == 0:
            h, hn = rowwise_fwd("resnorm_05", functools.partial(_fn_resnorm, 0.5), [full(h), full(f2)],
                                [nrow(0, 5), nrow(1, 0)], [(D, F32), (D, BF16)])
        else:
            (h,) = rowwise_fwd("res_last", functools.partial(_fn_res_last, 0.5), [full(h), full(f2)],
                               [nrow(1, 5)], [(D, F32)])
        saved.append(st)

    loss_blk, dy = loss_call(h[PADR:], tgt)
    dh = jnp.concatenate([jnp.zeros((PADR, D), F32), dy], axis=0)

    G = {}
    dnorm = [[None] * 6 for _ in range(2)]
    dWg = [[None, None], [None, None]]
    dWu = [[None, None], [None, None]]
    dWd = [[None, None], [None, None]]
    dhn = None
    for l in (1, 0):
        st = saved[l]
        if l == 1:
            (dh_, df), (dw5,) = rowwise_bwd(
                "res_last_b", functools.partial(_fn_res_last, 0.5), [full(st["h_c"]), full(st["f2"])], [nrow(1, 5)],
                [full(dh)], [F32, BF16])
            dnorm[1][5] = dw5
        else:
            (dh_, df), (dw5, dw0n) = rowwise_bwd(
                "resnorm_05_b", functools.partial(_fn_resnorm, 0.5), [full(st["h_c"]), full(st["f2"])],
                [nrow(0, 5), nrow(1, 0)], [full(dh), full(dhn)], [F32, BF16])
            dnorm[0][5] = dw5
            dnorm[1][0] = dw0n
        dh = dh_
        if comm and l == 0:
            units = (0, 1, 2, U_IN, U_OUT)
            comm.grads(1, units, layer_grad_items(1, dWg, dWu, dWd, G["o_in"], G["o_out"], only=units))
        (dxn, da, db, hm), got = ffn_bwd_x(f"ffn_bx_{l}1", df, st["a2"], st["b2"], *ffw(l, 1), side=site(f"ffn_bx_{l}1"))
        landed(f"ffn_bx_{l}1", got)
        dWg[l][1], dWu[l][1], dWd[l][1] = ffn_bwd_w(f"ffn_bw_{l}1", st["hn_c"], df, da, db, hm)
        if comm:
            comm.grads(l, (3, 4, 5), layer_grad_items(l, dWg, dWu, dWd, None, None, only=(3, 4, 5)))
        (dh_, dmix), (dw3, dw4) = rowwise_bwd(
            f"resnorm_{l}3_b", functools.partial(_fn_resnorm, 1.0), [full(st["h_b"]), full(st["mix"])],
            [nrow(l, 3), nrow(l, 4)], [full(dh), full(dxn)], [F32, BF16])
        dnorm[l][3], dnorm[l][4] = dw3, dw4
        dh = dh_
        proj = st["proj"]
        if l == 1:
            G["o_out"] = mm_tn(st["omix"], dmix, "o_out_dw")
            domix = mm_nt(dmix, projw(1)["w_out"], "o_mix_dx")
            (do_g, dgate), (dgn,) = rowwise_bwd(
                "gla_out_b", functools.partial(_fn_gate_out, 256), [full(st["o_g"]), (proj, 1024, O_G // 1024)],
                [W["gla_norm"]], [full(domix)], [F32, F32])
            G["gla_norm"] = dgn
            ins = [(proj, 128, O_Q // 128), (proj, 128, O_K // 128), (proj, 256, O_V // 256), (st["glog"], 128, 0)]
            (dq, dk, dv, dglog), _ = chunk_bwd("gla_bwd", _gla_block, ins, 256, st["s_all"], do_g)
            (dgk,), (dwgu, dbg) = rowwise_bwd("gla_prep_b", _fn_gla_prep, [(proj, LANE, O_GK // LANE)], [wgu, bg],
                                              [full(dglog)], [F32])
            G["gate_up"] = dwgu[:16]
            G["b_gate"] = dbg
            dproj = _bf(jnp.concatenate([dq, dk, dv, dgate, dgk, jnp.zeros((M, O_END - O_GK - LANE), F32)], axis=1))
            G["o_in"] = mm_tn(dproj, st["hn_b"], "o_in_dw")
            dhn_b = mm_nn(dproj, projw(1)["w_in"], "o_proj_dx")
        else:
            G["e_out"] = mm_tn(st["omix"], dmix, "e_out_dw")
            domix = mm_nt(dmix, projw(0)["w_out"], "e_mix_dx")
            (do_dn, dz), (ddn,) = rowwise_bwd(
                "dn_out_b", functools.partial(_fn_gate_out, 128), [full(st["o_dn"]), (proj, 512, E_ZB // 512)],
                [W["dn_norm"]], [(domix, 512, 2)], [F32, F32])
            G["dn_norm"] = ddn
            ins = [(st["qn"], 128, 0), (st["kn"], 128, 0), (st["y"], 128, 8), (st["gb"], 128, 0), (st["bb"], 128, 0)]
            (dqn, dkn, dvv, dgb, dbb), got = chunk_bwd("dn_bwd", _dn_block, ins, 128, st["s_all"], do_dn, side=site("dn_bwd"),
                                                        kept=[st["t_inv"]])
            landed("dn_bwd", got)
            (dyq, dyk, dba), (ddnp,) = rowwise_bwd(
                "dn_prep_b", _fn_dn_prep, [(st["y"], 512, 0), (st["y"], 512, 1), (proj, LANE, E_BA // LANE)], [dnp],
                [full(dqn), full(dkn), full(dgb), full(dbb)], [F32, F32, F32])
            G["a_log"] = ddnp[:, 0:4]
            G["dt_bias"] = ddnp[:, 4:8]
            dyc = jnp.concatenate([dyq, dyk, dvv], axis=1)
            dxc, dconv = conv_bwd(proj, W["conv"], dyc)
            G["conv"] = dconv
            (dq_a, dk_a, dv_a, dbias, dsink), got = swa_bwd(proj, bias, sinks, domix, side=site("swa_bwd"))
            landed("swa_bwd", got)
            G["sinks"] = dsink[:, :8]
            G["rel"] = swa_bias_bwd(dbias, ids)[:, :8]
            dproj = _bf(jnp.concatenate([dq_a, dk_a, dv_a, dxc, dz, dba, jnp.zeros((M, E_END - E_BA - LANE), F32)], axis=1))
            G["e_in"] = mm_tn(dproj, st["hn_b"], "e_in_dw")
            dhn_b = mm_nn(dproj, projw(0)["w_in"], "e_proj_dx")
        (dh_, df), (dw1, dw2) = rowwise_bwd(
            f"resnorm_{l}1_b", functools.partial(_fn_resnorm, 0.5), [full(st["h_a"]), full(st["f1"])],
            [nrow(l, 1), nrow(l, 2)], [full(dh), full(dhn_b)], [F32, BF16])
        dnorm[l][1], dnorm[l][2] = dw1, dw2
        dh = dh_
        if comm and l == 0:
            units = (U_IN, U_OUT)
            comm.grads(0, units, layer_grad_items(0, dWg, dWu, dWd, G["e_in"], G["e_out"], only=units))
        (dxn, da, db, hm), got = ffn_bwd_x(f"ffn_bx_{l}0", df, st["a1"], st["b1"], *ffw(l, 0), side=site(f"ffn_bx_{l}0"))
        landed(f"ffn_bx_{l}0", got)
        dWg[l][0], dWu[l][0], dWd[l][0] = ffn_bwd_w(f"ffn_bw_{l}0", st["hn_a"], df, da, db, hm)
        dhn = dxn
    (dh0p,), (dw00,) = rowwise_bwd("prenorm_0_b", _fn_prenorm, [full(saved[0]["h_a"])], [nrow(0, 0)], [full(dhn)], [F32])
    dnorm[0][0] = dw00
    dh = dh + dh0p
    G["meta"] = dh[ZROWS:PADR]
    G["norm"] = jnp.stack([jnp.concatenate(r, axis=0) for r in dnorm], axis=0)
    G["items"] = [layer_grad_items(0, dWg, dWu, dWd, G["e_in"], G["e_out"], only=(0, 1, 2) if comm else range(NUNITS)),
                  None if comm else layer_grad_items(1, dWg, dWu, dWd, G["o_in"], G["o_out"])]
    return loss_blk, dh[PADR:], G


NAMES = [("meta", "meta_tokens"), ("norm", "norm_w"), ("ffn_g", "ffn_w_gate"), ("ffn_u", "ffn_w_up"),
         ("ffn_d", "ffn_w_down"), ("rel", "rel_bias_table"), ("e_in", "even_w_in"), ("conv", "even_conv_w"),
         ("sinks", "swa_sinks"), ("a_log", "dn_a_log"), ("dt_bias", "dn_dt_bias"), ("dn_norm", "dn_norm_w"),
         ("e_out", "even_w_out"), ("o_in", "odd_w_in"), ("gate_up", "gla_w_gate_up"), ("b_gate", "gla_b_gate"),
         ("gla_norm", "gla_norm_w"), ("o_out", "odd_w_out")]
BIG = ["ffn_g", "ffn_u", "ffn_d", "e_in", "e_out", "o_in", "o_out"]
IN_ROWS = 800
SMALL = [("meta", (16, 256)), ("norm", (2, 6, 256)), ("conv", (1, 4, 384)), ("gate_up", (1, 16, 128)),
         ("b_gate", (1, 128)), ("gla_norm", (1, 64))]
REPL = [("rel", (32, 8)), ("sinks", (1, 8)), ("a_log", (1, 4)), ("dt_bias", (1, 4)), ("dn_norm", (1, 128))]
SMALL_REP = 88 * LANE
SMALL_ROWS = 96


def pack_small(t):
    a = jnp.concatenate([t[n].reshape(-1) for n, _ in SMALL])
    b = jnp.concatenate([t[n].reshape(-1) for n, _ in REPL])
    flat = jnp.concatenate([a, jnp.zeros((SMALL_REP - a.shape[0],), F32), b,
                            jnp.zeros((SMALL_ROWS * LANE - SMALL_REP - b.shape[0],), F32)])
    return flat.reshape(SMALL_ROWS, LANE)


def unpack_small(p):
    flat = p.reshape(-1)
    out, r = {}, 0
    for n, shp in SMALL:
        k = int(np.prod(shp))
        out[n] = flat[r:r + k].reshape(shp)
        r += k
    r = SMALL_REP
    for n, shp in REPL:
        k = int(np.prod(shp))
        out[n] = flat[r:r + k].reshape(shp)
        r += k
    return out


IN_SRC = (706, 772)


NUNITS = 8
U_IN, U_OUT = 6, 7


def _halves(a):
    return a.reshape(2, a.shape[0] // 2, D)


def weight_pieces(wt, l):
    inn = wt["e_in" if l == 0 else "o_in"]
    inn = jnp.pad(inn, ((0, IN_ROWS - inn.shape[0]), (0, 0)))
    ffn = [_halves(wt[n][l][j]) for j in range(2) for n in ("ffn_g", "ffn_u", "ffn_d")]
    return ffn + [_halves(inn), _halves(wt["e_out" if l == 0 else "o_out"])]


def ffn_weights(q, j):
    return tuple(q[3 * j + k].reshape(NSH, FSH, D) for k in range(3))


def proj_weights(l, q_in, q_out):
    m = _even_in_map() if l == 0 else _odd_in_map()
    src = np.where(m >= 0, (m // IN_SRC[l]) * IN_ROWS + m % IN_SRC[l], -1)
    w_out = q_out.reshape(NSH * 256, D)
    w_in = q_in.reshape(NSH * IN_ROWS, D)
    w_in = jnp.where(jnp.asarray(src >= 0)[:, None], jnp.take(w_in, jnp.asarray(np.maximum(src, 0)), axis=0), 0)
    return {"w_in": w_in.astype(q_in.dtype),
            "w_out": _rows(w_out, _even_out_map()) if l == 0 else w_out}


def layer_grad_items(l, dwg, dwu, dwd, g_in, g_out, only=range(NUNITS)):
    units = [[None] * NUNITS for _ in range(2)]

    def put(i, a):
        a = a.reshape(NSH, 2, a.shape[1] // 2, D)
        units[0][i], units[1][i] = a[:, 0], a[:, 1]

    for j in range(2):
        for k, t in enumerate((dwg, dwu, dwd)):
            if 3 * j + k in only:
                put(3 * j + k, t[l][j])
    if U_IN in only:
        m = _even_in_map() if l == 0 else _odd_in_map()
        gi = _rows(g_in, _inverse(m, NSH * IN_SRC[l])).reshape(NSH, IN_SRC[l], D)
        put(U_IN, _bf(jnp.pad(gi, ((0, 0), (0, IN_ROWS - IN_SRC[l]), (0, 0)))))
    if U_OUT in only:
        if l == 0:
            g_out = _rows(g_out, _inverse(_even_out_map(), 1024))
        put(U_OUT, _bf(g_out).reshape(NSH, 256, D))
    return units


def assemble_layer(l, r0, r1):
    whole = lambda i: jnp.concatenate([r0[i], r1[i]])
    return {"ffn_g": jnp.stack([whole(0), whole(3)]), "ffn_u": jnp.stack([whole(1), whole(4)]),
            "ffn_d": jnp.stack([whole(2), whole(5)]), "in": whole(U_IN)[:IN_SRC[l]], "out": whole(U_OUT)}


def big_grads(l0, l1):
    st = lambda n: jnp.stack([l0[n], l1[n]])
    return {"ffn_g": st("ffn_g"), "ffn_u": st("ffn_u"), "ffn_d": st("ffn_d"), "e_in": l0["in"], "e_out": l0["out"],
            "o_in": l1["in"], "o_out": l1["out"]}


def small_from_gathered(gs):
    sm = [unpack_small(gs[s]) for s in range(NSH)]
    full = {}
    full["meta"] = jnp.concatenate([sm[s]["meta"] for s in range(NSH)], axis=1)
    full["norm"] = jnp.concatenate([sm[s]["norm"] for s in range(NSH)], axis=2)
    full["conv"] = jnp.concatenate([sm[s]["conv"][0] for s in range(NSH)], axis=1)
    full["gate_up"] = jnp.concatenate([sm[s]["gate_up"][0] for s in range(NSH)], axis=1)
    full["b_gate"] = jnp.concatenate([sm[s]["b_gate"] for s in range(NSH)], axis=1)
    full["gla_norm"] = jnp.concatenate([sm[s]["gla_norm"] for s in range(NSH)], axis=1)
    return full


def _col_sh(w):
    return jnp.moveaxis(w.reshape(w.shape[0], NSH, w.shape[1] // NSH), 1, 0)


def _to_t(n, a):
    if n in ("ffn_g", "ffn_u"):
        return jnp.swapaxes(a, 2, 3)
    if n in ("e_in", "o_in"):
        return jnp.swapaxes(a[0], 0, 1)
    return a if n == "ffn_d" else a[0]


def _from_t(n, a):
    if n in ("ffn_g", "ffn_u"):
        return jnp.swapaxes(a, 2, 3)
    if n in ("e_in", "o_in"):
        return jnp.swapaxes(a, 0, 1)[None]
    return a if n == "ffn_d" else a[None]


def pack_small_grads(G):
    col_sh = _col_sh
    norm_sh = jnp.moveaxis(G["norm"].reshape(2, 6, NSH, 256), 2, 0)
    a = jnp.concatenate([col_sh(G["meta"]).reshape(NSH, -1), norm_sh.reshape(NSH, -1), col_sh(G["conv"]).reshape(NSH, -1),
                         col_sh(G["gate_up"]).reshape(NSH, -1), G["b_gate"].reshape(NSH, -1),
                         G["gla_norm"].reshape(NSH, -1)], axis=1)
    b = jnp.concatenate([G[n].reshape(-1) for n, _ in REPL])
    b = jnp.broadcast_to(b[None], (NSH, b.shape[0]))
    small = jnp.concatenate([a, jnp.zeros((NSH, SMALL_REP - a.shape[1]), F32), b,
                             jnp.zeros((NSH, SMALL_ROWS * LANE - SMALL_REP - b.shape[1]), F32)], axis=1)
    return small.reshape(NSH, SMALL_ROWS, LANE)


MESH = pl.DeviceIdType.MESH
ANY = pl.BlockSpec(memory_space=pl.ANY)
VMEM = pl.BlockSpec(memory_space=pltpu.VMEM)


def _place():
    return lax.axis_index("x"), lax.axis_index("y"), lax.axis_index("c")


def _other_chips(x, y):
    return [(1 - x, y), (x, 1 - y), (1 - x, 1 - y)]


def _rcopy(send_sems, recv_sems, k, src, dst, to):
    return pltpu.make_async_remote_copy(src_ref=src, dst_ref=dst, send_sem=send_sems.at[k], recv_sem=recv_sems.at[k],
                                        device_id=to, device_id_type=MESH)


def _gather_steps(in_refs, out_refs, send_sems, recv_sems):
    n = len(in_refs)
    x, y, c = _place()
    s = 2 * x + y
    chips = _other_chips(x, y)
    copy = functools.partial(_rcopy, send_sems, recv_sems)
    pairs = [(i, j, cx, cy) for i in range(n) for j, (cx, cy) in enumerate(chips)]
    pushes = lambda: [copy(i * 3 + j, in_refs[i].at[c], out_refs[i].at[s, c], (cx, cy, c)) for i, j, cx, cy in pairs]
    landed = lambda i, cx, cy, half: out_refs[i].at[2 * cx + cy, half]
    relays = lambda: [copy(3 * n + i * 3 + j, landed(i, cx, cy, c), landed(i, cx, cy, c), (x, y, 1 - c)) for i, j, cx, cy in pairs]

    def start():
        for cp in pushes():
            cp.start()

    def relay():
        for i, j, cx, cy in pairs:
            copy(i * 3 + j, landed(i, cx, cy, c), landed(i, cx, cy, c), (x, y, c)).wait_recv()
        for cp in relays():
            cp.start()

    def finish():
        for i, j, cx, cy in pairs:
            copy(3 * n + i * 3 + j, landed(i, cx, cy, 1 - c), landed(i, cx, cy, 1 - c), (x, y, c)).wait_recv()
        for cp in pushes() + relays():
            cp.wait_send()

    return start, relay, finish


def _gather_shapes(pieces):
    return [jax.ShapeDtypeStruct((NSH,) + a.shape, a.dtype) for a in pieces]


def ag_layer(name, pieces):
    n = len(pieces)

    def body(*refs):
        for fn in _gather_steps(refs[:n], refs[n:2 * n], *refs[2 * n:]):
            fn()

    return pl.pallas_call(
        body, name=name, in_specs=[ANY] * n, out_specs=[ANY] * n, out_shape=_gather_shapes(pieces),
        scratch_shapes=[pltpu.SemaphoreType.DMA((6 * n,)), pltpu.SemaphoreType.DMA((6 * n,))],
    )(*pieces)


def gather_side(pieces):
    def events(n_steps, in_refs, out_refs, send_sems, recv_sems):
        start, relay, finish = _gather_steps(in_refs, out_refs, send_sems, recv_sems)
        return [(0, start), (max(3 * n_steps // 4, 1), relay), (n_steps - 1, finish)]

    return Side(pieces, _gather_shapes(pieces), 6 * len(pieces), events)


def ag_small(pack):
    def body(x_ref, out_ref, send_sems, recv_sems):
        x, y, c = _place()
        s = 2 * x + y
        chips = _other_chips(x, y)

        def copy(k, src, dst, to):
            return pltpu.make_async_remote_copy(src_ref=src, dst_ref=dst, send_sem=send_sems.at[k], recv_sem=recv_sems.at[k],
                                                device_id=to, device_id_type=MESH)

        out_ref[s] = x_ref[...]
        sends = [copy(j, x_ref, out_ref.at[s], (cx, cy, c)) for j, (cx, cy) in enumerate(chips)]
        for cp in sends:
            cp.start()
        for j, (cx, cy) in enumerate(chips):
            blk = out_ref.at[2 * cx + cy]
            copy(j, blk, blk, (x, y, c)).wait_recv()
        for cp in sends:
            cp.wait_send()

    return pl.pallas_call(
        body, name="ag_small", in_specs=[VMEM], out_specs=VMEM,
        out_shape=jax.ShapeDtypeStruct((NSH,) + pack.shape, pack.dtype),
        scratch_shapes=[pltpu.SemaphoreType.DMA((3,)), pltpu.SemaphoreType.DMA((3,))],
    )(pack)


def rs_pair(name, items):
    ni = len(items[0])

    def body(*refs):
        in_refs = [refs[:ni], refs[ni:2 * ni]]
        recv_refs = refs[2 * ni:3 * ni]
        send_sems, recv_sems = refs[3 * ni:]
        x, y, c = _place()
        copy = functools.partial(_rcopy, send_sems, recv_sems)
        for cc in range(2):
            @pl.when(c == cc)
            def _():
                cps = [copy(i * NSH + s, in_refs[1 - cc][i].at[s], recv_refs[i].at[s], (x, y, 1 - c))
                       for i in range(ni) for s in range(NSH)]
                for cp in cps:
                    cp.start()
                for cp in cps:
                    cp.wait()

    return pl.pallas_call(
        body, name=name, in_specs=[ANY] * (2 * ni), out_specs=[ANY] * ni,
        out_shape=[jax.ShapeDtypeStruct(a.shape, a.dtype) for a in items[0]],
        scratch_shapes=[pltpu.SemaphoreType.DMA((ni * NSH,)), pltpu.SemaphoreType.DMA((ni * NSH,))],
    )(*items[0], *items[1])


def _scatter_steps(a_refs, out_refs, send_sems, recv_sems):
    n = len(a_refs)
    x, y, c = _place()
    s = 2 * x + y
    chips = _other_chips(x, y)
    copy = functools.partial(_rcopy, send_sems, recv_sems)
    pairs = [(i, j, cx, cy) for i in range(n) for j, (cx, cy) in enumerate(chips)]
    sends = lambda: [copy(i * 3 + j, a_refs[i].at[2 * cx + cy], out_refs[i].at[s], (cx, cy, c)) for i, j, cx, cy in pairs]

    def start():
        for cp in sends():
            cp.start()

    def finish():
        for i, j, cx, cy in pairs:
            blk = out_refs[i].at[2 * cx + cy]
            copy(i * 3 + j, blk, blk, (x, y, c)).wait_recv()
        for cp in sends():
            cp.wait_send()

    return start, finish


def rs_chips(name, arrs):
    n = len(arrs)

    def body(*refs):
        for fn in _scatter_steps(refs[:n], refs[n:2 * n], *refs[2 * n:]):
            fn()

    return pl.pallas_call(
        body, name=name, in_specs=[ANY] * n, out_specs=[ANY] * n,
        out_shape=[jax.ShapeDtypeStruct(a.shape, a.dtype) for a in arrs],
        scratch_shapes=[pltpu.SemaphoreType.DMA((3 * n,)), pltpu.SemaphoreType.DMA((3 * n,))],
    )(*arrs)


def scatter_side(arrs):
    def events(n_steps, in_refs, out_refs, send_sems, recv_sems):
        start, finish = _scatter_steps(in_refs, out_refs, send_sems, recv_sems)
        return [(0, start), (n_steps - 1, finish)]

    return Side(arrs, [jax.ShapeDtypeStruct(a.shape, a.dtype) for a in arrs], 3 * len(arrs), events)


def _pair_chunks(rows):
    return 4 if rows % 32 == 0 else (2 if rows % 16 == 0 else 1)


def ag_pair(name, arrs):
    n = len(arrs)
    chunks = [(i, k * (a.shape[0] // _pair_chunks(a.shape[0])), a.shape[0] // _pair_chunks(a.shape[0]))
              for i, a in enumerate(arrs) for k in range(_pair_chunks(a.shape[0]))]

    def body(*refs):
        g_refs, out_refs = refs[:n], refs[n:2 * n]
        send_sems, recv_sems = refs[2 * n:]
        x, y, c = _place()
        give = [_rcopy(send_sems, recv_sems, q, g_refs[i].at[pl.ds(r0, rc)], out_refs[i].at[pl.ds(r0, rc)], (x, y, 1 - c))
                for q, (i, r0, rc) in enumerate(chunks)]
        for cp in give:
            cp.start()
        for cp in give:
            cp.wait()

    return pl.pallas_call(
        body, name=name, in_specs=[ANY] * n, out_specs=[ANY] * n,
        out_shape=[jax.ShapeDtypeStruct(a.shape, a.dtype) for a in arrs],
        scratch_shapes=[pltpu.SemaphoreType.DMA((len(chunks),)), pltpu.SemaphoreType.DMA((len(chunks),))],
    )(*arrs)


def small_allreduce(p):
    def body(p_ref, out_ref, rbuf, send_sems, recv_sems):
        x, y, c = _place()
        me = 4 * x + 2 * y + c
        rbuf[me] = p_ref[2 * x + y]
        flip = lambda v, f: (1 - v) if f else v
        peers = [(flip(x, k >> 2 & 1), flip(y, k >> 1 & 1), flip(c, k & 1)) for k in range(1, 8)]

        def copy(k, src, dst, to):
            return pltpu.make_async_remote_copy(src_ref=src, dst_ref=dst, send_sem=send_sems.at[k], recv_sem=recv_sems.at[k],
                                                device_id=to, device_id_type=MESH)

        sends = [copy(k, p_ref.at[2 * px + py], rbuf.at[me], (px, py, pc)) for k, (px, py, pc) in enumerate(peers)]
        for cp in sends:
            cp.start()
        for k, (px, py, pc) in enumerate(peers):
            blk = rbuf.at[4 * px + 2 * py + pc]
            copy(k, blk, blk, (x, y, c)).wait_recv()
        for cp in sends:
            cp.wait_send()
        acc = rbuf[0]
        for d in range(1, 8):
            acc = acc + rbuf[d]
        out_ref[...] = acc

    return pl.pallas_call(
        body, name="small_allreduce", in_specs=[VMEM], out_specs=VMEM,
        out_shape=jax.ShapeDtypeStruct(p.shape[1:], F32),
        scratch_shapes=[pltpu.VMEM((8,) + p.shape[1:], F32), pltpu.SemaphoreType.DMA((7,)), pltpu.SemaphoreType.DMA((7,))],
    )(p)


def _rows_tile(rows, cap):
    return _pick(rows, cap) if rows % 128 == 0 else rows


def sum_pair(name, a0s, a1s, recvs, cflag):
    n = len(recvs)

    def body(c_ref, *refs):
        for i in range(n):
            a0_ref, a1_ref, b_ref, o_ref = refs[i], refs[n + i], refs[2 * n + i], refs[3 * n + i]
            own = jnp.where(c_ref[0] == 0, a0_ref[...].astype(F32), a1_ref[...].astype(F32))
            o_ref[...] = (own + b_ref[...].astype(F32)).astype(o_ref.dtype)

    specs = [pl.BlockSpec((None,) + a.shape[1:], lambda s: (s, 0, 0)) for a in recvs]
    return pl.pallas_call(
        body, name=name, grid=(NSH,), in_specs=[pl.BlockSpec(memory_space=pltpu.SMEM)] + specs * 3, out_specs=specs,
        out_shape=[jax.ShapeDtypeStruct(a.shape, BF16) for a in recvs], compiler_params=_cparams(("parallel",)),
    )(cflag, *a0s, *a1s, *recvs)


def sum_chips(name, parts, owns, sflag):
    n = len(parts)

    def body(s_ref, *refs):
        for i in range(n):
            p_ref, a_ref, o_ref = refs[i], refs[n + i], refs[2 * n + i]
            acc = None
            for t in range(NSH):
                term = jnp.where(s_ref[0] == t, a_ref[t].astype(F32), p_ref[t].astype(F32))
                acc = term if acc is None else acc + term
            o_ref[...] = acc

    specs = [pl.BlockSpec(a.shape, lambda i: (0, 0, 0)) for a in parts]
    return pl.pallas_call(
        body, name=name, grid=(1,), in_specs=[pl.BlockSpec(memory_space=pltpu.SMEM)] + specs * 2,
        out_specs=[pl.BlockSpec(a.shape[1:], lambda i: (0, 0)) for a in parts],
        out_shape=[jax.ShapeDtypeStruct(a.shape[1:], F32) for a in parts], compiler_params=_cparams(("arbitrary",)),
    )(sflag, *parts, *owns)


ADAM_LR, ADAM_B1, ADAM_B2, ADAM_EPS, ADAM_WD, ADAM_STEP = 0.001, 0.9, 0.999, 1e-08, 0.01, 10


def adamw_call(name, w, g, m, v):
    rows, cols = w.shape
    tr = _rows_tile(rows, 512)

    def body(w_ref, g_ref, m_ref, v_ref, d_ref, nm_ref, nv_ref):
        g_ = g_ref[...]
        m_ = ADAM_B1 * m_ref[...] + (1.0 - ADAM_B1) * g_
        v_ = ADAM_B2 * v_ref[...] + (1.0 - ADAM_B2) * (g_ * g_)
        m_hat = m_ / (1.0 - ADAM_B1 ** ADAM_STEP)
        v_hat = v_ / (1.0 - ADAM_B2 ** ADAM_STEP)
        d_ref[...] = -ADAM_LR * (m_hat / (jnp.sqrt(v_hat) + ADAM_EPS) + ADAM_WD * w_ref[...])
        nm_ref[...] = m_
        nv_ref[...] = v_

    spec = pl.BlockSpec((tr, cols), lambda i: (i, 0))
    sh = jax.ShapeDtypeStruct((rows, cols), F32)
    return pl.pallas_call(
        body, name=name, grid=(rows // tr,), in_specs=[spec] * 4, out_specs=[spec] * 3, out_shape=[sh] * 3,
        compiler_params=_cparams(("parallel",)),
    )(w, g, m, v)


def kernel(x, meta_tokens, norm_w, ffn_w_gate, ffn_w_up, ffn_w_down, rel_bias_table, even_w_in, even_conv_w, swa_sinks, dn_a_log, dn_dt_bias, dn_norm_w, even_w_out, odd_w_in, gla_w_gate_up, gla_b_gate, gla_norm_w, odd_w_out, loss_target, m_meta_tokens, m_norm_w, m_ffn_w_gate, m_ffn_w_up, m_ffn_w_down, m_rel_bias_table, m_even_w_in, m_even_conv_w, m_swa_sinks, m_dn_a_log, m_dn_dt_bias, m_dn_norm_w, m_even_w_out, m_odd_w_in, m_gla_w_gate_up, m_gla_b_gate, m_gla_norm_w, m_odd_w_out, v_meta_tokens, v_norm_w, v_ffn_w_gate, v_ffn_w_up, v_ffn_w_down, v_rel_bias_table, v_even_w_in, v_even_conv_w, v_swa_sinks, v_dn_a_log, v_dn_dt_bias, v_dn_norm_w, v_even_w_out, v_odd_w_in, v_gla_w_gate_up, v_gla_b_gate, v_gla_norm_w, v_odd_w_out):
    ws = [meta_tokens, norm_w, ffn_w_gate, ffn_w_up, ffn_w_down, rel_bias_table, even_w_in, even_conv_w, swa_sinks, dn_a_log,
          dn_dt_bias, dn_norm_w, even_w_out, odd_w_in, gla_w_gate_up, gla_b_gate, gla_norm_w, odd_w_out]
    ms = [m_meta_tokens, m_norm_w, m_ffn_w_gate, m_ffn_w_up, m_ffn_w_down, m_rel_bias_table, m_even_w_in, m_even_conv_w,
          m_swa_sinks, m_dn_a_log, m_dn_dt_bias, m_dn_norm_w, m_even_w_out, m_odd_w_in, m_gla_w_gate_up, m_gla_b_gate,
          m_gla_norm_w, m_odd_w_out]
    vs = [v_meta_tokens, v_norm_w, v_ffn_w_gate, v_ffn_w_up, v_ffn_w_down, v_rel_bias_table, v_even_w_in, v_even_conv_w,
          v_swa_sinks, v_dn_a_log, v_dn_dt_bias, v_dn_norm_w, v_even_w_out, v_odd_w_in, v_gla_w_gate_up, v_gla_b_gate,
          v_gla_norm_w, v_odd_w_out]
    short = [n for n, _ in NAMES]
    w = dict(zip(short, ws))
    m = dict(zip(short, ms))
    v = dict(zip(short, vs))

    wt = {n: _to_t(n, w[n]) for n in BIG}
    own = {n: wt[n].astype(BF16) for n in BIG}
    sflag = (2 * lax.axis_index("x") + lax.axis_index("y")).astype(jnp.int32).reshape(1)
    cflag = lax.axis_index("c").astype(jnp.int32).reshape(1)
    is0 = cflag[0] == 0
    fill = lambda got, pieces: [lax.dynamic_update_index_in_dim(g_, p_, sflag[0], 0) for g_, p_ in zip(got, pieces)]
    pieces = [weight_pieces(own, l) for l in range(2)]
    small = small_from_gathered(ag_small(pack_small(w)))
    def reduce_finish(l, mine, parts):
        red = [None] * NUNITS
        for grp in ((0, 1, 2), (3, 4, 5), (U_IN, U_OUT)):
            outs = sum_chips(f"sum_chips_{l}{grp[0]}", [parts[i] for i in grp], [mine[i] for i in grp], sflag)
            for i, o in zip(grp, outs):
                red[i] = o
        got = ag_pair(f"ag_pair_{l}", red)
        return [jnp.where(is0, r_, g_) for r_, g_ in zip(red, got)], [jnp.where(is0, g_, r_) for r_, g_ in zip(red, got)]

    class Exchanges:
        gathers = {"ffn_fwd_00": [(0, 3), (0, U_IN), (0, U_OUT)], "swa_fwd": [(0, 4)], "dn_fwd": [(0, 5), (1, 0), (1, 1)],
                   "ffn_fwd_01": [(1, 2), (1, U_IN), (1, U_OUT)], "ffn_fwd_10": [(1, 3), (1, 4), (1, 5)]}
        scatters = {"ffn_bx_10": [(1, 3), (1, 4), (1, 5)], "ffn_bx_01": [(1, 0), (1, 1), (1, 2)],
                    "dn_bwd": [(1, U_IN), (1, U_OUT), (0, 3)], "swa_bwd": [(0, 4), (0, 5)], "ffn_bx_00": [(0, U_IN), (0, U_OUT)]}

        def __init__(self):
            self.q = [[None] * NUNITS for _ in range(2)]
            self.mine = [[None] * NUNITS for _ in range(2)]
            self.parts = [[None] * NUNITS for _ in range(2)]

        def side(self, name):
            if name in self.gathers:
                return gather_side([pieces[l][i] for l, i in self.gathers[name]])
            if name in self.scatters:
                return scatter_side([self.mine[l][i] for l, i in self.scatters[name]])
            return None

        def done(self, name, outs):
            if name in self.gathers:
                units = self.gathers[name]
                for (l, i), a in zip(units, fill(outs, [pieces[l][i] for l, i in units])):
                    self.q[l][i] = a
            if name in self.scatters:
                for (l, i), a in zip(self.scatters[name], outs):
                    self.parts[l][i] = a

        def ffn(self, l, j):
            return ffn_weights(self.q[l], j)

        def proj(self, l):
            return proj_weights(l, self.q[l][U_IN], self.q[l][U_OUT])

        def grads(self, l, units, items):
            units = list(units)
            half0, half1 = [[items[c][i] for i in units] for c in range(2)]
            recv = rs_pair(f"rs_pair_{l}{units[0]}", [half0, half1])
            for i, a in zip(units, sum_pair(f"sum_pair_{l}{units[0]}", half0, half1, recv, cflag)):
                self.mine[l][i] = a

    ex = Exchanges()
    first = [0, 1, 2]
    for i, a in zip(first, fill(ag_layer("ag_layer_0", [pieces[0][i] for i in first]), [pieces[0][i] for i in first])):
        ex.q[0][i] = a
    W = {**small, **{n: w[n] for n, _ in REPL}, "ffn": ex.ffn, "proj": ex.proj}
    loss_blk, gx, G = core_step(x[0], loss_target[0], W, comm=ex)

    ex.grads(0, first, G["items"][0])
    for i, a in zip(first, rs_chips("rs_chips_0", [ex.mine[0][i] for i in first])):
        ex.parts[0][i] = a
    lay0 = assemble_layer(0, *reduce_finish(0, ex.mine[0], ex.parts[0]))
    lay1 = assemble_layer(1, *reduce_finish(1, ex.mine[1], ex.parts[1]))
    gt = big_grads(lay0, lay1)
    g_small_pack = small_allreduce(pack_small_grads(G))
    g = {**{n: _from_t(n, gt[n]) for n in BIG}, **unpack_small(g_small_pack)}

    delta, new_m, new_v = {}, {}, {}
    for n in BIG:
        shp = wt[n].shape
        two = lambda t: t.reshape(-1, D)
        d_, m_, v_ = adamw_call("adamw_" + n, two(wt[n]), two(gt[n]), two(_to_t(n, m[n])), two(_to_t(n, v[n])))
        delta[n], new_m[n], new_v[n] = (_from_t(n, t.reshape(shp)) for t in (d_, m_, v_))
    d_, m_, v_ = adamw_call("adamw_small", pack_small(w), g_small_pack, pack_small(m), pack_small(v))
    delta.update(unpack_small(d_))
    new_m.update(unpack_small(m_))
    new_v.update(unpack_small(v_))

    loss = lax.psum(loss_blk[0, 0], ("x", "y", "c"))
    return (loss, gx[None], *[g[n] for n in short], *[delta[n] for n in short], *[new_m[n] for n in short],
            *[new_v[n] for n in short])
```

```python
import functools
import math

import numpy as np
import jax
import jax.numpy as jnp
from jax import lax
from jax.experimental import pallas as pl
from jax.experimental.pallas import tpu as pltpu

F32 = jnp.float32
BF16 = jnp.bfloat16
HI = lax.Precision.HIGHEST

D = 1024
N_META = 16
PADR = 128
ZROWS = PADR - N_META
D_FF = 2816
NSH = 4
FSH = D_FF // NSH
EPS = 1e-6
NEG = -1e30
CH = 64
CPS = 3
BLK = 128
LANE = 128
VMEM_LIMIT = 56 * 1024 * 1024
FFN_SUB = 4

E_QA, E_KA, E_VA, E_QB, E_KB, E_VB, E_ZB, E_BA, E_END = 0, 1024, 1280, 1536, 2048, 2560, 3072, 3584, 4096


def _even_in_map():
    m = np.full((E_END,), -1, np.int64)
    for h in range(8):
        m[E_QA + h * 128:E_QA + h * 128 + 64] = np.arange(h * 64, (h + 1) * 64)
    for h in range(2):
        m[E_KA + h * 128:E_KA + h * 128 + 64] = 512 + np.arange(h * 64, (h + 1) * 64)
        m[E_VA + h * 128:E_VA + h * 128 + 64] = 640 + np.arange(h * 64, (h + 1) * 64)
    m[E_QB:E_QB + 2048] = 768 + np.arange(2048)
    m[E_BA:E_BA + 8] = 2816 + np.arange(8)
    return m


def _even_out_map():
    m = np.full((1536,), -1, np.int64)
    for h in range(8):
        m[h * 128:h * 128 + 64] = np.arange(h * 64, (h + 1) * 64)
    m[1024:1536] = 512 + np.arange(512)
    return m


O_Q, O_K, O_V, O_G, O_GK, O_END = 0, 512, 1024, 2048, 3072, 3584


def _odd_in_map():
    m = np.full((O_END,), -1, np.int64)
    m[:3072] = np.arange(3072)
    m[O_GK:O_GK + 16] = 3072 + np.arange(16)
    return m


def _inverse(m, n):
    inv = np.zeros((n,), np.int64)
    for p, o in enumerate(m):
        if o >= 0:
            inv[o] = p
    return inv


def _rows(w, m):
    parts, i, n = [], 0, len(m)
    while i < n:
        j = i + 1
        if m[i] < 0:
            while j < n and m[j] < 0:
                j += 1
            parts.append(jnp.zeros((j - i,) + w.shape[1:], w.dtype))
        else:
            while j < n and m[j] == m[j - 1] + 1:
                j += 1
            parts.append(lax.slice_in_dim(w, int(m[i]), int(m[i]) + j - i, axis=0))
        i = j
    return jnp.concatenate(parts, axis=0)


def _mm(a, b, prec=HI):
    return lax.dot_general(a, b, (((1,), (0,)), ((), ())), precision=prec, preferred_element_type=F32)


def _mm_nt(a, b, prec=HI):
    return lax.dot_general(a, b, (((1,), (1,)), ((), ())), precision=prec, preferred_element_type=F32)


def _mm_tn(a, b, prec=HI):
    return lax.dot_general(a, b, (((0,), (0,)), ((), ())), precision=prec, preferred_element_type=F32)


def _bdot(a, b, dims):
    return lax.dot_general(a.astype(BF16), b.astype(BF16), (dims, ((), ())), preferred_element_type=F32)


@jax.custom_vjp
def _bmm(a, b):
    return _bdot(a, b, ((1,), (0,)))


@jax.custom_vjp
def _bmm_nt(a, b):
    return _bdot(a, b, ((1,), (1,)))


@jax.custom_vjp
def _bmm_tn(a, b):
    return _bdot(a, b, ((0,), (0,)))


_bmm.defvjp(lambda a, b: (_bmm(a, b), (a, b)), lambda r, g: (_bmm_nt(g, r[1]), _bmm_tn(r[0], g)))
_bmm_nt.defvjp(lambda a, b: (_bmm_nt(a, b), (a, b)), lambda r, g: (_bmm(g, r[1]), _bmm_tn(g, r[0])))
_bmm_tn.defvjp(lambda a, b: (_bmm_tn(a, b), (a, b)), lambda r, g: (_bmm_nt(r[1], g), _bmm(r[0], g)))


def _hi_lo(x):
    h = x.astype(BF16)
    return h, (x - h.astype(F32)).astype(BF16)


def _xdot(a, b, dims):
    ah, al = _hi_lo(a)
    bh, bl = _hi_lo(b)
    d = lambda p, q: lax.dot_general(p, q, (dims, ((), ())), preferred_element_type=F32)
    return d(ah, bh) + (d(ah, bl) + d(al, bh))


@jax.custom_vjp
def _xmm(a, b):
    return _xdot(a, b, ((1,), (0,)))


@jax.custom_vjp
def _xmm_nt(a, b):
    return _xdot(a, b, ((1,), (1,)))


@jax.custom_vjp
def _xmm_tn(a, b):
    return _xdot(a, b, ((0,), (0,)))


_xmm.defvjp(lambda a, b: (_xmm(a, b), (a, b)), lambda r, g: (_xmm_nt(g, r[1]), _xmm_tn(r[0], g)))
_xmm_nt.defvjp(lambda a, b: (_xmm_nt(a, b), (a, b)), lambda r, g: (_xmm(g, r[1]), _xmm_tn(g, r[0])))
_xmm_tn.defvjp(lambda a, b: (_xmm_tn(a, b), (a, b)), lambda r, g: (_xmm_nt(r[1], g), _xmm(r[0], g)))


def _sum01(m01, x, dims):
    h, l = _hi_lo(x)
    l2 = (x - h.astype(F32) - l.astype(F32)).astype(BF16)
    m = m01.astype(BF16)
    d = lambda q: lax.dot_general(m, q, (dims, ((), ())), preferred_element_type=F32)
    return d(h) + (d(l) + d(l2))


@jax.custom_vjp
def _cumsum_rows(x):
    n = x.shape[0]
    tri = lax.broadcasted_iota(jnp.int32, (n, n), 0) >= lax.broadcasted_iota(jnp.int32, (n, n), 1)
    return _sum01(tri, x, ((1,), (0,)))


def _cumsum_rows_b(_, g):
    n = g.shape[0]
    tri = lax.broadcasted_iota(jnp.int32, (n, n), 0) >= lax.broadcasted_iota(jnp.int32, (n, n), 1)
    return (_sum01(tri, g, ((0,), (0,))),)


_cumsum_rows.defvjp(lambda x: (_cumsum_rows(x), None), _cumsum_rows_b)


@functools.partial(jax.custom_vjp, nondiff_argnums=(1,))
def _colsum_as_rows(x, width):
    return _colsum_impl(x, width)


def _colsum_impl(x, width):
    h, l = _hi_lo(x)
    l2 = (x - h.astype(F32) - l.astype(F32)).astype(BF16)
    ones = jnp.ones((x.shape[0], width), BF16)
    d = lambda q: lax.dot_general(q, ones, (((0,), (0,)), ((), ())), preferred_element_type=F32)
    return d(h) + (d(l) + d(l2))


def _colsum_as_rows_f(x, width):
    return _colsum_impl(x, width), x.shape[0]


def _colsum_as_rows_b(width, n, g):
    return (_sum01(jnp.ones((n, width), F32), g, ((1,), (1,))),)


_colsum_as_rows.defvjp(_colsum_as_rows_f, _colsum_as_rows_b)


def _rms(x, w):
    return x * lax.rsqrt(jnp.mean(x * x, axis=-1, keepdims=True) + EPS) * w


def _sigmoid(x):
    return 1.0 / (1.0 + jnp.exp(-x))


def _silu(x):
    return x * _sigmoid(x)


def _softplus(x):
    return jnp.maximum(x, 0.0) + jnp.log(1.0 + jnp.exp(-jnp.abs(x)))


def _lane_pick(row, idx):
    lane = lax.broadcasted_iota(jnp.int32, row.shape, row.ndim - 1)
    return jnp.sum(jnp.where(lane == idx, row, 0.0), axis=-1, keepdims=True)


def _row_ids(row0, n):
    return row0 + lax.broadcasted_iota(jnp.int32, (n, 1), 0)


def _pick(m, cap):
    best = 64
    for t in range(64, min(m, cap) + 1, 64):
        if m % t == 0:
            best = t
    return best


def _cparams(sem):
    return pltpu.CompilerParams(dimension_semantics=sem, vmem_limit_bytes=VMEM_LIMIT)


def mm_nn(a, b, name, out_dtype=F32):
    M, K = a.shape
    N = b.shape[1]
    tm = _pick(M, 1408 if K <= 2048 else 704)
    tn = _pick(N, 512)

    def body(a_ref, b_ref, o_ref):
        o_ref[...] = _mm(a_ref[...], b_ref[...], None).astype(o_ref.dtype)

    return pl.pallas_call(
        body, name=name, grid=(N // tn, M // tm),
        in_specs=[pl.BlockSpec((tm, K), lambda j, i: (i, 0)), pl.BlockSpec((K, tn), lambda j, i: (0, j))],
        out_specs=pl.BlockSpec((tm, tn), lambda j, i: (i, j)),
        out_shape=jax.ShapeDtypeStruct((M, N), out_dtype),
        compiler_params=_cparams(("parallel", "parallel")),
    )(a, b)


def mm_nt(a, b, name, out_dtype=F32):
    M, K = a.shape
    N = b.shape[0]
    tm = _pick(M, 1408)
    tn = _pick(N, 512 if K > 2048 else 1024)

    def body(a_ref, b_ref, o_ref):
        o_ref[...] = _mm_nt(a_ref[...], b_ref[...], None).astype(o_ref.dtype)

    return pl.pallas_call(
        body, name=name, grid=(N // tn, M // tm),
        in_specs=[pl.BlockSpec((tm, K), lambda j, i: (i, 0)), pl.BlockSpec((tn, K), lambda j, i: (j, 0))],
        out_specs=pl.BlockSpec((tm, tn), lambda j, i: (i, j)),
        out_shape=jax.ShapeDtypeStruct((M, N), out_dtype),
        compiler_params=_cparams(("parallel", "parallel")),
    )(a, b)


def mm_tn(a, b, name):
    M, K = a.shape
    N = b.shape[1]
    tk = _pick(K, 512)
    tn = _pick(N, 512)

    def body(a_ref, b_ref, o_ref):
        o_ref[...] = _mm_tn(a_ref[...], b_ref[...], None)

    return pl.pallas_call(
        body, name=name, grid=(K // tk, N // tn),
        in_specs=[pl.BlockSpec((M, tk), lambda i, j: (0, i)), pl.BlockSpec((M, tn), lambda i, j: (0, j))],
        out_specs=pl.BlockSpec((tk, tn), lambda i, j: (i, j)),
        out_shape=jax.ShapeDtypeStruct((K, N), F32),
        compiler_params=_cparams(("parallel", "parallel")),
    )(a, b)


def _row_specs(rows, tm):
    return [pl.BlockSpec((tm, w), functools.partial(lambda i, cb: (i, cb), cb=cb)) for (_, w, cb) in rows]


def _param_specs(params):
    return [pl.BlockSpec(p.shape, functools.partial(lambda i, nd: (0,) * nd, nd=p.ndim)) for p in params]


def rowwise_fwd(name, fn, rows, params, outs, tm=None):
    M = rows[0][0].shape[0]
    tm = tm or _pick(M, 704)
    nr, npar = len(rows), len(params)

    def body(*refs):
        row0 = pl.program_id(0) * tm
        vals = [r[...].astype(F32) for r in refs[:nr]] + [p[...] for p in refs[nr:nr + npar]]
        res = fn(row0, *vals)
        for o_ref, r in zip(refs[nr + npar:], res):
            o_ref[...] = r.astype(o_ref.dtype)

    return pl.pallas_call(
        body, name=name, grid=(M // tm,),
        in_specs=_row_specs(rows, tm) + _param_specs(params),
        out_specs=[pl.BlockSpec((tm, w), lambda i: (i, 0)) for (w, _) in outs],
        out_shape=[jax.ShapeDtypeStruct((M, w), dt) for (w, dt) in outs],
        compiler_params=_cparams(("parallel",)),
    )(*[r[0] for r in rows], *params)


def rowwise_bwd(name, fn, rows, params, douts, drow_dtypes, tm=None):
    M = rows[0][0].shape[0]
    tm = tm or _pick(M, 704)
    nr, npar, nd = len(rows), len(params), len(douts)
    want = [k for k, dt in enumerate(drow_dtypes) if dt is not None]

    def body(*refs):
        i = pl.program_id(0)
        row0 = i * tm
        vals = [r[...].astype(F32) for r in refs[:nr]] + [p[...] for p in refs[nr:nr + npar]]
        cots = tuple(d[...].astype(F32) for d in refs[nr + npar:nr + npar + nd])
        _, vjp = jax.vjp(functools.partial(fn, row0), *vals)
        grads = vjp(cots)
        o_refs = refs[nr + npar + nd:]
        for o_ref, k in zip(o_refs[:len(want)], want):
            o_ref[...] = grads[k].astype(o_ref.dtype)
        for o_ref, g in zip(o_refs[len(want):], grads[nr:]):
            @pl.when(i == 0)
            def _():
                o_ref[...] = g

            @pl.when(i > 0)
            def _():
                o_ref[...] += g

    res = pl.pallas_call(
        body, name=name, grid=(M // tm,),
        in_specs=_row_specs(rows, tm) + _param_specs(params) + _row_specs(douts, tm),
        out_specs=[pl.BlockSpec((tm, rows[k][1]), lambda i: (i, 0)) for k in want] + _param_specs(params),
        out_shape=[jax.ShapeDtypeStruct((M, rows[k][1]), drow_dtypes[k]) for k in want]
        + [jax.ShapeDtypeStruct(p.shape, F32) for p in params],
        compiler_params=_cparams(("arbitrary",)),
    )(*[r[0] for r in rows], *params, *[d[0] for d in douts])
    return res[:len(want)], res[len(want):]


def _fn_prenorm(row0, h, wpre):
    return (_rms(h, wpre),)


def _fn_resnorm(scale, row0, h, f, wpost, wpre):
    h2 = h + scale * _rms(f, wpost)
    return h2, _rms(h2, wpre)


def _fn_res_last(scale, row0, h, f, wpost):
    return (h + scale * _rms(f, wpost),)


def ffn_fwd(name, xn, wg, wu, wd, side=None):
    M = xn.shape[0]
    tm = _pick(M, 704)
    s_ins, s_specs, s_shapes, s_sems = _side_parts(side)
    ns, nso = len(s_ins), len(s_shapes)

    def body(x_ref, wg_ref, wu_ref, wd_ref, *rest):
        f_ref, a_ref, b_ref = rest[ns:ns + 3]
        s = pl.program_id(1)
        _side_run(side, (M // tm) * NSH, rest[:ns], rest[ns + 3:ns + 3 + nso], rest[ns + 3 + nso:],
                  step=pl.program_id(0) * NSH + s)
        x = x_ref[...]
        a = _mm_nt(x, wg_ref[...], None)
        b = _mm_nt(x, wu_ref[...], None)
        c = _mm((_silu(a) * b).astype(BF16), wd_ref[...], None)

        @pl.when(s == 0)
        def _():
            f_ref[...] = c

        @pl.when(s > 0)
        def _():
            f_ref[...] += c

        a_ref[...] = a.astype(BF16)
        b_ref[...] = b.astype(BF16)

    wspec = wdspec = pl.BlockSpec((None, FSH, D), lambda i, s: (s, 0, 0))
    abspec = pl.BlockSpec((None, tm, FSH), lambda i, s: (s, i, 0))
    res = pl.pallas_call(
        body, name=name, grid=(M // tm, NSH),
        in_specs=[pl.BlockSpec((tm, D), lambda i, s: (i, 0)), wspec, wspec, wdspec] + s_specs,
        out_specs=[pl.BlockSpec((tm, D), lambda i, s: (i, 0)), abspec, abspec] + [pl.BlockSpec(memory_space=pl.ANY)] * nso,
        out_shape=[jax.ShapeDtypeStruct((M, D), F32), jax.ShapeDtypeStruct((NSH, M, FSH), BF16),
                   jax.ShapeDtypeStruct((NSH, M, FSH), BF16)] + s_shapes,
        scratch_shapes=s_sems,
        compiler_params=_cparams(("arbitrary", "arbitrary")),
    )(xn, wg, wu, wd, *s_ins)
    return res[:3], res[3:]


def ffn_bwd_x(name, df, a, b, wg, wu, wd, side=None):
    M = df.shape[0]
    tm = _pick(M, 704)
    ts = tm // FFN_SUB
    s_ins, s_specs, s_shapes, s_sems = _side_parts(side)
    ns, nso = len(s_ins), len(s_shapes)

    def body(df_ref, a_ref, b_ref, wg_ref, wu_ref, wd_ref, *rest):
        dx_ref, da_ref, db_ref, hm_ref = rest[ns:ns + 4]
        _side_run(side, (M // tm) * NSH, rest[:ns], rest[ns + 4:ns + 4 + nso], rest[ns + 4 + nso:],
                  step=pl.program_id(0) * NSH + pl.program_id(1))

        @pl.when(pl.program_id(1) == 0)
        def _():
            dx_ref[...] = jnp.zeros_like(dx_ref)

        for r in range(FFN_SUB):
            rows = pl.ds(r * ts, ts)
            a_ = a_ref[rows, :].astype(F32)
            b_ = b_ref[rows, :].astype(F32)
            dh = _mm_nt(df_ref[rows, :], wd_ref[...], None)
            sig = _sigmoid(a_)
            sil = a_ * sig
            da = (dh * b_ * (sig * (1.0 + a_ * (1.0 - sig)))).astype(BF16)
            db = (dh * sil).astype(BF16)
            dx_ref[rows, :] += _mm(da, wg_ref[...], None) + _mm(db, wu_ref[...], None)
            da_ref[rows, :] = da
            db_ref[rows, :] = db
            hm_ref[rows, :] = (sil * b_).astype(BF16)

    wspec = wdspec = pl.BlockSpec((None, FSH, D), lambda i, s: (s, 0, 0))
    abspec = pl.BlockSpec((None, tm, FSH), lambda i, s: (s, i, 0))
    ab = jax.ShapeDtypeStruct((NSH, M, FSH), BF16)
    res = pl.pallas_call(
        body, name=name, grid=(M // tm, NSH),
        in_specs=[pl.BlockSpec((tm, D), lambda i, s: (i, 0)), abspec, abspec, wspec, wspec, wdspec] + s_specs,
        out_specs=[pl.BlockSpec((tm, D), lambda i, s: (i, 0)), abspec, abspec, abspec] + [pl.BlockSpec(memory_space=pl.ANY)] * nso,
        out_shape=[jax.ShapeDtypeStruct((M, D), F32), ab, ab, ab] + s_shapes,
        scratch_shapes=s_sems,
        compiler_params=_cparams(("arbitrary", "arbitrary")),
    )(df, a, b, wg, wu, wd, *s_ins)
    return res[:4], res[4:]


def ffn_bwd_w(name, xn, df, da, db, hm):
    M = xn.shape[0]
    tm = _pick(M, 1408)
    nt = M // tm

    def body(x_ref, df_ref, da_ref, db_ref, hm_ref, dwg_ref, dwu_ref, dwd_ref, ag, au, ad):
        i = pl.program_id(1)
        x = x_ref[...]
        g = _mm_tn(da_ref[...], x, None)
        u = _mm_tn(db_ref[...], x, None)
        d = _mm_tn(hm_ref[...], df_ref[...], None)

        @pl.when(i == 0)
        def _():
            ag[...] = g
            au[...] = u
            ad[...] = d

        @pl.when(i > 0)
        def _():
            ag[...] += g
            au[...] += u
            ad[...] += d

        @pl.when(i == nt - 1)
        def _():
            dwg_ref[...] = ag[...].astype(BF16)
            dwu_ref[...] = au[...].astype(BF16)
            dwd_ref[...] = ad[...].astype(BF16)

    xspec = pl.BlockSpec((tm, D), lambda s, i: (i, 0))
    abspec = pl.BlockSpec((None, tm, FSH), lambda s, i: (s, i, 0))
    return pl.pallas_call(
        body, name=name, grid=(NSH, nt),
        in_specs=[xspec, xspec, abspec, abspec, abspec],
        out_specs=[pl.BlockSpec((None, FSH, D), lambda s, i: (s, 0, 0))] * 3,
        out_shape=[jax.ShapeDtypeStruct((NSH, FSH, D), BF16)] * 3,
        scratch_shapes=[pltpu.VMEM((FSH, D), F32)] * 3,
        compiler_params=_cparams(("parallel", "arbitrary")),
    )(xn, df, da, db, hm)


def _t5_bucket_np(rel):
    n = np.maximum(rel, 0)
    n_f = np.maximum(n, 1).astype(np.float32)
    large = 16 + (np.log(n_f / np.float32(16)) / np.float32(math.log(8.0)) * np.float32(16)).astype(np.int32)
    large = np.minimum(large, 31)
    return np.where(n < 16, n, large).astype(np.int32)


def _swa_bucket_ids():
    qi = np.arange(BLK)[:, None]
    kj = np.arange(BLK)[None, :]
    out = np.full((3, BLK, 3 * BLK), -1, np.int32)
    for v in range(3):
        pos_q = v * BLK + qi - ZROWS
        rel_m = pos_q - (kj - ZROWS)
        ok_m = (kj >= ZROWS) & (rel_m >= 0) & (pos_q >= 0)
        out[v, :, 0:BLK] = np.where(ok_m, _t5_bucket_np(rel_m), -1)
        pos_kp = (v - 1) * BLK + kj - ZROWS
        rel_p = BLK + qi - kj
        ok_p = (pos_kp >= N_META) & (rel_p >= 0) & (rel_p < BLK) & np.full_like(ok_m, v >= 1)
        out[v, :, BLK:2 * BLK] = np.where(ok_p, _t5_bucket_np(rel_p), -1)
        pos_kc = v * BLK + kj - ZROWS
        rel_c = qi - kj
        ok_c = (pos_kc >= N_META) & (rel_c >= 0) & (rel_c < BLK)
        out[v, :, 2 * BLK:] = np.where(ok_c, _t5_bucket_np(rel_c), -1)
    return out


def swa_bias_fwd(table, ids):
    def body(t_ref, id_ref, o_ref):
        for v in range(3):
            for h in range(8):
                o_ref[v, h] = jnp.where(id_ref[v] < 0, NEG, 0.0)

            def step(b, carry):
                hit = id_ref[v] == b
                for h in range(8):
                    o_ref[v, h] += jnp.where(hit, t_ref[b, h], 0.0)
                return carry

            lax.fori_loop(0, 32, step, 0)

    return pl.pallas_call(
        body, name="swa_bias_fwd",
        in_specs=[pl.BlockSpec(memory_space=pltpu.SMEM), pl.BlockSpec(memory_space=pltpu.VMEM)],
        out_specs=pl.BlockSpec(memory_space=pltpu.VMEM),
        out_shape=jax.ShapeDtypeStruct((3, 8, BLK, 3 * BLK), F32),
        compiler_params=pltpu.CompilerParams(vmem_limit_bytes=VMEM_LIMIT),
    )(table, ids)


def swa_bias_bwd(dbias, ids):
    def body(d_ref, id_ref, o_ref):
        r = lax.broadcasted_iota(jnp.int32, (32, LANE), 0)
        c = lax.broadcasted_iota(jnp.int32, (32, LANE), 1)

        def step(b, acc):
            for v in range(3):
                hit = id_ref[v] == b
                for h in range(8):
                    m = jnp.where(hit, d_ref[v, h], 0.0)
                    s = jnp.sum(jnp.sum(m, axis=1, keepdims=True), axis=0, keepdims=True)
                    acc = acc + jnp.where((r == b) & (c == h), s, 0.0)
            return acc

        o_ref[...] = lax.fori_loop(0, 32, step, jnp.zeros((32, LANE), F32))

    return pl.pallas_call(
        body, name="swa_bias_bwd",
        in_specs=[pl.BlockSpec(memory_space=pltpu.VMEM), pl.BlockSpec(memory_space=pltpu.VMEM)],
        out_specs=pl.BlockSpec(memory_space=pltpu.VMEM),
        out_shape=jax.ShapeDtypeStruct((32, LANE), F32),
        compiler_params=pltpu.CompilerParams(vmem_limit_bytes=VMEM_LIMIT),
    )(dbias, ids)


def _swa_block(q, k3, v3, bias, sinks):
    heads = range(8)
    kh = [k3[:, (h // 4) * 128:(h // 4 + 1) * 128] for h in heads]
    vh = [v3[:, (h // 4) * 128:(h // 4 + 1) * 128] for h in heads]
    s = [_bmm_nt(q[:, h * 128:(h + 1) * 128], kh[h]) * 0.125 + bias[h] for h in heads]
    sink = [_lane_pick(sinks, h) for h in heads]
    m = [lax.stop_gradient(jnp.maximum(jnp.max(s[h], axis=-1, keepdims=True), sink[h])) for h in heads]
    e = [jnp.exp(s[h] - m[h]) for h in heads]
    p = [e[h] / (jnp.sum(e[h], axis=-1, keepdims=True) + jnp.exp(sink[h] - m[h])) for h in heads]
    return jnp.concatenate([_bmm(p[h], vh[h]) for h in heads], axis=1)


def _swa_in_specs():
    qs = pl.BlockSpec((BLK, 1024), lambda n: (n, E_QA // 1024))
    ks = [pl.BlockSpec((BLK, 256), lambda n: (0, E_KA // 256)),
          pl.BlockSpec((BLK, 256), lambda n: (jnp.maximum(n - 1, 0), E_KA // 256)),
          pl.BlockSpec((BLK, 256), lambda n: (n, E_KA // 256))]
    vs = [pl.BlockSpec((BLK, 256), lambda n: (0, E_VA // 256)),
          pl.BlockSpec((BLK, 256), lambda n: (jnp.maximum(n - 1, 0), E_VA // 256)),
          pl.BlockSpec((BLK, 256), lambda n: (n, E_VA // 256))]
    bs = pl.BlockSpec((None, 8, BLK, 3 * BLK), lambda n: (jnp.minimum(n, 2), 0, 0, 0))
    ss = pl.BlockSpec((1, LANE), lambda n: (0, 0))
    return [qs] + ks + vs + [bs, ss]


def swa_fwd(proj, bias, sinks, side=None):
    M = proj.shape[0]
    s_ins, s_specs, s_shapes, s_sems = _side_parts(side)
    ns, nso = len(s_ins), len(s_shapes)

    def body(q_ref, k0, k1, k2, v0, v1, v2, b_ref, s_ref, *rest):
        o_ref = rest[ns]
        _side_run(side, M // BLK, rest[:ns], rest[ns + 1:ns + 1 + nso], rest[ns + 1 + nso:])
        k3 = jnp.concatenate([k0[...], k1[...], k2[...]], axis=0)
        v3 = jnp.concatenate([v0[...], v1[...], v2[...]], axis=0)
        o_ref[...] = _swa_block(q_ref[...], k3, v3, b_ref[...], s_ref[...]).astype(o_ref.dtype)

    res = pl.pallas_call(
        body, name="swa_fwd", grid=(M // BLK,),
        in_specs=_swa_in_specs() + s_specs,
        out_specs=[pl.BlockSpec((BLK, 1024), lambda n: (n, 0))] + [pl.BlockSpec(memory_space=pl.ANY)] * nso,
        out_shape=[jax.ShapeDtypeStruct((M, 1024), BF16)] + s_shapes,
        scratch_shapes=s_sems,
        compiler_params=_cparams(("arbitrary",)),
    )(proj, proj, proj, proj, proj, proj, proj, bias, sinks, *s_ins)
    return res[0], res[1:]


def swa_bwd(proj, bias, sinks, do, side=None):
    M = proj.shape[0]
    s_ins, s_specs, s_shapes, s_sems = _side_parts(side)
    ns, nso = len(s_ins), len(s_shapes)

    def body(q_ref, k0, k1, k2, v0, v1, v2, b_ref, s_ref, do_ref, *rest):
        dq_ref, dk_ref, dv_ref, db_ref, ds_ref = rest[ns:ns + 5]
        _side_run(side, M // BLK, rest[:ns], rest[ns + 5:ns + 5 + nso], rest[ns + 5 + nso:])
        n = pl.program_id(0)

        @pl.when(n == 0)
        def _():
            dk_ref[...] = jnp.zeros_like(dk_ref)
            dv_ref[...] = jnp.zeros_like(dv_ref)
            ds_ref[...] = jnp.zeros_like(ds_ref)

        k3 = jnp.concatenate([k0[...], k1[...], k2[...]], axis=0)
        v3 = jnp.concatenate([v0[...], v1[...], v2[...]], axis=0)
        _, vjp = jax.vjp(_swa_block, q_ref[...], k3, v3, b_ref[...], s_ref[...])
        dq, dk3, dv3, dbias, dsink = vjp(do_ref[...].astype(F32))
        dq_ref[...] = dq
        prev = pl.multiple_of(jnp.maximum(n - 1, 0) * BLK, BLK)
        cur = pl.multiple_of(n * BLK, BLK)
        dk_ref[pl.ds(0, BLK), :] += dk3[0:BLK]
        dv_ref[pl.ds(0, BLK), :] += dv3[0:BLK]
        dk_ref[pl.ds(prev, BLK), :] += dk3[BLK:2 * BLK]
        dv_ref[pl.ds(prev, BLK), :] += dv3[BLK:2 * BLK]
        dk_ref[pl.ds(cur, BLK), :] += dk3[2 * BLK:]
        dv_ref[pl.ds(cur, BLK), :] += dv3[2 * BLK:]
        ds_ref[...] += dsink

        @pl.when(n <= 2)
        def _():
            db_ref[...] = dbias

        @pl.when(n > 2)
        def _():
            db_ref[...] += dbias

    res = pl.pallas_call(
        body, name="swa_bwd", grid=(M // BLK,),
        in_specs=_swa_in_specs() + [pl.BlockSpec((BLK, 1024), lambda n: (n, 0))] + s_specs,
        out_specs=[pl.BlockSpec((BLK, 1024), lambda n: (n, 0)),
                   pl.BlockSpec((M, 256), lambda n: (0, 0)), pl.BlockSpec((M, 256), lambda n: (0, 0)),
                   pl.BlockSpec((None, 8, BLK, 3 * BLK), lambda n: (jnp.minimum(n, 2), 0, 0, 0)),
                   pl.BlockSpec((1, LANE), lambda n: (0, 0))] + [pl.BlockSpec(memory_space=pl.ANY)] * nso,
        out_shape=[jax.ShapeDtypeStruct((M, 1024), F32), jax.ShapeDtypeStruct((M, 256), F32),
                   jax.ShapeDtypeStruct((M, 256), F32), jax.ShapeDtypeStruct((3, 8, BLK, 3 * BLK), F32),
                   jax.ShapeDtypeStruct((1, LANE), F32)] + s_shapes,
        scratch_shapes=s_sems,
        compiler_params=_cparams(("arbitrary",)),
    )(proj, proj, proj, proj, proj, proj, proj, bias, sinks, do, *s_ins)
    return res[:5], res[5:]


def _shift_rows_impl(x, k):
    n = x.shape[0]
    rolled = pltpu.roll(x, k, 0)
    return jnp.where(_row_ids(0, n) >= k, rolled, 0.0)


def _unshift_rows_impl(g, k):
    n = g.shape[0]
    rolled = pltpu.roll(g, n - k, 0)
    return jnp.where(_row_ids(0, n) < n - k, rolled, 0.0)


@functools.partial(jax.custom_vjp, nondiff_argnums=(1,))
def _shift_rows(x, k):
    return _shift_rows_impl(x, k)


def _shift_rows_f(x, k):
    return _shift_rows_impl(x, k), None


def _shift_rows_b(k, _, g):
    return (_unshift_rows_impl(g, k),)


_shift_rows.defvjp(_shift_rows_f, _shift_rows_b)


def _conv_silu(x, w):
    rid = lax.broadcasted_iota(jnp.int32, w.shape, 0)
    y = x * jnp.sum(jnp.where(rid == 3, w, 0.0), axis=0, keepdims=True)
    for k in range(1, 4):
        y = y + _shift_rows(x, k) * jnp.sum(jnp.where(rid == 3 - k, w, 0.0), axis=0, keepdims=True)
    y = jnp.where(_row_ids(0, x.shape[0]) >= ZROWS, y, 0.0)
    return _silu(y)


def conv_fwd(proj, conv_w):
    M = proj.shape[0]
    nb = conv_w.shape[1] // LANE

    def body(x_ref, w_ref, o_ref):
        o_ref[...] = _conv_silu(x_ref[...], w_ref[...])

    return pl.pallas_call(
        body, name="conv_fwd", grid=(nb,),
        in_specs=[pl.BlockSpec((M, LANE), lambda c: (0, E_QB // LANE + c)), pl.BlockSpec((4, LANE), lambda c: (0, c))],
        out_specs=pl.BlockSpec((M, LANE), lambda c: (0, c)),
        out_shape=jax.ShapeDtypeStruct((M, conv_w.shape[1]), F32),
        compiler_params=_cparams(("parallel",)),
    )(proj, conv_w)


def conv_bwd(proj, conv_w, dy):
    M = proj.shape[0]
    nb = conv_w.shape[1] // LANE

    def body(x_ref, w_ref, dy_ref, dx_ref, dw_ref):
        _, vjp = jax.vjp(_conv_silu, x_ref[...], w_ref[...])
        dx, dw = vjp(dy_ref[...])
        dx_ref[...] = dx
        dw_ref[...] = dw

    return pl.pallas_call(
        body, name="conv_bwd", grid=(nb,),
        in_specs=[pl.BlockSpec((M, LANE), lambda c: (0, E_QB // LANE + c)), pl.BlockSpec((4, LANE), lambda c: (0, c)),
                  pl.BlockSpec((M, LANE), lambda c: (0, c))],
        out_specs=[pl.BlockSpec((M, LANE), lambda c: (0, c)), pl.BlockSpec((4, LANE), lambda c: (0, c))],
        out_shape=[jax.ShapeDtypeStruct((M, conv_w.shape[1]), F32), jax.ShapeDtypeStruct(conv_w.shape, F32)],
        compiler_params=_cparams(("parallel",)),
    )(proj, conv_w, dy)


def _fn_dn_prep(row0, yq, yk, ba, dnp):
    tm = yq.shape[0]
    real = _row_ids(row0, tm) >= ZROWS
    qs, ks, gs, bs = [], [], [], []
    for h in range(4):
        q = yq[:, h * 128:(h + 1) * 128]
        k = yk[:, h * 128:(h + 1) * 128]
        qs.append(q * lax.rsqrt(jnp.sum(q * q, axis=-1, keepdims=True) + 1e-6) * (128.0 ** -0.5))
        ks.append(k * lax.rsqrt(jnp.sum(k * k, axis=-1, keepdims=True) + 1e-6))
        beta = _sigmoid(_lane_pick(ba, h))
        g = -jnp.exp(_lane_pick(dnp, h)) * _softplus(_lane_pick(ba, 4 + h) + _lane_pick(dnp, 4 + h))
        g = jnp.where(real, g, 0.0)
        gs.append(jnp.broadcast_to(g, (tm, 128)))
        bs.append(jnp.broadcast_to(beta, (tm, 128)))
    cat = lambda xs: jnp.concatenate(xs, axis=1)
    return cat(qs), cat(ks), cat(gs), cat(bs)


def _zip(f, *lists):
    return [f(*args) for args in zip(*lists)]


def _unit_lower_inv_impl(a):
    n = a[0].shape[0]
    eye = (lax.broadcasted_iota(jnp.int32, (n, n), 0) == lax.broadcasted_iota(jnp.int32, (n, n), 1)).astype(F32)
    nn = ((1,), (0,))
    p = [-x for x in a]
    t = [eye + x for x in p]
    for _ in range(int(math.log2(n)) - 1):
        p = _zip(lambda x: _xdot(x, x, nn), p)
        t = _zip(lambda x, y: x + _xdot(x, y, nn), t, p)
    return t


@jax.custom_vjp
def _unit_lower_inv(a):
    return _unit_lower_inv_impl(a)


def _unit_lower_inv_f(a):
    t = _unit_lower_inv_impl(a)
    return t, t


def _unit_lower_inv_b(t, g):
    tg = _zip(lambda x, y: _xdot(x, y, ((0,), (0,))), t, g)
    return (_zip(lambda x, y: -_xdot(x, y, ((1,), (1,))), tg, t),)


_unit_lower_inv.defvjp(_unit_lower_inv_f, _unit_lower_inv_b)


@jax.custom_vjp
def _known_inv(a, t):
    return t


_known_inv.defvjp(lambda a, t: (t, t), lambda t, g: (_unit_lower_inv_b(t, g)[0], [jnp.zeros_like(x) for x in t]))


def _dn_block(q, k, v, gb, bb, S, t_kept=None):
    nh = len(S)
    r = lax.broadcasted_iota(jnp.int32, (CH, CH), 0)
    c = lax.broadcasted_iota(jnp.int32, (CH, CH), 1)
    tri_incl = r >= c
    gcb = _zip(_cumsum_rows, gb)
    gamma = _zip(lambda x: jnp.where(tri_incl, jnp.exp(jnp.where(tri_incl, x[:, :CH] - x[:, :CH].T, 0.0)), 0.0), gcb)
    kb = _zip(jnp.multiply, k, bb)
    vb = _zip(jnp.multiply, v, bb)
    a = _zip(lambda m, g: jnp.where(r > c, m * g, 0.0), _zip(_bmm_nt, kb, k), gamma)
    t = _unit_lower_inv(a) if t_kept is None else _known_inv(a, list(t_kept))
    eg = _zip(jnp.exp, gcb)
    u = _zip(_xmm, t, vb)
    w = _zip(_xmm, t, _zip(jnp.multiply, kb, eg))
    attn = _zip(lambda m, g: m * g, _zip(_bmm_nt, q, k), gamma)
    gtot = _zip(lambda x: jnp.sum(x, axis=0, keepdims=True), gb)
    k_dec = _zip(lambda x, gt, gc: x * jnp.exp(gt - gc), k, gtot, gcb)
    q_dec = _zip(jnp.multiply, q, eg)
    S = list(S)
    o, starts = [], []
    for i0 in range(0, len(q), nh):
        idx = range(i0, i0 + nh)
        starts.append(list(S))
        v_new = [u[i] - m for i, m in zip(idx, [_bmm(w[i], S[h]) for h, i in enumerate(idx)])]
        oq = [_bmm(q_dec[i], S[h]) for h, i in enumerate(idx)]
        oa = [_bmm(attn[i], vn) for i, vn in zip(idx, v_new)]
        kv = [_bmm_tn(k_dec[i], vn) for i, vn in zip(idx, v_new)]
        o += _zip(jnp.add, oq, oa)
        S = [S[h] * jnp.exp(jnp.broadcast_to(gtot[i], S[h].shape)) + kv[h] for h, i in enumerate(idx)]
    return o, S, starts, t


def _gla_block(q, k, v, glog, S):
    nh = len(S)
    tri = lax.broadcasted_iota(jnp.int32, (CH, CH), 0) >= lax.broadcasted_iota(jnp.int32, (CH, CH), 1)
    bcum = _zip(_cumsum_rows, glog)
    q_dec = _zip(lambda x, b: x * (128.0 ** -0.5) * jnp.exp(b), q, bcum)
    attn = _zip(lambda m: jnp.where(tri, m, 0.0), _zip(_bmm_nt, q_dec, _zip(lambda x, b: x * jnp.exp(-b), k, bcum)))
    o_in = _zip(_bmm, attn, v)
    k_dec = _zip(lambda x, g, b: x * jnp.exp(jnp.sum(g, axis=0, keepdims=True) - b), k, glog, bcum)
    decay = _zip(lambda g, x: jnp.exp(_colsum_as_rows(g, x.shape[1])), glog, v)
    kv = _zip(_bmm_tn, k_dec, v)
    S = list(S)
    o, starts = [], []
    for i0 in range(0, len(q), nh):
        idx = range(i0, i0 + nh)
        starts.append(list(S))
        o += [o_in[i] + m for i, m in zip(idx, [_bmm(q_dec[i], S[h]) for h, i in enumerate(idx)])]
        S = [S[h] * decay[i] + kv[i] for h, i in enumerate(idx)]
    return o, S, starts


class Side:
    def __init__(self, ins, out_shapes, nsem, events):
        self.ins, self.out_shapes, self.nsem, self.events = list(ins), list(out_shapes), nsem, events


def _side_parts(side):
    if side is None:
        return [], [], [], []
    anyspec = pl.BlockSpec(memory_space=pl.ANY)
    return (side.ins, [anyspec] * len(side.ins), side.out_shapes,
            [pltpu.SemaphoreType.DMA((side.nsem,)), pltpu.SemaphoreType.DMA((side.nsem,))])


def _side_run(side, n_steps, in_refs, out_refs, sems, step=None):
    if side is None:
        return
    step = pl.program_id(0) if step is None else step
    for at, fn in side.events(n_steps, in_refs, out_refs, *sems):
        pl.when(step == at)(fn)


def _chunks_per_step(n_chunks):
    return next(c for c in (CPS, 2, 1) if n_chunks % c == 0)


def chunk_fwd(name, chunk_fn, ins, dv, side=None, keep=()):
    M = ins[0][0].shape[0]
    NC = M // CH
    cps = _chunks_per_step(NC)
    N = NC // cps
    ni, nk = len(ins), len(keep)
    ws = [w for (_, w, _) in ins]
    s_ins, s_specs, s_shapes, s_sems = _side_parts(side)
    ns, nso = len(s_ins), len(s_shapes)

    def body(*refs):
        o0 = ni + ns
        o_ref, sall_ref = refs[o0:o0 + 2]
        k_refs = refs[o0 + 2:o0 + 2 + nk]
        s_ref = refs[o0 + 2 + nk + nso]
        _side_run(side, N, refs[ni:o0], refs[o0 + 2 + nk:o0 + 2 + nk + nso], refs[o0 + 3 + nk + nso:])

        @pl.when(pl.program_id(0) == 0)
        def _():
            s_ref[...] = jnp.zeros_like(s_ref)

        problems = [(cc, h) for cc in range(cps) for h in range(4)]
        lists = [[r[cc * CH:(cc + 1) * CH, h * w:(h + 1) * w] for cc, h in problems] for r, w in zip(refs[:ni], ws)]
        o, s_new, starts, *kept = chunk_fn(*lists, [s_ref[h] for h in range(4)])
        for b, (cc, h) in enumerate(problems):
            o_ref[cc * CH:(cc + 1) * CH, h * dv:(h + 1) * dv] = o[b]
            sall_ref[h, cc] = starts[cc][h]
            for k_ref, vals in zip(k_refs, kept):
                k_ref[h, cc] = vals[b]
        for h in range(4):
            s_ref[h] = s_new[h]

    per_chunk = lambda r, c: pl.BlockSpec((4, cps, r, c), lambda n: (0, n, 0, 0))
    specs = [pl.BlockSpec((cps * CH, 4 * w), functools.partial(lambda n, cb: (n, cb), cb=cb // 4)) for (_, w, cb) in ins]
    res = pl.pallas_call(
        body, name=name, grid=(N,),
        in_specs=specs + s_specs,
        out_specs=[pl.BlockSpec((cps * CH, 4 * dv), lambda n: (n, 0)), per_chunk(128, dv)] + [per_chunk(r, c) for r, c in keep]
        + [pl.BlockSpec(memory_space=pl.ANY)] * nso,
        out_shape=[jax.ShapeDtypeStruct((M, 4 * dv), F32), jax.ShapeDtypeStruct((4, NC, 128, dv), F32)]
        + [jax.ShapeDtypeStruct((4, NC, r, c), F32) for r, c in keep] + s_shapes,
        scratch_shapes=[pltpu.VMEM((4, 128, dv), F32)] + s_sems,
        compiler_params=_cparams(("arbitrary",)),
    )(*[a for (a, _, _) in ins], *s_ins)
    return res[0], res[1], res[2:2 + nk], res[2 + nk:]


def chunk_bwd(name, chunk_fn, ins, dv, s_all, do, side=None, kept=()):
    M = ins[0][0].shape[0]
    cps = _chunks_per_step(M // CH)
    N = M // CH // cps
    ni, nk = len(ins), len(kept)
    ws = [w for (_, w, _) in ins]
    s_ins, s_specs, s_shapes, s_sems = _side_parts(side)
    ns, nso = len(s_ins), len(s_shapes)

    def body(*refs):
        sall_ref, do_ref = refs[ni:ni + 2]
        k_refs = refs[ni + 2:ni + 2 + nk]
        o0 = ni + 2 + nk + ns
        d_refs = refs[o0:o0 + ni]
        ds_ref = refs[o0 + ni + nso]
        _side_run(side, N, refs[ni + 2 + nk:o0], refs[o0 + ni:o0 + ni + nso], refs[o0 + ni + nso + 1:])

        @pl.when(pl.program_id(0) == 0)
        def _():
            ds_ref[...] = jnp.zeros_like(ds_ref)

        problems = [(cc, h) for cc in range(cps) for h in range(4)]
        lists = [[r[cc * CH:(cc + 1) * CH, h * w:(h + 1) * w] for cc, h in problems] for r, w in zip(refs[:ni], ws)]
        kept_lists = [[k_ref[h, cc] for cc, h in problems] for k_ref in k_refs]
        _, vjp = jax.vjp(lambda *a: tuple(chunk_fn(*a)[:2]), *lists, [sall_ref[h, 0] for h in range(4)], *kept_lists)
        grads = vjp(([do_ref[cc * CH:(cc + 1) * CH, h * dv:(h + 1) * dv] for cc, h in problems],
                     [ds_ref[h] for h in range(4)]))
        for d_ref, w, g in zip(d_refs, ws, grads[:ni]):
            for b, (cc, h) in enumerate(problems):
                d_ref[cc * CH:(cc + 1) * CH, h * w:(h + 1) * w] = g[b]
        for h in range(4):
            ds_ref[h] = grads[ni][h]

    rev = lambda n: N - 1 - n
    per_chunk = lambda r, c: pl.BlockSpec((4, cps, r, c), lambda n: (0, rev(n), 0, 0))
    specs = [pl.BlockSpec((cps * CH, 4 * w), functools.partial(lambda n, cb: (rev(n), cb), cb=cb // 4)) for (_, w, cb) in ins]
    res = pl.pallas_call(
        body, name=name, grid=(N,),
        in_specs=specs + [per_chunk(128, dv), pl.BlockSpec((cps * CH, 4 * dv), lambda n: (rev(n), 0))]
        + [per_chunk(*a.shape[2:]) for a in kept] + s_specs,
        out_specs=[pl.BlockSpec((cps * CH, 4 * w), lambda n: (rev(n), 0)) for w in ws] + [pl.BlockSpec(memory_space=pl.ANY)] * nso,
        out_shape=[jax.ShapeDtypeStruct((M, 4 * w), F32) for w in ws] + s_shapes,
        scratch_shapes=[pltpu.VMEM((4, 128, dv), F32)] + s_sems,
        compiler_params=_cparams(("arbitrary",)),
    )(*[a for (a, _, _) in ins], s_all, do, *kept, *s_ins)
    return res[:ni], res[ni:]


def _fn_gate_out(hd, row0, o, z, w):
    outs = []
    for h in range(4):
        outs.append(_rms(o[:, h * hd:(h + 1) * hd], w) * _silu(z[:, h * hd:(h + 1) * hd]))
    return (jnp.concatenate(outs, axis=1),)


def _fn_gla_prep(row0, gk, wgu, bg):
    x = _mm(gk, wgu) + bg
    ls = jnp.minimum(x, 0.0) - jnp.log(1.0 + jnp.exp(-jnp.abs(x)))
    return (jnp.where(_row_ids(row0, gk.shape[0]) >= ZROWS, ls / 16.0, 0.0),)


def loss_call(y, tgt):
    M = y.shape[0]
    tm = _pick(M, 512)

    def body(y_ref, t_ref, l_ref, dy_ref):
        i = pl.program_id(0)
        e = y_ref[...] - t_ref[...]
        dy_ref[...] = e * (1.0 / D)
        part = 0.5 * jnp.sum(jnp.sum(e * e, axis=1, keepdims=True) * (1.0 / D), axis=0, keepdims=True)
        part = jnp.broadcast_to(part, (8, LANE))

        @pl.when(i == 0)
        def _():
            l_ref[...] = part

        @pl.when(i > 0)
        def _():
            l_ref[...] += part

    return pl.pallas_call(
        body, name="loss", grid=(M // tm,),
        in_specs=[pl.BlockSpec((tm, D), lambda i: (i, 0))] * 2,
        out_specs=[pl.BlockSpec((8, LANE), lambda i: (0, 0)), pl.BlockSpec((tm, D), lambda i: (i, 0))],
        out_shape=[jax.ShapeDtypeStruct((8, LANE), F32), jax.ShapeDtypeStruct((M, D), F32)],
        compiler_params=_cparams(("arbitrary",)),
    )(y, tgt)


def _bf(x):
    return x.astype(BF16)


def core_step(x, tgt, W, comm=None):
    S = x.shape[0]
    M = S + PADR
    ids = jnp.asarray(_swa_bucket_ids())
    h0 = jnp.concatenate([jnp.zeros((ZROWS, D), F32), W["meta"], x], axis=0)
    nw = W["norm"]
    nrow = lambda l, k: nw[l, k][None, :]
    ffw = W["ffn"]
    projs = {}

    def projw(l):
        if l not in projs:
            projs[l] = W["proj"](l)
        return projs[l]

    site = (lambda name: comm.side(name)) if comm else (lambda name: None)
    landed = (lambda name, outs: comm.done(name, outs)) if comm else (lambda name, outs: None)
    sinks = jnp.pad(W["sinks"], ((0, 0), (0, LANE - 8)))
    dnp = jnp.pad(jnp.concatenate([W["a_log"], W["dt_bias"]], axis=1), ((0, 0), (0, LANE - 8)))
    wgu = jnp.pad(W["gate_up"], ((0, LANE - 16), (0, 0)))
    bg = W["b_gate"]
    full = lambda a: (a, a.shape[1], 0)

    saved = []
    h = h0
    (hn,) = rowwise_fwd("prenorm_0", _fn_prenorm, [full(h)], [nrow(0, 0)], [(D, BF16)])
    bias = swa_bias_fwd(W["rel"], ids)
    for l in range(2):
        st = {"h_a": h, "hn_a": hn}
        (f1, a1, b1), got = ffn_fwd(f"ffn_fwd_{l}0", hn, *ffw(l, 0), side=site(f"ffn_fwd_{l}0"))
        landed(f"ffn_fwd_{l}0", got)
        h, hn = rowwise_fwd(f"resnorm_{l}1", functools.partial(_fn_resnorm, 0.5), [full(h), full(f1)],
                            [nrow(l, 1), nrow(l, 2)], [(D, F32), (D, BF16)])
        st.update(f1=f1, a1=a1, b1=b1, h_b=h, hn_b=hn)
        if l == 0:
            proj = mm_nt(hn, projw(0)["w_in"], "e_proj")
            o_a, got = swa_fwd(proj, bias, sinks, side=site("swa_fwd"))
            landed("swa_fwd", got)
            y = conv_fwd(proj, W["conv"])
            qn, kn, gb, bb = rowwise_fwd(
                "dn_prep", _fn_dn_prep, [(y, 512, 0), (y, 512, 1), (proj, LANE, E_BA // LANE)], [dnp], [(512, F32)] * 4)
            ins = [(qn, 128, 0), (kn, 128, 0), (y, 128, 8), (gb, 128, 0), (bb, 128, 0)]
            o_dn, s_all, (t_inv,), got = chunk_fwd("dn_fwd", _dn_block, ins, 128, keep=[(CH, CH)],
                                                   side=site("dn_fwd"))
            landed("dn_fwd", got)
            (o_b,) = rowwise_fwd("dn_out", functools.partial(_fn_gate_out, 128),
                                 [full(o_dn), (proj, 512, E_ZB // 512)], [W["dn_norm"]], [(512, BF16)])
            omix = jnp.concatenate([o_a, o_b], axis=1)
            mix = mm_nn(omix, projw(0)["w_out"], "e_mix")
            st.update(proj=proj, y=y, qn=qn, kn=kn, gb=gb, bb=bb, o_dn=o_dn, s_all=s_all, t_inv=t_inv, omix=omix)
        else:
            proj = mm_nt(hn, projw(1)["w_in"], "o_proj")
            (glog,) = rowwise_fwd("gla_prep", _fn_gla_prep, [(proj, LANE, O_GK // LANE)], [wgu, bg], [(512, F32)])
            ins = [(proj, 128, O_Q // 128), (proj, 128, O_K // 128), (proj, 256, O_V // 256), (glog, 128, 0)]
            o_g, s_all, _, _ = chunk_fwd("gla_fwd", _gla_block, ins, 256)
            (omix,) = rowwise_fwd("gla_out", functools.partial(_fn_gate_out, 256),
                                  [full(o_g), (proj, 1024, O_G // 1024)], [W["gla_norm"]], [(1024, BF16)])
            mix = mm_nn(omix, projw(1)["w_out"], "o_mix")
            st.update(proj=proj, glog=glog, o_g=o_g, s_all=s_all, omix=omix)
        h, hn = rowwise_fwd(f"resnorm_{l}3", functools.partial(_fn_resnorm, 1.0), [full(h), full(mix)],
                            [nrow(l, 3), nrow(l, 4)], [(D, F32), (D, BF16)])
        st.update(mix=mix, h_c=h, hn_c=hn)
        (f2, a2, b2), got = ffn_fwd(f"ffn_fwd_{l}1", hn, *ffw(l, 1), side=site(f"ffn_fwd_{l}1"))
        landed(f"ffn_fwd_{l}1", got)
        st.update(f2=f2, a2=a2, b2=b2)
        if l == 0:
            h, hn = rowwise_fwd("resnorm_05", functools.partial(_fn_resnorm, 0.5), [full(h), full(f2)],
                                [nrow(0, 5), nrow(1, 0)], [(D, F32), (D, BF16)])
        else:
            (h,) = rowwise_fwd("res_last", functools.partial(_fn_res_last, 0.5), [full(h), full(f2)],
                               [nrow(1, 5)], [(D, F32)])
        saved.append(st)

    loss_blk, dy = loss_call(h[PADR:], tgt)
    dh = jnp.concatenate([jnp.zeros((PADR, D), F32), dy], axis=0)

    G = {}
    dnorm = [[None] * 6 for _ in range(2)]
    dWg = [[None, None], [None, None]]
    dWu = [[None, None], [None, None]]
    dWd = [[None, None], [None, None]]
    dhn = None
    for l in (1, 0):
        st = saved[l]
        if l == 1:
            (dh_, df), (dw5,) = rowwise_bwd(
                "res_last_b", functools.partial(_fn_res_last, 0.5), [full(st["h_c"]), full(st["f2"])], [nrow(1, 5)],
                [full(dh)], [F32, BF16])
            dnorm[1][5] = dw5
        else:
            (dh_, df), (dw5, dw0n) = rowwise_bwd(
                "resnorm_05_b", functools.partial(_fn_resnorm, 0.5), [full(st["h_c"]), full(st["f2"])],
                [nrow(0, 5), nrow(1, 0)], [full(dh), full(dhn)], [F32, BF16])
            dnorm[0][5] = dw5
            dnorm[1][0] = dw0n
        dh = dh_
        if comm and l == 0:
            units = (0, 1, 2, U_IN, U_OUT)
            comm.grads(1, units, layer_grad_items(1, dWg, dWu, dWd, G["o_in"], G["o_out"], only=units))
        (dxn, da, db, hm), got = ffn_bwd_x(f"ffn_bx_{l}1", df, st["a2"], st["b2"], *ffw(l, 1), side=site(f"ffn_bx_{l}1"))
        landed(f"ffn_bx_{l}1", got)
        dWg[l][1], dWu[l][1], dWd[l][1] = ffn_bwd_w(f"ffn_bw_{l}1", st["hn_c"], df, da, db, hm)
        if comm:
            comm.grads(l, (3, 4, 5), layer_grad_items(l, dWg, dWu, dWd, None, None, only=(3, 4, 5)))
        (dh_, dmix), (dw3, dw4) = rowwise_bwd(
            f"resnorm_{l}3_b", functools.partial(_fn_resnorm, 1.0), [full(st["h_b"]), full(st["mix"])],
            [nrow(l, 3), nrow(l, 4)], [full(dh), full(dxn)], [F32, BF16])
        dnorm[l][3], dnorm[l][4] = dw3, dw4
        dh = dh_
        proj = st["proj"]
        if l == 1:
            G["o_out"] = mm_tn(st["omix"], dmix, "o_out_dw")
            domix = mm_nt(dmix, projw(1)["w_out"], "o_mix_dx")
            (do_g, dgate), (dgn,) = rowwise_bwd(
                "gla_out_b", functools.partial(_fn_gate_out, 256), [full(st["o_g"]), (proj, 1024, O_G // 1024)],
                [W["gla_norm"]], [full(domix)], [F32, F32])
            G["gla_norm"] = dgn
            ins = [(proj, 128, O_Q // 128), (proj, 128, O_K // 128), (proj, 256, O_V // 256), (st["glog"], 128, 0)]
            (dq, dk, dv, dglog), _ = chunk_bwd("gla_bwd", _gla_block, ins, 256, st["s_all"], do_g)
            (dgk,), (dwgu, dbg) = rowwise_bwd("gla_prep_b", _fn_gla_prep, [(proj, LANE, O_GK // LANE)], [wgu, bg],
                                              [full(dglog)], [F32])
            G["gate_up"] = dwgu[:16]
            G["b_gate"] = dbg
            dproj = _bf(jnp.concatenate([dq, dk, dv, dgate, dgk, jnp.zeros((M, O_END - O_GK - LANE), F32)], axis=1))
            G["o_in"] = mm_tn(dproj, st["hn_b"], "o_in_dw")
            dhn_b = mm_nn(dproj, projw(1)["w_in"], "o_proj_dx")
        else:
            G["e_out"] = mm_tn(st["omix"], dmix, "e_out_dw")
            domix = mm_nt(dmix, projw(0)["w_out"], "e_mix_dx")
            (do_dn, dz), (ddn,) = rowwise_bwd(
                "dn_out_b", functools.partial(_fn_gate_out, 128), [full(st["o_dn"]), (proj, 512, E_ZB // 512)],
                [W["dn_norm"]], [(domix, 512, 2)], [F32, F32])
            G["dn_norm"] = ddn
            ins = [(st["qn"], 128, 0), (st["kn"], 128, 0), (st["y"], 128, 8), (st["gb"], 128, 0), (st["bb"], 128, 0)]
            (dqn, dkn, dvv, dgb, dbb), got = chunk_bwd("dn_bwd", _dn_block, ins, 128, st["s_all"], do_dn, side=site("dn_bwd"),
                                                        kept=[st["t_inv"]])
            landed("dn_bwd", got)
            (dyq, dyk, dba), (ddnp,) = rowwise_bwd(
                "dn_prep_b", _fn_dn_prep, [(st["y"], 512, 0), (st["y"], 512, 1), (proj, LANE, E_BA // LANE)], [dnp],
                [full(dqn), full(dkn), full(dgb), full(dbb)], [F32, F32, F32])
            G["a_log"] = ddnp[:, 0:4]
            G["dt_bias"] = ddnp[:, 4:8]
            dyc = jnp.concatenate([dyq, dyk, dvv], axis=1)
            dxc, dconv = conv_bwd(proj, W["conv"], dyc)
            G["conv"] = dconv
            (dq_a, dk_a, dv_a, dbias, dsink), got = swa_bwd(proj, bias, sinks, domix, side=site("swa_bwd"))
            landed("swa_bwd", got)
            G["sinks"] = dsink[:, :8]
            G["rel"] = swa_bias_bwd(dbias, ids)[:, :8]
            dproj = _bf(jnp.concatenate([dq_a, dk_a, dv_a, dxc, dz, dba, jnp.zeros((M, E_END - E_BA - LANE), F32)], axis=1))
            G["e_in"] = mm_tn(dproj, st["hn_b"], "e_in_dw")
            dhn_b = mm_nn(dproj, projw(0)["w_in"], "e_proj_dx")
        (dh_, df), (dw1, dw2) = rowwise_bwd(
            f"resnorm_{l}1_b", functools.partial(_fn_resnorm, 0.5), [full(st["h_a"]), full(st["f1"])],
            [nrow(l, 1), nrow(l, 2)], [full(dh), full(dhn_b)], [F32, BF16])
        dnorm[l][1], dnorm[l][2] = dw1, dw2
        dh = dh_
        if comm and l == 0:
            units = (U_IN, U_OUT)
            comm.grads(0, units, layer_grad_items(0, dWg, dWu, dWd, G["e_in"], G["e_out"], only=units))
        (dxn, da, db, hm), got = ffn_bwd_x(f"ffn_bx_{l}0", df, st["a1"], st["b1"], *ffw(l, 0), side=site(f"ffn_bx_{l}0"))
        landed(f"ffn_bx_{l}0", got)
        dWg[l][0], dWu[l][0], dWd[l][0] = ffn_bwd_w(f"ffn_bw_{l}0", st["hn_a"], df, da, db, hm)
        dhn = dxn
    (dh0p,), (dw00,) = rowwise_bwd("prenorm_0_b", _fn_prenorm, [full(saved[0]["h_a"])], [nrow(0, 0)], [full(dhn)], [F32])
    dnorm[0][0] = dw00
    dh = dh + dh0p
    G["meta"] = dh[ZROWS:PADR]
    G["norm"] = jnp.stack([jnp.concatenate(r, axis=0) for r in dnorm], axis=0)
    G["items"] = [layer_grad_items(0, dWg, dWu, dWd, G["e_in"], G["e_out"], only=(0, 1, 2) if comm else range(NUNITS)),
                  None if comm else layer_grad_items(1, dWg, dWu, dWd, G["o_in"], G["o_out"])]
    return loss_blk, dh[PADR:], G


NAMES = [("meta", "meta_tokens"), ("norm", "norm_w"), ("ffn_g", "ffn_w_gate"), ("ffn_u", "ffn_w_up"),
         ("ffn_d", "ffn_w_down"), ("rel", "rel_bias_table"), ("e_in", "even_w_in"), ("conv", "even_conv_w"),
         ("sinks", "swa_sinks"), ("a_log", "dn_a_log"), ("dt_bias", "dn_dt_bias"), ("dn_norm", "dn_norm_w"),
         ("e_out", "even_w_out"), ("o_in", "odd_w_in"), ("gate_up", "gla_w_gate_up"), ("b_gate", "gla_b_gate"),
         ("gla_norm", "gla_norm_w"), ("o_out", "odd_w_out")]
BIG = ["ffn_g", "ffn_u", "ffn_d", "e_in", "e_out", "o_in", "o_out"]
IN_ROWS = 800
SMALL = [("meta", (16, 256)), ("norm", (2, 6, 256)), ("conv", (1, 4, 384)), ("gate_up", (1, 16, 128)),
         ("b_gate", (1, 128)), ("gla_norm", (1, 64))]
REPL = [("rel", (32, 8)), ("sinks", (1, 8)), ("a_log", (1, 4)), ("dt_bias", (1, 4)), ("dn_norm", (1, 128))]
SMALL_REP = 88 * LANE
SMALL_ROWS = 96


def pack_small(t):
    a = jnp.concatenate([t[n].reshape(-1) for n, _ in SMALL])
    b = jnp.concatenate([t[n].reshape(-1) for n, _ in REPL])
    flat = jnp.concatenate([a, jnp.zeros((SMALL_REP - a.shape[0],), F32), b,
                            jnp.zeros((SMALL_ROWS * LANE - SMALL_REP - b.shape[0],), F32)])
    return flat.reshape(SMALL_ROWS, LANE)


def unpack_small(p):
    flat = p.reshape(-1)
    out, r = {}, 0
    for n, shp in SMALL:
        k = int(np.prod(shp))
        out[n] = flat[r:r + k].reshape(shp)
        r += k
    r = SMALL_REP
    for n, shp in REPL:
        k = int(np.prod(shp))
        out[n] = flat[r:r + k].reshape(shp)
        r += k
    return out


IN_SRC = (706, 772)


NUNITS = 8
U_IN, U_OUT = 6, 7


def _halves(a):
    return a.reshape(2, a.shape[0] // 2, D)


def weight_pieces(wt, l):
    inn = wt["e_in" if l == 0 else "o_in"]
    inn = jnp.pad(inn, ((0, IN_ROWS - inn.shape[0]), (0, 0)))
    ffn = [_halves(wt[n][l][j]) for j in range(2) for n in ("ffn_g", "ffn_u", "ffn_d")]
    return ffn + [_halves(inn), _halves(wt["e_out" if l == 0 else "o_out"])]


def ffn_weights(q, j):
    return tuple(q[3 * j + k].reshape(NSH, FSH, D) for k in range(3))


def proj_weights(l, q_in, q_out):
    m = _even_in_map() if l == 0 else _odd_in_map()
    src = np.where(m >= 0, (m // IN_SRC[l]) * IN_ROWS + m % IN_SRC[l], -1)
    w_out = q_out.reshape(NSH * 256, D)
    w_in = q_in.reshape(NSH * IN_ROWS, D)
    w_in = jnp.where(jnp.asarray(src >= 0)[:, None], jnp.take(w_in, jnp.asarray(np.maximum(src, 0)), axis=0), 0)
    return {"w_in": w_in.astype(q_in.dtype),
            "w_out": _rows(w_out, _even_out_map()) if l == 0 else w_out}


def layer_grad_items(l, dwg, dwu, dwd, g_in, g_out, only=range(NUNITS)):
    units = [[None] * NUNITS for _ in range(2)]

    def put(i, a):
        a = a.reshape(NSH, 2, a.shape[1] // 2, D)
        units[0][i], units[1][i] = a[:, 0], a[:, 1]

    for j in range(2):
        for k, t in enumerate((dwg, dwu, dwd)):
            if 3 * j + k in only:
                put(3 * j + k, t[l][j])
    if U_IN in only:
        m = _even_in_map() if l == 0 else _odd_in_map()
        gi = _rows(g_in, _inverse(m, NSH * IN_SRC[l])).reshape(NSH, IN_SRC[l], D)
        put(U_IN, _bf(jnp.pad(gi, ((0, 0), (0, IN_ROWS - IN_SRC[l]), (0, 0)))))
    if U_OUT in only:
        if l == 0:
            g_out = _rows(g_out, _inverse(_even_out_map(), 1024))
        put(U_OUT, _bf(g_out).reshape(NSH, 256, D))
    return units


def assemble_layer(l, r0, r1):
    whole = lambda i: jnp.concatenate([r0[i], r1[i]])
    return {"ffn_g": jnp.stack([whole(0), whole(3)]), "ffn_u": jnp.stack([whole(1), whole(4)]),
            "ffn_d": jnp.stack([whole(2), whole(5)]), "in": whole(U_IN)[:IN_SRC[l]], "out": whole(U_OUT)}


def big_grads(l0, l1):
    st = lambda n: jnp.stack([l0[n], l1[n]])
    return {"ffn_g": st("ffn_g"), "ffn_u": st("ffn_u"), "ffn_d": st("ffn_d"), "e_in": l0["in"], "e_out": l0["out"],
            "o_in": l1["in"], "o_out": l1["out"]}


def small_from_gathered(gs):
    sm = [unpack_small(gs[s]) for s in range(NSH)]
    full = {}
    full["meta"] = jnp.concatenate([sm[s]["meta"] for s in range(NSH)], axis=1)
    full["norm"] = jnp.concatenate([sm[s]["norm"] for s in range(NSH)], axis=2)
    full["conv"] = jnp.concatenate([sm[s]["conv"][0] for s in range(NSH)], axis=1)
    full["gate_up"] = jnp.concatenate([sm[s]["gate_up"][0] for s in range(NSH)], axis=1)
    full["b_gate"] = jnp.concatenate([sm[s]["b_gate"] for s in range(NSH)], axis=1)
    full["gla_norm"] = jnp.concatenate([sm[s]["gla_norm"] for s in range(NSH)], axis=1)
    return full


def _col_sh(w):
    return jnp.moveaxis(w.reshape(w.shape[0], NSH, w.shape[1] // NSH), 1, 0)


def _to_t(n, a):
    if n in ("ffn_g", "ffn_u"):
        return jnp.swapaxes(a, 2, 3)
    if n in ("e_in", "o_in"):
        return jnp.swapaxes(a[0], 0, 1)
    return a if n == "ffn_d" else a[0]


def _from_t(n, a):
    if n in ("ffn_g", "ffn_u"):
        return jnp.swapaxes(a, 2, 3)
    if n in ("e_in", "o_in"):
        return jnp.swapaxes(a, 0, 1)[None]
    return a if n == "ffn_d" else a[None]


def pack_small_grads(G):
    col_sh = _col_sh
    norm_sh = jnp.moveaxis(G["norm"].reshape(2, 6, NSH, 256), 2, 0)
    a = jnp.concatenate([col_sh(G["meta"]).reshape(NSH, -1), norm_sh.reshape(NSH, -1), col_sh(G["conv"]).reshape(NSH, -1),
                         col_sh(G["gate_up"]).reshape(NSH, -1), G["b_gate"].reshape(NSH, -1),
                         G["gla_norm"].reshape(NSH, -1)], axis=1)
    b = jnp.concatenate([G[n].reshape(-1) for n, _ in REPL])
    b = jnp.broadcast_to(b[None], (NSH, b.shape[0]))
    small = jnp.concatenate([a, jnp.zeros((NSH, SMALL_REP - a.shape[1]), F32), b,
                             jnp.zeros((NSH, SMALL_ROWS * LANE - SMALL_REP - b.shape[1]), F32)], axis=1)
    return small.reshape(NSH, SMALL_ROWS, LANE)


MESH = pl.DeviceIdType.MESH
ANY = pl.BlockSpec(memory_space=pl.ANY)
VMEM = pl.BlockSpec(memory_space=pltpu.VMEM)


def _place():
    return lax.axis_index("x"), lax.axis_index("y"), lax.axis_index("c")


def _other_chips(x, y):
    return [(1 - x, y), (x, 1 - y), (1 - x, 1 - y)]


def _rcopy(send_sems, recv_sems, k, src, dst, to):
    return pltpu.make_async_remote_copy(src_ref=src, dst_ref=dst, send_sem=send_sems.at[k], recv_sem=recv_sems.at[k],
                                        device_id=to, device_id_type=MESH)


def _gather_steps(in_refs, out_refs, send_sems, recv_sems):
    n = len(in_refs)
    x, y, c = _place()
    s = 2 * x + y
    chips = _other_chips(x, y)
    copy = functools.partial(_rcopy, send_sems, recv_sems)
    pairs = [(i, j, cx, cy) for i in range(n) for j, (cx, cy) in enumerate(chips)]
    pushes = lambda: [copy(i * 3 + j, in_refs[i].at[c], out_refs[i].at[s, c], (cx, cy, c)) for i, j, cx, cy in pairs]
    landed = lambda i, cx, cy, half: out_refs[i].at[2 * cx + cy, half]
    relays = lambda: [copy(3 * n + i * 3 + j, landed(i, cx, cy, c), landed(i, cx, cy, c), (x, y, 1 - c)) for i, j, cx, cy in pairs]

    def start():
        for cp in pushes():
            cp.start()

    def relay():
        for i, j, cx, cy in pairs:
            copy(i * 3 + j, landed(i, cx, cy, c), landed(i, cx, cy, c), (x, y, c)).wait_recv()
        for cp in relays():
            cp.start()

    def finish():
        for i, j, cx, cy in pairs:
            copy(3 * n + i * 3 + j, landed(i, cx, cy, 1 - c), landed(i, cx, cy, 1 - c), (x, y, c)).wait_recv()
        for cp in pushes() + relays():
            cp.wait_send()

    return start, relay, finish


def _gather_shapes(pieces):
    return [jax.ShapeDtypeStruct((NSH,) + a.shape, a.dtype) for a in pieces]


def ag_layer(name, pieces):
    n = len(pieces)

    def body(*refs):
        for fn in _gather_steps(refs[:n], refs[n:2 * n], *refs[2 * n:]):
            fn()

    return pl.pallas_call(
        body, name=name, in_specs=[ANY] * n, out_specs=[ANY] * n, out_shape=_gather_shapes(pieces),
        scratch_shapes=[pltpu.SemaphoreType.DMA((6 * n,)), pltpu.SemaphoreType.DMA((6 * n,))],
    )(*pieces)


def gather_side(pieces):
    def events(n_steps, in_refs, out_refs, send_sems, recv_sems):
        start, relay, finish = _gather_steps(in_refs, out_refs, send_sems, recv_sems)
        return [(0, start), (max(3 * n_steps // 4, 1), relay), (n_steps - 1, finish)]

    return Side(pieces, _gather_shapes(pieces), 6 * len(pieces), events)


def ag_small(pack):
    def body(x_ref, out_ref, send_sems, recv_sems):
        x, y, c = _place()
        s = 2 * x + y
        chips = _other_chips(x, y)

        def copy(k, src, dst, to):
            return pltpu.make_async_remote_copy(src_ref=src, dst_ref=dst, send_sem=send_sems.at[k], recv_sem=recv_sems.at[k],
                                                device_id=to, device_id_type=MESH)

        out_ref[s] = x_ref[...]
        sends = [copy(j, x_ref, out_ref.at[s], (cx, cy, c)) for j, (cx, cy) in enumerate(chips)]
        for cp in sends:
            cp.start()
        for j, (cx, cy) in enumerate(chips):
            blk = out_ref.at[2 * cx + cy]
            copy(j, blk, blk, (x, y, c)).wait_recv()
        for cp in sends:
            cp.wait_send()

    return pl.pallas_call(
        body, name="ag_small", in_specs=[VMEM], out_specs=VMEM,
        out_shape=jax.ShapeDtypeStruct((NSH,) + pack.shape, pack.dtype),
        scratch_shapes=[pltpu.SemaphoreType.DMA((3,)), pltpu.SemaphoreType.DMA((3,))],
    )(pack)


def rs_pair(name, items):
    ni = len(items[0])

    def body(*refs):
        in_refs = [refs[:ni], refs[ni:2 * ni]]
        recv_refs = refs[2 * ni:3 * ni]
        send_sems, recv_sems = refs[3 * ni:]
        x, y, c = _place()
        copy = functools.partial(_rcopy, send_sems, recv_sems)
        for cc in range(2):
            @pl.when(c == cc)
            def _():
                cps = [copy(i * NSH + s, in_refs[1 - cc][i].at[s], recv_refs[i].at[s], (x, y, 1 - c))
                       for i in range(ni) for s in range(NSH)]
                for cp in cps:
                    cp.start()
                for cp in cps:
                    cp.wait()

    return pl.pallas_call(
        body, name=name, in_specs=[ANY] * (2 * ni), out_specs=[ANY] * ni,
        out_shape=[jax.ShapeDtypeStruct(a.shape, a.dtype) for a in items[0]],
        scratch_shapes=[pltpu.SemaphoreType.DMA((ni * NSH,)), pltpu.SemaphoreType.DMA((ni * NSH,))],
    )(*items[0], *items[1])


def _scatter_steps(a_refs, out_refs, send_sems, recv_sems):
    n = len(a_refs)
    x, y, c = _place()
    s = 2 * x + y
    chips = _other_chips(x, y)
    copy = functools.partial(_rcopy, send_sems, recv_sems)
    pairs = [(i, j, cx, cy) for i in range(n) for j, (cx, cy) in enumerate(chips)]
    sends = lambda: [copy(i * 3 + j, a_refs[i].at[2 * cx + cy], out_refs[i].at[s], (cx, cy, c)) for i, j, cx, cy in pairs]

    def start():
        for cp in sends():
            cp.start()

    def finish():
        for i, j, cx, cy in pairs:
            blk = out_refs[i].at[2 * cx + cy]
            copy(i * 3 + j, blk, blk, (x, y, c)).wait_recv()
        for cp in sends():
            cp.wait_send()

    return start, finish


def rs_chips(name, arrs):
    n = len(arrs)

    def body(*refs):
        for fn in _scatter_steps(refs[:n], refs[n:2 * n], *refs[2 * n:]):
            fn()

    return pl.pallas_call(
        body, name=name, in_specs=[ANY] * n, out_specs=[ANY] * n,
        out_shape=[jax.ShapeDtypeStruct(a.shape, a.dtype) for a in arrs],
        scratch_shapes=[pltpu.SemaphoreType.DMA((3 * n,)), pltpu.SemaphoreType.DMA((3 * n,))],
    )(*arrs)


def scatter_side(arrs):
    def events(n_steps, in_refs, out_refs, send_sems, recv_sems):
        start, finish = _scatter_steps(in_refs, out_refs, send_sems, recv_sems)
        return [(0, start), (n_steps - 1, finish)]

    return Side(arrs, [jax.ShapeDtypeStruct(a.shape, a.dtype) for a in arrs], 3 * len(arrs), events)


def _pair_chunks(rows):
    return 4 if rows % 32 == 0 else (2 if rows % 16 == 0 else 1)


def ag_pair(name, arrs):
    n = len(arrs)
    chunks = [(i, k * (a.shape[0] // _pair_chunks(a.shape[0])), a.shape[0] // _pair_chunks(a.shape[0]))
              for i, a in enumerate(arrs) for k in range(_pair_chunks(a.shape[0]))]

    def body(*refs):
        g_refs, out_refs = refs[:n], refs[n:2 * n]
        send_sems, recv_sems = refs[2 * n:]
        x, y, c = _place()
        give = [_rcopy(send_sems, recv_sems, q, g_refs[i].at[pl.ds(r0, rc)], out_refs[i].at[pl.ds(r0, rc)], (x, y, 1 - c))
                for q, (i, r0, rc) in enumerate(chunks)]
        for cp in give:
            cp.start()
        for cp in give:
            cp.wait()

    return pl.pallas_call(
        body, name=name, in_specs=[ANY] * n, out_specs=[ANY] * n,
        out_shape=[jax.ShapeDtypeStruct(a.shape, a.dtype) for a in arrs],
        scratch_shapes=[pltpu.SemaphoreType.DMA((len(chunks),)), pltpu.SemaphoreType.DMA((len(chunks),))],
    )(*arrs)


def small_allreduce(p):
    def body(p_ref, out_ref, rbuf, send_sems, recv_sems):
        x, y, c = _place()
        me = 4 * x + 2 * y + c
        rbuf[me] = p_ref[2 * x + y]
        flip = lambda v, f: (1 - v) if f else v
        peers = [(flip(x, k >> 2 & 1), flip(y, k >> 1 & 1), flip(c, k & 1)) for k in range(1, 8)]

        def copy(k, src, dst, to):
            return pltpu.make_async_remote_copy(src_ref=src, dst_ref=dst, send_sem=send_sems.at[k], recv_sem=recv_sems.at[k],
                                                device_id=to, device_id_type=MESH)

        sends = [copy(k, p_ref.at[2 * px + py], rbuf.at[me], (px, py, pc)) for k, (px, py, pc) in enumerate(peers)]
        for cp in sends:
            cp.start()
        for k, (px, py, pc) in enumerate(peers):
            blk = rbuf.at[4 * px + 2 * py + pc]
            copy(k, blk, blk, (x, y, c)).wait_recv()
        for cp in sends:
            cp.wait_send()
        acc = rbuf[0]
        for d in range(1, 8):
            acc = acc + rbuf[d]
        out_ref[...] = acc

    return pl.pallas_call(
        body, name="small_allreduce", in_specs=[VMEM], out_specs=VMEM,
        out_shape=jax.ShapeDtypeStruct(p.shape[1:], F32),
        scratch_shapes=[pltpu.VMEM((8,) + p.shape[1:], F32), pltpu.SemaphoreType.DMA((7,)), pltpu.SemaphoreType.DMA((7,))],
    )(p)


def _rows_tile(rows, cap):
    return _pick(rows, cap) if rows % 128 == 0 else rows


def sum_pair(name, a0s, a1s, recvs, cflag):
    n = len(recvs)

    def body(c_ref, *refs):
        for i in range(n):
            a0_ref, a1_ref, b_ref, o_ref = refs[i], refs[n + i], refs[2 * n + i], refs[3 * n + i]
            own = jnp.where(c_ref[0] == 0, a0_ref[...].astype(F32), a1_ref[...].astype(F32))
            o_ref[...] = (own + b_ref[...].astype(F32)).astype(o_ref.dtype)

    specs = [pl.BlockSpec((None,) + a.shape[1:], lambda s: (s, 0, 0)) for a in recvs]
    return pl.pallas_call(
        body, name=name, grid=(NSH,), in_specs=[pl.BlockSpec(memory_space=pltpu.SMEM)] + specs * 3, out_specs=specs,
        out_shape=[jax.ShapeDtypeStruct(a.shape, BF16) for a in recvs], compiler_params=_cparams(("parallel",)),
    )(cflag, *a0s, *a1s, *recvs)


def sum_chips(name, parts, owns, sflag):
    n = len(parts)

    def body(s_ref, *refs):
        for i in range(n):
            p_ref, a_ref, o_ref = refs[i], refs[n + i], refs[2 * n + i]
            acc = None
            for t in range(NSH):
                term = jnp.where(s_ref[0] == t, a_ref[t].astype(F32), p_ref[t].astype(F32))
                acc = term if acc is None else acc + term
            o_ref[...] = acc

    specs = [pl.BlockSpec(a.shape, lambda i: (0, 0, 0)) for a in parts]
    return pl.pallas_call(
        body, name=name, grid=(1,), in_specs=[pl.BlockSpec(memory_space=pltpu.SMEM)] + specs * 2,
        out_specs=[pl.BlockSpec(a.shape[1:], lambda i: (0, 0)) for a in parts],
        out_shape=[jax.ShapeDtypeStruct(a.shape[1:], F32) for a in parts], compiler_params=_cparams(("arbitrary",)),
    )(sflag, *parts, *owns)


ADAM_LR, ADAM_B1, ADAM_B2, ADAM_EPS, ADAM_WD, ADAM_STEP = 0.001, 0.9, 0.999, 1e-08, 0.01, 10


def adamw_call(name, w, g, m, v):
    rows, cols = w.shape
    tr = _rows_tile(rows, 512)

    def body(w_ref, g_ref, m_ref, v_ref, d_ref, nm_ref, nv_ref):
        g_ = g_ref[...]
        m_ = ADAM_B1 * m_ref[...] + (1.0 - ADAM_B1) * g_
        v_ = ADAM_B2 * v_ref[...] + (1.0 - ADAM_B2) * (g_ * g_)
        m_hat = m_ / (1.0 - ADAM_B1 ** ADAM_STEP)
        v_hat = v_ / (1.0 - ADAM_B2 ** ADAM_STEP)
        d_ref[...] = -ADAM_LR * (m_hat / (jnp.sqrt(v_hat) + ADAM_EPS) + ADAM_WD * w_ref[...])
        nm_ref[...] = m_
        nv_ref[...] = v_

    spec = pl.BlockSpec((tr, cols), lambda i: (i, 0))
    sh = jax.ShapeDtypeStruct((rows, cols), F32)
    return pl.pallas_call(
        body, name=name, grid=(rows // tr,), in_specs=[spec] * 4, out_specs=[spec] * 3, out_shape=[sh] * 3,
        compiler_params=_cparams(("parallel",)),
    )(w, g, m, v)


def kernel(x, meta_tokens, norm_w, ffn_w_gate, ffn_w_up, ffn_w_down, rel_bias_table, even_w_in, even_conv_w, swa_sinks, dn_a_log, dn_dt_bias, dn_norm_w, even_w_out, odd_w_in, gla_w_gate_up, gla_b_gate, gla_norm_w, odd_w_out, loss_target, m_meta_tokens, m_norm_w, m_ffn_w_gate, m_ffn_w_up, m_ffn_w_down, m_rel_bias_table, m_even_w_in, m_even_conv_w, m_swa_sinks, m_dn_a_log, m_dn_dt_bias, m_dn_norm_w, m_even_w_out, m_odd_w_in, m_gla_w_gate_up, m_gla_b_gate, m_gla_norm_w, m_odd_w_out, v_meta_tokens, v_norm_w, v_ffn_w_gate, v_ffn_w_up, v_ffn_w_down, v_rel_bias_table, v_even_w_in, v_even_conv_w, v_swa_sinks, v_dn_a_log, v_dn_dt_bias, v_dn_norm_w, v_even_w_out, v_odd_w_in, v_gla_w_gate_up, v_gla_b_gate, v_gla_norm_w, v_odd_w_out):
    ws = [meta_tokens, norm_w, ffn_w_gate, ffn_w_up, ffn_w_down, rel_bias_table, even_w_in, even_conv_w, swa_sinks, dn_a_log,
          dn_dt_bias, dn_norm_w, even_w_out, odd_w_in, gla_w_gate_up, gla_b_gate, gla_norm_w, odd_w_out]
    ms = [m_meta_tokens, m_norm_w, m_ffn_w_gate, m_ffn_w_up, m_ffn_w_down, m_rel_bias_table, m_even_w_in, m_even_conv_w,
          m_swa_sinks, m_dn_a_log, m_dn_dt_bias, m_dn_norm_w, m_even_w_out, m_odd_w_in, m_gla_w_gate_up, m_gla_b_gate,
          m_gla_norm_w, m_odd_w_out]
    vs = [v_meta_tokens, v_norm_w, v_ffn_w_gate, v_ffn_w_up, v_ffn_w_down, v_rel_bias_table, v_even_w_in, v_even_conv_w,
          v_swa_sinks, v_dn_a_log, v_dn_dt_bias, v_dn_norm_w, v_even_w_out, v_odd_w_in, v_gla_w_gate_up, v_gla_b_gate,
          v_gla_norm_w, v_odd_w_out]
    short = [n for n, _ in NAMES]
    w = dict(zip(short, ws))
    m = dict(zip(short, ms))
    v = dict(zip(short, vs))

    wt = {n: _to_t(n, w[n]) for n in BIG}
    own = {n: wt[n].astype(BF16) for n in BIG}
    sflag = (2 * lax.axis_index("x") + lax.axis_index("y")).astype(jnp.int32).reshape(1)
    cflag = lax.axis_index("c").astype(jnp.int32).reshape(1)
    is0 = cflag[0] == 0
    fill = lambda got, pieces: [lax.dynamic_update_index_in_dim(g_, p_, sflag[0], 0) for g_, p_ in zip(got, pieces)]
    pieces = [weight_pieces(own, l) for l in range(2)]
    small = small_from_gathered(ag_small(pack_small(w)))
    def reduce_finish(l, mine, parts):
        red = [None] * NUNITS
        for grp in ((0, 1, 2), (3, 4, 5), (U_IN, U_OUT)):
            outs = sum_chips(f"sum_chips_{l}{grp[0]}", [parts[i] for i in grp], [mine[i] for i in grp], sflag)
            for i, o in zip(grp, outs):
                red[i] = o
        got = ag_pair(f"ag_pair_{l}", red)
        return [jnp.where(is0, r_, g_) for r_, g_ in zip(red, got)], [jnp.where(is0, g_, r_) for r_, g_ in zip(red, got)]

    class Exchanges:
        gathers = {"ffn_fwd_00": [(0, 3), (0, U_IN), (0, U_OUT)], "swa_fwd": [(0, 4)], "dn_fwd": [(0, 5), (1, 0), (1, 1)],
                   "ffn_fwd_01": [(1, 2), (1, U_IN), (1, U_OUT)], "ffn_fwd_10": [(1, 3), (1, 4), (1, 5)]}
        scatters = {"ffn_bx_10": [(1, 3), (1, 4), (1, 5)], "ffn_bx_01": [(1, 0), (1, 1), (1, 2)],
                    "dn_bwd": [(1, U_IN), (1, U_OUT), (0, 3)], "swa_bwd": [(0, 4), (0, 5)], "ffn_bx_00": [(0, U_IN), (0, U_OUT)]}

        def __init__(self):
            self.q = [[None] * NUNITS for _ in range(2)]
            self.mine = [[None] * NUNITS for _ in range(2)]
            self.parts = [[None] * NUNITS for _ in range(2)]

        def side(self, name):
            if name in self.gathers:
                return gather_side([pieces[l][i] for l, i in self.gathers[name]])
            if name in self.scatters:
                return scatter_side([self.mine[l][i] for l, i in self.scatters[name]])
            return None

        def done(self, name, outs):
            if name in self.gathers:
                units = self.gathers[name]
                for (l, i), a in zip(units, fill(outs, [pieces[l][i] for l, i in units])):
                    self.q[l][i] = a
            if name in self.scatters:
                for (l, i), a in zip(self.scatters[name], outs):
                    self.parts[l][i] = a

        def ffn(self, l, j):
            return ffn_weights(self.q[l], j)

        def proj(self, l):
            return proj_weights(l, self.q[l][U_IN], self.q[l][U_OUT])

        def grads(self, l, units, items):
            units = list(units)
            half0, half1 = [[items[c][i] for i in units] for c in range(2)]
            recv = rs_pair(f"rs_pair_{l}{units[0]}", [half0, half1])
            for i, a in zip(units, sum_pair(f"sum_pair_{l}{units[0]}", half0, half1, recv, cflag)):
                self.mine[l][i] = a

    ex = Exchanges()
    first = [0, 1, 2]
    for i, a in zip(first, fill(ag_layer("ag_layer_0", [pieces[0][i] for i in first]), [pieces[0][i] for i in first])):
        ex.q[0][i] = a
    W = {**small, **{n: w[n] for n, _ in REPL}, "ffn": ex.ffn, "proj": ex.proj}
    loss_blk, gx, G = core_step(x[0], loss_target[0], W, comm=ex)

    ex.grads(0, first, G["items"][0])
    for i, a in zip(first, rs_chips("rs_chips_0", [ex.mine[0][i] for i in first])):
        ex.parts[0][i] = a
    lay0 = assemble_layer(0, *reduce_finish(0, ex.mine[0], ex.parts[0]))
    lay1 = assemble_layer(1, *reduce_finish(1, ex.mine[1], ex.parts[1]))
    gt = big_grads(lay0, lay1)
    g_small_pack = small_allreduce(pack_small_grads(G))
    g = {**{n: _from_t(n, gt[n]) for n in BIG}, **unpack_small(g_small_pack)}

    delta, new_m, new_v = {}, {}, {}
    for n in BIG:
        shp = wt[n].shape
        two = lambda t: t.reshape(-1, D)
        d_, m_, v_ = adamw_call("adamw_" + n, two(wt[n]), two(gt[n]), two(_to_t(n, m[n])), two(_to_t(n, v[n])))
        delta[n], new_m[n], new_v[n] = (_from_t(n, t.reshape(shp)) for t in (d_, m_, v_))
    d_, m_, v_ = adamw_call("adamw_small", pack_small(w), g_small_pack, pack_small(m), pack_small(v))
    delta.update(unpack_small(d_))
    new_m.update(unpack_small(m_))
    new_v.update(unpack_small(v_))

    loss = lax.psum(loss_blk[0, 0], ("x", "y", "c"))
    return (loss, gx[None], *[g[n] for n in short], *[delta[n] for n in short], *[new_m[n] for n in short],
            *[new_v[n] for n in short])
```

```python
import functools
import math

import numpy as np
import jax
import jax.numpy as jnp
from jax import lax
from jax.experimental import pallas as pl
from jax.experimental.pallas import tpu as pltpu

F32 = jnp.float32
BF16 = jnp.bfloat16
HI = lax.Precision.HIGHEST

D = 1024
N_META = 16
PADR = 128
ZROWS = PADR - N_META
D_FF = 2816
NSH = 4
FSH = D_FF // NSH
EPS = 1e-6
NEG = -1e30
CH = 64
CPS = 3
BLK = 128
LANE = 128
VMEM_LIMIT = 56 * 1024 * 1024
FFN_SUB = 4

E_QA, E_KA, E_VA, E_QB, E_KB, E_VB, E_ZB, E_BA, E_END = 0, 1024, 1280, 1536, 2048, 2560, 3072, 3584, 4096


def _even_in_map():
    m = np.full((E_END,), -1, np.int64)
    for h in range(8):
        m[E_QA + h * 128:E_QA + h * 128 + 64] = np.arange(h * 64, (h + 1) * 64)
    for h in range(2):
        m[E_KA + h * 128:E_KA + h * 128 + 64] = 512 + np.arange(h * 64, (h + 1) * 64)
        m[E_VA + h * 128:E_VA + h * 128 + 64] = 640 + np.arange(h * 64, (h + 1) * 64)
    m[E_QB:E_QB + 2048] = 768 + np.arange(2048)
    m[E_BA:E_BA + 8] = 2816 + np.arange(8)
    return m


def _even_out_map():
    m = np.full((1536,), -1, np.int64)
    for h in range(8):
        m[h * 128:h * 128 + 64] = np.arange(h * 64, (h + 1) * 64)
    m[1024:1536] = 512 + np.arange(512)
    return m


O_Q, O_K, O_V, O_G, O_GK, O_END = 0, 512, 1024, 2048, 3072, 3584


def _odd_in_map():
    m = np.full((O_END,), -1, np.int64)
    m[:3072] = np.arange(3072)
    m[O_GK:O_GK + 16] = 3072 + np.arange(16)
    return m


def _inverse(m, n):
    inv = np.zeros((n,), np.int64)
    for p, o in enumerate(m):
        if o >= 0:
            inv[o] = p
    return inv


def _rows(w, m):
    parts, i, n = [], 0, len(m)
    while i < n:
        j = i + 1
        if m[i] < 0:
            while j < n and m[j] < 0:
                j += 1
            parts.append(jnp.zeros((j - i,) + w.shape[1:], w.dtype))
        else:
            while j < n and m[j] == m[j - 1] + 1:
                j += 1
            parts.append(lax.slice_in_dim(w, int(m[i]), int(m[i]) + j - i, axis=0))
        i = j
    return jnp.concatenate(parts, axis=0)


def _mm(a, b, prec=HI):
    return lax.dot_general(a, b, (((1,), (0,)), ((), ())), precision=prec, preferred_element_type=F32)


def _mm_nt(a, b, prec=HI):
    return lax.dot_general(a, b, (((1,), (1,)), ((), ())), precision=prec, preferred_element_type=F32)


def _mm_tn(a, b, prec=HI):
    return lax.dot_general(a, b, (((0,), (0,)), ((), ())), precision=prec, preferred_element_type=F32)


def _bdot(a, b, dims):
    return lax.dot_general(a.astype(BF16), b.astype(BF16), (dims, ((), ())), preferred_element_type=F32)


@jax.custom_vjp
def _bmm(a, b):
    return _bdot(a, b, ((1,), (0,)))


@jax.custom_vjp
def _bmm_nt(a, b):
    return _bdot(a, b, ((1,), (1,)))


@jax.custom_vjp
def _bmm_tn(a, b):
    return _bdot(a, b, ((0,), (0,)))


_bmm.defvjp(lambda a, b: (_bmm(a, b), (a, b)), lambda r, g: (_bmm_nt(g, r[1]), _bmm_tn(r[0], g)))
_bmm_nt.defvjp(lambda a, b: (_bmm_nt(a, b), (a, b)), lambda r, g: (_bmm(g, r[1]), _bmm_tn(g, r[0])))
_bmm_tn.defvjp(lambda a, b: (_bmm_tn(a, b), (a, b)), lambda r, g: (_bmm_nt(r[1], g), _bmm(r[0], g)))


def _hi_lo(x):
    h = x.astype(BF16)
    return h, (x - h.astype(F32)).astype(BF16)


def _xdot(a, b, dims):
    ah, al = _hi_lo(a)
    bh, bl = _hi_lo(b)
    d = lambda p, q: lax.dot_general(p, q, (dims, ((), ())), preferred_element_type=F32)
    return d(ah, bh) + (d(ah, bl) + d(al, bh))


@jax.custom_vjp
def _xmm(a, b):
    return _xdot(a, b, ((1,), (0,)))


@jax.custom_vjp
def _xmm_nt(a, b):
    return _xdot(a, b, ((1,), (1,)))


@jax.custom_vjp
def _xmm_tn(a, b):
    return _xdot(a, b, ((0,), (0,)))


_xmm.defvjp(lambda a, b: (_xmm(a, b), (a, b)), lambda r, g: (_xmm_nt(g, r[1]), _xmm_tn(r[0], g)))
_xmm_nt.defvjp(lambda a, b: (_xmm_nt(a, b), (a, b)), lambda r, g: (_xmm(g, r[1]), _xmm_tn(g, r[0])))
_xmm_tn.defvjp(lambda a, b: (_xmm_tn(a, b), (a, b)), lambda r, g: (_xmm_nt(r[1], g), _xmm(r[0], g)))


def _sum01(m01, x, dims):
    h, l = _hi_lo(x)
    l2 = (x - h.astype(F32) - l.astype(F32)).astype(BF16)
    m = m01.astype(BF16)
    d = lambda q: lax.dot_general(m, q, (dims, ((), ())), preferred_element_type=F32)
    return d(h) + (d(l) + d(l2))


@jax.custom_vjp
def _cumsum_rows(x):
    n = x.shape[0]
    tri = lax.broadcasted_iota(jnp.int32, (n, n), 0) >= lax.broadcasted_iota(jnp.int32, (n, n), 1)
    return _sum01(tri, x, ((1,), (0,)))


def _cumsum_rows_b(_, g):
    n = g.shape[0]
    tri = lax.broadcasted_iota(jnp.int32, (n, n), 0) >= lax.broadcasted_iota(jnp.int32, (n, n), 1)
    return (_sum01(tri, g, ((0,), (0,))),)


_cumsum_rows.defvjp(lambda x: (_cumsum_rows(x), None), _cumsum_rows_b)


@functools.partial(jax.custom_vjp, nondiff_argnums=(1,))
def _colsum_as_rows(x, width):
    return _colsum_impl(x, width)


def _colsum_impl(x, width):
    h, l = _hi_lo(x)
    l2 = (x - h.astype(F32) - l.astype(F32)).astype(BF16)
    ones = jnp.ones((x.shape[0], width), BF16)
    d = lambda q: lax.dot_general(q, ones, (((0,), (0,)), ((), ())), preferred_element_type=F32)
    return d(h) + (d(l) + d(l2))


def _colsum_as_rows_f(x, width):
    return _colsum_impl(x, width), x.shape[0]


def _colsum_as_rows_b(width, n, g):
    return (_sum01(jnp.ones((n, width), F32), g, ((1,), (1,))),)


_colsum_as_rows.defvjp(_colsum_as_rows_f, _colsum_as_rows_b)


def _rms(x, w):
    return x * lax.rsqrt(jnp.mean(x * x, axis=-1, keepdims=True) + EPS) * w


def _sigmoid(x):
    return 1.0 / (1.0 + jnp.exp(-x))


def _silu(x):
    return x * _sigmoid(x)


def _softplus(x):
    return jnp.maximum(x, 0.0) + jnp.log(1.0 + jnp.exp(-jnp.abs(x)))


def _lane_pick(row, idx):
    lane = lax.broadcasted_iota(jnp.int32, row.shape, row.ndim - 1)
    return jnp.sum(jnp.where(lane == idx, row, 0.0), axis=-1, keepdims=True)


def _row_ids(row0, n):
    return row0 + lax.broadcasted_iota(jnp.int32, (n, 1), 0)


def _pick(m, cap):
    best = 64
    for t in range(64, min(m, cap) + 1, 64):
        if m % t == 0:
            best = t
    return best


def _cparams(sem):
    return pltpu.CompilerParams(dimension_semantics=sem, vmem_limit_bytes=VMEM_LIMIT)


def mm_nn(a, b, name, out_dtype=F32):
    M, K = a.shape
    N = b.shape[1]
    tm = _pick(M, 1408 if K <= 2048 else 704)
    tn = _pick(N, 512)

    def body(a_ref, b_ref, o_ref):
        o_ref[...] = _mm(a_ref[...], b_ref[...], None).astype(o_ref.dtype)

    return pl.pallas_call(
        body, name=name, grid=(N // tn, M // tm),
        in_specs=[pl.BlockSpec((tm, K), lambda j, i: (i, 0)), pl.BlockSpec((K, tn), lambda j, i: (0, j))],
        out_specs=pl.BlockSpec((tm, tn), lambda j, i: (i, j)),
        out_shape=jax.ShapeDtypeStruct((M, N), out_dtype),
        compiler_params=_cparams(("parallel", "parallel")),
    )(a, b)


def mm_nt(a, b, name, out_dtype=F32):
    M, K = a.shape
    N = b.shape[0]
    tm = _pick(M, 1408)
    tn = _pick(N, 512 if K > 2048 else 1024)

    def body(a_ref, b_ref, o_ref):
        o_ref[...] = _mm_nt(a_ref[...], b_ref[...], None).astype(o_ref.dtype)

    return pl.pallas_call(
        body, name=name, grid=(N // tn, M // tm),
        in_specs=[pl.BlockSpec((tm, K), lambda j, i: (i, 0)), pl.BlockSpec((tn, K), lambda j, i: (j, 0))],
        out_specs=pl.BlockSpec((tm, tn), lambda j, i: (i, j)),
        out_shape=jax.ShapeDtypeStruct((M, N), out_dtype),
        compiler_params=_cparams(("parallel", "parallel")),
    )(a, b)


def mm_tn(a, b, name):
    M, K = a.shape
    N = b.shape[1]
    tk = _pick(K, 512)
    tn = _pick(N, 512)

    def body(a_ref, b_ref, o_ref):
        o_ref[...] = _mm_tn(a_ref[...], b_ref[...], None)

    return pl.pallas_call(
        body, name=name, grid=(K // tk, N // tn),
        in_specs=[pl.BlockSpec((M, tk), lambda i, j: (0, i)), pl.BlockSpec((M, tn), lambda i, j: (0, j))],
        out_specs=pl.BlockSpec((tk, tn), lambda i, j: (i, j)),
        out_shape=jax.ShapeDtypeStruct((K, N), F32),
        compiler_params=_cparams(("parallel", "parallel")),
    )(a, b)


def _row_specs(rows, tm):
    return [pl.BlockSpec((tm, w), functools.partial(lambda i, cb: (i, cb), cb=cb)) for (_, w, cb) in rows]


def _param_specs(params):
    return [pl.BlockSpec(p.shape, functools.partial(lambda i, nd: (0,) * nd, nd=p.ndim)) for p in params]


def rowwise_fwd(name, fn, rows, params, outs, tm=None):
    M = rows[0][0].shape[0]
    tm = tm or _pick(M, 704)
    nr, npar = len(rows), len(params)

    def body(*refs):
        row0 = pl.program_id(0) * tm
        vals = [r[...].astype(F32) for r in refs[:nr]] + [p[...] for p in refs[nr:nr + npar]]
        res = fn(row0, *vals)
        for o_ref, r in zip(refs[nr + npar:], res):
            o_ref[...] = r.astype(o_ref.dtype)

    return pl.pallas_call(
        body, name=name, grid=(M // tm,),
        in_specs=_row_specs(rows, tm) + _param_specs(params),
        out_specs=[pl.BlockSpec((tm, w), lambda i: (i, 0)) for (w, _) in outs],
        out_shape=[jax.ShapeDtypeStruct((M, w), dt) for (w, dt) in outs],
        compiler_params=_cparams(("parallel",)),
    )(*[r[0] for r in rows], *params)


def rowwise_bwd(name, fn, rows, params, douts, drow_dtypes, tm=None):
    M = rows[0][0].shape[0]
    tm = tm or _pick(M, 704)
    nr, npar, nd = len(rows), len(params), len(douts)
    want = [k for k, dt in enumerate(drow_dtypes) if dt is not None]

    def body(*refs):
        i = pl.program_id(0)
        row0 = i * tm
        vals = [r[...].astype(F32) for r in refs[:nr]] + [p[...] for p in refs[nr:nr + npar]]
        cots = tuple(d[...].astype(F32) for d in refs[nr + npar:nr + npar + nd])
        _, vjp = jax.vjp(functools.partial(fn, row0), *vals)
        grads = vjp(cots)
        o_refs = refs[nr + npar + nd:]
        for o_ref, k in zip(o_refs[:len(want)], want):
            o_ref[...] = grads[k].astype(o_ref.dtype)
        for o_ref, g in zip(o_refs[len(want):], grads[nr:]):
            @pl.when(i == 0)
            def _():
                o_ref[...] = g

            @pl.when(i > 0)
            def _():
                o_ref[...] += g

    res = pl.pallas_call(
        body, name=name, grid=(M // tm,),
        in_specs=_row_specs(rows, tm) + _param_specs(params) + _row_specs(douts, tm),
        out_specs=[pl.BlockSpec((tm, rows[k][1]), lambda i: (i, 0)) for k in want] + _param_specs(params),
        out_shape=[jax.ShapeDtypeStruct((M, rows[k][1]), drow_dtypes[k]) for k in want]
        + [jax.ShapeDtypeStruct(p.shape, F32) for p in params],
        compiler_params=_cparams(("arbitrary",)),
    )(*[r[0] for r in rows], *params, *[d[0] for d in douts])
    return res[:len(want)], res[len(want):]


def _fn_prenorm(row0, h, wpre):
    return (_rms(h, wpre),)


def _fn_resnorm(scale, row0, h, f, wpost, wpre):
    h2 = h + scale * _rms(f, wpost)
    return h2, _rms(h2, wpre)


def _fn_res_last(scale, row0, h, f, wpost):
    return (h + scale * _rms(f, wpost),)


def ffn_fwd(name, xn, wg, wu, wd, side=None):
    M = xn.shape[0]
    tm = _pick(M, 1408)
    s_ins, s_specs, s_shapes, s_sems = _side_parts(side)
    ns, nso = len(s_ins), len(s_shapes)

    def body(x_ref, wg_ref, wu_ref, wd_ref, *rest):
        f_ref, a_ref, b_ref = rest[ns:ns + 3]
        s = pl.program_id(1)
        _side_run(side, (M // tm) * NSH, rest[:ns], rest[ns + 3:ns + 3 + nso], rest[ns + 3 + nso:],
                  step=pl.program_id(0) * NSH + s)
        x = x_ref[...]
        a = _mm_nt(x, wg_ref[...], None)
        b = _mm_nt(x, wu_ref[...], None)
        c = _mm((_silu(a) * b).astype(BF16), wd_ref[...], None)

        @pl.when(s == 0)
        def _():
            f_ref[...] = c

        @pl.when(s > 0)
        def _():
            f_ref[...] += c

        a_ref[...] = a.astype(BF16)
        b_ref[...] = b.astype(BF16)

    wspec = wdspec = pl.BlockSpec((None, FSH, D), lambda i, s: (s, 0, 0))
    abspec = pl.BlockSpec((None, tm, FSH), lambda i, s: (s, i, 0))
    res = pl.pallas_call(
        body, name=name, grid=(M // tm, NSH),
        in_specs=[pl.BlockSpec((tm, D), lambda i, s: (i, 0)), wspec, wspec, wdspec] + s_specs,
        out_specs=[pl.BlockSpec((tm, D), lambda i, s: (i, 0)), abspec, abspec] + [pl.BlockSpec(memory_space=pl.ANY)] * nso,
        out_shape=[jax.ShapeDtypeStruct((M, D), F32), jax.ShapeDtypeStruct((NSH, M, FSH), BF16),
                   jax.ShapeDtypeStruct((NSH, M, FSH), BF16)] + s_shapes,
        scratch_shapes=s_sems,
        compiler_params=_cparams(("arbitrary", "arbitrary")),
    )(xn, wg, wu, wd, *s_ins)
    return res[:3], res[3:]


def ffn_bwd_x(name, df, a, b, wg, wu, wd, side=None):
    M = df.shape[0]
    tm = _pick(M, 704)
    ts = tm // FFN_SUB
    s_ins, s_specs, s_shapes, s_sems = _side_parts(side)
    ns, nso = len(s_ins), len(s_shapes)

    def body(df_ref, a_ref, b_ref, wg_ref, wu_ref, wd_ref, *rest):
        dx_ref, da_ref, db_ref, hm_ref = rest[ns:ns + 4]
        _side_run(side, (M // tm) * NSH, rest[:ns], rest[ns + 4:ns + 4 + nso], rest[ns + 4 + nso:],
                  step=pl.program_id(0) * NSH + pl.program_id(1))

        @pl.when(pl.program_id(1) == 0)
        def _():
            dx_ref[...] = jnp.zeros_like(dx_ref)

        for r in range(FFN_SUB):
            rows = pl.ds(r * ts, ts)
            a_ = a_ref[rows, :].astype(F32)
            b_ = b_ref[rows, :].astype(F32)
            dh = _mm_nt(df_ref[rows, :], wd_ref[...], None)
            sig = _sigmoid(a_)
            sil = a_ * sig
            da = (dh * b_ * (sig * (1.0 + a_ * (1.0 - sig)))).astype(BF16)
            db = (dh * sil).astype(BF16)
            dx_ref[rows, :] += _mm(da, wg_ref[...], None) + _mm(db, wu_ref[...], None)
            da_ref[rows, :] = da
            db_ref[rows, :] = db
            hm_ref[rows, :] = (sil * b_).astype(BF16)

    wspec = wdspec = pl.BlockSpec((None, FSH, D), lambda i, s: (s, 0, 0))
    abspec = pl.BlockSpec((None, tm, FSH), lambda i, s: (s, i, 0))
    ab = jax.ShapeDtypeStruct((NSH, M, FSH), BF16)
    res = pl.pallas_call(
        body, name=name, grid=(M // tm, NSH),
        in_specs=[pl.BlockSpec((tm, D), lambda i, s: (i, 0)), abspec, abspec, wspec, wspec, wdspec] + s_specs,
        out_specs=[pl.BlockSpec((tm, D), lambda i, s: (i, 0)), abspec, abspec, abspec] + [pl.BlockSpec(memory_space=pl.ANY)] * nso,
        out_shape=[jax.ShapeDtypeStruct((M, D), F32), ab, ab, ab] + s_shapes,
        scratch_shapes=s_sems,
        compiler_params=_cparams(("arbitrary", "arbitrary")),
    )(df, a, b, wg, wu, wd, *s_ins)
    return res[:4], res[4:]


def ffn_bwd_w(name, xn, df, da, db, hm):
    M = xn.shape[0]
    tm = _pick(M, 1408)
    nt = M // tm

    def body(x_ref, df_ref, da_ref, db_ref, hm_ref, dwg_ref, dwu_ref, dwd_ref, ag, au, ad):
        i = pl.program_id(1)
        x = x_ref[...]
        g = _mm_tn(da_ref[...], x, None)
        u = _mm_tn(db_ref[...], x, None)
        d = _mm_tn(hm_ref[...], df_ref[...], None)

        @pl.when(i == 0)
        def _():
            ag[...] = g
            au[...] = u
            ad[...] = d

        @pl.when(i > 0)
        def _():
            ag[...] += g
            au[...] += u
            ad[...] += d

        @pl.when(i == nt - 1)
        def _():
            dwg_ref[...] = ag[...].astype(BF16)
            dwu_ref[...] = au[...].astype(BF16)
            dwd_ref[...] = ad[...].astype(BF16)

    xspec = pl.BlockSpec((tm, D), lambda s, i: (i, 0))
    abspec = pl.BlockSpec((None, tm, FSH), lambda s, i: (s, i, 0))
    return pl.pallas_call(
        body, name=name, grid=(NSH, nt),
        in_specs=[xspec, xspec, abspec, abspec, abspec],
        out_specs=[pl.BlockSpec((None, FSH, D), lambda s, i: (s, 0, 0))] * 3,
        out_shape=[jax.ShapeDtypeStruct((NSH, FSH, D), BF16)] * 3,
        scratch_shapes=[pltpu.VMEM((FSH, D), F32)] * 3,
        compiler_params=_cparams(("parallel", "arbitrary")),
    )(xn, df, da, db, hm)


def _t5_bucket_np(rel):
    n = np.maximum(rel, 0)
    n_f = np.maximum(n, 1).astype(np.float32)
    large = 16 + (np.log(n_f / np.float32(16)) / np.float32(math.log(8.0)) * np.float32(16)).astype(np.int32)
    large = np.minimum(large, 31)
    return np.where(n < 16, n, large).astype(np.int32)


def _swa_bucket_ids():
    qi = np.arange(BLK)[:, None]
    kj = np.arange(BLK)[None, :]
    out = np.full((3, BLK, 3 * BLK), -1, np.int32)
    for v in range(3):
        pos_q = v * BLK + qi - ZROWS
        rel_m = pos_q - (kj - ZROWS)
        ok_m = (kj >= ZROWS) & (rel_m >= 0) & (pos_q >= 0)
        out[v, :, 0:BLK] = np.where(ok_m, _t5_bucket_np(rel_m), -1)
        pos_kp = (v - 1) * BLK + kj - ZROWS
        rel_p = BLK + qi - kj
        ok_p = (pos_kp >= N_META) & (rel_p >= 0) & (rel_p < BLK) & np.full_like(ok_m, v >= 1)
        out[v, :, BLK:2 * BLK] = np.where(ok_p, _t5_bucket_np(rel_p), -1)
        pos_kc = v * BLK + kj - ZROWS
        rel_c = qi - kj
        ok_c = (pos_kc >= N_META) & (rel_c >= 0) & (rel_c < BLK)
        out[v, :, 2 * BLK:] = np.where(ok_c, _t5_bucket_np(rel_c), -1)
    return out


def swa_bias_fwd(table, ids):
    def body(t_ref, id_ref, o_ref):
        for v in range(3):
            for h in range(8):
                o_ref[v, h] = jnp.where(id_ref[v] < 0, NEG, 0.0)

            def step(b, carry):
                hit = id_ref[v] == b
                for h in range(8):
                    o_ref[v, h] += jnp.where(hit, t_ref[b, h], 0.0)
                return carry

            lax.fori_loop(0, 32, step, 0)

    return pl.pallas_call(
        body, name="swa_bias_fwd",
        in_specs=[pl.BlockSpec(memory_space=pltpu.SMEM), pl.BlockSpec(memory_space=pltpu.VMEM)],
        out_specs=pl.BlockSpec(memory_space=pltpu.VMEM),
        out_shape=jax.ShapeDtypeStruct((3, 8, BLK, 3 * BLK), F32),
        compiler_params=pltpu.CompilerParams(vmem_limit_bytes=VMEM_LIMIT),
    )(table, ids)


def swa_bias_bwd(dbias, ids):
    def body(d_ref, id_ref, o_ref):
        r = lax.broadcasted_iota(jnp.int32, (32, LANE), 0)
        c = lax.broadcasted_iota(jnp.int32, (32, LANE), 1)

        def step(b, acc):
            for v in range(3):
                hit = id_ref[v] == b
                for h in range(8):
                    m = jnp.where(hit, d_ref[v, h], 0.0)
                    s = jnp.sum(jnp.sum(m, axis=1, keepdims=True), axis=0, keepdims=True)
                    acc = acc + jnp.where((r == b) & (c == h), s, 0.0)
            return acc

        o_ref[...] = lax.fori_loop(0, 32, step, jnp.zeros((32, LANE), F32))

    return pl.pallas_call(
        body, name="swa_bias_bwd",
        in_specs=[pl.BlockSpec(memory_space=pltpu.VMEM), pl.BlockSpec(memory_space=pltpu.VMEM)],
        out_specs=pl.BlockSpec(memory_space=pltpu.VMEM),
        out_shape=jax.ShapeDtypeStruct((32, LANE), F32),
        compiler_params=pltpu.CompilerParams(vmem_limit_bytes=VMEM_LIMIT),
    )(dbias, ids)


def _swa_block(q, k3, v3, bias, sinks):
    heads = range(8)
    kh = [k3[:, (h // 4) * 128:(h // 4 + 1) * 128] for h in heads]
    vh = [v3[:, (h // 4) * 128:(h // 4 + 1) * 128] for h in heads]
    s = [_bmm_nt(q[:, h * 128:(h + 1) * 128], kh[h]) * 0.125 + bias[h] for h in heads]
    sink = [_lane_pick(sinks, h) for h in heads]
    m = [lax.stop_gradient(jnp.maximum(jnp.max(s[h], axis=-1, keepdims=True), sink[h])) for h in heads]
    e = [jnp.exp(s[h] - m[h]) for h in heads]
    p = [e[h] / (jnp.sum(e[h], axis=-1, keepdims=True) + jnp.exp(sink[h] - m[h])) for h in heads]
    return jnp.concatenate([_bmm(p[h], vh[h]) for h in heads], axis=1)


def _swa_in_specs():
    qs = pl.BlockSpec((BLK, 1024), lambda n: (n, E_QA // 1024))
    ks = [pl.BlockSpec((BLK, 256), lambda n: (0, E_KA // 256)),
          pl.BlockSpec((BLK, 256), lambda n: (jnp.maximum(n - 1, 0), E_KA // 256)),
          pl.BlockSpec((BLK, 256), lambda n: (n, E_KA // 256))]
    vs = [pl.BlockSpec((BLK, 256), lambda n: (0, E_VA // 256)),
          pl.BlockSpec((BLK, 256), lambda n: (jnp.maximum(n - 1, 0), E_VA // 256)),
          pl.BlockSpec((BLK, 256), lambda n: (n, E_VA // 256))]
    bs = pl.BlockSpec((None, 8, BLK, 3 * BLK), lambda n: (jnp.minimum(n, 2), 0, 0, 0))
    ss = pl.BlockSpec((1, LANE), lambda n: (0, 0))
    return [qs] + ks + vs + [bs, ss]


def swa_fwd(proj, bias, sinks, side=None):
    M = proj.shape[0]
    s_ins, s_specs, s_shapes, s_sems = _side_parts(side)
    ns, nso = len(s_ins), len(s_shapes)

    def body(q_ref, k0, k1, k2, v0, v1, v2, b_ref, s_ref, *rest):
        o_ref = rest[ns]
        _side_run(side, M // BLK, rest[:ns], rest[ns + 1:ns + 1 + nso], rest[ns + 1 + nso:])
        k3 = jnp.concatenate([k0[...], k1[...], k2[...]], axis=0)
        v3 = jnp.concatenate([v0[...], v1[...], v2[...]], axis=0)
        o_ref[...] = _swa_block(q_ref[...], k3, v3, b_ref[...], s_ref[...]).astype(o_ref.dtype)

    res = pl.pallas_call(
        body, name="swa_fwd", grid=(M // BLK,),
        in_specs=_swa_in_specs() + s_specs,
        out_specs=[pl.BlockSpec((BLK, 1024), lambda n: (n, 0))] + [pl.BlockSpec(memory_space=pl.ANY)] * nso,
        out_shape=[jax.ShapeDtypeStruct((M, 1024), BF16)] + s_shapes,
        scratch_shapes=s_sems,
        compiler_params=_cparams(("arbitrary",)),
    )(proj, proj, proj, proj, proj, proj, proj, bias, sinks, *s_ins)
    return res[0], res[1:]


def swa_bwd(proj, bias, sinks, do, side=None):
    M = proj.shape[0]
    s_ins, s_specs, s_shapes, s_sems = _side_parts(side)
    ns, nso = len(s_ins), len(s_shapes)

    def body(q_ref, k0, k1, k2, v0, v1, v2, b_ref, s_ref, do_ref, *rest):
        dq_ref, dk_ref, dv_ref, db_ref, ds_ref = rest[ns:ns + 5]
        _side_run(side, M // BLK, rest[:ns], rest[ns + 5:ns + 5 + nso], rest[ns + 5 + nso:])
        n = pl.program_id(0)

        @pl.when(n == 0)
        def _():
            dk_ref[...] = jnp.zeros_like(dk_ref)
            dv_ref[...] = jnp.zeros_like(dv_ref)
            ds_ref[...] = jnp.zeros_like(ds_ref)

        k3 = jnp.concatenate([k0[...], k1[...], k2[...]], axis=0)
        v3 = jnp.concatenate([v0[...], v1[...], v2[...]], axis=0)
        _, vjp = jax.vjp(_swa_block, q_ref[...], k3, v3, b_ref[...], s_ref[...])
        dq, dk3, dv3, dbias, dsink = vjp(do_ref[...].astype(F32))
        dq_ref[...] = dq
        prev = pl.multiple_of(jnp.maximum(n - 1, 0) * BLK, BLK)
        cur = pl.multiple_of(n * BLK, BLK)
        dk_ref[pl.ds(0, BLK), :] += dk3[0:BLK]
        dv_ref[pl.ds(0, BLK), :] += dv3[0:BLK]
        dk_ref[pl.ds(prev, BLK), :] += dk3[BLK:2 * BLK]
        dv_ref[pl.ds(prev, BLK), :] += dv3[BLK:2 * BLK]
        dk_ref[pl.ds(cur, BLK), :] += dk3[2 * BLK:]
        dv_ref[pl.ds(cur, BLK), :] += dv3[2 * BLK:]
        ds_ref[...] += dsink

        @pl.when(n <= 2)
        def _():
            db_ref[...] = dbias

        @pl.when(n > 2)
        def _():
            db_ref[...] += dbias

    res = pl.pallas_call(
        body, name="swa_bwd", grid=(M // BLK,),
        in_specs=_swa_in_specs() + [pl.BlockSpec((BLK, 1024), lambda n: (n, 0))] + s_specs,
        out_specs=[pl.BlockSpec((BLK, 1024), lambda n: (n, 0)),
                   pl.BlockSpec((M, 256), lambda n: (0, 0)), pl.BlockSpec((M, 256), lambda n: (0, 0)),
                   pl.BlockSpec((None, 8, BLK, 3 * BLK), lambda n: (jnp.minimum(n, 2), 0, 0, 0)),
                   pl.BlockSpec((1, LANE), lambda n: (0, 0))] + [pl.BlockSpec(memory_space=pl.ANY)] * nso,
        out_shape=[jax.ShapeDtypeStruct((M, 1024), F32), jax.ShapeDtypeStruct((M, 256), F32),
                   jax.ShapeDtypeStruct((M, 256), F32), jax.ShapeDtypeStruct((3, 8, BLK, 3 * BLK), F32),
                   jax.ShapeDtypeStruct((1, LANE), F32)] + s_shapes,
        scratch_shapes=s_sems,
        compiler_params=_cparams(("arbitrary",)),
    )(proj, proj, proj, proj, proj, proj, proj, bias, sinks, do, *s_ins)
    return res[:5], res[5:]


def _shift_rows_impl(x, k):
    n = x.shape[0]
    rolled = pltpu.roll(x, k, 0)
    return jnp.where(_row_ids(0, n) >= k, rolled, 0.0)


def _unshift_rows_impl(g, k):
    n = g.shape[0]
    rolled = pltpu.roll(g, n - k, 0)
    return jnp.where(_row_ids(0, n) < n - k, rolled, 0.0)


@functools.partial(jax.custom_vjp, nondiff_argnums=(1,))
def _shift_rows(x, k):
    return _shift_rows_impl(x, k)


def _shift_rows_f(x, k):
    return _shift_rows_impl(x, k), None


def _shift_rows_b(k, _, g):
    return (_unshift_rows_impl(g, k),)


_shift_rows.defvjp(_shift_rows_f, _shift_rows_b)


def _conv_silu(x, w):
    rid = lax.broadcasted_iota(jnp.int32, w.shape, 0)
    y = x * jnp.sum(jnp.where(rid == 3, w, 0.0), axis=0, keepdims=True)
    for k in range(1, 4):
        y = y + _shift_rows(x, k) * jnp.sum(jnp.where(rid == 3 - k, w, 0.0), axis=0, keepdims=True)
    y = jnp.where(_row_ids(0, x.shape[0]) >= ZROWS, y, 0.0)
    return _silu(y)


def conv_fwd(proj, conv_w):
    M = proj.shape[0]
    nb = conv_w.shape[1] // LANE

    def body(x_ref, w_ref, o_ref):
        o_ref[...] = _conv_silu(x_ref[...], w_ref[...])

    return pl.pallas_call(
        body, name="conv_fwd", grid=(nb,),
        in_specs=[pl.BlockSpec((M, LANE), lambda c: (0, E_QB // LANE + c)), pl.BlockSpec((4, LANE), lambda c: (0, c))],
        out_specs=pl.BlockSpec((M, LANE), lambda c: (0, c)),
        out_shape=jax.ShapeDtypeStruct((M, conv_w.shape[1]), F32),
        compiler_params=_cparams(("parallel",)),
    )(proj, conv_w)


def conv_bwd(proj, conv_w, dy):
    M = proj.shape[0]
    nb = conv_w.shape[1] // LANE

    def body(x_ref, w_ref, dy_ref, dx_ref, dw_ref):
        _, vjp = jax.vjp(_conv_silu, x_ref[...], w_ref[...])
        dx, dw = vjp(dy_ref[...])
        dx_ref[...] = dx
        dw_ref[...] = dw

    return pl.pallas_call(
        body, name="conv_bwd", grid=(nb,),
        in_specs=[pl.BlockSpec((M, LANE), lambda c: (0, E_QB // LANE + c)), pl.BlockSpec((4, LANE), lambda c: (0, c)),
                  pl.BlockSpec((M, LANE), lambda c: (0, c))],
        out_specs=[pl.BlockSpec((M, LANE), lambda c: (0, c)), pl.BlockSpec((4, LANE), lambda c: (0, c))],
        out_shape=[jax.ShapeDtypeStruct((M, conv_w.shape[1]), F32), jax.ShapeDtypeStruct(conv_w.shape, F32)],
        compiler_params=_cparams(("parallel",)),
    )(proj, conv_w, dy)


def _fn_dn_prep(row0, yq, yk, ba, dnp):
    tm = yq.shape[0]
    real = _row_ids(row0, tm) >= ZROWS
    qs, ks, gs, bs = [], [], [], []
    for h in range(4):
        q = yq[:, h * 128:(h + 1) * 128]
        k = yk[:, h * 128:(h + 1) * 128]
        qs.append(q * lax.rsqrt(jnp.sum(q * q, axis=-1, keepdims=True) + 1e-6) * (128.0 ** -0.5))
        ks.append(k * lax.rsqrt(jnp.sum(k * k, axis=-1, keepdims=True) + 1e-6))
        beta = _sigmoid(_lane_pick(ba, h))
        g = -jnp.exp(_lane_pick(dnp, h)) * _softplus(_lane_pick(ba, 4 + h) + _lane_pick(dnp, 4 + h))
        g = jnp.where(real, g, 0.0)
        gs.append(jnp.broadcast_to(g, (tm, 128)))
        bs.append(jnp.broadcast_to(beta, (tm, 128)))
    cat = lambda xs: jnp.concatenate(xs, axis=1)
    return cat(qs), cat(ks), cat(gs), cat(bs)


def _zip(f, *lists):
    return [f(*args) for args in zip(*lists)]


def _unit_lower_inv_impl(a):
    n = a[0].shape[0]
    eye = (lax.broadcasted_iota(jnp.int32, (n, n), 0) == lax.broadcasted_iota(jnp.int32, (n, n), 1)).astype(F32)
    nn = ((1,), (0,))
    p = [-x for x in a]
    t = [eye + x for x in p]
    for _ in range(int(math.log2(n)) - 1):
        p = _zip(lambda x: _xdot(x, x, nn), p)
        t = _zip(lambda x, y: x + _xdot(x, y, nn), t, p)
    return t


@jax.custom_vjp
def _unit_lower_inv(a):
    return _unit_lower_inv_impl(a)


def _unit_lower_inv_f(a):
    t = _unit_lower_inv_impl(a)
    return t, t


def _unit_lower_inv_b(t, g):
    tg = _zip(lambda x, y: _xdot(x, y, ((0,), (0,))), t, g)
    return (_zip(lambda x, y: -_xdot(x, y, ((1,), (1,))), tg, t),)


_unit_lower_inv.defvjp(_unit_lower_inv_f, _unit_lower_inv_b)


@jax.custom_vjp
def _known_inv(a, t):
    return t


_known_inv.defvjp(lambda a, t: (t, t), lambda t, g: (_unit_lower_inv_b(t, g)[0], [jnp.zeros_like(x) for x in t]))


def _dn_block(q, k, v, gb, bb, S, t_kept=None):
    nh = len(S)
    r = lax.broadcasted_iota(jnp.int32, (CH, CH), 0)
    c = lax.broadcasted_iota(jnp.int32, (CH, CH), 1)
    tri_incl = r >= c
    gcb = _zip(_cumsum_rows, gb)
    gamma = _zip(lambda x: jnp.where(tri_incl, jnp.exp(jnp.where(tri_incl, x[:, :CH] - x[:, :CH].T, 0.0)), 0.0), gcb)
    kb = _zip(jnp.multiply, k, bb)
    vb = _zip(jnp.multiply, v, bb)
    a = _zip(lambda m, g: jnp.where(r > c, m * g, 0.0), _zip(_bmm_nt, kb, k), gamma)
    t = _unit_lower_inv(a) if t_kept is None else _known_inv(a, list(t_kept))
    eg = _zip(jnp.exp, gcb)
    u = _zip(_xmm, t, vb)
    w = _zip(_xmm, t, _zip(jnp.multiply, kb, eg))
    attn = _zip(lambda m, g: m * g, _zip(_bmm_nt, q, k), gamma)
    gtot = _zip(lambda x: jnp.sum(x, axis=0, keepdims=True), gb)
    k_dec = _zip(lambda x, gt, gc: x * jnp.exp(gt - gc), k, gtot, gcb)
    q_dec = _zip(jnp.multiply, q, eg)
    S = list(S)
    o, starts = [], []
    for i0 in range(0, len(q), nh):
        idx = range(i0, i0 + nh)
        starts.append(list(S))
        v_new = [u[i] - m for i, m in zip(idx, [_bmm(w[i], S[h]) for h, i in enumerate(idx)])]
        oq = [_bmm(q_dec[i], S[h]) for h, i in enumerate(idx)]
        oa = [_bmm(attn[i], vn) for i, vn in zip(idx, v_new)]
        kv = [_bmm_tn(k_dec[i], vn) for i, vn in zip(idx, v_new)]
        o += _zip(jnp.add, oq, oa)
        S = [S[h] * jnp.exp(jnp.broadcast_to(gtot[i], S[h].shape)) + kv[h] for h, i in enumerate(idx)]
    return o, S, starts, t


def _gla_block(q, k, v, glog, S):
    nh = len(S)
    tri = lax.broadcasted_iota(jnp.int32, (CH, CH), 0) >= lax.broadcasted_iota(jnp.int32, (CH, CH), 1)
    bcum = _zip(_cumsum_rows, glog)
    q_dec = _zip(lambda x, b: x * (128.0 ** -0.5) * jnp.exp(b), q, bcum)
    attn = _zip(lambda m: jnp.where(tri, m, 0.0), _zip(_bmm_nt, q_dec, _zip(lambda x, b: x * jnp.exp(-b), k, bcum)))
    o_in = _zip(_bmm, attn, v)
    k_dec = _zip(lambda x, g, b: x * jnp.exp(jnp.sum(g, axis=0, keepdims=True) - b), k, glog, bcum)
    decay = _zip(lambda g, x: jnp.exp(_colsum_as_rows(g, x.shape[1])), glog, v)
    kv = _zip(_bmm_tn, k_dec, v)
    S = list(S)
    o, starts = [], []
    for i0 in range(0, len(q), nh):
        idx = range(i0, i0 + nh)
        starts.append(list(S))
        o += [o_in[i] + m for i, m in zip(idx, [_bmm(q_dec[i], S[h]) for h, i in enumerate(idx)])]
        S = [S[h] * decay[i] + kv[i] for h, i in enumerate(idx)]
    return o, S, starts


class Side:
    def __init__(self, ins, out_shapes, nsem, events):
        self.ins, self.out_shapes, self.nsem, self.events = list(ins), list(out_shapes), nsem, events


def _side_parts(side):
    if side is None:
        return [], [], [], []
    anyspec = pl.BlockSpec(memory_space=pl.ANY)
    return (side.ins, [anyspec] * len(side.ins), side.out_shapes,
            [pltpu.SemaphoreType.DMA((side.nsem,)), pltpu.SemaphoreType.DMA((side.nsem,))])


def _side_run(side, n_steps, in_refs, out_refs, sems, step=None):
    if side is None:
        return
    step = pl.program_id(0) if step is None else step
    for at, fn in side.events(n_steps, in_refs, out_refs, *sems):
        pl.when(step == at)(fn)


def _chunks_per_step(n_chunks):
    return next(c for c in (CPS, 2, 1) if n_chunks % c == 0)


def chunk_fwd(name, chunk_fn, ins, dv, side=None, keep=()):
    M = ins[0][0].shape[0]
    NC = M // CH
    cps = _chunks_per_step(NC)
    N = NC // cps
    ni, nk = len(ins), len(keep)
    ws = [w for (_, w, _) in ins]
    s_ins, s_specs, s_shapes, s_sems = _side_parts(side)
    ns, nso = len(s_ins), len(s_shapes)

    def body(*refs):
        o0 = ni + ns
        o_ref, sall_ref = refs[o0:o0 + 2]
        k_refs = refs[o0 + 2:o0 + 2 + nk]
        s_ref = refs[o0 + 2 + nk + nso]
        _side_run(side, N, refs[ni:o0], refs[o0 + 2 + nk:o0 + 2 + nk + nso], refs[o0 + 3 + nk + nso:])

        @pl.when(pl.program_id(0) == 0)
        def _():
            s_ref[...] = jnp.zeros_like(s_ref)

        problems = [(cc, h) for cc in range(cps) for h in range(4)]
        lists = [[r[cc * CH:(cc + 1) * CH, h * w:(h + 1) * w] for cc, h in problems] for r, w in zip(refs[:ni], ws)]
        o, s_new, starts, *kept = chunk_fn(*lists, [s_ref[h] for h in range(4)])
        for b, (cc, h) in enumerate(problems):
            o_ref[cc * CH:(cc + 1) * CH, h * dv:(h + 1) * dv] = o[b]
            sall_ref[h, cc] = starts[cc][h]
            for k_ref, vals in zip(k_refs, kept):
                k_ref[h, cc] = vals[b]
        for h in range(4):
            s_ref[h] = s_new[h]

    per_chunk = lambda r, c: pl.BlockSpec((4, cps, r, c), lambda n: (0, n, 0, 0))
    specs = [pl.BlockSpec((cps * CH, 4 * w), functools.partial(lambda n, cb: (n, cb), cb=cb // 4)) for (_, w, cb) in ins]
    res = pl.pallas_call(
        body, name=name, grid=(N,),
        in_specs=specs + s_specs,
        out_specs=[pl.BlockSpec((cps * CH, 4 * dv), lambda n: (n, 0)), per_chunk(128, dv)] + [per_chunk(r, c) for r, c in keep]
        + [pl.BlockSpec(memory_space=pl.ANY)] * nso,
        out_shape=[jax.ShapeDtypeStruct((M, 4 * dv), F32), jax.ShapeDtypeStruct((4, NC, 128, dv), F32)]
        + [jax.ShapeDtypeStruct((4, NC, r, c), F32) for r, c in keep] + s_shapes,
        scratch_shapes=[pltpu.VMEM((4, 128, dv), F32)] + s_sems,
        compiler_params=_cparams(("arbitrary",)),
    )(*[a for (a, _, _) in ins], *s_ins)
    return res[0], res[1], res[2:2 + nk], res[2 + nk:]


def chunk_bwd(name, chunk_fn, ins, dv, s_all, do, side=None, kept=()):
    M = ins[0][0].shape[0]
    cps = _chunks_per_step(M // CH)
    N = M // CH // cps
    ni, nk = len(ins), len(kept)
    ws = [w for (_, w, _) in ins]
    s_ins, s_specs, s_shapes, s_sems = _side_parts(side)
    ns, nso = len(s_ins), len(s_shapes)

    def body(*refs):
        sall_ref, do_ref = refs[ni:ni + 2]
        k_refs = refs[ni + 2:ni + 2 + nk]
        o0 = ni + 2 + nk + ns
        d_refs = refs[o0:o0 + ni]
        ds_ref = refs[o0 + ni + nso]
        _side_run(side, N, refs[ni + 2 + nk:o0], refs[o0 + ni:o0 + ni + nso], refs[o0 + ni + nso + 1:])

        @pl.when(pl.program_id(0) == 0)
        def _():
            ds_ref[...] = jnp.zeros_like(ds_ref)

        problems = [(cc, h) for cc in range(cps) for h in range(4)]
        lists = [[r[cc * CH:(cc + 1) * CH, h * w:(h + 1) * w] for cc, h in problems] for r, w in zip(refs[:ni], ws)]
        kept_lists = [[k_ref[h, cc] for cc, h in problems] for k_ref in k_refs]
        _, vjp = jax.vjp(lambda *a: tuple(chunk_fn(*a)[:2]), *lists, [sall_ref[h, 0] for h in range(4)], *kept_lists)
        grads = vjp(([do_ref[cc * CH:(cc + 1) * CH, h * dv:(h + 1) * dv] for cc, h in problems],
                     [ds_ref[h] for h in range(4)]))
        for d_ref, w, g in zip(d_refs, ws, grads[:ni]):
            for b, (cc, h) in enumerate(problems):
                d_ref[cc * CH:(cc + 1) * CH, h * w:(h + 1) * w] = g[b]
        for h in range(4):
            ds_ref[h] = grads[ni][h]

    rev = lambda n: N - 1 - n
    per_chunk = lambda r, c: pl.BlockSpec((4, cps, r, c), lambda n: (0, rev(n), 0, 0))
    specs = [pl.BlockSpec((cps * CH, 4 * w), functools.partial(lambda n, cb: (rev(n), cb), cb=cb // 4)) for (_, w, cb) in ins]
    res = pl.pallas_call(
        body, name=name, grid=(N,),
        in_specs=specs + [per_chunk(128, dv), pl.BlockSpec((cps * CH, 4 * dv), lambda n: (rev(n), 0))]
        + [per_chunk(*a.shape[2:]) for a in kept] + s_specs,
        out_specs=[pl.BlockSpec((cps * CH, 4 * w), lambda n: (rev(n), 0)) for w in ws] + [pl.BlockSpec(memory_space=pl.ANY)] * nso,
        out_shape=[jax.ShapeDtypeStruct((M, 4 * w), F32) for w in ws] + s_shapes,
        scratch_shapes=[pltpu.VMEM((4, 128, dv), F32)] + s_sems,
        compiler_params=_cparams(("arbitrary",)),
    )(*[a for (a, _, _) in ins], s_all, do, *kept, *s_ins)
    return res[:ni], res[ni:]


def _fn_gate_out(hd, row0, o, z, w):
    outs = []
    for h in range(4):
        outs.append(_rms(o[:, h * hd:(h + 1) * hd], w) * _silu(z[:, h * hd:(h + 1) * hd]))
    return (jnp.concatenate(outs, axis=1),)


def _fn_gla_prep(row0, gk, wgu, bg):
    x = _mm(gk, wgu) + bg
    ls = jnp.minimum(x, 0.0) - jnp.log(1.0 + jnp.exp(-jnp.abs(x)))
    return (jnp.where(_row_ids(row0, gk.shape[0]) >= ZROWS, ls / 16.0, 0.0),)


def loss_call(y, tgt):
    M = y.shape[0]
    tm = _pick(M, 512)

    def body(y_ref, t_ref, l_ref, dy_ref):
        i = pl.program_id(0)
        e = y_ref[...] - t_ref[...]
        dy_ref[...] = e * (1.0 / D)
        part = 0.5 * jnp.sum(jnp.sum(e * e, axis=1, keepdims=True) * (1.0 / D), axis=0, keepdims=True)
        part = jnp.broadcast_to(part, (8, LANE))

        @pl.when(i == 0)
        def _():
            l_ref[...] = part

        @pl.when(i > 0)
        def _():
            l_ref[...] += part

    return pl.pallas_call(
        body, name="loss", grid=(M // tm,),
        in_specs=[pl.BlockSpec((tm, D), lambda i: (i, 0))] * 2,
        out_specs=[pl.BlockSpec((8, LANE), lambda i: (0, 0)), pl.BlockSpec((tm, D), lambda i: (i, 0))],
        out_shape=[jax.ShapeDtypeStruct((8, LANE), F32), jax.ShapeDtypeStruct((M, D), F32)],
        compiler_params=_cparams(("arbitrary",)),
    )(y, tgt)


def _bf(x):
    return x.astype(BF16)


def core_step(x, tgt, W, comm=None):
    S = x.shape[0]
    M = S + PADR
    ids = jnp.asarray(_swa_bucket_ids())
    h0 = jnp.concatenate([jnp.zeros((ZROWS, D), F32), W["meta"], x], axis=0)
    nw = W["norm"]
    nrow = lambda l, k: nw[l, k][None, :]
    ffw = W["ffn"]
    projs = {}

    def projw(l):
        if l not in projs:
            projs[l] = W["proj"](l)
        return projs[l]

    site = (lambda name: comm.side(name)) if comm else (lambda name: None)
    landed = (lambda name, outs: comm.done(name, outs)) if comm else (lambda name, outs: None)
    sinks = jnp.pad(W["sinks"], ((0, 0), (0, LANE - 8)))
    dnp = jnp.pad(jnp.concatenate([W["a_log"], W["dt_bias"]], axis=1), ((0, 0), (0, LANE - 8)))
    wgu = jnp.pad(W["gate_up"], ((0, LANE - 16), (0, 0)))
    bg = W["b_gate"]
    full = lambda a: (a, a.shape[1], 0)

    saved = []
    h = h0
    (hn,) = rowwise_fwd("prenorm_0", _fn_prenorm, [full(h)], [nrow(0, 0)], [(D, BF16)])
    bias = swa_bias_fwd(W["rel"], ids)
    for l in range(2):
        st = {"h_a": h, "hn_a": hn}
        (f1, a1, b1), got = ffn_fwd(f"ffn_fwd_{l}0", hn, *ffw(l, 0), side=site(f"ffn_fwd_{l}0"))
        landed(f"ffn_fwd_{l}0", got)
        h, hn = rowwise_fwd(f"resnorm_{l}1", functools.partial(_fn_resnorm, 0.5), [full(h), full(f1)],
                            [nrow(l, 1), nrow(l, 2)], [(D, F32), (D, BF16)])
        st.update(f1=f1, a1=a1, b1=b1, h_b=h, hn_b=hn)
        if l == 0:
            proj = mm_nt(hn, projw(0)["w_in"], "e_proj")
            o_a, got = swa_fwd(proj, bias, sinks, side=site("swa_fwd"))
            landed("swa_fwd", got)
            y = conv_fwd(proj, W["conv"])
            qn, kn, gb, bb = rowwise_fwd(
                "dn_prep", _fn_dn_prep, [(y, 512, 0), (y, 512, 1), (proj, LANE, E_BA // LANE)], [dnp], [(512, F32)] * 4)
            ins = [(qn, 128, 0), (kn, 128, 0), (y, 128, 8), (gb, 128, 0), (bb, 128, 0)]
            o_dn, s_all, (t_inv,), got = chunk_fwd("dn_fwd", _dn_block, ins, 128, keep=[(CH, CH)],
                                                   side=site("dn_fwd"))
            landed("dn_fwd", got)
            (o_b,) = rowwise_fwd("dn_out", functools.partial(_fn_gate_out, 128),
                                 [full(o_dn), (proj, 512, E_ZB // 512)], [W["dn_norm"]], [(512, BF16)])
            omix = jnp.concatenate([o_a, o_b], axis=1)
            mix = mm_nn(omix, projw(0)["w_out"], "e_mix")
            st.update(proj=proj, y=y, qn=qn, kn=kn, gb=gb, bb=bb, o_dn=o_dn, s_all=s_all, t_inv=t_inv, omix=omix)
        else:
            proj = mm_nt(hn, projw(1)["w_in"], "o_proj")
            (glog,) = rowwise_fwd("gla_prep", _fn_gla_prep, [(proj, LANE, O_GK // LANE)], [wgu, bg], [(512, F32)])
            ins = [(proj, 128, O_Q // 128), (proj, 128, O_K // 128), (proj, 256, O_V // 256), (glog, 128, 0)]
            o_g, s_all, _, _ = chunk_fwd("gla_fwd", _gla_block, ins, 256)
            (omix,) = rowwise_fwd("gla_out", functools.partial(_fn_gate_out, 256),
                                  [full(o_g), (proj, 1024, O_G // 1024)], [W["gla_norm"]], [(1024, BF16)])
            mix = mm_nn(omix, projw(1)["w_out"], "o_mix")
            st.update(proj=proj, glog=glog, o_g=o_g, s_all=s_all, omix=omix)
        h, hn = rowwise_fwd(f"resnorm_{l}3", functools.partial(_fn_resnorm, 1.0), [full(h), full(mix)],
                            [nrow(l, 3), nrow(l, 4)], [(D, F32), (D, BF16)])
        st.update(mix=mix, h_c=h, hn_c=hn)
        (f2, a2, b2), got = ffn_fwd(f"ffn_fwd_{l}1", hn, *ffw(l, 1), side=site(f"ffn_fwd_{l}1"))
        landed(f"ffn_fwd_{l}1", got)
        st.update(f2=f2, a2=a2, b2=b2)
        if l == 0:
            h, hn = rowwise_fwd("resnorm_05", functools.partial(_fn_resnorm, 0.5), [full(h), full(f2)],
                                [nrow(0, 5), nrow(1, 0)], [(D, F32), (D, BF16)])
        else:
            (h,) = rowwise_fwd("res_last", functools.partial(_fn_res_last, 0.5), [full(h), full(f2)],
                               [nrow(1, 5)], [(D, F32)])
        saved.append(st)

    loss_blk, dy = loss_call(h[PADR:], tgt)
    dh = jnp.concatenate([jnp.zeros((PADR, D), F32), dy], axis=0)

    G = {}
    dnorm = [[None] * 6 for _ in range(2)]
    dWg = [[None, None], [None, None]]
    dWu = [[None, None], [None, None]]
    dWd = [[None, None], [None, None]]
    dhn = None
    for l in (1, 0):
        st = saved[l]
        if l == 1:
            (dh_, df), (dw5,) = rowwise_bwd(
                "res_last_b", functools.partial(_fn_res_last, 0.5), [full(st["h_c"]), full(st["f2"])], [nrow(1, 5)],
                [full(dh)], [F32, BF16])
            dnorm[1][5] = dw5
        else:
            (dh_, df), (dw5, dw0n) = rowwise_bwd(
                "resnorm_05_b", functools.partial(_fn_resnorm, 0.5), [full(st["h_c"]), full(st["f2"])],
                [nrow(0, 5), nrow(1, 0)], [full(dh), full(dhn)], [F32, BF16])
            dnorm[0][5] = dw5
            dnorm[1][0] = dw0n
        dh = dh_
        if comm and l == 0:
            units = (0, 1, 2, U_IN, U_OUT)
            comm.grads(1, units, layer_grad_items(1, dWg, dWu, dWd, G["o_in"], G["o_out"], only=units))
        (dxn, da, db, hm), got = ffn_bwd_x(f"ffn_bx_{l}1", df, st["a2"], st["b2"], *ffw(l, 1), side=site(f"ffn_bx_{l}1"))
        landed(f"ffn_bx_{l}1", got)
        dWg[l][1], dWu[l][1], dWd[l][1] = ffn_bwd_w(f"ffn_bw_{l}1", st["hn_c"], df, da, db, hm)
        if comm:
            comm.grads(l, (3, 4, 5), layer_grad_items(l, dWg, dWu, dWd, None, None, only=(3, 4, 5)))
        (dh_, dmix), (dw3, dw4) = rowwise_bwd(
            f"resnorm_{l}3_b", functools.partial(_fn_resnorm, 1.0), [full(st["h_b"]), full(st["mix"])],
            [nrow(l, 3), nrow(l, 4)], [full(dh), full(dxn)], [F32, BF16])
        dnorm[l][3], dnorm[l][4] = dw3, dw4
        dh = dh_
        proj = st["proj"]
        if l == 1:
            G["o_out"] = mm_tn(st["omix"], dmix, "o_out_dw")
            domix = mm_nt(dmix, projw(1)["w_out"], "o_mix_dx")
            (do_g, dgate), (dgn,) = rowwise_bwd(
                "gla_out_b", functools.partial(_fn_gate_out, 256), [full(st["o_g"]), (proj, 1024, O_G // 1024)],
                [W["gla_norm"]], [full(domix)], [F32, F32])
            G["gla_norm"] = dgn
            ins = [(proj, 128, O_Q // 128), (proj, 128, O_K // 128), (proj, 256, O_V // 256), (st["glog"], 128, 0)]
            (dq, dk, dv, dglog), _ = chunk_bwd("gla_bwd", _gla_block, ins, 256, st["s_all"], do_g)
            (dgk,), (dwgu, dbg) = rowwise_bwd("gla_prep_b", _fn_gla_prep, [(proj, LANE, O_GK // LANE)], [wgu, bg],
                                              [full(dglog)], [F32])
            G["gate_up"] = dwgu[:16]
            G["b_gate"] = dbg
            dproj = _bf(jnp.concatenate([dq, dk, dv, dgate, dgk, jnp.zeros((M, O_END - O_GK - LANE), F32)], axis=1))
            G["o_in"] = mm_tn(dproj, st["hn_b"], "o_in_dw")
            dhn_b = mm_nn(dproj, projw(1)["w_in"], "o_proj_dx")
        else:
            G["e_out"] = mm_tn(st["omix"], dmix, "e_out_dw")
            domix = mm_nt(dmix, projw(0)["w_out"], "e_mix_dx")
            (do_dn, dz), (ddn,) = rowwise_bwd(
                "dn_out_b", functools.partial(_fn_gate_out, 128), [full(st["o_dn"]), (proj, 512, E_ZB // 512)],
                [W["dn_norm"]], [(domix, 512, 2)], [F32, F32])
            G["dn_norm"] = ddn
            ins = [(st["qn"], 128, 0), (st["kn"], 128, 0), (st["y"], 128, 8), (st["gb"], 128, 0), (st["bb"], 128, 0)]
            (dqn, dkn, dvv, dgb, dbb), got = chunk_bwd("dn_bwd", _dn_block, ins, 128, st["s_all"], do_dn, side=site("dn_bwd"),
                                                        kept=[st["t_inv"]])
            landed("dn_bwd", got)
            (dyq, dyk, dba), (ddnp,) = rowwise_bwd(
                "dn_prep_b", _fn_dn_prep, [(st["y"], 512, 0), (st["y"], 512, 1), (proj, LANE, E_BA // LANE)], [dnp],
                [full(dqn), full(dkn), full(dgb), full(dbb)], [F32, F32, F32])
            G["a_log"] = ddnp[:, 0:4]
            G["dt_bias"] = ddnp[:, 4:8]
            dyc = jnp.concatenate([dyq, dyk, dvv], axis=1)
            dxc, dconv = conv_bwd(proj, W["conv"], dyc)
            G["conv"] = dconv
            (dq_a, dk_a, dv_a, dbias, dsink), got = swa_bwd(proj, bias, sinks, domix, side=site("swa_bwd"))
            landed("swa_bwd", got)
            G["sinks"] = dsink[:, :8]
            G["rel"] = swa_bias_bwd(dbias, ids)[:, :8]
            dproj = _bf(jnp.concatenate([dq_a, dk_a, dv_a, dxc, dz, dba, jnp.zeros((M, E_END - E_BA - LANE), F32)], axis=1))
            G["e_in"] = mm_tn(dproj, st["hn_b"], "e_in_dw")
            dhn_b = mm_nn(dproj, projw(0)["w_in"], "e_proj_dx")
        (dh_, df), (dw1, dw2) = rowwise_bwd(
            f"resnorm_{l}1_b", functools.partial(_fn_resnorm, 0.5), [full(st["h_a"]), full(st["f1"])],
            [nrow(l, 1), nrow(l, 2)], [full(dh), full(dhn_b)], [F32, BF16])
        dnorm[l][1], dnorm[l][2] = dw1, dw2
        dh = dh_
        if comm and l == 0:
            units = (U_IN, U_OUT)
            comm.grads(0, units, layer_grad_items(0, dWg, dWu, dWd, G["e_in"], G["e_out"], only=units))
        (dxn, da, db, hm), got = ffn_bwd_x(f"ffn_bx_{l}0", df, st["a1"], st["b1"], *ffw(l, 0), side=site(f"ffn_bx_{l}0"))
        landed(f"ffn_bx_{l}0", got)
        dWg[l][0], dWu[l][0], dWd[l][0] = ffn_bwd_w(f"ffn_bw_{l}0", st["hn_a"], df, da, db, hm)
        dhn = dxn
    (dh0p,), (dw00,) = rowwise_bwd("prenorm_0_b", _fn_prenorm, [full(saved[0]["h_a"])], [nrow(0, 0)], [full(dhn)], [F32])
    dnorm[0][0] = dw00
    dh = dh + dh0p
    G["meta"] = dh[ZROWS:PADR]
    G["norm"] = jnp.stack([jnp.concatenate(r, axis=0) for r in dnorm], axis=0)
    G["items"] = [layer_grad_items(0, dWg, dWu, dWd, G["e_in"], G["e_out"], only=(0, 1, 2) if comm else range(NUNITS)),
                  None if comm else layer_grad_items(1, dWg, dWu, dWd, G["o_in"], G["o_out"])]
    return loss_blk, dh[PADR:], G


NAMES = [("meta", "meta_tokens"), ("norm", "norm_w"), ("ffn_g", "ffn_w_gate"), ("ffn_u", "ffn_w_up"),
         ("ffn_d", "ffn_w_down"), ("rel", "rel_bias_table"), ("e_in", "even_w_in"), ("conv", "even_conv_w"),
         ("sinks", "swa_sinks"), ("a_log", "dn_a_log"), ("dt_bias", "dn_dt_bias"), ("dn_norm", "dn_norm_w"),
         ("e_out", "even_w_out"), ("o_in", "odd_w_in"), ("gate_up", "gla_w_gate_up"), ("b_gate", "gla_b_gate"),
         ("gla_norm", "gla_norm_w"), ("o_out", "odd_w_out")]
BIG = ["ffn_g", "ffn_u", "ffn_d", "e_in", "e_out", "o_in", "o_out"]
IN_ROWS = 800
SMALL = [("meta", (16, 256)), ("norm", (2, 6, 256)), ("conv", (1, 4, 384)), ("gate_up", (1, 16, 128)),
         ("b_gate", (1, 128)), ("gla_norm", (1, 64))]
REPL = [("rel", (32, 8)), ("sinks", (1, 8)), ("a_log", (1, 4)), ("dt_bias", (1, 4)), ("dn_norm", (1, 128))]
SMALL_REP = 88 * LANE
SMALL_ROWS = 96


def pack_small(t):
    a = jnp.concatenate([t[n].reshape(-1) for n, _ in SMALL])
    b = jnp.concatenate([t[n].reshape(-1) for n, _ in REPL])
    flat = jnp.concatenate([a, jnp.zeros((SMALL_REP - a.shape[0],), F32), b,
                            jnp.zeros((SMALL_ROWS * LANE - SMALL_REP - b.shape[0],), F32)])
    return flat.reshape(SMALL_ROWS, LANE)


def unpack_small(p):
    flat = p.reshape(-1)
    out, r = {}, 0
    for n, shp in SMALL:
        k = int(np.prod(shp))
        out[n] = flat[r:r + k].reshape(shp)
        r += k
    r = SMALL_REP
    for n, shp in REPL:
        k = int(np.prod(shp))
        out[n] = flat[r:r + k].reshape(shp)
        r += k
    return out


IN_SRC = (706, 772)


NUNITS = 8
U_IN, U_OUT = 6, 7


def _halves(a):
    return a.reshape(2, a.shape[0] // 2, D)


def weight_pieces(wt, l):
    inn = wt["e_in" if l == 0 else "o_in"]
    inn = jnp.pad(inn, ((0, IN_ROWS - inn.shape[0]), (0, 0)))
    ffn = [_halves(wt[n][l][j]) for j in range(2) for n in ("ffn_g", "ffn_u", "ffn_d")]
    return ffn + [_halves(inn), _halves(wt["e_out" if l == 0 else "o_out"])]


def ffn_weights(q, j):
    return tuple(q[3 * j + k].reshape(NSH, FSH, D) for k in range(3))


def proj_weights(l, q_in, q_out):
    m = _even_in_map() if l == 0 else _odd_in_map()
    src = np.where(m >= 0, (m // IN_SRC[l]) * IN_ROWS + m % IN_SRC[l], -1)
    w_out = q_out.reshape(NSH * 256, D)
    w_in = q_in.reshape(NSH * IN_ROWS, D)
    w_in = jnp.where(jnp.asarray(src >= 0)[:, None], jnp.take(w_in, jnp.asarray(np.maximum(src, 0)), axis=0), 0)
    return {"w_in": w_in.astype(q_in.dtype),
            "w_out": _rows(w_out, _even_out_map()) if l == 0 else w_out}


def layer_grad_items(l, dwg, dwu, dwd, g_in, g_out, only=range(NUNITS)):
    units = [[None] * NUNITS for _ in range(2)]

    def put(i, a):
        a = a.reshape(NSH, 2, a.shape[1] // 2, D)
        units[0][i], units[1][i] = a[:, 0], a[:, 1]

    for j in range(2):
        for k, t in enumerate((dwg, dwu, dwd)):
            if 3 * j + k in only:
                put(3 * j + k, t[l][j])
    if U_IN in only:
        m = _even_in_map() if l == 0 else _odd_in_map()
        gi = _rows(g_in, _inverse(m, NSH * IN_SRC[l])).reshape(NSH, IN_SRC[l], D)
        put(U_IN, _bf(jnp.pad(gi, ((0, 0), (0, IN_ROWS - IN_SRC[l]), (0, 0)))))
    if U_OUT in only:
        if l == 0:
            g_out = _rows(g_out, _inverse(_even_out_map(), 1024))
        put(U_OUT, _bf(g_out).reshape(NSH, 256, D))
    return units


def assemble_layer(l, r0, r1):
    whole = lambda i: jnp.concatenate([r0[i], r1[i]])
    return {"ffn_g": jnp.stack([whole(0), whole(3)]), "ffn_u": jnp.stack([whole(1), whole(4)]),
            "ffn_d": jnp.stack([whole(2), whole(5)]), "in": whole(U_IN)[:IN_SRC[l]], "out": whole(U_OUT)}


def big_grads(l0, l1):
    st = lambda n: jnp.stack([l0[n], l1[n]])
    return {"ffn_g": st("ffn_g"), "ffn_u": st("ffn_u"), "ffn_d": st("ffn_d"), "e_in": l0["in"], "e_out": l0["out"],
            "o_in": l1["in"], "o_out": l1["out"]}


def small_from_gathered(gs):
    sm = [unpack_small(gs[s]) for s in range(NSH)]
    full = {}
    full["meta"] = jnp.concatenate([sm[s]["meta"] for s in range(NSH)], axis=1)
    full["norm"] = jnp.concatenate([sm[s]["norm"] for s in range(NSH)], axis=2)
    full["conv"] = jnp.concatenate([sm[s]["conv"][0] for s in range(NSH)], axis=1)
    full["gate_up"] = jnp.concatenate([sm[s]["gate_up"][0] for s in range(NSH)], axis=1)
    full["b_gate"] = jnp.concatenate([sm[s]["b_gate"] for s in range(NSH)], axis=1)
    full["gla_norm"] = jnp.concatenate([sm[s]["gla_norm"] for s in range(NSH)], axis=1)
    return full


def _col_sh(w):
    return jnp.moveaxis(w.reshape(w.shape[0], NSH, w.shape[1] // NSH), 1, 0)


def _to_t(n, a):
    if n in ("ffn_g", "ffn_u"):
        return jnp.swapaxes(a, 2, 3)
    if n in ("e_in", "o_in"):
        return jnp.swapaxes(a[0], 0, 1)
    return a if n == "ffn_d" else a[0]


def _from_t(n, a):
    if n in ("ffn_g", "ffn_u"):
        return jnp.swapaxes(a, 2, 3)
    if n in ("e_in", "o_in"):
        return jnp.swapaxes(a, 0, 1)[None]
    return a if n == "ffn_d" else a[None]


def pack_small_grads(G):
    col_sh = _col_sh
    norm_sh = jnp.moveaxis(G["norm"].reshape(2, 6, NSH, 256), 2, 0)
    a = jnp.concatenate([col_sh(G["meta"]).reshape(NSH, -1), norm_sh.reshape(NSH, -1), col_sh(G["conv"]).reshape(NSH, -1),
                         col_sh(G["gate_up"]).reshape(NSH, -1), G["b_gate"].reshape(NSH, -1),
                         G["gla_norm"].reshape(NSH, -1)], axis=1)
    b = jnp.concatenate([G[n].reshape(-1) for n, _ in REPL])
    b = jnp.broadcast_to(b[None], (NSH, b.shape[0]))
    small = jnp.concatenate([a, jnp.zeros((NSH, SMALL_REP - a.shape[1]), F32), b,
                             jnp.zeros((NSH, SMALL_ROWS * LANE - SMALL_REP - b.shape[1]), F32)], axis=1)
    return small.reshape(NSH, SMALL_ROWS, LANE)


MESH = pl.DeviceIdType.MESH
ANY = pl.BlockSpec(memory_space=pl.ANY)
VMEM = pl.BlockSpec(memory_space=pltpu.VMEM)


def _place():
    return lax.axis_index("x"), lax.axis_index("y"), lax.axis_index("c")


def _other_chips(x, y):
    return [(1 - x, y), (x, 1 - y), (1 - x, 1 - y)]


def _rcopy(send_sems, recv_sems, k, src, dst, to):
    return pltpu.make_async_remote_copy(src_ref=src, dst_ref=dst, send_sem=send_sems.at[k], recv_sem=recv_sems.at[k],
                                        device_id=to, device_id_type=MESH)


def _gather_steps(in_refs, out_refs, send_sems, recv_sems):
    n = len(in_refs)
    x, y, c = _place()
    s = 2 * x + y
    chips = _other_chips(x, y)
    copy = functools.partial(_rcopy, send_sems, recv_sems)
    pairs = [(i, j, cx, cy) for i in range(n) for j, (cx, cy) in enumerate(chips)]
    pushes = lambda: [copy(i * 3 + j, in_refs[i].at[c], out_refs[i].at[s, c], (cx, cy, c)) for i, j, cx, cy in pairs]
    landed = lambda i, cx, cy, half: out_refs[i].at[2 * cx + cy, half]
    relays = lambda: [copy(3 * n + i * 3 + j, landed(i, cx, cy, c), landed(i, cx, cy, c), (x, y, 1 - c)) for i, j, cx, cy in pairs]

    def start():
        for cp in pushes():
            cp.start()

    def relay():
        for i, j, cx, cy in pairs:
            copy(i * 3 + j, landed(i, cx, cy, c), landed(i, cx, cy, c), (x, y, c)).wait_recv()
        for cp in relays():
            cp.start()

    def finish():
        for i, j, cx, cy in pairs:
            copy(3 * n + i * 3 + j, landed(i, cx, cy, 1 - c), landed(i, cx, cy, 1 - c), (x, y, c)).wait_recv()
        for cp in pushes() + relays():
            cp.wait_send()

    return start, relay, finish


def _gather_shapes(pieces):
    return [jax.ShapeDtypeStruct((NSH,) + a.shape, a.dtype) for a in pieces]


def ag_layer(name, pieces):
    n = len(pieces)

    def body(*refs):
        for fn in _gather_steps(refs[:n], refs[n:2 * n], *refs[2 * n:]):
            fn()

    return pl.pallas_call(
        body, name=name, in_specs=[ANY] * n, out_specs=[ANY] * n, out_shape=_gather_shapes(pieces),
        scratch_shapes=[pltpu.SemaphoreType.DMA((6 * n,)), pltpu.SemaphoreType.DMA((6 * n,))],
    )(*pieces)


def gather_side(pieces):
    def events(n_steps, in_refs, out_refs, send_sems, recv_sems):
        start, relay, finish = _gather_steps(in_refs, out_refs, send_sems, recv_sems)
        return [(0, start), (max(3 * n_steps // 4, 1), relay), (n_steps - 1, finish)]

    return Side(pieces, _gather_shapes(pieces), 6 * len(pieces), events)


def ag_small(pack):
    def body(x_ref, out_ref, send_sems, recv_sems):
        x, y, c = _place()
        s = 2 * x + y
        chips = _other_chips(x, y)

        def copy(k, src, dst, to):
            return pltpu.make_async_remote_copy(src_ref=src, dst_ref=dst, send_sem=send_sems.at[k], recv_sem=recv_sems.at[k],
                                                device_id=to, device_id_type=MESH)

        out_ref[s] = x_ref[...]
        sends = [copy(j, x_ref, out_ref.at[s], (cx, cy, c)) for j, (cx, cy) in enumerate(chips)]
        for cp in sends:
            cp.start()
        for j, (cx, cy) in enumerate(chips):
            blk = out_ref.at[2 * cx + cy]
            copy(j, blk, blk, (x, y, c)).wait_recv()
        for cp in sends:
            cp.wait_send()

    return pl.pallas_call(
        body, name="ag_small", in_specs=[VMEM], out_specs=VMEM,
        out_shape=jax.ShapeDtypeStruct((NSH,) + pack.shape, pack.dtype),
        scratch_shapes=[pltpu.SemaphoreType.DMA((3,)), pltpu.SemaphoreType.DMA((3,))],
    )(pack)


def rs_pair(name, items):
    ni = len(items[0])

    def body(*refs):
        in_refs = [refs[:ni], refs[ni:2 * ni]]
        recv_refs = refs[2 * ni:3 * ni]
        send_sems, recv_sems = refs[3 * ni:]
        x, y, c = _place()
        copy = functools.partial(_rcopy, send_sems, recv_sems)
        for cc in range(2):
            @pl.when(c == cc)
            def _():
                cps = [copy(i * NSH + s, in_refs[1 - cc][i].at[s], recv_refs[i].at[s], (x, y, 1 - c))
                       for i in range(ni) for s in range(NSH)]
                for cp in cps:
                    cp.start()
                for cp in cps:
                    cp.wait()

    return pl.pallas_call(
        body, name=name, in_specs=[ANY] * (2 * ni), out_specs=[ANY] * ni,
        out_shape=[jax.ShapeDtypeStruct(a.shape, a.dtype) for a in items[0]],
        scratch_shapes=[pltpu.SemaphoreType.DMA((ni * NSH,)), pltpu.SemaphoreType.DMA((ni * NSH,))],
    )(*items[0], *items[1])


def _scatter_steps(a_refs, out_refs, send_sems, recv_sems):
    n = len(a_refs)
    x, y, c = _place()
    s = 2 * x + y
    chips = _other_chips(x, y)
    copy = functools.partial(_rcopy, send_sems, recv_sems)
    pairs = [(i, j, cx, cy) for i in range(n) for j, (cx, cy) in enumerate(chips)]
    sends = lambda: [copy(i * 3 + j, a_refs[i].at[2 * cx + cy], out_refs[i].at[s], (cx, cy, c)) for i, j, cx, cy in pairs]

    def start():
        for cp in sends():
            cp.start()

    def finish():
        for i, j, cx, cy in pairs:
            blk = out_refs[i].at[2 * cx + cy]
            copy(i * 3 + j, blk, blk, (x, y, c)).wait_recv()
        for cp in sends():
            cp.wait_send()

    return start, finish


def rs_chips(name, arrs):
    n = len(arrs)

    def body(*refs):
        for fn in _scatter_steps(refs[:n], refs[n:2 * n], *refs[2 * n:]):
            fn()

    return pl.pallas_call(
        body, name=name, in_specs=[ANY] * n, out_specs=[ANY] * n,
        out_shape=[jax.ShapeDtypeStruct(a.shape, a.dtype) for a in arrs],
        scratch_shapes=[pltpu.SemaphoreType.DMA((3 * n,)), pltpu.SemaphoreType.DMA((3 * n,))],
    )(*arrs)


def scatter_side(arrs):
    def events(n_steps, in_refs, out_refs, send_sems, recv_sems):
        start, finish = _scatter_steps(in_refs, out_refs, send_sems, recv_sems)
        return [(0, start), (n_steps - 1, finish)]

    return Side(arrs, [jax.ShapeDtypeStruct(a.shape, a.dtype) for a in arrs], 3 * len(arrs), events)


def _pair_chunks(rows):
    return 4 if rows % 32 == 0 else (2 if rows % 16 == 0 else 1)


def ag_pair(name, arrs):
    n = len(arrs)
    chunks = [(i, k * (a.shape[0] // _pair_chunks(a.shape[0])), a.shape[0] // _pair_chunks(a.shape[0]))
              for i, a in enumerate(arrs) for k in range(_pair_chunks(a.shape[0]))]

    def body(*refs):
        g_refs, out_refs = refs[:n], refs[n:2 * n]
        send_sems, recv_sems = refs[2 * n:]
        x, y, c = _place()
        give = [_rcopy(send_sems, recv_sems, q, g_refs[i].at[pl.ds(r0, rc)], out_refs[i].at[pl.ds(r0, rc)], (x, y, 1 - c))
                for q, (i, r0, rc) in enumerate(chunks)]
        for cp in give:
            cp.start()
        for cp in give:
            cp.wait()

    return pl.pallas_call(
        body, name=name, in_specs=[ANY] * n, out_specs=[ANY] * n,
        out_shape=[jax.ShapeDtypeStruct(a.shape, a.dtype) for a in arrs],
        scratch_shapes=[pltpu.SemaphoreType.DMA((len(chunks),)), pltpu.SemaphoreType.DMA((len(chunks),))],
    )(*arrs)


def small_allreduce(p):
    def body(p_ref, out_ref, rbuf, send_sems, recv_sems):
        x, y, c = _place()
        me = 4 * x + 2 * y + c
        rbuf[me] = p_ref[2 * x + y]
        flip = lambda v, f: (1 - v) if f else v
        peers = [(flip(x, k >> 2 & 1), flip(y, k >> 1 & 1), flip(c, k & 1)) for k in range(1, 8)]

        def copy(k, src, dst, to):
            return pltpu.make_async_remote_copy(src_ref=src, dst_ref=dst, send_sem=send_sems.at[k], recv_sem=recv_sems.at[k],
                                                device_id=to, device_id_type=MESH)

        sends = [copy(k, p_ref.at[2 * px + py], rbuf.at[me], (px, py, pc)) for k, (px, py, pc) in enumerate(peers)]
        for cp in sends:
            cp.start()
        for k, (px, py, pc) in enumerate(peers):
            blk = rbuf.at[4 * px + 2 * py + pc]
            copy(k, blk, blk, (x, y, c)).wait_recv()
        for cp in sends:
            cp.wait_send()
        acc = rbuf[0]
        for d in range(1, 8):
            acc = acc + rbuf[d]
        out_ref[...] = acc

    return pl.pallas_call(
        body, name="small_allreduce", in_specs=[VMEM], out_specs=VMEM,
        out_shape=jax.ShapeDtypeStruct(p.shape[1:], F32),
        scratch_shapes=[pltpu.VMEM((8,) + p.shape[1:], F32), pltpu.SemaphoreType.DMA((7,)), pltpu.SemaphoreType.DMA((7,))],
    )(p)


def _rows_tile(rows, cap):
    return _pick(rows, cap) if rows % 128 == 0 else rows


def sum_pair(name, a0s, a1s, recvs, cflag):
    n = len(recvs)

    def body(c_ref, *refs):
        for i in range(n):
            a0_ref, a1_ref, b_ref, o_ref = refs[i], refs[n + i], refs[2 * n + i], refs[3 * n + i]
            own = jnp.where(c_ref[0] == 0, a0_ref[...].astype(F32), a1_ref[...].astype(F32))
            o_ref[...] = (own + b_ref[...].astype(F32)).astype(o_ref.dtype)

    specs = [pl.BlockSpec((None,) + a.shape[1:], lambda s: (s, 0, 0)) for a in recvs]
    return pl.pallas_call(
        body, name=name, grid=(NSH,), in_specs=[pl.BlockSpec(memory_space=pltpu.SMEM)] + specs * 3, out_specs=specs,
        out_shape=[jax.ShapeDtypeStruct(a.shape, BF16) for a in recvs], compiler_params=_cparams(("parallel",)),
    )(cflag, *a0s, *a1s, *recvs)


def sum_chips(name, parts, owns, sflag):
    n = len(parts)

    def body(s_ref, *refs):
        for i in range(n):
            p_ref, a_ref, o_ref = refs[i], refs[n + i], refs[2 * n + i]
            acc = None
            for t in range(NSH):
                term = jnp.where(s_ref[0] == t, a_ref[t].astype(F32), p_ref[t].astype(F32))
                acc = term if acc is None else acc + term
            o_ref[...] = acc

    specs = [pl.BlockSpec(a.shape, lambda i: (0, 0, 0)) for a in parts]
    return pl.pallas_call(
        body, name=name, grid=(1,), in_specs=[pl.BlockSpec(memory_space=pltpu.SMEM)] + specs * 2,
        out_specs=[pl.BlockSpec(a.shape[1:], lambda i: (0, 0)) for a in parts],
        out_shape=[jax.ShapeDtypeStruct(a.shape[1:], F32) for a in parts], compiler_params=_cparams(("arbitrary",)),
    )(sflag, *parts, *owns)


ADAM_LR, ADAM_B1, ADAM_B2, ADAM_EPS, ADAM_WD, ADAM_STEP = 0.001, 0.9, 0.999, 1e-08, 0.01, 10


def adamw_call(name, w, g, m, v):
    rows, cols = w.shape
    tr = _rows_tile(rows, 512)

    def body(w_ref, g_ref, m_ref, v_ref, d_ref, nm_ref, nv_ref):
        g_ = g_ref[...]
        m_ = ADAM_B1 * m_ref[...] + (1.0 - ADAM_B1) * g_
        v_ = ADAM_B2 * v_ref[...] + (1.0 - ADAM_B2) * (g_ * g_)
        m_hat = m_ / (1.0 - ADAM_B1 ** ADAM_STEP)
        v_hat = v_ / (1.0 - ADAM_B2 ** ADAM_STEP)
        d_ref[...] = -ADAM_LR * (m_hat / (jnp.sqrt(v_hat) + ADAM_EPS) + ADAM_WD * w_ref[...])
        nm_ref[...] = m_
        nv_ref[...] = v_

    spec = pl.BlockSpec((tr, cols), lambda i: (i, 0))
    sh = jax.ShapeDtypeStruct((rows, cols), F32)
    return pl.pallas_call(
        body, name=name, grid=(rows // tr,), in_specs=[spec] * 4, out_specs=[spec] * 3, out_shape=[sh] * 3,
        compiler_params=_cparams(("parallel",)),
    )(w, g, m, v)


def kernel(x, meta_tokens, norm_w, ffn_w_gate, ffn_w_up, ffn_w_down, rel_bias_table, even_w_in, even_conv_w, swa_sinks, dn_a_log, dn_dt_bias, dn_norm_w, even_w_out, odd_w_in, gla_w_gate_up, gla_b_gate, gla_norm_w, odd_w_out, loss_target, m_meta_tokens, m_norm_w, m_ffn_w_gate, m_ffn_w_up, m_ffn_w_down, m_rel_bias_table, m_even_w_in, m_even_conv_w, m_swa_sinks, m_dn_a_log, m_dn_dt_bias, m_dn_norm_w, m_even_w_out, m_odd_w_in, m_gla_w_gate_up, m_gla_b_gate, m_gla_norm_w, m_odd_w_out, v_meta_tokens, v_norm_w, v_ffn_w_gate, v_ffn_w_up, v_ffn_w_down, v_rel_bias_table, v_even_w_in, v_even_conv_w, v_swa_sinks, v_dn_a_log, v_dn_dt_bias, v_dn_norm_w, v_even_w_out, v_odd_w_in, v_gla_w_gate_up, v_gla_b_gate, v_gla_norm_w, v_odd_w_out):
    ws = [meta_tokens, norm_w, ffn_w_gate, ffn_w_up, ffn_w_down, rel_bias_table, even_w_in, even_conv_w, swa_sinks, dn_a_log,
          dn_dt_bias, dn_norm_w, even_w_out, odd_w_in, gla_w_gate_up, gla_b_gate, gla_norm_w, odd_w_out]
    ms = [m_meta_tokens, m_norm_w, m_ffn_w_gate, m_ffn_w_up, m_ffn_w_down, m_rel_bias_table, m_even_w_in, m_even_conv_w,
          m_swa_sinks, m_dn_a_log, m_dn_dt_bias, m_dn_norm_w, m_even_w_out, m_odd_w_in, m_gla_w_gate_up, m_gla_b_gate,
          m_gla_norm_w, m_odd_w_out]
    vs = [v_meta_tokens, v_norm_w, v_ffn_w_gate, v_ffn_w_up, v_ffn_w_down, v_rel_bias_table, v_even_w_in, v_even_conv_w,
          v_swa_sinks, v_dn_a_log, v_dn_dt_bias, v_dn_norm_w, v_even_w_out, v_odd_w_in, v_gla_w_gate_up, v_gla_b_gate,
          v_gla_norm_w, v_odd_w_out]
    short = [n for n, _ in NAMES]
    w = dict(zip(short, ws))
    m = dict(zip(short, ms))
    v = dict(zip(short, vs))

    wt = {n: _to_t(n, w[n]) for n in BIG}
    own = {n: wt[n].astype(BF16) for n in BIG}
    sflag = (2 * lax.axis_index("x") + lax.axis_index("y")).astype(jnp.int32).reshape(1)
    cflag = lax.axis_index("c").astype(jnp.int32).reshape(1)
    is0 = cflag[0] == 0
    fill = lambda got, pieces: [lax.dynamic_update_index_in_dim(g_, p_, sflag[0], 0) for g_, p_ in zip(got, pieces)]
    pieces = [weight_pieces(own, l) for l in range(2)]
    small = small_from_gathered(ag_small(pack_small(w)))
    def reduce_finish(l, mine, parts):
        red = [None] * NUNITS
        for grp in ((0, 1, 2), (3, 4, 5), (U_IN, U_OUT)):
            outs = sum_chips(f"sum_chips_{l}{grp[0]}", [parts[i] for i in grp], [mine[i] for i in grp], sflag)
            for i, o in zip(grp, outs):
                red[i] = o
        got = ag_pair(f"ag_pair_{l}", red)
        return [jnp.where(is0, r_, g_) for r_, g_ in zip(red, got)], [jnp.where(is0, g_, r_) for r_, g_ in zip(red, got)]

    class Exchanges:
        gathers = {"ffn_fwd_00": [(0, 3), (0, U_IN), (0, U_OUT)], "swa_fwd": [(0, 4)], "dn_fwd": [(0, 5), (1, 0), (1, 1)],
                   "ffn_fwd_01": [(1, 2), (1, U_IN), (1, U_OUT)], "ffn_fwd_10": [(1, 3), (1, 4), (1, 5)]}
        scatters = {"ffn_bx_10": [(1, 3), (1, 4), (1, 5)], "ffn_bx_01": [(1, 0), (1, 1), (1, 2)],
                    "dn_bwd": [(1, U_IN), (1, U_OUT), (0, 3)], "swa_bwd": [(0, 4), (0, 5)], "ffn_bx_00": [(0, U_IN), (0, U_OUT)]}

        def __init__(self):
            self.q = [[None] * NUNITS for _ in range(2)]
            self.mine = [[None] * NUNITS for _ in range(2)]
            self.parts = [[None] * NUNITS for _ in range(2)]

        def side(self, name):
            if name in self.gathers:
                return gather_side([pieces[l][i] for l, i in self.gathers[name]])
            if name in self.scatters:
                return scatter_side([self.mine[l][i] for l, i in self.scatters[name]])
            return None

        def done(self, name, outs):
            if name in self.gathers:
                units = self.gathers[name]
                for (l, i), a in zip(units, fill(outs, [pieces[l][i] for l, i in units])):
                    self.q[l][i] = a
            if name in self.scatters:
                for (l, i), a in zip(self.scatters[name], outs):
                    self.parts[l][i] = a

        def ffn(self, l, j):
            return ffn_weights(self.q[l], j)

        def proj(self, l):
            return proj_weights(l, self.q[l][U_IN], self.q[l][U_OUT])

        def grads(self, l, units, items):
            units = list(units)
            half0, half1 = [[items[c][i] for i in units] for c in range(2)]
            recv = rs_pair(f"rs_pair_{l}{units[0]}", [half0, half1])
            for i, a in zip(units, sum_pair(f"sum_pair_{l}{units[0]}", half0, half1, recv, cflag)):
                self.mine[l][i] = a

    ex = Exchanges()
    first = [0, 1, 2]
    for i, a in zip(first, fill(ag_layer("ag_layer_0", [pieces[0][i] for i in first]), [pieces[0][i] for i in first])):
        ex.q[0][i] = a
    W = {**small, **{n: w[n] for n, _ in REPL}, "ffn": ex.ffn, "proj": ex.proj}
    loss_blk, gx, G = core_step(x[0], loss_target[0], W, comm=ex)

    ex.grads(0, first, G["items"][0])
    for i, a in zip(first, rs_chips("rs_chips_0", [ex.mine[0][i] for i in first])):
        ex.parts[0][i] = a
    lay0 = assemble_layer(0, *reduce_finish(0, ex.mine[0], ex.parts[0]))
    lay1 = assemble_layer(1, *reduce_finish(1, ex.mine[1], ex.parts[1]))
    gt = big_grads(lay0, lay1)
    g_small_pack = small_allreduce(pack_small_grads(G))
    g = {**{n: _from_t(n, gt[n]) for n in BIG}, **unpack_small(g_small_pack)}

    delta, new_m, new_v = {}, {}, {}
    for n in BIG:
        shp = wt[n].shape
        two = lambda t: t.reshape(-1, D)
        d_, m_, v_ = adamw_call("adamw_" + n, two(wt[n]), two(gt[n]), two(_to_t(n, m[n])), two(_to_t(n, v[n])))
        delta[n], new_m[n], new_v[n] = (_from_t(n, t.reshape(shp)) for t in (d_, m_, v_))
    d_, m_, v_ = adamw_call("adamw_small", pack_small(w), g_small_pack, pack_small(m), pack_small(v))
    delta.update(unpack_small(d_))
    new_m.update(unpack_small(m_))
    new_v.update(unpack_small(v_))

    loss = lax.psum(loss_blk[0, 0], ("x", "y", "c"))
    return (loss, gx[None], *[g[n] for n in short], *[delta[n] for n in short], *[new_m[n] for n in short],
            *[new_v[n] for n in short])
```
